```python
import jax, jax.numpy as jnp
from jax import lax
import numpy as np

D_MODEL = 1024
BATCH = 4
SEQ = 8192
DEPTH = 1

CHUNK = 64
N_LEFT_CHUNKS = 8
BAND = (N_LEFT_CHUNKS + 1) * CHUNK
ATT_HEADS = 8
HEAD_DIM = 64
ATT_WIDTH = ATT_HEADS * HEAD_DIM
MAX_REL = 128
SGU_BLOCK = 128
SGU_GROUPS = 8
SGU_WIDTH = 512
SGU_GROUP_DIM = SGU_WIDTH // SGU_GROUPS
IN_WIDTH = 3 * ATT_WIDTH + 2 * SGU_WIDTH
N_BRANCHES = 2
N_EXPERTS = 32
TOP_K = 4
D_EXPERT = D_MODEL
SWIGLU_LIMIT = 7.0
SWIGLU_ALPHA = 1.702
EXPERT_BLOCK = 256
N_MOD = 6
EPS = 1e-6

kernel_name = "hybrid_chunk_attn_gmlp_moe_block"


def rmsnorm(x, g):
    xf = x.astype(jnp.float32)
    y = xf * lax.rsqrt(jnp.mean(xf * xf, axis=-1, keepdims=True) + EPS)
    return (y * g.astype(jnp.float32)).astype(x.dtype)


def layernorm(x, g, b):
    xf = x.astype(jnp.float32)
    mu = jnp.mean(xf, axis=-1, keepdims=True)
    var = jnp.mean(jnp.square(xf - mu), axis=-1, keepdims=True)
    y = (xf - mu) * lax.rsqrt(var + EPS)
    return (y * g.astype(jnp.float32) + b.astype(jnp.float32)).astype(x.dtype)


def chunk_attention(q, k, v, rel_bias):
    B, S, _ = q.shape
    n_chunks = S // CHUNK
    q = q.reshape(B, n_chunks, CHUNK, ATT_HEADS, HEAD_DIM)
    k = k.reshape(B, n_chunks, CHUNK, ATT_HEADS, HEAD_DIM)
    v = v.reshape(B, n_chunks, CHUNK, ATT_HEADS, HEAD_DIM)
    pad = ((0, 0), (N_LEFT_CHUNKS, 0), (0, 0), (0, 0), (0, 0))
    kp = jnp.pad(k, pad)
    vp = jnp.pad(v, pad)
    kb = jnp.concatenate([kp[:, j:j + n_chunks] for j in range(N_LEFT_CHUNKS + 1)], axis=2)
    vb = jnp.concatenate([vp[:, j:j + n_chunks] for j in range(N_LEFT_CHUNKS + 1)], axis=2)
    scores = jnp.einsum('bcqhd,bckhd->bchqk', q, kb).astype(jnp.float32) * (HEAD_DIM ** -0.5)
    rel = (N_LEFT_CHUNKS * CHUNK + jnp.arange(CHUNK, dtype=jnp.int32)[:, None]) \
        - jnp.arange(BAND, dtype=jnp.int32)[None, :]
    idx = jnp.clip(rel, -MAX_REL, MAX_REL) + MAX_REL
    bias = rel_bias[:, idx].astype(jnp.float32)
    scores = scores + bias[None, None]
    kpos = jnp.arange(n_chunks, dtype=jnp.int32)[:, None] * CHUNK - N_LEFT_CHUNKS * CHUNK \
        + jnp.arange(BAND, dtype=jnp.int32)[None, :]
    valid = (kpos >= 0)[None, :, None, None, :]
    scores = jnp.where(valid, scores, -1e30)
    p = jax.nn.softmax(scores, axis=-1).astype(v.dtype)
    o = jnp.einsum('bchqk,bckhd->bcqhd', p, vb)
    return o.reshape(B, S, ATT_WIDTH)


def spatial_gating(z, ln_g, ln_b, w_s, b_s):
    B, S, _ = z.shape
    z = jax.nn.gelu(z, approximate=False)
    u, vv = jnp.split(z, 2, axis=-1)
    vv = layernorm(vv, ln_g, ln_b)
    n_blocks = S // SGU_BLOCK
    vv = vv.reshape(B, n_blocks, SGU_BLOCK, SGU_GROUPS, SGU_GROUP_DIM)
    causal = jnp.tril(jnp.ones((SGU_BLOCK, SGU_BLOCK), dtype=w_s.dtype))
    wm = w_s * causal[None]
    s = jnp.einsum('gts,bnsgd->bntgd', wm, vv) + b_s.T[None, None, :, :, None]
    return u * s.reshape(B, S, SGU_WIDTH)


def moe_ffn(h, w_router, b_router, w_gate_up, b_gate_up, w_down, b_down):
    B, S, D = h.shape
    xt = h.reshape(-1, D)
    N = xt.shape[0]
    logits = (xt @ w_router + b_router).astype(jnp.float32)
    top_v, top_i = lax.top_k(logits, TOP_K)
    gates = jax.nn.softmax(top_v, axis=-1).astype(h.dtype)
    n_assign = N * TOP_K
    flat_e = top_i.reshape(-1).astype(jnp.int32)
    flat_tok = jnp.arange(n_assign, dtype=jnp.int32) // TOP_K
    flat_g = gates.reshape(-1)
    order = jnp.argsort(flat_e)
    sorted_e = flat_e[order]
    sorted_tok = flat_tok[order]
    sorted_g = flat_g[order]
    counts = jnp.zeros((N_EXPERTS,), jnp.int32).at[flat_e].add(1)
    padded = (counts + EXPERT_BLOCK - 1) // EXPERT_BLOCK * EXPERT_BLOCK
    pad_end = jnp.cumsum(padded)
    pad_start = pad_end - padded
    start = jnp.cumsum(counts) - counts
    rank = jnp.arange(n_assign, dtype=jnp.int32) - start[sorted_e]
    dest = pad_start[sorted_e] + rank
    n_blocks = -(-n_assign // EXPERT_BLOCK) + N_EXPERTS
    P = n_blocks * EXPERT_BLOCK
    buf_tok = jnp.zeros((P,), jnp.int32).at[dest].set(sorted_tok)
    buf_g = jnp.zeros((P,), h.dtype).at[dest].set(sorted_g)
    block_start = jnp.arange(n_blocks, dtype=jnp.int32) * EXPERT_BLOCK
    block_e = jnp.minimum(jnp.searchsorted(pad_end, block_start, side='right'), N_EXPERTS - 1)
    xb = xt[buf_tok].reshape(n_blocks, EXPERT_BLOCK, D)

    def expert_block(args):
        xblk, e = args
        gu = xblk @ w_gate_up[e] + b_gate_up[e]
        x_glu = jnp.minimum(gu[:, :D_EXPERT], SWIGLU_LIMIT)
        x_lin = jnp.clip(gu[:, D_EXPERT:], -SWIGLU_LIMIT, SWIGLU_LIMIT)
        act = x_glu * jax.nn.sigmoid(SWIGLU_ALPHA * x_glu) * (x_lin + 1)
        return act @ w_down[e] + b_down[e]

    yb = lax.map(expert_block, (xb, block_e)).reshape(P, D)
    out = jnp.zeros_like(xt).at[buf_tok].add(yb * buf_g[:, None])
    return out.reshape(B, S, D)


def setup_inputs(seed: int = 0) -> dict:
    key = jax.random.key(seed)
    ks = jax.random.split(key, 26)
    L, D = DEPTH, D_MODEL

    def nrm(k, shape, scale):
        return jax.random.normal(k, shape, jnp.float32) * scale

    return {
        "x": nrm(ks[0], (BATCH, SEQ, D), 1.0),
        "c": nrm(ks[1], (BATCH, D), 1.0),
        "w_ada": nrm(ks[2], (L, D, N_MOD * D), D ** -0.5),
        "b_ada": nrm(ks[3], (L, N_MOD * D), 0.02),
        "g_pre_mix": 1.0 + nrm(ks[4], (L, D), 0.05),
        "g_post_mix": 1.0 + nrm(ks[5], (L, D), 0.05),
        "w_in": nrm(ks[6], (L, D, IN_WIDTH), D ** -0.5),
        "rel_bias": nrm(ks[7], (L, ATT_HEADS, 2 * MAX_REL + 1), 0.5),
        "sgu_ln_g": 1.0 + nrm(ks[8], (L, SGU_WIDTH), 0.05),
        "sgu_ln_b": nrm(ks[9], (L, SGU_WIDTH), 0.02),
        "w_spatial": nrm(ks[10], (L, SGU_GROUPS, SGU_BLOCK, SGU_BLOCK), 0.5 * SGU_BLOCK ** -0.5),
        "b_spatial": 1.0 + nrm(ks[11], (L, SGU_GROUPS, SGU_BLOCK), 0.1),
        "w_branch_a": nrm(ks[12], (L, ATT_WIDTH, D), ATT_WIDTH ** -0.5),
        "w_branch_b": nrm(ks[13], (L, SGU_WIDTH, D), SGU_WIDTH ** -0.5),
        "w_gate": nrm(ks[14], (L, D, N_BRANCHES * D), D ** -0.5),
        "b_gate": nrm(ks[15], (L, N_BRANCHES * D), 0.02),
        "w_out": nrm(ks[16], (L, D, D), D ** -0.5),
        "g_pre_ffn": 1.0 + nrm(ks[17], (L, D), 0.05),
        "g_post_ffn": 1.0 + nrm(ks[18], (L, D), 0.05),
        "w_router": nrm(ks[19], (L, D, N_EXPERTS), D ** -0.5),
        "b_router": nrm(ks[20], (L, N_EXPERTS), 0.01),
        "w_gate_up": nrm(ks[21], (L, N_EXPERTS, D, 2 * D_EXPERT), D ** -0.5),
        "b_gate_up": nrm(ks[22], (L, N_EXPERTS, 2 * D_EXPERT), 0.02),
        "w_down": nrm(ks[23], (L, N_EXPERTS, D_EXPERT, D), D_EXPERT ** -0.5),
        "b_down": nrm(ks[24], (L, N_EXPERTS, D), 0.02),
    }


def reference(x, c, w_ada, b_ada, g_pre_mix, g_post_mix, w_in, rel_bias, sgu_ln_g, sgu_ln_b,
              w_spatial, b_spatial, w_branch_a, w_branch_b, w_gate, b_gate, w_out,
              g_pre_ffn, g_post_ffn, w_router, b_router, w_gate_up, b_gate_up, w_down, b_down):
    c_act = jax.nn.silu(c)
    for l in range(DEPTH):
        mod = (c_act @ w_ada[l] + b_ada[l])[:, None, :]
        sh1, sc1, gt1, sh2, sc2, gt2 = jnp.split(mod, N_MOD, axis=-1)

        h = rmsnorm(x, g_pre_mix[l]) * (1 + sc1) + sh1
        proj = h @ w_in[l]
        q = proj[..., :ATT_WIDTH]
        k = proj[..., ATT_WIDTH:2 * ATT_WIDTH]
        v = proj[..., 2 * ATT_WIDTH:3 * ATT_WIDTH]
        z = proj[..., 3 * ATT_WIDTH:]
        y_a = chunk_attention(q, k, v, rel_bias[l])
        y_b = spatial_gating(z, sgu_ln_g[l], sgu_ln_b[l], w_spatial[l], b_spatial[l])
        gates = jax.nn.sigmoid(h @ w_gate[l] + b_gate[l])
        g_a, g_b = jnp.split(gates, N_BRANCHES, axis=-1)
        merged = g_a * (y_a @ w_branch_a[l]) + g_b * (y_b @ w_branch_b[l])
        y = rmsnorm(merged @ w_out[l], g_post_mix[l])
        x = x + gt1 * y

        h = rmsnorm(x, g_pre_ffn[l]) * (1 + sc2) + sh2
        y = moe_ffn(h, w_router[l], b_router[l], w_gate_up[l], b_gate_up[l], w_down[l], b_down[l])
        y = rmsnorm(y, g_post_ffn[l])
        x = x + gt2 * y
    return x
```

```python
import functools

import jax
import jax.numpy as jnp
from jax import lax
from jax.experimental import pallas as pl
from jax.experimental.pallas import tpu as pltpu

bf16 = jnp.bfloat16
f32 = jnp.float32
i32 = jnp.int32

D_MODEL = 1024
CHUNK = 64
N_LEFT = 8
ATT_HEADS = 8
HEAD_DIM = 64
ATT_WIDTH = 512
MAX_REL = 128
SGU_BLOCK = 128
SGU_WIDTH = 512
N_EXPERTS = 32
TOP_K = 4
SWIGLU_LIMIT = 7.0
SWIGLU_ALPHA = 1.702
EPS = 1e-6
NEG = -1e30

LANES = 128
ROW_ALIGN = 8
TM = 512
QG = 2 * CHUNK
KBAND = (N_LEFT + 2) * CHUNK
TS = 256
RS = TS * TOP_K + N_EXPERTS * ROW_ALIGN
BM = 256
VMEM_LIMIT = 56 * 2**20


def _params(sem):
    return pltpu.CompilerParams(dimension_semantics=sem, vmem_limit_bytes=VMEM_LIMIT)


def _adaln(x, g, sc, sh):
    ms = jnp.mean(x * x, axis=-1, keepdims=True)
    return (x * lax.rsqrt(ms + EPS) * g) * (1.0 + sc) + sh


def _rms(x, g):
    ms = jnp.mean(x * x, axis=-1, keepdims=True)
    return x * lax.rsqrt(ms + EPS) * g


def _sigmoid(x):
    return 1.0 / (1.0 + jnp.exp(-x))


def _ada_body(c_ref, w_ref, b_ref, o_ref):
    c = c_ref[...]
    ca = c * _sigmoid(c)
    o_ref[...] = jnp.dot(ca.astype(bf16), w_ref[...].astype(bf16),
                         preferred_element_type=f32) + b_ref[...]


def _ada(c8, w, b):
    d = w.shape[0]
    n = w.shape[1] // d
    return pl.pallas_call(
        _ada_body,
        out_shape=jax.ShapeDtypeStruct((8, n * d), f32),
        grid=(n,),
        in_specs=[pl.BlockSpec((8, d), lambda j: (0, 0)),
                  pl.BlockSpec((d, d), lambda j: (0, j)),
                  pl.BlockSpec((1, d), lambda j: (0, j))],
        out_specs=pl.BlockSpec((8, d), lambda j: (0, j)),
        compiler_params=_params(("arbitrary",)),
        name="ada",
    )(c8, w, b)


def _proj_body(x_ref, mod_ref, g_ref, w_ref, lng_ref, lnb_ref,
               q_ref, k_ref, v_ref, u_ref, vv_ref):
    h = _adaln(x_ref[0], g_ref[...], mod_ref[0, 1:2, :], mod_ref[0, 0:1, :])
    p = jnp.dot(h.astype(bf16), w_ref[...], preferred_element_type=f32)
    aw = ATT_WIDTH
    q_ref[0] = (p[:, 0:aw] * (HEAD_DIM ** -0.5)).astype(bf16)
    k_ref[0] = p[:, aw:2 * aw].astype(bf16)
    v_ref[0] = p[:, 2 * aw:3 * aw].astype(bf16)
    z = p[:, 3 * aw:]
    zg = 0.5 * z * (1.0 + lax.erf(z * (2.0 ** -0.5)))
    u_ref[0] = zg[:, :SGU_WIDTH].astype(bf16)
    vv = zg[:, SGU_WIDTH:]
    mu = jnp.mean(vv, axis=-1, keepdims=True)
    var = jnp.mean(jnp.square(vv - mu), axis=-1, keepdims=True)
    vn = (vv - mu) * lax.rsqrt(var + EPS) * lng_ref[...] + lnb_ref[...]
    vv_ref[0] = vn.astype(bf16)


def _proj(x, mod3, g, w_in, lng, lnb):
    b, s, d = x.shape
    tok = lambda w: pl.BlockSpec((1, TM, w), lambda bi, i: (bi, i, 0))
    full = lambda a: pl.BlockSpec(a.shape, lambda bi, i: (0,) * a.ndim)
    o512 = jax.ShapeDtypeStruct((b, s, ATT_WIDTH), bf16)
    return pl.pallas_call(
        _proj_body,
        out_shape=(o512,) * 5,
        grid=(b, s // TM),
        in_specs=[tok(d), pl.BlockSpec((1, 6, d), lambda bi, i: (bi, 0, 0)),
                  full(g), full(w_in), full(lng), full(lnb)],
        out_specs=(tok(ATT_WIDTH),) * 5,
        compiler_params=_params(("parallel", "arbitrary")),
        name="proj",
    )(x, mod3, g, w_in, lng, lnb)


def _attn_body(q_ref, kp_ref, kc_ref, vp_ref, vc_ref, bias_ref, o_ref, kbuf, vbuf):
    first = pl.program_id(1) == 0
    kbuf[0:TM, :] = kp_ref[0]
    kbuf[TM:2 * TM, :] = kc_ref[0]
    vbuf[0:TM, :] = vp_ref[0]
    vbuf[TM:2 * TM, :] = vc_ref[0]
    lo = lax.broadcasted_iota(i32, (QG, LANES), 1) < HEAD_DIM
    col = lax.broadcasted_iota(i32, (2 * QG, KBAND), 1)

    def group(p, carry):
        r0 = pl.multiple_of(p * QG, QG)
        pad_keys = jnp.logical_and(first, col + r0 < TM)
        for hp in range(ATT_HEADS // 2):
            c0 = hp * LANES
            qp = q_ref[0, pl.ds(r0, QG), c0:c0 + LANES]
            zero = jnp.zeros_like(qp)
            q2 = jnp.concatenate([jnp.where(lo, qp, zero), jnp.where(lo, zero, qp)], axis=0)
            kb = kbuf[pl.ds(r0, KBAND), c0:c0 + LANES]
            s = lax.dot_general(q2, kb, (((1,), (1,)), ((), ())), preferred_element_type=f32)
            s = s + bias_ref[hp]
            s = jnp.where(pad_keys, NEG, s)
            m = jnp.max(s, axis=-1, keepdims=True)
            e = jnp.exp(s - m)
            l = jnp.sum(e, axis=-1, keepdims=True)
            vb = vbuf[pl.ds(r0, KBAND), c0:c0 + LANES]
            o2 = jnp.dot(e.astype(bf16), vb, preferred_element_type=f32) / l
            o = jnp.where(lo, o2[:QG], o2[QG:])
            o_ref[0, pl.ds(r0, QG), c0:c0 + LANES] = o.astype(bf16)
        return carry

    lax.fori_loop(0, TM // QG, group, 0)


def _attn(q, k, v, bias2):
    b, s, w = q.shape
    cur = pl.BlockSpec((1, TM, w), lambda bi, i: (bi, i, 0))
    prev = pl.BlockSpec((1, TM, w), lambda bi, i: (bi, jnp.maximum(i - 1, 0), 0))
    return pl.pallas_call(
        _attn_body,
        out_shape=jax.ShapeDtypeStruct((b, s, w), bf16),
        grid=(b, s // TM),
        in_specs=[cur, prev, cur, prev, cur,
                  pl.BlockSpec(bias2.shape, lambda bi, i: (0, 0, 0))],
        out_specs=cur,
        scratch_shapes=[pltpu.VMEM((2 * TM, w), bf16), pltpu.VMEM((2 * TM, w), bf16)],
        compiler_params=_params(("parallel", "arbitrary")),
        name="attn",
    )(q, k, k, v, v, bias2)


def _attn_bias(rel_bias):
    i = jnp.arange(QG, dtype=i32)[:, None]
    j = jnp.arange(KBAND, dtype=i32)[None, :]
    jb = j - (i // CHUNK) * CHUNK
    band = CHUNK * (N_LEFT + 1)
    valid = jnp.logical_and(jb >= 0, jb < band)
    rel = N_LEFT * CHUNK + (i % CHUNK) - jb
    idx = jnp.clip(rel, -MAX_REL, MAX_REL) + MAX_REL
    bias = jnp.where(valid[None], rel_bias[:, idx].astype(f32), NEG)
    return bias.reshape(ATT_HEADS // 2, 2 * QG, KBAND)


def _mix_body(x_ref, mod_ref, gpre_ref, gpost_ref, u_ref, vv_ref, ya_ref, ws_ref, bs_ref,
              wa_ref, wb_ref, wg_ref, bg_ref, wo_ref, o_ref, ybuf):
    d = D_MODEL
    x = x_ref[0]
    h = _adaln(x, gpre_ref[...], mod_ref[0, 1:2, :], mod_ref[0, 0:1, :]).astype(bf16)
    gs = _sigmoid(jnp.dot(h, wg_ref[...], preferred_element_type=f32) + bg_ref[...])

    blk = SGU_BLOCK
    row = lax.broadcasted_iota(i32, (2 * blk, blk), 0)
    colv = lax.broadcasted_iota(i32, (2 * blk, blk), 1)
    causal = colv <= jnp.bitwise_and(row, blk - 1)
    lo = lax.broadcasted_iota(i32, (blk, LANES), 1) < (LANES // 2)
    for gp in range(SGU_WIDTH // LANES):
        c0 = gp * LANES
        w2 = ws_ref[gp]
        w2 = jnp.where(causal, w2, jnp.zeros_like(w2))
        for bi in range(TM // blk):
            r0 = bi * blk
            s2 = jnp.dot(w2, vv_ref[0, r0:r0 + blk, c0:c0 + LANES], preferred_element_type=f32)
            s = jnp.where(lo, s2[:blk], s2[blk:]) + bs_ref[:, c0:c0 + LANES]
            yb = u_ref[0, r0:r0 + blk, c0:c0 + LANES].astype(f32) * s
            ybuf[r0:r0 + blk, c0:c0 + LANES] = yb.astype(bf16)

    a = jnp.dot(ya_ref[0], wa_ref[...], preferred_element_type=f32)
    bb = jnp.dot(ybuf[...], wb_ref[...], preferred_element_type=f32)
    merged = gs[:, :d] * a + gs[:, d:] * bb
    y = jnp.dot(merged.astype(bf16), wo_ref[...], preferred_element_type=f32)
    o_ref[0] = x + mod_ref[0, 2:3, :] * _rms(y, gpost_ref[...])


def _mix(x, mod3, gpre, gpost, u, vv, ya, ws2, bsf, wa, wb, wg, bg, wo):
    b, s, d = x.shape
    tok = lambda w: pl.BlockSpec((1, TM, w), lambda bi, i: (bi, i, 0))
    full = lambda a: pl.BlockSpec(a.shape, lambda bi, i: (0,) * a.ndim)
    return pl.pallas_call(
        _mix_body,
        out_shape=jax.ShapeDtypeStruct((b, s, d), f32),
        grid=(b, s // TM),
        in_specs=[tok(d), pl.BlockSpec((1, 6, d), lambda bi, i: (bi, 0, 0)),
                  full(gpre), full(gpost), tok(SGU_WIDTH), tok(SGU_WIDTH), tok(ATT_WIDTH),
                  full(ws2), full(bsf), full(wa), full(wb), full(wg), full(bg), full(wo)],
        out_specs=tok(d),
        scratch_shapes=[pltpu.VMEM((TM, SGU_WIDTH), bf16)],
        compiler_params=_params(("parallel", "arbitrary")),
        name="mix",
    )(x, mod3, gpre, gpost, u, vv, ya, ws2, bsf, wa, wb, wg, bg, wo)


def _router_body(x_ref, mod_ref, g_ref, wr_ref, br_ref, h2_ref, ti_ref, gt_ref, tab_ref):
    i = pl.program_id(0)
    h = _adaln(x_ref[...], g_ref[...], mod_ref[0, 4:5, :], mod_ref[0, 3:4, :])
    hb = h.astype(bf16)
    h2_ref[...] = hb
    lg = lax.dot_general(wr_ref[...], hb, (((1,), (1,)), ((), ())),
                         preferred_element_type=f32) + br_ref[...]
    e_iota = lax.broadcasted_iota(i32, lg.shape, 0)
    vals, idxs = [], []
    cnt = jnp.zeros((N_EXPERTS, 1), f32)
    for _ in range(TOP_K):
        m = jnp.max(lg, axis=0, keepdims=True)
        idx = jnp.min(jnp.where(lg == m, e_iota, N_EXPERTS), axis=0, keepdims=True)
        hit = e_iota == idx
        cnt = cnt + jnp.sum(hit.astype(f32), axis=1, keepdims=True)
        lg = jnp.where(hit, -jnp.inf, lg)
        vals.append(m)
        idxs.append(idx)
    ex = [jnp.exp(v - vals[0]) for v in vals]
    den = ex[0] + ex[1] + ex[2] + ex[3]
    ti_ref[...] = jnp.concatenate(idxs, axis=0)
    gt_ref[...] = jnp.concatenate([e / den for e in ex], axis=0)
    c8 = jnp.floor((cnt + (ROW_ALIGN - 1.0)) * (1.0 / ROW_ALIGN)) * ROW_ALIGN

    @pl.when(i == 0)
    def _():
        tab_ref[...] = jnp.zeros_like(tab_ref)

    lane = lax.broadcasted_iota(i32, tab_ref.shape, 1)
    tab_ref[...] += jnp.where(lane == i, c8, 0.0)


def _router(x1, mod3, g, wrt, br, tiles_per_batch, ntp):
    n, d = x1.shape
    nt = n // TS
    tok = pl.BlockSpec((TS, d), lambda i: (i, 0))
    full = lambda a: pl.BlockSpec(a.shape, lambda i: (0,) * a.ndim)
    lane = pl.BlockSpec((TOP_K, TS), lambda i: (0, i))
    return pl.pallas_call(
        _router_body,
        out_shape=(jax.ShapeDtypeStruct((n, d), bf16),
                   jax.ShapeDtypeStruct((TOP_K, n), i32),
                   jax.ShapeDtypeStruct((TOP_K, n), f32),
                   jax.ShapeDtypeStruct((N_EXPERTS, ntp), f32)),
        grid=(nt,),
        in_specs=[tok, pl.BlockSpec((1, 6, d), lambda i: (i // tiles_per_batch, 0, 0)),
                  full(g), full(wrt), full(br)],
        out_specs=(tok, lane, lane, pl.BlockSpec((N_EXPERTS, ntp), lambda i: (0, 0))),
        compiler_params=_params(("arbitrary",)),
        name="router",
    )(x1, mod3, g, wrt, br)


def _sorted_positions(ti):
    ts = ti.shape[1]
    e_iota = lax.broadcasted_iota(i32, (N_EXPERTS, ts), 0)
    upper = (lax.broadcasted_iota(i32, (ts, ts), 0) < lax.broadcasted_iota(i32, (ts, ts), 1))
    upper = jnp.where(upper, 1.0, 0.0).astype(bf16)
    hits, prefs, cnts = [], [], []
    for k in range(TOP_K):
        hit = ti[k:k + 1, :] == e_iota
        hf = jnp.where(hit, 1.0, 0.0)
        prefs.append(jnp.dot(hf.astype(bf16), upper, preferred_element_type=f32))
        cnts.append(jnp.sum(hf, axis=1, keepdims=True))
        hits.append(hit)
    total = cnts[0] + cnts[1] + cnts[2] + cnts[3]
    c8 = jnp.floor((total + (ROW_ALIGN - 1.0)) * (1.0 / ROW_ALIGN)) * ROW_ALIGN
    lower = (lax.broadcasted_iota(i32, (N_EXPERTS, N_EXPERTS), 1)
             < lax.broadcasted_iota(i32, (N_EXPERTS, N_EXPERTS), 0))
    lower = jnp.where(lower, 1.0, 0.0).astype(bf16)
    c8b = jnp.broadcast_to(c8, (N_EXPERTS, LANES)).astype(bf16)
    start = jnp.dot(lower, c8b, preferred_element_type=f32)[:, 0:1]
    pos = []
    for k in range(TOP_K):
        pe = start + prefs[k]
        pos.append(jnp.sum(jnp.where(hits[k], pe, 0.0), axis=0, keepdims=True).astype(i32))
        start = start + cnts[k]
    return pos


def _run_copies(i, toff_ref, dst_ref, c8_ref, make, action):
    def one(e, carry):
        j = i * N_EXPERTS + e
        n = pl.multiple_of(c8_ref[j], ROW_ALIGN)
        so = pl.multiple_of(toff_ref[j], ROW_ALIGN)
        do = pl.multiple_of(dst_ref[j], ROW_ALIGN)

        @pl.when(n > 0)
        def _():
            action(make(so, do, n))
        return carry
    lax.fori_loop(0, N_EXPERTS, one, 0)


def _dispatch_body(toff_ref, dst_ref, c8_ref, tstart_ref, tlen_ref, nu_ref,
                   h2_ref, ti_ref, xb_ref, sbuf, zbuf, sem):
    i = pl.program_id(0)
    p = _sorted_positions(ti_ref[...])
    r = lax.broadcasted_iota(i32, (RS, TS), 0)
    pm = jnp.where(r == p[0], 1.0, jnp.where(r == p[1], 1.0,
         jnp.where(r == p[2], 1.0, jnp.where(r == p[3], 1.0, 0.0))))
    sbuf[...] = jnp.dot(pm.astype(bf16), h2_ref[...], preferred_element_type=f32)

    def make(so, do, n):
        return pltpu.make_async_copy(sbuf.at[pl.ds(so, n)], xb_ref.at[pl.ds(do, n)], sem)
    _run_copies(i, toff_ref, dst_ref, c8_ref, make, lambda c: c.start())
    _run_copies(i, toff_ref, dst_ref, c8_ref, make, lambda c: c.wait())

    @pl.when(i == pl.num_programs(0) - 1)
    def _():
        zbuf[...] = jnp.zeros_like(zbuf)

        def fill(action):
            def tail(e, carry):
                n = pl.multiple_of(tlen_ref[e], ROW_ALIGN)
                do = pl.multiple_of(tstart_ref[e], ROW_ALIGN)

                @pl.when(n > 0)
                def _():
                    action(pltpu.make_async_copy(zbuf.at[pl.ds(0, n)], xb_ref.at[pl.ds(do, n)], sem))
                return carry
            lax.fori_loop(0, N_EXPERTS, tail, 0)

            def unused(b, carry):
                do = pl.multiple_of(b * BM, BM)
                action(pltpu.make_async_copy(zbuf, xb_ref.at[pl.ds(do, BM)], sem))
                return carry
            lax.fori_loop(nu_ref[0], xb_ref.shape[0] // BM, unused, 0)
        fill(lambda c: c.start())
        fill(lambda c: c.wait())


def _dispatch(toff, dst, c8s, tstart, tlen, n_used, h2, ti, p_rows):
    n, d = h2.shape
    return pl.pallas_call(
        _dispatch_body,
        out_shape=jax.ShapeDtypeStruct((p_rows, d), f32),
        grid_spec=pltpu.PrefetchScalarGridSpec(
            num_scalar_prefetch=6,
            grid=(n // TS,),
            in_specs=[pl.BlockSpec((TS, d), lambda i, *_: (i, 0)),
                      pl.BlockSpec((TOP_K, TS), lambda i, *_: (0, i))],
            out_specs=pl.BlockSpec(memory_space=pl.ANY),
            scratch_shapes=[pltpu.VMEM((RS, d), f32), pltpu.VMEM((BM, d), f32),
                            pltpu.SemaphoreType.DMA(())]),
        compiler_params=_params(("arbitrary",)),
        name="dispatch",
    )(toff, dst, c8s, tstart, tlen, n_used, h2, ti)


def _expert_body(be_ref, nu_ref, x_ref, wgu_ref, bgu_ref, wd_ref, bd_ref, o_ref,
                 wgu_bf, wd_bf):
    b = pl.program_id(0)
    d = D_MODEL

    @pl.when(b >= nu_ref[0])
    def _():
        o_ref[...] = jnp.zeros_like(o_ref)

    @pl.when(b < nu_ref[0])
    def _():
        new_expert = jnp.logical_or(b == 0, be_ref[b] != be_ref[jnp.maximum(b - 1, 0)])

        @pl.when(new_expert)
        def _():
            wgu_bf[...] = wgu_ref[0].astype(bf16)
            wd_bf[...] = wd_ref[0].astype(bf16)

        gu = jnp.dot(x_ref[...].astype(bf16), wgu_bf[...], preferred_element_type=f32) + bgu_ref[0]
        glu = jnp.minimum(gu[:, :d], SWIGLU_LIMIT)
        lin = jnp.clip(gu[:, d:], -SWIGLU_LIMIT, SWIGLU_LIMIT)
        act = glu * _sigmoid(SWIGLU_ALPHA * glu) * (lin + 1.0)
        o_ref[...] = jnp.dot(act.astype(bf16), wd_bf[...], preferred_element_type=f32) + bd_ref[0]


def _experts(blk_e, n_used, xb, wgu, bgu, wd, bd):
    p_rows, d = xb.shape
    rows = pl.BlockSpec((BM, d), lambda b, be, nu: (b, 0))
    per_e = lambda a: pl.BlockSpec((1,) + a.shape[1:], lambda b, be, nu: (be[b], 0, 0))
    return pl.pallas_call(
        _expert_body,
        out_shape=jax.ShapeDtypeStruct((p_rows, d), f32),
        grid_spec=pltpu.PrefetchScalarGridSpec(
            num_scalar_prefetch=2,
            grid=(p_rows // BM,),
            in_specs=[rows, per_e(wgu), per_e(bgu), per_e(wd), per_e(bd)],
            out_specs=rows,
            scratch_shapes=[pltpu.VMEM(wgu.shape[1:], bf16), pltpu.VMEM(wd.shape[1:], bf16)]),
        compiler_params=_params(("arbitrary",)),
        name="experts",
    )(blk_e, n_used, xb, wgu, bgu, wd, bd)


def _combine_body(toff_ref, dst_ref, c8_ref, yb_ref, ti_ref, gt_ref, x1_ref, mod_ref, g_ref,
                  o_ref, sbuf, sem):
    i = pl.program_id(0)

    @pl.when(i == 0)
    def _():
        sbuf[...] = jnp.zeros_like(sbuf)

    def make(so, do, n):
        return pltpu.make_async_copy(yb_ref.at[pl.ds(do, n)], sbuf.at[pl.ds(so, n)], sem)
    _run_copies(i, toff_ref, dst_ref, c8_ref, make, lambda c: c.start())

    p = _sorted_positions(ti_ref[...])
    gt = gt_ref[...]
    r = lax.broadcasted_iota(i32, (RS, TS), 0)
    m = [r == p[k] for k in range(TOP_K)]
    pm = jnp.where(m[0], 1.0, jnp.where(m[1], 1.0, jnp.where(m[2], 1.0, jnp.where(m[3], 1.0, 0.0))))
    gm = jnp.where(m[0], gt[0:1, :], jnp.where(m[1], gt[1:2, :],
         jnp.where(m[2], gt[2:3, :], jnp.where(m[3], gt[3:4, :], 0.0))))
    grow = jnp.sum(gm, axis=1, keepdims=True)

    _run_copies(i, toff_ref, dst_ref, c8_ref, make, lambda c: c.wait())
    sg = (sbuf[...] * grow).astype(bf16)
    y = lax.dot_general(pm.astype(bf16), sg, (((0,), (0,)), ((), ())), preferred_element_type=f32)
    o_ref[...] = x1_ref[...] + mod_ref[0, 5:6, :] * _rms(y, g_ref[...])


def _combine(toff, dst, c8s, yb, ti, gt, x1, mod3, g, tiles_per_batch):
    n, d = x1.shape
    tok = pl.BlockSpec((TS, d), lambda i, *_: (i, 0))
    lane = pl.BlockSpec((TOP_K, TS), lambda i, *_: (0, i))
    return pl.pallas_call(
        _combine_body,
        out_shape=jax.ShapeDtypeStruct((n, d), f32),
        grid_spec=pltpu.PrefetchScalarGridSpec(
            num_scalar_prefetch=3,
            grid=(n // TS,),
            in_specs=[pl.BlockSpec(memory_space=pl.ANY), lane, lane, tok,
                      pl.BlockSpec((1, 6, d), lambda i, *_: (i // tiles_per_batch, 0, 0)),
                      pl.BlockSpec(g.shape, lambda i, *_: (0, 0))],
            out_specs=tok,
            scratch_shapes=[pltpu.VMEM((RS, d), f32), pltpu.SemaphoreType.DMA(())]),
        compiler_params=_params(("arbitrary",)),
        name="combine",
    )(toff, dst, c8s, yb, ti, gt, x1, mod3, g)


def _layout_tables(tab, nt, p_rows):
    c8 = tab[:, :nt].T.astype(i32)
    toff = jnp.cumsum(c8, axis=1) - c8
    len8 = jnp.sum(c8, axis=0)
    seg = (len8 + BM - 1) // BM * BM
    gend = jnp.cumsum(seg)
    gstart = gend - seg
    dst = jnp.cumsum(c8, axis=0) - c8 + gstart[None, :]
    n_used = gend[-1] // BM
    blk = jnp.arange(p_rows // BM, dtype=i32)
    last = jnp.minimum(blk, n_used - 1)
    blk_e = jnp.minimum(jnp.searchsorted(gend, last * BM, side='right'), N_EXPERTS - 1).astype(i32)
    return (toff.reshape(-1), dst.reshape(-1), c8.reshape(-1), gstart + len8, seg - len8,
            blk_e, n_used.reshape(1).astype(i32))


def kernel(x, c, w_ada, b_ada, g_pre_mix, g_post_mix, w_in, rel_bias, sgu_ln_g, sgu_ln_b,
           w_spatial, b_spatial, w_branch_a, w_branch_b, w_gate, b_gate, w_out,
           g_pre_ffn, g_post_ffn, w_router, b_router, w_gate_up, b_gate_up, w_down, b_down):
    b, s, d = x.shape
    assert d == D_MODEL and s % TM == 0 and (b * s) % TS == 0 and s % TS == 0
    n = b * s
    nt = n // TS
    ntp = -(-nt // LANES) * LANES
    p_rows = -(-(n * TOP_K + nt * N_EXPERTS * (ROW_ALIGN - 1) + N_EXPERTS * (BM - 1)) // BM) * BM
    depth = w_ada.shape[0]
    c8 = jnp.pad(c, ((0, 8 - b), (0, 0)))
    row = lambda a: a.reshape(1, -1)

    for l in range(depth):
        mod = _ada(c8, w_ada[l], row(b_ada[l]))[:b]
        mod3 = mod.reshape(b, 6, d)

        q, k, v, u, vv = _proj(x, mod3, row(g_pre_mix[l]), w_in[l].astype(bf16),
                               row(sgu_ln_g[l]), row(sgu_ln_b[l]))
        ya = _attn(q, k, v, _attn_bias(rel_bias[l]))
        ws2 = w_spatial[l].astype(bf16).reshape(-1, 2 * SGU_BLOCK, SGU_BLOCK)
        bsf = jnp.repeat(b_spatial[l].T, SGU_WIDTH // b_spatial.shape[1], axis=1)
        x1 = _mix(x, mod3, row(g_pre_mix[l]), row(g_post_mix[l]), u, vv, ya, ws2, bsf,
                  w_branch_a[l].astype(bf16), w_branch_b[l].astype(bf16),
                  w_gate[l].astype(bf16), row(b_gate[l]), w_out[l].astype(bf16))

        x1f = x1.reshape(n, d)
        h2, ti, gt, tab = _router(x1f, mod3, row(g_pre_ffn[l]), w_router[l].T.astype(bf16),
                                  b_router[l].reshape(-1, 1), s // TS, ntp)
        toff, dst, c8s, tstart, tlen, blk_e, n_used = _layout_tables(tab, nt, p_rows)
        xb = _dispatch(toff, dst, c8s, tstart, tlen, n_used, h2, ti, p_rows)
        yb = _experts(blk_e, n_used, xb, w_gate_up[l], b_gate_up[l][:, None, :],
                      w_down[l], b_down[l][:, None, :])
        x = _combine(toff, dst, c8s, yb, ti, gt, x1f, mod3, row(g_post_ffn[l]), s // TS).reshape(b, s, d)
    return x
```

```python
import functools

import jax
import jax.numpy as jnp
from jax import lax
from jax.experimental import pallas as pl
from jax.experimental.pallas import tpu as pltpu

bf16 = jnp.bfloat16
f32 = jnp.float32
i32 = jnp.int32

D_MODEL = 1024
CHUNK = 64
N_LEFT = 8
ATT_HEADS = 8
HEAD_DIM = 64
ATT_WIDTH = 512
MAX_REL = 128
SGU_BLOCK = 128
SGU_WIDTH = 512
N_EXPERTS = 32
TOP_K = 4
SWIGLU_LIMIT = 7.0
SWIGLU_ALPHA = 1.702
EPS = 1e-6
NEG = -1e30

LANES = 128
ROW_ALIGN = 8
TM = 512
QG = 2 * CHUNK
KBAND = (N_LEFT + 2) * CHUNK
TS = 256
RS = TS * TOP_K + N_EXPERTS * ROW_ALIGN
BM = 256
VMEM_LIMIT = 56 * 2**20


def _params(sem):
    return pltpu.CompilerParams(dimension_semantics=sem, vmem_limit_bytes=VMEM_LIMIT)


def _adaln(x, g, sc, sh):
    ms = jnp.mean(x * x, axis=-1, keepdims=True)
    return (x * lax.rsqrt(ms + EPS) * g) * (1.0 + sc) + sh


def _rms(x, g):
    ms = jnp.mean(x * x, axis=-1, keepdims=True)
    return x * lax.rsqrt(ms + EPS) * g


def _sigmoid(x):
    return 1.0 / (1.0 + jnp.exp(-x))


def _ada_body(c_ref, w_ref, b_ref, o_ref):
    c = c_ref[...]
    ca = c * _sigmoid(c)
    o_ref[...] = jnp.dot(ca.astype(bf16), w_ref[...].astype(bf16),
                         preferred_element_type=f32) + b_ref[...]


def _ada(c8, w, b):
    d = w.shape[0]
    n = w.shape[1] // d
    return pl.pallas_call(
        _ada_body,
        out_shape=jax.ShapeDtypeStruct((8, n * d), f32),
        grid=(n,),
        in_specs=[pl.BlockSpec((8, d), lambda j: (0, 0)),
                  pl.BlockSpec((d, d), lambda j: (0, j)),
                  pl.BlockSpec((1, d), lambda j: (0, j))],
        out_specs=pl.BlockSpec((8, d), lambda j: (0, j)),
        compiler_params=_params(("arbitrary",)),
        name="ada",
    )(c8, w, b)


def _proj_body(x_ref, mod_ref, g_ref, w_ref, lng_ref, lnb_ref,
               q_ref, k_ref, v_ref, u_ref, vv_ref):
    h = _adaln(x_ref[0], g_ref[...], mod_ref[0, 1:2, :], mod_ref[0, 0:1, :])
    p = jnp.dot(h.astype(bf16), w_ref[...], preferred_element_type=f32)
    aw = ATT_WIDTH
    q_ref[0] = (p[:, 0:aw] * (HEAD_DIM ** -0.5)).astype(bf16)
    k_ref[0] = p[:, aw:2 * aw].astype(bf16)
    v_ref[0] = p[:, 2 * aw:3 * aw].astype(bf16)
    z = p[:, 3 * aw:]
    zg = 0.5 * z * (1.0 + lax.erf(z * (2.0 ** -0.5)))
    u_ref[0] = zg[:, :SGU_WIDTH].astype(bf16)
    vv = zg[:, SGU_WIDTH:]
    mu = jnp.mean(vv, axis=-1, keepdims=True)
    var = jnp.mean(jnp.square(vv - mu), axis=-1, keepdims=True)
    vn = (vv - mu) * lax.rsqrt(var + EPS) * lng_ref[...] + lnb_ref[...]
    vv_ref[0] = vn.astype(bf16)


def _proj(x, mod3, g, w_in, lng, lnb):
    b, s, d = x.shape
    tok = lambda w: pl.BlockSpec((1, TM, w), lambda bi, i: (bi, i, 0))
    full = lambda a: pl.BlockSpec(a.shape, lambda bi, i: (0,) * a.ndim)
    o512 = jax.ShapeDtypeStruct((b, s, ATT_WIDTH), bf16)
    return pl.pallas_call(
        _proj_body,
        out_shape=(o512,) * 5,
        grid=(b, s // TM),
        in_specs=[tok(d), pl.BlockSpec((1, 6, d), lambda bi, i: (bi, 0, 0)),
                  full(g), full(w_in), full(lng), full(lnb)],
        out_specs=(tok(ATT_WIDTH),) * 5,
        compiler_params=_params(("parallel", "arbitrary")),
        name="proj",
    )(x, mod3, g, w_in, lng, lnb)


def _attn_body(q_ref, kp_ref, kc_ref, vp_ref, vc_ref, bias_ref, o_ref, kbuf, vbuf):
    first = pl.program_id(1) == 0
    kbuf[0:TM, :] = kp_ref[0]
    kbuf[TM:2 * TM, :] = kc_ref[0]
    vbuf[0:TM, :] = vp_ref[0]
    vbuf[TM:2 * TM, :] = vc_ref[0]
    lo = lax.broadcasted_iota(i32, (QG, LANES), 1) < HEAD_DIM
    col = lax.broadcasted_iota(i32, (2 * QG, KBAND), 1)

    def group(p, carry):
        r0 = pl.multiple_of(p * QG, QG)
        pad_keys = jnp.logical_and(first, col + r0 < TM)
        for hp in range(ATT_HEADS // 2):
            c0 = hp * LANES
            qp = q_ref[0, pl.ds(r0, QG), c0:c0 + LANES]
            zero = jnp.zeros_like(qp)
            q2 = jnp.concatenate([jnp.where(lo, qp, zero), jnp.where(lo, zero, qp)], axis=0)
            kb = kbuf[pl.ds(r0, KBAND), c0:c0 + LANES]
            s = lax.dot_general(q2, kb, (((1,), (1,)), ((), ())), preferred_element_type=f32)
            s = s + bias_ref[hp]
            s = jnp.where(pad_keys, NEG, s)
            m = jnp.max(s, axis=-1, keepdims=True)
            e = jnp.exp(s - m)
            l = jnp.sum(e, axis=-1, keepdims=True)
            vb = vbuf[pl.ds(r0, KBAND), c0:c0 + LANES]
            o2 = jnp.dot(e.astype(bf16), vb, preferred_element_type=f32) / l
            o = jnp.where(lo, o2[:QG], o2[QG:])
            o_ref[0, pl.ds(r0, QG), c0:c0 + LANES] = o.astype(bf16)
        return carry

    lax.fori_loop(0, TM // QG, group, 0)


def _attn(q, k, v, bias2):
    b, s, w = q.shape
    cur = pl.BlockSpec((1, TM, w), lambda bi, i: (bi, i, 0))
    prev = pl.BlockSpec((1, TM, w), lambda bi, i: (bi, jnp.maximum(i - 1, 0), 0))
    return pl.pallas_call(
        _attn_body,
        out_shape=jax.ShapeDtypeStruct((b, s, w), bf16),
        grid=(b, s // TM),
        in_specs=[cur, prev, cur, prev, cur,
                  pl.BlockSpec(bias2.shape, lambda bi, i: (0, 0, 0))],
        out_specs=cur,
        scratch_shapes=[pltpu.VMEM((2 * TM, w), bf16), pltpu.VMEM((2 * TM, w), bf16)],
        compiler_params=_params(("parallel", "arbitrary")),
        name="attn",
    )(q, k, k, v, v, bias2)


def _attn_bias(rel_bias):
    h = rel_bias.shape[0]
    period = 1024
    assert period >= QG + KBAND - 1 and KBAND - 2 * MAX_REL >= 0
    far = jnp.broadcast_to(rel_bias[:, 2 * MAX_REL:], (h, period))
    near = rel_bias[:, :0:-1]
    v = jnp.concatenate([far[:, :KBAND - 2 * MAX_REL], near, far[:, KBAND:]], axis=1).astype(f32)
    flat = jnp.broadcast_to(v[:, None, :], (h, QG, period)).reshape(h, QG * period)
    toep = flat[:, :QG * (period - 1)].reshape(h, QG, period - 1)[:, :, :KBAND]
    i = jnp.arange(QG, dtype=i32)[:, None]
    j = jnp.arange(KBAND, dtype=i32)[None, :]
    jb = j - (i // CHUNK) * CHUNK
    valid = jnp.logical_and(jb >= 0, jb < CHUNK * (N_LEFT + 1))
    bias = jnp.where(valid[None], toep, NEG)
    return bias.reshape(ATT_HEADS // 2, 2 * QG, KBAND)


def _mix_body(x_ref, mod_ref, gpre_ref, gpost_ref, u_ref, vv_ref, ya_ref, ws_ref, bs_ref,
              wa_ref, wb_ref, wg_ref, bg_ref, wo_ref, o_ref, ybuf):
    d = D_MODEL
    x = x_ref[0]
    h = _adaln(x, gpre_ref[...], mod_ref[0, 1:2, :], mod_ref[0, 0:1, :]).astype(bf16)
    gs = _sigmoid(jnp.dot(h, wg_ref[...], preferred_element_type=f32) + bg_ref[...])

    blk = SGU_BLOCK
    row = lax.broadcasted_iota(i32, (2 * blk, blk), 0)
    colv = lax.broadcasted_iota(i32, (2 * blk, blk), 1)
    causal = colv <= jnp.bitwise_and(row, blk - 1)
    lo = lax.broadcasted_iota(i32, (blk, LANES), 1) < (LANES // 2)
    for gp in range(SGU_WIDTH // LANES):
        c0 = gp * LANES
        w2 = ws_ref[gp]
        w2 = jnp.where(causal, w2, jnp.zeros_like(w2))
        for bi in range(TM // blk):
            r0 = bi * blk
            s2 = jnp.dot(w2, vv_ref[0, r0:r0 + blk, c0:c0 + LANES], preferred_element_type=f32)
            s = jnp.where(lo, s2[:blk], s2[blk:]) + bs_ref[:, c0:c0 + LANES]
            yb = u_ref[0, r0:r0 + blk, c0:c0 + LANES].astype(f32) * s
            ybuf[r0:r0 + blk, c0:c0 + LANES] = yb.astype(bf16)

    a = jnp.dot(ya_ref[0], wa_ref[...], preferred_element_type=f32)
    bb = jnp.dot(ybuf[...], wb_ref[...], preferred_element_type=f32)
    merged = gs[:, :d] * a + gs[:, d:] * bb
    y = jnp.dot(merged.astype(bf16), wo_ref[...], preferred_element_type=f32)
    o_ref[0] = x + mod_ref[0, 2:3, :] * _rms(y, gpost_ref[...])


def _mix(x, mod3, gpre, gpost, u, vv, ya, ws2, bsf, wa, wb, wg, bg, wo):
    b, s, d = x.shape
    tok = lambda w: pl.BlockSpec((1, TM, w), lambda bi, i: (bi, i, 0))
    full = lambda a: pl.BlockSpec(a.shape, lambda bi, i: (0,) * a.ndim)
    return pl.pallas_call(
        _mix_body,
        out_shape=jax.ShapeDtypeStruct((b, s, d), f32),
        grid=(b, s // TM),
        in_specs=[tok(d), pl.BlockSpec((1, 6, d), lambda bi, i: (bi, 0, 0)),
                  full(gpre), full(gpost), tok(SGU_WIDTH), tok(SGU_WIDTH), tok(ATT_WIDTH),
                  full(ws2), full(bsf), full(wa), full(wb), full(wg), full(bg), full(wo)],
        out_specs=tok(d),
        scratch_shapes=[pltpu.VMEM((TM, SGU_WIDTH), bf16)],
        compiler_params=_params(("parallel", "arbitrary")),
        name="mix",
    )(x, mod3, gpre, gpost, u, vv, ya, ws2, bsf, wa, wb, wg, bg, wo)


def _router_body(x_ref, mod_ref, g_ref, wr_ref, br_ref, h2_ref, ti_ref, gt_ref, tab_ref):
    i = pl.program_id(0)
    h = _adaln(x_ref[...], g_ref[...], mod_ref[0, 4:5, :], mod_ref[0, 3:4, :])
    hb = h.astype(bf16)
    h2_ref[...] = hb
    lg = lax.dot_general(wr_ref[...], hb, (((1,), (1,)), ((), ())),
                         preferred_element_type=f32) + br_ref[...]
    e_iota = lax.broadcasted_iota(i32, lg.shape, 0)
    vals, idxs = [], []
    cnt = jnp.zeros((N_EXPERTS, 1), f32)
    for _ in range(TOP_K):
        m = jnp.max(lg, axis=0, keepdims=True)
        idx = jnp.min(jnp.where(lg == m, e_iota, N_EXPERTS), axis=0, keepdims=True)
        hit = e_iota == idx
        cnt = cnt + jnp.sum(hit.astype(f32), axis=1, keepdims=True)
        lg = jnp.where(hit, -jnp.inf, lg)
        vals.append(m)
        idxs.append(idx)
    ex = [jnp.exp(v - vals[0]) for v in vals]
    den = ex[0] + ex[1] + ex[2] + ex[3]
    ti_ref[...] = jnp.concatenate(idxs, axis=0)
    gt_ref[...] = jnp.concatenate([e / den for e in ex], axis=0)
    c8 = jnp.floor((cnt + (ROW_ALIGN - 1.0)) * (1.0 / ROW_ALIGN)) * ROW_ALIGN

    @pl.when(i == 0)
    def _():
        tab_ref[...] = jnp.zeros_like(tab_ref)

    lane = lax.broadcasted_iota(i32, tab_ref.shape, 1)
    tab_ref[...] += jnp.where(lane == i, c8, 0.0)


def _router(x1, mod3, g, wrt, br, tiles_per_batch, ntp):
    n, d = x1.shape
    nt = n // TS
    tok = pl.BlockSpec((TS, d), lambda i: (i, 0))
    full = lambda a: pl.BlockSpec(a.shape, lambda i: (0,) * a.ndim)
    lane = pl.BlockSpec((TOP_K, TS), lambda i: (0, i))
    return pl.pallas_call(
        _router_body,
        out_shape=(jax.ShapeDtypeStruct((n, d), bf16),
                   jax.ShapeDtypeStruct((TOP_K, n), i32),
                   jax.ShapeDtypeStruct((TOP_K, n), f32),
                   jax.ShapeDtypeStruct((N_EXPERTS, ntp), f32)),
        grid=(nt,),
        in_specs=[tok, pl.BlockSpec((1, 6, d), lambda i: (i // tiles_per_batch, 0, 0)),
                  full(g), full(wrt), full(br)],
        out_specs=(tok, lane, lane, pl.BlockSpec((N_EXPERTS, ntp), lambda i: (0, 0))),
        compiler_params=_params(("arbitrary",)),
        name="router",
    )(x1, mod3, g, wrt, br)


def _sorted_positions(ti):
    ts = ti.shape[1]
    e_iota = lax.broadcasted_iota(i32, (N_EXPERTS, ts), 0)
    upper = (lax.broadcasted_iota(i32, (ts, ts), 0) < lax.broadcasted_iota(i32, (ts, ts), 1))
    upper = jnp.where(upper, 1.0, 0.0).astype(bf16)
    hits, prefs, cnts = [], [], []
    for k in range(TOP_K):
        hit = ti[k:k + 1, :] == e_iota
        hf = jnp.where(hit, 1.0, 0.0)
        prefs.append(jnp.dot(hf.astype(bf16), upper, preferred_element_type=f32))
        cnts.append(jnp.sum(hf, axis=1, keepdims=True))
        hits.append(hit)
    total = cnts[0] + cnts[1] + cnts[2] + cnts[3]
    c8 = jnp.floor((total + (ROW_ALIGN - 1.0)) * (1.0 / ROW_ALIGN)) * ROW_ALIGN
    lower = (lax.broadcasted_iota(i32, (N_EXPERTS, N_EXPERTS), 1)
             < lax.broadcasted_iota(i32, (N_EXPERTS, N_EXPERTS), 0))
    lower = jnp.where(lower, 1.0, 0.0).astype(bf16)
    c8b = jnp.broadcast_to(c8, (N_EXPERTS, LANES)).astype(bf16)
    start = jnp.dot(lower, c8b, preferred_element_type=f32)[:, 0:1]
    pos = []
    for k in range(TOP_K):
        pe = start + prefs[k]
        pos.append(jnp.sum(jnp.where(hits[k], pe, 0.0), axis=0, keepdims=True).astype(i32))
        start = start + cnts[k]
    return pos


def _start_runs(i, toff_ref, dst_ref, c8_ref, make):
    def one(e, carry):
        j = i * N_EXPERTS + e
        n = pl.multiple_of(c8_ref[j], ROW_ALIGN)
        so = pl.multiple_of(toff_ref[j], ROW_ALIGN)
        do = pl.multiple_of(dst_ref[j], ROW_ALIGN)

        @pl.when(n > 0)
        def _():
            make(so, do, n).start()
        return carry
    lax.fori_loop(0, N_EXPERTS, one, 0)


def _tile_rows(i, toff_ref, c8_ref):
    j = i * N_EXPERTS + (N_EXPERTS - 1)
    return pl.multiple_of(toff_ref[j] + c8_ref[j], ROW_ALIGN)


def _dispatch_body(toff_ref, dst_ref, c8_ref, tstart_ref, tlen_ref, nu_ref,
                   h2_ref, ti_ref, xb_ref, sbuf, zbuf, sems, sem):
    i = pl.program_id(0)
    slot = lax.rem(i, 2)
    p = _sorted_positions(ti_ref[...])
    r = lax.broadcasted_iota(i32, (RS, TS), 0)
    pm = jnp.where(r == p[0], 1.0, jnp.where(r == p[1], 1.0,
         jnp.where(r == p[2], 1.0, jnp.where(r == p[3], 1.0, 0.0))))
    sbuf[slot] = jnp.dot(pm.astype(bf16), h2_ref[...], preferred_element_type=f32)

    def make(so, do, n):
        return pltpu.make_async_copy(sbuf.at[slot, pl.ds(so, n)], xb_ref.at[pl.ds(do, n)],
                                     sems.at[slot])
    _start_runs(i, toff_ref, dst_ref, c8_ref, make)

    def wait_tile(t, sl):
        n = _tile_rows(t, toff_ref, c8_ref)
        pltpu.make_async_copy(sbuf.at[sl, pl.ds(0, n)], xb_ref.at[pl.ds(0, n)], sems.at[sl]).wait()

    @pl.when(i > 0)
    def _():
        wait_tile(i - 1, 1 - slot)

    @pl.when(i == pl.num_programs(0) - 1)
    def _():
        wait_tile(i, slot)
        zbuf[...] = jnp.zeros_like(zbuf)

        def fill(action):
            def tail(e, carry):
                n = pl.multiple_of(tlen_ref[e], ROW_ALIGN)
                do = pl.multiple_of(tstart_ref[e], ROW_ALIGN)

                @pl.when(n > 0)
                def _():
                    action(pltpu.make_async_copy(zbuf.at[pl.ds(0, n)], xb_ref.at[pl.ds(do, n)], sem))
                return carry
            lax.fori_loop(0, N_EXPERTS, tail, 0)

            def unused(b, carry):
                do = pl.multiple_of(b * BM, BM)
                action(pltpu.make_async_copy(zbuf, xb_ref.at[pl.ds(do, BM)], sem))
                return carry
            lax.fori_loop(nu_ref[0], xb_ref.shape[0] // BM, unused, 0)
        fill(lambda c: c.start())
        fill(lambda c: c.wait())


def _dispatch(toff, dst, c8s, tstart, tlen, n_used, h2, ti, p_rows):
    n, d = h2.shape
    return pl.pallas_call(
        _dispatch_body,
        out_shape=jax.ShapeDtypeStruct((p_rows, d), f32),
        grid_spec=pltpu.PrefetchScalarGridSpec(
            num_scalar_prefetch=6,
            grid=(n // TS,),
            in_specs=[pl.BlockSpec((TS, d), lambda i, *_: (i, 0)),
                      pl.BlockSpec((TOP_K, TS), lambda i, *_: (0, i))],
            out_specs=pl.BlockSpec(memory_space=pl.ANY),
            scratch_shapes=[pltpu.VMEM((2, RS, d), f32), pltpu.VMEM((BM, d), f32),
                            pltpu.SemaphoreType.DMA((2,)), pltpu.SemaphoreType.DMA(())]),
        compiler_params=_params(("arbitrary",)),
        name="dispatch",
    )(toff, dst, c8s, tstart, tlen, n_used, h2, ti)


def _expert_body(be_ref, nu_ref, x_ref, wgu_ref, bgu_ref, wd_ref, bd_ref, o_ref,
                 wgu_bf, wd_bf):
    b = pl.program_id(0)
    d = D_MODEL

    @pl.when(b >= nu_ref[0])
    def _():
        o_ref[...] = jnp.zeros_like(o_ref)

    @pl.when(b < nu_ref[0])
    def _():
        new_expert = jnp.logical_or(b == 0, be_ref[b] != be_ref[jnp.maximum(b - 1, 0)])

        @pl.when(new_expert)
        def _():
            wgu_bf[...] = wgu_ref[0].astype(bf16)
            wd_bf[...] = wd_ref[0].astype(bf16)

        gu = jnp.dot(x_ref[...].astype(bf16), wgu_bf[...], preferred_element_type=f32) + bgu_ref[0]
        glu = jnp.minimum(gu[:, :d], SWIGLU_LIMIT)
        lin = jnp.clip(gu[:, d:], -SWIGLU_LIMIT, SWIGLU_LIMIT)
        act = glu * _sigmoid(SWIGLU_ALPHA * glu) * (lin + 1.0)
        o_ref[...] = jnp.dot(act.astype(bf16), wd_bf[...], preferred_element_type=f32) + bd_ref[0]


def _experts(blk_e, n_used, xb, wgu, bgu, wd, bd):
    p_rows, d = xb.shape
    rows = pl.BlockSpec((BM, d), lambda b, be, nu: (b, 0))
    per_e = lambda a: pl.BlockSpec((1,) + a.shape[1:], lambda b, be, nu: (be[b], 0, 0))
    return pl.pallas_call(
        _expert_body,
        out_shape=jax.ShapeDtypeStruct((p_rows, d), f32),
        grid_spec=pltpu.PrefetchScalarGridSpec(
            num_scalar_prefetch=2,
            grid=(p_rows // BM,),
            in_specs=[rows, per_e(wgu), per_e(bgu), per_e(wd), per_e(bd)],
            out_specs=rows,
            scratch_shapes=[pltpu.VMEM(wgu.shape[1:], bf16), pltpu.VMEM(wd.shape[1:], bf16)]),
        compiler_params=_params(("arbitrary",)),
        name="experts",
    )(blk_e, n_used, xb, wgu, bgu, wd, bd)


def _combine_body(toff_ref, dst_ref, c8_ref, yb_ref, ti_ref, gt_ref, x1_ref, mod_ref, g_ref,
                  o_ref, sbuf, sems):
    i = pl.program_id(0)
    slot = lax.rem(i, 2)

    def fetch(t, sl):
        def make(so, do, n):
            return pltpu.make_async_copy(yb_ref.at[pl.ds(do, n)], sbuf.at[sl, pl.ds(so, n)],
                                         sems.at[sl])
        _start_runs(t, toff_ref, dst_ref, c8_ref, make)

    @pl.when(i == 0)
    def _():
        sbuf[...] = jnp.zeros_like(sbuf)
        fetch(i, slot)

    @pl.when(i + 1 < pl.num_programs(0))
    def _():
        fetch(i + 1, 1 - slot)

    p = _sorted_positions(ti_ref[...])
    gt = gt_ref[...]
    r = lax.broadcasted_iota(i32, (RS, TS), 0)
    m = [r == p[k] for k in range(TOP_K)]
    pm = jnp.where(m[0], 1.0, jnp.where(m[1], 1.0, jnp.where(m[2], 1.0, jnp.where(m[3], 1.0, 0.0))))
    gm = jnp.where(m[0], gt[0:1, :], jnp.where(m[1], gt[1:2, :],
         jnp.where(m[2], gt[2:3, :], jnp.where(m[3], gt[3:4, :], 0.0))))
    grow = jnp.sum(gm, axis=1, keepdims=True)

    n = _tile_rows(i, toff_ref, c8_ref)
    pltpu.make_async_copy(yb_ref.at[pl.ds(0, n)], sbuf.at[slot, pl.ds(0, n)], sems.at[slot]).wait()
    sg = (sbuf[slot] * grow).astype(bf16)
    y = lax.dot_general(pm.astype(bf16), sg, (((0,), (0,)), ((), ())), preferred_element_type=f32)
    o_ref[...] = x1_ref[...] + mod_ref[0, 5:6, :] * _rms(y, g_ref[...])


def _combine(toff, dst, c8s, yb, ti, gt, x1, mod3, g, tiles_per_batch):
    n, d = x1.shape
    tok = pl.BlockSpec((TS, d), lambda i, *_: (i, 0))
    lane = pl.BlockSpec((TOP_K, TS), lambda i, *_: (0, i))
    return pl.pallas_call(
        _combine_body,
        out_shape=jax.ShapeDtypeStruct((n, d), f32),
        grid_spec=pltpu.PrefetchScalarGridSpec(
            num_scalar_prefetch=3,
            grid=(n // TS,),
            in_specs=[pl.BlockSpec(memory_space=pl.ANY), lane, lane, tok,
                      pl.BlockSpec((1, 6, d), lambda i, *_: (i // tiles_per_batch, 0, 0)),
                      pl.BlockSpec(g.shape, lambda i, *_: (0, 0))],
            out_specs=tok,
            scratch_shapes=[pltpu.VMEM((2, RS, d), f32), pltpu.SemaphoreType.DMA((2,))]),
        compiler_params=_params(("arbitrary",)),
        name="combine",
    )(toff, dst, c8s, yb, ti, gt, x1, mod3, g)


def _layout_tables(tab, nt, p_rows):
    c8 = tab[:, :nt].T.astype(i32)
    toff = jnp.cumsum(c8, axis=1) - c8
    len8 = jnp.sum(c8, axis=0)
    seg = (len8 + BM - 1) // BM * BM
    gend = jnp.cumsum(seg)
    gstart = gend - seg
    dst = jnp.cumsum(c8, axis=0) - c8 + gstart[None, :]
    n_used = gend[-1] // BM
    blk = jnp.arange(p_rows // BM, dtype=i32)
    last = jnp.minimum(blk, n_used - 1)
    blk_e = jnp.sum((gend[None, :] <= (last * BM)[:, None]).astype(i32), axis=1)
    blk_e = jnp.minimum(blk_e, N_EXPERTS - 1)
    return (toff.reshape(-1), dst.reshape(-1), c8.reshape(-1), gstart + len8, seg - len8,
            blk_e, n_used.reshape(1).astype(i32))


def kernel(x, c, w_ada, b_ada, g_pre_mix, g_post_mix, w_in, rel_bias, sgu_ln_g, sgu_ln_b,
           w_spatial, b_spatial, w_branch_a, w_branch_b, w_gate, b_gate, w_out,
           g_pre_ffn, g_post_ffn, w_router, b_router, w_gate_up, b_gate_up, w_down, b_down):
    b, s, d = x.shape
    assert d == D_MODEL and s % TM == 0 and (b * s) % TS == 0 and s % TS == 0
    n = b * s
    nt = n // TS
    ntp = -(-nt // LANES) * LANES
    p_rows = -(-(n * TOP_K + nt * N_EXPERTS * (ROW_ALIGN - 1) + N_EXPERTS * (BM - 1)) // BM) * BM
    depth = w_ada.shape[0]
    c8 = jnp.pad(c, ((0, 8 - b), (0, 0)))
    row = lambda a: a.reshape(1, -1)

    for l in range(depth):
        mod = _ada(c8, w_ada[l], row(b_ada[l]))[:b]
        mod3 = mod.reshape(b, 6, d)

        q, k, v, u, vv = _proj(x, mod3, row(g_pre_mix[l]), w_in[l].astype(bf16),
                               row(sgu_ln_g[l]), row(sgu_ln_b[l]))
        ya = _attn(q, k, v, _attn_bias(rel_bias[l]))
        ws2 = w_spatial[l].astype(bf16).reshape(-1, 2 * SGU_BLOCK, SGU_BLOCK)
        bsf = jnp.repeat(b_spatial[l].T, SGU_WIDTH // b_spatial.shape[1], axis=1)
        x1 = _mix(x, mod3, row(g_pre_mix[l]), row(g_post_mix[l]), u, vv, ya, ws2, bsf,
                  w_branch_a[l].astype(bf16), w_branch_b[l].astype(bf16),
                  w_gate[l].astype(bf16), row(b_gate[l]), w_out[l].astype(bf16))

        x1f = x1.reshape(n, d)
        h2, ti, gt, tab = _router(x1f, mod3, row(g_pre_ffn[l]), w_router[l].T.astype(bf16),
                                  b_router[l].reshape(-1, 1), s // TS, ntp)
        toff, dst, c8s, tstart, tlen, blk_e, n_used = _layout_tables(tab, nt, p_rows)
        xb = _dispatch(toff, dst, c8s, tstart, tlen, n_used, h2, ti, p_rows)
        yb = _experts(blk_e, n_used, xb, w_gate_up[l], b_gate_up[l][:, None, :],
                      w_down[l], b_down[l][:, None, :])
        x = _combine(toff, dst, c8s, yb, ti, gt, x1f, mod3, row(g_post_ffn[l]), s // TS).reshape(b, s, d)
    return x
```

```python
import functools

import jax
import jax.numpy as jnp
from jax import lax
from jax.experimental import pallas as pl
from jax.experimental.pallas import tpu as pltpu

bf16 = jnp.bfloat16
f32 = jnp.float32
i32 = jnp.int32

D_MODEL = 1024
CHUNK = 64
N_LEFT = 8
ATT_HEADS = 8
HEAD_DIM = 64
ATT_WIDTH = 512
MAX_REL = 128
SGU_BLOCK = 128
SGU_WIDTH = 512
N_EXPERTS = 32
TOP_K = 4
SWIGLU_LIMIT = 7.0
SWIGLU_ALPHA = 1.702
EPS = 1e-6
NEG = -1e30

LANES = 128
ROW_ALIGN = 8
TM = 512
QG = 2 * CHUNK
KBAND = (N_LEFT + 2) * CHUNK
TS = 256
RS = TS * TOP_K + N_EXPERTS * ROW_ALIGN
BM = 512
VMEM_LIMIT = 56 * 2**20


def _params(sem):
    return pltpu.CompilerParams(dimension_semantics=sem, vmem_limit_bytes=VMEM_LIMIT)


def _adaln(x, g, sc, sh):
    ms = jnp.mean(x * x, axis=-1, keepdims=True)
    return (x * lax.rsqrt(ms + EPS) * g) * (1.0 + sc) + sh


def _rms(x, g):
    ms = jnp.mean(x * x, axis=-1, keepdims=True)
    return x * lax.rsqrt(ms + EPS) * g


def _sigmoid(x):
    return 1.0 / (1.0 + jnp.exp(-x))


def _ada_body(c_ref, w_ref, b_ref, o_ref):
    c = c_ref[...]
    ca = c * _sigmoid(c)
    o_ref[...] = jnp.dot(ca.astype(bf16), w_ref[...].astype(bf16),
                         preferred_element_type=f32) + b_ref[...]


def _ada(c8, w, b):
    d = w.shape[0]
    n = w.shape[1] // d
    return pl.pallas_call(
        _ada_body,
        out_shape=jax.ShapeDtypeStruct((8, n * d), f32),
        grid=(n,),
        in_specs=[pl.BlockSpec((8, d), lambda j: (0, 0)),
                  pl.BlockSpec((d, d), lambda j: (0, j)),
                  pl.BlockSpec((1, d), lambda j: (0, j))],
        out_specs=pl.BlockSpec((8, d), lambda j: (0, j)),
        compiler_params=_params(("arbitrary",)),
        name="ada",
    )(c8, w, b)


def _proj_body(x_ref, mod_ref, g_ref, w_ref, lng_ref, lnb_ref,
               q_ref, k_ref, v_ref, u_ref, vv_ref):
    h = _adaln(x_ref[0], g_ref[...], mod_ref[0, 1:2, :], mod_ref[0, 0:1, :])
    p = jnp.dot(h.astype(bf16), w_ref[...], preferred_element_type=f32)
    aw = ATT_WIDTH
    q_ref[0] = (p[:, 0:aw] * (HEAD_DIM ** -0.5)).astype(bf16)
    k_ref[0] = p[:, aw:2 * aw].astype(bf16)
    v_ref[0] = p[:, 2 * aw:3 * aw].astype(bf16)
    z = p[:, 3 * aw:]
    zg = 0.5 * z * (1.0 + lax.erf(z * (2.0 ** -0.5)))
    u_ref[0] = zg[:, :SGU_WIDTH].astype(bf16)
    vv = zg[:, SGU_WIDTH:]
    mu = jnp.mean(vv, axis=-1, keepdims=True)
    var = jnp.mean(jnp.square(vv - mu), axis=-1, keepdims=True)
    vn = (vv - mu) * lax.rsqrt(var + EPS) * lng_ref[...] + lnb_ref[...]
    vv_ref[0] = vn.astype(bf16)


def _proj(x, mod3, g, w_in, lng, lnb):
    b, s, d = x.shape
    tok = lambda w: pl.BlockSpec((1, TM, w), lambda bi, i: (bi, i, 0))
    full = lambda a: pl.BlockSpec(a.shape, lambda bi, i: (0,) * a.ndim)
    o512 = jax.ShapeDtypeStruct((b, s, ATT_WIDTH), bf16)
    return pl.pallas_call(
        _proj_body,
        out_shape=(o512,) * 5,
        grid=(b, s // TM),
        in_specs=[tok(d), pl.BlockSpec((1, 6, d), lambda bi, i: (bi, 0, 0)),
                  full(g), full(w_in), full(lng), full(lnb)],
        out_specs=(tok(ATT_WIDTH),) * 5,
        compiler_params=_params(("parallel", "arbitrary")),
        name="proj",
    )(x, mod3, g, w_in, lng, lnb)


def _attn_body(q_ref, kp_ref, kc_ref, vp_ref, vc_ref, bias_ref, o_ref, kbuf, vbuf):
    first = pl.program_id(1) == 0
    kbuf[0:TM, :] = kp_ref[0]
    kbuf[TM:2 * TM, :] = kc_ref[0]
    vbuf[0:TM, :] = vp_ref[0]
    vbuf[TM:2 * TM, :] = vc_ref[0]
    lo = lax.broadcasted_iota(i32, (QG, LANES), 1) < HEAD_DIM
    col = lax.broadcasted_iota(i32, (2 * QG, KBAND), 1)

    def group(p, carry):
        r0 = pl.multiple_of(p * QG, QG)
        pad_keys = jnp.logical_and(first, col + r0 < TM)
        for hp in range(ATT_HEADS // 2):
            c0 = hp * LANES
            qp = q_ref[0, pl.ds(r0, QG), c0:c0 + LANES]
            zero = jnp.zeros_like(qp)
            q2 = jnp.concatenate([jnp.where(lo, qp, zero), jnp.where(lo, zero, qp)], axis=0)
            kb = kbuf[pl.ds(r0, KBAND), c0:c0 + LANES]
            s = lax.dot_general(q2, kb, (((1,), (1,)), ((), ())), preferred_element_type=f32)
            s = s + bias_ref[hp]
            s = jnp.where(pad_keys, NEG, s)
            m = jnp.max(s, axis=-1, keepdims=True)
            e = jnp.exp(s - m)
            l = jnp.sum(e, axis=-1, keepdims=True)
            vb = vbuf[pl.ds(r0, KBAND), c0:c0 + LANES]
            o2 = jnp.dot(e.astype(bf16), vb, preferred_element_type=f32) / l
            o = jnp.where(lo, o2[:QG], o2[QG:])
            o_ref[0, pl.ds(r0, QG), c0:c0 + LANES] = o.astype(bf16)
        return carry

    lax.fori_loop(0, TM // QG, group, 0)


def _attn(q, k, v, bias2):
    b, s, w = q.shape
    cur = pl.BlockSpec((1, TM, w), lambda bi, i: (bi, i, 0))
    prev = pl.BlockSpec((1, TM, w), lambda bi, i: (bi, jnp.maximum(i - 1, 0), 0))
    return pl.pallas_call(
        _attn_body,
        out_shape=jax.ShapeDtypeStruct((b, s, w), bf16),
        grid=(b, s // TM),
        in_specs=[cur, prev, cur, prev, cur,
                  pl.BlockSpec(bias2.shape, lambda bi, i: (0, 0, 0))],
        out_specs=cur,
        scratch_shapes=[pltpu.VMEM((2 * TM, w), bf16), pltpu.VMEM((2 * TM, w), bf16)],
        compiler_params=_params(("parallel", "arbitrary")),
        name="attn",
    )(q, k, k, v, v, bias2)


def _attn_bias(rel_bias):
    h = rel_bias.shape[0]
    period = 1024
    assert period >= QG + KBAND - 1 and KBAND - 2 * MAX_REL >= 0
    far = jnp.broadcast_to(rel_bias[:, 2 * MAX_REL:], (h, period))
    near = rel_bias[:, :0:-1]
    v = jnp.concatenate([far[:, :KBAND - 2 * MAX_REL], near, far[:, KBAND:]], axis=1).astype(f32)
    flat = jnp.broadcast_to(v[:, None, :], (h, QG, period)).reshape(h, QG * period)
    toep = flat[:, :QG * (period - 1)].reshape(h, QG, period - 1)[:, :, :KBAND]
    i = jnp.arange(QG, dtype=i32)[:, None]
    j = jnp.arange(KBAND, dtype=i32)[None, :]
    jb = j - (i // CHUNK) * CHUNK
    valid = jnp.logical_and(jb >= 0, jb < CHUNK * (N_LEFT + 1))
    bias = jnp.where(valid[None], toep, NEG)
    return bias.reshape(ATT_HEADS // 2, 2 * QG, KBAND)


def _mix_body(x_ref, mod_ref, gpre_ref, gpost_ref, u_ref, vv_ref, ya_ref, ws_ref, bs_ref,
              wa_ref, wb_ref, wg_ref, bg_ref, wo_ref, o_ref, ybuf):
    d = D_MODEL
    x = x_ref[0]
    h = _adaln(x, gpre_ref[...], mod_ref[0, 1:2, :], mod_ref[0, 0:1, :]).astype(bf16)
    gs = _sigmoid(jnp.dot(h, wg_ref[...], preferred_element_type=f32) + bg_ref[...])

    blk = SGU_BLOCK
    row = lax.broadcasted_iota(i32, (2 * blk, blk), 0)
    colv = lax.broadcasted_iota(i32, (2 * blk, blk), 1)
    causal = colv <= jnp.bitwise_and(row, blk - 1)
    lo = lax.broadcasted_iota(i32, (blk, LANES), 1) < (LANES // 2)
    for gp in range(SGU_WIDTH // LANES):
        c0 = gp * LANES
        w2 = ws_ref[gp]
        w2 = jnp.where(causal, w2, jnp.zeros_like(w2))
        for bi in range(TM // blk):
            r0 = bi * blk
            s2 = jnp.dot(w2, vv_ref[0, r0:r0 + blk, c0:c0 + LANES], preferred_element_type=f32)
            s = jnp.where(lo, s2[:blk], s2[blk:]) + bs_ref[:, c0:c0 + LANES]
            yb = u_ref[0, r0:r0 + blk, c0:c0 + LANES].astype(f32) * s
            ybuf[r0:r0 + blk, c0:c0 + LANES] = yb.astype(bf16)

    a = jnp.dot(ya_ref[0], wa_ref[...], preferred_element_type=f32)
    bb = jnp.dot(ybuf[...], wb_ref[...], preferred_element_type=f32)
    merged = gs[:, :d] * a + gs[:, d:] * bb
    y = jnp.dot(merged.astype(bf16), wo_ref[...], preferred_element_type=f32)
    o_ref[0] = x + mod_ref[0, 2:3, :] * _rms(y, gpost_ref[...])


def _mix(x, mod3, gpre, gpost, u, vv, ya, ws2, bsf, wa, wb, wg, bg, wo):
    b, s, d = x.shape
    tok = lambda w: pl.BlockSpec((1, TM, w), lambda bi, i: (bi, i, 0))
    full = lambda a: pl.BlockSpec(a.shape, lambda bi, i: (0,) * a.ndim)
    return pl.pallas_call(
        _mix_body,
        out_shape=jax.ShapeDtypeStruct((b, s, d), f32),
        grid=(b, s // TM),
        in_specs=[tok(d), pl.BlockSpec((1, 6, d), lambda bi, i: (bi, 0, 0)),
                  full(gpre), full(gpost), tok(SGU_WIDTH), tok(SGU_WIDTH), tok(ATT_WIDTH),
                  full(ws2), full(bsf), full(wa), full(wb), full(wg), full(bg), full(wo)],
        out_specs=tok(d),
        scratch_shapes=[pltpu.VMEM((TM, SGU_WIDTH), bf16)],
        compiler_params=_params(("parallel", "arbitrary")),
        name="mix",
    )(x, mod3, gpre, gpost, u, vv, ya, ws2, bsf, wa, wb, wg, bg, wo)


def _router_body(x_ref, mod_ref, g_ref, wr_ref, br_ref, h2_ref, ti_ref, gt_ref, tab_ref):
    i = pl.program_id(0)
    h = _adaln(x_ref[...], g_ref[...], mod_ref[0, 4:5, :], mod_ref[0, 3:4, :])
    hb = h.astype(bf16)
    h2_ref[...] = hb
    lg = lax.dot_general(wr_ref[...], hb, (((1,), (1,)), ((), ())),
                         preferred_element_type=f32) + br_ref[...]
    e_iota = lax.broadcasted_iota(i32, lg.shape, 0)
    vals, idxs = [], []
    cnt = jnp.zeros((N_EXPERTS, 1), f32)
    for _ in range(TOP_K):
        m = jnp.max(lg, axis=0, keepdims=True)
        idx = jnp.min(jnp.where(lg == m, e_iota, N_EXPERTS), axis=0, keepdims=True)
        hit = e_iota == idx
        cnt = cnt + jnp.sum(hit.astype(f32), axis=1, keepdims=True)
        lg = jnp.where(hit, -jnp.inf, lg)
        vals.append(m)
        idxs.append(idx)
    ex = [jnp.exp(v - vals[0]) for v in vals]
    den = ex[0] + ex[1] + ex[2] + ex[3]
    ti_ref[...] = jnp.concatenate(idxs, axis=0)
    gt_ref[...] = jnp.concatenate([e / den for e in ex], axis=0)
    c8 = jnp.floor((cnt + (ROW_ALIGN - 1.0)) * (1.0 / ROW_ALIGN)) * ROW_ALIGN

    @pl.when(i == 0)
    def _():
        tab_ref[...] = jnp.zeros_like(tab_ref)

    lane = lax.broadcasted_iota(i32, tab_ref.shape, 1)
    tab_ref[...] += jnp.where(lane == i, c8, 0.0)


def _router(x1, mod3, g, wrt, br, tiles_per_batch, ntp):
    n, d = x1.shape
    nt = n // TS
    tok = pl.BlockSpec((TS, d), lambda i: (i, 0))
    full = lambda a: pl.BlockSpec(a.shape, lambda i: (0,) * a.ndim)
    lane = pl.BlockSpec((TOP_K, TS), lambda i: (0, i))
    return pl.pallas_call(
        _router_body,
        out_shape=(jax.ShapeDtypeStruct((n, d), bf16),
                   jax.ShapeDtypeStruct((TOP_K, n), i32),
                   jax.ShapeDtypeStruct((TOP_K, n), f32),
                   jax.ShapeDtypeStruct((N_EXPERTS, ntp), f32)),
        grid=(nt,),
        in_specs=[tok, pl.BlockSpec((1, 6, d), lambda i: (i // tiles_per_batch, 0, 0)),
                  full(g), full(wrt), full(br)],
        out_specs=(tok, lane, lane, pl.BlockSpec((N_EXPERTS, ntp), lambda i: (0, 0))),
        compiler_params=_params(("arbitrary",)),
        name="router",
    )(x1, mod3, g, wrt, br)


def _sorted_positions(ti):
    ts = ti.shape[1]
    e_iota = lax.broadcasted_iota(i32, (N_EXPERTS, ts), 0)
    upper = (lax.broadcasted_iota(i32, (ts, ts), 0) < lax.broadcasted_iota(i32, (ts, ts), 1))
    upper = jnp.where(upper, 1.0, 0.0).astype(bf16)
    hits, prefs, cnts = [], [], []
    for k in range(TOP_K):
        hit = ti[k:k + 1, :] == e_iota
        hf = jnp.where(hit, 1.0, 0.0)
        prefs.append(jnp.dot(hf.astype(bf16), upper, preferred_element_type=f32))
        cnts.append(jnp.sum(hf, axis=1, keepdims=True))
        hits.append(hit)
    total = cnts[0] + cnts[1] + cnts[2] + cnts[3]
    c8 = jnp.floor((total + (ROW_ALIGN - 1.0)) * (1.0 / ROW_ALIGN)) * ROW_ALIGN
    lower = (lax.broadcasted_iota(i32, (N_EXPERTS, N_EXPERTS), 1)
             < lax.broadcasted_iota(i32, (N_EXPERTS, N_EXPERTS), 0))
    lower = jnp.where(lower, 1.0, 0.0).astype(bf16)
    c8b = jnp.broadcast_to(c8, (N_EXPERTS, LANES)).astype(bf16)
    start = jnp.dot(lower, c8b, preferred_element_type=f32)[:, 0:1]
    pos = []
    for k in range(TOP_K):
        pe = start + prefs[k]
        pos.append(jnp.sum(jnp.where(hits[k], pe, 0.0), axis=0, keepdims=True).astype(i32))
        start = start + cnts[k]
    return pos


def _start_runs(i, toff_ref, dst_ref, c8_ref, make):
    def one(e, carry):
        j = i * N_EXPERTS + e
        n = pl.multiple_of(c8_ref[j], ROW_ALIGN)
        so = pl.multiple_of(toff_ref[j], ROW_ALIGN)
        do = pl.multiple_of(dst_ref[j], ROW_ALIGN)

        @pl.when(n > 0)
        def _():
            make(so, do, n).start()
        return carry
    lax.fori_loop(0, N_EXPERTS, one, 0)


def _tile_rows(i, toff_ref, c8_ref):
    j = i * N_EXPERTS + (N_EXPERTS - 1)
    return pl.multiple_of(toff_ref[j] + c8_ref[j], ROW_ALIGN)


def _dispatch_body(toff_ref, dst_ref, c8_ref, tstart_ref, tlen_ref, nu_ref,
                   h2_ref, ti_ref, xb_ref, sbuf, zbuf, sems, sem):
    i = pl.program_id(0)
    slot = lax.rem(i, 2)
    p = _sorted_positions(ti_ref[...])
    r = lax.broadcasted_iota(i32, (RS, TS), 0)
    pm = jnp.where(r == p[0], 1.0, jnp.where(r == p[1], 1.0,
         jnp.where(r == p[2], 1.0, jnp.where(r == p[3], 1.0, 0.0))))
    sbuf[slot] = jnp.dot(pm.astype(bf16), h2_ref[...], preferred_element_type=f32)

    def make(so, do, n):
        return pltpu.make_async_copy(sbuf.at[slot, pl.ds(so, n)], xb_ref.at[pl.ds(do, n)],
                                     sems.at[slot])
    _start_runs(i, toff_ref, dst_ref, c8_ref, make)

    def wait_tile(t, sl):
        n = _tile_rows(t, toff_ref, c8_ref)
        pltpu.make_async_copy(sbuf.at[sl, pl.ds(0, n)], xb_ref.at[pl.ds(0, n)], sems.at[sl]).wait()

    @pl.when(i > 0)
    def _():
        wait_tile(i - 1, 1 - slot)

    @pl.when(i == pl.num_programs(0) - 1)
    def _():
        wait_tile(i, slot)
        zbuf[...] = jnp.zeros_like(zbuf)

        def fill(action):
            def tail(e, carry):
                n = pl.multiple_of(tlen_ref[e], ROW_ALIGN)
                do = pl.multiple_of(tstart_ref[e], ROW_ALIGN)

                @pl.when(n > 0)
                def _():
                    action(pltpu.make_async_copy(zbuf.at[pl.ds(0, n)], xb_ref.at[pl.ds(do, n)], sem))
                return carry
            lax.fori_loop(0, N_EXPERTS, tail, 0)

            def unused(b, carry):
                do = pl.multiple_of(b * BM, BM)
                action(pltpu.make_async_copy(zbuf, xb_ref.at[pl.ds(do, BM)], sem))
                return carry
            lax.fori_loop(nu_ref[0], xb_ref.shape[0] // BM, unused, 0)
        fill(lambda c: c.start())
        fill(lambda c: c.wait())


def _dispatch(toff, dst, c8s, tstart, tlen, n_used, h2, ti, p_rows):
    n, d = h2.shape
    return pl.pallas_call(
        _dispatch_body,
        out_shape=jax.ShapeDtypeStruct((p_rows, d), f32),
        grid_spec=pltpu.PrefetchScalarGridSpec(
            num_scalar_prefetch=6,
            grid=(n // TS,),
            in_specs=[pl.BlockSpec((TS, d), lambda i, *_: (i, 0)),
                      pl.BlockSpec((TOP_K, TS), lambda i, *_: (0, i))],
            out_specs=pl.BlockSpec(memory_space=pl.ANY),
            scratch_shapes=[pltpu.VMEM((2, RS, d), f32), pltpu.VMEM((BM, d), f32),
                            pltpu.SemaphoreType.DMA((2,)), pltpu.SemaphoreType.DMA(())]),
        compiler_params=_params(("arbitrary",)),
        name="dispatch",
    )(toff, dst, c8s, tstart, tlen, n_used, h2, ti)


def _expert_body(be_ref, nxt_ref, nu_ref, x_ref, wgu_hbm, bgu_ref, wd_hbm, bd_ref, o_ref,
                 wgu_st, wd_st, wgu_bf, wd_bf, sems):
    b = pl.program_id(0)
    d = D_MODEL
    e = be_ref[b]

    def weight_copies(ex):
        return (pltpu.make_async_copy(wgu_hbm.at[ex], wgu_st, sems.at[0]),
                pltpu.make_async_copy(wd_hbm.at[ex], wd_st, sems.at[1]))

    @pl.when(b >= nu_ref[0])
    def _():
        o_ref[...] = jnp.zeros_like(o_ref)

    @pl.when(b < nu_ref[0])
    def _():
        @pl.when(b == 0)
        def _():
            for c in weight_copies(e):
                c.start()

        @pl.when(jnp.logical_or(b == 0, e != be_ref[jnp.maximum(b - 1, 0)]))
        def _():
            for c in weight_copies(e):
                c.wait()
            wgu_bf[...] = wgu_st[...].astype(bf16)
            wd_bf[...] = wd_st[...].astype(bf16)
            nxt = nxt_ref[e]

            @pl.when(nxt >= 0)
            def _():
                for c in weight_copies(nxt):
                    c.start()

        gu = jnp.dot(x_ref[...].astype(bf16), wgu_bf[...], preferred_element_type=f32) + bgu_ref[0]
        glu = jnp.minimum(gu[:, :d], SWIGLU_LIMIT)
        lin = jnp.clip(gu[:, d:], -SWIGLU_LIMIT, SWIGLU_LIMIT)
        act = glu * _sigmoid(SWIGLU_ALPHA * glu) * (lin + 1.0)
        o_ref[...] = jnp.dot(act.astype(bf16), wd_bf[...], preferred_element_type=f32) + bd_ref[0]


def _experts(blk_e, nxt_e, n_used, xb, wgu, bgu, wd, bd):
    p_rows, d = xb.shape
    rows = pl.BlockSpec((BM, d), lambda b, be, nx, nu: (b, 0))
    per_e = lambda a: pl.BlockSpec((1,) + a.shape[1:], lambda b, be, nx, nu: (be[b], 0, 0))
    hbm = pl.BlockSpec(memory_space=pl.ANY)
    return pl.pallas_call(
        _expert_body,
        out_shape=jax.ShapeDtypeStruct((p_rows, d), f32),
        grid_spec=pltpu.PrefetchScalarGridSpec(
            num_scalar_prefetch=3,
            grid=(p_rows // BM,),
            in_specs=[rows, hbm, per_e(bgu), hbm, per_e(bd)],
            out_specs=rows,
            scratch_shapes=[pltpu.VMEM(wgu.shape[1:], f32), pltpu.VMEM(wd.shape[1:], f32),
                            pltpu.VMEM(wgu.shape[1:], bf16), pltpu.VMEM(wd.shape[1:], bf16),
                            pltpu.SemaphoreType.DMA((2,))]),
        compiler_params=_params(("arbitrary",)),
        name="experts",
    )(blk_e, nxt_e, n_used, xb, wgu, bgu, wd, bd)


def _combine_body(toff_ref, dst_ref, c8_ref, yb_ref, ti_ref, gt_ref, x1_ref, mod_ref, g_ref,
                  o_ref, sbuf, sems):
    i = pl.program_id(0)
    slot = lax.rem(i, 2)

    def fetch(t, sl):
        def make(so, do, n):
            return pltpu.make_async_copy(yb_ref.at[pl.ds(do, n)], sbuf.at[sl, pl.ds(so, n)],
                                         sems.at[sl])
        _start_runs(t, toff_ref, dst_ref, c8_ref, make)

    @pl.when(i == 0)
    def _():
        sbuf[...] = jnp.zeros_like(sbuf)
        fetch(i, slot)

    @pl.when(i + 1 < pl.num_programs(0))
    def _():
        fetch(i + 1, 1 - slot)

    p = _sorted_positions(ti_ref[...])
    gt = gt_ref[...]
    r = lax.broadcasted_iota(i32, (RS, TS), 0)
    m = [r == p[k] for k in range(TOP_K)]
    pm = jnp.where(m[0], 1.0, jnp.where(m[1], 1.0, jnp.where(m[2], 1.0, jnp.where(m[3], 1.0, 0.0))))
    gm = jnp.where(m[0], gt[0:1, :], jnp.where(m[1], gt[1:2, :],
         jnp.where(m[2], gt[2:3, :], jnp.where(m[3], gt[3:4, :], 0.0))))
    grow = jnp.sum(gm, axis=1, keepdims=True)

    n = _tile_rows(i, toff_ref, c8_ref)
    pltpu.make_async_copy(yb_ref.at[pl.ds(0, n)], sbuf.at[slot, pl.ds(0, n)], sems.at[slot]).wait()
    sg = (sbuf[slot] * grow).astype(bf16)
    y = lax.dot_general(pm.astype(bf16), sg, (((0,), (0,)), ((), ())), preferred_element_type=f32)
    o_ref[...] = x1_ref[...] + mod_ref[0, 5:6, :] * _rms(y, g_ref[...])


def _combine(toff, dst, c8s, yb, ti, gt, x1, mod3, g, tiles_per_batch):
    n, d = x1.shape
    tok = pl.BlockSpec((TS, d), lambda i, *_: (i, 0))
    lane = pl.BlockSpec((TOP_K, TS), lambda i, *_: (0, i))
    return pl.pallas_call(
        _combine_body,
        out_shape=jax.ShapeDtypeStruct((n, d), f32),
        grid_spec=pltpu.PrefetchScalarGridSpec(
            num_scalar_prefetch=3,
            grid=(n // TS,),
            in_specs=[pl.BlockSpec(memory_space=pl.ANY), lane, lane, tok,
                      pl.BlockSpec((1, 6, d), lambda i, *_: (i // tiles_per_batch, 0, 0)),
                      pl.BlockSpec(g.shape, lambda i, *_: (0, 0))],
            out_specs=tok,
            scratch_shapes=[pltpu.VMEM((2, RS, d), f32), pltpu.SemaphoreType.DMA((2,))]),
        compiler_params=_params(("arbitrary",)),
        name="combine",
    )(toff, dst, c8s, yb, ti, gt, x1, mod3, g)


def _layout_tables(tab, nt, p_rows):
    c8 = tab[:, :nt].T.astype(i32)
    toff = jnp.cumsum(c8, axis=1) - c8
    len8 = jnp.sum(c8, axis=0)
    seg = (len8 + BM - 1) // BM * BM
    gend = jnp.cumsum(seg)
    gstart = gend - seg
    dst = jnp.cumsum(c8, axis=0) - c8 + gstart[None, :]
    n_used = gend[-1] // BM
    blk = jnp.arange(p_rows // BM, dtype=i32)
    last = jnp.minimum(blk, n_used - 1)
    blk_e = jnp.sum((gend[None, :] <= (last * BM)[:, None]).astype(i32), axis=1)
    blk_e = jnp.minimum(blk_e, N_EXPERTS - 1)
    ids = jnp.arange(N_EXPERTS, dtype=i32)
    later = jnp.logical_and(ids[None, :] > ids[:, None], (seg > 0)[None, :])
    nxt_e = jnp.min(jnp.where(later, ids[None, :], N_EXPERTS), axis=1)
    nxt_e = jnp.where(nxt_e == N_EXPERTS, -1, nxt_e).astype(i32)
    return (toff.reshape(-1), dst.reshape(-1), c8.reshape(-1), gstart + len8, seg - len8,
            blk_e, nxt_e, n_used.reshape(1).astype(i32))


def kernel(x, c, w_ada, b_ada, g_pre_mix, g_post_mix, w_in, rel_bias, sgu_ln_g, sgu_ln_b,
           w_spatial, b_spatial, w_branch_a, w_branch_b, w_gate, b_gate, w_out,
           g_pre_ffn, g_post_ffn, w_router, b_router, w_gate_up, b_gate_up, w_down, b_down):
    b, s, d = x.shape
    assert d == D_MODEL and s % TM == 0 and (b * s) % TS == 0 and s % TS == 0
    n = b * s
    nt = n // TS
    ntp = -(-nt // LANES) * LANES
    p_rows = -(-(n * TOP_K + nt * N_EXPERTS * (ROW_ALIGN - 1) + N_EXPERTS * (BM - 1)) // BM) * BM
    depth = w_ada.shape[0]
    c8 = jnp.pad(c, ((0, 8 - b), (0, 0)))
    row = lambda a: a.reshape(1, -1)

    for l in range(depth):
        mod = _ada(c8, w_ada[l], row(b_ada[l]))[:b]
        mod3 = mod.reshape(b, 6, d)

        q, k, v, u, vv = _proj(x, mod3, row(g_pre_mix[l]), w_in[l].astype(bf16),
                               row(sgu_ln_g[l]), row(sgu_ln_b[l]))
        ya = _attn(q, k, v, _attn_bias(rel_bias[l]))
        ws2 = w_spatial[l].astype(bf16).reshape(-1, 2 * SGU_BLOCK, SGU_BLOCK)
        bsf = jnp.repeat(b_spatial[l].T, SGU_WIDTH // b_spatial.shape[1], axis=1)
        x1 = _mix(x, mod3, row(g_pre_mix[l]), row(g_post_mix[l]), u, vv, ya, ws2, bsf,
                  w_branch_a[l].astype(bf16), w_branch_b[l].astype(bf16),
                  w_gate[l].astype(bf16), row(b_gate[l]), w_out[l].astype(bf16))

        x1f = x1.reshape(n, d)
        h2, ti, gt, tab = _router(x1f, mod3, row(g_pre_ffn[l]), w_router[l].T.astype(bf16),
                                  b_router[l].reshape(-1, 1), s // TS, ntp)
        toff, dst, c8s, tstart, tlen, blk_e, nxt_e, n_used = _layout_tables(tab, nt, p_rows)
        xb = _dispatch(toff, dst, c8s, tstart, tlen, n_used, h2, ti, p_rows)
        yb = _experts(blk_e, nxt_e, n_used, xb, w_gate_up[l], b_gate_up[l][:, None, :],
                      w_down[l], b_down[l][:, None, :])
        x = _combine(toff, dst, c8s, yb, ti, gt, x1f, mod3, row(g_post_ffn[l]), s // TS).reshape(b, s, d)
    return x
```

```python
import functools

import jax
import jax.numpy as jnp
from jax import lax
from jax.experimental import pallas as pl
from jax.experimental.pallas import tpu as pltpu

bf16 = jnp.bfloat16
f32 = jnp.float32
i32 = jnp.int32

D_MODEL = 1024
CHUNK = 64
N_LEFT = 8
ATT_HEADS = 8
HEAD_DIM = 64
ATT_WIDTH = 512
MAX_REL = 128
SGU_BLOCK = 128
SGU_WIDTH = 512
N_EXPERTS = 32
TOP_K = 4
SWIGLU_LIMIT = 7.0
SWIGLU_ALPHA = 1.702
EPS = 1e-6
NEG = -1e30
LOG2E = 1.4426950408889634

LANES = 128
ROW_ALIGN = 8
TM = 512
QG = 2 * CHUNK
KBAND = (N_LEFT + 2) * CHUNK
TS = 256
RS = TS * TOP_K + N_EXPERTS * ROW_ALIGN
BM = 512
VMEM_LIMIT = 56 * 2**20


def _params(sem):
    return pltpu.CompilerParams(dimension_semantics=sem, vmem_limit_bytes=VMEM_LIMIT)


def _adaln(x, g, sc, sh):
    ms = jnp.mean(x * x, axis=-1, keepdims=True)
    return (x * lax.rsqrt(ms + EPS) * g) * (1.0 + sc) + sh


def _rms(x, g):
    ms = jnp.mean(x * x, axis=-1, keepdims=True)
    return x * lax.rsqrt(ms + EPS) * g


def _sigmoid(x):
    return 1.0 / (1.0 + jnp.exp(-x))


def _ada_body(c_ref, w_ref, b_ref, o_ref):
    c = c_ref[...]
    ca = c * _sigmoid(c)
    o_ref[...] = jnp.dot(ca.astype(bf16), w_ref[...].astype(bf16),
                         preferred_element_type=f32) + b_ref[...]


def _ada(c8, w, b):
    d = w.shape[0]
    n = w.shape[1] // d
    return pl.pallas_call(
        _ada_body,
        out_shape=jax.ShapeDtypeStruct((8, n * d), f32),
        grid=(n,),
        in_specs=[pl.BlockSpec((8, d), lambda j: (0, 0)),
                  pl.BlockSpec((d, d), lambda j: (0, j)),
                  pl.BlockSpec((1, d), lambda j: (0, j))],
        out_specs=pl.BlockSpec((8, d), lambda j: (0, j)),
        compiler_params=_params(("arbitrary",)),
        name="ada",
    )(c8, w, b)


def _proj_body(x_ref, mod_ref, g_ref, w_ref, lng_ref, lnb_ref,
               q_ref, k_ref, v_ref, u_ref, vv_ref):
    h = _adaln(x_ref[0], g_ref[...], mod_ref[0, 1:2, :], mod_ref[0, 0:1, :])
    p = jnp.dot(h.astype(bf16), w_ref[...], preferred_element_type=f32)
    aw = ATT_WIDTH
    q_ref[0] = (p[:, 0:aw] * (HEAD_DIM ** -0.5 * LOG2E)).astype(bf16)
    k_ref[0] = p[:, aw:2 * aw].astype(bf16)
    v_ref[0] = p[:, 2 * aw:3 * aw].astype(bf16)
    z = p[:, 3 * aw:]
    zg = 0.5 * z * (1.0 + lax.erf(z * (2.0 ** -0.5)))
    u_ref[0] = zg[:, :SGU_WIDTH].astype(bf16)
    vv = zg[:, SGU_WIDTH:]
    mu = jnp.mean(vv, axis=-1, keepdims=True)
    var = jnp.mean(jnp.square(vv - mu), axis=-1, keepdims=True)
    vn = (vv - mu) * lax.rsqrt(var + EPS) * lng_ref[...] + lnb_ref[...]
    vv_ref[0] = vn.astype(bf16)


def _proj(x, mod3, g, w_in, lng, lnb):
    b, s, d = x.shape
    tok = lambda w: pl.BlockSpec((1, TM, w), lambda bi, i: (bi, i, 0))
    full = lambda a: pl.BlockSpec(a.shape, lambda bi, i: (0,) * a.ndim)
    o512 = jax.ShapeDtypeStruct((b, s, ATT_WIDTH), bf16)
    return pl.pallas_call(
        _proj_body,
        out_shape=(o512,) * 5,
        grid=(b, s // TM),
        in_specs=[tok(d), pl.BlockSpec((1, 6, d), lambda bi, i: (bi, 0, 0)),
                  full(g), full(w_in), full(lng), full(lnb)],
        out_specs=(tok(ATT_WIDTH),) * 5,
        compiler_params=_params(("parallel", "arbitrary")),
        name="proj",
    )(x, mod3, g, w_in, lng, lnb)


def _attn_body(q_ref, kp_ref, kc_ref, vp_ref, vc_ref, bias_ref, o_ref, kbuf, vbuf):
    first = pl.program_id(1) == 0
    nhp = ATT_HEADS // 2
    ones = jnp.ones((TM, LANES), bf16)

    def fill(rows, k_src, v_src):
        kbuf[rows, :] = k_src[0]
        for hp in range(nhp):
            vbuf[rows, 2 * hp * LANES:(2 * hp + 1) * LANES] = v_src[0, :, hp * LANES:(hp + 1) * LANES]
            vbuf[rows, (2 * hp + 1) * LANES:(2 * hp + 2) * LANES] = ones

    fill(slice(TM, 2 * TM), kc_ref, vc_ref)

    @pl.when(first)
    def _():
        kbuf[0:TM, :] = jnp.zeros((TM, kbuf.shape[1]), bf16)
        vbuf[0:TM, :] = jnp.zeros((TM, vbuf.shape[1]), bf16)

    @pl.when(jnp.logical_not(first))
    def _():
        fill(slice(0, TM), kp_ref, vp_ref)

    lo = lax.broadcasted_iota(i32, (QG, LANES), 1) < HEAD_DIM

    def group(p, carry):
        r0 = pl.multiple_of(p * QG, QG)
        for hp in range(nhp):
            c0 = hp * LANES
            qp = q_ref[0, pl.ds(r0, QG), c0:c0 + LANES]
            zero = jnp.zeros_like(qp)
            q2 = jnp.concatenate([jnp.where(lo, qp, zero), jnp.where(lo, zero, qp)], axis=0)
            kb = kbuf[pl.ds(r0, KBAND), c0:c0 + LANES]
            s = lax.dot_general(q2, kb, (((1,), (1,)), ((), ())), preferred_element_type=f32)
            sb = (s + bias_ref[hp]).astype(bf16)
            m = jnp.max(sb, axis=-1, keepdims=True)
            e = jnp.exp2(sb - m)
            vb = vbuf[pl.ds(r0, KBAND), 2 * c0:2 * c0 + 2 * LANES]
            o2 = jnp.dot(e, vb, preferred_element_type=f32)
            on = o2[:, :LANES] / o2[:, LANES:]
            o = jnp.where(lo, on[:QG], on[QG:])
            o_ref[0, pl.ds(r0, QG), c0:c0 + LANES] = o.astype(bf16)
        return carry

    lax.fori_loop(0, TM // QG, group, 0, unroll=2)


def _attn(q, k, v, bias2):
    b, s, w = q.shape
    cur = pl.BlockSpec((1, TM, w), lambda bi, i: (bi, i, 0))
    prev = pl.BlockSpec((1, TM, w), lambda bi, i: (bi, jnp.maximum(i - 1, 0), 0))
    return pl.pallas_call(
        _attn_body,
        out_shape=jax.ShapeDtypeStruct((b, s, w), bf16),
        grid=(b, s // TM),
        in_specs=[cur, prev, cur, prev, cur,
                  pl.BlockSpec(bias2.shape, lambda bi, i: (0, 0, 0))],
        out_specs=cur,
        scratch_shapes=[pltpu.VMEM((2 * TM, w), bf16), pltpu.VMEM((2 * TM, 2 * w), bf16)],
        compiler_params=_params(("parallel", "arbitrary")),
        name="attn",
    )(q, k, k, v, v, bias2)


def _attn_bias(rel_bias):
    h = rel_bias.shape[0]
    period = 1024
    assert period >= QG + KBAND - 1 and KBAND - 2 * MAX_REL >= 0
    far = jnp.broadcast_to(rel_bias[:, 2 * MAX_REL:], (h, period))
    near = rel_bias[:, :0:-1]
    v = jnp.concatenate([far[:, :KBAND - 2 * MAX_REL], near, far[:, KBAND:]], axis=1).astype(f32)
    flat = jnp.broadcast_to(v[:, None, :], (h, QG, period)).reshape(h, QG * period)
    toep = flat[:, :QG * (period - 1)].reshape(h, QG, period - 1)[:, :, :KBAND]
    i = jnp.arange(QG, dtype=i32)[:, None]
    j = jnp.arange(KBAND, dtype=i32)[None, :]
    jb = j - (i // CHUNK) * CHUNK
    valid = jnp.logical_and(jb >= 0, jb < CHUNK * (N_LEFT + 1))
    bias = jnp.where(valid[None], toep * LOG2E, NEG)
    return bias.reshape(ATT_HEADS // 2, 2 * QG, KBAND)


def _mix_body(x_ref, mod_ref, gpre_ref, gpost_ref, u_ref, vv_ref, ya_ref, ws_ref, bs_ref,
              wa_ref, wb_ref, wg_ref, bg_ref, wo_ref, o_ref, ybuf):
    d = D_MODEL
    x = x_ref[0]
    h = _adaln(x, gpre_ref[...], mod_ref[0, 1:2, :], mod_ref[0, 0:1, :]).astype(bf16)
    gs = _sigmoid(jnp.dot(h, wg_ref[...], preferred_element_type=f32) + bg_ref[...])

    blk = SGU_BLOCK
    row = lax.broadcasted_iota(i32, (2 * blk, blk), 0)
    colv = lax.broadcasted_iota(i32, (2 * blk, blk), 1)
    causal = colv <= jnp.bitwise_and(row, blk - 1)
    lo = lax.broadcasted_iota(i32, (blk, LANES), 1) < (LANES // 2)
    for gp in range(SGU_WIDTH // LANES):
        c0 = gp * LANES
        w2 = ws_ref[gp]
        w2 = jnp.where(causal, w2, jnp.zeros_like(w2))
        for bi in range(TM // blk):
            r0 = bi * blk
            s2 = jnp.dot(w2, vv_ref[0, r0:r0 + blk, c0:c0 + LANES], preferred_element_type=f32)
            s = jnp.where(lo, s2[:blk], s2[blk:]) + bs_ref[:, c0:c0 + LANES]
            yb = u_ref[0, r0:r0 + blk, c0:c0 + LANES].astype(f32) * s
            ybuf[r0:r0 + blk, c0:c0 + LANES] = yb.astype(bf16)

    a = jnp.dot(ya_ref[0], wa_ref[...], preferred_element_type=f32)
    bb = jnp.dot(ybuf[...], wb_ref[...], preferred_element_type=f32)
    merged = gs[:, :d] * a + gs[:, d:] * bb
    y = jnp.dot(merged.astype(bf16), wo_ref[...], preferred_element_type=f32)
    o_ref[0] = x + mod_ref[0, 2:3, :] * _rms(y, gpost_ref[...])


def _mix(x, mod3, gpre, gpost, u, vv, ya, ws2, bsf, wa, wb, wg, bg, wo):
    b, s, d = x.shape
    tok = lambda w: pl.BlockSpec((1, TM, w), lambda bi, i: (bi, i, 0))
    full = lambda a: pl.BlockSpec(a.shape, lambda bi, i: (0,) * a.ndim)
    return pl.pallas_call(
        _mix_body,
        out_shape=jax.ShapeDtypeStruct((b, s, d), f32),
        grid=(b, s // TM),
        in_specs=[tok(d), pl.BlockSpec((1, 6, d), lambda bi, i: (bi, 0, 0)),
                  full(gpre), full(gpost), tok(SGU_WIDTH), tok(SGU_WIDTH), tok(ATT_WIDTH),
                  full(ws2), full(bsf), full(wa), full(wb), full(wg), full(bg), full(wo)],
        out_specs=tok(d),
        scratch_shapes=[pltpu.VMEM((TM, SGU_WIDTH), bf16)],
        compiler_params=_params(("parallel", "arbitrary")),
        name="mix",
    )(x, mod3, gpre, gpost, u, vv, ya, ws2, bsf, wa, wb, wg, bg, wo)


def _router_body(x_ref, mod_ref, g_ref, wr_ref, br_ref, h2_ref, ti_ref, gt_ref, tab_ref):
    i = pl.program_id(0)
    h = _adaln(x_ref[...], g_ref[...], mod_ref[0, 4:5, :], mod_ref[0, 3:4, :])
    hb = h.astype(bf16)
    h2_ref[...] = hb
    lg = lax.dot_general(wr_ref[...], hb, (((1,), (1,)), ((), ())),
                         preferred_element_type=f32) + br_ref[...]
    e_iota = lax.broadcasted_iota(i32, lg.shape, 0)
    vals, idxs = [], []
    cnt = jnp.zeros((N_EXPERTS, 1), f32)
    for _ in range(TOP_K):
        m = jnp.max(lg, axis=0, keepdims=True)
        idx = jnp.min(jnp.where(lg == m, e_iota, N_EXPERTS), axis=0, keepdims=True)
        hit = e_iota == idx
        cnt = cnt + jnp.sum(hit.astype(f32), axis=1, keepdims=True)
        lg = jnp.where(hit, -jnp.inf, lg)
        vals.append(m)
        idxs.append(idx)
    ex = [jnp.exp(v - vals[0]) for v in vals]
    den = ex[0] + ex[1] + ex[2] + ex[3]
    ti_ref[...] = jnp.concatenate(idxs, axis=0)
    gt_ref[...] = jnp.concatenate([e / den for e in ex], axis=0)
    c8 = jnp.floor((cnt + (ROW_ALIGN - 1.0)) * (1.0 / ROW_ALIGN)) * ROW_ALIGN

    @pl.when(i == 0)
    def _():
        tab_ref[...] = jnp.zeros_like(tab_ref)

    lane = lax.broadcasted_iota(i32, tab_ref.shape, 1)
    tab_ref[...] += jnp.where(lane == i, c8, 0.0)


def _router(x1, mod3, g, wrt, br, tiles_per_batch, ntp):
    n, d = x1.shape
    nt = n // TS
    tok = pl.BlockSpec((TS, d), lambda i: (i, 0))
    full = lambda a: pl.BlockSpec(a.shape, lambda i: (0,) * a.ndim)
    lane = pl.BlockSpec((TOP_K, TS), lambda i: (0, i))
    return pl.pallas_call(
        _router_body,
        out_shape=(jax.ShapeDtypeStruct((n, d), bf16),
                   jax.ShapeDtypeStruct((TOP_K, n), i32),
                   jax.ShapeDtypeStruct((TOP_K, n), f32),
                   jax.ShapeDtypeStruct((N_EXPERTS, ntp), f32)),
        grid=(nt,),
        in_specs=[tok, pl.BlockSpec((1, 6, d), lambda i: (i // tiles_per_batch, 0, 0)),
                  full(g), full(wrt), full(br)],
        out_specs=(tok, lane, lane, pl.BlockSpec((N_EXPERTS, ntp), lambda i: (0, 0))),
        compiler_params=_params(("arbitrary",)),
        name="router",
    )(x1, mod3, g, wrt, br)


def _sorted_positions(ti):
    ts = ti.shape[1]
    e_iota = lax.broadcasted_iota(i32, (N_EXPERTS, ts), 0)
    upper = (lax.broadcasted_iota(i32, (ts, ts), 0) < lax.broadcasted_iota(i32, (ts, ts), 1))
    upper = jnp.where(upper, 1.0, 0.0).astype(bf16)
    hits, prefs, cnts = [], [], []
    for k in range(TOP_K):
        hit = ti[k:k + 1, :] == e_iota
        hf = jnp.where(hit, 1.0, 0.0)
        prefs.append(jnp.dot(hf.astype(bf16), upper, preferred_element_type=f32))
        cnts.append(jnp.sum(hf, axis=1, keepdims=True))
        hits.append(hit)
    total = cnts[0] + cnts[1] + cnts[2] + cnts[3]
    c8 = jnp.floor((total + (ROW_ALIGN - 1.0)) * (1.0 / ROW_ALIGN)) * ROW_ALIGN
    lower = (lax.broadcasted_iota(i32, (N_EXPERTS, N_EXPERTS), 1)
             < lax.broadcasted_iota(i32, (N_EXPERTS, N_EXPERTS), 0))
    lower = jnp.where(lower, 1.0, 0.0).astype(bf16)
    c8b = jnp.broadcast_to(c8, (N_EXPERTS, LANES)).astype(bf16)
    start = jnp.dot(lower, c8b, preferred_element_type=f32)[:, 0:1]
    pos = []
    for k in range(TOP_K):
        pe = start + prefs[k]
        pos.append(jnp.sum(jnp.where(hits[k], pe, 0.0), axis=0, keepdims=True).astype(i32))
        start = start + cnts[k]
    return pos


def _start_runs(i, toff_ref, dst_ref, c8_ref, make):
    def one(e, carry):
        j = i * N_EXPERTS + e
        n = pl.multiple_of(c8_ref[j], ROW_ALIGN)
        so = pl.multiple_of(toff_ref[j], ROW_ALIGN)
        do = pl.multiple_of(dst_ref[j], ROW_ALIGN)

        @pl.when(n > 0)
        def _():
            make(so, do, n).start()
        return carry
    lax.fori_loop(0, N_EXPERTS, one, 0)


def _tile_rows(i, toff_ref, c8_ref):
    j = i * N_EXPERTS + (N_EXPERTS - 1)
    return pl.multiple_of(toff_ref[j] + c8_ref[j], ROW_ALIGN)


def _dispatch_body(toff_ref, dst_ref, c8_ref, tstart_ref, tlen_ref, nu_ref,
                   h2_ref, ti_ref, xb_ref, sbuf, zbuf, sems, sem):
    i = pl.program_id(0)
    slot = lax.rem(i, 2)
    p = _sorted_positions(ti_ref[...])
    r = lax.broadcasted_iota(i32, (RS, TS), 0)
    pm = jnp.where(r == p[0], 1.0, jnp.where(r == p[1], 1.0,
         jnp.where(r == p[2], 1.0, jnp.where(r == p[3], 1.0, 0.0))))
    sbuf[slot] = jnp.dot(pm.astype(bf16), h2_ref[...], preferred_element_type=f32)

    def make(so, do, n):
        return pltpu.make_async_copy(sbuf.at[slot, pl.ds(so, n)], xb_ref.at[pl.ds(do, n)],
                                     sems.at[slot])
    _start_runs(i, toff_ref, dst_ref, c8_ref, make)

    def wait_tile(t, sl):
        n = _tile_rows(t, toff_ref, c8_ref)
        pltpu.make_async_copy(sbuf.at[sl, pl.ds(0, n)], xb_ref.at[pl.ds(0, n)], sems.at[sl]).wait()

    @pl.when(i > 0)
    def _():
        wait_tile(i - 1, 1 - slot)

    @pl.when(i == pl.num_programs(0) - 1)
    def _():
        wait_tile(i, slot)
        zbuf[...] = jnp.zeros_like(zbuf)

        def fill(action):
            def tail(e, carry):
                n = pl.multiple_of(tlen_ref[e], ROW_ALIGN)
                do = pl.multiple_of(tstart_ref[e], ROW_ALIGN)

                @pl.when(n > 0)
                def _():
                    action(pltpu.make_async_copy(zbuf.at[pl.ds(0, n)], xb_ref.at[pl.ds(do, n)], sem))
                return carry
            lax.fori_loop(0, N_EXPERTS, tail, 0)

            def unused(b, carry):
                do = pl.multiple_of(b * BM, BM)
                action(pltpu.make_async_copy(zbuf, xb_ref.at[pl.ds(do, BM)], sem))
                return carry
            lax.fori_loop(nu_ref[0], xb_ref.shape[0] // BM, unused, 0)
        fill(lambda c: c.start())
        fill(lambda c: c.wait())


def _dispatch(toff, dst, c8s, tstart, tlen, n_used, h2, ti, p_rows):
    n, d = h2.shape
    return pl.pallas_call(
        _dispatch_body,
        out_shape=jax.ShapeDtypeStruct((p_rows, d), f32),
        grid_spec=pltpu.PrefetchScalarGridSpec(
            num_scalar_prefetch=6,
            grid=(n // TS,),
            in_specs=[pl.BlockSpec((TS, d), lambda i, *_: (i, 0)),
                      pl.BlockSpec((TOP_K, TS), lambda i, *_: (0, i))],
            out_specs=pl.BlockSpec(memory_space=pl.ANY),
            scratch_shapes=[pltpu.VMEM((2, RS, d), f32), pltpu.VMEM((BM, d), f32),
                            pltpu.SemaphoreType.DMA((2,)), pltpu.SemaphoreType.DMA(())]),
        compiler_params=_params(("arbitrary",)),
        name="dispatch",
    )(toff, dst, c8s, tstart, tlen, n_used, h2, ti)


def _expert_body(be_ref, nxt_ref, nu_ref, x_ref, wgu_hbm, bgu_ref, wd_hbm, bd_ref, o_ref,
                 wgu_st, wd_st, wgu_bf, wd_bf, sems):
    b = pl.program_id(0)
    d = D_MODEL
    e = be_ref[b]

    def weight_copies(ex):
        return (pltpu.make_async_copy(wgu_hbm.at[ex], wgu_st, sems.at[0]),
                pltpu.make_async_copy(wd_hbm.at[ex], wd_st, sems.at[1]))

    @pl.when(b >= nu_ref[0])
    def _():
        o_ref[...] = jnp.zeros_like(o_ref)

    @pl.when(b < nu_ref[0])
    def _():
        @pl.when(b == 0)
        def _():
            for c in weight_copies(e):
                c.start()

        @pl.when(jnp.logical_or(b == 0, e != be_ref[jnp.maximum(b - 1, 0)]))
        def _():
            for c in weight_copies(e):
                c.wait()
            wgu_bf[...] = wgu_st[...].astype(bf16)
            wd_bf[...] = wd_st[...].astype(bf16)
            nxt = nxt_ref[e]

            @pl.when(nxt >= 0)
            def _():
                for c in weight_copies(nxt):
                    c.start()

        gu = jnp.dot(x_ref[...].astype(bf16), wgu_bf[...], preferred_element_type=f32) + bgu_ref[0]
        glu = jnp.minimum(gu[:, :d], SWIGLU_LIMIT)
        lin = jnp.clip(gu[:, d:], -SWIGLU_LIMIT, SWIGLU_LIMIT)
        act = glu * _sigmoid(SWIGLU_ALPHA * glu) * (lin + 1.0)
        o_ref[...] = jnp.dot(act.astype(bf16), wd_bf[...], preferred_element_type=f32) + bd_ref[0]


def _experts(blk_e, nxt_e, n_used, xb, wgu, bgu, wd, bd):
    p_rows, d = xb.shape
    rows = pl.BlockSpec((BM, d), lambda b, be, nx, nu: (b, 0))
    per_e = lambda a: pl.BlockSpec((1,) + a.shape[1:], lambda b, be, nx, nu: (be[b], 0, 0))
    hbm = pl.BlockSpec(memory_space=pl.ANY)
    return pl.pallas_call(
        _expert_body,
        out_shape=jax.ShapeDtypeStruct((p_rows, d), f32),
        grid_spec=pltpu.PrefetchScalarGridSpec(
            num_scalar_prefetch=3,
            grid=(p_rows // BM,),
            in_specs=[rows, hbm, per_e(bgu), hbm, per_e(bd)],
            out_specs=rows,
            scratch_shapes=[pltpu.VMEM(wgu.shape[1:], f32), pltpu.VMEM(wd.shape[1:], f32),
                            pltpu.VMEM(wgu.shape[1:], bf16), pltpu.VMEM(wd.shape[1:], bf16),
                            pltpu.SemaphoreType.DMA((2,))]),
        compiler_params=_params(("arbitrary",)),
        name="experts",
    )(blk_e, nxt_e, n_used, xb, wgu, bgu, wd, bd)


def _combine_body(toff_ref, dst_ref, c8_ref, yb_ref, ti_ref, gt_ref, x1_ref, mod_ref, g_ref,
                  o_ref, sbuf, sems):
    i = pl.program_id(0)
    slot = lax.rem(i, 2)

    def fetch(t, sl):
        def make(so, do, n):
            return pltpu.make_async_copy(yb_ref.at[pl.ds(do, n)], sbuf.at[sl, pl.ds(so, n)],
                                         sems.at[sl])
        _start_runs(t, toff_ref, dst_ref, c8_ref, make)

    @pl.when(i == 0)
    def _():
        sbuf[...] = jnp.zeros_like(sbuf)
        fetch(i, slot)

    @pl.when(i + 1 < pl.num_programs(0))
    def _():
        fetch(i + 1, 1 - slot)

    p = _sorted_positions(ti_ref[...])
    gt = gt_ref[...]
    r = lax.broadcasted_iota(i32, (RS, TS), 0)
    m = [r == p[k] for k in range(TOP_K)]
    pm = jnp.where(m[0], 1.0, jnp.where(m[1], 1.0, jnp.where(m[2], 1.0, jnp.where(m[3], 1.0, 0.0))))
    gm = jnp.where(m[0], gt[0:1, :], jnp.where(m[1], gt[1:2, :],
         jnp.where(m[2], gt[2:3, :], jnp.where(m[3], gt[3:4, :], 0.0))))
    grow = jnp.sum(gm, axis=1, keepdims=True)

    n = _tile_rows(i, toff_ref, c8_ref)
    pltpu.make_async_copy(yb_ref.at[pl.ds(0, n)], sbuf.at[slot, pl.ds(0, n)], sems.at[slot]).wait()
    sg = (sbuf[slot] * grow).astype(bf16)
    y = lax.dot_general(pm.astype(bf16), sg, (((0,), (0,)), ((), ())), preferred_element_type=f32)
    o_ref[...] = x1_ref[...] + mod_ref[0, 5:6, :] * _rms(y, g_ref[...])


def _combine(toff, dst, c8s, yb, ti, gt, x1, mod3, g, tiles_per_batch):
    n, d = x1.shape
    tok = pl.BlockSpec((TS, d), lambda i, *_: (i, 0))
    lane = pl.BlockSpec((TOP_K, TS), lambda i, *_: (0, i))
    return pl.pallas_call(
        _combine_body,
        out_shape=jax.ShapeDtypeStruct((n, d), f32),
        grid_spec=pltpu.PrefetchScalarGridSpec(
            num_scalar_prefetch=3,
            grid=(n // TS,),
            in_specs=[pl.BlockSpec(memory_space=pl.ANY), lane, lane, tok,
                      pl.BlockSpec((1, 6, d), lambda i, *_: (i // tiles_per_batch, 0, 0)),
                      pl.BlockSpec(g.shape, lambda i, *_: (0, 0))],
            out_specs=tok,
            scratch_shapes=[pltpu.VMEM((2, RS, d), f32), pltpu.SemaphoreType.DMA((2,))]),
        compiler_params=_params(("arbitrary",)),
        name="combine",
    )(toff, dst, c8s, yb, ti, gt, x1, mod3, g)


def _layout_tables(tab, nt, p_rows):
    c8 = tab[:, :nt].T.astype(i32)
    toff = jnp.cumsum(c8, axis=1) - c8
    len8 = jnp.sum(c8, axis=0)
    seg = (len8 + BM - 1) // BM * BM
    gend = jnp.cumsum(seg)
    gstart = gend - seg
    dst = jnp.cumsum(c8, axis=0) - c8 + gstart[None, :]
    n_used = gend[-1] // BM
    blk = jnp.arange(p_rows // BM, dtype=i32)
    last = jnp.minimum(blk, n_used - 1)
    blk_e = jnp.sum((gend[None, :] <= (last * BM)[:, None]).astype(i32), axis=1)
    blk_e = jnp.minimum(blk_e, N_EXPERTS - 1)
    ids = jnp.arange(N_EXPERTS, dtype=i32)
    later = jnp.logical_and(ids[None, :] > ids[:, None], (seg > 0)[None, :])
    nxt_e = jnp.min(jnp.where(later, ids[None, :], N_EXPERTS), axis=1)
    nxt_e = jnp.where(nxt_e == N_EXPERTS, -1, nxt_e).astype(i32)
    return (toff.reshape(-1), dst.reshape(-1), c8.reshape(-1), gstart + len8, seg - len8,
            blk_e, nxt_e, n_used.reshape(1).astype(i32))


def kernel(x, c, w_ada, b_ada, g_pre_mix, g_post_mix, w_in, rel_bias, sgu_ln_g, sgu_ln_b,
           w_spatial, b_spatial, w_branch_a, w_branch_b, w_gate, b_gate, w_out,
           g_pre_ffn, g_post_ffn, w_router, b_router, w_gate_up, b_gate_up, w_down, b_down):
    b, s, d = x.shape
    assert d == D_MODEL and s % TM == 0 and (b * s) % TS == 0 and s % TS == 0
    n = b * s
    nt = n // TS
    ntp = -(-nt // LANES) * LANES
    p_rows = -(-(n * TOP_K + nt * N_EXPERTS * (ROW_ALIGN - 1) + N_EXPERTS * (BM - 1)) // BM) * BM
    depth = w_ada.shape[0]
    c8 = jnp.pad(c, ((0, 8 - b), (0, 0)))
    row = lambda a: a.reshape(1, -1)

    for l in range(depth):
        mod = _ada(c8, w_ada[l], row(b_ada[l]))[:b]
        mod3 = mod.reshape(b, 6, d)

        q, k, v, u, vv = _proj(x, mod3, row(g_pre_mix[l]), w_in[l].astype(bf16),
                               row(sgu_ln_g[l]), row(sgu_ln_b[l]))
        ya = _attn(q, k, v, _attn_bias(rel_bias[l]))
        ws2 = w_spatial[l].astype(bf16).reshape(-1, 2 * SGU_BLOCK, SGU_BLOCK)
        bsf = jnp.repeat(b_spatial[l].T, SGU_WIDTH // b_spatial.shape[1], axis=1)
        x1 = _mix(x, mod3, row(g_pre_mix[l]), row(g_post_mix[l]), u, vv, ya, ws2, bsf,
                  w_branch_a[l].astype(bf16), w_branch_b[l].astype(bf16),
                  w_gate[l].astype(bf16), row(b_gate[l]), w_out[l].astype(bf16))

        x1f = x1.reshape(n, d)
        h2, ti, gt, tab = _router(x1f, mod3, row(g_pre_ffn[l]), w_router[l].T.astype(bf16),
                                  b_router[l].reshape(-1, 1), s // TS, ntp)
        toff, dst, c8s, tstart, tlen, blk_e, nxt_e, n_used = _layout_tables(tab, nt, p_rows)
        xb = _dispatch(toff, dst, c8s, tstart, tlen, n_used, h2, ti, p_rows)
        yb = _experts(blk_e, nxt_e, n_used, xb, w_gate_up[l], b_gate_up[l][:, None, :],
                      w_down[l], b_down[l][:, None, :])
        x = _combine(toff, dst, c8s, yb, ti, gt, x1f, mod3, row(g_post_ffn[l]), s // TS).reshape(b, s, d)
    return x
```

```python
import functools

import jax
import jax.numpy as jnp
from jax import lax
from jax.experimental import pallas as pl
from jax.experimental.pallas import tpu as pltpu

bf16 = jnp.bfloat16
f32 = jnp.float32
i32 = jnp.int32

D_MODEL = 1024
CHUNK = 64
N_LEFT = 8
ATT_HEADS = 8
HEAD_DIM = 64
ATT_WIDTH = 512
MAX_REL = 128
SGU_BLOCK = 128
SGU_WIDTH = 512
N_EXPERTS = 32
TOP_K = 4
SWIGLU_LIMIT = 7.0
SWIGLU_ALPHA = 1.702
EPS = 1e-6
NEG = -1e30
LOG2E = 1.4426950408889634

LANES = 128
ROW_ALIGN = 8
TM = 512
QG = 2 * CHUNK
KBAND = (N_LEFT + 2) * CHUNK
TS = 256
RS = TS * TOP_K + N_EXPERTS * ROW_ALIGN
BM = 512
VMEM_LIMIT = 56 * 2**20


def _params(sem):
    return pltpu.CompilerParams(dimension_semantics=sem, vmem_limit_bytes=VMEM_LIMIT)


def _adaln(x, g, sc, sh):
    ms = jnp.mean(x * x, axis=-1, keepdims=True)
    return (x * lax.rsqrt(ms + EPS) * g) * (1.0 + sc) + sh


def _rms(x, g):
    ms = jnp.mean(x * x, axis=-1, keepdims=True)
    return x * lax.rsqrt(ms + EPS) * g


def _sigmoid(x):
    return 1.0 / (1.0 + jnp.exp(-x))


def _ada_body(c_ref, w_ref, b_ref, o_ref):
    c = c_ref[...]
    ca = c * _sigmoid(c)
    o_ref[...] = jnp.dot(ca.astype(bf16), w_ref[...].astype(bf16),
                         preferred_element_type=f32) + b_ref[...]


def _ada(c8, w, b):
    d = w.shape[0]
    n = w.shape[1] // d
    return pl.pallas_call(
        _ada_body,
        out_shape=jax.ShapeDtypeStruct((8, n * d), f32),
        grid=(n,),
        in_specs=[pl.BlockSpec((8, d), lambda j: (0, 0)),
                  pl.BlockSpec((d, d), lambda j: (0, j)),
                  pl.BlockSpec((1, d), lambda j: (0, j))],
        out_specs=pl.BlockSpec((8, d), lambda j: (0, j)),
        compiler_params=_params(("arbitrary",)),
        name="ada",
    )(c8, w, b)


def _proj_body(x_ref, mod_ref, g_ref, w_ref, lng_ref, lnb_ref,
               q_ref, k_ref, v_ref, u_ref, vv_ref):
    h = _adaln(x_ref[0], g_ref[...], mod_ref[0, 1:2, :], mod_ref[0, 0:1, :])
    p = jnp.dot(h.astype(bf16), w_ref[...], preferred_element_type=f32)
    aw = ATT_WIDTH
    q_ref[0] = (p[:, 0:aw] * (HEAD_DIM ** -0.5 * LOG2E)).astype(bf16)
    k_ref[0] = p[:, aw:2 * aw].astype(bf16)
    v_ref[0] = p[:, 2 * aw:3 * aw].astype(bf16)
    z = p[:, 3 * aw:]
    zg = 0.5 * z * (1.0 + lax.erf(z * (2.0 ** -0.5)))
    u_ref[0] = zg[:, :SGU_WIDTH].astype(bf16)
    vv = zg[:, SGU_WIDTH:]
    mu = jnp.mean(vv, axis=-1, keepdims=True)
    var = jnp.mean(jnp.square(vv - mu), axis=-1, keepdims=True)
    vn = (vv - mu) * lax.rsqrt(var + EPS) * lng_ref[...] + lnb_ref[...]
    vv_ref[0] = vn.astype(bf16)


def _proj(x, mod3, g, w_in, lng, lnb):
    b, s, d = x.shape
    tok = lambda w: pl.BlockSpec((1, TM, w), lambda bi, i: (bi, i, 0))
    full = lambda a: pl.BlockSpec(a.shape, lambda bi, i: (0,) * a.ndim)
    o512 = jax.ShapeDtypeStruct((b, s, ATT_WIDTH), bf16)
    return pl.pallas_call(
        _proj_body,
        out_shape=(o512,) * 5,
        grid=(b, s // TM),
        in_specs=[tok(d), pl.BlockSpec((1, 6, d), lambda bi, i: (bi, 0, 0)),
                  full(g), full(w_in), full(lng), full(lnb)],
        out_specs=(tok(ATT_WIDTH),) * 5,
        compiler_params=_params(("parallel", "arbitrary")),
        name="proj",
    )(x, mod3, g, w_in, lng, lnb)


def _attn_body(q_ref, kp_ref, kc_ref, vp_ref, vc_ref, bias_ref, o_ref, kbuf, vbuf):
    first = pl.program_id(1) == 0
    nhp = ATT_HEADS // 2
    ones = jnp.ones((TM, LANES), bf16)

    def fill(rows, k_src, v_src):
        kbuf[rows, :] = k_src[0]
        for hp in range(nhp):
            vbuf[rows, 2 * hp * LANES:(2 * hp + 1) * LANES] = v_src[0, :, hp * LANES:(hp + 1) * LANES]
            vbuf[rows, (2 * hp + 1) * LANES:(2 * hp + 2) * LANES] = ones

    fill(slice(TM, 2 * TM), kc_ref, vc_ref)

    @pl.when(first)
    def _():
        kbuf[0:TM, :] = jnp.zeros((TM, kbuf.shape[1]), bf16)
        vbuf[0:TM, :] = jnp.zeros((TM, vbuf.shape[1]), bf16)

    @pl.when(jnp.logical_not(first))
    def _():
        fill(slice(0, TM), kp_ref, vp_ref)

    lo = lax.broadcasted_iota(i32, (QG, LANES), 1) < HEAD_DIM

    def group(p, carry):
        r0 = pl.multiple_of(p * QG, QG)
        for hp in range(nhp):
            c0 = hp * LANES
            qp = q_ref[0, pl.ds(r0, QG), c0:c0 + LANES]
            zero = jnp.zeros_like(qp)
            q2 = jnp.concatenate([jnp.where(lo, qp, zero), jnp.where(lo, zero, qp)], axis=0)
            kb = kbuf[pl.ds(r0, KBAND), c0:c0 + LANES]
            s = lax.dot_general(q2, kb, (((1,), (1,)), ((), ())), preferred_element_type=f32)
            sb = (s + bias_ref[hp]).astype(bf16)
            m = jnp.max(sb, axis=-1, keepdims=True)
            e = jnp.exp2(sb - m)
            vb = vbuf[pl.ds(r0, KBAND), 2 * c0:2 * c0 + 2 * LANES]
            o2 = jnp.dot(e, vb, preferred_element_type=f32)
            on = o2[:, :LANES] / o2[:, LANES:]
            o = jnp.where(lo, on[:QG], on[QG:])
            o_ref[0, pl.ds(r0, QG), c0:c0 + LANES] = o.astype(bf16)
        return carry

    lax.fori_loop(0, TM // QG, group, 0, unroll=2)


def _attn(q, k, v, bias2):
    b, s, w = q.shape
    cur = pl.BlockSpec((1, TM, w), lambda bi, i: (bi, i, 0))
    prev = pl.BlockSpec((1, TM, w), lambda bi, i: (bi, jnp.maximum(i - 1, 0), 0))
    return pl.pallas_call(
        _attn_body,
        out_shape=jax.ShapeDtypeStruct((b, s, w), bf16),
        grid=(b, s // TM),
        in_specs=[cur, prev, cur, prev, cur,
                  pl.BlockSpec(bias2.shape, lambda bi, i: (0, 0, 0))],
        out_specs=cur,
        scratch_shapes=[pltpu.VMEM((2 * TM, w), bf16), pltpu.VMEM((2 * TM, 2 * w), bf16)],
        compiler_params=_params(("parallel", "arbitrary")),
        name="attn",
    )(q, k, k, v, v, bias2)


def _attn_bias(rel_bias):
    h = rel_bias.shape[0]
    period = 1024
    assert period >= QG + KBAND - 1 and KBAND - 2 * MAX_REL >= 0
    far = jnp.broadcast_to(rel_bias[:, 2 * MAX_REL:], (h, period))
    near = rel_bias[:, :0:-1]
    v = jnp.concatenate([far[:, :KBAND - 2 * MAX_REL], near, far[:, KBAND:]], axis=1).astype(f32)
    flat = jnp.broadcast_to(v[:, None, :], (h, QG, period)).reshape(h, QG * period)
    toep = flat[:, :QG * (period - 1)].reshape(h, QG, period - 1)[:, :, :KBAND]
    i = jnp.arange(QG, dtype=i32)[:, None]
    j = jnp.arange(KBAND, dtype=i32)[None, :]
    jb = j - (i // CHUNK) * CHUNK
    valid = jnp.logical_and(jb >= 0, jb < CHUNK * (N_LEFT + 1))
    bias = jnp.where(valid[None], toep * LOG2E, NEG)
    return bias.reshape(ATT_HEADS // 2, 2 * QG, KBAND)


def _route(hb, wr_ref, br_ref):
    lg = lax.dot_general(wr_ref[...], hb, (((1,), (1,)), ((), ())),
                         preferred_element_type=f32) + br_ref[...]
    e_iota = lax.broadcasted_iota(i32, lg.shape, 0)
    vals, idxs = [], []
    hits = jnp.zeros(lg.shape, f32)
    for _ in range(TOP_K):
        m = jnp.max(lg, axis=0, keepdims=True)
        idx = jnp.min(jnp.where(lg == m, e_iota, N_EXPERTS), axis=0, keepdims=True)
        hit = e_iota == idx
        hits = hits + jnp.where(hit, 1.0, 0.0)
        lg = jnp.where(hit, -jnp.inf, lg)
        vals.append(m)
        idxs.append(idx)
    ex = [jnp.exp(v - vals[0]) for v in vals]
    den = ex[0] + ex[1] + ex[2] + ex[3]
    return (jnp.concatenate(idxs, axis=0), jnp.concatenate([e / den for e in ex], axis=0), hits)


def _mix_body(x_ref, mod_ref, gpre_ref, gpost_ref, u_ref, vv_ref, ya_ref, ws_ref, bs_ref,
              wa_ref, wb_ref, wg_ref, bg_ref, wo_ref, gffn_ref, wr_ref, br_ref,
              o_ref, h2_ref, ti_ref, gt_ref, tab_ref, ybuf):
    d = D_MODEL
    x = x_ref[0]
    h = _adaln(x, gpre_ref[...], mod_ref[0, 1:2, :], mod_ref[0, 0:1, :]).astype(bf16)
    gs = _sigmoid(jnp.dot(h, wg_ref[...], preferred_element_type=f32) + bg_ref[...])

    blk = SGU_BLOCK
    row = lax.broadcasted_iota(i32, (2 * blk, blk), 0)
    colv = lax.broadcasted_iota(i32, (2 * blk, blk), 1)
    causal = colv <= jnp.bitwise_and(row, blk - 1)
    lo = lax.broadcasted_iota(i32, (blk, LANES), 1) < (LANES // 2)
    for gp in range(SGU_WIDTH // LANES):
        c0 = gp * LANES
        w2 = ws_ref[gp]
        w2 = jnp.where(causal, w2, jnp.zeros_like(w2))
        for bi in range(TM // blk):
            r0 = bi * blk
            s2 = jnp.dot(w2, vv_ref[0, r0:r0 + blk, c0:c0 + LANES], preferred_element_type=f32)
            s = jnp.where(lo, s2[:blk], s2[blk:]) + bs_ref[:, c0:c0 + LANES]
            yb = u_ref[0, r0:r0 + blk, c0:c0 + LANES].astype(f32) * s
            ybuf[r0:r0 + blk, c0:c0 + LANES] = yb.astype(bf16)

    a = jnp.dot(ya_ref[0], wa_ref[...], preferred_element_type=f32)
    bb = jnp.dot(ybuf[...], wb_ref[...], preferred_element_type=f32)
    merged = gs[:, :d] * a + gs[:, d:] * bb
    y = jnp.dot(merged.astype(bf16), wo_ref[...], preferred_element_type=f32)
    x1 = x + mod_ref[0, 2:3, :] * _rms(y, gpost_ref[...])
    o_ref[0] = x1

    hb = _adaln(x1, gffn_ref[...], mod_ref[0, 4:5, :], mod_ref[0, 3:4, :]).astype(bf16)
    h2_ref[0] = hb
    ids, gates, hits = _route(hb, wr_ref, br_ref)
    ti_ref[...] = ids
    gt_ref[...] = gates

    step = pl.program_id(0) * pl.num_programs(1) + pl.program_id(1)

    @pl.when(step == 0)
    def _():
        tab_ref[...] = jnp.zeros_like(tab_ref)

    lane = lax.broadcasted_iota(i32, tab_ref.shape, 1)
    acc = tab_ref[...]
    for j in range(TM // TS):
        cnt = jnp.sum(hits[:, j * TS:(j + 1) * TS], axis=1, keepdims=True)
        c8 = jnp.floor((cnt + (ROW_ALIGN - 1.0)) * (1.0 / ROW_ALIGN)) * ROW_ALIGN
        acc = acc + jnp.where(lane == step * (TM // TS) + j, c8, 0.0)
    tab_ref[...] = acc


def _mix(x, mod3, gpre, gpost, u, vv, ya, ws2, bsf, wa, wb, wg, bg, wo, gffn, wrt, br, ntp):
    b, s, d = x.shape
    n = b * s
    tok = lambda w: pl.BlockSpec((1, TM, w), lambda bi, i: (bi, i, 0))
    full = lambda a: pl.BlockSpec(a.shape, lambda bi, i: (0,) * a.ndim)
    lane = pl.BlockSpec((TOP_K, TM), lambda bi, i: (0, bi * (s // TM) + i))
    return pl.pallas_call(
        _mix_body,
        out_shape=(jax.ShapeDtypeStruct((b, s, d), f32),
                   jax.ShapeDtypeStruct((b, s, d), bf16),
                   jax.ShapeDtypeStruct((TOP_K, n), i32),
                   jax.ShapeDtypeStruct((TOP_K, n), f32),
                   jax.ShapeDtypeStruct((N_EXPERTS, ntp), f32)),
        grid=(b, s // TM),
        in_specs=[tok(d), pl.BlockSpec((1, 6, d), lambda bi, i: (bi, 0, 0)),
                  full(gpre), full(gpost), tok(SGU_WIDTH), tok(SGU_WIDTH), tok(ATT_WIDTH),
                  full(ws2), full(bsf), full(wa), full(wb), full(wg), full(bg), full(wo),
                  full(gffn), full(wrt), full(br)],
        out_specs=(tok(d), tok(d), lane, lane,
                   pl.BlockSpec((N_EXPERTS, ntp), lambda bi, i: (0, 0))),
        scratch_shapes=[pltpu.VMEM((TM, SGU_WIDTH), bf16)],
        compiler_params=_params(("arbitrary", "arbitrary")),
        name="mix",
    )(x, mod3, gpre, gpost, u, vv, ya, ws2, bsf, wa, wb, wg, bg, wo, gffn, wrt, br)


def _sorted_positions(ti):
    ts = ti.shape[1]
    e_iota = lax.broadcasted_iota(i32, (N_EXPERTS, ts), 0)
    upper = (lax.broadcasted_iota(i32, (ts, ts), 0) < lax.broadcasted_iota(i32, (ts, ts), 1))
    upper = jnp.where(upper, 1.0, 0.0).astype(bf16)
    hits, prefs, cnts = [], [], []
    for k in range(TOP_K):
        hit = ti[k:k + 1, :] == e_iota
        hf = jnp.where(hit, 1.0, 0.0)
        prefs.append(jnp.dot(hf.astype(bf16), upper, preferred_element_type=f32))
        cnts.append(jnp.sum(hf, axis=1, keepdims=True))
        hits.append(hit)
    total = cnts[0] + cnts[1] + cnts[2] + cnts[3]
    c8 = jnp.floor((total + (ROW_ALIGN - 1.0)) * (1.0 / ROW_ALIGN)) * ROW_ALIGN
    lower = (lax.broadcasted_iota(i32, (N_EXPERTS, N_EXPERTS), 1)
             < lax.broadcasted_iota(i32, (N_EXPERTS, N_EXPERTS), 0))
    lower = jnp.where(lower, 1.0, 0.0).astype(bf16)
    c8b = jnp.broadcast_to(c8, (N_EXPERTS, LANES)).astype(bf16)
    start = jnp.dot(lower, c8b, preferred_element_type=f32)[:, 0:1]
    pos = []
    for k in range(TOP_K):
        pe = start + prefs[k]
        pos.append(jnp.sum(jnp.where(hits[k], pe, 0.0), axis=0, keepdims=True).astype(i32))
        start = start + cnts[k]
    return pos


def _start_runs(i, toff_ref, dst_ref, c8_ref, make):
    def one(e, carry):
        j = i * N_EXPERTS + e
        n = pl.multiple_of(c8_ref[j], ROW_ALIGN)
        so = pl.multiple_of(toff_ref[j], ROW_ALIGN)
        do = pl.multiple_of(dst_ref[j], ROW_ALIGN)

        @pl.when(n > 0)
        def _():
            make(so, do, n).start()
        return carry
    lax.fori_loop(0, N_EXPERTS, one, 0)


def _tile_rows(i, toff_ref, c8_ref):
    j = i * N_EXPERTS + (N_EXPERTS - 1)
    return pl.multiple_of(toff_ref[j] + c8_ref[j], ROW_ALIGN)


def _dispatch_body(toff_ref, dst_ref, c8_ref, tstart_ref, tlen_ref, nu_ref,
                   h2_ref, ti_ref, xb_ref, sbuf, zbuf, sems, sem):
    i = pl.program_id(0)
    slot = lax.rem(i, 2)
    p = _sorted_positions(ti_ref[...])
    r = lax.broadcasted_iota(i32, (RS, TS), 0)
    pm = jnp.where(r == p[0], 1.0, jnp.where(r == p[1], 1.0,
         jnp.where(r == p[2], 1.0, jnp.where(r == p[3], 1.0, 0.0))))
    sbuf[slot] = jnp.dot(pm.astype(bf16), h2_ref[...], preferred_element_type=f32)

    def make(so, do, n):
        return pltpu.make_async_copy(sbuf.at[slot, pl.ds(so, n)], xb_ref.at[pl.ds(do, n)],
                                     sems.at[slot])
    _start_runs(i, toff_ref, dst_ref, c8_ref, make)

    def wait_tile(t, sl):
        n = _tile_rows(t, toff_ref, c8_ref)
        pltpu.make_async_copy(sbuf.at[sl, pl.ds(0, n)], xb_ref.at[pl.ds(0, n)], sems.at[sl]).wait()

    @pl.when(i > 0)
    def _():
        wait_tile(i - 1, 1 - slot)

    @pl.when(i == pl.num_programs(0) - 1)
    def _():
        wait_tile(i, slot)
        zbuf[...] = jnp.zeros_like(zbuf)

        def fill(action):
            def tail(e, carry):
                n = pl.multiple_of(tlen_ref[e], ROW_ALIGN)
                do = pl.multiple_of(tstart_ref[e], ROW_ALIGN)

                @pl.when(n > 0)
                def _():
                    action(pltpu.make_async_copy(zbuf.at[pl.ds(0, n)], xb_ref.at[pl.ds(do, n)], sem))
                return carry
            lax.fori_loop(0, N_EXPERTS, tail, 0)

            def unused(b, carry):
                do = pl.multiple_of(b * BM, BM)
                action(pltpu.make_async_copy(zbuf, xb_ref.at[pl.ds(do, BM)], sem))
                return carry
            lax.fori_loop(nu_ref[0], xb_ref.shape[0] // BM, unused, 0)
        fill(lambda c: c.start())
        fill(lambda c: c.wait())


def _dispatch(toff, dst, c8s, tstart, tlen, n_used, h2, ti, p_rows):
    n, d = h2.shape
    return pl.pallas_call(
        _dispatch_body,
        out_shape=jax.ShapeDtypeStruct((p_rows, d), f32),
        grid_spec=pltpu.PrefetchScalarGridSpec(
            num_scalar_prefetch=6,
            grid=(n // TS,),
            in_specs=[pl.BlockSpec((TS, d), lambda i, *_: (i, 0)),
                      pl.BlockSpec((TOP_K, TS), lambda i, *_: (0, i))],
            out_specs=pl.BlockSpec(memory_space=pl.ANY),
            scratch_shapes=[pltpu.VMEM((2, RS, d), f32), pltpu.VMEM((BM, d), f32),
                            pltpu.SemaphoreType.DMA((2,)), pltpu.SemaphoreType.DMA(())]),
        compiler_params=_params(("arbitrary",)),
        name="dispatch",
    )(toff, dst, c8s, tstart, tlen, n_used, h2, ti)


def _expert_body(be_ref, nxt_ref, nu_ref, x_ref, wgu_hbm, bgu_ref, wd_hbm, bd_ref, o_ref,
                 wgu_st, wd_st, wgu_bf, wd_bf, sems):
    b = pl.program_id(0)
    d = D_MODEL
    e = be_ref[b]

    def weight_copies(ex):
        return (pltpu.make_async_copy(wgu_hbm.at[ex], wgu_st, sems.at[0]),
                pltpu.make_async_copy(wd_hbm.at[ex], wd_st, sems.at[1]))

    @pl.when(b >= nu_ref[0])
    def _():
        o_ref[...] = jnp.zeros_like(o_ref)

    @pl.when(b < nu_ref[0])
    def _():
        @pl.when(b == 0)
        def _():
            for c in weight_copies(e):
                c.start()

        @pl.when(jnp.logical_or(b == 0, e != be_ref[jnp.maximum(b - 1, 0)]))
        def _():
            for c in weight_copies(e):
                c.wait()
            wgu_bf[...] = wgu_st[...].astype(bf16)
            wd_bf[...] = wd_st[...].astype(bf16)
            nxt = nxt_ref[e]

            @pl.when(nxt >= 0)
            def _():
                for c in weight_copies(nxt):
                    c.start()

        gu = jnp.dot(x_ref[...].astype(bf16), wgu_bf[...], preferred_element_type=f32) + bgu_ref[0]
        glu = jnp.minimum(gu[:, :d], SWIGLU_LIMIT)
        lin = jnp.clip(gu[:, d:], -SWIGLU_LIMIT, SWIGLU_LIMIT)
        act = glu * _sigmoid(SWIGLU_ALPHA * glu) * (lin + 1.0)
        o_ref[...] = jnp.dot(act.astype(bf16), wd_bf[...], preferred_element_type=f32) + bd_ref[0]


def _experts(blk_e, nxt_e, n_used, xb, wgu, bgu, wd, bd):
    p_rows, d = xb.shape
    rows = pl.BlockSpec((BM, d), lambda b, be, nx, nu: (b, 0))
    per_e = lambda a: pl.BlockSpec((1,) + a.shape[1:], lambda b, be, nx, nu: (be[b], 0, 0))
    hbm = pl.BlockSpec(memory_space=pl.ANY)
    return pl.pallas_call(
        _expert_body,
        out_shape=jax.ShapeDtypeStruct((p_rows, d), f32),
        grid_spec=pltpu.PrefetchScalarGridSpec(
            num_scalar_prefetch=3,
            grid=(p_rows // BM,),
            in_specs=[rows, hbm, per_e(bgu), hbm, per_e(bd)],
            out_specs=rows,
            scratch_shapes=[pltpu.VMEM(wgu.shape[1:], f32), pltpu.VMEM(wd.shape[1:], f32),
                            pltpu.VMEM(wgu.shape[1:], bf16), pltpu.VMEM(wd.shape[1:], bf16),
                            pltpu.SemaphoreType.DMA((2,))]),
        compiler_params=_params(("arbitrary",)),
        name="experts",
    )(blk_e, nxt_e, n_used, xb, wgu, bgu, wd, bd)


def _combine_body(toff_ref, dst_ref, c8_ref, yb_ref, ti_ref, gt_ref, x1_ref, mod_ref, g_ref,
                  o_ref, sbuf, sems):
    i = pl.program_id(0)
    slot = lax.rem(i, 2)

    def fetch(t, sl):
        def make(so, do, n):
            return pltpu.make_async_copy(yb_ref.at[pl.ds(do, n)], sbuf.at[sl, pl.ds(so, n)],
                                         sems.at[sl])
        _start_runs(t, toff_ref, dst_ref, c8_ref, make)

    @pl.when(i == 0)
    def _():
        sbuf[...] = jnp.zeros_like(sbuf)
        fetch(i, slot)

    @pl.when(i + 1 < pl.num_programs(0))
    def _():
        fetch(i + 1, 1 - slot)

    p = _sorted_positions(ti_ref[...])
    gt = gt_ref[...]
    r = lax.broadcasted_iota(i32, (RS, TS), 0)
    m = [r == p[k] for k in range(TOP_K)]
    pm = jnp.where(m[0], 1.0, jnp.where(m[1], 1.0, jnp.where(m[2], 1.0, jnp.where(m[3], 1.0, 0.0))))
    gm = jnp.where(m[0], gt[0:1, :], jnp.where(m[1], gt[1:2, :],
         jnp.where(m[2], gt[2:3, :], jnp.where(m[3], gt[3:4, :], 0.0))))
    grow = jnp.sum(gm, axis=1, keepdims=True)

    n = _tile_rows(i, toff_ref, c8_ref)
    pltpu.make_async_copy(yb_ref.at[pl.ds(0, n)], sbuf.at[slot, pl.ds(0, n)], sems.at[slot]).wait()
    sg = (sbuf[slot] * grow).astype(bf16)
    y = lax.dot_general(pm.astype(bf16), sg, (((0,), (0,)), ((), ())), preferred_element_type=f32)
    o_ref[...] = x1_ref[...] + mod_ref[0, 5:6, :] * _rms(y, g_ref[...])


def _combine(toff, dst, c8s, yb, ti, gt, x1, mod3, g, tiles_per_batch):
    n, d = x1.shape
    tok = pl.BlockSpec((TS, d), lambda i, *_: (i, 0))
    lane = pl.BlockSpec((TOP_K, TS), lambda i, *_: (0, i))
    return pl.pallas_call(
        _combine_body,
        out_shape=jax.ShapeDtypeStruct((n, d), f32),
        grid_spec=pltpu.PrefetchScalarGridSpec(
            num_scalar_prefetch=3,
            grid=(n // TS,),
            in_specs=[pl.BlockSpec(memory_space=pl.ANY), lane, lane, tok,
                      pl.BlockSpec((1, 6, d), lambda i, *_: (i // tiles_per_batch, 0, 0)),
                      pl.BlockSpec(g.shape, lambda i, *_: (0, 0))],
            out_specs=tok,
            scratch_shapes=[pltpu.VMEM((2, RS, d), f32), pltpu.SemaphoreType.DMA((2,))]),
        compiler_params=_params(("arbitrary",)),
        name="combine",
    )(toff, dst, c8s, yb, ti, gt, x1, mod3, g)


def _layout_tables(tab, nt, p_rows):
    c8 = tab[:, :nt].T.astype(i32)
    toff = jnp.cumsum(c8, axis=1) - c8
    len8 = jnp.sum(c8, axis=0)
    seg = (len8 + BM - 1) // BM * BM
    gend = jnp.cumsum(seg)
    gstart = gend - seg
    dst = jnp.cumsum(c8, axis=0) - c8 + gstart[None, :]
    n_used = gend[-1] // BM
    blk = jnp.arange(p_rows // BM, dtype=i32)
    last = jnp.minimum(blk, n_used - 1)
    blk_e = jnp.sum((gend[None, :] <= (last * BM)[:, None]).astype(i32), axis=1)
    blk_e = jnp.minimum(blk_e, N_EXPERTS - 1)
    ids = jnp.arange(N_EXPERTS, dtype=i32)
    later = jnp.logical_and(ids[None, :] > ids[:, None], (seg > 0)[None, :])
    nxt_e = jnp.min(jnp.where(later, ids[None, :], N_EXPERTS), axis=1)
    nxt_e = jnp.where(nxt_e == N_EXPERTS, -1, nxt_e).astype(i32)
    return (toff.reshape(-1), dst.reshape(-1), c8.reshape(-1), gstart + len8, seg - len8,
            blk_e, nxt_e, n_used.reshape(1).astype(i32))


def kernel(x, c, w_ada, b_ada, g_pre_mix, g_post_mix, w_in, rel_bias, sgu_ln_g, sgu_ln_b,
           w_spatial, b_spatial, w_branch_a, w_branch_b, w_gate, b_gate, w_out,
           g_pre_ffn, g_post_ffn, w_router, b_router, w_gate_up, b_gate_up, w_down, b_down):
    b, s, d = x.shape
    assert d == D_MODEL and s % TM == 0 and (b * s) % TS == 0 and s % TS == 0
    n = b * s
    nt = n // TS
    ntp = -(-nt // LANES) * LANES
    p_rows = -(-(n * TOP_K + nt * N_EXPERTS * (ROW_ALIGN - 1) + N_EXPERTS * (BM - 1)) // BM) * BM
    depth = w_ada.shape[0]
    c8 = jnp.pad(c, ((0, 8 - b), (0, 0)))
    row = lambda a: a.reshape(1, -1)

    for l in range(depth):
        mod = _ada(c8, w_ada[l], row(b_ada[l]))[:b]
        mod3 = mod.reshape(b, 6, d)

        q, k, v, u, vv = _proj(x, mod3, row(g_pre_mix[l]), w_in[l].astype(bf16),
                               row(sgu_ln_g[l]), row(sgu_ln_b[l]))
        ya = _attn(q, k, v, _attn_bias(rel_bias[l]))
        ws2 = w_spatial[l].astype(bf16).reshape(-1, 2 * SGU_BLOCK, SGU_BLOCK)
        bsf = jnp.repeat(b_spatial[l].T, SGU_WIDTH // b_spatial.shape[1], axis=1)
        x1, h2, ti, gt, tab = _mix(x, mod3, row(g_pre_mix[l]), row(g_post_mix[l]), u, vv, ya, ws2, bsf,
                                   w_branch_a[l].astype(bf16), w_branch_b[l].astype(bf16),
                                   w_gate[l].astype(bf16), row(b_gate[l]), w_out[l].astype(bf16),
                                   row(g_pre_ffn[l]), w_router[l].T.astype(bf16),
                                   b_router[l].reshape(-1, 1), ntp)
        x1f = x1.reshape(n, d)
        h2 = h2.reshape(n, d)
        toff, dst, c8s, tstart, tlen, blk_e, nxt_e, n_used = _layout_tables(tab, nt, p_rows)
        xb = _dispatch(toff, dst, c8s, tstart, tlen, n_used, h2, ti, p_rows)
        yb = _experts(blk_e, nxt_e, n_used, xb, w_gate_up[l], b_gate_up[l][:, None, :],
                      w_down[l], b_down[l][:, None, :])
        x = _combine(toff, dst, c8s, yb, ti, gt, x1f, mod3, row(g_post_ffn[l]), s // TS).reshape(b, s, d)
    return x
```

```python
import functools

import jax
import jax.numpy as jnp
from jax import lax
from jax.experimental import pallas as pl
from jax.experimental.pallas import tpu as pltpu

bf16 = jnp.bfloat16
f32 = jnp.float32
i32 = jnp.int32
u32 = jnp.uint32

D_MODEL = 1024
CHUNK = 64
N_LEFT = 8
ATT_HEADS = 8
HEAD_DIM = 64
ATT_WIDTH = 512
MAX_REL = 128
SGU_BLOCK = 128
SGU_WIDTH = 512
N_EXPERTS = 32
TOP_K = 4
SWIGLU_LIMIT = 7.0
SWIGLU_ALPHA = 1.702
EPS = 1e-6
NEG = -1e30
LOG2E = 1.4426950408889634

LANES = 128
ROW_ALIGN = 8
TM = 512
QG = 2 * CHUNK
KBAND = (N_LEFT + 2) * CHUNK
TS = 256
RS = TS * TOP_K + N_EXPERTS * ROW_ALIGN
BM = 512
VMEM_LIMIT = 56 * 2**20


def _params(sem):
    return pltpu.CompilerParams(dimension_semantics=sem, vmem_limit_bytes=VMEM_LIMIT)


def _adaln(x, g, sc, sh):
    ms = jnp.mean(x * x, axis=-1, keepdims=True)
    return (x * lax.rsqrt(ms + EPS) * g) * (1.0 + sc) + sh


def _rms(x, g):
    ms = jnp.mean(x * x, axis=-1, keepdims=True)
    return x * lax.rsqrt(ms + EPS) * g


def _sigmoid(x):
    return 1.0 / (1.0 + jnp.exp(-x))


def _ada_body(c_ref, w_ref, b_ref, o_ref):
    c = c_ref[...]
    ca = c * _sigmoid(c)
    o_ref[...] = jnp.dot(ca.astype(bf16), w_ref[...].astype(bf16),
                         preferred_element_type=f32) + b_ref[...]


def _ada(c8, w, b):
    d = w.shape[0]
    n = w.shape[1] // d
    return pl.pallas_call(
        _ada_body,
        out_shape=jax.ShapeDtypeStruct((8, n * d), f32),
        grid=(n,),
        in_specs=[pl.BlockSpec((8, d), lambda j: (0, 0)),
                  pl.BlockSpec((d, d), lambda j: (0, j)),
                  pl.BlockSpec((1, d), lambda j: (0, j))],
        out_specs=pl.BlockSpec((8, d), lambda j: (0, j)),
        compiler_params=_params(("arbitrary",)),
        name="ada",
    )(c8, w, b)


def _proj_body(x_ref, mod_ref, g_ref, w_ref, lng_ref, lnb_ref,
               q_ref, k_ref, v_ref, u_ref, vv_ref):
    h = _adaln(x_ref[0], g_ref[...], mod_ref[0, 1:2, :], mod_ref[0, 0:1, :])
    p = jnp.dot(h.astype(bf16), w_ref[...], preferred_element_type=f32)
    aw = ATT_WIDTH
    q_ref[0] = (p[:, 0:aw] * (HEAD_DIM ** -0.5 * LOG2E)).astype(bf16)
    k_ref[0] = p[:, aw:2 * aw].astype(bf16)
    v_ref[0] = p[:, 2 * aw:3 * aw].astype(bf16)
    z = p[:, 3 * aw:]
    zg = 0.5 * z * (1.0 + lax.erf(z * (2.0 ** -0.5)))
    u_ref[0] = zg[:, :SGU_WIDTH].astype(bf16)
    vv = zg[:, SGU_WIDTH:]
    mu = jnp.mean(vv, axis=-1, keepdims=True)
    var = jnp.mean(jnp.square(vv - mu), axis=-1, keepdims=True)
    vn = (vv - mu) * lax.rsqrt(var + EPS) * lng_ref[...] + lnb_ref[...]
    vv_ref[0] = vn.astype(bf16)


def _proj(x, mod3, g, w_in, lng, lnb):
    b, s, d = x.shape
    tok = lambda w: pl.BlockSpec((1, TM, w), lambda bi, i: (bi, i, 0))
    full = lambda a: pl.BlockSpec(a.shape, lambda bi, i: (0,) * a.ndim)
    o512 = jax.ShapeDtypeStruct((b, s, ATT_WIDTH), bf16)
    return pl.pallas_call(
        _proj_body,
        out_shape=(o512,) * 5,
        grid=(b, s // TM),
        in_specs=[tok(d), pl.BlockSpec((1, 6, d), lambda bi, i: (bi, 0, 0)),
                  full(g), full(w_in), full(lng), full(lnb)],
        out_specs=(tok(ATT_WIDTH),) * 5,
        compiler_params=_params(("parallel", "arbitrary")),
        name="proj",
    )(x, mod3, g, w_in, lng, lnb)


def _attn_body(q_ref, kp_ref, kc_ref, vp_ref, vc_ref, bias_ref, o_ref, kbuf, vbuf):
    first = pl.program_id(1) == 0
    nhp = ATT_HEADS // 2
    ones = jnp.ones((TM, LANES), bf16)

    def fill(rows, k_src, v_src):
        kbuf[rows, :] = k_src[0]
        for hp in range(nhp):
            vbuf[rows, 2 * hp * LANES:(2 * hp + 1) * LANES] = v_src[0, :, hp * LANES:(hp + 1) * LANES]
            vbuf[rows, (2 * hp + 1) * LANES:(2 * hp + 2) * LANES] = ones

    fill(slice(TM, 2 * TM), kc_ref, vc_ref)

    @pl.when(first)
    def _():
        kbuf[0:TM, :] = jnp.zeros((TM, kbuf.shape[1]), bf16)
        vbuf[0:TM, :] = jnp.zeros((TM, vbuf.shape[1]), bf16)

    @pl.when(jnp.logical_not(first))
    def _():
        fill(slice(0, TM), kp_ref, vp_ref)

    lo = lax.broadcasted_iota(i32, (QG, LANES), 1) < HEAD_DIM

    def group(p, carry):
        r0 = pl.multiple_of(p * QG, QG)
        for hp in range(nhp):
            c0 = hp * LANES
            qp = q_ref[0, pl.ds(r0, QG), c0:c0 + LANES]
            zero = jnp.zeros_like(qp)
            q2 = jnp.concatenate([jnp.where(lo, qp, zero), jnp.where(lo, zero, qp)], axis=0)
            kb = kbuf[pl.ds(r0, KBAND), c0:c0 + LANES]
            s = lax.dot_general(q2, kb, (((1,), (1,)), ((), ())), preferred_element_type=f32)
            sb = (s + bias_ref[hp]).astype(bf16)
            m = jnp.max(sb, axis=-1, keepdims=True)
            e = jnp.exp2(sb - m)
            vb = vbuf[pl.ds(r0, KBAND), 2 * c0:2 * c0 + 2 * LANES]
            o2 = jnp.dot(e, vb, preferred_element_type=f32)
            on = o2[:, :LANES] / o2[:, LANES:]
            o = jnp.where(lo, on[:QG], on[QG:])
            o_ref[0, pl.ds(r0, QG), c0:c0 + LANES] = o.astype(bf16)
        return carry

    lax.fori_loop(0, TM // QG, group, 0, unroll=True)


def _attn(q, k, v, bias2):
    b, s, w = q.shape
    cur = pl.BlockSpec((1, TM, w), lambda bi, i: (bi, i, 0))
    prev = pl.BlockSpec((1, TM, w), lambda bi, i: (bi, jnp.maximum(i - 1, 0), 0))
    return pl.pallas_call(
        _attn_body,
        out_shape=jax.ShapeDtypeStruct((b, s, w), bf16),
        grid=(b, s // TM),
        in_specs=[cur, prev, cur, prev, cur,
                  pl.BlockSpec(bias2.shape, lambda bi, i: (0, 0, 0))],
        out_specs=cur,
        scratch_shapes=[pltpu.VMEM((2 * TM, w), bf16), pltpu.VMEM((2 * TM, 2 * w), bf16)],
        compiler_params=_params(("parallel", "arbitrary")),
        name="attn",
    )(q, k, k, v, v, bias2)


def _attn_bias(rel_bias):
    h = rel_bias.shape[0]
    period = 1024
    assert period >= QG + KBAND - 1 and KBAND - 2 * MAX_REL >= 0
    far = jnp.broadcast_to(rel_bias[:, 2 * MAX_REL:], (h, period))
    near = rel_bias[:, :0:-1]
    v = jnp.concatenate([far[:, :KBAND - 2 * MAX_REL], near, far[:, KBAND:]], axis=1).astype(f32)
    flat = jnp.broadcast_to(v[:, None, :], (h, QG, period)).reshape(h, QG * period)
    toep = flat[:, :QG * (period - 1)].reshape(h, QG, period - 1)[:, :, :KBAND]
    i = jnp.arange(QG, dtype=i32)[:, None]
    j = jnp.arange(KBAND, dtype=i32)[None, :]
    jb = j - (i // CHUNK) * CHUNK
    valid = jnp.logical_and(jb >= 0, jb < CHUNK * (N_LEFT + 1))
    bias = jnp.where(valid[None], toep * LOG2E, NEG)
    return bias.reshape(ATT_HEADS // 2, 2 * QG, KBAND)


def _route(hb, wr_ref, br_ref):
    lg = lax.dot_general(wr_ref[...], hb, (((1,), (1,)), ((), ())),
                         preferred_element_type=f32) + br_ref[...]
    e_iota = lax.broadcasted_iota(i32, lg.shape, 0)
    vals, idxs = [], []
    hits = jnp.zeros(lg.shape, f32)
    for _ in range(TOP_K):
        m = jnp.max(lg, axis=0, keepdims=True)
        idx = jnp.min(jnp.where(lg == m, e_iota, N_EXPERTS), axis=0, keepdims=True)
        hit = e_iota == idx
        hits = hits + jnp.where(hit, 1.0, 0.0)
        lg = jnp.where(hit, -jnp.inf, lg)
        vals.append(m)
        idxs.append(idx)
    ex = [jnp.exp(v - vals[0]) for v in vals]
    den = ex[0] + ex[1] + ex[2] + ex[3]
    return (jnp.concatenate(idxs, axis=0), jnp.concatenate([e / den for e in ex], axis=0), hits)


def _mix_body(x_ref, mod_ref, gpre_ref, gpost_ref, u_ref, vv_ref, ya_ref, ws_ref, bs_ref,
              wa_ref, wb_ref, wg_ref, bg_ref, wo_ref, gffn_ref, wr_ref, br_ref,
              o_ref, h2_ref, ti_ref, gt_ref, tab_ref, ybuf):
    d = D_MODEL
    x = x_ref[0]
    h = _adaln(x, gpre_ref[...], mod_ref[0, 1:2, :], mod_ref[0, 0:1, :]).astype(bf16)
    gs = _sigmoid(jnp.dot(h, wg_ref[...], preferred_element_type=f32) + bg_ref[...])

    blk = SGU_BLOCK
    row = lax.broadcasted_iota(i32, (2 * blk, blk), 0)
    colv = lax.broadcasted_iota(i32, (2 * blk, blk), 1)
    causal = colv <= jnp.bitwise_and(row, blk - 1)
    lo = lax.broadcasted_iota(i32, (blk, LANES), 1) < (LANES // 2)
    for gp in range(SGU_WIDTH // LANES):
        c0 = gp * LANES
        w2 = ws_ref[gp]
        w2 = jnp.where(causal, w2, jnp.zeros_like(w2))
        for bi in range(TM // blk):
            r0 = bi * blk
            s2 = jnp.dot(w2, vv_ref[0, r0:r0 + blk, c0:c0 + LANES], preferred_element_type=f32)
            s = jnp.where(lo, s2[:blk], s2[blk:]) + bs_ref[:, c0:c0 + LANES]
            yb = u_ref[0, r0:r0 + blk, c0:c0 + LANES].astype(f32) * s
            ybuf[r0:r0 + blk, c0:c0 + LANES] = yb.astype(bf16)

    a = jnp.dot(ya_ref[0], wa_ref[...], preferred_element_type=f32)
    bb = jnp.dot(ybuf[...], wb_ref[...], preferred_element_type=f32)
    merged = gs[:, :d] * a + gs[:, d:] * bb
    y = jnp.dot(merged.astype(bf16), wo_ref[...], preferred_element_type=f32)
    x1 = x + mod_ref[0, 2:3, :] * _rms(y, gpost_ref[...])
    o_ref[0] = x1

    hb = _adaln(x1, gffn_ref[...], mod_ref[0, 4:5, :], mod_ref[0, 3:4, :]).astype(bf16)
    h2_ref[0] = hb
    ids, gates, hits = _route(hb, wr_ref, br_ref)
    ti_ref[...] = ids
    gt_ref[...] = gates

    step = pl.program_id(0) * pl.num_programs(1) + pl.program_id(1)

    @pl.when(step == 0)
    def _():
        tab_ref[...] = jnp.zeros_like(tab_ref)

    lane = lax.broadcasted_iota(i32, tab_ref.shape, 1)
    acc = tab_ref[...]
    for j in range(TM // TS):
        cnt = jnp.sum(hits[:, j * TS:(j + 1) * TS], axis=1, keepdims=True)
        c8 = jnp.floor((cnt + (ROW_ALIGN - 1.0)) * (1.0 / ROW_ALIGN)) * ROW_ALIGN
        acc = acc + jnp.where(lane == step * (TM // TS) + j, c8, 0.0)
    tab_ref[...] = acc


def _mix(x, mod3, gpre, gpost, u, vv, ya, ws2, bsf, wa, wb, wg, bg, wo, gffn, wrt, br, ntp):
    b, s, d = x.shape
    n = b * s
    tok = lambda w: pl.BlockSpec((1, TM, w), lambda bi, i: (bi, i, 0))
    full = lambda a: pl.BlockSpec(a.shape, lambda bi, i: (0,) * a.ndim)
    lane = pl.BlockSpec((TOP_K, TM), lambda bi, i: (0, bi * (s // TM) + i))
    return pl.pallas_call(
        _mix_body,
        out_shape=(jax.ShapeDtypeStruct((b, s, d), f32),
                   jax.ShapeDtypeStruct((b, s, d), bf16),
                   jax.ShapeDtypeStruct((TOP_K, n), i32),
                   jax.ShapeDtypeStruct((TOP_K, n), f32),
                   jax.ShapeDtypeStruct((N_EXPERTS, ntp), f32)),
        grid=(b, s // TM),
        in_specs=[tok(d), pl.BlockSpec((1, 6, d), lambda bi, i: (bi, 0, 0)),
                  full(gpre), full(gpost), tok(SGU_WIDTH), tok(SGU_WIDTH), tok(ATT_WIDTH),
                  full(ws2), full(bsf), full(wa), full(wb), full(wg), full(bg), full(wo),
                  full(gffn), full(wrt), full(br)],
        out_specs=(tok(d), tok(d), lane, lane,
                   pl.BlockSpec((N_EXPERTS, ntp), lambda bi, i: (0, 0))),
        scratch_shapes=[pltpu.VMEM((TM, SGU_WIDTH), bf16)],
        compiler_params=_params(("arbitrary", "arbitrary")),
        name="mix",
    )(x, mod3, gpre, gpost, u, vv, ya, ws2, bsf, wa, wb, wg, bg, wo, gffn, wrt, br)


def _sorted_positions(ti):
    ts = ti.shape[1]
    e_iota = lax.broadcasted_iota(i32, (N_EXPERTS, ts), 0)
    upper = (lax.broadcasted_iota(i32, (ts, ts), 0) < lax.broadcasted_iota(i32, (ts, ts), 1))
    upper = jnp.where(upper, 1.0, 0.0).astype(bf16)
    hits, prefs, cnts = [], [], []
    for k in range(TOP_K):
        hit = ti[k:k + 1, :] == e_iota
        hf = jnp.where(hit, 1.0, 0.0)
        prefs.append(jnp.dot(hf.astype(bf16), upper, preferred_element_type=f32))
        cnts.append(jnp.sum(hf, axis=1, keepdims=True))
        hits.append(hit)
    total = cnts[0] + cnts[1] + cnts[2] + cnts[3]
    c8 = jnp.floor((total + (ROW_ALIGN - 1.0)) * (1.0 / ROW_ALIGN)) * ROW_ALIGN
    lower = (lax.broadcasted_iota(i32, (N_EXPERTS, N_EXPERTS), 1)
             < lax.broadcasted_iota(i32, (N_EXPERTS, N_EXPERTS), 0))
    lower = jnp.where(lower, 1.0, 0.0).astype(bf16)
    c8b = jnp.broadcast_to(c8, (N_EXPERTS, LANES)).astype(bf16)
    start = jnp.dot(lower, c8b, preferred_element_type=f32)[:, 0:1]
    pos = []
    for k in range(TOP_K):
        pe = start + prefs[k]
        pos.append(jnp.sum(jnp.where(hits[k], pe, 0.0), axis=0, keepdims=True).astype(i32))
        start = start + cnts[k]
    return pos


def _pack_halves(x):
    half = x.shape[1] // 2
    lo = lax.bitcast_convert_type(x[:, :half], u32)
    hi = lax.bitcast_convert_type(x[:, half:], u32)
    return jnp.bitwise_or(lax.shift_right_logical(lo, jnp.uint32(16)),
                          jnp.bitwise_and(hi, jnp.uint32(0xFFFF0000)))


def _unpack_halves(w):
    lo = lax.bitcast_convert_type(lax.shift_left(w, jnp.uint32(16)), f32)
    hi = lax.bitcast_convert_type(jnp.bitwise_and(w, jnp.uint32(0xFFFF0000)), f32)
    return jnp.concatenate([lo, hi], axis=1).astype(bf16)


def _start_runs(i, toff_ref, dst_ref, c8_ref, make):
    def one(e, carry):
        j = i * N_EXPERTS + e
        n = pl.multiple_of(c8_ref[j], ROW_ALIGN)
        so = pl.multiple_of(toff_ref[j], ROW_ALIGN)
        do = pl.multiple_of(dst_ref[j], ROW_ALIGN)

        @pl.when(n > 0)
        def _():
            make(so, do, n).start()
        return carry
    lax.fori_loop(0, N_EXPERTS, one, 0)


def _tile_rows(i, toff_ref, c8_ref):
    j = i * N_EXPERTS + (N_EXPERTS - 1)
    return pl.multiple_of(toff_ref[j] + c8_ref[j], ROW_ALIGN)


def _dispatch_body(toff_ref, dst_ref, c8_ref, tstart_ref, tlen_ref, nu_ref,
                   h2_ref, ti_ref, xb_ref, sbuf, zbuf, sems, sem):
    i = pl.program_id(0)
    slot = lax.rem(i, 2)
    p = _sorted_positions(ti_ref[...])
    r = lax.broadcasted_iota(i32, (RS, TS), 0)
    pm = jnp.where(r == p[0], 1.0, jnp.where(r == p[1], 1.0,
         jnp.where(r == p[2], 1.0, jnp.where(r == p[3], 1.0, 0.0))))
    srt = jnp.dot(pm.astype(bf16), h2_ref[...], preferred_element_type=f32)
    sbuf[slot] = _pack_halves(srt)

    def make(so, do, n):
        return pltpu.make_async_copy(sbuf.at[slot, pl.ds(so, n)], xb_ref.at[pl.ds(do, n)],
                                     sems.at[slot])
    _start_runs(i, toff_ref, dst_ref, c8_ref, make)

    def wait_tile(t, sl):
        n = _tile_rows(t, toff_ref, c8_ref)
        pltpu.make_async_copy(sbuf.at[sl, pl.ds(0, n)], xb_ref.at[pl.ds(0, n)], sems.at[sl]).wait()

    @pl.when(i > 0)
    def _():
        wait_tile(i - 1, 1 - slot)

    @pl.when(i == pl.num_programs(0) - 1)
    def _():
        wait_tile(i, slot)
        zbuf[...] = jnp.zeros_like(zbuf)

        def fill(action):
            def tail(e, carry):
                n = pl.multiple_of(tlen_ref[e], ROW_ALIGN)
                do = pl.multiple_of(tstart_ref[e], ROW_ALIGN)

                @pl.when(n > 0)
                def _():
                    action(pltpu.make_async_copy(zbuf.at[pl.ds(0, n)], xb_ref.at[pl.ds(do, n)], sem))
                return carry
            lax.fori_loop(0, N_EXPERTS, tail, 0)

            def unused(b, carry):
                do = pl.multiple_of(b * BM, BM)
                action(pltpu.make_async_copy(zbuf, xb_ref.at[pl.ds(do, BM)], sem))
                return carry
            lax.fori_loop(nu_ref[0], xb_ref.shape[0] // BM, unused, 0)
        fill(lambda c: c.start())
        fill(lambda c: c.wait())


def _dispatch(toff, dst, c8s, tstart, tlen, n_used, h2, ti, p_rows):
    n, d = h2.shape
    return pl.pallas_call(
        _dispatch_body,
        out_shape=jax.ShapeDtypeStruct((p_rows, d // 2), u32),
        grid_spec=pltpu.PrefetchScalarGridSpec(
            num_scalar_prefetch=6,
            grid=(n // TS,),
            in_specs=[pl.BlockSpec((TS, d), lambda i, *_: (i, 0)),
                      pl.BlockSpec((TOP_K, TS), lambda i, *_: (0, i))],
            out_specs=pl.BlockSpec(memory_space=pl.ANY),
            scratch_shapes=[pltpu.VMEM((2, RS, d // 2), u32), pltpu.VMEM((BM, d // 2), u32),
                            pltpu.SemaphoreType.DMA((2,)), pltpu.SemaphoreType.DMA(())]),
        compiler_params=_params(("arbitrary",)),
        name="dispatch",
    )(toff, dst, c8s, tstart, tlen, n_used, h2, ti)


def _expert_body(be_ref, nxt_ref, nu_ref, x_ref, wgu_hbm, bgu_ref, wd_hbm, bd_ref, o_ref,
                 wgu_st, wd_st, wgu_bf, wd_bf, sems):
    b = pl.program_id(0)
    d = D_MODEL
    e = be_ref[b]

    def weight_copies(ex):
        return (pltpu.make_async_copy(wgu_hbm.at[ex], wgu_st, sems.at[0]),
                pltpu.make_async_copy(wd_hbm.at[ex], wd_st, sems.at[1]))

    @pl.when(b >= nu_ref[0])
    def _():
        o_ref[...] = jnp.zeros_like(o_ref)

    @pl.when(b < nu_ref[0])
    def _():
        @pl.when(b == 0)
        def _():
            for c in weight_copies(e):
                c.start()

        @pl.when(jnp.logical_or(b == 0, e != be_ref[jnp.maximum(b - 1, 0)]))
        def _():
            for c in weight_copies(e):
                c.wait()
            wgu_bf[...] = wgu_st[...].astype(bf16)
            wd_bf[...] = wd_st[...].astype(bf16)
            nxt = nxt_ref[e]

            @pl.when(nxt >= 0)
            def _():
                for c in weight_copies(nxt):
                    c.start()

        gu = jnp.dot(_unpack_halves(x_ref[...]), wgu_bf[...], preferred_element_type=f32) + bgu_ref[0]
        glu = jnp.minimum(gu[:, :d], SWIGLU_LIMIT)
        lin = jnp.clip(gu[:, d:], -SWIGLU_LIMIT, SWIGLU_LIMIT)
        act = glu * _sigmoid(SWIGLU_ALPHA * glu) * (lin + 1.0)
        o_ref[...] = jnp.dot(act.astype(bf16), wd_bf[...], preferred_element_type=f32) + bd_ref[0]


def _experts(blk_e, nxt_e, n_used, xb, wgu, bgu, wd, bd):
    p_rows, d = xb.shape[0], wd.shape[2]
    rows = lambda w: pl.BlockSpec((BM, w), lambda b, be, nx, nu: (b, 0))
    per_e = lambda a: pl.BlockSpec((1,) + a.shape[1:], lambda b, be, nx, nu: (be[b], 0, 0))
    hbm = pl.BlockSpec(memory_space=pl.ANY)
    return pl.pallas_call(
        _expert_body,
        out_shape=jax.ShapeDtypeStruct((p_rows, d), f32),
        grid_spec=pltpu.PrefetchScalarGridSpec(
            num_scalar_prefetch=3,
            grid=(p_rows // BM,),
            in_specs=[rows(xb.shape[1]), hbm, per_e(bgu), hbm, per_e(bd)],
            out_specs=rows(d),
            scratch_shapes=[pltpu.VMEM(wgu.shape[1:], f32), pltpu.VMEM(wd.shape[1:], f32),
                            pltpu.VMEM(wgu.shape[1:], bf16), pltpu.VMEM(wd.shape[1:], bf16),
                            pltpu.SemaphoreType.DMA((2,))]),
        compiler_params=_params(("arbitrary",)),
        name="experts",
    )(blk_e, nxt_e, n_used, xb, wgu, bgu, wd, bd)


def _combine_body(toff_ref, dst_ref, c8_ref, yb_ref, ti_ref, gt_ref, x1_ref, mod_ref, g_ref,
                  o_ref, sbuf, sems):
    i = pl.program_id(0)
    slot = lax.rem(i, 2)

    def fetch(t, sl):
        def make(so, do, n):
            return pltpu.make_async_copy(yb_ref.at[pl.ds(do, n)], sbuf.at[sl, pl.ds(so, n)],
                                         sems.at[sl])
        _start_runs(t, toff_ref, dst_ref, c8_ref, make)

    @pl.when(i == 0)
    def _():
        sbuf[...] = jnp.zeros_like(sbuf)
        fetch(i, slot)

    @pl.when(i + 1 < pl.num_programs(0))
    def _():
        fetch(i + 1, 1 - slot)

    p = _sorted_positions(ti_ref[...])
    gt = gt_ref[...]
    r = lax.broadcasted_iota(i32, (RS, TS), 0)
    m = [r == p[k] for k in range(TOP_K)]
    pm = jnp.where(m[0], 1.0, jnp.where(m[1], 1.0, jnp.where(m[2], 1.0, jnp.where(m[3], 1.0, 0.0))))
    gm = jnp.where(m[0], gt[0:1, :], jnp.where(m[1], gt[1:2, :],
         jnp.where(m[2], gt[2:3, :], jnp.where(m[3], gt[3:4, :], 0.0))))
    grow = jnp.sum(gm, axis=1, keepdims=True)

    n = _tile_rows(i, toff_ref, c8_ref)
    pltpu.make_async_copy(yb_ref.at[pl.ds(0, n)], sbuf.at[slot, pl.ds(0, n)], sems.at[slot]).wait()
    sg = (sbuf[slot] * grow).astype(bf16)
    y = lax.dot_general(pm.astype(bf16), sg, (((0,), (0,)), ((), ())), preferred_element_type=f32)
    o_ref[...] = x1_ref[...] + mod_ref[0, 5:6, :] * _rms(y, g_ref[...])


def _combine(toff, dst, c8s, yb, ti, gt, x1, mod3, g, tiles_per_batch):
    n, d = x1.shape
    tok = pl.BlockSpec((TS, d), lambda i, *_: (i, 0))
    lane = pl.BlockSpec((TOP_K, TS), lambda i, *_: (0, i))
    return pl.pallas_call(
        _combine_body,
        out_shape=jax.ShapeDtypeStruct((n, d), f32),
        grid_spec=pltpu.PrefetchScalarGridSpec(
            num_scalar_prefetch=3,
            grid=(n // TS,),
            in_specs=[pl.BlockSpec(memory_space=pl.ANY), lane, lane, tok,
                      pl.BlockSpec((1, 6, d), lambda i, *_: (i // tiles_per_batch, 0, 0)),
                      pl.BlockSpec(g.shape, lambda i, *_: (0, 0))],
            out_specs=tok,
            scratch_shapes=[pltpu.VMEM((2, RS, d), f32), pltpu.SemaphoreType.DMA((2,))]),
        compiler_params=_params(("arbitrary",)),
        name="combine",
    )(toff, dst, c8s, yb, ti, gt, x1, mod3, g)


def _layout_tables(tab, nt, p_rows):
    c8 = tab[:, :nt].T.astype(i32)
    toff = jnp.cumsum(c8, axis=1) - c8
    len8 = jnp.sum(c8, axis=0)
    seg = (len8 + BM - 1) // BM * BM
    gend = jnp.cumsum(seg)
    gstart = gend - seg
    dst = jnp.cumsum(c8, axis=0) - c8 + gstart[None, :]
    n_used = gend[-1] // BM
    blk = jnp.arange(p_rows // BM, dtype=i32)
    last = jnp.minimum(blk, n_used - 1)
    blk_e = jnp.sum((gend[None, :] <= (last * BM)[:, None]).astype(i32), axis=1)
    blk_e = jnp.minimum(blk_e, N_EXPERTS - 1)
    ids = jnp.arange(N_EXPERTS, dtype=i32)
    later = jnp.logical_and(ids[None, :] > ids[:, None], (seg > 0)[None, :])
    nxt_e = jnp.min(jnp.where(later, ids[None, :], N_EXPERTS), axis=1)
    nxt_e = jnp.where(nxt_e == N_EXPERTS, -1, nxt_e).astype(i32)
    return (toff.reshape(-1), dst.reshape(-1), c8.reshape(-1), gstart + len8, seg - len8,
            blk_e, nxt_e, n_used.reshape(1).astype(i32))


def kernel(x, c, w_ada, b_ada, g_pre_mix, g_post_mix, w_in, rel_bias, sgu_ln_g, sgu_ln_b,
           w_spatial, b_spatial, w_branch_a, w_branch_b, w_gate, b_gate, w_out,
           g_pre_ffn, g_post_ffn, w_router, b_router, w_gate_up, b_gate_up, w_down, b_down):
    b, s, d = x.shape
    assert d == D_MODEL and s % TM == 0 and (b * s) % TS == 0 and s % TS == 0
    n = b * s
    nt = n // TS
    ntp = -(-nt // LANES) * LANES
    p_rows = -(-(n * TOP_K + nt * N_EXPERTS * (ROW_ALIGN - 1) + N_EXPERTS * (BM - 1)) // BM) * BM
    depth = w_ada.shape[0]
    c8 = jnp.pad(c, ((0, 8 - b), (0, 0)))
    row = lambda a: a.reshape(1, -1)

    for l in range(depth):
        mod = _ada(c8, w_ada[l], row(b_ada[l]))[:b]
        mod3 = mod.reshape(b, 6, d)

        q, k, v, u, vv = _proj(x, mod3, row(g_pre_mix[l]), w_in[l].astype(bf16),
                               row(sgu_ln_g[l]), row(sgu_ln_b[l]))
        ya = _attn(q, k, v, _attn_bias(rel_bias[l]))
        ws2 = w_spatial[l].astype(bf16).reshape(-1, 2 * SGU_BLOCK, SGU_BLOCK)
        bsf = jnp.repeat(b_spatial[l].T, SGU_WIDTH // b_spatial.shape[1], axis=1)
        x1, h2, ti, gt, tab = _mix(x, mod3, row(g_pre_mix[l]), row(g_post_mix[l]), u, vv, ya, ws2, bsf,
                                   w_branch_a[l].astype(bf16), w_branch_b[l].astype(bf16),
                                   w_gate[l].astype(bf16), row(b_gate[l]), w_out[l].astype(bf16),
                                   row(g_pre_ffn[l]), w_router[l].T.astype(bf16),
                                   b_router[l].reshape(-1, 1), ntp)
        x1f = x1.reshape(n, d)
        h2 = h2.reshape(n, d)
        toff, dst, c8s, tstart, tlen, blk_e, nxt_e, n_used = _layout_tables(tab, nt, p_rows)
        xb = _dispatch(toff, dst, c8s, tstart, tlen, n_used, h2, ti, p_rows)
        yb = _experts(blk_e, nxt_e, n_used, xb, w_gate_up[l], b_gate_up[l][:, None, :],
                      w_down[l], b_down[l][:, None, :])
        x = _combine(toff, dst, c8s, yb, ti, gt, x1f, mod3, row(g_post_ffn[l]), s // TS).reshape(b, s, d)
    return x
```

```python
import functools

import jax
import jax.numpy as jnp
from jax import lax
from jax.experimental import pallas as pl
from jax.experimental.pallas import tpu as pltpu

bf16 = jnp.bfloat16
f32 = jnp.float32
i32 = jnp.int32
u32 = jnp.uint32

D_MODEL = 1024
CHUNK = 64
N_LEFT = 8
ATT_HEADS = 8
HEAD_DIM = 64
ATT_WIDTH = 512
MAX_REL = 128
SGU_BLOCK = 128
SGU_WIDTH = 512
N_EXPERTS = 32
TOP_K = 4
SWIGLU_LIMIT = 7.0
SWIGLU_ALPHA = 1.702
EPS = 1e-6
NEG = -1e30
LOG2E = 1.4426950408889634

LANES = 128
ROW_ALIGN = 8
TM = 512
QG = 2 * CHUNK
KBAND = (N_LEFT + 2) * CHUNK
TS = 256
TG = 2
RS = TS * TOP_K + N_EXPERTS * ROW_ALIGN
BM = 512
VMEM_LIMIT = 56 * 2**20


def _params(sem):
    return pltpu.CompilerParams(dimension_semantics=sem, vmem_limit_bytes=VMEM_LIMIT)


def _adaln(x, g, sc, sh):
    ms = jnp.mean(x * x, axis=-1, keepdims=True)
    return (x * lax.rsqrt(ms + EPS) * g) * (1.0 + sc) + sh


def _rms(x, g):
    ms = jnp.mean(x * x, axis=-1, keepdims=True)
    return x * lax.rsqrt(ms + EPS) * g


def _sigmoid(x):
    return 1.0 / (1.0 + jnp.exp(-x))


def _ada_body(c_ref, w_ref, b_ref, o_ref):
    c = c_ref[...]
    ca = c * _sigmoid(c)
    o_ref[...] = jnp.dot(ca.astype(bf16), w_ref[...].astype(bf16),
                         preferred_element_type=f32) + b_ref[...]


def _ada(c8, w, b):
    d = w.shape[0]
    n = w.shape[1] // d
    return pl.pallas_call(
        _ada_body,
        out_shape=jax.ShapeDtypeStruct((8, n * d), f32),
        grid=(n,),
        in_specs=[pl.BlockSpec((8, d), lambda j: (0, 0)),
                  pl.BlockSpec((d, d), lambda j: (0, j)),
                  pl.BlockSpec((1, d), lambda j: (0, j))],
        out_specs=pl.BlockSpec((8, d), lambda j: (0, j)),
        compiler_params=_params(("arbitrary",)),
        name="ada",
    )(c8, w, b)


def _proj_body(x_ref, mod_ref, g_ref, w_ref, lng_ref, lnb_ref,
               q_ref, k_ref, v_ref, u_ref, vv_ref):
    h = _adaln(x_ref[0], g_ref[...], mod_ref[0, 1:2, :], mod_ref[0, 0:1, :])
    p = jnp.dot(h.astype(bf16), w_ref[...], preferred_element_type=f32)
    aw = ATT_WIDTH
    q_ref[0] = (p[:, 0:aw] * (HEAD_DIM ** -0.5 * LOG2E)).astype(bf16)
    k_ref[0] = p[:, aw:2 * aw].astype(bf16)
    v_ref[0] = p[:, 2 * aw:3 * aw].astype(bf16)
    z = p[:, 3 * aw:]
    zg = 0.5 * z * (1.0 + lax.erf(z * (2.0 ** -0.5)))
    u_ref[0] = zg[:, :SGU_WIDTH].astype(bf16)
    vv = zg[:, SGU_WIDTH:]
    mu = jnp.mean(vv, axis=-1, keepdims=True)
    var = jnp.mean(jnp.square(vv - mu), axis=-1, keepdims=True)
    vn = (vv - mu) * lax.rsqrt(var + EPS) * lng_ref[...] + lnb_ref[...]
    vv_ref[0] = vn.astype(bf16)


def _proj(x, mod3, g, w_in, lng, lnb):
    b, s, d = x.shape
    tok = lambda w: pl.BlockSpec((1, TM, w), lambda bi, i: (bi, i, 0))
    full = lambda a: pl.BlockSpec(a.shape, lambda bi, i: (0,) * a.ndim)
    o512 = jax.ShapeDtypeStruct((b, s, ATT_WIDTH), bf16)
    return pl.pallas_call(
        _proj_body,
        out_shape=(o512,) * 5,
        grid=(b, s // TM),
        in_specs=[tok(d), pl.BlockSpec((1, 6, d), lambda bi, i: (bi, 0, 0)),
                  full(g), full(w_in), full(lng), full(lnb)],
        out_specs=(tok(ATT_WIDTH),) * 5,
        compiler_params=_params(("parallel", "arbitrary")),
        name="proj",
    )(x, mod3, g, w_in, lng, lnb)


def _attn_body(q_ref, kp_ref, kc_ref, vp_ref, vc_ref, bias_ref, o_ref, kbuf, vbuf):
    first = pl.program_id(1) == 0
    nhp = ATT_HEADS // 2
    ones = jnp.ones((TM, LANES), bf16)

    def fill(rows, k_src, v_src):
        kbuf[rows, :] = k_src[0]
        for hp in range(nhp):
            vbuf[rows, 2 * hp * LANES:(2 * hp + 1) * LANES] = v_src[0, :, hp * LANES:(hp + 1) * LANES]
            vbuf[rows, (2 * hp + 1) * LANES:(2 * hp + 2) * LANES] = ones

    fill(slice(TM, 2 * TM), kc_ref, vc_ref)

    @pl.when(first)
    def _():
        kbuf[0:TM, :] = jnp.zeros((TM, kbuf.shape[1]), bf16)
        vbuf[0:TM, :] = jnp.zeros((TM, vbuf.shape[1]), bf16)

    @pl.when(jnp.logical_not(first))
    def _():
        fill(slice(0, TM), kp_ref, vp_ref)

    lo = lax.broadcasted_iota(i32, (QG, LANES), 1) < HEAD_DIM

    def group(p, carry):
        r0 = pl.multiple_of(p * QG, QG)
        for hp in range(nhp):
            c0 = hp * LANES
            qp = q_ref[0, pl.ds(r0, QG), c0:c0 + LANES]
            zero = jnp.zeros_like(qp)
            q2 = jnp.concatenate([jnp.where(lo, qp, zero), jnp.where(lo, zero, qp)], axis=0)
            kb = kbuf[pl.ds(r0, KBAND), c0:c0 + LANES]
            s = lax.dot_general(q2, kb, (((1,), (1,)), ((), ())), preferred_element_type=f32)
            sb = (s + bias_ref[hp]).astype(bf16)
            m = jnp.max(sb, axis=-1, keepdims=True)
            e = jnp.exp2(sb - m)
            vb = vbuf[pl.ds(r0, KBAND), 2 * c0:2 * c0 + 2 * LANES]
            o2 = jnp.dot(e, vb, preferred_element_type=f32)
            on = o2[:, :LANES] / o2[:, LANES:]
            o = jnp.where(lo, on[:QG], on[QG:])
            o_ref[0, pl.ds(r0, QG), c0:c0 + LANES] = o.astype(bf16)
        return carry

    lax.fori_loop(0, TM // QG, group, 0, unroll=True)


def _attn(q, k, v, bias2):
    b, s, w = q.shape
    cur = pl.BlockSpec((1, TM, w), lambda bi, i: (bi, i, 0))
    prev = pl.BlockSpec((1, TM, w), lambda bi, i: (bi, jnp.maximum(i - 1, 0), 0))
    return pl.pallas_call(
        _attn_body,
        out_shape=jax.ShapeDtypeStruct((b, s, w), bf16),
        grid=(b, s // TM),
        in_specs=[cur, prev, cur, prev, cur,
                  pl.BlockSpec(bias2.shape, lambda bi, i: (0, 0, 0))],
        out_specs=cur,
        scratch_shapes=[pltpu.VMEM((2 * TM, w), bf16), pltpu.VMEM((2 * TM, 2 * w), bf16)],
        compiler_params=_params(("parallel", "arbitrary")),
        name="attn",
    )(q, k, k, v, v, bias2)


def _attn_bias(rel_bias):
    h = rel_bias.shape[0]
    period = 1024
    assert period >= QG + KBAND - 1 and KBAND - 2 * MAX_REL >= 0
    far = jnp.broadcast_to(rel_bias[:, 2 * MAX_REL:], (h, period))
    near = rel_bias[:, :0:-1]
    v = jnp.concatenate([far[:, :KBAND - 2 * MAX_REL], near, far[:, KBAND:]], axis=1).astype(f32)
    flat = jnp.broadcast_to(v[:, None, :], (h, QG, period)).reshape(h, QG * period)
    toep = flat[:, :QG * (period - 1)].reshape(h, QG, period - 1)[:, :, :KBAND]
    i = jnp.arange(QG, dtype=i32)[:, None]
    j = jnp.arange(KBAND, dtype=i32)[None, :]
    jb = j - (i // CHUNK) * CHUNK
    valid = jnp.logical_and(jb >= 0, jb < CHUNK * (N_LEFT + 1))
    bias = jnp.where(valid[None], toep * LOG2E, NEG)
    return bias.reshape(ATT_HEADS // 2, 2 * QG, KBAND)


def _route(hb, wr_ref, br_ref):
    lg = lax.dot_general(wr_ref[...], hb, (((1,), (1,)), ((), ())),
                         preferred_element_type=f32) + br_ref[...]
    e_iota = lax.broadcasted_iota(i32, lg.shape, 0)
    vals, idxs = [], []
    hits = jnp.zeros(lg.shape, f32)
    for _ in range(TOP_K):
        m = jnp.max(lg, axis=0, keepdims=True)
        idx = jnp.min(jnp.where(lg == m, e_iota, N_EXPERTS), axis=0, keepdims=True)
        hit = e_iota == idx
        hits = hits + jnp.where(hit, 1.0, 0.0)
        lg = jnp.where(hit, -jnp.inf, lg)
        vals.append(m)
        idxs.append(idx)
    ex = [jnp.exp(v - vals[0]) for v in vals]
    den = ex[0] + ex[1] + ex[2] + ex[3]
    return (jnp.concatenate(idxs, axis=0), jnp.concatenate([e / den for e in ex], axis=0), hits)


def _mix_body(x_ref, mod_ref, gpre_ref, gpost_ref, u_ref, vv_ref, ya_ref, ws_ref, bs_ref,
              wa_ref, wb_ref, wg_ref, bg_ref, wo_ref, gffn_ref, wr_ref, br_ref,
              o_ref, h2_ref, ti_ref, gt_ref, tab_ref, ybuf):
    d = D_MODEL
    x = x_ref[0]
    h = _adaln(x, gpre_ref[...], mod_ref[0, 1:2, :], mod_ref[0, 0:1, :]).astype(bf16)
    gs = _sigmoid(jnp.dot(h, wg_ref[...], preferred_element_type=f32) + bg_ref[...])

    blk = SGU_BLOCK
    row = lax.broadcasted_iota(i32, (2 * blk, blk), 0)
    colv = lax.broadcasted_iota(i32, (2 * blk, blk), 1)
    causal = colv <= jnp.bitwise_and(row, blk - 1)
    lo = lax.broadcasted_iota(i32, (blk, LANES), 1) < (LANES // 2)
    for gp in range(SGU_WIDTH // LANES):
        c0 = gp * LANES
        w2 = ws_ref[gp]
        w2 = jnp.where(causal, w2, jnp.zeros_like(w2))
        for bi in range(TM // blk):
            r0 = bi * blk
            s2 = jnp.dot(w2, vv_ref[0, r0:r0 + blk, c0:c0 + LANES], preferred_element_type=f32)
            s = jnp.where(lo, s2[:blk], s2[blk:]) + bs_ref[:, c0:c0 + LANES]
            yb = u_ref[0, r0:r0 + blk, c0:c0 + LANES].astype(f32) * s
            ybuf[r0:r0 + blk, c0:c0 + LANES] = yb.astype(bf16)

    a = jnp.dot(ya_ref[0], wa_ref[...], preferred_element_type=f32)
    bb = jnp.dot(ybuf[...], wb_ref[...], preferred_element_type=f32)
    merged = gs[:, :d] * a + gs[:, d:] * bb
    y = jnp.dot(merged.astype(bf16), wo_ref[...], preferred_element_type=f32)
    x1 = x + mod_ref[0, 2:3, :] * _rms(y, gpost_ref[...])
    o_ref[0] = x1

    hb = _adaln(x1, gffn_ref[...], mod_ref[0, 4:5, :], mod_ref[0, 3:4, :]).astype(bf16)
    h2_ref[0] = hb
    ids, gates, hits = _route(hb, wr_ref, br_ref)
    ti_ref[...] = ids
    gt_ref[...] = gates

    step = pl.program_id(0) * pl.num_programs(1) + pl.program_id(1)

    @pl.when(step == 0)
    def _():
        tab_ref[...] = jnp.zeros_like(tab_ref)

    lane = lax.broadcasted_iota(i32, tab_ref.shape, 1)
    acc = tab_ref[...]
    for j in range(TM // TS):
        cnt = jnp.sum(hits[:, j * TS:(j + 1) * TS], axis=1, keepdims=True)
        c8 = jnp.floor((cnt + (ROW_ALIGN - 1.0)) * (1.0 / ROW_ALIGN)) * ROW_ALIGN
        acc = acc + jnp.where(lane == step * (TM // TS) + j, c8, 0.0)
    tab_ref[...] = acc


def _mix(x, mod3, gpre, gpost, u, vv, ya, ws2, bsf, wa, wb, wg, bg, wo, gffn, wrt, br, ntp):
    b, s, d = x.shape
    n = b * s
    tok = lambda w: pl.BlockSpec((1, TM, w), lambda bi, i: (bi, i, 0))
    full = lambda a: pl.BlockSpec(a.shape, lambda bi, i: (0,) * a.ndim)
    lane = pl.BlockSpec((TOP_K, TM), lambda bi, i: (0, bi * (s // TM) + i))
    return pl.pallas_call(
        _mix_body,
        out_shape=(jax.ShapeDtypeStruct((b, s, d), f32),
                   jax.ShapeDtypeStruct((b, s, d), bf16),
                   jax.ShapeDtypeStruct((TOP_K, n), i32),
                   jax.ShapeDtypeStruct((TOP_K, n), f32),
                   jax.ShapeDtypeStruct((N_EXPERTS, ntp), f32)),
        grid=(b, s // TM),
        in_specs=[tok(d), pl.BlockSpec((1, 6, d), lambda bi, i: (bi, 0, 0)),
                  full(gpre), full(gpost), tok(SGU_WIDTH), tok(SGU_WIDTH), tok(ATT_WIDTH),
                  full(ws2), full(bsf), full(wa), full(wb), full(wg), full(bg), full(wo),
                  full(gffn), full(wrt), full(br)],
        out_specs=(tok(d), tok(d), lane, lane,
                   pl.BlockSpec((N_EXPERTS, ntp), lambda bi, i: (0, 0))),
        scratch_shapes=[pltpu.VMEM((TM, SGU_WIDTH), bf16)],
        compiler_params=_params(("arbitrary", "arbitrary")),
        name="mix",
    )(x, mod3, gpre, gpost, u, vv, ya, ws2, bsf, wa, wb, wg, bg, wo, gffn, wrt, br)


def _sorted_positions(ti):
    ts = ti.shape[1]
    e_iota = lax.broadcasted_iota(i32, (N_EXPERTS, ts), 0)
    upper = (lax.broadcasted_iota(i32, (ts, ts), 0) < lax.broadcasted_iota(i32, (ts, ts), 1))
    upper = jnp.where(upper, 1.0, 0.0).astype(bf16)
    hits, prefs, cnts = [], [], []
    for k in range(TOP_K):
        hit = ti[k:k + 1, :] == e_iota
        hf = jnp.where(hit, 1.0, 0.0)
        prefs.append(jnp.dot(hf.astype(bf16), upper, preferred_element_type=f32))
        cnts.append(jnp.sum(hf, axis=1, keepdims=True))
        hits.append(hit)
    total = cnts[0] + cnts[1] + cnts[2] + cnts[3]
    c8 = jnp.floor((total + (ROW_ALIGN - 1.0)) * (1.0 / ROW_ALIGN)) * ROW_ALIGN
    lower = (lax.broadcasted_iota(i32, (N_EXPERTS, N_EXPERTS), 1)
             < lax.broadcasted_iota(i32, (N_EXPERTS, N_EXPERTS), 0))
    lower = jnp.where(lower, 1.0, 0.0).astype(bf16)
    c8b = jnp.broadcast_to(c8, (N_EXPERTS, LANES)).astype(bf16)
    start = jnp.dot(lower, c8b, preferred_element_type=f32)[:, 0:1]
    pos = []
    for k in range(TOP_K):
        pe = start + prefs[k]
        pos.append(jnp.sum(jnp.where(hits[k], pe, 0.0), axis=0, keepdims=True).astype(i32))
        start = start + cnts[k]
    return pos


def _pack_halves(x):
    half = x.shape[1] // 2
    lo = lax.bitcast_convert_type(x[:, :half], u32)
    hi = lax.bitcast_convert_type(x[:, half:], u32)
    return jnp.bitwise_or(lax.shift_right_logical(lo, jnp.uint32(16)),
                          jnp.bitwise_and(hi, jnp.uint32(0xFFFF0000)))


def _unpack_halves(w):
    lo = lax.bitcast_convert_type(lax.shift_left(w, jnp.uint32(16)), f32)
    hi = lax.bitcast_convert_type(jnp.bitwise_and(w, jnp.uint32(0xFFFF0000)), f32)
    return jnp.concatenate([lo, hi], axis=1).astype(bf16)


def _start_runs(i, toff_ref, dst_ref, c8_ref, make):
    def one(e, carry):
        j = i * N_EXPERTS + e
        n = pl.multiple_of(c8_ref[j], ROW_ALIGN)
        so = pl.multiple_of(toff_ref[j], ROW_ALIGN)
        do = pl.multiple_of(dst_ref[j], ROW_ALIGN)

        @pl.when(n > 0)
        def _():
            make(so, do, n).start()
        return carry
    lax.fori_loop(0, N_EXPERTS, one, 0)


def _tile_rows(i, toff_ref, c8_ref):
    j = i * N_EXPERTS + (N_EXPERTS - 1)
    return pl.multiple_of(toff_ref[j] + c8_ref[j], ROW_ALIGN)


def _dispatch_body(toff_ref, dst_ref, c8_ref, tstart_ref, tlen_ref, nu_ref,
                   h2_ref, ti_ref, xb_ref, sbuf, zbuf, sems, sem):
    i = pl.program_id(0)
    base = lax.rem(i, 2) * TG
    r = lax.broadcasted_iota(i32, (RS, TS), 0)
    for j in range(TG):
        p = _sorted_positions(ti_ref[:, j * TS:(j + 1) * TS])
        pm = jnp.where(r == p[0], 1.0, jnp.where(r == p[1], 1.0,
             jnp.where(r == p[2], 1.0, jnp.where(r == p[3], 1.0, 0.0))))
        srt = jnp.dot(pm.astype(bf16), h2_ref[j * TS:(j + 1) * TS, :], preferred_element_type=f32)
        sbuf[base + j] = _pack_halves(srt)

    for j in range(TG):
        def make(so, do, n, sl=base + j):
            return pltpu.make_async_copy(sbuf.at[sl, pl.ds(so, n)], xb_ref.at[pl.ds(do, n)],
                                         sems.at[sl])
        _start_runs(i * TG + j, toff_ref, dst_ref, c8_ref, make)

    def wait_tile(t, sl):
        n = _tile_rows(t, toff_ref, c8_ref)
        pltpu.make_async_copy(sbuf.at[sl, pl.ds(0, n)], xb_ref.at[pl.ds(0, n)], sems.at[sl]).wait()

    @pl.when(i > 0)
    def _():
        for j in range(TG):
            wait_tile((i - 1) * TG + j, TG - base + j)

    @pl.when(i == pl.num_programs(0) - 1)
    def _():
        for j in range(TG):
            wait_tile(i * TG + j, base + j)
        zbuf[...] = jnp.zeros_like(zbuf)

        def fill(action):
            def tail(e, carry):
                n = pl.multiple_of(tlen_ref[e], ROW_ALIGN)
                do = pl.multiple_of(tstart_ref[e], ROW_ALIGN)

                @pl.when(n > 0)
                def _():
                    action(pltpu.make_async_copy(zbuf.at[pl.ds(0, n)], xb_ref.at[pl.ds(do, n)], sem))
                return carry
            lax.fori_loop(0, N_EXPERTS, tail, 0)

            def unused(b, carry):
                do = pl.multiple_of(b * BM, BM)
                action(pltpu.make_async_copy(zbuf, xb_ref.at[pl.ds(do, BM)], sem))
                return carry
            lax.fori_loop(nu_ref[0], xb_ref.shape[0] // BM, unused, 0)
        fill(lambda c: c.start())
        fill(lambda c: c.wait())


def _dispatch(toff, dst, c8s, tstart, tlen, n_used, h2, ti, p_rows):
    n, d = h2.shape
    return pl.pallas_call(
        _dispatch_body,
        out_shape=jax.ShapeDtypeStruct((p_rows, d // 2), u32),
        grid_spec=pltpu.PrefetchScalarGridSpec(
            num_scalar_prefetch=6,
            grid=(n // (TG * TS),),
            in_specs=[pl.BlockSpec((TG * TS, d), lambda i, *_: (i, 0)),
                      pl.BlockSpec((TOP_K, TG * TS), lambda i, *_: (0, i))],
            out_specs=pl.BlockSpec(memory_space=pl.ANY),
            scratch_shapes=[pltpu.VMEM((2 * TG, RS, d // 2), u32), pltpu.VMEM((BM, d // 2), u32),
                            pltpu.SemaphoreType.DMA((2 * TG,)), pltpu.SemaphoreType.DMA(())]),
        compiler_params=_params(("arbitrary",)),
        name="dispatch",
    )(toff, dst, c8s, tstart, tlen, n_used, h2, ti)


def _expert_body(be_ref, nxt_ref, nu_ref, x_ref, wgu_hbm, bgu_ref, wd_hbm, bd_ref, o_ref,
                 wgu_st, wd_st, wgu_bf, wd_bf, sems):
    b = pl.program_id(0)
    d = D_MODEL
    e = be_ref[b]

    def weight_copies(ex):
        return (pltpu.make_async_copy(wgu_hbm.at[ex], wgu_st, sems.at[0]),
                pltpu.make_async_copy(wd_hbm.at[ex], wd_st, sems.at[1]))

    @pl.when(b >= nu_ref[0])
    def _():
        o_ref[...] = jnp.zeros_like(o_ref)

    @pl.when(b < nu_ref[0])
    def _():
        @pl.when(b == 0)
        def _():
            for c in weight_copies(e):
                c.start()

        @pl.when(jnp.logical_or(b == 0, e != be_ref[jnp.maximum(b - 1, 0)]))
        def _():
            for c in weight_copies(e):
                c.wait()
            wgu_bf[...] = wgu_st[...].astype(bf16)
            wd_bf[...] = wd_st[...].astype(bf16)
            nxt = nxt_ref[e]

            @pl.when(nxt >= 0)
            def _():
                for c in weight_copies(nxt):
                    c.start()

        gu = jnp.dot(_unpack_halves(x_ref[...]), wgu_bf[...], preferred_element_type=f32) + bgu_ref[0]
        glu = jnp.minimum(gu[:, :d], SWIGLU_LIMIT)
        lin = jnp.clip(gu[:, d:], -SWIGLU_LIMIT, SWIGLU_LIMIT)
        act = glu * _sigmoid(SWIGLU_ALPHA * glu) * (lin + 1.0)
        o_ref[...] = jnp.dot(act.astype(bf16), wd_bf[...], preferred_element_type=f32) + bd_ref[0]


def _experts(blk_e, nxt_e, n_used, xb, wgu, bgu, wd, bd):
    p_rows, d = xb.shape[0], wd.shape[2]
    rows = lambda w: pl.BlockSpec((BM, w), lambda b, be, nx, nu: (b, 0))
    per_e = lambda a: pl.BlockSpec((1,) + a.shape[1:], lambda b, be, nx, nu: (be[b], 0, 0))
    hbm = pl.BlockSpec(memory_space=pl.ANY)
    return pl.pallas_call(
        _expert_body,
        out_shape=jax.ShapeDtypeStruct((p_rows, d), f32),
        grid_spec=pltpu.PrefetchScalarGridSpec(
            num_scalar_prefetch=3,
            grid=(p_rows // BM,),
            in_specs=[rows(xb.shape[1]), hbm, per_e(bgu), hbm, per_e(bd)],
            out_specs=rows(d),
            scratch_shapes=[pltpu.VMEM(wgu.shape[1:], f32), pltpu.VMEM(wd.shape[1:], f32),
                            pltpu.VMEM(wgu.shape[1:], bf16), pltpu.VMEM(wd.shape[1:], bf16),
                            pltpu.SemaphoreType.DMA((2,))]),
        compiler_params=_params(("arbitrary",)),
        name="experts",
    )(blk_e, nxt_e, n_used, xb, wgu, bgu, wd, bd)


def _combine_body(toff_ref, dst_ref, c8_ref, yb_ref, ti_ref, gt_ref, x1_ref, mod_ref, g_ref,
                  o_ref, sbuf, sems):
    i = pl.program_id(0)
    base = lax.rem(i, 2) * TG

    def fetch(step, first_slot):
        for j in range(TG):
            def make(so, do, n, sl=first_slot + j):
                return pltpu.make_async_copy(yb_ref.at[pl.ds(do, n)], sbuf.at[sl, pl.ds(so, n)],
                                             sems.at[sl])
            _start_runs(step * TG + j, toff_ref, dst_ref, c8_ref, make)

    @pl.when(i == 0)
    def _():
        sbuf[...] = jnp.zeros_like(sbuf)
        fetch(i, base)

    @pl.when(i + 1 < pl.num_programs(0))
    def _():
        fetch(i + 1, TG - base)

    r = lax.broadcasted_iota(i32, (RS, TS), 0)
    pms, grows = [], []
    for j in range(TG):
        p = _sorted_positions(ti_ref[:, j * TS:(j + 1) * TS])
        gt = gt_ref[:, j * TS:(j + 1) * TS]
        m = [r == p[k] for k in range(TOP_K)]
        pm = jnp.where(m[0], 1.0, jnp.where(m[1], 1.0, jnp.where(m[2], 1.0, jnp.where(m[3], 1.0, 0.0))))
        gm = jnp.where(m[0], gt[0:1, :], jnp.where(m[1], gt[1:2, :],
             jnp.where(m[2], gt[2:3, :], jnp.where(m[3], gt[3:4, :], 0.0))))
        pms.append(pm.astype(bf16))
        grows.append(jnp.sum(gm, axis=1, keepdims=True))

    for j in range(TG):
        n = _tile_rows(i * TG + j, toff_ref, c8_ref)
        pltpu.make_async_copy(yb_ref.at[pl.ds(0, n)], sbuf.at[base + j, pl.ds(0, n)],
                              sems.at[base + j]).wait()
    for j in range(TG):
        sg = (sbuf[base + j] * grows[j]).astype(bf16)
        y = lax.dot_general(pms[j], sg, (((0,), (0,)), ((), ())), preferred_element_type=f32)
        rows = slice(j * TS, (j + 1) * TS)
        o_ref[rows, :] = x1_ref[rows, :] + mod_ref[0, 5:6, :] * _rms(y, g_ref[...])


def _combine(toff, dst, c8s, yb, ti, gt, x1, mod3, g, tiles_per_batch):
    n, d = x1.shape
    tok = pl.BlockSpec((TG * TS, d), lambda i, *_: (i, 0))
    lane = pl.BlockSpec((TOP_K, TG * TS), lambda i, *_: (0, i))
    steps_per_batch = tiles_per_batch // TG
    return pl.pallas_call(
        _combine_body,
        out_shape=jax.ShapeDtypeStruct((n, d), f32),
        grid_spec=pltpu.PrefetchScalarGridSpec(
            num_scalar_prefetch=3,
            grid=(n // (TG * TS),),
            in_specs=[pl.BlockSpec(memory_space=pl.ANY), lane, lane, tok,
                      pl.BlockSpec((1, 6, d), lambda i, *_: (i // steps_per_batch, 0, 0)),
                      pl.BlockSpec(g.shape, lambda i, *_: (0, 0))],
            out_specs=tok,
            scratch_shapes=[pltpu.VMEM((2 * TG, RS, d), f32), pltpu.SemaphoreType.DMA((2 * TG,))]),
        compiler_params=_params(("arbitrary",)),
        name="combine",
    )(toff, dst, c8s, yb, ti, gt, x1, mod3, g)


def _layout_tables(tab, nt, p_rows):
    c8 = tab[:, :nt].T.astype(i32)
    toff = jnp.cumsum(c8, axis=1) - c8
    len8 = jnp.sum(c8, axis=0)
    seg = (len8 + BM - 1) // BM * BM
    gend = jnp.cumsum(seg)
    gstart = gend - seg
    dst = jnp.cumsum(c8, axis=0) - c8 + gstart[None, :]
    n_used = gend[-1] // BM
    blk = jnp.arange(p_rows // BM, dtype=i32)
    last = jnp.minimum(blk, n_used - 1)
    blk_e = jnp.sum((gend[None, :] <= (last * BM)[:, None]).astype(i32), axis=1)
    blk_e = jnp.minimum(blk_e, N_EXPERTS - 1)
    ids = jnp.arange(N_EXPERTS, dtype=i32)
    later = jnp.logical_and(ids[None, :] > ids[:, None], (seg > 0)[None, :])
    nxt_e = jnp.min(jnp.where(later, ids[None, :], N_EXPERTS), axis=1)
    nxt_e = jnp.where(nxt_e == N_EXPERTS, -1, nxt_e).astype(i32)
    return (toff.reshape(-1), dst.reshape(-1), c8.reshape(-1), gstart + len8, seg - len8,
            blk_e, nxt_e, n_used.reshape(1).astype(i32))


def kernel(x, c, w_ada, b_ada, g_pre_mix, g_post_mix, w_in, rel_bias, sgu_ln_g, sgu_ln_b,
           w_spatial, b_spatial, w_branch_a, w_branch_b, w_gate, b_gate, w_out,
           g_pre_ffn, g_post_ffn, w_router, b_router, w_gate_up, b_gate_up, w_down, b_down):
    b, s, d = x.shape
    assert d == D_MODEL and s % TM == 0 and (b * s) % TS == 0 and s % TS == 0
    n = b * s
    nt = n // TS
    ntp = -(-nt // LANES) * LANES
    p_rows = -(-(n * TOP_K + nt * N_EXPERTS * (ROW_ALIGN - 1) + N_EXPERTS * (BM - 1)) // BM) * BM
    depth = w_ada.shape[0]
    c8 = jnp.pad(c, ((0, 8 - b), (0, 0)))
    row = lambda a: a.reshape(1, -1)

    for l in range(depth):
        mod = _ada(c8, w_ada[l], row(b_ada[l]))[:b]
        mod3 = mod.reshape(b, 6, d)

        q, k, v, u, vv = _proj(x, mod3, row(g_pre_mix[l]), w_in[l].astype(bf16),
                               row(sgu_ln_g[l]), row(sgu_ln_b[l]))
        ya = _attn(q, k, v, _attn_bias(rel_bias[l]))
        ws2 = w_spatial[l].astype(bf16).reshape(-1, 2 * SGU_BLOCK, SGU_BLOCK)
        bsf = jnp.repeat(b_spatial[l].T, SGU_WIDTH // b_spatial.shape[1], axis=1)
        x1, h2, ti, gt, tab = _mix(x, mod3, row(g_pre_mix[l]), row(g_post_mix[l]), u, vv, ya, ws2, bsf,
                                   w_branch_a[l].astype(bf16), w_branch_b[l].astype(bf16),
                                   w_gate[l].astype(bf16), row(b_gate[l]), w_out[l].astype(bf16),
                                   row(g_pre_ffn[l]), w_router[l].T.astype(bf16),
                                   b_router[l].reshape(-1, 1), ntp)
        x1f = x1.reshape(n, d)
        h2 = h2.reshape(n, d)
        toff, dst, c8s, tstart, tlen, blk_e, nxt_e, n_used = _layout_tables(tab, nt, p_rows)
        xb = _dispatch(toff, dst, c8s, tstart, tlen, n_used, h2, ti, p_rows)
        yb = _experts(blk_e, nxt_e, n_used, xb, w_gate_up[l], b_gate_up[l][:, None, :],
                      w_down[l], b_down[l][:, None, :])
        x = _combine(toff, dst, c8s, yb, ti, gt, x1f, mod3, row(g_post_ffn[l]), s // TS).reshape(b, s, d)
    return x
```

```python
import functools

import jax
import jax.numpy as jnp
from jax import lax
from jax.experimental import pallas as pl
from jax.experimental.pallas import tpu as pltpu

bf16 = jnp.bfloat16
f32 = jnp.float32
i32 = jnp.int32
u32 = jnp.uint32

D_MODEL = 1024
CHUNK = 64
N_LEFT = 8
ATT_HEADS = 8
HEAD_DIM = 64
ATT_WIDTH = 512
MAX_REL = 128
SGU_BLOCK = 128
SGU_WIDTH = 512
N_EXPERTS = 32
TOP_K = 4
SWIGLU_LIMIT = 7.0
SWIGLU_ALPHA = 1.702
EPS = 1e-6
NEG = -1e30
LOG2E = 1.4426950408889634

LANES = 128
ROW_ALIGN = 8
TM = 512
QG = 2 * CHUNK
KBAND = (N_LEFT + 2) * CHUNK
TS = 256
TG = 2
RS = TS * TOP_K + N_EXPERTS * ROW_ALIGN
BM = 512
VMEM_LIMIT = 56 * 2**20


def _params(sem):
    return pltpu.CompilerParams(dimension_semantics=sem, vmem_limit_bytes=VMEM_LIMIT)


def _adaln(x, g, sc, sh):
    ms = jnp.mean(x * x, axis=-1, keepdims=True)
    return (x * lax.rsqrt(ms + EPS) * g) * (1.0 + sc) + sh


def _rms(x, g):
    ms = jnp.mean(x * x, axis=-1, keepdims=True)
    return x * lax.rsqrt(ms + EPS) * g


def _sigmoid(x):
    return 1.0 / (1.0 + jnp.exp(-x))


def _ada_body(c_ref, w_ref, b_ref, o_ref):
    c = c_ref[...]
    ca = c * _sigmoid(c)
    o_ref[...] = jnp.dot(ca.astype(bf16), w_ref[...].astype(bf16),
                         preferred_element_type=f32) + b_ref[...]


def _ada(c8, w, b):
    d = w.shape[0]
    n = w.shape[1] // d
    return pl.pallas_call(
        _ada_body,
        out_shape=jax.ShapeDtypeStruct((8, n * d), f32),
        grid=(n,),
        in_specs=[pl.BlockSpec((8, d), lambda j: (0, 0)),
                  pl.BlockSpec((d, d), lambda j: (0, j)),
                  pl.BlockSpec((1, d), lambda j: (0, j))],
        out_specs=pl.BlockSpec((8, d), lambda j: (0, j)),
        compiler_params=_params(("arbitrary",)),
        name="ada",
    )(c8, w, b)


def _proj_body(x_ref, mod_ref, g_ref, w_ref, lng_ref, lnb_ref,
               q_ref, k_ref, v_ref, u_ref, vv_ref):
    h = _adaln(x_ref[0], g_ref[...], mod_ref[0, 1:2, :], mod_ref[0, 0:1, :])
    p = jnp.dot(h.astype(bf16), w_ref[...], preferred_element_type=f32)
    aw = ATT_WIDTH
    q_ref[0] = (p[:, 0:aw] * (HEAD_DIM ** -0.5 * LOG2E)).astype(bf16)
    k_ref[0] = p[:, aw:2 * aw].astype(bf16)
    v_ref[0] = p[:, 2 * aw:3 * aw].astype(bf16)
    z = p[:, 3 * aw:]
    zg = 0.5 * z * (1.0 + lax.erf(z * (2.0 ** -0.5)))
    u_ref[0] = zg[:, :SGU_WIDTH].astype(bf16)
    vv = zg[:, SGU_WIDTH:]
    mu = jnp.mean(vv, axis=-1, keepdims=True)
    var = jnp.mean(jnp.square(vv - mu), axis=-1, keepdims=True)
    vn = (vv - mu) * lax.rsqrt(var + EPS) * lng_ref[...] + lnb_ref[...]
    vv_ref[0] = vn.astype(bf16)


def _proj(x, mod3, g, w_in, lng, lnb):
    b, s, d = x.shape
    tok = lambda w: pl.BlockSpec((1, TM, w), lambda bi, i: (bi, i, 0))
    full = lambda a: pl.BlockSpec(a.shape, lambda bi, i: (0,) * a.ndim)
    o512 = jax.ShapeDtypeStruct((b, s, ATT_WIDTH), bf16)
    return pl.pallas_call(
        _proj_body,
        out_shape=(o512,) * 5,
        grid=(b, s // TM),
        in_specs=[tok(d), pl.BlockSpec((1, 6, d), lambda bi, i: (bi, 0, 0)),
                  full(g), full(w_in), full(lng), full(lnb)],
        out_specs=(tok(ATT_WIDTH),) * 5,
        compiler_params=_params(("parallel", "arbitrary")),
        name="proj",
    )(x, mod3, g, w_in, lng, lnb)


def _attn_body(q_ref, kp_ref, kc_ref, vp_ref, vc_ref, bias_ref, o_ref, kbuf, vbuf):
    first = pl.program_id(1) == 0
    nhp = ATT_HEADS // 2
    ones = jnp.ones((TM, LANES), bf16)

    def fill(rows, k_src, v_src):
        kbuf[rows, :] = k_src[0]
        for hp in range(nhp):
            vbuf[rows, 2 * hp * LANES:(2 * hp + 1) * LANES] = v_src[0, :, hp * LANES:(hp + 1) * LANES]
            vbuf[rows, (2 * hp + 1) * LANES:(2 * hp + 2) * LANES] = ones

    fill(slice(TM, 2 * TM), kc_ref, vc_ref)

    @pl.when(first)
    def _():
        kbuf[0:TM, :] = jnp.zeros((TM, kbuf.shape[1]), bf16)
        vbuf[0:TM, :] = jnp.zeros((TM, vbuf.shape[1]), bf16)

    @pl.when(jnp.logical_not(first))
    def _():
        fill(slice(0, TM), kp_ref, vp_ref)

    lo = lax.broadcasted_iota(i32, (QG, LANES), 1) < HEAD_DIM

    def group(p, carry):
        r0 = pl.multiple_of(p * QG, QG)
        for hp in range(nhp):
            c0 = hp * LANES
            qp = q_ref[0, pl.ds(r0, QG), c0:c0 + LANES]
            zero = jnp.zeros_like(qp)
            q2 = jnp.concatenate([jnp.where(lo, qp, zero), jnp.where(lo, zero, qp)], axis=0)
            kb = kbuf[pl.ds(r0, KBAND), c0:c0 + LANES]
            s = lax.dot_general(q2, kb, (((1,), (1,)), ((), ())), preferred_element_type=f32)
            sb = (s + bias_ref[hp]).astype(bf16)
            m = jnp.max(sb, axis=-1, keepdims=True)
            e = jnp.exp2(sb - m)
            vb = vbuf[pl.ds(r0, KBAND), 2 * c0:2 * c0 + 2 * LANES]
            o2 = jnp.dot(e, vb, preferred_element_type=f32)
            on = o2[:, :LANES] / o2[:, LANES:]
            o = jnp.where(lo, on[:QG], on[QG:])
            o_ref[0, pl.ds(r0, QG), c0:c0 + LANES] = o.astype(bf16)
        return carry

    lax.fori_loop(0, TM // QG, group, 0, unroll=True)


def _attn(q, k, v, bias2):
    b, s, w = q.shape
    cur = pl.BlockSpec((1, TM, w), lambda bi, i: (bi, i, 0))
    prev = pl.BlockSpec((1, TM, w), lambda bi, i: (bi, jnp.maximum(i - 1, 0), 0))
    return pl.pallas_call(
        _attn_body,
        out_shape=jax.ShapeDtypeStruct((b, s, w), bf16),
        grid=(b, s // TM),
        in_specs=[cur, prev, cur, prev, cur,
                  pl.BlockSpec(bias2.shape, lambda bi, i: (0, 0, 0))],
        out_specs=cur,
        scratch_shapes=[pltpu.VMEM((2 * TM, w), bf16), pltpu.VMEM((2 * TM, 2 * w), bf16)],
        compiler_params=_params(("parallel", "arbitrary")),
        name="attn",
    )(q, k, k, v, v, bias2)


def _attn_bias(rel_bias):
    h = rel_bias.shape[0]
    period = 1024
    assert period >= QG + KBAND - 1 and KBAND - 2 * MAX_REL >= 0
    far = jnp.broadcast_to(rel_bias[:, 2 * MAX_REL:], (h, period))
    near = rel_bias[:, :0:-1]
    v = jnp.concatenate([far[:, :KBAND - 2 * MAX_REL], near, far[:, KBAND:]], axis=1).astype(f32)
    flat = jnp.broadcast_to(v[:, None, :], (h, QG, period)).reshape(h, QG * period)
    toep = flat[:, :QG * (period - 1)].reshape(h, QG, period - 1)[:, :, :KBAND]
    i = jnp.arange(QG, dtype=i32)[:, None]
    j = jnp.arange(KBAND, dtype=i32)[None, :]
    jb = j - (i // CHUNK) * CHUNK
    valid = jnp.logical_and(jb >= 0, jb < CHUNK * (N_LEFT + 1))
    bias = jnp.where(valid[None], toep * LOG2E, NEG)
    return bias.reshape(ATT_HEADS // 2, 2 * QG, KBAND)


def _route(hb, wr_ref, br_ref):
    lg = lax.dot_general(wr_ref[...], hb, (((1,), (1,)), ((), ())),
                         preferred_element_type=f32) + br_ref[...]
    e_iota = lax.broadcasted_iota(i32, lg.shape, 0)
    vals, idxs = [], []
    hits = jnp.zeros(lg.shape, f32)
    for _ in range(TOP_K):
        m = jnp.max(lg, axis=0, keepdims=True)
        idx = jnp.min(jnp.where(lg == m, e_iota, N_EXPERTS), axis=0, keepdims=True)
        hit = e_iota == idx
        hits = hits + jnp.where(hit, 1.0, 0.0)
        lg = jnp.where(hit, -jnp.inf, lg)
        vals.append(m)
        idxs.append(idx)
    ex = [jnp.exp(v - vals[0]) for v in vals]
    den = ex[0] + ex[1] + ex[2] + ex[3]
    return (jnp.concatenate(idxs, axis=0), jnp.concatenate([e / den for e in ex], axis=0), hits)


def _mix_body(x_ref, mod_ref, gpre_ref, gpost_ref, u_ref, vv_ref, ya_ref, ws_ref, bs_ref,
              wa_ref, wb_ref, wg_ref, bg_ref, wo_ref, gffn_ref, wr_ref, br_ref,
              o_ref, h2_ref, ti_ref, gt_ref, tab_ref, ybuf):
    d = D_MODEL
    x = x_ref[0]
    h = _adaln(x, gpre_ref[...], mod_ref[0, 1:2, :], mod_ref[0, 0:1, :]).astype(bf16)
    gs = _sigmoid(jnp.dot(h, wg_ref[...], preferred_element_type=f32) + bg_ref[...])

    blk = SGU_BLOCK
    row = lax.broadcasted_iota(i32, (2 * blk, blk), 0)
    colv = lax.broadcasted_iota(i32, (2 * blk, blk), 1)
    causal = colv <= jnp.bitwise_and(row, blk - 1)
    lo = lax.broadcasted_iota(i32, (blk, LANES), 1) < (LANES // 2)
    for gp in range(SGU_WIDTH // LANES):
        c0 = gp * LANES
        w2 = ws_ref[gp]
        w2 = jnp.where(causal, w2, jnp.zeros_like(w2))
        for bi in range(TM // blk):
            r0 = bi * blk
            s2 = jnp.dot(w2, vv_ref[0, r0:r0 + blk, c0:c0 + LANES], preferred_element_type=f32)
            s = jnp.where(lo, s2[:blk], s2[blk:]) + bs_ref[:, c0:c0 + LANES]
            yb = u_ref[0, r0:r0 + blk, c0:c0 + LANES].astype(f32) * s
            ybuf[r0:r0 + blk, c0:c0 + LANES] = yb.astype(bf16)

    a = jnp.dot(ya_ref[0], wa_ref[...], preferred_element_type=f32)
    bb = jnp.dot(ybuf[...], wb_ref[...], preferred_element_type=f32)
    merged = gs[:, :d] * a + gs[:, d:] * bb
    y = jnp.dot(merged.astype(bf16), wo_ref[...], preferred_element_type=f32)
    x1 = x + mod_ref[0, 2:3, :] * _rms(y, gpost_ref[...])
    o_ref[0] = x1

    hb = _adaln(x1, gffn_ref[...], mod_ref[0, 4:5, :], mod_ref[0, 3:4, :]).astype(bf16)
    h2_ref[0] = hb
    ids, gates, hits = _route(hb, wr_ref, br_ref)
    ti_ref[...] = ids
    gt_ref[...] = gates

    step = pl.program_id(0) * pl.num_programs(1) + pl.program_id(1)

    @pl.when(step == 0)
    def _():
        tab_ref[...] = jnp.zeros_like(tab_ref)

    lane = lax.broadcasted_iota(i32, tab_ref.shape, 1)
    acc = tab_ref[...]
    for j in range(TM // TS):
        cnt = jnp.sum(hits[:, j * TS:(j + 1) * TS], axis=1, keepdims=True)
        c8 = jnp.floor((cnt + (ROW_ALIGN - 1.0)) * (1.0 / ROW_ALIGN)) * ROW_ALIGN
        acc = acc + jnp.where(lane == step * (TM // TS) + j, c8, 0.0)
    tab_ref[...] = acc


def _mix(x, mod3, gpre, gpost, u, vv, ya, ws2, bsf, wa, wb, wg, bg, wo, gffn, wrt, br, ntp):
    b, s, d = x.shape
    n = b * s
    tok = lambda w: pl.BlockSpec((1, TM, w), lambda bi, i: (bi, i, 0))
    full = lambda a: pl.BlockSpec(a.shape, lambda bi, i: (0,) * a.ndim)
    lane = pl.BlockSpec((TOP_K, TM), lambda bi, i: (0, bi * (s // TM) + i))
    return pl.pallas_call(
        _mix_body,
        out_shape=(jax.ShapeDtypeStruct((b, s, d), f32),
                   jax.ShapeDtypeStruct((b, s, d), bf16),
                   jax.ShapeDtypeStruct((TOP_K, n), i32),
                   jax.ShapeDtypeStruct((TOP_K, n), f32),
                   jax.ShapeDtypeStruct((N_EXPERTS, ntp), f32)),
        grid=(b, s // TM),
        in_specs=[tok(d), pl.BlockSpec((1, 6, d), lambda bi, i: (bi, 0, 0)),
                  full(gpre), full(gpost), tok(SGU_WIDTH), tok(SGU_WIDTH), tok(ATT_WIDTH),
                  full(ws2), full(bsf), full(wa), full(wb), full(wg), full(bg), full(wo),
                  full(gffn), full(wrt), full(br)],
        out_specs=(tok(d), tok(d), lane, lane,
                   pl.BlockSpec((N_EXPERTS, ntp), lambda bi, i: (0, 0))),
        scratch_shapes=[pltpu.VMEM((TM, SGU_WIDTH), bf16)],
        compiler_params=_params(("arbitrary", "arbitrary")),
        name="mix",
    )(x, mod3, gpre, gpost, u, vv, ya, ws2, bsf, wa, wb, wg, bg, wo, gffn, wrt, br)


def _sorted_positions(ti):
    ts = ti.shape[1]
    e_iota = lax.broadcasted_iota(i32, (N_EXPERTS, ts), 0)
    upper = (lax.broadcasted_iota(i32, (ts, ts), 0) < lax.broadcasted_iota(i32, (ts, ts), 1))
    upper = jnp.where(upper, 1.0, 0.0).astype(bf16)
    hits, prefs, cnts = [], [], []
    for k in range(TOP_K):
        hit = ti[k:k + 1, :] == e_iota
        hf = jnp.where(hit, 1.0, 0.0)
        prefs.append(jnp.dot(hf.astype(bf16), upper, preferred_element_type=f32))
        cnts.append(jnp.sum(hf, axis=1, keepdims=True))
        hits.append(hit)
    total = cnts[0] + cnts[1] + cnts[2] + cnts[3]
    c8 = jnp.floor((total + (ROW_ALIGN - 1.0)) * (1.0 / ROW_ALIGN)) * ROW_ALIGN
    lower = (lax.broadcasted_iota(i32, (N_EXPERTS, N_EXPERTS), 1)
             < lax.broadcasted_iota(i32, (N_EXPERTS, N_EXPERTS), 0))
    lower = jnp.where(lower, 1.0, 0.0).astype(bf16)
    c8b = jnp.broadcast_to(c8, (N_EXPERTS, LANES)).astype(bf16)
    start = jnp.dot(lower, c8b, preferred_element_type=f32)[:, 0:1]
    pos = []
    for k in range(TOP_K):
        pe = start + prefs[k]
        pos.append(jnp.sum(jnp.where(hits[k], pe, 0.0), axis=0, keepdims=True).astype(i32))
        start = start + cnts[k]
    return pos


def _pack_halves(x):
    half = x.shape[1] // 2
    lo = lax.bitcast_convert_type(x[:, :half], u32)
    hi = lax.bitcast_convert_type(x[:, half:], u32)
    return jnp.bitwise_or(lax.shift_right_logical(lo, jnp.uint32(16)),
                          jnp.bitwise_and(hi, jnp.uint32(0xFFFF0000)))


def _unpack_halves(w):
    lo = lax.bitcast_convert_type(lax.shift_left(w, jnp.uint32(16)), f32)
    hi = lax.bitcast_convert_type(jnp.bitwise_and(w, jnp.uint32(0xFFFF0000)), f32)
    return jnp.concatenate([lo, hi], axis=1).astype(bf16)


def _start_runs(i, toff_ref, dst_ref, c8_ref, make):
    def one(e, carry):
        j = i * N_EXPERTS + e
        n = pl.multiple_of(c8_ref[j], ROW_ALIGN)
        so = pl.multiple_of(toff_ref[j], ROW_ALIGN)
        do = pl.multiple_of(dst_ref[j], ROW_ALIGN)

        @pl.when(n > 0)
        def _():
            make(so, do, n).start()
        return carry
    lax.fori_loop(0, N_EXPERTS, one, 0)


def _tile_rows(i, toff_ref, c8_ref):
    j = i * N_EXPERTS + (N_EXPERTS - 1)
    return pl.multiple_of(toff_ref[j] + c8_ref[j], ROW_ALIGN)


def _dispatch_body(toff_ref, dst_ref, c8_ref, tstart_ref, tlen_ref, nu_ref,
                   h2_ref, ti_ref, xb_ref, sbuf, zbuf, sems, sem):
    i = pl.program_id(0)
    base = lax.rem(i, 2) * TG
    r = lax.broadcasted_iota(i32, (RS, TS), 0)
    for j in range(TG):
        p = _sorted_positions(ti_ref[:, j * TS:(j + 1) * TS])
        pm = jnp.where(r == p[0], 1.0, jnp.where(r == p[1], 1.0,
             jnp.where(r == p[2], 1.0, jnp.where(r == p[3], 1.0, 0.0))))
        srt = jnp.dot(pm.astype(bf16), h2_ref[j * TS:(j + 1) * TS, :], preferred_element_type=f32)
        sbuf[base + j] = _pack_halves(srt)

    for j in range(TG):
        def make(so, do, n, sl=base + j):
            return pltpu.make_async_copy(sbuf.at[sl, pl.ds(so, n)], xb_ref.at[pl.ds(do, n)],
                                         sems.at[sl])
        _start_runs(i * TG + j, toff_ref, dst_ref, c8_ref, make)

    def wait_tile(t, sl):
        n = _tile_rows(t, toff_ref, c8_ref)
        pltpu.make_async_copy(sbuf.at[sl, pl.ds(0, n)], xb_ref.at[pl.ds(0, n)], sems.at[sl]).wait()

    @pl.when(i > 0)
    def _():
        for j in range(TG):
            wait_tile((i - 1) * TG + j, TG - base + j)

    @pl.when(i == pl.num_programs(0) - 1)
    def _():
        for j in range(TG):
            wait_tile(i * TG + j, base + j)
        zbuf[...] = jnp.zeros_like(zbuf)

        def fill(action):
            def tail(e, carry):
                n = pl.multiple_of(tlen_ref[e], ROW_ALIGN)
                do = pl.multiple_of(tstart_ref[e], ROW_ALIGN)

                @pl.when(n > 0)
                def _():
                    action(pltpu.make_async_copy(zbuf.at[pl.ds(0, n)], xb_ref.at[pl.ds(do, n)], sem))
                return carry
            lax.fori_loop(0, N_EXPERTS, tail, 0)

            def unused(b, carry):
                do = pl.multiple_of(b * BM, BM)
                action(pltpu.make_async_copy(zbuf, xb_ref.at[pl.ds(do, BM)], sem))
                return carry
            lax.fori_loop(nu_ref[0], xb_ref.shape[0] // BM, unused, 0)
        fill(lambda c: c.start())
        fill(lambda c: c.wait())


def _dispatch(toff, dst, c8s, tstart, tlen, n_used, h2, ti, p_rows):
    n, d = h2.shape
    return pl.pallas_call(
        _dispatch_body,
        out_shape=jax.ShapeDtypeStruct((p_rows, d // 2), u32),
        grid_spec=pltpu.PrefetchScalarGridSpec(
            num_scalar_prefetch=6,
            grid=(n // (TG * TS),),
            in_specs=[pl.BlockSpec((TG * TS, d), lambda i, *_: (i, 0)),
                      pl.BlockSpec((TOP_K, TG * TS), lambda i, *_: (0, i))],
            out_specs=pl.BlockSpec(memory_space=pl.ANY),
            scratch_shapes=[pltpu.VMEM((2 * TG, RS, d // 2), u32), pltpu.VMEM((BM, d // 2), u32),
                            pltpu.SemaphoreType.DMA((2 * TG,)), pltpu.SemaphoreType.DMA(())]),
        compiler_params=_params(("arbitrary",)),
        name="dispatch",
    )(toff, dst, c8s, tstart, tlen, n_used, h2, ti)


def _expert_body(be_ref, nxt_ref, nu_ref, x_ref, wgu_hbm, bgu_ref, wd_hbm, bd_ref, o_ref,
                 wgu_st, wd_st, wgu_bf, wd_bf, sems):
    b = pl.program_id(0)
    d = D_MODEL
    e = be_ref[b]

    def weight_copies(ex):
        return (pltpu.make_async_copy(wgu_hbm.at[ex], wgu_st, sems.at[0]),
                pltpu.make_async_copy(wd_hbm.at[ex], wd_st, sems.at[1]))

    @pl.when(b >= nu_ref[0])
    def _():
        o_ref[...] = jnp.zeros_like(o_ref)

    @pl.when(b < nu_ref[0])
    def _():
        @pl.when(b == 0)
        def _():
            for c in weight_copies(e):
                c.start()

        @pl.when(jnp.logical_or(b == 0, e != be_ref[jnp.maximum(b - 1, 0)]))
        def _():
            for c in weight_copies(e):
                c.wait()
            wgu_bf[...] = wgu_st[...].astype(bf16)
            wd_bf[...] = wd_st[...].astype(bf16)
            nxt = nxt_ref[e]

            @pl.when(nxt >= 0)
            def _():
                for c in weight_copies(nxt):
                    c.start(priority=1)

        gu = jnp.dot(_unpack_halves(x_ref[...]), wgu_bf[...], preferred_element_type=f32) + bgu_ref[0]
        glu = jnp.minimum(gu[:, :d], SWIGLU_LIMIT)
        lin = jnp.clip(gu[:, d:], -SWIGLU_LIMIT, SWIGLU_LIMIT)
        act = glu * _sigmoid(SWIGLU_ALPHA * glu) * (lin + 1.0)
        o_ref[...] = jnp.dot(act.astype(bf16), wd_bf[...], preferred_element_type=f32) + bd_ref[0]


def _experts(blk_e, nxt_e, n_used, xb, wgu, bgu, wd, bd):
    p_rows, d = xb.shape[0], wd.shape[2]
    rows = lambda w: pl.BlockSpec((BM, w), lambda b, be, nx, nu: (b, 0))
    per_e = lambda a: pl.BlockSpec((1,) + a.shape[1:], lambda b, be, nx, nu: (be[b], 0, 0))
    hbm = pl.BlockSpec(memory_space=pl.ANY)
    return pl.pallas_call(
        _expert_body,
        out_shape=jax.ShapeDtypeStruct((p_rows, d), f32),
        grid_spec=pltpu.PrefetchScalarGridSpec(
            num_scalar_prefetch=3,
            grid=(p_rows // BM,),
            in_specs=[rows(xb.shape[1]), hbm, per_e(bgu), hbm, per_e(bd)],
            out_specs=rows(d),
            scratch_shapes=[pltpu.VMEM(wgu.shape[1:], f32), pltpu.VMEM(wd.shape[1:], f32),
                            pltpu.VMEM(wgu.shape[1:], bf16), pltpu.VMEM(wd.shape[1:], bf16),
                            pltpu.SemaphoreType.DMA((2,))]),
        compiler_params=_params(("arbitrary",)),
        name="experts",
    )(blk_e, nxt_e, n_used, xb, wgu, bgu, wd, bd)


def _combine_body(toff_ref, dst_ref, c8_ref, yb_ref, ti_ref, gt_ref, x1_ref, mod_ref, g_ref,
                  o_ref, sbuf, sems):
    i = pl.program_id(0)
    base = lax.rem(i, 2) * TG

    def fetch(step, first_slot):
        for j in range(TG):
            def make(so, do, n, sl=first_slot + j):
                return pltpu.make_async_copy(yb_ref.at[pl.ds(do, n)], sbuf.at[sl, pl.ds(so, n)],
                                             sems.at[sl])
            _start_runs(step * TG + j, toff_ref, dst_ref, c8_ref, make)

    @pl.when(i == 0)
    def _():
        sbuf[...] = jnp.zeros_like(sbuf)
        fetch(i, base)

    @pl.when(i + 1 < pl.num_programs(0))
    def _():
        fetch(i + 1, TG - base)

    r = lax.broadcasted_iota(i32, (RS, TS), 0)
    pms, grows = [], []
    for j in range(TG):
        p = _sorted_positions(ti_ref[:, j * TS:(j + 1) * TS])
        gt = gt_ref[:, j * TS:(j + 1) * TS]
        m = [r == p[k] for k in range(TOP_K)]
        pm = jnp.where(m[0], 1.0, jnp.where(m[1], 1.0, jnp.where(m[2], 1.0, jnp.where(m[3], 1.0, 0.0))))
        gm = jnp.where(m[0], gt[0:1, :], jnp.where(m[1], gt[1:2, :],
             jnp.where(m[2], gt[2:3, :], jnp.where(m[3], gt[3:4, :], 0.0))))
        pms.append(pm.astype(bf16))
        grows.append(jnp.sum(gm, axis=1, keepdims=True))

    for j in range(TG):
        n = _tile_rows(i * TG + j, toff_ref, c8_ref)
        pltpu.make_async_copy(yb_ref.at[pl.ds(0, n)], sbuf.at[base + j, pl.ds(0, n)],
                              sems.at[base + j]).wait()
    for j in range(TG):
        sg = (sbuf[base + j] * grows[j]).astype(bf16)
        y = lax.dot_general(pms[j], sg, (((0,), (0,)), ((), ())), preferred_element_type=f32)
        rows = slice(j * TS, (j + 1) * TS)
        o_ref[rows, :] = x1_ref[rows, :] + mod_ref[0, 5:6, :] * _rms(y, g_ref[...])


def _combine(toff, dst, c8s, yb, ti, gt, x1, mod3, g, tiles_per_batch):
    n, d = x1.shape
    tok = pl.BlockSpec((TG * TS, d), lambda i, *_: (i, 0))
    lane = pl.BlockSpec((TOP_K, TG * TS), lambda i, *_: (0, i))
    steps_per_batch = tiles_per_batch // TG
    return pl.pallas_call(
        _combine_body,
        out_shape=jax.ShapeDtypeStruct((n, d), f32),
        grid_spec=pltpu.PrefetchScalarGridSpec(
            num_scalar_prefetch=3,
            grid=(n // (TG * TS),),
            in_specs=[pl.BlockSpec(memory_space=pl.ANY), lane, lane, tok,
                      pl.BlockSpec((1, 6, d), lambda i, *_: (i // steps_per_batch, 0, 0)),
                      pl.BlockSpec(g.shape, lambda i, *_: (0, 0))],
            out_specs=tok,
            scratch_shapes=[pltpu.VMEM((2 * TG, RS, d), f32), pltpu.SemaphoreType.DMA((2 * TG,))]),
        compiler_params=_params(("arbitrary",)),
        name="combine",
    )(toff, dst, c8s, yb, ti, gt, x1, mod3, g)


def _layout_tables(tab, nt, p_rows):
    c8 = tab[:, :nt].T.astype(i32)
    toff = jnp.cumsum(c8, axis=1) - c8
    len8 = jnp.sum(c8, axis=0)
    seg = (len8 + BM - 1) // BM * BM
    gend = jnp.cumsum(seg)
    gstart = gend - seg
    dst = jnp.cumsum(c8, axis=0) - c8 + gstart[None, :]
    n_used = gend[-1] // BM
    blk = jnp.arange(p_rows // BM, dtype=i32)
    last = jnp.minimum(blk, n_used - 1)
    blk_e = jnp.sum((gend[None, :] <= (last * BM)[:, None]).astype(i32), axis=1)
    blk_e = jnp.minimum(blk_e, N_EXPERTS - 1)
    ids = jnp.arange(N_EXPERTS, dtype=i32)
    later = jnp.logical_and(ids[None, :] > ids[:, None], (seg > 0)[None, :])
    nxt_e = jnp.min(jnp.where(later, ids[None, :], N_EXPERTS), axis=1)
    nxt_e = jnp.where(nxt_e == N_EXPERTS, -1, nxt_e).astype(i32)
    return (toff.reshape(-1), dst.reshape(-1), c8.reshape(-1), gstart + len8, seg - len8,
            blk_e, nxt_e, n_used.reshape(1).astype(i32))


def kernel(x, c, w_ada, b_ada, g_pre_mix, g_post_mix, w_in, rel_bias, sgu_ln_g, sgu_ln_b,
           w_spatial, b_spatial, w_branch_a, w_branch_b, w_gate, b_gate, w_out,
           g_pre_ffn, g_post_ffn, w_router, b_router, w_gate_up, b_gate_up, w_down, b_down):
    b, s, d = x.shape
    assert d == D_MODEL and s % TM == 0 and (b * s) % TS == 0 and s % TS == 0
    n = b * s
    nt = n // TS
    ntp = -(-nt // LANES) * LANES
    p_rows = -(-(n * TOP_K + nt * N_EXPERTS * (ROW_ALIGN - 1) + N_EXPERTS * (BM - 1)) // BM) * BM
    depth = w_ada.shape[0]
    c8 = jnp.pad(c, ((0, 8 - b), (0, 0)))
    row = lambda a: a.reshape(1, -1)

    for l in range(depth):
        mod = _ada(c8, w_ada[l], row(b_ada[l]))[:b]
        mod3 = mod.reshape(b, 6, d)

        q, k, v, u, vv = _proj(x, mod3, row(g_pre_mix[l]), w_in[l].astype(bf16),
                               row(sgu_ln_g[l]), row(sgu_ln_b[l]))
        ya = _attn(q, k, v, _attn_bias(rel_bias[l]))
        ws2 = w_spatial[l].astype(bf16).reshape(-1, 2 * SGU_BLOCK, SGU_BLOCK)
        bsf = jnp.repeat(b_spatial[l].T, SGU_WIDTH // b_spatial.shape[1], axis=1)
        x1, h2, ti, gt, tab = _mix(x, mod3, row(g_pre_mix[l]), row(g_post_mix[l]), u, vv, ya, ws2, bsf,
                                   w_branch_a[l].astype(bf16), w_branch_b[l].astype(bf16),
                                   w_gate[l].astype(bf16), row(b_gate[l]), w_out[l].astype(bf16),
                                   row(g_pre_ffn[l]), w_router[l].T.astype(bf16),
                                   b_router[l].reshape(-1, 1), ntp)
        x1f = x1.reshape(n, d)
        h2 = h2.reshape(n, d)
        toff, dst, c8s, tstart, tlen, blk_e, nxt_e, n_used = _layout_tables(tab, nt, p_rows)
        xb = _dispatch(toff, dst, c8s, tstart, tlen, n_used, h2, ti, p_rows)
        yb = _experts(blk_e, nxt_e, n_used, xb, w_gate_up[l], b_gate_up[l][:, None, :],
                      w_down[l], b_down[l][:, None, :])
        x = _combine(toff, dst, c8s, yb, ti, gt, x1f, mod3, row(g_post_ffn[l]), s // TS).reshape(b, s, d)
    return x
```

```python
import functools

import jax
import jax.numpy as jnp
from jax import lax
from jax.experimental import pallas as pl
from jax.experimental.pallas import tpu as pltpu

bf16 = jnp.bfloat16
f32 = jnp.float32
i32 = jnp.int32
u32 = jnp.uint32

D_MODEL = 1024
CHUNK = 64
N_LEFT = 8
ATT_HEADS = 8
HEAD_DIM = 64
ATT_WIDTH = 512
MAX_REL = 128
SGU_BLOCK = 128
SGU_WIDTH = 512
N_EXPERTS = 32
TOP_K = 4
SWIGLU_LIMIT = 7.0
SWIGLU_ALPHA = 1.702
EPS = 1e-6
NEG = -1e30
LOG2E = 1.4426950408889634

LANES = 128
ROW_ALIGN = 8
TM = 512
QG = 2 * CHUNK
KBAND = (N_LEFT + 2) * CHUNK
TS = 256
TG = 2
RS = TS * TOP_K + N_EXPERTS * ROW_ALIGN
BM = 512
FC = 256
VMEM_LIMIT = 56 * 2**20


def _params(sem):
    return pltpu.CompilerParams(dimension_semantics=sem, vmem_limit_bytes=VMEM_LIMIT)


def _adaln(x, g, sc, sh):
    ms = jnp.mean(x * x, axis=-1, keepdims=True)
    return (x * lax.rsqrt(ms + EPS) * g) * (1.0 + sc) + sh


def _rms(x, g):
    ms = jnp.mean(x * x, axis=-1, keepdims=True)
    return x * lax.rsqrt(ms + EPS) * g


def _sigmoid(x):
    return 1.0 / (1.0 + jnp.exp(-x))


def _ada_body(c_ref, w_ref, b_ref, o_ref):
    c = c_ref[...]
    ca = c * _sigmoid(c)
    o_ref[...] = jnp.dot(ca.astype(bf16), w_ref[...].astype(bf16),
                         preferred_element_type=f32) + b_ref[...]


def _ada(c8, w, b):
    d = w.shape[0]
    n = w.shape[1] // d
    return pl.pallas_call(
        _ada_body,
        out_shape=jax.ShapeDtypeStruct((8, n * d), f32),
        grid=(n,),
        in_specs=[pl.BlockSpec((8, d), lambda j: (0, 0)),
                  pl.BlockSpec((d, d), lambda j: (0, j)),
                  pl.BlockSpec((1, d), lambda j: (0, j))],
        out_specs=pl.BlockSpec((8, d), lambda j: (0, j)),
        compiler_params=_params(("arbitrary",)),
        name="ada",
    )(c8, w, b)


def _proj_body(x_ref, mod_ref, g_ref, w_ref, lng_ref, lnb_ref,
               q_ref, k_ref, v_ref, u_ref, vv_ref):
    h = _adaln(x_ref[0], g_ref[...], mod_ref[0, 1:2, :], mod_ref[0, 0:1, :])
    p = jnp.dot(h.astype(bf16), w_ref[...], preferred_element_type=f32)
    aw = ATT_WIDTH
    q_ref[0] = (p[:, 0:aw] * (HEAD_DIM ** -0.5 * LOG2E)).astype(bf16)
    k_ref[0] = p[:, aw:2 * aw].astype(bf16)
    v_ref[0] = p[:, 2 * aw:3 * aw].astype(bf16)
    z = p[:, 3 * aw:]
    zg = 0.5 * z * (1.0 + lax.erf(z * (2.0 ** -0.5)))
    u_ref[0] = zg[:, :SGU_WIDTH].astype(bf16)
    vv = zg[:, SGU_WIDTH:]
    mu = jnp.mean(vv, axis=-1, keepdims=True)
    var = jnp.mean(jnp.square(vv - mu), axis=-1, keepdims=True)
    vn = (vv - mu) * lax.rsqrt(var + EPS) * lng_ref[...] + lnb_ref[...]
    vv_ref[0] = vn.astype(bf16)


def _proj(x, mod3, g, w_in, lng, lnb):
    b, s, d = x.shape
    tok = lambda w: pl.BlockSpec((1, TM, w), lambda bi, i: (bi, i, 0))
    full = lambda a: pl.BlockSpec(a.shape, lambda bi, i: (0,) * a.ndim)
    o512 = jax.ShapeDtypeStruct((b, s, ATT_WIDTH), bf16)
    return pl.pallas_call(
        _proj_body,
        out_shape=(o512,) * 5,
        grid=(b, s // TM),
        in_specs=[tok(d), pl.BlockSpec((1, 6, d), lambda bi, i: (bi, 0, 0)),
                  full(g), full(w_in), full(lng), full(lnb)],
        out_specs=(tok(ATT_WIDTH),) * 5,
        compiler_params=_params(("parallel", "arbitrary")),
        name="proj",
    )(x, mod3, g, w_in, lng, lnb)


def _attn_body(q_ref, kp_ref, kc_ref, vp_ref, vc_ref, bias_ref, o_ref, kbuf, vbuf):
    first = pl.program_id(1) == 0
    nhp = ATT_HEADS // 2
    ones = jnp.ones((TM, LANES), bf16)

    def fill(rows, k_src, v_src):
        kbuf[rows, :] = k_src[0]
        for hp in range(nhp):
            vbuf[rows, 2 * hp * LANES:(2 * hp + 1) * LANES] = v_src[0, :, hp * LANES:(hp + 1) * LANES]
            vbuf[rows, (2 * hp + 1) * LANES:(2 * hp + 2) * LANES] = ones

    fill(slice(TM, 2 * TM), kc_ref, vc_ref)

    @pl.when(first)
    def _():
        kbuf[0:TM, :] = jnp.zeros((TM, kbuf.shape[1]), bf16)
        vbuf[0:TM, :] = jnp.zeros((TM, vbuf.shape[1]), bf16)

    @pl.when(jnp.logical_not(first))
    def _():
        fill(slice(0, TM), kp_ref, vp_ref)

    lo = lax.broadcasted_iota(i32, (QG, LANES), 1) < HEAD_DIM

    def group(p, carry):
        r0 = pl.multiple_of(p * QG, QG)
        for hp in range(nhp):
            c0 = hp * LANES
            qp = q_ref[0, pl.ds(r0, QG), c0:c0 + LANES]
            zero = jnp.zeros_like(qp)
            q2 = jnp.concatenate([jnp.where(lo, qp, zero), jnp.where(lo, zero, qp)], axis=0)
            kb = kbuf[pl.ds(r0, KBAND), c0:c0 + LANES]
            s = lax.dot_general(q2, kb, (((1,), (1,)), ((), ())), preferred_element_type=f32)
            sb = (s + bias_ref[hp]).astype(bf16)
            m = jnp.max(sb, axis=-1, keepdims=True)
            e = jnp.exp2(sb - m)
            vb = vbuf[pl.ds(r0, KBAND), 2 * c0:2 * c0 + 2 * LANES]
            o2 = jnp.dot(e, vb, preferred_element_type=f32)
            on = o2[:, :LANES] / o2[:, LANES:]
            o = jnp.where(lo, on[:QG], on[QG:])
            o_ref[0, pl.ds(r0, QG), c0:c0 + LANES] = o.astype(bf16)
        return carry

    lax.fori_loop(0, TM // QG, group, 0, unroll=True)


def _attn(q, k, v, bias2):
    b, s, w = q.shape
    cur = pl.BlockSpec((1, TM, w), lambda bi, i: (bi, i, 0))
    prev = pl.BlockSpec((1, TM, w), lambda bi, i: (bi, jnp.maximum(i - 1, 0), 0))
    return pl.pallas_call(
        _attn_body,
        out_shape=jax.ShapeDtypeStruct((b, s, w), bf16),
        grid=(b, s // TM),
        in_specs=[cur, prev, cur, prev, cur,
                  pl.BlockSpec(bias2.shape, lambda bi, i: (0, 0, 0))],
        out_specs=cur,
        scratch_shapes=[pltpu.VMEM((2 * TM, w), bf16), pltpu.VMEM((2 * TM, 2 * w), bf16)],
        compiler_params=_params(("parallel", "arbitrary")),
        name="attn",
    )(q, k, k, v, v, bias2)


def _attn_bias(rel_bias):
    h = rel_bias.shape[0]
    period = 1024
    assert period >= QG + KBAND - 1 and KBAND - 2 * MAX_REL >= 0
    far = jnp.broadcast_to(rel_bias[:, 2 * MAX_REL:], (h, period))
    near = rel_bias[:, :0:-1]
    v = jnp.concatenate([far[:, :KBAND - 2 * MAX_REL], near, far[:, KBAND:]], axis=1).astype(f32)
    flat = jnp.broadcast_to(v[:, None, :], (h, QG, period)).reshape(h, QG * period)
    toep = flat[:, :QG * (period - 1)].reshape(h, QG, period - 1)[:, :, :KBAND]
    i = jnp.arange(QG, dtype=i32)[:, None]
    j = jnp.arange(KBAND, dtype=i32)[None, :]
    jb = j - (i // CHUNK) * CHUNK
    valid = jnp.logical_and(jb >= 0, jb < CHUNK * (N_LEFT + 1))
    bias = jnp.where(valid[None], toep * LOG2E, NEG)
    return bias.reshape(ATT_HEADS // 2, 2 * QG, KBAND)


def _route(hb, wr_ref, br_ref):
    lg = lax.dot_general(wr_ref[...], hb, (((1,), (1,)), ((), ())),
                         preferred_element_type=f32) + br_ref[...]
    e_iota = lax.broadcasted_iota(i32, lg.shape, 0)
    vals, idxs = [], []
    hits = jnp.zeros(lg.shape, f32)
    for _ in range(TOP_K):
        m = jnp.max(lg, axis=0, keepdims=True)
        idx = jnp.min(jnp.where(lg == m, e_iota, N_EXPERTS), axis=0, keepdims=True)
        hit = e_iota == idx
        hits = hits + jnp.where(hit, 1.0, 0.0)
        lg = jnp.where(hit, -jnp.inf, lg)
        vals.append(m)
        idxs.append(idx)
    ex = [jnp.exp(v - vals[0]) for v in vals]
    den = ex[0] + ex[1] + ex[2] + ex[3]
    return (jnp.concatenate(idxs, axis=0), jnp.concatenate([e / den for e in ex], axis=0), hits)


def _mix_body(x_ref, mod_ref, gpre_ref, gpost_ref, u_ref, vv_ref, ya_ref, ws_ref, bs_ref,
              wa_ref, wb_ref, wg_ref, bg_ref, wo_ref, gffn_ref, wr_ref, br_ref,
              o_ref, h2_ref, ti_ref, gt_ref, tab_ref, ybuf):
    d = D_MODEL
    x = x_ref[0]
    h = _adaln(x, gpre_ref[...], mod_ref[0, 1:2, :], mod_ref[0, 0:1, :]).astype(bf16)
    gs = _sigmoid(jnp.dot(h, wg_ref[...], preferred_element_type=f32) + bg_ref[...])

    blk = SGU_BLOCK
    row = lax.broadcasted_iota(i32, (2 * blk, blk), 0)
    colv = lax.broadcasted_iota(i32, (2 * blk, blk), 1)
    causal = colv <= jnp.bitwise_and(row, blk - 1)
    lo = lax.broadcasted_iota(i32, (blk, LANES), 1) < (LANES // 2)
    for gp in range(SGU_WIDTH // LANES):
        c0 = gp * LANES
        w2 = ws_ref[gp]
        w2 = jnp.where(causal, w2, jnp.zeros_like(w2))
        for bi in range(TM // blk):
            r0 = bi * blk
            s2 = jnp.dot(w2, vv_ref[0, r0:r0 + blk, c0:c0 + LANES], preferred_element_type=f32)
            s = jnp.where(lo, s2[:blk], s2[blk:]) + bs_ref[:, c0:c0 + LANES]
            yb = u_ref[0, r0:r0 + blk, c0:c0 + LANES].astype(f32) * s
            ybuf[r0:r0 + blk, c0:c0 + LANES] = yb.astype(bf16)

    a = jnp.dot(ya_ref[0], wa_ref[...], preferred_element_type=f32)
    bb = jnp.dot(ybuf[...], wb_ref[...], preferred_element_type=f32)
    merged = gs[:, :d] * a + gs[:, d:] * bb
    y = jnp.dot(merged.astype(bf16), wo_ref[...], preferred_element_type=f32)
    x1 = x + mod_ref[0, 2:3, :] * _rms(y, gpost_ref[...])
    o_ref[0] = x1

    hb = _adaln(x1, gffn_ref[...], mod_ref[0, 4:5, :], mod_ref[0, 3:4, :]).astype(bf16)
    h2_ref[0] = hb
    ids, gates, hits = _route(hb, wr_ref, br_ref)
    ti_ref[...] = ids
    gt_ref[...] = gates

    step = pl.program_id(0) * pl.num_programs(1) + pl.program_id(1)

    @pl.when(step == 0)
    def _():
        tab_ref[...] = jnp.zeros_like(tab_ref)

    lane = lax.broadcasted_iota(i32, tab_ref.shape, 1)
    acc = tab_ref[...]
    for j in range(TM // TS):
        cnt = jnp.sum(hits[:, j * TS:(j + 1) * TS], axis=1, keepdims=True)
        c8 = jnp.floor((cnt + (ROW_ALIGN - 1.0)) * (1.0 / ROW_ALIGN)) * ROW_ALIGN
        acc = acc + jnp.where(lane == step * (TM // TS) + j, c8, 0.0)
    tab_ref[...] = acc


def _mix(x, mod3, gpre, gpost, u, vv, ya, ws2, bsf, wa, wb, wg, bg, wo, gffn, wrt, br, ntp):
    b, s, d = x.shape
    n = b * s
    tok = lambda w: pl.BlockSpec((1, TM, w), lambda bi, i: (bi, i, 0))
    full = lambda a: pl.BlockSpec(a.shape, lambda bi, i: (0,) * a.ndim)
    lane = pl.BlockSpec((TOP_K, TM), lambda bi, i: (0, bi * (s // TM) + i))
    return pl.pallas_call(
        _mix_body,
        out_shape=(jax.ShapeDtypeStruct((b, s, d), f32),
                   jax.ShapeDtypeStruct((b, s, d), bf16),
                   jax.ShapeDtypeStruct((TOP_K, n), i32),
                   jax.ShapeDtypeStruct((TOP_K, n), f32),
                   jax.ShapeDtypeStruct((N_EXPERTS, ntp), f32)),
        grid=(b, s // TM),
        in_specs=[tok(d), pl.BlockSpec((1, 6, d), lambda bi, i: (bi, 0, 0)),
                  full(gpre), full(gpost), tok(SGU_WIDTH), tok(SGU_WIDTH), tok(ATT_WIDTH),
                  full(ws2), full(bsf), full(wa), full(wb), full(wg), full(bg), full(wo),
                  full(gffn), full(wrt), full(br)],
        out_specs=(tok(d), tok(d), lane, lane,
                   pl.BlockSpec((N_EXPERTS, ntp), lambda bi, i: (0, 0))),
        scratch_shapes=[pltpu.VMEM((TM, SGU_WIDTH), bf16)],
        compiler_params=_params(("arbitrary", "arbitrary")),
        name="mix",
    )(x, mod3, gpre, gpost, u, vv, ya, ws2, bsf, wa, wb, wg, bg, wo, gffn, wrt, br)


def _sorted_positions(ti):
    ts = ti.shape[1]
    e_iota = lax.broadcasted_iota(i32, (N_EXPERTS, ts), 0)
    upper = (lax.broadcasted_iota(i32, (ts, ts), 0) < lax.broadcasted_iota(i32, (ts, ts), 1))
    upper = jnp.where(upper, 1.0, 0.0).astype(bf16)
    hits, prefs, cnts = [], [], []
    for k in range(TOP_K):
        hit = ti[k:k + 1, :] == e_iota
        hf = jnp.where(hit, 1.0, 0.0)
        prefs.append(jnp.dot(hf.astype(bf16), upper, preferred_element_type=f32))
        cnts.append(jnp.sum(hf, axis=1, keepdims=True))
        hits.append(hit)
    total = cnts[0] + cnts[1] + cnts[2] + cnts[3]
    c8 = jnp.floor((total + (ROW_ALIGN - 1.0)) * (1.0 / ROW_ALIGN)) * ROW_ALIGN
    lower = (lax.broadcasted_iota(i32, (N_EXPERTS, N_EXPERTS), 1)
             < lax.broadcasted_iota(i32, (N_EXPERTS, N_EXPERTS), 0))
    lower = jnp.where(lower, 1.0, 0.0).astype(bf16)
    c8b = jnp.broadcast_to(c8, (N_EXPERTS, LANES)).astype(bf16)
    start = jnp.dot(lower, c8b, preferred_element_type=f32)[:, 0:1]
    pos = []
    for k in range(TOP_K):
        pe = start + prefs[k]
        pos.append(jnp.sum(jnp.where(hits[k], pe, 0.0), axis=0, keepdims=True).astype(i32))
        start = start + cnts[k]
    return pos


def _pack_halves(x):
    half = x.shape[1] // 2
    lo = lax.bitcast_convert_type(x[:, :half], u32)
    hi = lax.bitcast_convert_type(x[:, half:], u32)
    return jnp.bitwise_or(lax.shift_right_logical(lo, jnp.uint32(16)),
                          jnp.bitwise_and(hi, jnp.uint32(0xFFFF0000)))


def _unpack_halves(w):
    lo = lax.bitcast_convert_type(lax.shift_left(w, jnp.uint32(16)), f32)
    hi = lax.bitcast_convert_type(jnp.bitwise_and(w, jnp.uint32(0xFFFF0000)), f32)
    return jnp.concatenate([lo, hi], axis=1).astype(bf16)


def _start_runs(i, toff_ref, dst_ref, c8_ref, make):
    def one(e, carry):
        j = i * N_EXPERTS + e
        n = pl.multiple_of(c8_ref[j], ROW_ALIGN)
        so = pl.multiple_of(toff_ref[j], ROW_ALIGN)
        do = pl.multiple_of(dst_ref[j], ROW_ALIGN)

        @pl.when(n > 0)
        def _():
            make(so, do, n).start()
        return carry
    lax.fori_loop(0, N_EXPERTS, one, 0)


def _tile_rows(i, toff_ref, c8_ref):
    j = i * N_EXPERTS + (N_EXPERTS - 1)
    return pl.multiple_of(toff_ref[j] + c8_ref[j], ROW_ALIGN)


def _dispatch_body(toff_ref, dst_ref, c8_ref, tstart_ref, tlen_ref, nu_ref,
                   h2_ref, ti_ref, xb_ref, sbuf, zbuf, sems, sem):
    i = pl.program_id(0)
    base = lax.rem(i, 2) * TG
    r = lax.broadcasted_iota(i32, (RS, TS), 0)
    for j in range(TG):
        p = _sorted_positions(ti_ref[:, j * TS:(j + 1) * TS])
        pm = jnp.where(r == p[0], 1.0, jnp.where(r == p[1], 1.0,
             jnp.where(r == p[2], 1.0, jnp.where(r == p[3], 1.0, 0.0))))
        srt = jnp.dot(pm.astype(bf16), h2_ref[j * TS:(j + 1) * TS, :], preferred_element_type=f32)
        sbuf[base + j] = _pack_halves(srt)

    for j in range(TG):
        def make(so, do, n, sl=base + j):
            return pltpu.make_async_copy(sbuf.at[sl, pl.ds(so, n)], xb_ref.at[pl.ds(do, n)],
                                         sems.at[sl])
        _start_runs(i * TG + j, toff_ref, dst_ref, c8_ref, make)

    def wait_tile(t, sl):
        n = _tile_rows(t, toff_ref, c8_ref)
        pltpu.make_async_copy(sbuf.at[sl, pl.ds(0, n)], xb_ref.at[pl.ds(0, n)], sems.at[sl]).wait()

    @pl.when(i > 0)
    def _():
        for j in range(TG):
            wait_tile((i - 1) * TG + j, TG - base + j)

    @pl.when(i == pl.num_programs(0) - 1)
    def _():
        for j in range(TG):
            wait_tile(i * TG + j, base + j)
        zbuf[...] = jnp.zeros_like(zbuf)

        def fill(action):
            def tail(e, carry):
                n = pl.multiple_of(tlen_ref[e], ROW_ALIGN)
                do = pl.multiple_of(tstart_ref[e], ROW_ALIGN)

                @pl.when(n > 0)
                def _():
                    action(pltpu.make_async_copy(zbuf.at[pl.ds(0, n)], xb_ref.at[pl.ds(do, n)], sem))
                return carry
            lax.fori_loop(0, N_EXPERTS, tail, 0)

            def unused(b, carry):
                do = pl.multiple_of(b * BM, BM)
                action(pltpu.make_async_copy(zbuf, xb_ref.at[pl.ds(do, BM)], sem))
                return carry
            lax.fori_loop(nu_ref[0], xb_ref.shape[0] // BM, unused, 0)
        fill(lambda c: c.start())
        fill(lambda c: c.wait())


def _dispatch(toff, dst, c8s, tstart, tlen, n_used, h2, ti, p_rows):
    n, d = h2.shape
    return pl.pallas_call(
        _dispatch_body,
        out_shape=jax.ShapeDtypeStruct((p_rows, d // 2), u32),
        grid_spec=pltpu.PrefetchScalarGridSpec(
            num_scalar_prefetch=6,
            grid=(n // (TG * TS),),
            in_specs=[pl.BlockSpec((TG * TS, d), lambda i, *_: (i, 0)),
                      pl.BlockSpec((TOP_K, TG * TS), lambda i, *_: (0, i))],
            out_specs=pl.BlockSpec(memory_space=pl.ANY),
            scratch_shapes=[pltpu.VMEM((2 * TG, RS, d // 2), u32), pltpu.VMEM((BM, d // 2), u32),
                            pltpu.SemaphoreType.DMA((2 * TG,)), pltpu.SemaphoreType.DMA(())]),
        compiler_params=_params(("arbitrary",)),
        name="dispatch",
    )(toff, dst, c8s, tstart, tlen, n_used, h2, ti)


def _expert_body(be_ref, nxt_ref, nu_ref, x_ref, wgu_hbm, bgu_ref, wd_hbm, bd_ref, o_ref,
                 wgu_st, wd_st, wgu_bf, wd_bf, abuf, sems):
    b = pl.program_id(0)
    d = D_MODEL
    e = be_ref[b]

    def weight_copies(ex):
        return (pltpu.make_async_copy(wgu_hbm.at[ex], wgu_st, sems.at[0]),
                pltpu.make_async_copy(wd_hbm.at[ex], wd_st, sems.at[1]))

    @pl.when(b >= nu_ref[0])
    def _():
        o_ref[...] = jnp.zeros_like(o_ref)

    @pl.when(b < nu_ref[0])
    def _():
        @pl.when(b == 0)
        def _():
            for c in weight_copies(e):
                c.start()

        @pl.when(jnp.logical_or(b == 0, e != be_ref[jnp.maximum(b - 1, 0)]))
        def _():
            for c in weight_copies(e):
                c.wait()
            wgu_bf[...] = wgu_st[...].astype(bf16)
            wd_bf[...] = wd_st[...].astype(bf16)
            nxt = nxt_ref[e]

            @pl.when(nxt >= 0)
            def _():
                for c in weight_copies(nxt):
                    c.start()

        xb = _unpack_halves(x_ref[...])
        for j in range(d // FC):
            cg = slice(j * FC, (j + 1) * FC)
            cl = slice(d + j * FC, d + (j + 1) * FC)
            g = jnp.dot(xb, wgu_bf[:, cg], preferred_element_type=f32) + bgu_ref[0, :, cg]
            u = jnp.dot(xb, wgu_bf[:, cl], preferred_element_type=f32) + bgu_ref[0, :, cl]
            glu = jnp.minimum(g, SWIGLU_LIMIT)
            lin = jnp.clip(u, -SWIGLU_LIMIT, SWIGLU_LIMIT)
            abuf[:, cg] = (glu * _sigmoid(SWIGLU_ALPHA * glu) * (lin + 1.0)).astype(bf16)
        o_ref[...] = jnp.dot(abuf[...], wd_bf[...], preferred_element_type=f32) + bd_ref[0]


def _experts(blk_e, nxt_e, n_used, xb, wgu, bgu, wd, bd):
    p_rows, d = xb.shape[0], wd.shape[2]
    rows = lambda w: pl.BlockSpec((BM, w), lambda b, be, nx, nu: (b, 0))
    per_e = lambda a: pl.BlockSpec((1,) + a.shape[1:], lambda b, be, nx, nu: (be[b], 0, 0))
    hbm = pl.BlockSpec(memory_space=pl.ANY)
    return pl.pallas_call(
        _expert_body,
        out_shape=jax.ShapeDtypeStruct((p_rows, d), f32),
        grid_spec=pltpu.PrefetchScalarGridSpec(
            num_scalar_prefetch=3,
            grid=(p_rows // BM,),
            in_specs=[rows(xb.shape[1]), hbm, per_e(bgu), hbm, per_e(bd)],
            out_specs=rows(d),
            scratch_shapes=[pltpu.VMEM(wgu.shape[1:], f32), pltpu.VMEM(wd.shape[1:], f32),
                            pltpu.VMEM(wgu.shape[1:], bf16), pltpu.VMEM(wd.shape[1:], bf16),
                            pltpu.VMEM((BM, d), bf16),
                            pltpu.SemaphoreType.DMA((2,))]),
        compiler_params=_params(("arbitrary",)),
        name="experts",
    )(blk_e, nxt_e, n_used, xb, wgu, bgu, wd, bd)


def _combine_body(toff_ref, dst_ref, c8_ref, yb_ref, ti_ref, gt_ref, x1_ref, mod_ref, g_ref,
                  o_ref, sbuf, sems):
    i = pl.program_id(0)
    base = lax.rem(i, 2) * TG

    def fetch(step, first_slot):
        for j in range(TG):
            def make(so, do, n, sl=first_slot + j):
                return pltpu.make_async_copy(yb_ref.at[pl.ds(do, n)], sbuf.at[sl, pl.ds(so, n)],
                                             sems.at[sl])
            _start_runs(step * TG + j, toff_ref, dst_ref, c8_ref, make)

    @pl.when(i == 0)
    def _():
        sbuf[...] = jnp.zeros_like(sbuf)
        fetch(i, base)

    @pl.when(i + 1 < pl.num_programs(0))
    def _():
        fetch(i + 1, TG - base)

    r = lax.broadcasted_iota(i32, (RS, TS), 0)
    pms, grows = [], []
    for j in range(TG):
        p = _sorted_positions(ti_ref[:, j * TS:(j + 1) * TS])
        gt = gt_ref[:, j * TS:(j + 1) * TS]
        m = [r == p[k] for k in range(TOP_K)]
        pm = jnp.where(m[0], 1.0, jnp.where(m[1], 1.0, jnp.where(m[2], 1.0, jnp.where(m[3], 1.0, 0.0))))
        gm = jnp.where(m[0], gt[0:1, :], jnp.where(m[1], gt[1:2, :],
             jnp.where(m[2], gt[2:3, :], jnp.where(m[3], gt[3:4, :], 0.0))))
        pms.append(pm.astype(bf16))
        grows.append(jnp.sum(gm, axis=1, keepdims=True))

    for j in range(TG):
        n = _tile_rows(i * TG + j, toff_ref, c8_ref)
        pltpu.make_async_copy(yb_ref.at[pl.ds(0, n)], sbuf.at[base + j, pl.ds(0, n)],
                              sems.at[base + j]).wait()
    for j in range(TG):
        sg = (sbuf[base + j] * grows[j]).astype(bf16)
        y = lax.dot_general(pms[j], sg, (((0,), (0,)), ((), ())), preferred_element_type=f32)
        rows = slice(j * TS, (j + 1) * TS)
        o_ref[rows, :] = x1_ref[rows, :] + mod_ref[0, 5:6, :] * _rms(y, g_ref[...])


def _combine(toff, dst, c8s, yb, ti, gt, x1, mod3, g, tiles_per_batch):
    n, d = x1.shape
    tok = pl.BlockSpec((TG * TS, d), lambda i, *_: (i, 0))
    lane = pl.BlockSpec((TOP_K, TG * TS), lambda i, *_: (0, i))
    steps_per_batch = tiles_per_batch // TG
    return pl.pallas_call(
        _combine_body,
        out_shape=jax.ShapeDtypeStruct((n, d), f32),
        grid_spec=pltpu.PrefetchScalarGridSpec(
            num_scalar_prefetch=3,
            grid=(n // (TG * TS),),
            in_specs=[pl.BlockSpec(memory_space=pl.ANY), lane, lane, tok,
                      pl.BlockSpec((1, 6, d), lambda i, *_: (i // steps_per_batch, 0, 0)),
                      pl.BlockSpec(g.shape, lambda i, *_: (0, 0))],
            out_specs=tok,
            scratch_shapes=[pltpu.VMEM((2 * TG, RS, d), f32), pltpu.SemaphoreType.DMA((2 * TG,))]),
        compiler_params=_params(("arbitrary",)),
        name="combine",
    )(toff, dst, c8s, yb, ti, gt, x1, mod3, g)


def _layout_tables(tab, nt, p_rows):
    c8 = tab[:, :nt].T.astype(i32)
    toff = jnp.cumsum(c8, axis=1) - c8
    len8 = jnp.sum(c8, axis=0)
    seg = (len8 + BM - 1) // BM * BM
    gend = jnp.cumsum(seg)
    gstart = gend - seg
    dst = jnp.cumsum(c8, axis=0) - c8 + gstart[None, :]
    n_used = gend[-1] // BM
    blk = jnp.arange(p_rows // BM, dtype=i32)
    last = jnp.minimum(blk, n_used - 1)
    blk_e = jnp.sum((gend[None, :] <= (last * BM)[:, None]).astype(i32), axis=1)
    blk_e = jnp.minimum(blk_e, N_EXPERTS - 1)
    ids = jnp.arange(N_EXPERTS, dtype=i32)
    later = jnp.logical_and(ids[None, :] > ids[:, None], (seg > 0)[None, :])
    nxt_e = jnp.min(jnp.where(later, ids[None, :], N_EXPERTS), axis=1)
    nxt_e = jnp.where(nxt_e == N_EXPERTS, -1, nxt_e).astype(i32)
    return (toff.reshape(-1), dst.reshape(-1), c8.reshape(-1), gstart + len8, seg - len8,
            blk_e, nxt_e, n_used.reshape(1).astype(i32))


def kernel(x, c, w_ada, b_ada, g_pre_mix, g_post_mix, w_in, rel_bias, sgu_ln_g, sgu_ln_b,
           w_spatial, b_spatial, w_branch_a, w_branch_b, w_gate, b_gate, w_out,
           g_pre_ffn, g_post_ffn, w_router, b_router, w_gate_up, b_gate_up, w_down, b_down):
    b, s, d = x.shape
    assert d == D_MODEL and s % TM == 0 and (b * s) % TS == 0 and s % TS == 0
    n = b * s
    nt = n // TS
    ntp = -(-nt // LANES) * LANES
    p_rows = -(-(n * TOP_K + nt * N_EXPERTS * (ROW_ALIGN - 1) + N_EXPERTS * (BM - 1)) // BM) * BM
    depth = w_ada.shape[0]
    c8 = jnp.pad(c, ((0, 8 - b), (0, 0)))
    row = lambda a: a.reshape(1, -1)

    for l in range(depth):
        mod = _ada(c8, w_ada[l], row(b_ada[l]))[:b]
        mod3 = mod.reshape(b, 6, d)

        q, k, v, u, vv = _proj(x, mod3, row(g_pre_mix[l]), w_in[l].astype(bf16),
                               row(sgu_ln_g[l]), row(sgu_ln_b[l]))
        ya = _attn(q, k, v, _attn_bias(rel_bias[l]))
        ws2 = w_spatial[l].astype(bf16).reshape(-1, 2 * SGU_BLOCK, SGU_BLOCK)
        bsf = jnp.repeat(b_spatial[l].T, SGU_WIDTH // b_spatial.shape[1], axis=1)
        x1, h2, ti, gt, tab = _mix(x, mod3, row(g_pre_mix[l]), row(g_post_mix[l]), u, vv, ya, ws2, bsf,
                                   w_branch_a[l].astype(bf16), w_branch_b[l].astype(bf16),
                                   w_gate[l].astype(bf16), row(b_gate[l]), w_out[l].astype(bf16),
                                   row(g_pre_ffn[l]), w_router[l].T.astype(bf16),
                                   b_router[l].reshape(-1, 1), ntp)
        x1f = x1.reshape(n, d)
        h2 = h2.reshape(n, d)
        toff, dst, c8s, tstart, tlen, blk_e, nxt_e, n_used = _layout_tables(tab, nt, p_rows)
        xb = _dispatch(toff, dst, c8s, tstart, tlen, n_used, h2, ti, p_rows)
        yb = _experts(blk_e, nxt_e, n_used, xb, w_gate_up[l], b_gate_up[l][:, None, :],
                      w_down[l], b_down[l][:, None, :])
        x = _combine(toff, dst, c8s, yb, ti, gt, x1f, mod3, row(g_post_ffn[l]), s // TS).reshape(b, s, d)
    return x
```

```python
import functools

import jax
import jax.numpy as jnp
from jax import lax
from jax.experimental import pallas as pl
from jax.experimental.pallas import tpu as pltpu

bf16 = jnp.bfloat16
f32 = jnp.float32
i32 = jnp.int32
u32 = jnp.uint32

D_MODEL = 1024
CHUNK = 64
N_LEFT = 8
ATT_HEADS = 8
HEAD_DIM = 64
ATT_WIDTH = 512
MAX_REL = 128
SGU_BLOCK = 128
SGU_WIDTH = 512
N_EXPERTS = 32
TOP_K = 4
SWIGLU_LIMIT = 7.0
SWIGLU_ALPHA = 1.702
EPS = 1e-6
NEG = -1e30
LOG2E = 1.4426950408889634

LANES = 128
ROW_ALIGN = 8
TM = 512
TMP = 512
TMX = 1024
QG = 2 * CHUNK
KBAND = (N_LEFT + 2) * CHUNK
TS = 256
TG = 2
RS = TS * TOP_K + N_EXPERTS * ROW_ALIGN
BM = 512
VMEM_LIMIT = 56 * 2**20


def _params(sem):
    return pltpu.CompilerParams(dimension_semantics=sem, vmem_limit_bytes=VMEM_LIMIT)


def _adaln(x, g, sc, sh):
    ms = jnp.mean(x * x, axis=-1, keepdims=True)
    return (x * lax.rsqrt(ms + EPS) * g) * (1.0 + sc) + sh


def _rms(x, g):
    ms = jnp.mean(x * x, axis=-1, keepdims=True)
    return x * lax.rsqrt(ms + EPS) * g


def _sigmoid(x):
    return 1.0 / (1.0 + jnp.exp(-x))


def _ada_body(c_ref, w_ref, b_ref, o_ref):
    c = c_ref[...]
    ca = c * _sigmoid(c)
    o_ref[...] = jnp.dot(ca.astype(bf16), w_ref[...].astype(bf16),
                         preferred_element_type=f32) + b_ref[...]


def _ada(c8, w, b):
    d = w.shape[0]
    n = w.shape[1] // d
    return pl.pallas_call(
        _ada_body,
        out_shape=jax.ShapeDtypeStruct((8, n * d), f32),
        grid=(n,),
        in_specs=[pl.BlockSpec((8, d), lambda j: (0, 0)),
                  pl.BlockSpec((d, d), lambda j: (0, j)),
                  pl.BlockSpec((1, d), lambda j: (0, j))],
        out_specs=pl.BlockSpec((8, d), lambda j: (0, j)),
        compiler_params=_params(("arbitrary",)),
        name="ada",
    )(c8, w, b)


def _proj_body(x_ref, mod_ref, g_ref, w_ref, lng_ref, lnb_ref,
               q_ref, k_ref, v_ref, u_ref, vv_ref):
    h = _adaln(x_ref[0], g_ref[...], mod_ref[0, 1:2, :], mod_ref[0, 0:1, :]).astype(bf16)
    aw = ATT_WIDTH
    z = jnp.dot(h, w_ref[:, 3 * aw:], preferred_element_type=f32)
    p = jnp.dot(h, w_ref[:, :3 * aw], preferred_element_type=f32)
    q_ref[0] = (p[:, 0:aw] * (HEAD_DIM ** -0.5 * LOG2E)).astype(bf16)
    k_ref[0] = p[:, aw:2 * aw].astype(bf16)
    v_ref[0] = p[:, 2 * aw:3 * aw].astype(bf16)
    zg = 0.5 * z * (1.0 + lax.erf(z * (2.0 ** -0.5)))
    u_ref[0] = zg[:, :SGU_WIDTH].astype(bf16)
    vv = zg[:, SGU_WIDTH:]
    mu = jnp.mean(vv, axis=-1, keepdims=True)
    var = jnp.mean(jnp.square(vv - mu), axis=-1, keepdims=True)
    vn = (vv - mu) * lax.rsqrt(var + EPS) * lng_ref[...] + lnb_ref[...]
    vv_ref[0] = vn.astype(bf16)


def _proj(x, mod3, g, w_in, lng, lnb):
    b, s, d = x.shape
    tok = lambda w: pl.BlockSpec((1, TMP, w), lambda bi, i: (bi, i, 0))
    full = lambda a: pl.BlockSpec(a.shape, lambda bi, i: (0,) * a.ndim)
    o512 = jax.ShapeDtypeStruct((b, s, ATT_WIDTH), bf16)
    return pl.pallas_call(
        _proj_body,
        out_shape=(o512,) * 5,
        grid=(b, s // TMP),
        in_specs=[tok(d), pl.BlockSpec((1, 6, d), lambda bi, i: (bi, 0, 0)),
                  full(g), full(w_in), full(lng), full(lnb)],
        out_specs=(tok(ATT_WIDTH),) * 5,
        compiler_params=_params(("parallel", "arbitrary")),
        name="proj",
    )(x, mod3, g, w_in, lng, lnb)


def _attn_body(q_ref, kp_ref, kc_ref, vp_ref, vc_ref, bias_ref, o_ref, kbuf, vbuf):
    first = pl.program_id(1) == 0
    nhp = ATT_HEADS // 2
    ones = jnp.ones((TM, LANES), bf16)

    def fill(rows, k_src, v_src):
        kbuf[rows, :] = k_src[0]
        for hp in range(nhp):
            vbuf[rows, 2 * hp * LANES:(2 * hp + 1) * LANES] = v_src[0, :, hp * LANES:(hp + 1) * LANES]
            vbuf[rows, (2 * hp + 1) * LANES:(2 * hp + 2) * LANES] = ones

    fill(slice(TM, 2 * TM), kc_ref, vc_ref)

    @pl.when(first)
    def _():
        kbuf[0:TM, :] = jnp.zeros((TM, kbuf.shape[1]), bf16)
        vbuf[0:TM, :] = jnp.zeros((TM, vbuf.shape[1]), bf16)

    @pl.when(jnp.logical_not(first))
    def _():
        fill(slice(0, TM), kp_ref, vp_ref)

    lo = lax.broadcasted_iota(i32, (QG, LANES), 1) < HEAD_DIM

    def group(p, carry):
        r0 = pl.multiple_of(p * QG, QG)
        for hp in range(nhp):
            c0 = hp * LANES
            qp = q_ref[0, pl.ds(r0, QG), c0:c0 + LANES]
            zero = jnp.zeros_like(qp)
            q2 = jnp.concatenate([jnp.where(lo, qp, zero), jnp.where(lo, zero, qp)], axis=0)
            kb = kbuf[pl.ds(r0, KBAND), c0:c0 + LANES]
            s = lax.dot_general(q2, kb, (((1,), (1,)), ((), ())), preferred_element_type=f32)
            sb = (s + bias_ref[hp]).astype(bf16)
            m = jnp.max(sb, axis=-1, keepdims=True)
            e = jnp.exp2(sb - m)
            vb = vbuf[pl.ds(r0, KBAND), 2 * c0:2 * c0 + 2 * LANES]
            o2 = jnp.dot(e, vb, preferred_element_type=f32)
            on = o2[:, :LANES] / o2[:, LANES:]
            o = jnp.where(lo, on[:QG], on[QG:])
            o_ref[0, pl.ds(r0, QG), c0:c0 + LANES] = o.astype(bf16)
        return carry

    lax.fori_loop(0, TM // QG, group, 0, unroll=True)


def _attn(q, k, v, bias2):
    b, s, w = q.shape
    cur = pl.BlockSpec((1, TM, w), lambda bi, i: (bi, i, 0))
    prev = pl.BlockSpec((1, TM, w), lambda bi, i: (bi, jnp.maximum(i - 1, 0), 0))
    return pl.pallas_call(
        _attn_body,
        out_shape=jax.ShapeDtypeStruct((b, s, w), bf16),
        grid=(b, s // TM),
        in_specs=[cur, prev, cur, prev, cur,
                  pl.BlockSpec(bias2.shape, lambda bi, i: (0, 0, 0))],
        out_specs=cur,
        scratch_shapes=[pltpu.VMEM((2 * TM, w), bf16), pltpu.VMEM((2 * TM, 2 * w), bf16)],
        compiler_params=_params(("parallel", "arbitrary")),
        name="attn",
    )(q, k, k, v, v, bias2)


def _attn_bias(rel_bias):
    h = rel_bias.shape[0]
    period = 1024
    assert period >= QG + KBAND - 1 and KBAND - 2 * MAX_REL >= 0
    far = jnp.broadcast_to(rel_bias[:, 2 * MAX_REL:], (h, period))
    near = rel_bias[:, :0:-1]
    v = jnp.concatenate([far[:, :KBAND - 2 * MAX_REL], near, far[:, KBAND:]], axis=1).astype(f32)
    flat = jnp.broadcast_to(v[:, None, :], (h, QG, period)).reshape(h, QG * period)
    toep = flat[:, :QG * (period - 1)].reshape(h, QG, period - 1)[:, :, :KBAND]
    i = jnp.arange(QG, dtype=i32)[:, None]
    j = jnp.arange(KBAND, dtype=i32)[None, :]
    jb = j - (i // CHUNK) * CHUNK
    valid = jnp.logical_and(jb >= 0, jb < CHUNK * (N_LEFT + 1))
    bias = jnp.where(valid[None], toep * LOG2E, NEG)
    return bias.reshape(ATT_HEADS // 2, 2 * QG, KBAND)


def _route(hb, wr_ref, br_ref):
    lg = lax.dot_general(wr_ref[...], hb, (((1,), (1,)), ((), ())),
                         preferred_element_type=f32) + br_ref[...]
    e_iota = lax.broadcasted_iota(i32, lg.shape, 0)
    vals, idxs = [], []
    hits = jnp.zeros(lg.shape, f32)
    for _ in range(TOP_K):
        m = jnp.max(lg, axis=0, keepdims=True)
        idx = jnp.min(jnp.where(lg == m, e_iota, N_EXPERTS), axis=0, keepdims=True)
        hit = e_iota == idx
        hits = hits + jnp.where(hit, 1.0, 0.0)
        lg = jnp.where(hit, -jnp.inf, lg)
        vals.append(m)
        idxs.append(idx)
    ex = [jnp.exp(v - vals[0]) for v in vals]
    den = ex[0] + ex[1] + ex[2] + ex[3]
    return (jnp.concatenate(idxs, axis=0), jnp.concatenate([e / den for e in ex], axis=0), hits)


def _mix_body(x_ref, mod_ref, gpre_ref, gpost_ref, u_ref, vv_ref, ya_ref, ws_ref, bs_ref,
              wa_ref, wb_ref, wg_ref, bg_ref, wo_ref, gffn_ref, wr_ref, br_ref,
              o_ref, h2_ref, ti_ref, gt_ref, tab_ref, ybuf):
    d = D_MODEL
    x = x_ref[0]
    h = _adaln(x, gpre_ref[...], mod_ref[0, 1:2, :], mod_ref[0, 0:1, :]).astype(bf16)
    gs = _sigmoid(jnp.dot(h, wg_ref[...], preferred_element_type=f32) + bg_ref[...])

    blk = SGU_BLOCK
    row = lax.broadcasted_iota(i32, (2 * blk, blk), 0)
    colv = lax.broadcasted_iota(i32, (2 * blk, blk), 1)
    causal = colv <= jnp.bitwise_and(row, blk - 1)
    lo = lax.broadcasted_iota(i32, (blk, LANES), 1) < (LANES // 2)
    for gp in range(SGU_WIDTH // LANES):
        c0 = gp * LANES
        w2 = ws_ref[gp]
        w2 = jnp.where(causal, w2, jnp.zeros_like(w2))
        for bi in range(TMX // blk):
            r0 = bi * blk
            s2 = jnp.dot(w2, vv_ref[0, r0:r0 + blk, c0:c0 + LANES], preferred_element_type=f32)
            s = jnp.where(lo, s2[:blk], s2[blk:]) + bs_ref[:, c0:c0 + LANES]
            yb = u_ref[0, r0:r0 + blk, c0:c0 + LANES].astype(f32) * s
            ybuf[r0:r0 + blk, c0:c0 + LANES] = yb.astype(bf16)

    a = jnp.dot(ya_ref[0], wa_ref[...], preferred_element_type=f32)
    bb = jnp.dot(ybuf[...], wb_ref[...], preferred_element_type=f32)
    merged = gs[:, :d] * a + gs[:, d:] * bb
    y = jnp.dot(merged.astype(bf16), wo_ref[...], preferred_element_type=f32)
    x1 = x + mod_ref[0, 2:3, :] * _rms(y, gpost_ref[...])
    o_ref[0] = x1

    hb = _adaln(x1, gffn_ref[...], mod_ref[0, 4:5, :], mod_ref[0, 3:4, :]).astype(bf16)
    h2_ref[0] = hb
    ids, gates, hits = _route(hb, wr_ref, br_ref)
    ti_ref[...] = ids
    gt_ref[...] = gates

    step = pl.program_id(0) * pl.num_programs(1) + pl.program_id(1)

    @pl.when(step == 0)
    def _():
        tab_ref[...] = jnp.zeros_like(tab_ref)

    lane = lax.broadcasted_iota(i32, tab_ref.shape, 1)
    acc = tab_ref[...]
    for j in range(TMX // TS):
        cnt = jnp.sum(hits[:, j * TS:(j + 1) * TS], axis=1, keepdims=True)
        c8 = jnp.floor((cnt + (ROW_ALIGN - 1.0)) * (1.0 / ROW_ALIGN)) * ROW_ALIGN
        acc = acc + jnp.where(lane == step * (TMX // TS) + j, c8, 0.0)
    tab_ref[...] = acc


def _mix(x, mod3, gpre, gpost, u, vv, ya, ws2, bsf, wa, wb, wg, bg, wo, gffn, wrt, br, ntp):
    b, s, d = x.shape
    n = b * s
    tok = lambda w: pl.BlockSpec((1, TMX, w), lambda bi, i: (bi, i, 0))
    full = lambda a: pl.BlockSpec(a.shape, lambda bi, i: (0,) * a.ndim)
    lane = pl.BlockSpec((TOP_K, TMX), lambda bi, i: (0, bi * (s // TMX) + i))
    return pl.pallas_call(
        _mix_body,
        out_shape=(jax.ShapeDtypeStruct((b, s, d), f32),
                   jax.ShapeDtypeStruct((b, s, d), bf16),
                   jax.ShapeDtypeStruct((TOP_K, n), i32),
                   jax.ShapeDtypeStruct((TOP_K, n), f32),
                   jax.ShapeDtypeStruct((N_EXPERTS, ntp), f32)),
        grid=(b, s // TMX),
        in_specs=[tok(d), pl.BlockSpec((1, 6, d), lambda bi, i: (bi, 0, 0)),
                  full(gpre), full(gpost), tok(SGU_WIDTH), tok(SGU_WIDTH), tok(ATT_WIDTH),
                  full(ws2), full(bsf), full(wa), full(wb), full(wg), full(bg), full(wo),
                  full(gffn), full(wrt), full(br)],
        out_specs=(tok(d), tok(d), lane, lane,
                   pl.BlockSpec((N_EXPERTS, ntp), lambda bi, i: (0, 0))),
        scratch_shapes=[pltpu.VMEM((TMX, SGU_WIDTH), bf16)],
        compiler_params=_params(("arbitrary", "arbitrary")),
        name="mix",
    )(x, mod3, gpre, gpost, u, vv, ya, ws2, bsf, wa, wb, wg, bg, wo, gffn, wrt, br)


def _sorted_positions(ti):
    ts = ti.shape[1]
    e_iota = lax.broadcasted_iota(i32, (N_EXPERTS, ts), 0)
    upper = (lax.broadcasted_iota(i32, (ts, ts), 0) < lax.broadcasted_iota(i32, (ts, ts), 1))
    upper = jnp.where(upper, 1.0, 0.0).astype(bf16)
    hits, prefs, cnts = [], [], []
    for k in range(TOP_K):
        hit = ti[k:k + 1, :] == e_iota
        hf = jnp.where(hit, 1.0, 0.0)
        prefs.append(jnp.dot(hf.astype(bf16), upper, preferred_element_type=f32))
        cnts.append(jnp.sum(hf, axis=1, keepdims=True))
        hits.append(hit)
    total = cnts[0] + cnts[1] + cnts[2] + cnts[3]
    c8 = jnp.floor((total + (ROW_ALIGN - 1.0)) * (1.0 / ROW_ALIGN)) * ROW_ALIGN
    lower = (lax.broadcasted_iota(i32, (N_EXPERTS, N_EXPERTS), 1)
             < lax.broadcasted_iota(i32, (N_EXPERTS, N_EXPERTS), 0))
    lower = jnp.where(lower, 1.0, 0.0).astype(bf16)
    c8b = jnp.broadcast_to(c8, (N_EXPERTS, LANES)).astype(bf16)
    start = jnp.dot(lower, c8b, preferred_element_type=f32)[:, 0:1]
    pos = []
    for k in range(TOP_K):
        pe = start + prefs[k]
        pos.append(jnp.sum(jnp.where(hits[k], pe, 0.0), axis=0, keepdims=True).astype(i32))
        start = start + cnts[k]
    return pos


def _pack_halves(x):
    half = x.shape[1] // 2
    lo = lax.bitcast_convert_type(x[:, :half], u32)
    hi = lax.bitcast_convert_type(x[:, half:], u32)
    return jnp.bitwise_or(lax.shift_right_logical(lo, jnp.uint32(16)),
                          jnp.bitwise_and(hi, jnp.uint32(0xFFFF0000)))


def _unpack_halves(w):
    lo = lax.bitcast_convert_type(lax.shift_left(w, jnp.uint32(16)), f32)
    hi = lax.bitcast_convert_type(jnp.bitwise_and(w, jnp.uint32(0xFFFF0000)), f32)
    return lo, hi


def _start_runs(i, toff_ref, dst_ref, c8_ref, make):
    def one(e, carry):
        j = i * N_EXPERTS + e
        n = pl.multiple_of(c8_ref[j], ROW_ALIGN)
        so = pl.multiple_of(toff_ref[j], ROW_ALIGN)
        do = pl.multiple_of(dst_ref[j], ROW_ALIGN)

        @pl.when(n > 0)
        def _():
            make(so, do, n).start()
        return carry
    lax.fori_loop(0, N_EXPERTS, one, 0)


def _tile_rows(i, toff_ref, c8_ref):
    j = i * N_EXPERTS + (N_EXPERTS - 1)
    return pl.multiple_of(toff_ref[j] + c8_ref[j], ROW_ALIGN)


def _dispatch_body(toff_ref, dst_ref, c8_ref, tstart_ref, tlen_ref, nu_ref,
                   h2_ref, ti_ref, xb_ref, sbuf, zbuf, sems, sem):
    i = pl.program_id(0)
    base = lax.rem(i, 2) * TG
    r = lax.broadcasted_iota(i32, (RS, TS), 0)
    for j in range(TG):
        p = _sorted_positions(ti_ref[:, j * TS:(j + 1) * TS])
        pm = jnp.where(r == p[0], 1.0, jnp.where(r == p[1], 1.0,
             jnp.where(r == p[2], 1.0, jnp.where(r == p[3], 1.0, 0.0))))
        srt = jnp.dot(pm.astype(bf16), h2_ref[j * TS:(j + 1) * TS, :], preferred_element_type=f32)
        sbuf[base + j] = _pack_halves(srt)

    for j in range(TG):
        def make(so, do, n, sl=base + j):
            return pltpu.make_async_copy(sbuf.at[sl, pl.ds(so, n)], xb_ref.at[pl.ds(do, n)],
                                         sems.at[sl])
        _start_runs(i * TG + j, toff_ref, dst_ref, c8_ref, make)

    def wait_tile(t, sl):
        n = _tile_rows(t, toff_ref, c8_ref)
        pltpu.make_async_copy(sbuf.at[sl, pl.ds(0, n)], xb_ref.at[pl.ds(0, n)], sems.at[sl]).wait()

    @pl.when(i > 0)
    def _():
        for j in range(TG):
            wait_tile((i - 1) * TG + j, TG - base + j)

    @pl.when(i == pl.num_programs(0) - 1)
    def _():
        for j in range(TG):
            wait_tile(i * TG + j, base + j)
        zbuf[...] = jnp.zeros_like(zbuf)

        def fill(action):
            def tail(e, carry):
                n = pl.multiple_of(tlen_ref[e], ROW_ALIGN)
                do = pl.multiple_of(tstart_ref[e], ROW_ALIGN)

                @pl.when(n > 0)
                def _():
                    action(pltpu.make_async_copy(zbuf.at[pl.ds(0, n)], xb_ref.at[pl.ds(do, n)], sem))
                return carry
            lax.fori_loop(0, N_EXPERTS, tail, 0)

            def unused(b, carry):
                do = pl.multiple_of(b * BM, BM)
                action(pltpu.make_async_copy(zbuf, xb_ref.at[pl.ds(do, BM)], sem))
                return carry
            lax.fori_loop(nu_ref[0], xb_ref.shape[0] // BM, unused, 0)
        fill(lambda c: c.start())
        fill(lambda c: c.wait())


def _dispatch(toff, dst, c8s, tstart, tlen, n_used, h2, ti, p_rows):
    n, d = h2.shape
    return pl.pallas_call(
        _dispatch_body,
        out_shape=jax.ShapeDtypeStruct((p_rows, d // 2), u32),
        grid_spec=pltpu.PrefetchScalarGridSpec(
            num_scalar_prefetch=6,
            grid=(n // (TG * TS),),
            in_specs=[pl.BlockSpec((TG * TS, d), lambda i, *_: (i, 0)),
                      pl.BlockSpec((TOP_K, TG * TS), lambda i, *_: (0, i))],
            out_specs=pl.BlockSpec(memory_space=pl.ANY),
            scratch_shapes=[pltpu.VMEM((2 * TG, RS, d // 2), u32), pltpu.VMEM((BM, d // 2), u32),
                            pltpu.SemaphoreType.DMA((2 * TG,)), pltpu.SemaphoreType.DMA(())]),
        compiler_params=_params(("arbitrary",)),
        name="dispatch",
    )(toff, dst, c8s, tstart, tlen, n_used, h2, ti)


def _expert_body(be_ref, nxt_ref, nu_ref, x_ref, wgu_hbm, bgu_ref, wd_hbm, bd_ref, o_ref,
                 wgu_st, wd_st, wgu_bf, wd_bf, sems):
    b = pl.program_id(0)
    d = D_MODEL
    e = be_ref[b]

    def weight_copies(ex):
        return (pltpu.make_async_copy(wgu_hbm.at[ex], wgu_st, sems.at[0]),
                pltpu.make_async_copy(wd_hbm.at[ex], wd_st, sems.at[1]))

    @pl.when(b >= nu_ref[0])
    def _():
        o_ref[...] = jnp.zeros_like(o_ref)

    @pl.when(b < nu_ref[0])
    def _():
        @pl.when(b == 0)
        def _():
            for c in weight_copies(e):
                c.start()

        @pl.when(jnp.logical_or(b == 0, e != be_ref[jnp.maximum(b - 1, 0)]))
        def _():
            for c in weight_copies(e):
                c.wait()
            wgu_bf[...] = wgu_st[...].astype(bf16)
            wd_bf[...] = wd_st[...].astype(bf16)
            nxt = nxt_ref[e]

            @pl.when(nxt >= 0)
            def _():
                for c in weight_copies(nxt):
                    c.start()

        xb = jnp.concatenate(_unpack_halves(x_ref[...]), axis=1).astype(bf16)
        gu = jnp.dot(xb, wgu_bf[...], preferred_element_type=f32) + bgu_ref[0]
        glu = jnp.minimum(gu[:, :d], SWIGLU_LIMIT)
        lin = jnp.clip(gu[:, d:], -SWIGLU_LIMIT, SWIGLU_LIMIT)
        act = glu * _sigmoid(SWIGLU_ALPHA * glu) * (lin + 1.0)
        y = jnp.dot(act.astype(bf16), wd_bf[...], preferred_element_type=f32) + bd_ref[0]
        o_ref[...] = _pack_halves(y)


def _experts(blk_e, nxt_e, n_used, xb, wgu, bgu, wd, bd):
    p_rows, d = xb.shape[0], wd.shape[2]
    rows = lambda w: pl.BlockSpec((BM, w), lambda b, be, nx, nu: (b, 0))
    per_e = lambda a: pl.BlockSpec((1,) + a.shape[1:], lambda b, be, nx, nu: (be[b], 0, 0))
    hbm = pl.BlockSpec(memory_space=pl.ANY)
    return pl.pallas_call(
        _expert_body,
        out_shape=jax.ShapeDtypeStruct((p_rows, d // 2), u32),
        grid_spec=pltpu.PrefetchScalarGridSpec(
            num_scalar_prefetch=3,
            grid=(p_rows // BM,),
            in_specs=[rows(xb.shape[1]), hbm, per_e(bgu), hbm, per_e(bd)],
            out_specs=rows(d // 2),
            scratch_shapes=[pltpu.VMEM(wgu.shape[1:], f32), pltpu.VMEM(wd.shape[1:], f32),
                            pltpu.VMEM(wgu.shape[1:], bf16), pltpu.VMEM(wd.shape[1:], bf16),
                            pltpu.SemaphoreType.DMA((2,))]),
        compiler_params=_params(("arbitrary",)),
        name="experts",
    )(blk_e, nxt_e, n_used, xb, wgu, bgu, wd, bd)


def _combine_body(toff_ref, dst_ref, c8_ref, yb_ref, ti_ref, gt_ref, x1_ref, mod_ref, g_ref,
                  o_ref, sbuf, sems):
    i = pl.program_id(0)
    base = lax.rem(i, 2) * TG

    def fetch(step, first_slot):
        for j in range(TG):
            def make(so, do, n, sl=first_slot + j):
                return pltpu.make_async_copy(yb_ref.at[pl.ds(do, n)], sbuf.at[sl, pl.ds(so, n)],
                                             sems.at[sl])
            _start_runs(step * TG + j, toff_ref, dst_ref, c8_ref, make)

    @pl.when(i == 0)
    def _():
        sbuf[...] = jnp.zeros_like(sbuf)
        fetch(i, base)

    @pl.when(i + 1 < pl.num_programs(0))
    def _():
        fetch(i + 1, TG - base)

    r = lax.broadcasted_iota(i32, (RS, TS), 0)
    pms, grows = [], []
    for j in range(TG):
        p = _sorted_positions(ti_ref[:, j * TS:(j + 1) * TS])
        gt = gt_ref[:, j * TS:(j + 1) * TS]
        m = [r == p[k] for k in range(TOP_K)]
        pm = jnp.where(m[0], 1.0, jnp.where(m[1], 1.0, jnp.where(m[2], 1.0, jnp.where(m[3], 1.0, 0.0))))
        gm = jnp.where(m[0], gt[0:1, :], jnp.where(m[1], gt[1:2, :],
             jnp.where(m[2], gt[2:3, :], jnp.where(m[3], gt[3:4, :], 0.0))))
        pms.append(pm.astype(bf16))
        grows.append(jnp.sum(gm, axis=1, keepdims=True))

    for j in range(TG):
        n = _tile_rows(i * TG + j, toff_ref, c8_ref)
        pltpu.make_async_copy(yb_ref.at[pl.ds(0, n)], sbuf.at[base + j, pl.ds(0, n)],
                              sems.at[base + j]).wait()
    for j in range(TG):
        halves = [lax.dot_general(pms[j], (h * grows[j]).astype(bf16), (((0,), (0,)), ((), ())),
                                  preferred_element_type=f32)
                  for h in _unpack_halves(sbuf[base + j])]
        y = jnp.concatenate(halves, axis=1)
        rows = slice(j * TS, (j + 1) * TS)
        o_ref[rows, :] = x1_ref[rows, :] + mod_ref[0, 5:6, :] * _rms(y, g_ref[...])


def _combine(toff, dst, c8s, yb, ti, gt, x1, mod3, g, tiles_per_batch):
    n, d = x1.shape
    tok = pl.BlockSpec((TG * TS, d), lambda i, *_: (i, 0))
    lane = pl.BlockSpec((TOP_K, TG * TS), lambda i, *_: (0, i))
    steps_per_batch = tiles_per_batch // TG
    return pl.pallas_call(
        _combine_body,
        out_shape=jax.ShapeDtypeStruct((n, d), f32),
        grid_spec=pltpu.PrefetchScalarGridSpec(
            num_scalar_prefetch=3,
            grid=(n // (TG * TS),),
            in_specs=[pl.BlockSpec(memory_space=pl.ANY), lane, lane, tok,
                      pl.BlockSpec((1, 6, d), lambda i, *_: (i // steps_per_batch, 0, 0)),
                      pl.BlockSpec(g.shape, lambda i, *_: (0, 0))],
            out_specs=tok,
            scratch_shapes=[pltpu.VMEM((2 * TG, RS, d // 2), u32), pltpu.SemaphoreType.DMA((2 * TG,))]),
        compiler_params=_params(("arbitrary",)),
        name="combine",
    )(toff, dst, c8s, yb, ti, gt, x1, mod3, g)


def _layout_tables(tab, nt, p_rows):
    c8 = tab[:, :nt].T.astype(i32)
    toff = jnp.cumsum(c8, axis=1) - c8
    len8 = jnp.sum(c8, axis=0)
    seg = (len8 + BM - 1) // BM * BM
    gend = jnp.cumsum(seg)
    gstart = gend - seg
    dst = jnp.cumsum(c8, axis=0) - c8 + gstart[None, :]
    n_used = gend[-1] // BM
    blk = jnp.arange(p_rows // BM, dtype=i32)
    last = jnp.minimum(blk, n_used - 1)
    blk_e = jnp.sum((gend[None, :] <= (last * BM)[:, None]).astype(i32), axis=1)
    blk_e = jnp.minimum(blk_e, N_EXPERTS - 1)
    ids = jnp.arange(N_EXPERTS, dtype=i32)
    later = jnp.logical_and(ids[None, :] > ids[:, None], (seg > 0)[None, :])
    nxt_e = jnp.min(jnp.where(later, ids[None, :], N_EXPERTS), axis=1)
    nxt_e = jnp.where(nxt_e == N_EXPERTS, -1, nxt_e).astype(i32)
    return (toff.reshape(-1), dst.reshape(-1), c8.reshape(-1), gstart + len8, seg - len8,
            blk_e, nxt_e, n_used.reshape(1).astype(i32))


def kernel(x, c, w_ada, b_ada, g_pre_mix, g_post_mix, w_in, rel_bias, sgu_ln_g, sgu_ln_b,
           w_spatial, b_spatial, w_branch_a, w_branch_b, w_gate, b_gate, w_out,
           g_pre_ffn, g_post_ffn, w_router, b_router, w_gate_up, b_gate_up, w_down, b_down):
    b, s, d = x.shape
    assert d == D_MODEL and s % max(TM, TMP, TMX) == 0 and s % (TG * TS) == 0
    n = b * s
    nt = n // TS
    ntp = -(-nt // LANES) * LANES
    p_rows = -(-(n * TOP_K + nt * N_EXPERTS * (ROW_ALIGN - 1) + N_EXPERTS * (BM - 1)) // BM) * BM
    depth = w_ada.shape[0]
    c8 = jnp.pad(c, ((0, 8 - b), (0, 0)))
    row = lambda a: a.reshape(1, -1)

    for l in range(depth):
        mod = _ada(c8, w_ada[l], row(b_ada[l]))[:b]
        mod3 = mod.reshape(b, 6, d)

        q, k, v, u, vv = _proj(x, mod3, row(g_pre_mix[l]), w_in[l].astype(bf16),
                               row(sgu_ln_g[l]), row(sgu_ln_b[l]))
        ya = _attn(q, k, v, _attn_bias(rel_bias[l]))
        ws2 = w_spatial[l].astype(bf16).reshape(-1, 2 * SGU_BLOCK, SGU_BLOCK)
        bsf = jnp.repeat(b_spatial[l].T, SGU_WIDTH // b_spatial.shape[1], axis=1)
        x1, h2, ti, gt, tab = _mix(x, mod3, row(g_pre_mix[l]), row(g_post_mix[l]), u, vv, ya, ws2, bsf,
                                   w_branch_a[l].astype(bf16), w_branch_b[l].astype(bf16),
                                   w_gate[l].astype(bf16), row(b_gate[l]), w_out[l].astype(bf16),
                                   row(g_pre_ffn[l]), w_router[l].T.astype(bf16),
                                   b_router[l].reshape(-1, 1), ntp)
        x1f = x1.reshape(n, d)
        h2 = h2.reshape(n, d)
        toff, dst, c8s, tstart, tlen, blk_e, nxt_e, n_used = _layout_tables(tab, nt, p_rows)
        xb = _dispatch(toff, dst, c8s, tstart, tlen, n_used, h2, ti, p_rows)
        yb = _experts(blk_e, nxt_e, n_used, xb, w_gate_up[l], b_gate_up[l][:, None, :],
                      w_down[l], b_down[l][:, None, :])
        x = _combine(toff, dst, c8s, yb, ti, gt, x1f, mod3, row(g_post_ffn[l]), s // TS).reshape(b, s, d)
    return x
```

```python
import functools

import jax
import jax.numpy as jnp
from jax import lax
from jax.experimental import pallas as pl
from jax.experimental.pallas import tpu as pltpu

bf16 = jnp.bfloat16
f32 = jnp.float32
i32 = jnp.int32
u32 = jnp.uint32

D_MODEL = 1024
CHUNK = 64
N_LEFT = 8
ATT_HEADS = 8
HEAD_DIM = 64
ATT_WIDTH = 512
MAX_REL = 128
SGU_BLOCK = 128
SGU_WIDTH = 512
N_EXPERTS = 32
TOP_K = 4
SWIGLU_LIMIT = 7.0
SWIGLU_ALPHA = 1.702
EPS = 1e-6
NEG = -1e30
LOG2E = 1.4426950408889634

LANES = 128
ROW_ALIGN = 8
TM = 512
TMP = 512
TMX = 1024
QCH = 4
QG = QCH * CHUNK
KBAND = (N_LEFT + QCH) * CHUNK
TS = 256
TG = 2
RS = TS * TOP_K + N_EXPERTS * ROW_ALIGN
BM = 512
VMEM_LIMIT = 56 * 2**20


def _params(sem):
    return pltpu.CompilerParams(dimension_semantics=sem, vmem_limit_bytes=VMEM_LIMIT)


def _adaln(x, g, sc, sh):
    ms = jnp.mean(x * x, axis=-1, keepdims=True)
    return (x * lax.rsqrt(ms + EPS) * g) * (1.0 + sc) + sh


def _rms(x, g):
    ms = jnp.mean(x * x, axis=-1, keepdims=True)
    return x * lax.rsqrt(ms + EPS) * g


def _sigmoid(x):
    return 1.0 / (1.0 + jnp.exp(-x))


def _ada_body(c_ref, w_ref, b_ref, o_ref):
    c = c_ref[...]
    ca = c * _sigmoid(c)
    o_ref[...] = jnp.dot(ca.astype(bf16), w_ref[...].astype(bf16),
                         preferred_element_type=f32) + b_ref[...]


def _ada(c8, w, b):
    d = w.shape[0]
    n = w.shape[1] // d
    return pl.pallas_call(
        _ada_body,
        out_shape=jax.ShapeDtypeStruct((8, n * d), f32),
        grid=(n,),
        in_specs=[pl.BlockSpec((8, d), lambda j: (0, 0)),
                  pl.BlockSpec((d, d), lambda j: (0, j)),
                  pl.BlockSpec((1, d), lambda j: (0, j))],
        out_specs=pl.BlockSpec((8, d), lambda j: (0, j)),
        compiler_params=_params(("arbitrary",)),
        name="ada",
    )(c8, w, b)


def _proj_body(x_ref, mod_ref, g_ref, w_ref, lng_ref, lnb_ref,
               q_ref, k_ref, v_ref, u_ref, vv_ref):
    h = _adaln(x_ref[0], g_ref[...], mod_ref[0, 1:2, :], mod_ref[0, 0:1, :]).astype(bf16)
    aw = ATT_WIDTH
    z = jnp.dot(h, w_ref[:, 3 * aw:], preferred_element_type=f32)
    p = jnp.dot(h, w_ref[:, :3 * aw], preferred_element_type=f32)
    q_ref[0] = (p[:, 0:aw] * (HEAD_DIM ** -0.5 * LOG2E)).astype(bf16)
    k_ref[0] = p[:, aw:2 * aw].astype(bf16)
    v_ref[0] = p[:, 2 * aw:3 * aw].astype(bf16)
    zg = 0.5 * z * (1.0 + lax.erf(z * (2.0 ** -0.5)))
    u_ref[0] = zg[:, :SGU_WIDTH].astype(bf16)
    vv = zg[:, SGU_WIDTH:]
    mu = jnp.mean(vv, axis=-1, keepdims=True)
    var = jnp.mean(jnp.square(vv - mu), axis=-1, keepdims=True)
    vn = (vv - mu) * lax.rsqrt(var + EPS) * lng_ref[...] + lnb_ref[...]
    vv_ref[0] = vn.astype(bf16)


def _proj(x, mod3, g, w_in, lng, lnb):
    b, s, d = x.shape
    tok = lambda w: pl.BlockSpec((1, TMP, w), lambda bi, i: (bi, i, 0))
    full = lambda a: pl.BlockSpec(a.shape, lambda bi, i: (0,) * a.ndim)
    o512 = jax.ShapeDtypeStruct((b, s, ATT_WIDTH), bf16)
    return pl.pallas_call(
        _proj_body,
        out_shape=(o512,) * 5,
        grid=(b, s // TMP),
        in_specs=[tok(d), pl.BlockSpec((1, 6, d), lambda bi, i: (bi, 0, 0)),
                  full(g), full(w_in), full(lng), full(lnb)],
        out_specs=(tok(ATT_WIDTH),) * 5,
        compiler_params=_params(("parallel", "arbitrary")),
        name="proj",
    )(x, mod3, g, w_in, lng, lnb)


def _attn_body(q_ref, kp_ref, kc_ref, vp_ref, vc_ref, bias_ref, o_ref, kbuf, vbuf):
    first = pl.program_id(1) == 0
    nhp = ATT_HEADS // 2
    ones = jnp.ones((TM, LANES), bf16)

    def fill(rows, k_src, v_src):
        kbuf[rows, :] = k_src[0]
        for hp in range(nhp):
            vbuf[rows, 2 * hp * LANES:(2 * hp + 1) * LANES] = v_src[0, :, hp * LANES:(hp + 1) * LANES]
            vbuf[rows, (2 * hp + 1) * LANES:(2 * hp + 2) * LANES] = ones

    fill(slice(TM, 2 * TM), kc_ref, vc_ref)

    @pl.when(first)
    def _():
        kbuf[0:TM, :] = jnp.zeros((TM, kbuf.shape[1]), bf16)
        vbuf[0:TM, :] = jnp.zeros((TM, vbuf.shape[1]), bf16)

    @pl.when(jnp.logical_not(first))
    def _():
        fill(slice(0, TM), kp_ref, vp_ref)

    lo = lax.broadcasted_iota(i32, (QG, LANES), 1) < HEAD_DIM

    def group(p, carry):
        r0 = pl.multiple_of(p * QG, QG)
        for hp in range(nhp):
            c0 = hp * LANES
            qp = q_ref[0, pl.ds(r0, QG), c0:c0 + LANES]
            zero = jnp.zeros_like(qp)
            q2 = jnp.concatenate([jnp.where(lo, qp, zero), jnp.where(lo, zero, qp)], axis=0)
            kb = kbuf[pl.ds(r0, KBAND), c0:c0 + LANES]
            s = lax.dot_general(q2, kb, (((1,), (1,)), ((), ())), preferred_element_type=f32)
            sb = (s + bias_ref[hp]).astype(bf16)
            m = jnp.max(sb, axis=-1, keepdims=True)
            e = jnp.exp2(sb - m)
            vb = vbuf[pl.ds(r0, KBAND), 2 * c0:2 * c0 + 2 * LANES]
            o2 = jnp.dot(e, vb, preferred_element_type=f32)
            on = o2[:, :LANES] / o2[:, LANES:]
            o = jnp.where(lo, on[:QG], on[QG:])
            o_ref[0, pl.ds(r0, QG), c0:c0 + LANES] = o.astype(bf16)
        return carry

    lax.fori_loop(0, TM // QG, group, 0, unroll=True)


def _attn(q, k, v, bias2):
    b, s, w = q.shape
    cur = pl.BlockSpec((1, TM, w), lambda bi, i: (bi, i, 0))
    prev = pl.BlockSpec((1, TM, w), lambda bi, i: (bi, jnp.maximum(i - 1, 0), 0))
    return pl.pallas_call(
        _attn_body,
        out_shape=jax.ShapeDtypeStruct((b, s, w), bf16),
        grid=(b, s // TM),
        in_specs=[cur, prev, cur, prev, cur,
                  pl.BlockSpec(bias2.shape, lambda bi, i: (0, 0, 0))],
        out_specs=cur,
        scratch_shapes=[pltpu.VMEM((2 * TM, w), bf16), pltpu.VMEM((2 * TM, 2 * w), bf16)],
        compiler_params=_params(("parallel", "arbitrary")),
        name="attn",
    )(q, k, k, v, v, bias2)


def _attn_bias(rel_bias):
    h = rel_bias.shape[0]
    period = 1024
    n_far = N_LEFT * CHUNK - MAX_REL
    assert period >= QG + KBAND - 1 and n_far >= 0
    far = jnp.broadcast_to(rel_bias[:, 2 * MAX_REL:], (h, period))
    near = rel_bias[:, :0:-1]
    v = jnp.concatenate([far[:, :n_far], near, far[:, n_far + 2 * MAX_REL:]], axis=1).astype(f32)
    flat = jnp.broadcast_to(v[:, None, :], (h, QG, period)).reshape(h, QG * period)
    toep = flat[:, :QG * (period - 1)].reshape(h, QG, period - 1)[:, :, :KBAND]
    i = jnp.arange(QG, dtype=i32)[:, None]
    j = jnp.arange(KBAND, dtype=i32)[None, :]
    jb = j - (i // CHUNK) * CHUNK
    valid = jnp.logical_and(jb >= 0, jb < CHUNK * (N_LEFT + 1))
    bias = jnp.where(valid[None], toep * LOG2E, NEG)
    return bias.reshape(ATT_HEADS // 2, 2 * QG, KBAND)


def _route(hb, wr_ref, br_ref):
    lg = lax.dot_general(wr_ref[...], hb, (((1,), (1,)), ((), ())),
                         preferred_element_type=f32) + br_ref[...]
    e_iota = lax.broadcasted_iota(i32, lg.shape, 0)
    vals, idxs = [], []
    hits = jnp.zeros(lg.shape, f32)
    for _ in range(TOP_K):
        m = jnp.max(lg, axis=0, keepdims=True)
        idx = jnp.min(jnp.where(lg == m, e_iota, N_EXPERTS), axis=0, keepdims=True)
        hit = e_iota == idx
        hits = hits + jnp.where(hit, 1.0, 0.0)
        lg = jnp.where(hit, -jnp.inf, lg)
        vals.append(m)
        idxs.append(idx)
    ex = [jnp.exp(v - vals[0]) for v in vals]
    den = ex[0] + ex[1] + ex[2] + ex[3]
    return (jnp.concatenate(idxs, axis=0), jnp.concatenate([e / den for e in ex], axis=0), hits)


def _mix_body(x_ref, mod_ref, gpre_ref, gpost_ref, u_ref, vv_ref, ya_ref, ws_ref, bs_ref,
              wa_ref, wb_ref, wg_ref, bg_ref, wo_ref, gffn_ref, wr_ref, br_ref,
              o_ref, h2_ref, ti_ref, gt_ref, tab_ref, ybuf):
    d = D_MODEL
    x = x_ref[0]
    h = _adaln(x, gpre_ref[...], mod_ref[0, 1:2, :], mod_ref[0, 0:1, :]).astype(bf16)
    gs = _sigmoid(jnp.dot(h, wg_ref[...], preferred_element_type=f32) + bg_ref[...])

    blk = SGU_BLOCK
    row = lax.broadcasted_iota(i32, (2 * blk, blk), 0)
    colv = lax.broadcasted_iota(i32, (2 * blk, blk), 1)
    causal = colv <= jnp.bitwise_and(row, blk - 1)
    lo = lax.broadcasted_iota(i32, (blk, LANES), 1) < (LANES // 2)
    for gp in range(SGU_WIDTH // LANES):
        c0 = gp * LANES
        w2 = ws_ref[gp]
        w2 = jnp.where(causal, w2, jnp.zeros_like(w2))
        for bi in range(TMX // blk):
            r0 = bi * blk
            s2 = jnp.dot(w2, vv_ref[0, r0:r0 + blk, c0:c0 + LANES], preferred_element_type=f32)
            s = jnp.where(lo, s2[:blk], s2[blk:]) + bs_ref[:, c0:c0 + LANES]
            yb = u_ref[0, r0:r0 + blk, c0:c0 + LANES].astype(f32) * s
            ybuf[r0:r0 + blk, c0:c0 + LANES] = yb.astype(bf16)

    a = jnp.dot(ya_ref[0], wa_ref[...], preferred_element_type=f32)
    bb = jnp.dot(ybuf[...], wb_ref[...], preferred_element_type=f32)
    merged = gs[:, :d] * a + gs[:, d:] * bb
    y = jnp.dot(merged.astype(bf16), wo_ref[...], preferred_element_type=f32)
    x1 = x + mod_ref[0, 2:3, :] * _rms(y, gpost_ref[...])
    o_ref[0] = x1

    hb = _adaln(x1, gffn_ref[...], mod_ref[0, 4:5, :], mod_ref[0, 3:4, :]).astype(bf16)
    h2_ref[0] = hb
    ids, gates, hits = _route(hb, wr_ref, br_ref)
    ti_ref[...] = ids
    gt_ref[...] = gates

    step = pl.program_id(0) * pl.num_programs(1) + pl.program_id(1)

    @pl.when(step == 0)
    def _():
        tab_ref[...] = jnp.zeros_like(tab_ref)

    lane = lax.broadcasted_iota(i32, tab_ref.shape, 1)
    acc = tab_ref[...]
    for j in range(TMX // TS):
        cnt = jnp.sum(hits[:, j * TS:(j + 1) * TS], axis=1, keepdims=True)
        c8 = jnp.floor((cnt + (ROW_ALIGN - 1.0)) * (1.0 / ROW_ALIGN)) * ROW_ALIGN
        acc = acc + jnp.where(lane == step * (TMX // TS) + j, c8, 0.0)
    tab_ref[...] = acc


def _mix(x, mod3, gpre, gpost, u, vv, ya, ws2, bsf, wa, wb, wg, bg, wo, gffn, wrt, br, ntp):
    b, s, d = x.shape
    n = b * s
    tok = lambda w: pl.BlockSpec((1, TMX, w), lambda bi, i: (bi, i, 0))
    full = lambda a: pl.BlockSpec(a.shape, lambda bi, i: (0,) * a.ndim)
    lane = pl.BlockSpec((TOP_K, TMX), lambda bi, i: (0, bi * (s // TMX) + i))
    return pl.pallas_call(
        _mix_body,
        out_shape=(jax.ShapeDtypeStruct((b, s, d), f32),
                   jax.ShapeDtypeStruct((b, s, d), bf16),
                   jax.ShapeDtypeStruct((TOP_K, n), i32),
                   jax.ShapeDtypeStruct((TOP_K, n), f32),
                   jax.ShapeDtypeStruct((N_EXPERTS, ntp), f32)),
        grid=(b, s // TMX),
        in_specs=[tok(d), pl.BlockSpec((1, 6, d), lambda bi, i: (bi, 0, 0)),
                  full(gpre), full(gpost), tok(SGU_WIDTH), tok(SGU_WIDTH), tok(ATT_WIDTH),
                  full(ws2), full(bsf), full(wa), full(wb), full(wg), full(bg), full(wo),
                  full(gffn), full(wrt), full(br)],
        out_specs=(tok(d), tok(d), lane, lane,
                   pl.BlockSpec((N_EXPERTS, ntp), lambda bi, i: (0, 0))),
        scratch_shapes=[pltpu.VMEM((TMX, SGU_WIDTH), bf16)],
        compiler_params=_params(("arbitrary", "arbitrary")),
        name="mix",
    )(x, mod3, gpre, gpost, u, vv, ya, ws2, bsf, wa, wb, wg, bg, wo, gffn, wrt, br)


def _sorted_positions(ti):
    ts = ti.shape[1]
    e_iota = lax.broadcasted_iota(i32, (N_EXPERTS, ts), 0)
    upper = (lax.broadcasted_iota(i32, (ts, ts), 0) < lax.broadcasted_iota(i32, (ts, ts), 1))
    upper = jnp.where(upper, 1.0, 0.0).astype(bf16)
    hits, prefs, cnts = [], [], []
    for k in range(TOP_K):
        hit = ti[k:k + 1, :] == e_iota
        hf = jnp.where(hit, 1.0, 0.0)
        prefs.append(jnp.dot(hf.astype(bf16), upper, preferred_element_type=f32))
        cnts.append(jnp.sum(hf, axis=1, keepdims=True))
        hits.append(hit)
    total = cnts[0] + cnts[1] + cnts[2] + cnts[3]
    c8 = jnp.floor((total + (ROW_ALIGN - 1.0)) * (1.0 / ROW_ALIGN)) * ROW_ALIGN
    lower = (lax.broadcasted_iota(i32, (N_EXPERTS, N_EXPERTS), 1)
             < lax.broadcasted_iota(i32, (N_EXPERTS, N_EXPERTS), 0))
    lower = jnp.where(lower, 1.0, 0.0).astype(bf16)
    c8b = jnp.broadcast_to(c8, (N_EXPERTS, LANES)).astype(bf16)
    start = jnp.dot(lower, c8b, preferred_element_type=f32)[:, 0:1]
    pos = []
    for k in range(TOP_K):
        pe = start + prefs[k]
        pos.append(jnp.sum(jnp.where(hits[k], pe, 0.0), axis=0, keepdims=True).astype(i32))
        start = start + cnts[k]
    return pos


def _pack_halves(x):
    half = x.shape[1] // 2
    lo = lax.bitcast_convert_type(x[:, :half], u32)
    hi = lax.bitcast_convert_type(x[:, half:], u32)
    return jnp.bitwise_or(lax.shift_right_logical(lo, jnp.uint32(16)),
                          jnp.bitwise_and(hi, jnp.uint32(0xFFFF0000)))


def _unpack_halves(w):
    lo = lax.bitcast_convert_type(lax.shift_left(w, jnp.uint32(16)), f32)
    hi = lax.bitcast_convert_type(jnp.bitwise_and(w, jnp.uint32(0xFFFF0000)), f32)
    return jnp.concatenate([lo, hi], axis=1).astype(bf16)


def _start_runs(i, toff_ref, dst_ref, c8_ref, make):
    def one(e, carry):
        j = i * N_EXPERTS + e
        n = pl.multiple_of(c8_ref[j], ROW_ALIGN)
        so = pl.multiple_of(toff_ref[j], ROW_ALIGN)
        do = pl.multiple_of(dst_ref[j], ROW_ALIGN)

        @pl.when(n > 0)
        def _():
            make(so, do, n).start()
        return carry
    lax.fori_loop(0, N_EXPERTS, one, 0)


def _tile_rows(i, toff_ref, c8_ref):
    j = i * N_EXPERTS + (N_EXPERTS - 1)
    return pl.multiple_of(toff_ref[j] + c8_ref[j], ROW_ALIGN)


def _dispatch_body(toff_ref, dst_ref, c8_ref, tstart_ref, tlen_ref, nu_ref,
                   h2_ref, ti_ref, xb_ref, sbuf, zbuf, sems, sem):
    i = pl.program_id(0)
    base = lax.rem(i, 2) * TG
    r = lax.broadcasted_iota(i32, (RS, TS), 0)
    for j in range(TG):
        p = _sorted_positions(ti_ref[:, j * TS:(j + 1) * TS])
        pm = jnp.where(r == p[0], 1.0, jnp.where(r == p[1], 1.0,
             jnp.where(r == p[2], 1.0, jnp.where(r == p[3], 1.0, 0.0))))
        srt = jnp.dot(pm.astype(bf16), h2_ref[j * TS:(j + 1) * TS, :], preferred_element_type=f32)
        sbuf[base + j] = _pack_halves(srt)

    for j in range(TG):
        def make(so, do, n, sl=base + j):
            return pltpu.make_async_copy(sbuf.at[sl, pl.ds(so, n)], xb_ref.at[pl.ds(do, n)],
                                         sems.at[sl])
        _start_runs(i * TG + j, toff_ref, dst_ref, c8_ref, make)

    def wait_tile(t, sl):
        n = _tile_rows(t, toff_ref, c8_ref)
        pltpu.make_async_copy(sbuf.at[sl, pl.ds(0, n)], xb_ref.at[pl.ds(0, n)], sems.at[sl]).wait()

    @pl.when(i > 0)
    def _():
        for j in range(TG):
            wait_tile((i - 1) * TG + j, TG - base + j)

    @pl.when(i == pl.num_programs(0) - 1)
    def _():
        for j in range(TG):
            wait_tile(i * TG + j, base + j)
        zbuf[...] = jnp.zeros_like(zbuf)

        def fill(action):
            def tail(e, carry):
                n = pl.multiple_of(tlen_ref[e], ROW_ALIGN)
                do = pl.multiple_of(tstart_ref[e], ROW_ALIGN)

                @pl.when(n > 0)
                def _():
                    action(pltpu.make_async_copy(zbuf.at[pl.ds(0, n)], xb_ref.at[pl.ds(do, n)], sem))
                return carry
            lax.fori_loop(0, N_EXPERTS, tail, 0)

            def unused(b, carry):
                do = pl.multiple_of(b * BM, BM)
                action(pltpu.make_async_copy(zbuf, xb_ref.at[pl.ds(do, BM)], sem))
                return carry
            lax.fori_loop(nu_ref[0], xb_ref.shape[0] // BM, unused, 0)
        fill(lambda c: c.start())
        fill(lambda c: c.wait())


def _dispatch(toff, dst, c8s, tstart, tlen, n_used, h2, ti, p_rows):
    n, d = h2.shape
    return pl.pallas_call(
        _dispatch_body,
        out_shape=jax.ShapeDtypeStruct((p_rows, d // 2), u32),
        grid_spec=pltpu.PrefetchScalarGridSpec(
            num_scalar_prefetch=6,
            grid=(n // (TG * TS),),
            in_specs=[pl.BlockSpec((TG * TS, d), lambda i, *_: (i, 0)),
                      pl.BlockSpec((TOP_K, TG * TS), lambda i, *_: (0, i))],
            out_specs=pl.BlockSpec(memory_space=pl.ANY),
            scratch_shapes=[pltpu.VMEM((2 * TG, RS, d // 2), u32), pltpu.VMEM((BM, d // 2), u32),
                            pltpu.SemaphoreType.DMA((2 * TG,)), pltpu.SemaphoreType.DMA(())]),
        compiler_params=_params(("arbitrary",)),
        name="dispatch",
    )(toff, dst, c8s, tstart, tlen, n_used, h2, ti)


def _expert_body(be_ref, nxt_ref, nu_ref, x_ref, wgu_hbm, bgu_ref, wd_hbm, bd_ref, o_ref,
                 wgu_st, wd_st, wgu_bf, wd_bf, sems):
    b = pl.program_id(0)
    d = D_MODEL
    e = be_ref[b]

    def weight_copies(ex):
        return (pltpu.make_async_copy(wgu_hbm.at[ex], wgu_st, sems.at[0]),
                pltpu.make_async_copy(wd_hbm.at[ex], wd_st, sems.at[1]))

    @pl.when(b >= nu_ref[0])
    def _():
        o_ref[...] = jnp.zeros_like(o_ref)

    @pl.when(b < nu_ref[0])
    def _():
        @pl.when(b == 0)
        def _():
            for c in weight_copies(e):
                c.start()

        @pl.when(jnp.logical_or(b == 0, e != be_ref[jnp.maximum(b - 1, 0)]))
        def _():
            for c in weight_copies(e):
                c.wait()
            wgu_bf[...] = wgu_st[...].astype(bf16)
            wd_bf[...] = wd_st[...].astype(bf16)
            nxt = nxt_ref[e]

            @pl.when(nxt >= 0)
            def _():
                for c in weight_copies(nxt):
                    c.start()

        gu = jnp.dot(_unpack_halves(x_ref[...]), wgu_bf[...], preferred_element_type=f32) + bgu_ref[0]
        glu = jnp.minimum(gu[:, :d], SWIGLU_LIMIT)
        lin = jnp.clip(gu[:, d:], -SWIGLU_LIMIT, SWIGLU_LIMIT)
        act = glu * _sigmoid(SWIGLU_ALPHA * glu) * (lin + 1.0)
        o_ref[...] = jnp.dot(act.astype(bf16), wd_bf[...], preferred_element_type=f32) + bd_ref[0]


def _experts(blk_e, nxt_e, n_used, xb, wgu, bgu, wd, bd):
    p_rows, d = xb.shape[0], wd.shape[2]
    rows = lambda w: pl.BlockSpec((BM, w), lambda b, be, nx, nu: (b, 0))
    per_e = lambda a: pl.BlockSpec((1,) + a.shape[1:], lambda b, be, nx, nu: (be[b], 0, 0))
    hbm = pl.BlockSpec(memory_space=pl.ANY)
    return pl.pallas_call(
        _expert_body,
        out_shape=jax.ShapeDtypeStruct((p_rows, d), f32),
        grid_spec=pltpu.PrefetchScalarGridSpec(
            num_scalar_prefetch=3,
            grid=(p_rows // BM,),
            in_specs=[rows(xb.shape[1]), hbm, per_e(bgu), hbm, per_e(bd)],
            out_specs=rows(d),
            scratch_shapes=[pltpu.VMEM(wgu.shape[1:], f32), pltpu.VMEM(wd.shape[1:], f32),
                            pltpu.VMEM(wgu.shape[1:], bf16), pltpu.VMEM(wd.shape[1:], bf16),
                            pltpu.SemaphoreType.DMA((2,))]),
        compiler_params=_params(("arbitrary",)),
        name="experts",
    )(blk_e, nxt_e, n_used, xb, wgu, bgu, wd, bd)


def _combine_body(toff_ref, dst_ref, c8_ref, yb_ref, ti_ref, gt_ref, x1_ref, mod_ref, g_ref,
                  o_ref, sbuf, sems):
    i = pl.program_id(0)
    base = lax.rem(i, 2) * TG

    def fetch(step, first_slot):
        for j in range(TG):
            def make(so, do, n, sl=first_slot + j):
                return pltpu.make_async_copy(yb_ref.at[pl.ds(do, n)], sbuf.at[sl, pl.ds(so, n)],
                                             sems.at[sl])
            _start_runs(step * TG + j, toff_ref, dst_ref, c8_ref, make)

    @pl.when(i == 0)
    def _():
        sbuf[...] = jnp.zeros_like(sbuf)
        fetch(i, base)

    @pl.when(i + 1 < pl.num_programs(0))
    def _():
        fetch(i + 1, TG - base)

    r = lax.broadcasted_iota(i32, (RS, TS), 0)
    pms, grows = [], []
    for j in range(TG):
        p = _sorted_positions(ti_ref[:, j * TS:(j + 1) * TS])
        gt = gt_ref[:, j * TS:(j + 1) * TS]
        m = [r == p[k] for k in range(TOP_K)]
        pm = jnp.where(m[0], 1.0, jnp.where(m[1], 1.0, jnp.where(m[2], 1.0, jnp.where(m[3], 1.0, 0.0))))
        gm = jnp.where(m[0], gt[0:1, :], jnp.where(m[1], gt[1:2, :],
             jnp.where(m[2], gt[2:3, :], jnp.where(m[3], gt[3:4, :], 0.0))))
        pms.append(pm.astype(bf16))
        grows.append(jnp.sum(gm, axis=1, keepdims=True))

    for j in range(TG):
        n = _tile_rows(i * TG + j, toff_ref, c8_ref)
        pltpu.make_async_copy(yb_ref.at[pl.ds(0, n)], sbuf.at[base + j, pl.ds(0, n)],
                              sems.at[base + j]).wait()
    for j in range(TG):
        sg = (sbuf[base + j] * grows[j]).astype(bf16)
        y = lax.dot_general(pms[j], sg, (((0,), (0,)), ((), ())), preferred_element_type=f32)
        rows = slice(j * TS, (j + 1) * TS)
        o_ref[rows, :] = x1_ref[rows, :] + mod_ref[0, 5:6, :] * _rms(y, g_ref[...])


def _combine(toff, dst, c8s, yb, ti, gt, x1, mod3, g, tiles_per_batch):
    n, d = x1.shape
    tok = pl.BlockSpec((TG * TS, d), lambda i, *_: (i, 0))
    lane = pl.BlockSpec((TOP_K, TG * TS), lambda i, *_: (0, i))
    steps_per_batch = tiles_per_batch // TG
    return pl.pallas_call(
        _combine_body,
        out_shape=jax.ShapeDtypeStruct((n, d), f32),
        grid_spec=pltpu.PrefetchScalarGridSpec(
            num_scalar_prefetch=3,
            grid=(n // (TG * TS),),
            in_specs=[pl.BlockSpec(memory_space=pl.ANY), lane, lane, tok,
                      pl.BlockSpec((1, 6, d), lambda i, *_: (i // steps_per_batch, 0, 0)),
                      pl.BlockSpec(g.shape, lambda i, *_: (0, 0))],
            out_specs=tok,
            scratch_shapes=[pltpu.VMEM((2 * TG, RS, d), f32), pltpu.SemaphoreType.DMA((2 * TG,))]),
        compiler_params=_params(("arbitrary",)),
        name="combine",
    )(toff, dst, c8s, yb, ti, gt, x1, mod3, g)


def _layout_tables(tab, nt, p_rows):
    c8 = tab[:, :nt].T.astype(i32)
    toff = jnp.cumsum(c8, axis=1) - c8
    len8 = jnp.sum(c8, axis=0)
    seg = (len8 + BM - 1) // BM * BM
    gend = jnp.cumsum(seg)
    gstart = gend - seg
    dst = jnp.cumsum(c8, axis=0) - c8 + gstart[None, :]
    n_used = gend[-1] // BM
    blk = jnp.arange(p_rows // BM, dtype=i32)
    last = jnp.minimum(blk, n_used - 1)
    blk_e = jnp.sum((gend[None, :] <= (last * BM)[:, None]).astype(i32), axis=1)
    blk_e = jnp.minimum(blk_e, N_EXPERTS - 1)
    ids = jnp.arange(N_EXPERTS, dtype=i32)
    later = jnp.logical_and(ids[None, :] > ids[:, None], (seg > 0)[None, :])
    nxt_e = jnp.min(jnp.where(later, ids[None, :], N_EXPERTS), axis=1)
    nxt_e = jnp.where(nxt_e == N_EXPERTS, -1, nxt_e).astype(i32)
    return (toff.reshape(-1), dst.reshape(-1), c8.reshape(-1), gstart + len8, seg - len8,
            blk_e, nxt_e, n_used.reshape(1).astype(i32))


def kernel(x, c, w_ada, b_ada, g_pre_mix, g_post_mix, w_in, rel_bias, sgu_ln_g, sgu_ln_b,
           w_spatial, b_spatial, w_branch_a, w_branch_b, w_gate, b_gate, w_out,
           g_pre_ffn, g_post_ffn, w_router, b_router, w_gate_up, b_gate_up, w_down, b_down):
    b, s, d = x.shape
    assert d == D_MODEL and s % max(TM, TMP, TMX) == 0 and s % (TG * TS) == 0
    n = b * s
    nt = n // TS
    ntp = -(-nt // LANES) * LANES
    p_rows = -(-(n * TOP_K + nt * N_EXPERTS * (ROW_ALIGN - 1) + N_EXPERTS * (BM - 1)) // BM) * BM
    depth = w_ada.shape[0]
    c8 = jnp.pad(c, ((0, 8 - b), (0, 0)))
    row = lambda a: a.reshape(1, -1)

    for l in range(depth):
        mod = _ada(c8, w_ada[l], row(b_ada[l]))[:b]
        mod3 = mod.reshape(b, 6, d)

        q, k, v, u, vv = _proj(x, mod3, row(g_pre_mix[l]), w_in[l].astype(bf16),
                               row(sgu_ln_g[l]), row(sgu_ln_b[l]))
        ya = _attn(q, k, v, _attn_bias(rel_bias[l]))
        ws2 = w_spatial[l].astype(bf16).reshape(-1, 2 * SGU_BLOCK, SGU_BLOCK)
        bsf = jnp.repeat(b_spatial[l].T, SGU_WIDTH // b_spatial.shape[1], axis=1)
        x1, h2, ti, gt, tab = _mix(x, mod3, row(g_pre_mix[l]), row(g_post_mix[l]), u, vv, ya, ws2, bsf,
                                   w_branch_a[l].astype(bf16), w_branch_b[l].astype(bf16),
                                   w_gate[l].astype(bf16), row(b_gate[l]), w_out[l].astype(bf16),
                                   row(g_pre_ffn[l]), w_router[l].T.astype(bf16),
                                   b_router[l].reshape(-1, 1), ntp)
        x1f = x1.reshape(n, d)
        h2 = h2.reshape(n, d)
        toff, dst, c8s, tstart, tlen, blk_e, nxt_e, n_used = _layout_tables(tab, nt, p_rows)
        xb = _dispatch(toff, dst, c8s, tstart, tlen, n_used, h2, ti, p_rows)
        yb = _experts(blk_e, nxt_e, n_used, xb, w_gate_up[l], b_gate_up[l][:, None, :],
                      w_down[l], b_down[l][:, None, :])
        x = _combine(toff, dst, c8s, yb, ti, gt, x1f, mod3, row(g_post_ffn[l]), s // TS).reshape(b, s, d)
    return x
```

```python
import functools

import jax
import jax.numpy as jnp
from jax import lax
from jax.experimental import pallas as pl
from jax.experimental.pallas import tpu as pltpu

bf16 = jnp.bfloat16
f32 = jnp.float32
i32 = jnp.int32
u32 = jnp.uint32

D_MODEL = 1024
CHUNK = 64
N_LEFT = 8
ATT_HEADS = 8
HEAD_DIM = 64
ATT_WIDTH = 512
MAX_REL = 128
SGU_BLOCK = 128
SGU_WIDTH = 512
N_EXPERTS = 32
TOP_K = 4
SWIGLU_LIMIT = 7.0
SWIGLU_ALPHA = 1.702
EPS = 1e-6
NEG = -1e30
LOG2E = 1.4426950408889634

LANES = 128
ROW_ALIGN = 8
TM = 512
TMP = 512
TMX = 1024
QCH = 4
QG = QCH * CHUNK
KBAND = (N_LEFT + QCH) * CHUNK
TS = 256
TG = 2
RS = TS * TOP_K + N_EXPERTS * ROW_ALIGN
BM = 512
VMEM_LIMIT = 56 * 2**20


def _params(sem):
    return pltpu.CompilerParams(dimension_semantics=sem, vmem_limit_bytes=VMEM_LIMIT)


def _adaln(x, g, sc, sh):
    ms = jnp.mean(x * x, axis=-1, keepdims=True)
    return (x * lax.rsqrt(ms + EPS) * g) * (1.0 + sc) + sh


def _rms(x, g):
    ms = jnp.mean(x * x, axis=-1, keepdims=True)
    return x * lax.rsqrt(ms + EPS) * g


def _sigmoid(x):
    return 1.0 / (1.0 + jnp.exp(-x))


def _ada_body(c_ref, w_ref, b_ref, o_ref):
    c = c_ref[...]
    ca = c * _sigmoid(c)
    o_ref[...] = jnp.dot(ca.astype(bf16), w_ref[...].astype(bf16),
                         preferred_element_type=f32) + b_ref[...]


def _ada(c8, w, b):
    d = w.shape[0]
    n = w.shape[1] // d
    return pl.pallas_call(
        _ada_body,
        out_shape=jax.ShapeDtypeStruct((8, n * d), f32),
        grid=(n,),
        in_specs=[pl.BlockSpec((8, d), lambda j: (0, 0)),
                  pl.BlockSpec((d, d), lambda j: (0, j)),
                  pl.BlockSpec((1, d), lambda j: (0, j))],
        out_specs=pl.BlockSpec((8, d), lambda j: (0, j)),
        compiler_params=_params(("arbitrary",)),
        name="ada",
    )(c8, w, b)


def _proj_body(x_ref, mod_ref, g_ref, w_ref, lng_ref, lnb_ref,
               q_ref, k_ref, v_ref, u_ref, vv_ref):
    h = _adaln(x_ref[0], g_ref[...], mod_ref[0, 1:2, :], mod_ref[0, 0:1, :]).astype(bf16)
    aw = ATT_WIDTH
    z = jnp.dot(h, w_ref[:, 3 * aw:], preferred_element_type=f32)
    p = jnp.dot(h, w_ref[:, :3 * aw], preferred_element_type=f32)
    q_ref[0] = (p[:, 0:aw] * (HEAD_DIM ** -0.5 * LOG2E)).astype(bf16)
    k_ref[0] = p[:, aw:2 * aw].astype(bf16)
    v_ref[0] = p[:, 2 * aw:3 * aw].astype(bf16)
    zg = 0.5 * z * (1.0 + lax.erf(z * (2.0 ** -0.5)))
    u_ref[0] = zg[:, :SGU_WIDTH].astype(bf16)
    vv = zg[:, SGU_WIDTH:]
    mu = jnp.mean(vv, axis=-1, keepdims=True)
    var = jnp.mean(jnp.square(vv - mu), axis=-1, keepdims=True)
    vn = (vv - mu) * lax.rsqrt(var + EPS) * lng_ref[...] + lnb_ref[...]
    vv_ref[0] = vn.astype(bf16)


def _proj(x, mod3, g, w_in, lng, lnb):
    b, s, d = x.shape
    tok = lambda w: pl.BlockSpec((1, TMP, w), lambda bi, i: (bi, i, 0))
    full = lambda a: pl.BlockSpec(a.shape, lambda bi, i: (0,) * a.ndim)
    o512 = jax.ShapeDtypeStruct((b, s, ATT_WIDTH), bf16)
    return pl.pallas_call(
        _proj_body,
        out_shape=(o512,) * 5,
        grid=(b, s // TMP),
        in_specs=[tok(d), pl.BlockSpec((1, 6, d), lambda bi, i: (bi, 0, 0)),
                  full(g), full(w_in), full(lng), full(lnb)],
        out_specs=(tok(ATT_WIDTH),) * 5,
        compiler_params=_params(("parallel", "arbitrary")),
        name="proj",
    )(x, mod3, g, w_in, lng, lnb)


def _attn_body(q_ref, kp_ref, kc_ref, vp_ref, vc_ref, bias_ref, o_ref, kbuf, vbuf):
    first = pl.program_id(1) == 0
    nhp = ATT_HEADS // 2
    ones = jnp.ones((TM, LANES), bf16)

    def fill(rows, k_src, v_src):
        kbuf[rows, :] = k_src[0]
        for hp in range(nhp):
            vbuf[rows, 2 * hp * LANES:(2 * hp + 1) * LANES] = v_src[0, :, hp * LANES:(hp + 1) * LANES]
            vbuf[rows, (2 * hp + 1) * LANES:(2 * hp + 2) * LANES] = ones

    fill(slice(TM, 2 * TM), kc_ref, vc_ref)

    @pl.when(first)
    def _():
        kbuf[0:TM, :] = jnp.zeros((TM, kbuf.shape[1]), bf16)
        vbuf[0:TM, :] = jnp.zeros((TM, vbuf.shape[1]), bf16)

    @pl.when(jnp.logical_not(first))
    def _():
        fill(slice(0, TM), kp_ref, vp_ref)

    lo = lax.broadcasted_iota(i32, (QG, LANES), 1) < HEAD_DIM

    def group(p, carry):
        r0 = pl.multiple_of(p * QG, QG)
        for hp in range(nhp):
            c0 = hp * LANES
            qp = q_ref[0, pl.ds(r0, QG), c0:c0 + LANES]
            zero = jnp.zeros_like(qp)
            q2 = jnp.concatenate([jnp.where(lo, qp, zero), jnp.where(lo, zero, qp)], axis=0)
            kb = kbuf[pl.ds(r0, KBAND), c0:c0 + LANES]
            s = lax.dot_general(q2, kb, (((1,), (1,)), ((), ())), preferred_element_type=f32)
            sb = (s + bias_ref[hp]).astype(bf16)
            m = jnp.max(sb, axis=-1, keepdims=True)
            e = jnp.exp2(sb - m)
            vb = vbuf[pl.ds(r0, KBAND), 2 * c0:2 * c0 + 2 * LANES]
            o2 = jnp.dot(e, vb, preferred_element_type=f32)
            on = o2[:, :LANES] / o2[:, LANES:]
            o = jnp.where(lo, on[:QG], on[QG:])
            o_ref[0, pl.ds(r0, QG), c0:c0 + LANES] = o.astype(bf16)
        return carry

    lax.fori_loop(0, TM // QG, group, 0, unroll=True)


def _attn(q, k, v, bias2):
    b, s, w = q.shape
    cur = pl.BlockSpec((1, TM, w), lambda bi, i: (bi, i, 0))
    prev = pl.BlockSpec((1, TM, w), lambda bi, i: (bi, jnp.maximum(i - 1, 0), 0))
    return pl.pallas_call(
        _attn_body,
        out_shape=jax.ShapeDtypeStruct((b, s, w), bf16),
        grid=(b, s // TM),
        in_specs=[cur, prev, cur, prev, cur,
                  pl.BlockSpec(bias2.shape, lambda bi, i: (0, 0, 0))],
        out_specs=cur,
        scratch_shapes=[pltpu.VMEM((2 * TM, w), bf16), pltpu.VMEM((2 * TM, 2 * w), bf16)],
        compiler_params=_params(("parallel", "arbitrary")),
        name="attn",
    )(q, k, k, v, v, bias2)


def _attn_bias(rel_bias):
    h = rel_bias.shape[0]
    period = 1024
    n_far = N_LEFT * CHUNK - MAX_REL
    assert period >= QG + KBAND - 1 and n_far >= 0
    far = jnp.broadcast_to(rel_bias[:, 2 * MAX_REL:], (h, period))
    near = rel_bias[:, :0:-1]
    v = jnp.concatenate([far[:, :n_far], near, far[:, n_far + 2 * MAX_REL:]], axis=1).astype(f32)
    flat = jnp.broadcast_to(v[:, None, :], (h, QG, period)).reshape(h, QG * period)
    toep = flat[:, :QG * (period - 1)].reshape(h, QG, period - 1)[:, :, :KBAND]
    i = jnp.arange(QG, dtype=i32)[:, None]
    j = jnp.arange(KBAND, dtype=i32)[None, :]
    jb = j - (i // CHUNK) * CHUNK
    valid = jnp.logical_and(jb >= 0, jb < CHUNK * (N_LEFT + 1))
    bias = jnp.where(valid[None], toep * LOG2E, NEG)
    return bias.reshape(ATT_HEADS // 2, 2 * QG, KBAND)


def _route(hb, wr_ref, br_ref):
    lg = lax.dot_general(wr_ref[...], hb, (((1,), (1,)), ((), ())),
                         preferred_element_type=f32) + br_ref[...]
    e_iota = lax.broadcasted_iota(i32, lg.shape, 0)
    vals, idxs = [], []
    hits = jnp.zeros(lg.shape, f32)
    for _ in range(TOP_K):
        m = jnp.max(lg, axis=0, keepdims=True)
        idx = jnp.min(jnp.where(lg == m, e_iota, N_EXPERTS), axis=0, keepdims=True)
        hit = e_iota == idx
        hits = hits + jnp.where(hit, 1.0, 0.0)
        lg = jnp.where(hit, -jnp.inf, lg)
        vals.append(m)
        idxs.append(idx)
    ex = [jnp.exp(v - vals[0]) for v in vals]
    den = ex[0] + ex[1] + ex[2] + ex[3]
    return (jnp.concatenate(idxs, axis=0), jnp.concatenate([e / den for e in ex], axis=0), hits)


def _mix_body(x_ref, mod_ref, gpre_ref, gpost_ref, u_ref, vv_ref, ya_ref, ws_ref, bs_ref,
              wa_ref, wb_ref, wg_ref, bg_ref, wo_ref, gffn_ref, wr_ref, br_ref,
              o_ref, h2_ref, pos_ref, gt_ref, tab_ref, ybuf):
    d = D_MODEL
    x = x_ref[0]
    h = _adaln(x, gpre_ref[...], mod_ref[0, 1:2, :], mod_ref[0, 0:1, :]).astype(bf16)
    gs = _sigmoid(jnp.dot(h, wg_ref[...], preferred_element_type=f32) + bg_ref[...])

    blk = SGU_BLOCK
    row = lax.broadcasted_iota(i32, (2 * blk, blk), 0)
    colv = lax.broadcasted_iota(i32, (2 * blk, blk), 1)
    causal = colv <= jnp.bitwise_and(row, blk - 1)
    lo = lax.broadcasted_iota(i32, (blk, LANES), 1) < (LANES // 2)
    for gp in range(SGU_WIDTH // LANES):
        c0 = gp * LANES
        w2 = ws_ref[gp]
        w2 = jnp.where(causal, w2, jnp.zeros_like(w2))
        for bi in range(TMX // blk):
            r0 = bi * blk
            s2 = jnp.dot(w2, vv_ref[0, r0:r0 + blk, c0:c0 + LANES], preferred_element_type=f32)
            s = jnp.where(lo, s2[:blk], s2[blk:]) + bs_ref[:, c0:c0 + LANES]
            yb = u_ref[0, r0:r0 + blk, c0:c0 + LANES].astype(f32) * s
            ybuf[r0:r0 + blk, c0:c0 + LANES] = yb.astype(bf16)

    a = jnp.dot(ya_ref[0], wa_ref[...], preferred_element_type=f32)
    bb = jnp.dot(ybuf[...], wb_ref[...], preferred_element_type=f32)
    merged = gs[:, :d] * a + gs[:, d:] * bb
    y = jnp.dot(merged.astype(bf16), wo_ref[...], preferred_element_type=f32)
    x1 = x + mod_ref[0, 2:3, :] * _rms(y, gpost_ref[...])
    o_ref[0] = x1

    hb = _adaln(x1, gffn_ref[...], mod_ref[0, 4:5, :], mod_ref[0, 3:4, :]).astype(bf16)
    h2_ref[0] = hb
    ids, gates, hits = _route(hb, wr_ref, br_ref)
    pos_ref[...] = jnp.concatenate(
        [jnp.concatenate(_sorted_positions(ids[:, j * TS:(j + 1) * TS]), axis=0)
         for j in range(TMX // TS)], axis=1)
    gt_ref[...] = gates

    step = pl.program_id(0) * pl.num_programs(1) + pl.program_id(1)

    @pl.when(step == 0)
    def _():
        tab_ref[...] = jnp.zeros_like(tab_ref)

    lane = lax.broadcasted_iota(i32, tab_ref.shape, 1)
    acc = tab_ref[...]
    for j in range(TMX // TS):
        cnt = jnp.sum(hits[:, j * TS:(j + 1) * TS], axis=1, keepdims=True)
        c8 = jnp.floor((cnt + (ROW_ALIGN - 1.0)) * (1.0 / ROW_ALIGN)) * ROW_ALIGN
        acc = acc + jnp.where(lane == step * (TMX // TS) + j, c8, 0.0)
    tab_ref[...] = acc


def _mix(x, mod3, gpre, gpost, u, vv, ya, ws2, bsf, wa, wb, wg, bg, wo, gffn, wrt, br, ntp):
    b, s, d = x.shape
    n = b * s
    tok = lambda w: pl.BlockSpec((1, TMX, w), lambda bi, i: (bi, i, 0))
    full = lambda a: pl.BlockSpec(a.shape, lambda bi, i: (0,) * a.ndim)
    lane = pl.BlockSpec((TOP_K, TMX), lambda bi, i: (0, bi * (s // TMX) + i))
    return pl.pallas_call(
        _mix_body,
        out_shape=(jax.ShapeDtypeStruct((b, s, d), f32),
                   jax.ShapeDtypeStruct((b, s, d), bf16),
                   jax.ShapeDtypeStruct((TOP_K, n), i32),
                   jax.ShapeDtypeStruct((TOP_K, n), f32),
                   jax.ShapeDtypeStruct((N_EXPERTS, ntp), f32)),
        grid=(b, s // TMX),
        in_specs=[tok(d), pl.BlockSpec((1, 6, d), lambda bi, i: (bi, 0, 0)),
                  full(gpre), full(gpost), tok(SGU_WIDTH), tok(SGU_WIDTH), tok(ATT_WIDTH),
                  full(ws2), full(bsf), full(wa), full(wb), full(wg), full(bg), full(wo),
                  full(gffn), full(wrt), full(br)],
        out_specs=(tok(d), tok(d), lane, lane,
                   pl.BlockSpec((N_EXPERTS, ntp), lambda bi, i: (0, 0))),
        scratch_shapes=[pltpu.VMEM((TMX, SGU_WIDTH), bf16)],
        compiler_params=_params(("arbitrary", "arbitrary")),
        name="mix",
    )(x, mod3, gpre, gpost, u, vv, ya, ws2, bsf, wa, wb, wg, bg, wo, gffn, wrt, br)


def _sorted_positions(ti):
    ts = ti.shape[1]
    e_iota = lax.broadcasted_iota(i32, (N_EXPERTS, ts), 0)
    upper = (lax.broadcasted_iota(i32, (ts, ts), 0) < lax.broadcasted_iota(i32, (ts, ts), 1))
    upper = jnp.where(upper, 1.0, 0.0).astype(bf16)
    hits, prefs, cnts = [], [], []
    for k in range(TOP_K):
        hit = ti[k:k + 1, :] == e_iota
        hf = jnp.where(hit, 1.0, 0.0)
        prefs.append(jnp.dot(hf.astype(bf16), upper, preferred_element_type=f32))
        cnts.append(jnp.sum(hf, axis=1, keepdims=True))
        hits.append(hit)
    total = cnts[0] + cnts[1] + cnts[2] + cnts[3]
    c8 = jnp.floor((total + (ROW_ALIGN - 1.0)) * (1.0 / ROW_ALIGN)) * ROW_ALIGN
    lower = (lax.broadcasted_iota(i32, (N_EXPERTS, N_EXPERTS), 1)
             < lax.broadcasted_iota(i32, (N_EXPERTS, N_EXPERTS), 0))
    lower = jnp.where(lower, 1.0, 0.0).astype(bf16)
    c8b = jnp.broadcast_to(c8, (N_EXPERTS, LANES)).astype(bf16)
    start = jnp.dot(lower, c8b, preferred_element_type=f32)[:, 0:1]
    pos = []
    for k in range(TOP_K):
        pe = start + prefs[k]
        pos.append(jnp.sum(jnp.where(hits[k], pe, 0.0), axis=0, keepdims=True).astype(i32))
        start = start + cnts[k]
    return pos


def _pack_halves(x):
    half = x.shape[1] // 2
    lo = lax.bitcast_convert_type(x[:, :half], u32)
    hi = lax.bitcast_convert_type(x[:, half:], u32)
    return jnp.bitwise_or(lax.shift_right_logical(lo, jnp.uint32(16)), hi)


def _unpack_halves(w):
    lo = lax.bitcast_convert_type(lax.shift_left(w, jnp.uint32(16)), f32)
    hi = lax.bitcast_convert_type(jnp.bitwise_and(w, jnp.uint32(0xFFFF0000)), f32)
    return jnp.concatenate([lo, hi], axis=1).astype(bf16)


def _start_runs(i, toff_ref, dst_ref, c8_ref, make):
    def one(e, carry):
        j = i * N_EXPERTS + e
        n = pl.multiple_of(c8_ref[j], ROW_ALIGN)
        so = pl.multiple_of(toff_ref[j], ROW_ALIGN)
        do = pl.multiple_of(dst_ref[j], ROW_ALIGN)

        @pl.when(n > 0)
        def _():
            make(so, do, n).start()
        return carry
    lax.fori_loop(0, N_EXPERTS, one, 0, unroll=4)


def _tile_rows(i, toff_ref, c8_ref):
    j = i * N_EXPERTS + (N_EXPERTS - 1)
    return pl.multiple_of(toff_ref[j] + c8_ref[j], ROW_ALIGN)


def _dispatch_body(toff_ref, dst_ref, c8_ref, tstart_ref, tlen_ref, nu_ref,
                   h2_ref, pos_ref, xb_ref, sbuf, zbuf, sems, sem):
    i = pl.program_id(0)
    base = lax.rem(i, 2) * TG
    r = lax.broadcasted_iota(i32, (RS, TS), 0)
    for j in range(TG):
        p = [pos_ref[k:k + 1, j * TS:(j + 1) * TS] for k in range(TOP_K)]
        pm = jnp.where(r == p[0], 1.0, jnp.where(r == p[1], 1.0,
             jnp.where(r == p[2], 1.0, jnp.where(r == p[3], 1.0, 0.0))))
        srt = jnp.dot(pm.astype(bf16), h2_ref[j * TS:(j + 1) * TS, :], preferred_element_type=f32)
        sbuf[base + j] = _pack_halves(srt)

    for j in range(TG):
        def make(so, do, n, sl=base + j):
            return pltpu.make_async_copy(sbuf.at[sl, pl.ds(so, n)], xb_ref.at[pl.ds(do, n)],
                                         sems.at[sl])
        _start_runs(i * TG + j, toff_ref, dst_ref, c8_ref, make)

    def wait_tile(t, sl):
        n = _tile_rows(t, toff_ref, c8_ref)
        pltpu.make_async_copy(sbuf.at[sl, pl.ds(0, n)], xb_ref.at[pl.ds(0, n)], sems.at[sl]).wait()

    @pl.when(i > 0)
    def _():
        for j in range(TG):
            wait_tile((i - 1) * TG + j, TG - base + j)

    @pl.when(i == pl.num_programs(0) - 1)
    def _():
        for j in range(TG):
            wait_tile(i * TG + j, base + j)
        zbuf[...] = jnp.zeros_like(zbuf)

        def fill(action):
            def tail(e, carry):
                n = pl.multiple_of(tlen_ref[e], ROW_ALIGN)
                do = pl.multiple_of(tstart_ref[e], ROW_ALIGN)

                @pl.when(n > 0)
                def _():
                    action(pltpu.make_async_copy(zbuf.at[pl.ds(0, n)], xb_ref.at[pl.ds(do, n)], sem))
                return carry
            lax.fori_loop(0, N_EXPERTS, tail, 0)

            def unused(b, carry):
                do = pl.multiple_of(b * BM, BM)
                action(pltpu.make_async_copy(zbuf, xb_ref.at[pl.ds(do, BM)], sem))
                return carry
            lax.fori_loop(nu_ref[0], xb_ref.shape[0] // BM, unused, 0)
        fill(lambda c: c.start())
        fill(lambda c: c.wait())


def _dispatch(toff, dst, c8s, tstart, tlen, n_used, h2, pos, p_rows):
    n, d = h2.shape
    return pl.pallas_call(
        _dispatch_body,
        out_shape=jax.ShapeDtypeStruct((p_rows, d // 2), u32),
        grid_spec=pltpu.PrefetchScalarGridSpec(
            num_scalar_prefetch=6,
            grid=(n // (TG * TS),),
            in_specs=[pl.BlockSpec((TG * TS, d), lambda i, *_: (i, 0)),
                      pl.BlockSpec((TOP_K, TG * TS), lambda i, *_: (0, i))],
            out_specs=pl.BlockSpec(memory_space=pl.ANY),
            scratch_shapes=[pltpu.VMEM((2 * TG, RS, d // 2), u32), pltpu.VMEM((BM, d // 2), u32),
                            pltpu.SemaphoreType.DMA((2 * TG,)), pltpu.SemaphoreType.DMA(())]),
        compiler_params=_params(("arbitrary",)),
        name="dispatch",
    )(toff, dst, c8s, tstart, tlen, n_used, h2, pos)


def _expert_body(be_ref, nxt_ref, nu_ref, x_ref, wgu_hbm, bgu_ref, wd_hbm, bd_ref, o_ref,
                 wgu_st, wd_st, wgu_bf, wd_bf, sems):
    b = pl.program_id(0)
    d = D_MODEL
    e = be_ref[b]

    def weight_copies(ex):
        return (pltpu.make_async_copy(wgu_hbm.at[ex], wgu_st, sems.at[0]),
                pltpu.make_async_copy(wd_hbm.at[ex], wd_st, sems.at[1]))

    @pl.when(b >= nu_ref[0])
    def _():
        o_ref[...] = jnp.zeros_like(o_ref)

    @pl.when(b < nu_ref[0])
    def _():
        @pl.when(b == 0)
        def _():
            for c in weight_copies(e):
                c.start()

        @pl.when(jnp.logical_or(b == 0, e != be_ref[jnp.maximum(b - 1, 0)]))
        def _():
            for c in weight_copies(e):
                c.wait()
            wgu_bf[...] = wgu_st[...].astype(bf16)
            wd_bf[...] = wd_st[...].astype(bf16)
            nxt = nxt_ref[e]

            @pl.when(nxt >= 0)
            def _():
                for c in weight_copies(nxt):
                    c.start()

        gu = jnp.dot(_unpack_halves(x_ref[...]), wgu_bf[...], preferred_element_type=f32) + bgu_ref[0]
        glu = jnp.minimum(gu[:, :d], SWIGLU_LIMIT)
        lin = jnp.clip(gu[:, d:], -SWIGLU_LIMIT, SWIGLU_LIMIT)
        act = glu * _sigmoid(SWIGLU_ALPHA * glu) * (lin + 1.0)
        o_ref[...] = jnp.dot(act.astype(bf16), wd_bf[...], preferred_element_type=f32) + bd_ref[0]


def _experts(blk_e, nxt_e, n_used, xb, wgu, bgu, wd, bd):
    p_rows, d = xb.shape[0], wd.shape[2]
    rows = lambda w: pl.BlockSpec((BM, w), lambda b, be, nx, nu: (b, 0))
    per_e = lambda a: pl.BlockSpec((1,) + a.shape[1:], lambda b, be, nx, nu: (be[b], 0, 0))
    hbm = pl.BlockSpec(memory_space=pl.ANY)
    return pl.pallas_call(
        _expert_body,
        out_shape=jax.ShapeDtypeStruct((p_rows, d), f32),
        grid_spec=pltpu.PrefetchScalarGridSpec(
            num_scalar_prefetch=3,
            grid=(p_rows // BM,),
            in_specs=[rows(xb.shape[1]), hbm, per_e(bgu), hbm, per_e(bd)],
            out_specs=rows(d),
            scratch_shapes=[pltpu.VMEM(wgu.shape[1:], f32), pltpu.VMEM(wd.shape[1:], f32),
                            pltpu.VMEM(wgu.shape[1:], bf16), pltpu.VMEM(wd.shape[1:], bf16),
                            pltpu.SemaphoreType.DMA((2,))]),
        compiler_params=_params(("arbitrary",)),
        name="experts",
    )(blk_e, nxt_e, n_used, xb, wgu, bgu, wd, bd)


def _combine_body(toff_ref, dst_ref, c8_ref, yb_ref, pos_ref, gt_ref, x1_ref, mod_ref, g_ref,
                  o_ref, sbuf, sems):
    i = pl.program_id(0)
    base = lax.rem(i, 2) * TG

    def fetch(step, first_slot):
        for j in range(TG):
            def make(so, do, n, sl=first_slot + j):
                return pltpu.make_async_copy(yb_ref.at[pl.ds(do, n)], sbuf.at[sl, pl.ds(so, n)],
                                             sems.at[sl])
            _start_runs(step * TG + j, toff_ref, dst_ref, c8_ref, make)

    @pl.when(i == 0)
    def _():
        sbuf[...] = jnp.zeros_like(sbuf)
        fetch(i, base)

    @pl.when(i + 1 < pl.num_programs(0))
    def _():
        fetch(i + 1, TG - base)

    r = lax.broadcasted_iota(i32, (RS, TS), 0)
    pms, grows = [], []
    for j in range(TG):
        p = [pos_ref[k:k + 1, j * TS:(j + 1) * TS] for k in range(TOP_K)]
        gt = gt_ref[:, j * TS:(j + 1) * TS]
        m = [r == p[k] for k in range(TOP_K)]
        pm = jnp.where(m[0], 1.0, jnp.where(m[1], 1.0, jnp.where(m[2], 1.0, jnp.where(m[3], 1.0, 0.0))))
        gm = jnp.where(m[0], gt[0:1, :], jnp.where(m[1], gt[1:2, :],
             jnp.where(m[2], gt[2:3, :], jnp.where(m[3], gt[3:4, :], 0.0))))
        pms.append(pm.astype(bf16))
        grows.append(jnp.sum(gm, axis=1, keepdims=True))

    for j in range(TG):
        n = _tile_rows(i * TG + j, toff_ref, c8_ref)
        pltpu.make_async_copy(yb_ref.at[pl.ds(0, n)], sbuf.at[base + j, pl.ds(0, n)],
                              sems.at[base + j]).wait()
    for j in range(TG):
        sg = (sbuf[base + j] * grows[j]).astype(bf16)
        y = lax.dot_general(pms[j], sg, (((0,), (0,)), ((), ())), preferred_element_type=f32)
        rows = slice(j * TS, (j + 1) * TS)
        o_ref[rows, :] = x1_ref[rows, :] + mod_ref[0, 5:6, :] * _rms(y, g_ref[...])


def _combine(toff, dst, c8s, yb, pos, gt, x1, mod3, g, tiles_per_batch):
    n, d = x1.shape
    tok = pl.BlockSpec((TG * TS, d), lambda i, *_: (i, 0))
    lane = pl.BlockSpec((TOP_K, TG * TS), lambda i, *_: (0, i))
    steps_per_batch = tiles_per_batch // TG
    return pl.pallas_call(
        _combine_body,
        out_shape=jax.ShapeDtypeStruct((n, d), f32),
        grid_spec=pltpu.PrefetchScalarGridSpec(
            num_scalar_prefetch=3,
            grid=(n // (TG * TS),),
            in_specs=[pl.BlockSpec(memory_space=pl.ANY), lane, lane, tok,
                      pl.BlockSpec((1, 6, d), lambda i, *_: (i // steps_per_batch, 0, 0)),
                      pl.BlockSpec(g.shape, lambda i, *_: (0, 0))],
            out_specs=tok,
            scratch_shapes=[pltpu.VMEM((2 * TG, RS, d), f32), pltpu.SemaphoreType.DMA((2 * TG,))]),
        compiler_params=_params(("arbitrary",)),
        name="combine",
    )(toff, dst, c8s, yb, pos, gt, x1, mod3, g)


def _layout_tables(tab, nt, p_rows):
    c8 = tab[:, :nt].T.astype(i32)
    toff = jnp.cumsum(c8, axis=1) - c8
    len8 = jnp.sum(c8, axis=0)
    seg = (len8 + BM - 1) // BM * BM
    gend = jnp.cumsum(seg)
    gstart = gend - seg
    dst = jnp.cumsum(c8, axis=0) - c8 + gstart[None, :]
    n_used = gend[-1] // BM
    blk = jnp.arange(p_rows // BM, dtype=i32)
    last = jnp.minimum(blk, n_used - 1)
    blk_e = jnp.sum((gend[None, :] <= (last * BM)[:, None]).astype(i32), axis=1)
    blk_e = jnp.minimum(blk_e, N_EXPERTS - 1)
    ids = jnp.arange(N_EXPERTS, dtype=i32)
    later = jnp.logical_and(ids[None, :] > ids[:, None], (seg > 0)[None, :])
    nxt_e = jnp.min(jnp.where(later, ids[None, :], N_EXPERTS), axis=1)
    nxt_e = jnp.where(nxt_e == N_EXPERTS, -1, nxt_e).astype(i32)
    return (toff.reshape(-1), dst.reshape(-1), c8.reshape(-1), gstart + len8, seg - len8,
            blk_e, nxt_e, n_used.reshape(1).astype(i32))


def kernel(x, c, w_ada, b_ada, g_pre_mix, g_post_mix, w_in, rel_bias, sgu_ln_g, sgu_ln_b,
           w_spatial, b_spatial, w_branch_a, w_branch_b, w_gate, b_gate, w_out,
           g_pre_ffn, g_post_ffn, w_router, b_router, w_gate_up, b_gate_up, w_down, b_down):
    b, s, d = x.shape
    assert d == D_MODEL and s % max(TM, TMP, TMX) == 0 and s % (TG * TS) == 0
    n = b * s
    nt = n // TS
    ntp = -(-nt // LANES) * LANES
    p_rows = -(-(n * TOP_K + nt * N_EXPERTS * (ROW_ALIGN - 1) + N_EXPERTS * (BM - 1)) // BM) * BM
    depth = w_ada.shape[0]
    c8 = jnp.pad(c, ((0, 8 - b), (0, 0)))
    row = lambda a: a.reshape(1, -1)

    for l in range(depth):
        mod = _ada(c8, w_ada[l], row(b_ada[l]))[:b]
        mod3 = mod.reshape(b, 6, d)

        q, k, v, u, vv = _proj(x, mod3, row(g_pre_mix[l]), w_in[l].astype(bf16),
                               row(sgu_ln_g[l]), row(sgu_ln_b[l]))
        ya = _attn(q, k, v, _attn_bias(rel_bias[l]))
        ws2 = w_spatial[l].astype(bf16).reshape(-1, 2 * SGU_BLOCK, SGU_BLOCK)
        bsf = jnp.repeat(b_spatial[l].T, SGU_WIDTH // b_spatial.shape[1], axis=1)
        x1, h2, pos, gt, tab = _mix(x, mod3, row(g_pre_mix[l]), row(g_post_mix[l]), u, vv, ya, ws2, bsf,
                                   w_branch_a[l].astype(bf16), w_branch_b[l].astype(bf16),
                                   w_gate[l].astype(bf16), row(b_gate[l]), w_out[l].astype(bf16),
                                   row(g_pre_ffn[l]), w_router[l].T.astype(bf16),
                                   b_router[l].reshape(-1, 1), ntp)
        x1f = x1.reshape(n, d)
        h2 = h2.reshape(n, d)
        toff, dst, c8s, tstart, tlen, blk_e, nxt_e, n_used = _layout_tables(tab, nt, p_rows)
        xb = _dispatch(toff, dst, c8s, tstart, tlen, n_used, h2, pos, p_rows)
        yb = _experts(blk_e, nxt_e, n_used, xb, w_gate_up[l], b_gate_up[l][:, None, :],
                      w_down[l], b_down[l][:, None, :])
        x = _combine(toff, dst, c8s, yb, pos, gt, x1f, mod3, row(g_post_ffn[l]), s // TS).reshape(b, s, d)
    return x
```

```python
import functools

import jax
import jax.numpy as jnp
from jax import lax
from jax.experimental import pallas as pl
from jax.experimental.pallas import tpu as pltpu

bf16 = jnp.bfloat16
f32 = jnp.float32
i32 = jnp.int32
u32 = jnp.uint32

D_MODEL = 1024
CHUNK = 64
N_LEFT = 8
ATT_HEADS = 8
HEAD_DIM = 64
ATT_WIDTH = 512
MAX_REL = 128
SGU_BLOCK = 128
SGU_WIDTH = 512
N_EXPERTS = 32
TOP_K = 4
SWIGLU_LIMIT = 7.0
SWIGLU_ALPHA = 1.702
EPS = 1e-6
NEG = -1e30
LOG2E = 1.4426950408889634

LANES = 128
ROW_ALIGN = 8
TM = 512
TMP = 1024
TMX = 1024
QCH = 4
QG = QCH * CHUNK
KBAND = (N_LEFT + QCH) * CHUNK
TS = 256
TG = 2
RS = TS * TOP_K + N_EXPERTS * ROW_ALIGN
BM = 512
VMEM_LIMIT = 56 * 2**20


def _params(sem):
    return pltpu.CompilerParams(dimension_semantics=sem, vmem_limit_bytes=VMEM_LIMIT)


def _adaln(x, g, sc, sh):
    ms = jnp.mean(x * x, axis=-1, keepdims=True)
    return (x * lax.rsqrt(ms + EPS) * g) * (1.0 + sc) + sh


def _rms(x, g):
    ms = jnp.mean(x * x, axis=-1, keepdims=True)
    return x * lax.rsqrt(ms + EPS) * g


def _sigmoid(x):
    return 1.0 / (1.0 + jnp.exp(-x))


def _ada_body(c_ref, w_ref, b_ref, o_ref):
    c = c_ref[...]
    ca = c * _sigmoid(c)
    o_ref[...] = jnp.dot(ca.astype(bf16), w_ref[...].astype(bf16),
                         preferred_element_type=f32) + b_ref[...]


def _ada(c8, w, b):
    d = w.shape[0]
    n = w.shape[1] // d
    return pl.pallas_call(
        _ada_body,
        out_shape=jax.ShapeDtypeStruct((8, n * d), f32),
        grid=(n,),
        in_specs=[pl.BlockSpec((8, d), lambda j: (0, 0)),
                  pl.BlockSpec((d, d), lambda j: (0, j)),
                  pl.BlockSpec((1, d), lambda j: (0, j))],
        out_specs=pl.BlockSpec((8, d), lambda j: (0, j)),
        compiler_params=_params(("arbitrary",)),
        name="ada",
    )(c8, w, b)


def _proj_body(x_ref, mod_ref, g_ref, w_ref, lng_ref, lnb_ref,
               q_ref, k_ref, v_ref, u_ref, vv_ref):
    h = _adaln(x_ref[0], g_ref[...], mod_ref[0, 1:2, :], mod_ref[0, 0:1, :]).astype(bf16)
    aw = ATT_WIDTH
    z = jnp.dot(h, w_ref[:, 3 * aw:], preferred_element_type=f32)
    p = jnp.dot(h, w_ref[:, :3 * aw], preferred_element_type=f32)
    q_ref[0] = (p[:, 0:aw] * (HEAD_DIM ** -0.5 * LOG2E)).astype(bf16)
    k_ref[0] = p[:, aw:2 * aw].astype(bf16)
    v_ref[0] = p[:, 2 * aw:3 * aw].astype(bf16)
    zg = 0.5 * z * (1.0 + lax.erf(z * (2.0 ** -0.5)))
    u_ref[0] = zg[:, :SGU_WIDTH].astype(bf16)
    vv = zg[:, SGU_WIDTH:]
    mu = jnp.mean(vv, axis=-1, keepdims=True)
    var = jnp.mean(jnp.square(vv - mu), axis=-1, keepdims=True)
    vn = (vv - mu) * lax.rsqrt(var + EPS) * lng_ref[...] + lnb_ref[...]
    vv_ref[0] = vn.astype(bf16)


def _proj(x, mod3, g, w_in, lng, lnb):
    b, s, d = x.shape
    tok = lambda w: pl.BlockSpec((1, TMP, w), lambda bi, i: (bi, i, 0))
    full = lambda a: pl.BlockSpec(a.shape, lambda bi, i: (0,) * a.ndim)
    o512 = jax.ShapeDtypeStruct((b, s, ATT_WIDTH), bf16)
    return pl.pallas_call(
        _proj_body,
        out_shape=(o512,) * 5,
        grid=(b, s // TMP),
        in_specs=[tok(d), pl.BlockSpec((1, 6, d), lambda bi, i: (bi, 0, 0)),
                  full(g), full(w_in), full(lng), full(lnb)],
        out_specs=(tok(ATT_WIDTH),) * 5,
        compiler_params=_params(("parallel", "arbitrary")),
        name="proj",
    )(x, mod3, g, w_in, lng, lnb)


def _attn_body(q_ref, kp_ref, kc_ref, vp_ref, vc_ref, bias_ref, o_ref, kbuf, vbuf):
    first = pl.program_id(1) == 0
    nhp = ATT_HEADS // 2
    ones = jnp.ones((TM, LANES), bf16)

    def fill(rows, k_src, v_src):
        kbuf[rows, :] = k_src[0]
        for hp in range(nhp):
            vbuf[rows, 2 * hp * LANES:(2 * hp + 1) * LANES] = v_src[0, :, hp * LANES:(hp + 1) * LANES]
            vbuf[rows, (2 * hp + 1) * LANES:(2 * hp + 2) * LANES] = ones

    fill(slice(TM, 2 * TM), kc_ref, vc_ref)

    @pl.when(first)
    def _():
        kbuf[0:TM, :] = jnp.zeros((TM, kbuf.shape[1]), bf16)
        vbuf[0:TM, :] = jnp.zeros((TM, vbuf.shape[1]), bf16)

    @pl.when(jnp.logical_not(first))
    def _():
        fill(slice(0, TM), kp_ref, vp_ref)

    lo = lax.broadcasted_iota(i32, (QG, LANES), 1) < HEAD_DIM

    def group(p, carry):
        r0 = pl.multiple_of(p * QG, QG)
        for hp in range(nhp):
            c0 = hp * LANES
            qp = q_ref[0, pl.ds(r0, QG), c0:c0 + LANES]
            zero = jnp.zeros_like(qp)
            q2 = jnp.concatenate([jnp.where(lo, qp, zero), jnp.where(lo, zero, qp)], axis=0)
            kb = kbuf[pl.ds(r0, KBAND), c0:c0 + LANES]
            s = lax.dot_general(q2, kb, (((1,), (1,)), ((), ())), preferred_element_type=f32)
            sb = (s + bias_ref[hp]).astype(bf16)
            m = jnp.max(sb, axis=-1, keepdims=True)
            e = jnp.exp2(sb - m)
            vb = vbuf[pl.ds(r0, KBAND), 2 * c0:2 * c0 + 2 * LANES]
            o2 = jnp.dot(e, vb, preferred_element_type=f32)
            on = o2[:, :LANES] / o2[:, LANES:]
            o = jnp.where(lo, on[:QG], on[QG:])
            o_ref[0, pl.ds(r0, QG), c0:c0 + LANES] = o.astype(bf16)
        return carry

    lax.fori_loop(0, TM // QG, group, 0, unroll=True)


def _attn(q, k, v, bias2):
    b, s, w = q.shape
    cur = pl.BlockSpec((1, TM, w), lambda bi, i: (bi, i, 0))
    prev = pl.BlockSpec((1, TM, w), lambda bi, i: (bi, jnp.maximum(i - 1, 0), 0))
    return pl.pallas_call(
        _attn_body,
        out_shape=jax.ShapeDtypeStruct((b, s, w), bf16),
        grid=(b, s // TM),
        in_specs=[cur, prev, cur, prev, cur,
                  pl.BlockSpec(bias2.shape, lambda bi, i: (0, 0, 0))],
        out_specs=cur,
        scratch_shapes=[pltpu.VMEM((2 * TM, w), bf16), pltpu.VMEM((2 * TM, 2 * w), bf16)],
        compiler_params=_params(("parallel", "arbitrary")),
        name="attn",
    )(q, k, k, v, v, bias2)


def _attn_bias(rel_bias):
    h = rel_bias.shape[0]
    period = 1024
    n_far = N_LEFT * CHUNK - MAX_REL
    assert period >= QG + KBAND - 1 and n_far >= 0
    far = jnp.broadcast_to(rel_bias[:, 2 * MAX_REL:], (h, period))
    near = rel_bias[:, :0:-1]
    v = jnp.concatenate([far[:, :n_far], near, far[:, n_far + 2 * MAX_REL:]], axis=1).astype(f32)
    flat = jnp.broadcast_to(v[:, None, :], (h, QG, period)).reshape(h, QG * period)
    toep = flat[:, :QG * (period - 1)].reshape(h, QG, period - 1)[:, :, :KBAND]
    i = jnp.arange(QG, dtype=i32)[:, None]
    j = jnp.arange(KBAND, dtype=i32)[None, :]
    jb = j - (i // CHUNK) * CHUNK
    valid = jnp.logical_and(jb >= 0, jb < CHUNK * (N_LEFT + 1))
    bias = jnp.where(valid[None], toep * LOG2E, NEG)
    return bias.reshape(ATT_HEADS // 2, 2 * QG, KBAND)


def _route(hb, wr_ref, br_ref):
    lg = lax.dot_general(wr_ref[...], hb, (((1,), (1,)), ((), ())),
                         preferred_element_type=f32) + br_ref[...]
    e_iota = lax.broadcasted_iota(i32, lg.shape, 0)
    vals, idxs = [], []
    hits = jnp.zeros(lg.shape, f32)
    for _ in range(TOP_K):
        m = jnp.max(lg, axis=0, keepdims=True)
        idx = jnp.min(jnp.where(lg == m, e_iota, N_EXPERTS), axis=0, keepdims=True)
        hit = e_iota == idx
        hits = hits + jnp.where(hit, 1.0, 0.0)
        lg = jnp.where(hit, -jnp.inf, lg)
        vals.append(m)
        idxs.append(idx)
    ex = [jnp.exp(v - vals[0]) for v in vals]
    den = ex[0] + ex[1] + ex[2] + ex[3]
    return (jnp.concatenate(idxs, axis=0), jnp.concatenate([e / den for e in ex], axis=0), hits)


def _mix_body(x_ref, mod_ref, gpre_ref, gpost_ref, u_ref, vv_ref, ya_ref, ws_ref, bs_ref,
              wa_ref, wb_ref, wg_ref, bg_ref, wo_ref, gffn_ref, wr_ref, br_ref,
              o_ref, h2_ref, pos_ref, gt_ref, tab_ref, ybuf):
    d = D_MODEL
    x = x_ref[0]
    h = _adaln(x, gpre_ref[...], mod_ref[0, 1:2, :], mod_ref[0, 0:1, :]).astype(bf16)
    gs = _sigmoid(jnp.dot(h, wg_ref[...], preferred_element_type=f32) + bg_ref[...])

    blk = SGU_BLOCK
    row = lax.broadcasted_iota(i32, (2 * blk, blk), 0)
    colv = lax.broadcasted_iota(i32, (2 * blk, blk), 1)
    causal = colv <= jnp.bitwise_and(row, blk - 1)
    lo = lax.broadcasted_iota(i32, (blk, LANES), 1) < (LANES // 2)
    for gp in range(SGU_WIDTH // LANES):
        c0 = gp * LANES
        w2 = ws_ref[gp]
        w2 = jnp.where(causal, w2, jnp.zeros_like(w2))
        for bi in range(TMX // blk):
            r0 = bi * blk
            s2 = jnp.dot(w2, vv_ref[0, r0:r0 + blk, c0:c0 + LANES], preferred_element_type=f32)
            s = jnp.where(lo, s2[:blk], s2[blk:]) + bs_ref[:, c0:c0 + LANES]
            yb = u_ref[0, r0:r0 + blk, c0:c0 + LANES].astype(f32) * s
            ybuf[r0:r0 + blk, c0:c0 + LANES] = yb.astype(bf16)

    a = jnp.dot(ya_ref[0], wa_ref[...], preferred_element_type=f32)
    bb = jnp.dot(ybuf[...], wb_ref[...], preferred_element_type=f32)
    merged = gs[:, :d] * a + gs[:, d:] * bb
    y = jnp.dot(merged.astype(bf16), wo_ref[...], preferred_element_type=f32)
    x1 = x + mod_ref[0, 2:3, :] * _rms(y, gpost_ref[...])
    o_ref[0] = x1

    hb = _adaln(x1, gffn_ref[...], mod_ref[0, 4:5, :], mod_ref[0, 3:4, :]).astype(bf16)
    h2_ref[0] = hb
    ids, gates, hits = _route(hb, wr_ref, br_ref)
    pos_ref[...] = jnp.concatenate(
        [jnp.concatenate(_sorted_positions(ids[:, j * TS:(j + 1) * TS]), axis=0)
         for j in range(TMX // TS)], axis=1)
    gt_ref[...] = gates

    step = pl.program_id(0) * pl.num_programs(1) + pl.program_id(1)

    @pl.when(step == 0)
    def _():
        tab_ref[...] = jnp.zeros_like(tab_ref)

    lane = lax.broadcasted_iota(i32, tab_ref.shape, 1)
    acc = tab_ref[...]
    for j in range(TMX // TS):
        cnt = jnp.sum(hits[:, j * TS:(j + 1) * TS], axis=1, keepdims=True)
        c8 = jnp.floor((cnt + (ROW_ALIGN - 1.0)) * (1.0 / ROW_ALIGN)) * ROW_ALIGN
        acc = acc + jnp.where(lane == step * (TMX // TS) + j, c8, 0.0)
    tab_ref[...] = acc


def _mix(x, mod3, gpre, gpost, u, vv, ya, ws2, bsf, wa, wb, wg, bg, wo, gffn, wrt, br, ntp):
    b, s, d = x.shape
    n = b * s
    tok = lambda w: pl.BlockSpec((1, TMX, w), lambda bi, i: (bi, i, 0))
    full = lambda a: pl.BlockSpec(a.shape, lambda bi, i: (0,) * a.ndim)
    lane = pl.BlockSpec((TOP_K, TMX), lambda bi, i: (0, bi * (s // TMX) + i))
    return pl.pallas_call(
        _mix_body,
        out_shape=(jax.ShapeDtypeStruct((b, s, d), f32),
                   jax.ShapeDtypeStruct((b, s, d), bf16),
                   jax.ShapeDtypeStruct((TOP_K, n), i32),
                   jax.ShapeDtypeStruct((TOP_K, n), f32),
                   jax.ShapeDtypeStruct((N_EXPERTS, ntp), f32)),
        grid=(b, s // TMX),
        in_specs=[tok(d), pl.BlockSpec((1, 6, d), lambda bi, i: (bi, 0, 0)),
                  full(gpre), full(gpost), tok(SGU_WIDTH), tok(SGU_WIDTH), tok(ATT_WIDTH),
                  full(ws2), full(bsf), full(wa), full(wb), full(wg), full(bg), full(wo),
                  full(gffn), full(wrt), full(br)],
        out_specs=(tok(d), tok(d), lane, lane,
                   pl.BlockSpec((N_EXPERTS, ntp), lambda bi, i: (0, 0))),
        scratch_shapes=[pltpu.VMEM((TMX, SGU_WIDTH), bf16)],
        compiler_params=_params(("arbitrary", "arbitrary")),
        name="mix",
    )(x, mod3, gpre, gpost, u, vv, ya, ws2, bsf, wa, wb, wg, bg, wo, gffn, wrt, br)


def _sorted_positions(ti):
    ts = ti.shape[1]
    e_iota = lax.broadcasted_iota(i32, (N_EXPERTS, ts), 0)
    upper = (lax.broadcasted_iota(i32, (ts, ts), 0) < lax.broadcasted_iota(i32, (ts, ts), 1))
    upper = jnp.where(upper, 1.0, 0.0).astype(bf16)
    hits, prefs, cnts = [], [], []
    for k in range(TOP_K):
        hit = ti[k:k + 1, :] == e_iota
        hf = jnp.where(hit, 1.0, 0.0)
        prefs.append(jnp.dot(hf.astype(bf16), upper, preferred_element_type=f32))
        cnts.append(jnp.sum(hf, axis=1, keepdims=True))
        hits.append(hit)
    total = cnts[0] + cnts[1] + cnts[2] + cnts[3]
    c8 = jnp.floor((total + (ROW_ALIGN - 1.0)) * (1.0 / ROW_ALIGN)) * ROW_ALIGN
    lower = (lax.broadcasted_iota(i32, (N_EXPERTS, N_EXPERTS), 1)
             < lax.broadcasted_iota(i32, (N_EXPERTS, N_EXPERTS), 0))
    lower = jnp.where(lower, 1.0, 0.0).astype(bf16)
    c8b = jnp.broadcast_to(c8, (N_EXPERTS, LANES)).astype(bf16)
    start = jnp.dot(lower, c8b, preferred_element_type=f32)[:, 0:1]
    pos = []
    for k in range(TOP_K):
        pe = start + prefs[k]
        pos.append(jnp.sum(jnp.where(hits[k], pe, 0.0), axis=0, keepdims=True).astype(i32))
        start = start + cnts[k]
    return pos


def _pack_halves(x):
    half = x.shape[1] // 2
    lo = lax.bitcast_convert_type(x[:, :half], u32)
    hi = lax.bitcast_convert_type(x[:, half:], u32)
    return jnp.bitwise_or(lax.shift_right_logical(lo, jnp.uint32(16)), hi)


def _unpack_halves(w):
    lo = lax.bitcast_convert_type(lax.shift_left(w, jnp.uint32(16)), f32)
    hi = lax.bitcast_convert_type(jnp.bitwise_and(w, jnp.uint32(0xFFFF0000)), f32)
    return jnp.concatenate([lo, hi], axis=1).astype(bf16)


def _start_runs(i, toff_ref, dst_ref, c8_ref, make):
    def one(e, carry):
        j = i * N_EXPERTS + e
        n = pl.multiple_of(c8_ref[j], ROW_ALIGN)
        so = pl.multiple_of(toff_ref[j], ROW_ALIGN)
        do = pl.multiple_of(dst_ref[j], ROW_ALIGN)

        @pl.when(n > 0)
        def _():
            make(so, do, n).start()
        return carry
    lax.fori_loop(0, N_EXPERTS, one, 0, unroll=4)


def _tile_rows(i, toff_ref, c8_ref):
    j = i * N_EXPERTS + (N_EXPERTS - 1)
    return pl.multiple_of(toff_ref[j] + c8_ref[j], ROW_ALIGN)


def _dispatch_body(toff_ref, dst_ref, c8_ref, tstart_ref, tlen_ref, nu_ref,
                   h2_ref, pos_ref, xb_ref, sbuf, zbuf, sems, sem):
    i = pl.program_id(0)
    base = lax.rem(i, 2) * TG
    r = lax.broadcasted_iota(i32, (RS, TS), 0)
    for j in range(TG):
        p = [pos_ref[k:k + 1, j * TS:(j + 1) * TS] for k in range(TOP_K)]
        pm = jnp.where(r == p[0], 1.0, jnp.where(r == p[1], 1.0,
             jnp.where(r == p[2], 1.0, jnp.where(r == p[3], 1.0, 0.0))))
        srt = jnp.dot(pm.astype(bf16), h2_ref[j * TS:(j + 1) * TS, :], preferred_element_type=f32)
        sbuf[base + j] = _pack_halves(srt)

    for j in range(TG):
        def make(so, do, n, sl=base + j):
            return pltpu.make_async_copy(sbuf.at[sl, pl.ds(so, n)], xb_ref.at[pl.ds(do, n)],
                                         sems.at[sl])
        _start_runs(i * TG + j, toff_ref, dst_ref, c8_ref, make)

    def wait_tile(t, sl):
        n = _tile_rows(t, toff_ref, c8_ref)
        pltpu.make_async_copy(sbuf.at[sl, pl.ds(0, n)], xb_ref.at[pl.ds(0, n)], sems.at[sl]).wait()

    @pl.when(i > 0)
    def _():
        for j in range(TG):
            wait_tile((i - 1) * TG + j, TG - base + j)

    @pl.when(i == pl.num_programs(0) - 1)
    def _():
        for j in range(TG):
            wait_tile(i * TG + j, base + j)
        zbuf[...] = jnp.zeros_like(zbuf)

        def fill(action):
            def tail(e, carry):
                n = pl.multiple_of(tlen_ref[e], ROW_ALIGN)
                do = pl.multiple_of(tstart_ref[e], ROW_ALIGN)

                @pl.when(n > 0)
                def _():
                    action(pltpu.make_async_copy(zbuf.at[pl.ds(0, n)], xb_ref.at[pl.ds(do, n)], sem))
                return carry
            lax.fori_loop(0, N_EXPERTS, tail, 0)

            def unused(b, carry):
                do = pl.multiple_of(b * BM, BM)
                action(pltpu.make_async_copy(zbuf, xb_ref.at[pl.ds(do, BM)], sem))
                return carry
            lax.fori_loop(nu_ref[0], xb_ref.shape[0] // BM, unused, 0)
        fill(lambda c: c.start())
        fill(lambda c: c.wait())


def _dispatch(toff, dst, c8s, tstart, tlen, n_used, h2, pos, p_rows):
    n, d = h2.shape
    return pl.pallas_call(
        _dispatch_body,
        out_shape=jax.ShapeDtypeStruct((p_rows, d // 2), u32),
        grid_spec=pltpu.PrefetchScalarGridSpec(
            num_scalar_prefetch=6,
            grid=(n // (TG * TS),),
            in_specs=[pl.BlockSpec((TG * TS, d), lambda i, *_: (i, 0)),
                      pl.BlockSpec((TOP_K, TG * TS), lambda i, *_: (0, i))],
            out_specs=pl.BlockSpec(memory_space=pl.ANY),
            scratch_shapes=[pltpu.VMEM((2 * TG, RS, d // 2), u32), pltpu.VMEM((BM, d // 2), u32),
                            pltpu.SemaphoreType.DMA((2 * TG,)), pltpu.SemaphoreType.DMA(())]),
        compiler_params=_params(("arbitrary",)),
        name="dispatch",
    )(toff, dst, c8s, tstart, tlen, n_used, h2, pos)


def _expert_body(be_ref, nxt_ref, nu_ref, x_ref, wgu_hbm, bgu_ref, wd_hbm, bd_ref, o_ref,
                 wgu_st, wd_st, wgu_bf, wd_bf, sems):
    b = pl.program_id(0)
    d = D_MODEL
    e = be_ref[b]

    def weight_copies(ex):
        return (pltpu.make_async_copy(wgu_hbm.at[ex], wgu_st, sems.at[0]),
                pltpu.make_async_copy(wd_hbm.at[ex], wd_st, sems.at[1]))

    @pl.when(b >= nu_ref[0])
    def _():
        o_ref[...] = jnp.zeros_like(o_ref)

    @pl.when(b < nu_ref[0])
    def _():
        @pl.when(b == 0)
        def _():
            for c in weight_copies(e):
                c.start()

        @pl.when(jnp.logical_or(b == 0, e != be_ref[jnp.maximum(b - 1, 0)]))
        def _():
            for c in weight_copies(e):
                c.wait()
            wgu_bf[...] = wgu_st[...].astype(bf16)
            wd_bf[...] = wd_st[...].astype(bf16)
            nxt = nxt_ref[e]

            @pl.when(nxt >= 0)
            def _():
                for c in weight_copies(nxt):
                    c.start()

        gu = jnp.dot(_unpack_halves(x_ref[...]), wgu_bf[...], preferred_element_type=f32) + bgu_ref[0]
        glu = jnp.minimum(gu[:, :d], SWIGLU_LIMIT)
        lin = jnp.clip(gu[:, d:], -SWIGLU_LIMIT, SWIGLU_LIMIT)
        act = glu * _sigmoid(SWIGLU_ALPHA * glu) * (lin + 1.0)
        o_ref[...] = jnp.dot(act.astype(bf16), wd_bf[...], preferred_element_type=f32) + bd_ref[0]


def _experts(blk_e, nxt_e, n_used, xb, wgu, bgu, wd, bd):
    p_rows, d = xb.shape[0], wd.shape[2]
    rows = lambda w: pl.BlockSpec((BM, w), lambda b, be, nx, nu: (b, 0))
    rows_in = lambda w: pl.BlockSpec((BM, w), lambda b, be, nx, nu: (jnp.minimum(b, nu[0] - 1), 0))
    per_e = lambda a: pl.BlockSpec((1,) + a.shape[1:], lambda b, be, nx, nu: (be[b], 0, 0))
    hbm = pl.BlockSpec(memory_space=pl.ANY)
    return pl.pallas_call(
        _expert_body,
        out_shape=jax.ShapeDtypeStruct((p_rows, d), f32),
        grid_spec=pltpu.PrefetchScalarGridSpec(
            num_scalar_prefetch=3,
            grid=(p_rows // BM,),
            in_specs=[rows_in(xb.shape[1]), hbm, per_e(bgu), hbm, per_e(bd)],
            out_specs=rows(d),
            scratch_shapes=[pltpu.VMEM(wgu.shape[1:], f32), pltpu.VMEM(wd.shape[1:], f32),
                            pltpu.VMEM(wgu.shape[1:], bf16), pltpu.VMEM(wd.shape[1:], bf16),
                            pltpu.SemaphoreType.DMA((2,))]),
        compiler_params=_params(("arbitrary",)),
        name="experts",
    )(blk_e, nxt_e, n_used, xb, wgu, bgu, wd, bd)


def _combine_body(toff_ref, dst_ref, c8_ref, yb_ref, pos_ref, gt_ref, x1_ref, mod_ref, g_ref,
                  o_ref, sbuf, sems):
    i = pl.program_id(0)
    base = lax.rem(i, 2) * TG

    def fetch(step, first_slot):
        for j in range(TG):
            def make(so, do, n, sl=first_slot + j):
                return pltpu.make_async_copy(yb_ref.at[pl.ds(do, n)], sbuf.at[sl, pl.ds(so, n)],
                                             sems.at[sl])
            _start_runs(step * TG + j, toff_ref, dst_ref, c8_ref, make)

    @pl.when(i == 0)
    def _():
        sbuf[...] = jnp.zeros_like(sbuf)
        fetch(i, base)

    @pl.when(i + 1 < pl.num_programs(0))
    def _():
        fetch(i + 1, TG - base)

    r = lax.broadcasted_iota(i32, (RS, TS), 0)
    pms, grows = [], []
    for j in range(TG):
        p = [pos_ref[k:k + 1, j * TS:(j + 1) * TS] for k in range(TOP_K)]
        gt = gt_ref[:, j * TS:(j + 1) * TS]
        m = [r == p[k] for k in range(TOP_K)]
        pm = jnp.where(m[0], 1.0, jnp.where(m[1], 1.0, jnp.where(m[2], 1.0, jnp.where(m[3], 1.0, 0.0))))
        gm = jnp.where(m[0], gt[0:1, :], jnp.where(m[1], gt[1:2, :],
             jnp.where(m[2], gt[2:3, :], jnp.where(m[3], gt[3:4, :], 0.0))))
        pms.append(pm.astype(bf16))
        grows.append(jnp.sum(gm, axis=1, keepdims=True))

    for j in range(TG):
        n = _tile_rows(i * TG + j, toff_ref, c8_ref)
        pltpu.make_async_copy(yb_ref.at[pl.ds(0, n)], sbuf.at[base + j, pl.ds(0, n)],
                              sems.at[base + j]).wait()
    for j in range(TG):
        sg = (sbuf[base + j] * grows[j]).astype(bf16)
        y = lax.dot_general(pms[j], sg, (((0,), (0,)), ((), ())), preferred_element_type=f32)
        rows = slice(j * TS, (j + 1) * TS)
        o_ref[rows, :] = x1_ref[rows, :] + mod_ref[0, 5:6, :] * _rms(y, g_ref[...])


def _combine(toff, dst, c8s, yb, pos, gt, x1, mod3, g, tiles_per_batch):
    n, d = x1.shape
    tok = pl.BlockSpec((TG * TS, d), lambda i, *_: (i, 0))
    lane = pl.BlockSpec((TOP_K, TG * TS), lambda i, *_: (0, i))
    steps_per_batch = tiles_per_batch // TG
    return pl.pallas_call(
        _combine_body,
        out_shape=jax.ShapeDtypeStruct((n, d), f32),
        grid_spec=pltpu.PrefetchScalarGridSpec(
            num_scalar_prefetch=3,
            grid=(n // (TG * TS),),
            in_specs=[pl.BlockSpec(memory_space=pl.ANY), lane, lane, tok,
                      pl.BlockSpec((1, 6, d), lambda i, *_: (i // steps_per_batch, 0, 0)),
                      pl.BlockSpec(g.shape, lambda i, *_: (0, 0))],
            out_specs=tok,
            scratch_shapes=[pltpu.VMEM((2 * TG, RS, d), f32), pltpu.SemaphoreType.DMA((2 * TG,))]),
        compiler_params=_params(("arbitrary",)),
        name="combine",
    )(toff, dst, c8s, yb, pos, gt, x1, mod3, g)


def _layout_tables(tab, nt, p_rows):
    c8 = tab[:, :nt].T.astype(i32)
    toff = jnp.cumsum(c8, axis=1) - c8
    len8 = jnp.sum(c8, axis=0)
    seg = (len8 + BM - 1) // BM * BM
    gend = jnp.cumsum(seg)
    gstart = gend - seg
    dst = jnp.cumsum(c8, axis=0) - c8 + gstart[None, :]
    n_used = gend[-1] // BM
    blk = jnp.arange(p_rows // BM, dtype=i32)
    last = jnp.minimum(blk, n_used - 1)
    blk_e = jnp.sum((gend[None, :] <= (last * BM)[:, None]).astype(i32), axis=1)
    blk_e = jnp.minimum(blk_e, N_EXPERTS - 1)
    ids = jnp.arange(N_EXPERTS, dtype=i32)
    later = jnp.logical_and(ids[None, :] > ids[:, None], (seg > 0)[None, :])
    nxt_e = jnp.min(jnp.where(later, ids[None, :], N_EXPERTS), axis=1)
    nxt_e = jnp.where(nxt_e == N_EXPERTS, -1, nxt_e).astype(i32)
    return (toff.reshape(-1), dst.reshape(-1), c8.reshape(-1), gstart + len8, seg - len8,
            blk_e, nxt_e, n_used.reshape(1).astype(i32))


def kernel(x, c, w_ada, b_ada, g_pre_mix, g_post_mix, w_in, rel_bias, sgu_ln_g, sgu_ln_b,
           w_spatial, b_spatial, w_branch_a, w_branch_b, w_gate, b_gate, w_out,
           g_pre_ffn, g_post_ffn, w_router, b_router, w_gate_up, b_gate_up, w_down, b_down):
    b, s, d = x.shape
    assert d == D_MODEL and s % max(TM, TMP, TMX) == 0 and s % (TG * TS) == 0
    n = b * s
    nt = n // TS
    ntp = -(-nt // LANES) * LANES
    p_rows = -(-(n * TOP_K + nt * N_EXPERTS * (ROW_ALIGN - 1) + N_EXPERTS * (BM - 1)) // BM) * BM
    depth = w_ada.shape[0]
    c8 = jnp.pad(c, ((0, 8 - b), (0, 0)))
    row = lambda a: a.reshape(1, -1)

    for l in range(depth):
        mod = _ada(c8, w_ada[l], row(b_ada[l]))[:b]
        mod3 = mod.reshape(b, 6, d)

        q, k, v, u, vv = _proj(x, mod3, row(g_pre_mix[l]), w_in[l].astype(bf16),
                               row(sgu_ln_g[l]), row(sgu_ln_b[l]))
        ya = _attn(q, k, v, _attn_bias(rel_bias[l]))
        ws2 = w_spatial[l].astype(bf16).reshape(-1, 2 * SGU_BLOCK, SGU_BLOCK)
        bsf = jnp.repeat(b_spatial[l].T, SGU_WIDTH // b_spatial.shape[1], axis=1)
        x1, h2, pos, gt, tab = _mix(x, mod3, row(g_pre_mix[l]), row(g_post_mix[l]), u, vv, ya, ws2, bsf,
                                   w_branch_a[l].astype(bf16), w_branch_b[l].astype(bf16),
                                   w_gate[l].astype(bf16), row(b_gate[l]), w_out[l].astype(bf16),
                                   row(g_pre_ffn[l]), w_router[l].T.astype(bf16),
                                   b_router[l].reshape(-1, 1), ntp)
        x1f = x1.reshape(n, d)
        h2 = h2.reshape(n, d)
        toff, dst, c8s, tstart, tlen, blk_e, nxt_e, n_used = _layout_tables(tab, nt, p_rows)
        xb = _dispatch(toff, dst, c8s, tstart, tlen, n_used, h2, pos, p_rows)
        yb = _experts(blk_e, nxt_e, n_used, xb, w_gate_up[l], b_gate_up[l][:, None, :],
                      w_down[l], b_down[l][:, None, :])
        x = _combine(toff, dst, c8s, yb, pos, gt, x1f, mod3, row(g_post_ffn[l]), s // TS).reshape(b, s, d)
    return x
```

```python
import functools

import jax
import jax.numpy as jnp
from jax import lax
from jax.experimental import pallas as pl
from jax.experimental.pallas import tpu as pltpu

bf16 = jnp.bfloat16
f32 = jnp.float32
i32 = jnp.int32
u32 = jnp.uint32

D_MODEL = 1024
CHUNK = 64
N_LEFT = 8
ATT_HEADS = 8
HEAD_DIM = 64
ATT_WIDTH = 512
MAX_REL = 128
SGU_BLOCK = 128
SGU_WIDTH = 512
N_EXPERTS = 32
TOP_K = 4
SWIGLU_LIMIT = 7.0
SWIGLU_ALPHA = 1.702
EPS = 1e-6
NEG = -1e30
LOG2E = 1.4426950408889634

LANES = 128
ROW_ALIGN = 8
TM = 512
TMP = 1024
TMX = 1024
QCH = 4
QG = QCH * CHUNK
KBAND = (N_LEFT + QCH) * CHUNK
TS = 256
TG = 2
RS = TS * TOP_K + N_EXPERTS * ROW_ALIGN
BM = 512
VMEM_LIMIT = 56 * 2**20


def _params(sem):
    return pltpu.CompilerParams(dimension_semantics=sem, vmem_limit_bytes=VMEM_LIMIT)


def _adaln(x, g, sc, sh):
    ms = jnp.mean(x * x, axis=-1, keepdims=True)
    return (x * lax.rsqrt(ms + EPS) * g) * (1.0 + sc) + sh


def _rms(x, g):
    ms = jnp.mean(x * x, axis=-1, keepdims=True)
    return x * lax.rsqrt(ms + EPS) * g


def _sigmoid(x):
    return 1.0 / (1.0 + jnp.exp(-x))


def _ada_body(c_ref, w_ref, b_ref, o_ref):
    c = c_ref[...]
    ca = c * _sigmoid(c)
    o_ref[...] = jnp.dot(ca.astype(bf16), w_ref[...].astype(bf16),
                         preferred_element_type=f32) + b_ref[...]


def _ada(c8, w, b):
    d = w.shape[0]
    n = w.shape[1] // d
    return pl.pallas_call(
        _ada_body,
        out_shape=jax.ShapeDtypeStruct((8, n * d), f32),
        grid=(n,),
        in_specs=[pl.BlockSpec((8, d), lambda j: (0, 0)),
                  pl.BlockSpec((d, d), lambda j: (0, j)),
                  pl.BlockSpec((1, d), lambda j: (0, j))],
        out_specs=pl.BlockSpec((8, d), lambda j: (0, j)),
        compiler_params=_params(("arbitrary",)),
        name="ada",
    )(c8, w, b)


def _proj_body(x_ref, mod_ref, g_ref, w_ref, lng_ref, lnb_ref,
               q_ref, k_ref, v_ref, u_ref, vv_ref):
    h = _adaln(x_ref[0], g_ref[...], mod_ref[0, 1:2, :], mod_ref[0, 0:1, :]).astype(bf16)
    aw = ATT_WIDTH
    z = jnp.dot(h, w_ref[:, 3 * aw:], preferred_element_type=f32)
    p = jnp.dot(h, w_ref[:, :3 * aw], preferred_element_type=f32)
    q_ref[0] = (p[:, 0:aw] * (HEAD_DIM ** -0.5 * LOG2E)).astype(bf16)
    k_ref[0] = p[:, aw:2 * aw].astype(bf16)
    v_ref[0] = p[:, 2 * aw:3 * aw].astype(bf16)
    zg = 0.5 * z * (1.0 + lax.erf(z * (2.0 ** -0.5)))
    u_ref[0] = zg[:, :SGU_WIDTH].astype(bf16)
    vv = zg[:, SGU_WIDTH:]
    mu = jnp.mean(vv, axis=-1, keepdims=True)
    var = jnp.mean(jnp.square(vv - mu), axis=-1, keepdims=True)
    vn = (vv - mu) * lax.rsqrt(var + EPS) * lng_ref[...] + lnb_ref[...]
    vv_ref[0] = vn.astype(bf16)


def _proj(x, mod3, g, w_in, lng, lnb):
    b, s, d = x.shape
    tok = lambda w: pl.BlockSpec((1, TMP, w), lambda bi, i: (bi, i, 0))
    full = lambda a: pl.BlockSpec(a.shape, lambda bi, i: (0,) * a.ndim)
    o512 = jax.ShapeDtypeStruct((b, s, ATT_WIDTH), bf16)
    return pl.pallas_call(
        _proj_body,
        out_shape=(o512,) * 5,
        grid=(b, s // TMP),
        in_specs=[tok(d), pl.BlockSpec((1, 6, d), lambda bi, i: (bi, 0, 0)),
                  full(g), full(w_in), full(lng), full(lnb)],
        out_specs=(tok(ATT_WIDTH),) * 5,
        compiler_params=_params(("parallel", "arbitrary")),
        name="proj",
    )(x, mod3, g, w_in, lng, lnb)


def _attn_body(q_ref, kp_ref, kc_ref, vp_ref, vc_ref, bias_ref, o_ref, kbuf, vbuf):
    first = pl.program_id(1) == 0
    nhp = ATT_HEADS // 2
    ones = jnp.ones((TM, LANES), bf16)

    def fill(rows, k_src, v_src):
        kbuf[rows, :] = k_src[0]
        for hp in range(nhp):
            vbuf[rows, 2 * hp * LANES:(2 * hp + 1) * LANES] = v_src[0, :, hp * LANES:(hp + 1) * LANES]
            vbuf[rows, (2 * hp + 1) * LANES:(2 * hp + 2) * LANES] = ones

    fill(slice(TM, 2 * TM), kc_ref, vc_ref)

    @pl.when(first)
    def _():
        kbuf[0:TM, :] = jnp.zeros((TM, kbuf.shape[1]), bf16)
        vbuf[0:TM, :] = jnp.zeros((TM, vbuf.shape[1]), bf16)

    @pl.when(jnp.logical_not(first))
    def _():
        fill(slice(0, TM), kp_ref, vp_ref)

    lo = lax.broadcasted_iota(i32, (QG, LANES), 1) < HEAD_DIM

    def group(p, carry):
        r0 = pl.multiple_of(p * QG, QG)
        for hp in range(nhp):
            c0 = hp * LANES
            qp = q_ref[0, pl.ds(r0, QG), c0:c0 + LANES]
            zero = jnp.zeros_like(qp)
            q2 = jnp.concatenate([jnp.where(lo, qp, zero), jnp.where(lo, zero, qp)], axis=0)
            kb = kbuf[pl.ds(r0, KBAND), c0:c0 + LANES]
            s = lax.dot_general(q2, kb, (((1,), (1,)), ((), ())), preferred_element_type=f32)
            sb = (s + bias_ref[hp]).astype(bf16)
            m = jnp.max(sb, axis=-1, keepdims=True)
            e = jnp.exp2(sb - m)
            vb = vbuf[pl.ds(r0, KBAND), 2 * c0:2 * c0 + 2 * LANES]
            o2 = jnp.dot(e, vb, preferred_element_type=f32)
            on = o2[:, :LANES] / o2[:, LANES:]
            o = jnp.where(lo, on[:QG], on[QG:])
            o_ref[0, pl.ds(r0, QG), c0:c0 + LANES] = o.astype(bf16)
        return carry

    lax.fori_loop(0, TM // QG, group, 0, unroll=True)


def _attn(q, k, v, bias2):
    b, s, w = q.shape
    cur = pl.BlockSpec((1, TM, w), lambda bi, i: (bi, i, 0))
    prev = pl.BlockSpec((1, TM, w), lambda bi, i: (bi, jnp.maximum(i - 1, 0), 0))
    return pl.pallas_call(
        _attn_body,
        out_shape=jax.ShapeDtypeStruct((b, s, w), bf16),
        grid=(b, s // TM),
        in_specs=[cur, prev, cur, prev, cur,
                  pl.BlockSpec(bias2.shape, lambda bi, i: (0, 0, 0))],
        out_specs=cur,
        scratch_shapes=[pltpu.VMEM((2 * TM, w), bf16), pltpu.VMEM((2 * TM, 2 * w), bf16)],
        compiler_params=_params(("parallel", "arbitrary")),
        name="attn",
    )(q, k, k, v, v, bias2)


def _attn_bias(rel_bias):
    h = rel_bias.shape[0]
    period = 1024
    n_far = N_LEFT * CHUNK - MAX_REL
    assert period >= QG + KBAND - 1 and n_far >= 0
    far = jnp.broadcast_to(rel_bias[:, 2 * MAX_REL:], (h, period))
    near = rel_bias[:, :0:-1]
    v = jnp.concatenate([far[:, :n_far], near, far[:, n_far + 2 * MAX_REL:]], axis=1).astype(f32)
    flat = jnp.broadcast_to(v[:, None, :], (h, QG, period)).reshape(h, QG * period)
    toep = flat[:, :QG * (period - 1)].reshape(h, QG, period - 1)[:, :, :KBAND]
    i = jnp.arange(QG, dtype=i32)[:, None]
    j = jnp.arange(KBAND, dtype=i32)[None, :]
    jb = j - (i // CHUNK) * CHUNK
    valid = jnp.logical_and(jb >= 0, jb < CHUNK * (N_LEFT + 1))
    bias = jnp.where(valid[None], toep * LOG2E, NEG)
    return bias.reshape(ATT_HEADS // 2, 2 * QG, KBAND)


def _route(hb, wr_ref, br_ref):
    lg = lax.dot_general(wr_ref[...], hb, (((1,), (1,)), ((), ())),
                         preferred_element_type=f32) + br_ref[...]
    e_iota = lax.broadcasted_iota(i32, lg.shape, 0)
    vals, idxs = [], []
    hits = jnp.zeros(lg.shape, f32)
    for _ in range(TOP_K):
        m = jnp.max(lg, axis=0, keepdims=True)
        idx = jnp.min(jnp.where(lg == m, e_iota, N_EXPERTS), axis=0, keepdims=True)
        hit = e_iota == idx
        hits = hits + jnp.where(hit, 1.0, 0.0)
        lg = jnp.where(hit, -jnp.inf, lg)
        vals.append(m)
        idxs.append(idx)
    ex = [jnp.exp(v - vals[0]) for v in vals]
    den = ex[0] + ex[1] + ex[2] + ex[3]
    return (jnp.concatenate(idxs, axis=0), jnp.concatenate([e / den for e in ex], axis=0), hits)


def _mix_body(x_ref, mod_ref, gpre_ref, gpost_ref, u_ref, vv_ref, ya_ref, ws_ref, bs_ref,
              wa_ref, wb_ref, wg_ref, bg_ref, wo_ref, gffn_ref, wr_ref, br_ref,
              o_ref, h2_ref, pos_ref, gt_ref, tab_ref, ybuf):
    d = D_MODEL
    x = x_ref[0]
    h = _adaln(x, gpre_ref[...], mod_ref[0, 1:2, :], mod_ref[0, 0:1, :]).astype(bf16)
    gs = _sigmoid(jnp.dot(h, wg_ref[...], preferred_element_type=f32) + bg_ref[...])

    blk = SGU_BLOCK
    row = lax.broadcasted_iota(i32, (2 * blk, blk), 0)
    colv = lax.broadcasted_iota(i32, (2 * blk, blk), 1)
    causal = colv <= jnp.bitwise_and(row, blk - 1)
    lo = lax.broadcasted_iota(i32, (blk, LANES), 1) < (LANES // 2)
    for gp in range(SGU_WIDTH // LANES):
        c0 = gp * LANES
        w2 = ws_ref[gp]
        w2 = jnp.where(causal, w2, jnp.zeros_like(w2))
        for bi in range(TMX // blk):
            r0 = bi * blk
            s2 = jnp.dot(w2, vv_ref[0, r0:r0 + blk, c0:c0 + LANES], preferred_element_type=f32)
            s = jnp.where(lo, s2[:blk], s2[blk:]) + bs_ref[:, c0:c0 + LANES]
            yb = u_ref[0, r0:r0 + blk, c0:c0 + LANES].astype(f32) * s
            ybuf[r0:r0 + blk, c0:c0 + LANES] = yb.astype(bf16)

    a = jnp.dot(ya_ref[0], wa_ref[...], preferred_element_type=f32)
    bb = jnp.dot(ybuf[...], wb_ref[...], preferred_element_type=f32)
    merged = gs[:, :d] * a + gs[:, d:] * bb
    y = jnp.dot(merged.astype(bf16), wo_ref[...], preferred_element_type=f32)
    x1 = x + mod_ref[0, 2:3, :] * _rms(y, gpost_ref[...])
    o_ref[0] = x1

    hb = _adaln(x1, gffn_ref[...], mod_ref[0, 4:5, :], mod_ref[0, 3:4, :]).astype(bf16)
    h2_ref[0] = hb
    ids, gates, hits = _route(hb, wr_ref, br_ref)
    pos_ref[...] = jnp.concatenate(
        [jnp.concatenate(_sorted_positions(ids[:, j * TS:(j + 1) * TS]), axis=0)
         for j in range(TMX // TS)], axis=1)
    gt_ref[...] = gates

    step = pl.program_id(0) * pl.num_programs(1) + pl.program_id(1)

    @pl.when(step == 0)
    def _():
        tab_ref[...] = jnp.zeros_like(tab_ref)

    lane = lax.broadcasted_iota(i32, tab_ref.shape, 1)
    acc = tab_ref[...]
    for j in range(TMX // TS):
        cnt = jnp.sum(hits[:, j * TS:(j + 1) * TS], axis=1, keepdims=True)
        c8 = jnp.floor((cnt + (ROW_ALIGN - 1.0)) * (1.0 / ROW_ALIGN)) * ROW_ALIGN
        acc = acc + jnp.where(lane == step * (TMX // TS) + j, c8, 0.0)
    tab_ref[...] = acc


def _mix(x, mod3, gpre, gpost, u, vv, ya, ws2, bsf, wa, wb, wg, bg, wo, gffn, wrt, br, ntp):
    b, s, d = x.shape
    n = b * s
    tok = lambda w: pl.BlockSpec((1, TMX, w), lambda bi, i: (bi, i, 0))
    full = lambda a: pl.BlockSpec(a.shape, lambda bi, i: (0,) * a.ndim)
    lane = pl.BlockSpec((TOP_K, TMX), lambda bi, i: (0, bi * (s // TMX) + i))
    return pl.pallas_call(
        _mix_body,
        out_shape=(jax.ShapeDtypeStruct((b, s, d), f32),
                   jax.ShapeDtypeStruct((b, s, d), bf16),
                   jax.ShapeDtypeStruct((TOP_K, n), i32),
                   jax.ShapeDtypeStruct((TOP_K, n), f32),
                   jax.ShapeDtypeStruct((N_EXPERTS, ntp), f32)),
        grid=(b, s // TMX),
        in_specs=[tok(d), pl.BlockSpec((1, 6, d), lambda bi, i: (bi, 0, 0)),
                  full(gpre), full(gpost), tok(SGU_WIDTH), tok(SGU_WIDTH), tok(ATT_WIDTH),
                  full(ws2), full(bsf), full(wa), full(wb), full(wg), full(bg), full(wo),
                  full(gffn), full(wrt), full(br)],
        out_specs=(tok(d), tok(d), lane, lane,
                   pl.BlockSpec((N_EXPERTS, ntp), lambda bi, i: (0, 0))),
        scratch_shapes=[pltpu.VMEM((TMX, SGU_WIDTH), bf16)],
        compiler_params=_params(("arbitrary", "arbitrary")),
        name="mix",
    )(x, mod3, gpre, gpost, u, vv, ya, ws2, bsf, wa, wb, wg, bg, wo, gffn, wrt, br)


def _sorted_positions(ti):
    ts = ti.shape[1]
    e_iota = lax.broadcasted_iota(i32, (N_EXPERTS, ts), 0)
    upper = (lax.broadcasted_iota(i32, (ts, ts), 0) < lax.broadcasted_iota(i32, (ts, ts), 1))
    upper = jnp.where(upper, 1.0, 0.0).astype(bf16)
    hits, prefs, cnts = [], [], []
    for k in range(TOP_K):
        hit = ti[k:k + 1, :] == e_iota
        hf = jnp.where(hit, 1.0, 0.0)
        prefs.append(jnp.dot(hf.astype(bf16), upper, preferred_element_type=f32))
        cnts.append(jnp.sum(hf, axis=1, keepdims=True))
        hits.append(hit)
    total = cnts[0] + cnts[1] + cnts[2] + cnts[3]
    c8 = jnp.floor((total + (ROW_ALIGN - 1.0)) * (1.0 / ROW_ALIGN)) * ROW_ALIGN
    lower = (lax.broadcasted_iota(i32, (N_EXPERTS, N_EXPERTS), 1)
             < lax.broadcasted_iota(i32, (N_EXPERTS, N_EXPERTS), 0))
    lower = jnp.where(lower, 1.0, 0.0).astype(bf16)
    c8b = jnp.broadcast_to(c8, (N_EXPERTS, LANES)).astype(bf16)
    start = jnp.dot(lower, c8b, preferred_element_type=f32)[:, 0:1]
    pos = []
    for k in range(TOP_K):
        pe = start + prefs[k]
        pos.append(jnp.sum(jnp.where(hits[k], pe, 0.0), axis=0, keepdims=True).astype(i32))
        start = start + cnts[k]
    return pos


def _pack_halves(x, exact):
    half = x.shape[1] // 2
    lo = lax.bitcast_convert_type(x[:, :half], u32)
    hi = lax.bitcast_convert_type(x[:, half:], u32)
    if not exact:
        hi = jnp.bitwise_and(hi, jnp.uint32(0xFFFF0000))
    return jnp.bitwise_or(lax.shift_right_logical(lo, jnp.uint32(16)), hi)


def _unpack_halves(w):
    lo = lax.bitcast_convert_type(lax.shift_left(w, jnp.uint32(16)), f32)
    hi = lax.bitcast_convert_type(jnp.bitwise_and(w, jnp.uint32(0xFFFF0000)), f32)
    return lo, hi


def _start_runs(i, toff_ref, dst_ref, c8_ref, make):
    def one(e, carry):
        j = i * N_EXPERTS + e
        n = pl.multiple_of(c8_ref[j], ROW_ALIGN)
        so = pl.multiple_of(toff_ref[j], ROW_ALIGN)
        do = pl.multiple_of(dst_ref[j], ROW_ALIGN)

        @pl.when(n > 0)
        def _():
            make(so, do, n).start()
        return carry
    lax.fori_loop(0, N_EXPERTS, one, 0, unroll=4)


def _tile_rows(i, toff_ref, c8_ref):
    j = i * N_EXPERTS + (N_EXPERTS - 1)
    return pl.multiple_of(toff_ref[j] + c8_ref[j], ROW_ALIGN)


def _dispatch_body(toff_ref, dst_ref, c8_ref, tstart_ref, tlen_ref, nu_ref,
                   h2_ref, pos_ref, xb_ref, sbuf, zbuf, sems, sem):
    i = pl.program_id(0)
    base = lax.rem(i, 2) * TG
    r = lax.broadcasted_iota(i32, (RS, TS), 0)
    for j in range(TG):
        p = [pos_ref[k:k + 1, j * TS:(j + 1) * TS] for k in range(TOP_K)]
        pm = jnp.where(r == p[0], 1.0, jnp.where(r == p[1], 1.0,
             jnp.where(r == p[2], 1.0, jnp.where(r == p[3], 1.0, 0.0))))
        srt = jnp.dot(pm.astype(bf16), h2_ref[j * TS:(j + 1) * TS, :], preferred_element_type=f32)
        sbuf[base + j] = _pack_halves(srt, exact=True)

    for j in range(TG):
        def make(so, do, n, sl=base + j):
            return pltpu.make_async_copy(sbuf.at[sl, pl.ds(so, n)], xb_ref.at[pl.ds(do, n)],
                                         sems.at[sl])
        _start_runs(i * TG + j, toff_ref, dst_ref, c8_ref, make)

    def wait_tile(t, sl):
        n = _tile_rows(t, toff_ref, c8_ref)
        pltpu.make_async_copy(sbuf.at[sl, pl.ds(0, n)], xb_ref.at[pl.ds(0, n)], sems.at[sl]).wait()

    @pl.when(i > 0)
    def _():
        for j in range(TG):
            wait_tile((i - 1) * TG + j, TG - base + j)

    @pl.when(i == pl.num_programs(0) - 1)
    def _():
        for j in range(TG):
            wait_tile(i * TG + j, base + j)
        zbuf[...] = jnp.zeros_like(zbuf)

        def fill(action):
            def tail(e, carry):
                n = pl.multiple_of(tlen_ref[e], ROW_ALIGN)
                do = pl.multiple_of(tstart_ref[e], ROW_ALIGN)

                @pl.when(n > 0)
                def _():
                    action(pltpu.make_async_copy(zbuf.at[pl.ds(0, n)], xb_ref.at[pl.ds(do, n)], sem))
                return carry
            lax.fori_loop(0, N_EXPERTS, tail, 0)

            def unused(b, carry):
                do = pl.multiple_of(b * BM, BM)
                action(pltpu.make_async_copy(zbuf, xb_ref.at[pl.ds(do, BM)], sem))
                return carry
            lax.fori_loop(nu_ref[0], xb_ref.shape[0] // BM, unused, 0)
        fill(lambda c: c.start())
        fill(lambda c: c.wait())


def _dispatch(toff, dst, c8s, tstart, tlen, n_used, h2, pos, p_rows):
    n, d = h2.shape
    return pl.pallas_call(
        _dispatch_body,
        out_shape=jax.ShapeDtypeStruct((p_rows, d // 2), u32),
        grid_spec=pltpu.PrefetchScalarGridSpec(
            num_scalar_prefetch=6,
            grid=(n // (TG * TS),),
            in_specs=[pl.BlockSpec((TG * TS, d), lambda i, *_: (i, 0)),
                      pl.BlockSpec((TOP_K, TG * TS), lambda i, *_: (0, i))],
            out_specs=pl.BlockSpec(memory_space=pl.ANY),
            scratch_shapes=[pltpu.VMEM((2 * TG, RS, d // 2), u32), pltpu.VMEM((BM, d // 2), u32),
                            pltpu.SemaphoreType.DMA((2 * TG,)), pltpu.SemaphoreType.DMA(())]),
        compiler_params=_params(("arbitrary",)),
        name="dispatch",
    )(toff, dst, c8s, tstart, tlen, n_used, h2, pos)


def _expert_body(be_ref, nxt_ref, nu_ref, x_ref, wgu_hbm, bgu_ref, wd_hbm, bd_ref, o_ref,
                 wgu_st, wd_st, wgu_bf, wd_bf, sems):
    b = pl.program_id(0)
    d = D_MODEL
    e = be_ref[b]

    def weight_copies(ex):
        return (pltpu.make_async_copy(wgu_hbm.at[ex], wgu_st, sems.at[0]),
                pltpu.make_async_copy(wd_hbm.at[ex], wd_st, sems.at[1]))

    @pl.when(b >= nu_ref[0])
    def _():
        o_ref[...] = jnp.zeros_like(o_ref)

    @pl.when(b < nu_ref[0])
    def _():
        @pl.when(b == 0)
        def _():
            for c in weight_copies(e):
                c.start()

        @pl.when(jnp.logical_or(b == 0, e != be_ref[jnp.maximum(b - 1, 0)]))
        def _():
            for c in weight_copies(e):
                c.wait()
            wgu_bf[...] = wgu_st[...].astype(bf16)
            wd_bf[...] = wd_st[...].astype(bf16)
            nxt = nxt_ref[e]

            @pl.when(nxt >= 0)
            def _():
                for c in weight_copies(nxt):
                    c.start()

        xb = jnp.concatenate(_unpack_halves(x_ref[...]), axis=1).astype(bf16)
        gu = jnp.dot(xb, wgu_bf[...], preferred_element_type=f32) + bgu_ref[0]
        glu = jnp.minimum(gu[:, :d], SWIGLU_LIMIT)
        lin = jnp.clip(gu[:, d:], -SWIGLU_LIMIT, SWIGLU_LIMIT)
        act = glu * _sigmoid(SWIGLU_ALPHA * glu) * (lin + 1.0)
        y = jnp.dot(act.astype(bf16), wd_bf[...], preferred_element_type=f32) + bd_ref[0]
        o_ref[...] = _pack_halves(y, exact=False)


def _experts(blk_e, nxt_e, n_used, xb, wgu, bgu, wd, bd):
    p_rows, d = xb.shape[0], wd.shape[2]
    rows = lambda w: pl.BlockSpec((BM, w), lambda b, be, nx, nu: (b, 0))
    rows_in = lambda w: pl.BlockSpec((BM, w), lambda b, be, nx, nu: (jnp.minimum(b, nu[0] - 1), 0))
    per_e = lambda a: pl.BlockSpec((1,) + a.shape[1:], lambda b, be, nx, nu: (be[b], 0, 0))
    hbm = pl.BlockSpec(memory_space=pl.ANY)
    return pl.pallas_call(
        _expert_body,
        out_shape=jax.ShapeDtypeStruct((p_rows, d // 2), u32),
        grid_spec=pltpu.PrefetchScalarGridSpec(
            num_scalar_prefetch=3,
            grid=(p_rows // BM,),
            in_specs=[rows_in(xb.shape[1]), hbm, per_e(bgu), hbm, per_e(bd)],
            out_specs=rows(d // 2),
            scratch_shapes=[pltpu.VMEM(wgu.shape[1:], f32), pltpu.VMEM(wd.shape[1:], f32),
                            pltpu.VMEM(wgu.shape[1:], bf16), pltpu.VMEM(wd.shape[1:], bf16),
                            pltpu.SemaphoreType.DMA((2,))]),
        compiler_params=_params(("arbitrary",)),
        name="experts",
    )(blk_e, nxt_e, n_used, xb, wgu, bgu, wd, bd)


def _combine_body(toff_ref, dst_ref, c8_ref, yb_ref, pos_ref, gt_ref, x1_ref, mod_ref, g_ref,
                  o_ref, sbuf, sems):
    i = pl.program_id(0)
    base = lax.rem(i, 2) * TG

    def fetch(step, first_slot):
        for j in range(TG):
            def make(so, do, n, sl=first_slot + j):
                return pltpu.make_async_copy(yb_ref.at[pl.ds(do, n)], sbuf.at[sl, pl.ds(so, n)],
                                             sems.at[sl])
            _start_runs(step * TG + j, toff_ref, dst_ref, c8_ref, make)

    @pl.when(i == 0)
    def _():
        sbuf[...] = jnp.zeros_like(sbuf)
        fetch(i, base)

    @pl.when(i + 1 < pl.num_programs(0))
    def _():
        fetch(i + 1, TG - base)

    r = lax.broadcasted_iota(i32, (RS, TS), 0)
    pms, grows = [], []
    for j in range(TG):
        p = [pos_ref[k:k + 1, j * TS:(j + 1) * TS] for k in range(TOP_K)]
        gt = gt_ref[:, j * TS:(j + 1) * TS]
        m = [r == p[k] for k in range(TOP_K)]
        pm = jnp.where(m[0], 1.0, jnp.where(m[1], 1.0, jnp.where(m[2], 1.0, jnp.where(m[3], 1.0, 0.0))))
        gm = jnp.where(m[0], gt[0:1, :], jnp.where(m[1], gt[1:2, :],
             jnp.where(m[2], gt[2:3, :], jnp.where(m[3], gt[3:4, :], 0.0))))
        pms.append(pm.astype(bf16))
        grows.append(jnp.sum(gm, axis=1, keepdims=True))

    for j in range(TG):
        n = _tile_rows(i * TG + j, toff_ref, c8_ref)
        pltpu.make_async_copy(yb_ref.at[pl.ds(0, n)], sbuf.at[base + j, pl.ds(0, n)],
                              sems.at[base + j]).wait()
    for j in range(TG):
        halves = [lax.dot_general(pms[j], (h * grows[j]).astype(bf16), (((0,), (0,)), ((), ())),
                                  preferred_element_type=f32)
                  for h in _unpack_halves(sbuf[base + j])]
        y = jnp.concatenate(halves, axis=1)
        rows = slice(j * TS, (j + 1) * TS)
        o_ref[rows, :] = x1_ref[rows, :] + mod_ref[0, 5:6, :] * _rms(y, g_ref[...])


def _combine(toff, dst, c8s, yb, pos, gt, x1, mod3, g, tiles_per_batch):
    n, d = x1.shape
    tok = pl.BlockSpec((TG * TS, d), lambda i, *_: (i, 0))
    lane = pl.BlockSpec((TOP_K, TG * TS), lambda i, *_: (0, i))
    steps_per_batch = tiles_per_batch // TG
    return pl.pallas_call(
        _combine_body,
        out_shape=jax.ShapeDtypeStruct((n, d), f32),
        grid_spec=pltpu.PrefetchScalarGridSpec(
            num_scalar_prefetch=3,
            grid=(n // (TG * TS),),
            in_specs=[pl.BlockSpec(memory_space=pl.ANY), lane, lane, tok,
                      pl.BlockSpec((1, 6, d), lambda i, *_: (i // steps_per_batch, 0, 0)),
                      pl.BlockSpec(g.shape, lambda i, *_: (0, 0))],
            out_specs=tok,
            scratch_shapes=[pltpu.VMEM((2 * TG, RS, d // 2), u32), pltpu.SemaphoreType.DMA((2 * TG,))]),
        compiler_params=_params(("arbitrary",)),
        name="combine",
    )(toff, dst, c8s, yb, pos, gt, x1, mod3, g)


def _layout_tables(tab, nt, p_rows):
    c8 = tab[:, :nt].T.astype(i32)
    toff = jnp.cumsum(c8, axis=1) - c8
    len8 = jnp.sum(c8, axis=0)
    seg = (len8 + BM - 1) // BM * BM
    gend = jnp.cumsum(seg)
    gstart = gend - seg
    dst = jnp.cumsum(c8, axis=0) - c8 + gstart[None, :]
    n_used = gend[-1] // BM
    blk = jnp.arange(p_rows // BM, dtype=i32)
    last = jnp.minimum(blk, n_used - 1)
    blk_e = jnp.sum((gend[None, :] <= (last * BM)[:, None]).astype(i32), axis=1)
    blk_e = jnp.minimum(blk_e, N_EXPERTS - 1)
    ids = jnp.arange(N_EXPERTS, dtype=i32)
    later = jnp.logical_and(ids[None, :] > ids[:, None], (seg > 0)[None, :])
    nxt_e = jnp.min(jnp.where(later, ids[None, :], N_EXPERTS), axis=1)
    nxt_e = jnp.where(nxt_e == N_EXPERTS, -1, nxt_e).astype(i32)
    return (toff.reshape(-1), dst.reshape(-1), c8.reshape(-1), gstart + len8, seg - len8,
            blk_e, nxt_e, n_used.reshape(1).astype(i32))


def kernel(x, c, w_ada, b_ada, g_pre_mix, g_post_mix, w_in, rel_bias, sgu_ln_g, sgu_ln_b,
           w_spatial, b_spatial, w_branch_a, w_branch_b, w_gate, b_gate, w_out,
           g_pre_ffn, g_post_ffn, w_router, b_router, w_gate_up, b_gate_up, w_down, b_down):
    b, s, d = x.shape
    assert d == D_MODEL and s % max(TM, TMP, TMX) == 0 and s % (TG * TS) == 0
    n = b * s
    nt = n // TS
    ntp = -(-nt // LANES) * LANES
    p_rows = -(-(n * TOP_K + nt * N_EXPERTS * (ROW_ALIGN - 1) + N_EXPERTS * (BM - 1)) // BM) * BM
    depth = w_ada.shape[0]
    c8 = jnp.pad(c, ((0, 8 - b), (0, 0)))
    row = lambda a: a.reshape(1, -1)

    for l in range(depth):
        mod = _ada(c8, w_ada[l], row(b_ada[l]))[:b]
        mod3 = mod.reshape(b, 6, d)

        q, k, v, u, vv = _proj(x, mod3, row(g_pre_mix[l]), w_in[l].astype(bf16),
                               row(sgu_ln_g[l]), row(sgu_ln_b[l]))
        ya = _attn(q, k, v, _attn_bias(rel_bias[l]))
        ws2 = w_spatial[l].astype(bf16).reshape(-1, 2 * SGU_BLOCK, SGU_BLOCK)
        bsf = jnp.repeat(b_spatial[l].T, SGU_WIDTH // b_spatial.shape[1], axis=1)
        x1, h2, pos, gt, tab = _mix(x, mod3, row(g_pre_mix[l]), row(g_post_mix[l]), u, vv, ya, ws2, bsf,
                                   w_branch_a[l].astype(bf16), w_branch_b[l].astype(bf16),
                                   w_gate[l].astype(bf16), row(b_gate[l]), w_out[l].astype(bf16),
                                   row(g_pre_ffn[l]), w_router[l].T.astype(bf16),
                                   b_router[l].reshape(-1, 1), ntp)
        x1f = x1.reshape(n, d)
        h2 = h2.reshape(n, d)
        toff, dst, c8s, tstart, tlen, blk_e, nxt_e, n_used = _layout_tables(tab, nt, p_rows)
        xb = _dispatch(toff, dst, c8s, tstart, tlen, n_used, h2, pos, p_rows)
        yb = _experts(blk_e, nxt_e, n_used, xb, w_gate_up[l], b_gate_up[l][:, None, :],
                      w_down[l], b_down[l][:, None, :])
        x = _combine(toff, dst, c8s, yb, pos, gt, x1f, mod3, row(g_post_ffn[l]), s // TS).reshape(b, s, d)
    return x
```

```python
import functools

import jax
import jax.numpy as jnp
from jax import lax
from jax.experimental import pallas as pl
from jax.experimental.pallas import tpu as pltpu

bf16 = jnp.bfloat16
f32 = jnp.float32
i32 = jnp.int32
u32 = jnp.uint32

D_MODEL = 1024
CHUNK = 64
N_LEFT = 8
ATT_HEADS = 8
HEAD_DIM = 64
ATT_WIDTH = 512
MAX_REL = 128
SGU_BLOCK = 128
SGU_WIDTH = 512
N_EXPERTS = 32
TOP_K = 4
SWIGLU_LIMIT = 7.0
SWIGLU_ALPHA = 1.702
EPS = 1e-6
NEG = -1e30
LOG2E = 1.4426950408889634

LANES = 128
ROW_ALIGN = 8
TM = 512
TMP = 1024
TMX = 1024
QCH = 4
QG = QCH * CHUNK
KBAND = (N_LEFT + QCH) * CHUNK
TS = 256
TG = 2
RS = TS * TOP_K + N_EXPERTS * ROW_ALIGN
BM = 512
VMEM_LIMIT = 56 * 2**20


def _params(sem):
    return pltpu.CompilerParams(dimension_semantics=sem, vmem_limit_bytes=VMEM_LIMIT)


def _adaln(x, g, sc, sh):
    ms = jnp.mean(x * x, axis=-1, keepdims=True)
    return (x * lax.rsqrt(ms + EPS) * g) * (1.0 + sc) + sh


def _rms(x, g):
    ms = jnp.mean(x * x, axis=-1, keepdims=True)
    return x * lax.rsqrt(ms + EPS) * g


def _sigmoid(x):
    return 1.0 / (1.0 + jnp.exp(-x))


def _ada_body(c_ref, w_ref, b_ref, o_ref):
    c = c_ref[...]
    ca = c * _sigmoid(c)
    o_ref[...] = jnp.dot(ca.astype(bf16), w_ref[...].astype(bf16),
                         preferred_element_type=f32) + b_ref[...]


def _ada(c8, w, b):
    d = w.shape[0]
    n = w.shape[1] // d
    return pl.pallas_call(
        _ada_body,
        out_shape=jax.ShapeDtypeStruct((8, n * d), f32),
        grid=(n,),
        in_specs=[pl.BlockSpec((8, d), lambda j: (0, 0)),
                  pl.BlockSpec((d, d), lambda j: (0, j)),
                  pl.BlockSpec((1, d), lambda j: (0, j))],
        out_specs=pl.BlockSpec((8, d), lambda j: (0, j)),
        compiler_params=_params(("arbitrary",)),
        name="ada",
    )(c8, w, b)


def _proj_body(x_ref, mod_ref, g_ref, w_ref, lng_ref, lnb_ref,
               q_ref, k_ref, v_ref, u_ref, vv_ref):
    h = _adaln(x_ref[0], g_ref[...], mod_ref[0, 1:2, :], mod_ref[0, 0:1, :]).astype(bf16)
    aw = ATT_WIDTH
    z = jnp.dot(h, w_ref[:, 3 * aw:], preferred_element_type=f32)
    p = jnp.dot(h, w_ref[:, :3 * aw], preferred_element_type=f32)
    q_ref[0] = (p[:, 0:aw] * (HEAD_DIM ** -0.5 * LOG2E)).astype(bf16)
    k_ref[0] = p[:, aw:2 * aw].astype(bf16)
    v_ref[0] = p[:, 2 * aw:3 * aw].astype(bf16)
    zg = 0.5 * z * (1.0 + lax.erf(z * (2.0 ** -0.5)))
    u_ref[0] = zg[:, :SGU_WIDTH].astype(bf16)
    vv = zg[:, SGU_WIDTH:]
    mu = jnp.mean(vv, axis=-1, keepdims=True)
    var = jnp.mean(jnp.square(vv - mu), axis=-1, keepdims=True)
    vn = (vv - mu) * lax.rsqrt(var + EPS) * lng_ref[...] + lnb_ref[...]
    vv_ref[0] = vn.astype(bf16)


def _proj(x, mod3, g, w_in, lng, lnb):
    b, s, d = x.shape
    tok = lambda w: pl.BlockSpec((1, TMP, w), lambda bi, i: (bi, i, 0))
    full = lambda a: pl.BlockSpec(a.shape, lambda bi, i: (0,) * a.ndim)
    o512 = jax.ShapeDtypeStruct((b, s, ATT_WIDTH), bf16)
    return pl.pallas_call(
        _proj_body,
        out_shape=(o512,) * 5,
        grid=(b, s // TMP),
        in_specs=[tok(d), pl.BlockSpec((1, 6, d), lambda bi, i: (bi, 0, 0)),
                  full(g), full(w_in), full(lng), full(lnb)],
        out_specs=(tok(ATT_WIDTH),) * 5,
        compiler_params=_params(("parallel", "arbitrary")),
        name="proj",
    )(x, mod3, g, w_in, lng, lnb)


def _attn_body(q_ref, kp_ref, kc_ref, vp_ref, vc_ref, bias_ref, o_ref, kbuf, vbuf):
    first = pl.program_id(1) == 0
    nhp = ATT_HEADS // 2
    ones = jnp.ones((TM, LANES), bf16)

    def fill(rows, k_src, v_src):
        kbuf[rows, :] = k_src[0]
        for hp in range(nhp):
            vbuf[rows, 2 * hp * LANES:(2 * hp + 1) * LANES] = v_src[0, :, hp * LANES:(hp + 1) * LANES]
            vbuf[rows, (2 * hp + 1) * LANES:(2 * hp + 2) * LANES] = ones

    fill(slice(TM, 2 * TM), kc_ref, vc_ref)

    @pl.when(first)
    def _():
        kbuf[0:TM, :] = jnp.zeros((TM, kbuf.shape[1]), bf16)
        vbuf[0:TM, :] = jnp.zeros((TM, vbuf.shape[1]), bf16)

    @pl.when(jnp.logical_not(first))
    def _():
        fill(slice(0, TM), kp_ref, vp_ref)

    lo = lax.broadcasted_iota(i32, (QG, LANES), 1) < HEAD_DIM

    def group(p, carry):
        r0 = pl.multiple_of(p * QG, QG)
        for hp in range(nhp):
            c0 = hp * LANES
            qp = q_ref[0, pl.ds(r0, QG), c0:c0 + LANES]
            zero = jnp.zeros_like(qp)
            q2 = jnp.concatenate([jnp.where(lo, qp, zero), jnp.where(lo, zero, qp)], axis=0)
            kb = kbuf[pl.ds(r0, KBAND), c0:c0 + LANES]
            s = lax.dot_general(q2, kb, (((1,), (1,)), ((), ())), preferred_element_type=f32)
            sb = (s + bias_ref[hp]).astype(bf16)
            m = jnp.max(sb, axis=-1, keepdims=True)
            e = jnp.exp2(sb - m)
            vb = vbuf[pl.ds(r0, KBAND), 2 * c0:2 * c0 + 2 * LANES]
            o2 = jnp.dot(e, vb, preferred_element_type=f32)
            on = o2[:, :LANES] / o2[:, LANES:]
            o = jnp.where(lo, on[:QG], on[QG:])
            o_ref[0, pl.ds(r0, QG), c0:c0 + LANES] = o.astype(bf16)
        return carry

    lax.fori_loop(0, TM // QG, group, 0, unroll=True)


def _attn(q, k, v, bias2):
    b, s, w = q.shape
    cur = pl.BlockSpec((1, TM, w), lambda bi, i: (bi, i, 0))
    prev = pl.BlockSpec((1, TM, w), lambda bi, i: (bi, jnp.maximum(i - 1, 0), 0))
    return pl.pallas_call(
        _attn_body,
        out_shape=jax.ShapeDtypeStruct((b, s, w), bf16),
        grid=(b, s // TM),
        in_specs=[cur, prev, cur, prev, cur,
                  pl.BlockSpec(bias2.shape, lambda bi, i: (0, 0, 0))],
        out_specs=cur,
        scratch_shapes=[pltpu.VMEM((2 * TM, w), bf16), pltpu.VMEM((2 * TM, 2 * w), bf16)],
        compiler_params=_params(("parallel", "arbitrary")),
        name="attn",
    )(q, k, k, v, v, bias2)


def _attn_bias(rel_bias):
    h = rel_bias.shape[0]
    period = 1024
    n_far = N_LEFT * CHUNK - MAX_REL
    assert period >= QG + KBAND - 1 and n_far >= 0
    far = jnp.broadcast_to(rel_bias[:, 2 * MAX_REL:], (h, period))
    near = rel_bias[:, :0:-1]
    v = jnp.concatenate([far[:, :n_far], near, far[:, n_far + 2 * MAX_REL:]], axis=1).astype(f32)
    flat = jnp.broadcast_to(v[:, None, :], (h, QG, period)).reshape(h, QG * period)
    toep = flat[:, :QG * (period - 1)].reshape(h, QG, period - 1)[:, :, :KBAND]
    i = jnp.arange(QG, dtype=i32)[:, None]
    j = jnp.arange(KBAND, dtype=i32)[None, :]
    jb = j - (i // CHUNK) * CHUNK
    valid = jnp.logical_and(jb >= 0, jb < CHUNK * (N_LEFT + 1))
    bias = jnp.where(valid[None], toep * LOG2E, NEG)
    return bias.reshape(ATT_HEADS // 2, 2 * QG, KBAND)


def _route(hb, wr_ref, br_ref):
    lg = lax.dot_general(wr_ref[...], hb, (((1,), (1,)), ((), ())),
                         preferred_element_type=f32) + br_ref[...]
    e_iota = lax.broadcasted_iota(i32, lg.shape, 0)
    vals, idxs = [], []
    hits = jnp.zeros(lg.shape, f32)
    for _ in range(TOP_K):
        m = jnp.max(lg, axis=0, keepdims=True)
        idx = jnp.min(jnp.where(lg == m, e_iota, N_EXPERTS), axis=0, keepdims=True)
        hit = e_iota == idx
        hits = hits + jnp.where(hit, 1.0, 0.0)
        lg = jnp.where(hit, -jnp.inf, lg)
        vals.append(m)
        idxs.append(idx)
    ex = [jnp.exp(v - vals[0]) for v in vals]
    den = ex[0] + ex[1] + ex[2] + ex[3]
    return (jnp.concatenate(idxs, axis=0), jnp.concatenate([e / den for e in ex], axis=0), hits)


def _mix_body(x_ref, mod_ref, gpre_ref, gpost_ref, u_ref, vv_ref, ya_ref, ws_ref, bs_ref,
              wa_ref, wb_ref, wg_ref, bg_ref, wo_ref, gffn_ref, wr_ref, br_ref,
              o_ref, h2_ref, pos_ref, gt_ref, tab_ref, ybuf):
    d = D_MODEL
    x = x_ref[0]
    h = _adaln(x, gpre_ref[...], mod_ref[0, 1:2, :], mod_ref[0, 0:1, :]).astype(bf16)
    gs = _sigmoid(jnp.dot(h, wg_ref[...], preferred_element_type=f32) + bg_ref[...])

    blk = SGU_BLOCK
    row = lax.broadcasted_iota(i32, (2 * blk, blk), 0)
    colv = lax.broadcasted_iota(i32, (2 * blk, blk), 1)
    causal = colv <= jnp.bitwise_and(row, blk - 1)
    lo = lax.broadcasted_iota(i32, (blk, LANES), 1) < (LANES // 2)
    for gp in range(SGU_WIDTH // LANES):
        c0 = gp * LANES
        w2 = ws_ref[gp]
        w2 = jnp.where(causal, w2, jnp.zeros_like(w2))
        for bi in range(TMX // blk):
            r0 = bi * blk
            s2 = jnp.dot(w2, vv_ref[0, r0:r0 + blk, c0:c0 + LANES], preferred_element_type=f32)
            s = jnp.where(lo, s2[:blk], s2[blk:]) + bs_ref[:, c0:c0 + LANES]
            yb = u_ref[0, r0:r0 + blk, c0:c0 + LANES].astype(f32) * s
            ybuf[r0:r0 + blk, c0:c0 + LANES] = yb.astype(bf16)

    a = jnp.dot(ya_ref[0], wa_ref[...], preferred_element_type=f32)
    bb = jnp.dot(ybuf[...], wb_ref[...], preferred_element_type=f32)
    merged = gs[:, :d] * a + gs[:, d:] * bb
    y = jnp.dot(merged.astype(bf16), wo_ref[...], preferred_element_type=f32)
    x1 = x + mod_ref[0, 2:3, :] * _rms(y, gpost_ref[...])
    o_ref[0] = x1

    hb = _adaln(x1, gffn_ref[...], mod_ref[0, 4:5, :], mod_ref[0, 3:4, :]).astype(bf16)
    h2_ref[0] = hb
    ids, gates, hits = _route(hb, wr_ref, br_ref)
    pos_ref[...] = jnp.concatenate(
        [jnp.concatenate(_sorted_positions(ids[:, j * TS:(j + 1) * TS]), axis=0)
         for j in range(TMX // TS)], axis=1)
    gt_ref[...] = gates

    step = pl.program_id(0) * pl.num_programs(1) + pl.program_id(1)

    @pl.when(step == 0)
    def _():
        tab_ref[...] = jnp.zeros_like(tab_ref)

    lane = lax.broadcasted_iota(i32, tab_ref.shape, 1)
    acc = tab_ref[...]
    for j in range(TMX // TS):
        cnt = jnp.sum(hits[:, j * TS:(j + 1) * TS], axis=1, keepdims=True)
        c8 = jnp.floor((cnt + (ROW_ALIGN - 1.0)) * (1.0 / ROW_ALIGN)) * ROW_ALIGN
        acc = acc + jnp.where(lane == step * (TMX // TS) + j, c8, 0.0)
    tab_ref[...] = acc


def _mix(x, mod3, gpre, gpost, u, vv, ya, ws2, bsf, wa, wb, wg, bg, wo, gffn, wrt, br, ntp):
    b, s, d = x.shape
    n = b * s
    tok = lambda w: pl.BlockSpec((1, TMX, w), lambda bi, i: (bi, i, 0))
    full = lambda a: pl.BlockSpec(a.shape, lambda bi, i: (0,) * a.ndim)
    lane = pl.BlockSpec((TOP_K, TMX), lambda bi, i: (0, bi * (s // TMX) + i))
    return pl.pallas_call(
        _mix_body,
        out_shape=(jax.ShapeDtypeStruct((b, s, d), f32),
                   jax.ShapeDtypeStruct((b, s, d), bf16),
                   jax.ShapeDtypeStruct((TOP_K, n), i32),
                   jax.ShapeDtypeStruct((TOP_K, n), f32),
                   jax.ShapeDtypeStruct((N_EXPERTS, ntp), f32)),
        grid=(b, s // TMX),
        in_specs=[tok(d), pl.BlockSpec((1, 6, d), lambda bi, i: (bi, 0, 0)),
                  full(gpre), full(gpost), tok(SGU_WIDTH), tok(SGU_WIDTH), tok(ATT_WIDTH),
                  full(ws2), full(bsf), full(wa), full(wb), full(wg), full(bg), full(wo),
                  full(gffn), full(wrt), full(br)],
        out_specs=(tok(d), tok(d), lane, lane,
                   pl.BlockSpec((N_EXPERTS, ntp), lambda bi, i: (0, 0))),
        scratch_shapes=[pltpu.VMEM((TMX, SGU_WIDTH), bf16)],
        compiler_params=_params(("arbitrary", "arbitrary")),
        name="mix",
    )(x, mod3, gpre, gpost, u, vv, ya, ws2, bsf, wa, wb, wg, bg, wo, gffn, wrt, br)


def _sorted_positions(ti):
    ts = ti.shape[1]
    e_iota = lax.broadcasted_iota(i32, (N_EXPERTS, ts), 0)
    upper = (lax.broadcasted_iota(i32, (ts, ts), 0) < lax.broadcasted_iota(i32, (ts, ts), 1))
    upper = jnp.where(upper, 1.0, 0.0).astype(bf16)
    hits, prefs, cnts = [], [], []
    for k in range(TOP_K):
        hit = ti[k:k + 1, :] == e_iota
        hf = jnp.where(hit, 1.0, 0.0)
        prefs.append(jnp.dot(hf.astype(bf16), upper, preferred_element_type=f32))
        cnts.append(jnp.sum(hf, axis=1, keepdims=True))
        hits.append(hit)
    total = cnts[0] + cnts[1] + cnts[2] + cnts[3]
    c8 = jnp.floor((total + (ROW_ALIGN - 1.0)) * (1.0 / ROW_ALIGN)) * ROW_ALIGN
    lower = (lax.broadcasted_iota(i32, (N_EXPERTS, N_EXPERTS), 1)
             < lax.broadcasted_iota(i32, (N_EXPERTS, N_EXPERTS), 0))
    lower = jnp.where(lower, 1.0, 0.0).astype(bf16)
    c8b = jnp.broadcast_to(c8, (N_EXPERTS, LANES)).astype(bf16)
    start = jnp.dot(lower, c8b, preferred_element_type=f32)[:, 0:1]
    pos = []
    for k in range(TOP_K):
        pe = start + prefs[k]
        pos.append(jnp.sum(jnp.where(hits[k], pe, 0.0), axis=0, keepdims=True).astype(i32))
        start = start + cnts[k]
    return pos


def _pack_halves(x, exact):
    half = x.shape[1] // 2
    lo = lax.bitcast_convert_type(x[:, :half], u32)
    hi = lax.bitcast_convert_type(x[:, half:], u32)
    if not exact:
        hi = jnp.bitwise_and(hi, jnp.uint32(0xFFFF0000))
    return jnp.bitwise_or(lax.shift_right_logical(lo, jnp.uint32(16)), hi)


def _unpack_halves(w):
    lo = lax.bitcast_convert_type(lax.shift_left(w, jnp.uint32(16)), f32)
    hi = lax.bitcast_convert_type(jnp.bitwise_and(w, jnp.uint32(0xFFFF0000)), f32)
    return lo, hi


def _start_runs(i, toff_ref, dst_ref, c8_ref, make):
    unroll = 4

    def some(q, carry):
        for k in range(unroll):
            j = i * N_EXPERTS + q * unroll + k
            n = pl.multiple_of(c8_ref[j], ROW_ALIGN)
            so = pl.multiple_of(toff_ref[j], ROW_ALIGN)
            do = pl.multiple_of(dst_ref[j], ROW_ALIGN)

            @pl.when(n > 0)
            def _():
                make(so, do, n).start(priority=k % 2)
        return carry
    lax.fori_loop(0, N_EXPERTS // unroll, some, 0)


def _tile_rows(i, toff_ref, c8_ref):
    j = i * N_EXPERTS + (N_EXPERTS - 1)
    return pl.multiple_of(toff_ref[j] + c8_ref[j], ROW_ALIGN)


def _dispatch_body(toff_ref, dst_ref, c8_ref, tstart_ref, tlen_ref, nu_ref,
                   h2_ref, pos_ref, xb_ref, sbuf, zbuf, sems, sem):
    i = pl.program_id(0)
    base = lax.rem(i, 2) * TG
    r = lax.broadcasted_iota(i32, (RS, TS), 0)
    for j in range(TG):
        p = [pos_ref[k:k + 1, j * TS:(j + 1) * TS] for k in range(TOP_K)]
        pm = jnp.where(r == p[0], 1.0, jnp.where(r == p[1], 1.0,
             jnp.where(r == p[2], 1.0, jnp.where(r == p[3], 1.0, 0.0))))
        srt = jnp.dot(pm.astype(bf16), h2_ref[j * TS:(j + 1) * TS, :], preferred_element_type=f32)
        sbuf[base + j] = _pack_halves(srt, exact=True)

    for j in range(TG):
        def make(so, do, n, sl=base + j):
            return pltpu.make_async_copy(sbuf.at[sl, pl.ds(so, n)], xb_ref.at[pl.ds(do, n)],
                                         sems.at[sl])
        _start_runs(i * TG + j, toff_ref, dst_ref, c8_ref, make)

    def wait_tile(t, sl):
        n = _tile_rows(t, toff_ref, c8_ref)
        pltpu.make_async_copy(sbuf.at[sl, pl.ds(0, n)], xb_ref.at[pl.ds(0, n)], sems.at[sl]).wait()

    @pl.when(i > 0)
    def _():
        for j in range(TG):
            wait_tile((i - 1) * TG + j, TG - base + j)

    @pl.when(i == pl.num_programs(0) - 1)
    def _():
        for j in range(TG):
            wait_tile(i * TG + j, base + j)
        zbuf[...] = jnp.zeros_like(zbuf)

        def fill(action):
            def tail(e, carry):
                n = pl.multiple_of(tlen_ref[e], ROW_ALIGN)
                do = pl.multiple_of(tstart_ref[e], ROW_ALIGN)

                @pl.when(n > 0)
                def _():
                    action(pltpu.make_async_copy(zbuf.at[pl.ds(0, n)], xb_ref.at[pl.ds(do, n)], sem))
                return carry
            lax.fori_loop(0, N_EXPERTS, tail, 0)

            def unused(b, carry):
                do = pl.multiple_of(b * BM, BM)
                action(pltpu.make_async_copy(zbuf, xb_ref.at[pl.ds(do, BM)], sem))
                return carry
            lax.fori_loop(nu_ref[0], xb_ref.shape[0] // BM, unused, 0)
        fill(lambda c: c.start())
        fill(lambda c: c.wait())


def _dispatch(toff, dst, c8s, tstart, tlen, n_used, h2, pos, p_rows):
    n, d = h2.shape
    return pl.pallas_call(
        _dispatch_body,
        out_shape=jax.ShapeDtypeStruct((p_rows, d // 2), u32),
        grid_spec=pltpu.PrefetchScalarGridSpec(
            num_scalar_prefetch=6,
            grid=(n // (TG * TS),),
            in_specs=[pl.BlockSpec((TG * TS, d), lambda i, *_: (i, 0)),
                      pl.BlockSpec((TOP_K, TG * TS), lambda i, *_: (0, i))],
            out_specs=pl.BlockSpec(memory_space=pl.ANY),
            scratch_shapes=[pltpu.VMEM((2 * TG, RS, d // 2), u32), pltpu.VMEM((BM, d // 2), u32),
                            pltpu.SemaphoreType.DMA((2 * TG,)), pltpu.SemaphoreType.DMA(())]),
        compiler_params=_params(("arbitrary",)),
        name="dispatch",
    )(toff, dst, c8s, tstart, tlen, n_used, h2, pos)


def _expert_body(be_ref, nxt_ref, nu_ref, x_ref, wgu_hbm, bgu_ref, wd_hbm, bd_ref, o_ref,
                 wgu_st, wd_st, wgu_bf, wd_bf, sems):
    b = pl.program_id(0)
    d = D_MODEL
    e = be_ref[b]

    def weight_copies(ex):
        return (pltpu.make_async_copy(wgu_hbm.at[ex], wgu_st, sems.at[0]),
                pltpu.make_async_copy(wd_hbm.at[ex], wd_st, sems.at[1]))

    @pl.when(b >= nu_ref[0])
    def _():
        o_ref[...] = jnp.zeros_like(o_ref)

    @pl.when(b < nu_ref[0])
    def _():
        @pl.when(b == 0)
        def _():
            for c in weight_copies(e):
                c.start()

        @pl.when(jnp.logical_or(b == 0, e != be_ref[jnp.maximum(b - 1, 0)]))
        def _():
            for c in weight_copies(e):
                c.wait()
            wgu_bf[...] = wgu_st[...].astype(bf16)
            wd_bf[...] = wd_st[...].astype(bf16)
            nxt = nxt_ref[e]

            @pl.when(nxt >= 0)
            def _():
                for c in weight_copies(nxt):
                    c.start()

        xb = jnp.concatenate(_unpack_halves(x_ref[...]), axis=1).astype(bf16)
        gu = jnp.dot(xb, wgu_bf[...], preferred_element_type=f32) + bgu_ref[0]
        glu = jnp.minimum(gu[:, :d], SWIGLU_LIMIT)
        lin = jnp.clip(gu[:, d:], -SWIGLU_LIMIT, SWIGLU_LIMIT)
        act = glu * _sigmoid(SWIGLU_ALPHA * glu) * (lin + 1.0)
        y = jnp.dot(act.astype(bf16), wd_bf[...], preferred_element_type=f32) + bd_ref[0]
        o_ref[...] = _pack_halves(y, exact=False)


def _experts(blk_e, nxt_e, n_used, xb, wgu, bgu, wd, bd):
    p_rows, d = xb.shape[0], wd.shape[2]
    rows = lambda w: pl.BlockSpec((BM, w), lambda b, be, nx, nu: (b, 0))
    rows_in = lambda w: pl.BlockSpec((BM, w), lambda b, be, nx, nu: (jnp.minimum(b, nu[0] - 1), 0))
    per_e = lambda a: pl.BlockSpec((1,) + a.shape[1:], lambda b, be, nx, nu: (be[b], 0, 0))
    hbm = pl.BlockSpec(memory_space=pl.ANY)
    return pl.pallas_call(
        _expert_body,
        out_shape=jax.ShapeDtypeStruct((p_rows, d // 2), u32),
        grid_spec=pltpu.PrefetchScalarGridSpec(
            num_scalar_prefetch=3,
            grid=(p_rows // BM,),
            in_specs=[rows_in(xb.shape[1]), hbm, per_e(bgu), hbm, per_e(bd)],
            out_specs=rows(d // 2),
            scratch_shapes=[pltpu.VMEM(wgu.shape[1:], f32), pltpu.VMEM(wd.shape[1:], f32),
                            pltpu.VMEM(wgu.shape[1:], bf16), pltpu.VMEM(wd.shape[1:], bf16),
                            pltpu.SemaphoreType.DMA((2,))]),
        compiler_params=_params(("arbitrary",)),
        name="experts",
    )(blk_e, nxt_e, n_used, xb, wgu, bgu, wd, bd)


def _combine_body(toff_ref, dst_ref, c8_ref, yb_ref, pos_ref, gt_ref, x1_ref, mod_ref, g_ref,
                  o_ref, sbuf, sems):
    i = pl.program_id(0)
    base = lax.rem(i, 2) * TG

    def fetch(step, first_slot):
        for j in range(TG):
            def make(so, do, n, sl=first_slot + j):
                return pltpu.make_async_copy(yb_ref.at[pl.ds(do, n)], sbuf.at[sl, pl.ds(so, n)],
                                             sems.at[sl])
            _start_runs(step * TG + j, toff_ref, dst_ref, c8_ref, make)

    @pl.when(i == 0)
    def _():
        sbuf[...] = jnp.zeros_like(sbuf)
        fetch(i, base)

    @pl.when(i + 1 < pl.num_programs(0))
    def _():
        fetch(i + 1, TG - base)

    r = lax.broadcasted_iota(i32, (RS, TS), 0)
    pms, grows = [], []
    for j in range(TG):
        p = [pos_ref[k:k + 1, j * TS:(j + 1) * TS] for k in range(TOP_K)]
        gt = gt_ref[:, j * TS:(j + 1) * TS]
        m = [r == p[k] for k in range(TOP_K)]
        pm = jnp.where(m[0], 1.0, jnp.where(m[1], 1.0, jnp.where(m[2], 1.0, jnp.where(m[3], 1.0, 0.0))))
        gm = jnp.where(m[0], gt[0:1, :], jnp.where(m[1], gt[1:2, :],
             jnp.where(m[2], gt[2:3, :], jnp.where(m[3], gt[3:4, :], 0.0))))
        pms.append(pm.astype(bf16))
        grows.append(jnp.sum(gm, axis=1, keepdims=True))

    for j in range(TG):
        n = _tile_rows(i * TG + j, toff_ref, c8_ref)
        pltpu.make_async_copy(yb_ref.at[pl.ds(0, n)], sbuf.at[base + j, pl.ds(0, n)],
                              sems.at[base + j]).wait()
    for j in range(TG):
        halves = [lax.dot_general(pms[j], (h * grows[j]).astype(bf16), (((0,), (0,)), ((), ())),
                                  preferred_element_type=f32)
                  for h in _unpack_halves(sbuf[base + j])]
        y = jnp.concatenate(halves, axis=1)
        rows = slice(j * TS, (j + 1) * TS)
        o_ref[rows, :] = x1_ref[rows, :] + mod_ref[0, 5:6, :] * _rms(y, g_ref[...])


def _combine(toff, dst, c8s, yb, pos, gt, x1, mod3, g, tiles_per_batch):
    n, d = x1.shape
    tok = pl.BlockSpec((TG * TS, d), lambda i, *_: (i, 0))
    lane = pl.BlockSpec((TOP_K, TG * TS), lambda i, *_: (0, i))
    steps_per_batch = tiles_per_batch // TG
    return pl.pallas_call(
        _combine_body,
        out_shape=jax.ShapeDtypeStruct((n, d), f32),
        grid_spec=pltpu.PrefetchScalarGridSpec(
            num_scalar_prefetch=3,
            grid=(n // (TG * TS),),
            in_specs=[pl.BlockSpec(memory_space=pl.ANY), lane, lane, tok,
                      pl.BlockSpec((1, 6, d), lambda i, *_: (i // steps_per_batch, 0, 0)),
                      pl.BlockSpec(g.shape, lambda i, *_: (0, 0))],
            out_specs=tok,
            scratch_shapes=[pltpu.VMEM((2 * TG, RS, d // 2), u32), pltpu.SemaphoreType.DMA((2 * TG,))]),
        compiler_params=_params(("arbitrary",)),
        name="combine",
    )(toff, dst, c8s, yb, pos, gt, x1, mod3, g)


def _layout_tables(tab, nt, p_rows):
    c8 = tab[:, :nt].T.astype(i32)
    toff = jnp.cumsum(c8, axis=1) - c8
    len8 = jnp.sum(c8, axis=0)
    seg = (len8 + BM - 1) // BM * BM
    gend = jnp.cumsum(seg)
    gstart = gend - seg
    dst = jnp.cumsum(c8, axis=0) - c8 + gstart[None, :]
    n_used = gend[-1] // BM
    blk = jnp.arange(p_rows // BM, dtype=i32)
    last = jnp.minimum(blk, n_used - 1)
    blk_e = jnp.sum((gend[None, :] <= (last * BM)[:, None]).astype(i32), axis=1)
    blk_e = jnp.minimum(blk_e, N_EXPERTS - 1)
    ids = jnp.arange(N_EXPERTS, dtype=i32)
    later = jnp.logical_and(ids[None, :] > ids[:, None], (seg > 0)[None, :])
    nxt_e = jnp.min(jnp.where(later, ids[None, :], N_EXPERTS), axis=1)
    nxt_e = jnp.where(nxt_e == N_EXPERTS, -1, nxt_e).astype(i32)
    return (toff.reshape(-1), dst.reshape(-1), c8.reshape(-1), gstart + len8, seg - len8,
            blk_e, nxt_e, n_used.reshape(1).astype(i32))


def kernel(x, c, w_ada, b_ada, g_pre_mix, g_post_mix, w_in, rel_bias, sgu_ln_g, sgu_ln_b,
           w_spatial, b_spatial, w_branch_a, w_branch_b, w_gate, b_gate, w_out,
           g_pre_ffn, g_post_ffn, w_router, b_router, w_gate_up, b_gate_up, w_down, b_down):
    b, s, d = x.shape
    assert d == D_MODEL and s % max(TM, TMP, TMX) == 0 and s % (TG * TS) == 0
    n = b * s
    nt = n // TS
    ntp = -(-nt // LANES) * LANES
    p_rows = -(-(n * TOP_K + nt * N_EXPERTS * (ROW_ALIGN - 1) + N_EXPERTS * (BM - 1)) // BM) * BM
    depth = w_ada.shape[0]
    c8 = jnp.pad(c, ((0, 8 - b), (0, 0)))
    row = lambda a: a.reshape(1, -1)

    for l in range(depth):
        mod = _ada(c8, w_ada[l], row(b_ada[l]))[:b]
        mod3 = mod.reshape(b, 6, d)

        q, k, v, u, vv = _proj(x, mod3, row(g_pre_mix[l]), w_in[l].astype(bf16),
                               row(sgu_ln_g[l]), row(sgu_ln_b[l]))
        ya = _attn(q, k, v, _attn_bias(rel_bias[l]))
        ws2 = w_spatial[l].astype(bf16).reshape(-1, 2 * SGU_BLOCK, SGU_BLOCK)
        bsf = jnp.repeat(b_spatial[l].T, SGU_WIDTH // b_spatial.shape[1], axis=1)
        x1, h2, pos, gt, tab = _mix(x, mod3, row(g_pre_mix[l]), row(g_post_mix[l]), u, vv, ya, ws2, bsf,
                                   w_branch_a[l].astype(bf16), w_branch_b[l].astype(bf16),
                                   w_gate[l].astype(bf16), row(b_gate[l]), w_out[l].astype(bf16),
                                   row(g_pre_ffn[l]), w_router[l].T.astype(bf16),
                                   b_router[l].reshape(-1, 1), ntp)
        x1f = x1.reshape(n, d)
        h2 = h2.reshape(n, d)
        toff, dst, c8s, tstart, tlen, blk_e, nxt_e, n_used = _layout_tables(tab, nt, p_rows)
        xb = _dispatch(toff, dst, c8s, tstart, tlen, n_used, h2, pos, p_rows)
        yb = _experts(blk_e, nxt_e, n_used, xb, w_gate_up[l], b_gate_up[l][:, None, :],
                      w_down[l], b_down[l][:, None, :])
        x = _combine(toff, dst, c8s, yb, pos, gt, x1f, mod3, row(g_post_ffn[l]), s // TS).reshape(b, s, d)
    return x
```

```python
import functools

import jax
import jax.numpy as jnp
from jax import lax
from jax.experimental import pallas as pl
from jax.experimental.pallas import tpu as pltpu

bf16 = jnp.bfloat16
f32 = jnp.float32
i32 = jnp.int32
u32 = jnp.uint32

D_MODEL = 1024
CHUNK = 64
N_LEFT = 8
ATT_HEADS = 8
HEAD_DIM = 64
ATT_WIDTH = 512
MAX_REL = 128
SGU_BLOCK = 128
SGU_WIDTH = 512
N_EXPERTS = 32
TOP_K = 4
SWIGLU_LIMIT = 7.0
SWIGLU_ALPHA = 1.702
EPS = 1e-6
NEG = -1e30
LOG2E = 1.4426950408889634

LANES = 128
ROW_ALIGN = 8
TM = 512
TMP = 1024
TMX = 1024
QCH = 4
QG = QCH * CHUNK
KBAND = (N_LEFT + QCH) * CHUNK
TS = 256
TG = 2
RS = TS * TOP_K + N_EXPERTS * ROW_ALIGN
BM = 512
VMEM_LIMIT = 56 * 2**20


def _params(sem):
    return pltpu.CompilerParams(dimension_semantics=sem, vmem_limit_bytes=VMEM_LIMIT)


def _adaln(x, g, sc, sh):
    ms = jnp.mean(x * x, axis=-1, keepdims=True)
    return (x * lax.rsqrt(ms + EPS) * g) * (1.0 + sc) + sh


def _rms(x, g):
    ms = jnp.mean(x * x, axis=-1, keepdims=True)
    return x * lax.rsqrt(ms + EPS) * g


def _sigmoid(x):
    return 1.0 / (1.0 + jnp.exp(-x))


def _ada_body(c_ref, w_ref, b_ref, o_ref):
    c = c_ref[...]
    ca = c * _sigmoid(c)
    o_ref[...] = jnp.dot(ca.astype(bf16), w_ref[...].astype(bf16),
                         preferred_element_type=f32) + b_ref[...]


def _ada(c8, w, b):
    d = w.shape[0]
    n = w.shape[1] // d
    return pl.pallas_call(
        _ada_body,
        out_shape=jax.ShapeDtypeStruct((8, n * d), f32),
        grid=(n,),
        in_specs=[pl.BlockSpec((8, d), lambda j: (0, 0)),
                  pl.BlockSpec((d, d), lambda j: (0, j)),
                  pl.BlockSpec((1, d), lambda j: (0, j))],
        out_specs=pl.BlockSpec((8, d), lambda j: (0, j)),
        compiler_params=_params(("arbitrary",)),
        name="ada",
    )(c8, w, b)


def _proj_body(x_ref, mod_ref, g_ref, w_ref, lng_ref, lnb_ref,
               q_ref, k_ref, v_ref, u_ref, vv_ref):
    h = _adaln(x_ref[0], g_ref[...], mod_ref[0, 1:2, :], mod_ref[0, 0:1, :]).astype(bf16)
    aw = ATT_WIDTH
    z = jnp.dot(h, w_ref[:, 3 * aw:], preferred_element_type=f32)
    p = jnp.dot(h, w_ref[:, :3 * aw], preferred_element_type=f32)
    q_ref[0] = (p[:, 0:aw] * (HEAD_DIM ** -0.5 * LOG2E)).astype(bf16)
    k_ref[0] = p[:, aw:2 * aw].astype(bf16)
    v_ref[0] = p[:, 2 * aw:3 * aw].astype(bf16)
    zg = 0.5 * z * (1.0 + lax.erf(z * (2.0 ** -0.5)))
    u_ref[0] = zg[:, :SGU_WIDTH].astype(bf16)
    vv = zg[:, SGU_WIDTH:]
    mu = jnp.mean(vv, axis=-1, keepdims=True)
    var = jnp.mean(jnp.square(vv - mu), axis=-1, keepdims=True)
    vn = (vv - mu) * lax.rsqrt(var + EPS) * lng_ref[...] + lnb_ref[...]
    vv_ref[0] = vn.astype(bf16)


def _proj(x, mod3, g, w_in, lng, lnb):
    b, s, d = x.shape
    tok = lambda w: pl.BlockSpec((1, TMP, w), lambda bi, i: (bi, i, 0))
    full = lambda a: pl.BlockSpec(a.shape, lambda bi, i: (0,) * a.ndim)
    o512 = jax.ShapeDtypeStruct((b, s, ATT_WIDTH), bf16)
    return pl.pallas_call(
        _proj_body,
        out_shape=(o512,) * 5,
        grid=(b, s // TMP),
        in_specs=[tok(d), pl.BlockSpec((1, 6, d), lambda bi, i: (bi, 0, 0)),
                  full(g), full(w_in), full(lng), full(lnb)],
        out_specs=(tok(ATT_WIDTH),) * 5,
        compiler_params=_params(("parallel", "arbitrary")),
        name="proj",
    )(x, mod3, g, w_in, lng, lnb)


def _attn_body(q_ref, kp_ref, kc_ref, vp_ref, vc_ref, bias_ref, o_ref, kbuf, vbuf):
    first = pl.program_id(1) == 0
    nhp = ATT_HEADS // 2
    ones = jnp.ones((TM, LANES), bf16)

    def fill(rows, k_src, v_src):
        kbuf[rows, :] = k_src[0]
        for hp in range(nhp):
            vbuf[rows, 2 * hp * LANES:(2 * hp + 1) * LANES] = v_src[0, :, hp * LANES:(hp + 1) * LANES]
            vbuf[rows, (2 * hp + 1) * LANES:(2 * hp + 2) * LANES] = ones

    fill(slice(TM, 2 * TM), kc_ref, vc_ref)

    @pl.when(first)
    def _():
        kbuf[0:TM, :] = jnp.zeros((TM, kbuf.shape[1]), bf16)
        vbuf[0:TM, :] = jnp.zeros((TM, vbuf.shape[1]), bf16)

    @pl.when(jnp.logical_not(first))
    def _():
        fill(slice(0, TM), kp_ref, vp_ref)

    lo = lax.broadcasted_iota(i32, (QG, LANES), 1) < HEAD_DIM

    def group(p, carry):
        r0 = pl.multiple_of(p * QG, QG)
        for hp in range(nhp):
            c0 = hp * LANES
            qp = q_ref[0, pl.ds(r0, QG), c0:c0 + LANES]
            zero = jnp.zeros_like(qp)
            q2 = jnp.concatenate([jnp.where(lo, qp, zero), jnp.where(lo, zero, qp)], axis=0)
            kb = kbuf[pl.ds(r0, KBAND), c0:c0 + LANES]
            s = lax.dot_general(q2, kb, (((1,), (1,)), ((), ())), preferred_element_type=f32)
            sb = (s + bias_ref[hp]).astype(bf16)
            m = jnp.max(sb, axis=-1, keepdims=True)
            e = jnp.exp2(sb - m)
            vb = vbuf[pl.ds(r0, KBAND), 2 * c0:2 * c0 + 2 * LANES]
            o2 = jnp.dot(e, vb, preferred_element_type=f32)
            on = o2[:, :LANES] / o2[:, LANES:]
            o = jnp.where(lo, on[:QG], on[QG:])
            o_ref[0, pl.ds(r0, QG), c0:c0 + LANES] = o.astype(bf16)
        return carry

    lax.fori_loop(0, TM // QG, group, 0, unroll=True)


def _attn(q, k, v, bias2):
    b, s, w = q.shape
    cur = pl.BlockSpec((1, TM, w), lambda bi, i: (bi, i, 0))
    prev = pl.BlockSpec((1, TM, w), lambda bi, i: (bi, jnp.maximum(i - 1, 0), 0))
    return pl.pallas_call(
        _attn_body,
        out_shape=jax.ShapeDtypeStruct((b, s, w), bf16),
        grid=(b, s // TM),
        in_specs=[cur, prev, cur, prev, cur,
                  pl.BlockSpec(bias2.shape, lambda bi, i: (0, 0, 0))],
        out_specs=cur,
        scratch_shapes=[pltpu.VMEM((2 * TM, w), bf16), pltpu.VMEM((2 * TM, 2 * w), bf16)],
        compiler_params=_params(("parallel", "arbitrary")),
        name="attn",
    )(q, k, k, v, v, bias2)


def _attn_bias(rel_bias):
    h = rel_bias.shape[0]
    period = 1024
    n_far = N_LEFT * CHUNK - MAX_REL
    assert period >= QG + KBAND - 1 and n_far >= 0
    far = jnp.broadcast_to(rel_bias[:, 2 * MAX_REL:], (h, period))
    near = rel_bias[:, :0:-1]
    v = jnp.concatenate([far[:, :n_far], near, far[:, n_far + 2 * MAX_REL:]], axis=1).astype(f32)
    flat = jnp.broadcast_to(v[:, None, :], (h, QG, period)).reshape(h, QG * period)
    toep = flat[:, :QG * (period - 1)].reshape(h, QG, period - 1)[:, :, :KBAND]
    i = jnp.arange(QG, dtype=i32)[:, None]
    j = jnp.arange(KBAND, dtype=i32)[None, :]
    jb = j - (i // CHUNK) * CHUNK
    valid = jnp.logical_and(jb >= 0, jb < CHUNK * (N_LEFT + 1))
    bias = jnp.where(valid[None], toep * LOG2E, NEG)
    return bias.reshape(ATT_HEADS // 2, 2 * QG, KBAND)


def _route(hb, wr_ref, br_ref):
    lg = lax.dot_general(wr_ref[...], hb, (((1,), (1,)), ((), ())),
                         preferred_element_type=f32) + br_ref[...]
    e_iota = lax.broadcasted_iota(i32, lg.shape, 0)
    vals, idxs = [], []
    hits = jnp.zeros(lg.shape, f32)
    for _ in range(TOP_K):
        m = jnp.max(lg, axis=0, keepdims=True)
        idx = jnp.min(jnp.where(lg == m, e_iota, N_EXPERTS), axis=0, keepdims=True)
        hit = e_iota == idx
        hits = hits + jnp.where(hit, 1.0, 0.0)
        lg = jnp.where(hit, -jnp.inf, lg)
        vals.append(m)
        idxs.append(idx)
    ex = [jnp.exp(v - vals[0]) for v in vals]
    den = ex[0] + ex[1] + ex[2] + ex[3]
    return (jnp.concatenate(idxs, axis=0), jnp.concatenate([e / den for e in ex], axis=0), hits)


def _mix_body(x_ref, mod_ref, gpre_ref, gpost_ref, u_ref, vv_ref, ya_ref, ws_ref, bs_ref,
              wa_ref, wb_ref, wg_ref, bg_ref, wo_ref, gffn_ref, wr_ref, br_ref,
              o_ref, h2_ref, pos_ref, gt_ref, tab_ref, ybuf):
    d = D_MODEL
    x = x_ref[0]
    h = _adaln(x, gpre_ref[...], mod_ref[0, 1:2, :], mod_ref[0, 0:1, :]).astype(bf16)
    gs = _sigmoid(jnp.dot(h, wg_ref[...], preferred_element_type=f32) + bg_ref[...])

    blk = SGU_BLOCK
    row = lax.broadcasted_iota(i32, (2 * blk, blk), 0)
    colv = lax.broadcasted_iota(i32, (2 * blk, blk), 1)
    causal = colv <= jnp.bitwise_and(row, blk - 1)
    lo = lax.broadcasted_iota(i32, (blk, LANES), 1) < (LANES // 2)
    for gp in range(SGU_WIDTH // LANES):
        c0 = gp * LANES
        w2 = ws_ref[gp]
        w2 = jnp.where(causal, w2, jnp.zeros_like(w2))
        for bi in range(TMX // blk):
            r0 = bi * blk
            s2 = jnp.dot(w2, vv_ref[0, r0:r0 + blk, c0:c0 + LANES], preferred_element_type=f32)
            s = jnp.where(lo, s2[:blk], s2[blk:]) + bs_ref[:, c0:c0 + LANES]
            yb = u_ref[0, r0:r0 + blk, c0:c0 + LANES].astype(f32) * s
            ybuf[r0:r0 + blk, c0:c0 + LANES] = yb.astype(bf16)

    a = jnp.dot(ya_ref[0], wa_ref[...], preferred_element_type=f32)
    bb = jnp.dot(ybuf[...], wb_ref[...], preferred_element_type=f32)
    merged = gs[:, :d] * a + gs[:, d:] * bb
    y = jnp.dot(merged.astype(bf16), wo_ref[...], preferred_element_type=f32)
    x1 = x + mod_ref[0, 2:3, :] * _rms(y, gpost_ref[...])
    o_ref[0] = x1

    hb = _adaln(x1, gffn_ref[...], mod_ref[0, 4:5, :], mod_ref[0, 3:4, :]).astype(bf16)
    h2_ref[0] = hb
    ids, gates, hits = _route(hb, wr_ref, br_ref)
    pos_ref[...] = jnp.concatenate(
        [jnp.concatenate(_sorted_positions(ids[:, j * TS:(j + 1) * TS]), axis=0)
         for j in range(TMX // TS)], axis=1)
    gt_ref[...] = gates

    step = pl.program_id(0) * pl.num_programs(1) + pl.program_id(1)

    @pl.when(step == 0)
    def _():
        tab_ref[...] = jnp.zeros_like(tab_ref)

    lane = lax.broadcasted_iota(i32, tab_ref.shape, 1)
    acc = tab_ref[...]
    for j in range(TMX // TS):
        cnt = jnp.sum(hits[:, j * TS:(j + 1) * TS], axis=1, keepdims=True)
        c8 = jnp.floor((cnt + (ROW_ALIGN - 1.0)) * (1.0 / ROW_ALIGN)) * ROW_ALIGN
        acc = acc + jnp.where(lane == step * (TMX // TS) + j, c8, 0.0)
    tab_ref[...] = acc


def _mix(x, mod3, gpre, gpost, u, vv, ya, ws2, bsf, wa, wb, wg, bg, wo, gffn, wrt, br, ntp):
    b, s, d = x.shape
    n = b * s
    tok = lambda w: pl.BlockSpec((1, TMX, w), lambda bi, i: (bi, i, 0))
    full = lambda a: pl.BlockSpec(a.shape, lambda bi, i: (0,) * a.ndim)
    lane = pl.BlockSpec((TOP_K, TMX), lambda bi, i: (0, bi * (s // TMX) + i))
    return pl.pallas_call(
        _mix_body,
        out_shape=(jax.ShapeDtypeStruct((b, s, d), f32),
                   jax.ShapeDtypeStruct((b, s, d), bf16),
                   jax.ShapeDtypeStruct((TOP_K, n), i32),
                   jax.ShapeDtypeStruct((TOP_K, n), f32),
                   jax.ShapeDtypeStruct((N_EXPERTS, ntp), f32)),
        grid=(b, s // TMX),
        in_specs=[tok(d), pl.BlockSpec((1, 6, d), lambda bi, i: (bi, 0, 0)),
                  full(gpre), full(gpost), tok(SGU_WIDTH), tok(SGU_WIDTH), tok(ATT_WIDTH),
                  full(ws2), full(bsf), full(wa), full(wb), full(wg), full(bg), full(wo),
                  full(gffn), full(wrt), full(br)],
        out_specs=(tok(d), tok(d), lane, lane,
                   pl.BlockSpec((N_EXPERTS, ntp), lambda bi, i: (0, 0))),
        scratch_shapes=[pltpu.VMEM((TMX, SGU_WIDTH), bf16)],
        compiler_params=_params(("arbitrary", "arbitrary")),
        name="mix",
    )(x, mod3, gpre, gpost, u, vv, ya, ws2, bsf, wa, wb, wg, bg, wo, gffn, wrt, br)


def _sorted_positions(ti):
    ts = ti.shape[1]
    e_iota = lax.broadcasted_iota(i32, (N_EXPERTS, ts), 0)
    upper = (lax.broadcasted_iota(i32, (ts, ts), 0) < lax.broadcasted_iota(i32, (ts, ts), 1))
    upper = jnp.where(upper, 1.0, 0.0).astype(bf16)
    hits, prefs, cnts = [], [], []
    for k in range(TOP_K):
        hit = ti[k:k + 1, :] == e_iota
        hf = jnp.where(hit, 1.0, 0.0)
        prefs.append(jnp.dot(hf.astype(bf16), upper, preferred_element_type=f32))
        cnts.append(jnp.sum(hf, axis=1, keepdims=True))
        hits.append(hit)
    total = cnts[0] + cnts[1] + cnts[2] + cnts[3]
    c8 = jnp.floor((total + (ROW_ALIGN - 1.0)) * (1.0 / ROW_ALIGN)) * ROW_ALIGN
    lower = (lax.broadcasted_iota(i32, (N_EXPERTS, N_EXPERTS), 1)
             < lax.broadcasted_iota(i32, (N_EXPERTS, N_EXPERTS), 0))
    lower = jnp.where(lower, 1.0, 0.0).astype(bf16)
    c8b = jnp.broadcast_to(c8, (N_EXPERTS, LANES)).astype(bf16)
    start = jnp.dot(lower, c8b, preferred_element_type=f32)[:, 0:1]
    pos = []
    for k in range(TOP_K):
        pe = start + prefs[k]
        pos.append(jnp.sum(jnp.where(hits[k], pe, 0.0), axis=0, keepdims=True).astype(i32))
        start = start + cnts[k]
    return pos


def _pack_halves(x, exact):
    half = x.shape[1] // 2
    lo = lax.bitcast_convert_type(x[:, :half], u32)
    hi = lax.bitcast_convert_type(x[:, half:], u32)
    if not exact:
        hi = jnp.bitwise_and(hi, jnp.uint32(0xFFFF0000))
    return jnp.bitwise_or(lax.shift_right_logical(lo, jnp.uint32(16)), hi)


def _unpack_halves(w):
    lo = lax.bitcast_convert_type(lax.shift_left(w, jnp.uint32(16)), f32)
    hi = lax.bitcast_convert_type(jnp.bitwise_and(w, jnp.uint32(0xFFFF0000)), f32)
    return lo, hi


def _start_runs(i, toff_ref, dst_ref, c8_ref, make):
    unroll = 4

    def some(q, carry):
        for k in range(unroll):
            j = i * N_EXPERTS + q * unroll + k
            n = pl.multiple_of(c8_ref[j], ROW_ALIGN)
            so = pl.multiple_of(toff_ref[j], ROW_ALIGN)
            do = pl.multiple_of(dst_ref[j], ROW_ALIGN)

            @pl.when(n > 0)
            def _():
                make(so, do, n).start(priority=k % 2)
        return carry
    lax.fori_loop(0, N_EXPERTS // unroll, some, 0)


def _tile_rows(i, toff_ref, c8_ref):
    j = i * N_EXPERTS + (N_EXPERTS - 1)
    return pl.multiple_of(toff_ref[j] + c8_ref[j], ROW_ALIGN)


def _dispatch_body(toff_ref, dst_ref, c8_ref, tstart_ref, tlen_ref, nu_ref,
                   h2_ref, pos_ref, xb_ref, sbuf, zbuf, sems, sem):
    i = pl.program_id(0)
    base = lax.rem(i, 2) * TG
    r = lax.broadcasted_iota(i32, (RS, TS), 0)
    for j in range(TG):
        p = [pos_ref[k:k + 1, j * TS:(j + 1) * TS] for k in range(TOP_K)]
        pm = jnp.where(r == p[0], 1.0, jnp.where(r == p[1], 1.0,
             jnp.where(r == p[2], 1.0, jnp.where(r == p[3], 1.0, 0.0))))
        srt = jnp.dot(pm.astype(bf16), h2_ref[j * TS:(j + 1) * TS, :], preferred_element_type=f32)
        sbuf[base + j] = _pack_halves(srt, exact=True)

    for j in range(TG):
        def make(so, do, n, sl=base + j):
            return pltpu.make_async_copy(sbuf.at[sl, pl.ds(so, n)], xb_ref.at[pl.ds(do, n)],
                                         sems.at[sl])
        _start_runs(i * TG + j, toff_ref, dst_ref, c8_ref, make)

    def wait_tile(t, sl):
        n = _tile_rows(t, toff_ref, c8_ref)
        pltpu.make_async_copy(sbuf.at[sl, pl.ds(0, n)], xb_ref.at[pl.ds(0, n)], sems.at[sl]).wait()

    @pl.when(i > 0)
    def _():
        for j in range(TG):
            wait_tile((i - 1) * TG + j, TG - base + j)

    @pl.when(i == pl.num_programs(0) - 1)
    def _():
        for j in range(TG):
            wait_tile(i * TG + j, base + j)
        zbuf[...] = jnp.zeros_like(zbuf)

        def fill(action):
            def tail(e, carry):
                n = pl.multiple_of(tlen_ref[e], ROW_ALIGN)
                do = pl.multiple_of(tstart_ref[e], ROW_ALIGN)

                @pl.when(n > 0)
                def _():
                    action(pltpu.make_async_copy(zbuf.at[pl.ds(0, n)], xb_ref.at[pl.ds(do, n)], sem))
                return carry
            lax.fori_loop(0, N_EXPERTS, tail, 0)

            def unused(b, carry):
                do = pl.multiple_of(b * BM, BM)
                action(pltpu.make_async_copy(zbuf, xb_ref.at[pl.ds(do, BM)], sem))
                return carry
            lax.fori_loop(nu_ref[0], xb_ref.shape[0] // BM, unused, 0)
        fill(lambda c: c.start())
        fill(lambda c: c.wait())


def _dispatch(toff, dst, c8s, tstart, tlen, n_used, h2, pos, p_rows):
    n, d = h2.shape
    return pl.pallas_call(
        _dispatch_body,
        out_shape=jax.ShapeDtypeStruct((p_rows, d // 2), u32),
        grid_spec=pltpu.PrefetchScalarGridSpec(
            num_scalar_prefetch=6,
            grid=(n // (TG * TS),),
            in_specs=[pl.BlockSpec((TG * TS, d), lambda i, *_: (i, 0)),
                      pl.BlockSpec((TOP_K, TG * TS), lambda i, *_: (0, i))],
            out_specs=pl.BlockSpec(memory_space=pl.ANY),
            scratch_shapes=[pltpu.VMEM((2 * TG, RS, d // 2), u32), pltpu.VMEM((BM, d // 2), u32),
                            pltpu.SemaphoreType.DMA((2 * TG,)), pltpu.SemaphoreType.DMA(())]),
        compiler_params=_params(("arbitrary",)),
        name="dispatch",
    )(toff, dst, c8s, tstart, tlen, n_used, h2, pos)


def _expert_body(be_ref, nxt_ref, nu_ref, x_ref, wgu_hbm, bgu_ref, wd_hbm, bd_ref, o_ref,
                 wgu_st, wd_st, wgu_bf, wd_bf, sems):
    del nu_ref
    b = pl.program_id(0)
    d = D_MODEL
    e = be_ref[b]

    def weight_copies(ex):
        return (pltpu.make_async_copy(wgu_hbm.at[ex], wgu_st, sems.at[0]),
                pltpu.make_async_copy(wd_hbm.at[ex], wd_st, sems.at[1]))

    @pl.when(b == 0)
    def _():
        for c in weight_copies(e):
            c.start()

    @pl.when(jnp.logical_or(b == 0, e != be_ref[jnp.maximum(b - 1, 0)]))
    def _():
        for c in weight_copies(e):
            c.wait()
        wgu_bf[...] = wgu_st[...].astype(bf16)
        wd_bf[...] = wd_st[...].astype(bf16)
        nxt = nxt_ref[e]

        @pl.when(nxt >= 0)
        def _():
            for c in weight_copies(nxt):
                c.start()

    xb = jnp.concatenate(_unpack_halves(x_ref[...]), axis=1).astype(bf16)
    gu = jnp.dot(xb, wgu_bf[...], preferred_element_type=f32) + bgu_ref[0]
    glu = jnp.minimum(gu[:, :d], SWIGLU_LIMIT)
    lin = jnp.clip(gu[:, d:], -SWIGLU_LIMIT, SWIGLU_LIMIT)
    act = glu * _sigmoid(SWIGLU_ALPHA * glu) * (lin + 1.0)
    y = jnp.dot(act.astype(bf16), wd_bf[...], preferred_element_type=f32) + bd_ref[0]
    o_ref[...] = _pack_halves(y, exact=False)


def _experts(blk_e, nxt_e, n_used, xb, wgu, bgu, wd, bd):
    rows = pl.BlockSpec((BM, xb.shape[1]), lambda b, be, nx, nu: (b, 0))
    per_e = lambda a: pl.BlockSpec((1,) + a.shape[1:], lambda b, be, nx, nu: (be[b], 0, 0))
    hbm = pl.BlockSpec(memory_space=pl.ANY)
    return pl.pallas_call(
        _expert_body,
        out_shape=jax.ShapeDtypeStruct(xb.shape, xb.dtype),
        grid_spec=pltpu.PrefetchScalarGridSpec(
            num_scalar_prefetch=3,
            grid=(n_used[0],),
            in_specs=[rows, hbm, per_e(bgu), hbm, per_e(bd)],
            out_specs=rows,
            scratch_shapes=[pltpu.VMEM(wgu.shape[1:], f32), pltpu.VMEM(wd.shape[1:], f32),
                            pltpu.VMEM(wgu.shape[1:], bf16), pltpu.VMEM(wd.shape[1:], bf16),
                            pltpu.SemaphoreType.DMA((2,))]),
        input_output_aliases={3: 0},
        compiler_params=_params(("arbitrary",)),
        name="experts",
    )(blk_e, nxt_e, n_used, xb, wgu, bgu, wd, bd)


def _combine_body(toff_ref, dst_ref, c8_ref, yb_ref, pos_ref, gt_ref, x1_ref, mod_ref, g_ref,
                  o_ref, sbuf, sems):
    i = pl.program_id(0)
    base = lax.rem(i, 2) * TG

    def fetch(step, first_slot):
        for j in range(TG):
            def make(so, do, n, sl=first_slot + j):
                return pltpu.make_async_copy(yb_ref.at[pl.ds(do, n)], sbuf.at[sl, pl.ds(so, n)],
                                             sems.at[sl])
            _start_runs(step * TG + j, toff_ref, dst_ref, c8_ref, make)

    @pl.when(i == 0)
    def _():
        sbuf[...] = jnp.zeros_like(sbuf)
        fetch(i, base)

    @pl.when(i + 1 < pl.num_programs(0))
    def _():
        fetch(i + 1, TG - base)

    r = lax.broadcasted_iota(i32, (RS, TS), 0)
    pms, grows = [], []
    for j in range(TG):
        p = [pos_ref[k:k + 1, j * TS:(j + 1) * TS] for k in range(TOP_K)]
        gt = gt_ref[:, j * TS:(j + 1) * TS]
        m = [r == p[k] for k in range(TOP_K)]
        pm = jnp.where(m[0], 1.0, jnp.where(m[1], 1.0, jnp.where(m[2], 1.0, jnp.where(m[3], 1.0, 0.0))))
        gm = jnp.where(m[0], gt[0:1, :], jnp.where(m[1], gt[1:2, :],
             jnp.where(m[2], gt[2:3, :], jnp.where(m[3], gt[3:4, :], 0.0))))
        pms.append(pm.astype(bf16))
        grows.append(jnp.sum(gm, axis=1, keepdims=True))

    for j in range(TG):
        n = _tile_rows(i * TG + j, toff_ref, c8_ref)
        pltpu.make_async_copy(yb_ref.at[pl.ds(0, n)], sbuf.at[base + j, pl.ds(0, n)],
                              sems.at[base + j]).wait()
    for j in range(TG):
        halves = [lax.dot_general(pms[j], (h * grows[j]).astype(bf16), (((0,), (0,)), ((), ())),
                                  preferred_element_type=f32)
                  for h in _unpack_halves(sbuf[base + j])]
        y = jnp.concatenate(halves, axis=1)
        rows = slice(j * TS, (j + 1) * TS)
        o_ref[rows, :] = x1_ref[rows, :] + mod_ref[0, 5:6, :] * _rms(y, g_ref[...])


def _combine(toff, dst, c8s, yb, pos, gt, x1, mod3, g, tiles_per_batch):
    n, d = x1.shape
    tok = pl.BlockSpec((TG * TS, d), lambda i, *_: (i, 0))
    lane = pl.BlockSpec((TOP_K, TG * TS), lambda i, *_: (0, i))
    steps_per_batch = tiles_per_batch // TG
    return pl.pallas_call(
        _combine_body,
        out_shape=jax.ShapeDtypeStruct((n, d), f32),
        grid_spec=pltpu.PrefetchScalarGridSpec(
            num_scalar_prefetch=3,
            grid=(n // (TG * TS),),
            in_specs=[pl.BlockSpec(memory_space=pl.ANY), lane, lane, tok,
                      pl.BlockSpec((1, 6, d), lambda i, *_: (i // steps_per_batch, 0, 0)),
                      pl.BlockSpec(g.shape, lambda i, *_: (0, 0))],
            out_specs=tok,
            scratch_shapes=[pltpu.VMEM((2 * TG, RS, d // 2), u32), pltpu.SemaphoreType.DMA((2 * TG,))]),
        compiler_params=_params(("arbitrary",)),
        name="combine",
    )(toff, dst, c8s, yb, pos, gt, x1, mod3, g)


def _layout_tables(tab, nt, p_rows):
    c8 = tab[:, :nt].T.astype(i32)
    toff = jnp.cumsum(c8, axis=1) - c8
    len8 = jnp.sum(c8, axis=0)
    seg = (len8 + BM - 1) // BM * BM
    gend = jnp.cumsum(seg)
    gstart = gend - seg
    dst = jnp.cumsum(c8, axis=0) - c8 + gstart[None, :]
    n_used = gend[-1] // BM
    blk = jnp.arange(p_rows // BM, dtype=i32)
    last = jnp.minimum(blk, n_used - 1)
    blk_e = jnp.sum((gend[None, :] <= (last * BM)[:, None]).astype(i32), axis=1)
    blk_e = jnp.minimum(blk_e, N_EXPERTS - 1)
    ids = jnp.arange(N_EXPERTS, dtype=i32)
    later = jnp.logical_and(ids[None, :] > ids[:, None], (seg > 0)[None, :])
    nxt_e = jnp.min(jnp.where(later, ids[None, :], N_EXPERTS), axis=1)
    nxt_e = jnp.where(nxt_e == N_EXPERTS, -1, nxt_e).astype(i32)
    return (toff.reshape(-1), dst.reshape(-1), c8.reshape(-1), gstart + len8, seg - len8,
            blk_e, nxt_e, n_used.reshape(1).astype(i32))


def kernel(x, c, w_ada, b_ada, g_pre_mix, g_post_mix, w_in, rel_bias, sgu_ln_g, sgu_ln_b,
           w_spatial, b_spatial, w_branch_a, w_branch_b, w_gate, b_gate, w_out,
           g_pre_ffn, g_post_ffn, w_router, b_router, w_gate_up, b_gate_up, w_down, b_down):
    b, s, d = x.shape
    assert d == D_MODEL and s % max(TM, TMP, TMX) == 0 and s % (TG * TS) == 0
    n = b * s
    nt = n // TS
    ntp = -(-nt // LANES) * LANES
    p_rows = -(-(n * TOP_K + nt * N_EXPERTS * (ROW_ALIGN - 1) + N_EXPERTS * (BM - 1)) // BM) * BM
    depth = w_ada.shape[0]
    c8 = jnp.pad(c, ((0, 8 - b), (0, 0)))
    row = lambda a: a.reshape(1, -1)

    for l in range(depth):
        mod = _ada(c8, w_ada[l], row(b_ada[l]))[:b]
        mod3 = mod.reshape(b, 6, d)

        q, k, v, u, vv = _proj(x, mod3, row(g_pre_mix[l]), w_in[l].astype(bf16),
                               row(sgu_ln_g[l]), row(sgu_ln_b[l]))
        ya = _attn(q, k, v, _attn_bias(rel_bias[l]))
        ws2 = w_spatial[l].astype(bf16).reshape(-1, 2 * SGU_BLOCK, SGU_BLOCK)
        bsf = jnp.repeat(b_spatial[l].T, SGU_WIDTH // b_spatial.shape[1], axis=1)
        x1, h2, pos, gt, tab = _mix(x, mod3, row(g_pre_mix[l]), row(g_post_mix[l]), u, vv, ya, ws2, bsf,
                                   w_branch_a[l].astype(bf16), w_branch_b[l].astype(bf16),
                                   w_gate[l].astype(bf16), row(b_gate[l]), w_out[l].astype(bf16),
                                   row(g_pre_ffn[l]), w_router[l].T.astype(bf16),
                                   b_router[l].reshape(-1, 1), ntp)
        x1f = x1.reshape(n, d)
        h2 = h2.reshape(n, d)
        toff, dst, c8s, tstart, tlen, blk_e, nxt_e, n_used = _layout_tables(tab, nt, p_rows)
        xb = _dispatch(toff, dst, c8s, tstart, tlen, n_used, h2, pos, p_rows)
        yb = _experts(blk_e, nxt_e, n_used, xb, w_gate_up[l], b_gate_up[l][:, None, :],
                      w_down[l], b_down[l][:, None, :])
        x = _combine(toff, dst, c8s, yb, pos, gt, x1f, mod3, row(g_post_ffn[l]), s // TS).reshape(b, s, d)
    return x
```

```python
import jax
import jax.numpy as jnp
from jax import lax
from jax.experimental import pallas as pl
from jax.experimental.pallas import tpu as pltpu

bf16 = jnp.bfloat16
f32 = jnp.float32
i32 = jnp.int32
u32 = jnp.uint32

D_MODEL = 1024
CHUNK = 64
N_LEFT = 8
ATT_HEADS = 8
HEAD_DIM = 64
ATT_WIDTH = 512
MAX_REL = 128
SGU_BLOCK = 128
SGU_WIDTH = 512
N_EXPERTS = 32
TOP_K = 4
SWIGLU_LIMIT = 7.0
SWIGLU_ALPHA = 1.702
EPS = 1e-6
NEG = -1e30
LOG2E = 1.4426950408889634

LANES = 128
ROW_ALIGN = 8
TM = 512
TMP = 1024
TMX = 1024
QCH = 4
QG = QCH * CHUNK
KBAND = (N_LEFT + QCH) * CHUNK
BIAS_PERIOD = 1024
TS = 256
TG = 2
RS = TS * TOP_K + N_EXPERTS * ROW_ALIGN
BM = 512
VMEM_LIMIT = 56 * 2**20


def _params(sem):
    return pltpu.CompilerParams(dimension_semantics=sem, vmem_limit_bytes=VMEM_LIMIT)


def _adaln(x, g, sc, sh):
    ms = jnp.mean(x * x, axis=-1, keepdims=True)
    return (x * lax.rsqrt(ms + EPS) * g) * (1.0 + sc) + sh


def _rms(x, g):
    ms = jnp.mean(x * x, axis=-1, keepdims=True)
    return x * lax.rsqrt(ms + EPS) * g


def _sigmoid(x):
    return 1.0 / (1.0 + jnp.exp(-x))


def _ada_body(c_ref, w_ref, b_ref, o_ref):
    c = c_ref[...]
    ca = c * _sigmoid(c)
    o_ref[...] = jnp.dot(ca.astype(bf16), w_ref[...].astype(bf16),
                         preferred_element_type=f32) + b_ref[...]


def _ada(c8, w, b):
    d = w.shape[0]
    n = w.shape[1] // d
    return pl.pallas_call(
        _ada_body,
        out_shape=jax.ShapeDtypeStruct((8, n * d), f32),
        grid=(n,),
        in_specs=[pl.BlockSpec((8, d), lambda j: (0, 0)),
                  pl.BlockSpec((d, d), lambda j: (0, j)),
                  pl.BlockSpec((1, d), lambda j: (0, j))],
        out_specs=pl.BlockSpec((8, d), lambda j: (0, j)),
        compiler_params=_params(("arbitrary",)),
        name="ada",
    )(c8, w, b)


def _proj_body(x_ref, mod_ref, g_ref, w_ref, lng_ref, lnb_ref,
               q_ref, k_ref, v_ref, u_ref, vv_ref):
    h = _adaln(x_ref[0], g_ref[...], mod_ref[0, 1:2, :], mod_ref[0, 0:1, :]).astype(bf16)
    aw = ATT_WIDTH
    z = jnp.dot(h, w_ref[:, 3 * aw:], preferred_element_type=f32)
    p = jnp.dot(h, w_ref[:, :3 * aw], preferred_element_type=f32)
    q_ref[0] = (p[:, 0:aw] * (HEAD_DIM ** -0.5 * LOG2E)).astype(bf16)
    k_ref[0] = p[:, aw:2 * aw].astype(bf16)
    v_ref[0] = p[:, 2 * aw:3 * aw].astype(bf16)
    zg = 0.5 * z * (1.0 + lax.erf(z * (2.0 ** -0.5)))
    u_ref[0] = zg[:, :SGU_WIDTH].astype(bf16)
    vv = zg[:, SGU_WIDTH:]
    mu = jnp.mean(vv, axis=-1, keepdims=True)
    var = jnp.mean(jnp.square(vv - mu), axis=-1, keepdims=True)
    vn = (vv - mu) * lax.rsqrt(var + EPS) * lng_ref[...] + lnb_ref[...]
    vv_ref[0] = vn.astype(bf16)


def _proj(x, mod3, g, w_in, lng, lnb):
    b, s, d = x.shape
    tok = lambda w: pl.BlockSpec((1, TMP, w), lambda bi, i: (bi, i, 0))
    full = lambda a: pl.BlockSpec(a.shape, lambda bi, i: (0,) * a.ndim)
    o512 = jax.ShapeDtypeStruct((b, s, ATT_WIDTH), bf16)
    return pl.pallas_call(
        _proj_body,
        out_shape=(o512,) * 5,
        grid=(b, s // TMP),
        in_specs=[tok(d), pl.BlockSpec((1, 6, d), lambda bi, i: (bi, 0, 0)),
                  full(g), full(w_in), full(lng), full(lnb)],
        out_specs=(tok(ATT_WIDTH),) * 5,
        compiler_params=_params(("parallel", "arbitrary")),
        name="proj",
    )(x, mod3, g, w_in, lng, lnb)


def _attn_body(q_ref, kp_ref, kc_ref, vp_ref, vc_ref, bvec_ref, o_ref, kbuf, vbuf, bias_ref):
    first = pl.program_id(1) == 0
    nhp = ATT_HEADS // 2
    ones = jnp.ones((TM, LANES), bf16)

    @pl.when(first)
    def _():
        i = lax.broadcasted_iota(i32, (QG, KBAND), 0)
        j = lax.broadcasted_iota(i32, (QG, KBAND), 1)
        jb = j - (i // CHUNK) * CHUNK
        in_band = jnp.logical_and(jb >= 0, jb < CHUNK * (N_LEFT + 1))
        for h in range(ATT_HEADS):
            rows = jnp.broadcast_to(bvec_ref[h:h + 1, :], (QG, BIAS_PERIOD))
            toep = pltpu.roll(rows, 0, 1, stride=1, stride_axis=0)[:, :KBAND]
            bias_ref[h // 2, (h % 2) * QG:(h % 2 + 1) * QG, :] = jnp.where(in_band, toep, NEG)

    def fill(rows, k_src, v_src):
        kbuf[rows, :] = k_src[0]
        for hp in range(nhp):
            vbuf[rows, 2 * hp * LANES:(2 * hp + 1) * LANES] = v_src[0, :, hp * LANES:(hp + 1) * LANES]
            vbuf[rows, (2 * hp + 1) * LANES:(2 * hp + 2) * LANES] = ones

    fill(slice(TM, 2 * TM), kc_ref, vc_ref)

    @pl.when(first)
    def _():
        kbuf[0:TM, :] = jnp.zeros((TM, kbuf.shape[1]), bf16)
        vbuf[0:TM, :] = jnp.zeros((TM, vbuf.shape[1]), bf16)

    @pl.when(jnp.logical_not(first))
    def _():
        fill(slice(0, TM), kp_ref, vp_ref)

    lo = lax.broadcasted_iota(i32, (QG, LANES), 1) < HEAD_DIM

    def group(p, carry):
        r0 = pl.multiple_of(p * QG, QG)
        for hp in range(nhp):
            c0 = hp * LANES
            qp = q_ref[0, pl.ds(r0, QG), c0:c0 + LANES]
            zero = jnp.zeros_like(qp)
            q2 = jnp.concatenate([jnp.where(lo, qp, zero), jnp.where(lo, zero, qp)], axis=0)
            kb = kbuf[pl.ds(r0, KBAND), c0:c0 + LANES]
            s = lax.dot_general(q2, kb, (((1,), (1,)), ((), ())), preferred_element_type=f32)
            sb = (s + bias_ref[hp]).astype(bf16)
            m = jnp.max(sb, axis=-1, keepdims=True)
            e = jnp.exp2(sb - m)
            vb = vbuf[pl.ds(r0, KBAND), 2 * c0:2 * c0 + 2 * LANES]
            o2 = jnp.dot(e, vb, preferred_element_type=f32)
            on = o2[:, :LANES] / o2[:, LANES:]
            o = jnp.where(lo, on[:QG], on[QG:])
            o_ref[0, pl.ds(r0, QG), c0:c0 + LANES] = o.astype(bf16)
        return carry

    lax.fori_loop(0, TM // QG, group, 0, unroll=True)


def _attn(q, k, v, bvec):
    b, s, w = q.shape
    cur = pl.BlockSpec((1, TM, w), lambda bi, i: (bi, i, 0))
    prev = pl.BlockSpec((1, TM, w), lambda bi, i: (bi, jnp.maximum(i - 1, 0), 0))
    return pl.pallas_call(
        _attn_body,
        out_shape=jax.ShapeDtypeStruct((b, s, w), bf16),
        grid=(b, s // TM),
        in_specs=[cur, prev, cur, prev, cur,
                  pl.BlockSpec(bvec.shape, lambda bi, i: (0, 0))],
        out_specs=cur,
        scratch_shapes=[pltpu.VMEM((2 * TM, w), bf16), pltpu.VMEM((2 * TM, 2 * w), bf16),
                        pltpu.VMEM((ATT_HEADS // 2, 2 * QG, KBAND), f32)],
        compiler_params=_params(("parallel", "arbitrary")),
        name="attn",
    )(q, k, k, v, v, bvec)


def _attn_bias_vec(rel_bias):
    h = rel_bias.shape[0]
    n_far = N_LEFT * CHUNK - MAX_REL
    assert BIAS_PERIOD >= QG + KBAND - 1 and n_far >= 0
    far = jnp.broadcast_to(rel_bias[:, 2 * MAX_REL:], (h, BIAS_PERIOD))
    near = rel_bias[:, :0:-1]
    v = jnp.concatenate([far[:, :n_far], near, far[:, n_far + 2 * MAX_REL:]], axis=1)
    return v.astype(f32) * LOG2E


def _route(hb, wr_ref, br_ref):
    lg = lax.dot_general(wr_ref[...], hb, (((1,), (1,)), ((), ())),
                         preferred_element_type=f32) + br_ref[...]
    e_iota = lax.broadcasted_iota(i32, lg.shape, 0)
    vals, idxs = [], []
    hits = jnp.zeros(lg.shape, f32)
    for _ in range(TOP_K):
        m = jnp.max(lg, axis=0, keepdims=True)
        idx = jnp.min(jnp.where(lg == m, e_iota, N_EXPERTS), axis=0, keepdims=True)
        hit = e_iota == idx
        hits = hits + jnp.where(hit, 1.0, 0.0)
        lg = jnp.where(hit, -jnp.inf, lg)
        vals.append(m)
        idxs.append(idx)
    ex = [jnp.exp(v - vals[0]) for v in vals]
    den = ex[0] + ex[1] + ex[2] + ex[3]
    return (jnp.concatenate(idxs, axis=0), jnp.concatenate([e / den for e in ex], axis=0), hits)


def _mix_body(x_ref, mod_ref, gpre_ref, gpost_ref, u_ref, vv_ref, ya_ref, ws_ref, bs_ref,
              wa_ref, wb_ref, wg_ref, bg_ref, wo_ref, gffn_ref, wr_ref, br_ref,
              o_ref, h2_ref, pos_ref, gt_ref, tab_ref, ybuf):
    d = D_MODEL
    x = x_ref[0]
    h = _adaln(x, gpre_ref[...], mod_ref[0, 1:2, :], mod_ref[0, 0:1, :]).astype(bf16)
    gs = _sigmoid(jnp.dot(h, wg_ref[...], preferred_element_type=f32) + bg_ref[...])

    blk = SGU_BLOCK
    row = lax.broadcasted_iota(i32, (2 * blk, blk), 0)
    colv = lax.broadcasted_iota(i32, (2 * blk, blk), 1)
    causal = colv <= jnp.bitwise_and(row, blk - 1)
    lo = lax.broadcasted_iota(i32, (blk, LANES), 1) < (LANES // 2)
    for gp in range(SGU_WIDTH // LANES):
        c0 = gp * LANES
        w2 = ws_ref[gp]
        w2 = jnp.where(causal, w2, jnp.zeros_like(w2))
        for bi in range(TMX // blk):
            r0 = bi * blk
            s2 = jnp.dot(w2, vv_ref[0, r0:r0 + blk, c0:c0 + LANES], preferred_element_type=f32)
            s = jnp.where(lo, s2[:blk], s2[blk:]) + bs_ref[:, c0:c0 + LANES]
            yb = u_ref[0, r0:r0 + blk, c0:c0 + LANES].astype(f32) * s
            ybuf[r0:r0 + blk, c0:c0 + LANES] = yb.astype(bf16)

    a = jnp.dot(ya_ref[0], wa_ref[...], preferred_element_type=f32)
    bb = jnp.dot(ybuf[...], wb_ref[...], preferred_element_type=f32)
    merged = gs[:, :d] * a + gs[:, d:] * bb
    y = jnp.dot(merged.astype(bf16), wo_ref[...], preferred_element_type=f32)
    x1 = x + mod_ref[0, 2:3, :] * _rms(y, gpost_ref[...])
    o_ref[0] = x1

    hb = _adaln(x1, gffn_ref[...], mod_ref[0, 4:5, :], mod_ref[0, 3:4, :]).astype(bf16)
    h2_ref[0] = hb
    ids, gates, hits = _route(hb, wr_ref, br_ref)
    pos_ref[...] = jnp.concatenate(
        [jnp.concatenate(_sorted_positions(ids[:, j * TS:(j + 1) * TS]), axis=0)
         for j in range(TMX // TS)], axis=1)
    gt_ref[...] = gates

    step = pl.program_id(0) * pl.num_programs(1) + pl.program_id(1)

    @pl.when(step == 0)
    def _():
        tab_ref[...] = jnp.zeros_like(tab_ref)

    lane = lax.broadcasted_iota(i32, tab_ref.shape, 1)
    acc = tab_ref[...]
    for j in range(TMX // TS):
        cnt = jnp.sum(hits[:, j * TS:(j + 1) * TS], axis=1, keepdims=True)
        c8 = jnp.floor((cnt + (ROW_ALIGN - 1.0)) * (1.0 / ROW_ALIGN)) * ROW_ALIGN
        acc = acc + jnp.where(lane == step * (TMX // TS) + j, c8, 0.0)
    tab_ref[...] = acc


def _mix(x, mod3, gpre, gpost, u, vv, ya, ws2, bsf, wa, wb, wg, bg, wo, gffn, wrt, br, ntp):
    b, s, d = x.shape
    n = b * s
    tok = lambda w: pl.BlockSpec((1, TMX, w), lambda bi, i: (bi, i, 0))
    full = lambda a: pl.BlockSpec(a.shape, lambda bi, i: (0,) * a.ndim)
    lane = pl.BlockSpec((TOP_K, TMX), lambda bi, i: (0, bi * (s // TMX) + i))
    return pl.pallas_call(
        _mix_body,
        out_shape=(jax.ShapeDtypeStruct((b, s, d), f32),
                   jax.ShapeDtypeStruct((b, s, d), bf16),
                   jax.ShapeDtypeStruct((TOP_K, n), i32),
                   jax.ShapeDtypeStruct((TOP_K, n), f32),
                   jax.ShapeDtypeStruct((N_EXPERTS, ntp), f32)),
        grid=(b, s // TMX),
        in_specs=[tok(d), pl.BlockSpec((1, 6, d), lambda bi, i: (bi, 0, 0)),
                  full(gpre), full(gpost), tok(SGU_WIDTH), tok(SGU_WIDTH), tok(ATT_WIDTH),
                  full(ws2), full(bsf), full(wa), full(wb), full(wg), full(bg), full(wo),
                  full(gffn), full(wrt), full(br)],
        out_specs=(tok(d), tok(d), lane, lane,
                   pl.BlockSpec((N_EXPERTS, ntp), lambda bi, i: (0, 0))),
        scratch_shapes=[pltpu.VMEM((TMX, SGU_WIDTH), bf16)],
        compiler_params=_params(("arbitrary", "arbitrary")),
        name="mix",
    )(x, mod3, gpre, gpost, u, vv, ya, ws2, bsf, wa, wb, wg, bg, wo, gffn, wrt, br)


def _sorted_positions(ti):
    ts = ti.shape[1]
    e_iota = lax.broadcasted_iota(i32, (N_EXPERTS, ts), 0)
    upper = (lax.broadcasted_iota(i32, (ts, ts), 0) < lax.broadcasted_iota(i32, (ts, ts), 1))
    upper = jnp.where(upper, 1.0, 0.0).astype(bf16)
    hits, prefs, cnts = [], [], []
    for k in range(TOP_K):
        hit = ti[k:k + 1, :] == e_iota
        hf = jnp.where(hit, 1.0, 0.0)
        prefs.append(jnp.dot(hf.astype(bf16), upper, preferred_element_type=f32))
        cnts.append(jnp.sum(hf, axis=1, keepdims=True))
        hits.append(hit)
    total = cnts[0] + cnts[1] + cnts[2] + cnts[3]
    c8 = jnp.floor((total + (ROW_ALIGN - 1.0)) * (1.0 / ROW_ALIGN)) * ROW_ALIGN
    lower = (lax.broadcasted_iota(i32, (N_EXPERTS, N_EXPERTS), 1)
             < lax.broadcasted_iota(i32, (N_EXPERTS, N_EXPERTS), 0))
    lower = jnp.where(lower, 1.0, 0.0).astype(bf16)
    c8b = jnp.broadcast_to(c8, (N_EXPERTS, LANES)).astype(bf16)
    start = jnp.dot(lower, c8b, preferred_element_type=f32)[:, 0:1]
    pos = []
    for k in range(TOP_K):
        pe = start + prefs[k]
        pos.append(jnp.sum(jnp.where(hits[k], pe, 0.0), axis=0, keepdims=True).astype(i32))
        start = start + cnts[k]
    return pos


def _pack_halves(x, exact):
    half = x.shape[1] // 2
    lo = lax.bitcast_convert_type(x[:, :half], u32)
    hi = lax.bitcast_convert_type(x[:, half:], u32)
    if not exact:
        hi = jnp.bitwise_and(hi, jnp.uint32(0xFFFF0000))
    return jnp.bitwise_or(lax.shift_right_logical(lo, jnp.uint32(16)), hi)


def _unpack_halves(w):
    lo = lax.bitcast_convert_type(lax.shift_left(w, jnp.uint32(16)), f32)
    hi = lax.bitcast_convert_type(jnp.bitwise_and(w, jnp.uint32(0xFFFF0000)), f32)
    return lo, hi


def _start_runs(i, toff_ref, dst_ref, c8_ref, make):
    unroll = 4

    def some(q, carry):
        for k in range(unroll):
            j = i * N_EXPERTS + q * unroll + k
            n = pl.multiple_of(c8_ref[j], ROW_ALIGN)
            so = pl.multiple_of(toff_ref[j], ROW_ALIGN)
            do = pl.multiple_of(dst_ref[j], ROW_ALIGN)

            @pl.when(n > 0)
            def _():
                make(so, do, n).start(priority=k % 2)
        return carry
    lax.fori_loop(0, N_EXPERTS // unroll, some, 0)


def _tile_rows(i, toff_ref, c8_ref):
    j = i * N_EXPERTS + (N_EXPERTS - 1)
    return pl.multiple_of(toff_ref[j] + c8_ref[j], ROW_ALIGN)


def _dispatch_body(toff_ref, dst_ref, c8_ref, tstart_ref, tlen_ref, nu_ref,
                   h2_ref, pos_ref, xb_ref, sbuf, zbuf, sems, sem):
    i = pl.program_id(0)
    base = lax.rem(i, 2) * TG
    r = lax.broadcasted_iota(i32, (RS, TS), 0)
    for j in range(TG):
        p = [pos_ref[k:k + 1, j * TS:(j + 1) * TS] for k in range(TOP_K)]
        pm = jnp.where(r == p[0], 1.0, jnp.where(r == p[1], 1.0,
             jnp.where(r == p[2], 1.0, jnp.where(r == p[3], 1.0, 0.0))))
        srt = jnp.dot(pm.astype(bf16), h2_ref[j * TS:(j + 1) * TS, :], preferred_element_type=f32)
        sbuf[base + j] = _pack_halves(srt, exact=True)

    for j in range(TG):
        def make(so, do, n, sl=base + j):
            return pltpu.make_async_copy(sbuf.at[sl, pl.ds(so, n)], xb_ref.at[pl.ds(do, n)],
                                         sems.at[sl])
        _start_runs(i * TG + j, toff_ref, dst_ref, c8_ref, make)

    def wait_tile(t, sl):
        n = _tile_rows(t, toff_ref, c8_ref)
        pltpu.make_async_copy(sbuf.at[sl, pl.ds(0, n)], xb_ref.at[pl.ds(0, n)], sems.at[sl]).wait()

    @pl.when(i > 0)
    def _():
        for j in range(TG):
            wait_tile((i - 1) * TG + j, TG - base + j)

    @pl.when(i == pl.num_programs(0) - 1)
    def _():
        for j in range(TG):
            wait_tile(i * TG + j, base + j)
        zbuf[...] = jnp.zeros_like(zbuf)

        def fill(action):
            def tail(e, carry):
                n = pl.multiple_of(tlen_ref[e], ROW_ALIGN)
                do = pl.multiple_of(tstart_ref[e], ROW_ALIGN)

                @pl.when(n > 0)
                def _():
                    action(pltpu.make_async_copy(zbuf.at[pl.ds(0, n)], xb_ref.at[pl.ds(do, n)], sem))
                return carry
            lax.fori_loop(0, N_EXPERTS, tail, 0)

            def unused(b, carry):
                do = pl.multiple_of(b * BM, BM)
                action(pltpu.make_async_copy(zbuf, xb_ref.at[pl.ds(do, BM)], sem))
                return carry
            lax.fori_loop(nu_ref[0], xb_ref.shape[0] // BM, unused, 0)
        fill(lambda c: c.start())
        fill(lambda c: c.wait())


def _dispatch(toff, dst, c8s, tstart, tlen, n_used, h2, pos, p_rows):
    n, d = h2.shape
    return pl.pallas_call(
        _dispatch_body,
        out_shape=jax.ShapeDtypeStruct((p_rows, d // 2), u32),
        grid_spec=pltpu.PrefetchScalarGridSpec(
            num_scalar_prefetch=6,
            grid=(n // (TG * TS),),
            in_specs=[pl.BlockSpec((TG * TS, d), lambda i, *_: (i, 0)),
                      pl.BlockSpec((TOP_K, TG * TS), lambda i, *_: (0, i))],
            out_specs=pl.BlockSpec(memory_space=pl.ANY),
            scratch_shapes=[pltpu.VMEM((2 * TG, RS, d // 2), u32), pltpu.VMEM((BM, d // 2), u32),
                            pltpu.SemaphoreType.DMA((2 * TG,)), pltpu.SemaphoreType.DMA(())]),
        compiler_params=_params(("arbitrary",)),
        name="dispatch",
    )(toff, dst, c8s, tstart, tlen, n_used, h2, pos)


def _expert_body(be_ref, nxt_ref, nu_ref, x_ref, wgu_hbm, bgu_ref, wd_hbm, bd_ref, o_ref,
                 wgu_st, wd_st, wgu_bf, wd_bf, sems):
    del nu_ref
    b = pl.program_id(0)
    d = D_MODEL
    e = be_ref[b]

    def weight_copies(ex):
        return (pltpu.make_async_copy(wgu_hbm.at[ex], wgu_st, sems.at[0]),
                pltpu.make_async_copy(wd_hbm.at[ex], wd_st, sems.at[1]))

    @pl.when(b == 0)
    def _():
        for c in weight_copies(e):
            c.start()

    @pl.when(jnp.logical_or(b == 0, e != be_ref[jnp.maximum(b - 1, 0)]))
    def _():
        for c in weight_copies(e):
            c.wait()
        wgu_bf[...] = wgu_st[...].astype(bf16)
        wd_bf[...] = wd_st[...].astype(bf16)
        nxt = nxt_ref[e]

        @pl.when(nxt >= 0)
        def _():
            for c in weight_copies(nxt):
                c.start()

    xb = jnp.concatenate(_unpack_halves(x_ref[...]), axis=1).astype(bf16)
    gu = jnp.dot(xb, wgu_bf[...], preferred_element_type=f32) + bgu_ref[0]
    glu = jnp.minimum(gu[:, :d], SWIGLU_LIMIT)
    lin = jnp.clip(gu[:, d:], -SWIGLU_LIMIT, SWIGLU_LIMIT)
    act = glu * _sigmoid(SWIGLU_ALPHA * glu) * (lin + 1.0)
    y = jnp.dot(act.astype(bf16), wd_bf[...], preferred_element_type=f32) + bd_ref[0]
    o_ref[...] = _pack_halves(y, exact=False)


def _experts(blk_e, nxt_e, n_used, xb, wgu, bgu, wd, bd):
    rows = pl.BlockSpec((BM, xb.shape[1]), lambda b, be, nx, nu: (b, 0))
    per_e = lambda a: pl.BlockSpec((1,) + a.shape[1:], lambda b, be, nx, nu: (be[b], 0, 0))
    hbm = pl.BlockSpec(memory_space=pl.ANY)
    return pl.pallas_call(
        _expert_body,
        out_shape=jax.ShapeDtypeStruct(xb.shape, xb.dtype),
        grid_spec=pltpu.PrefetchScalarGridSpec(
            num_scalar_prefetch=3,
            grid=(n_used[0],),
            in_specs=[rows, hbm, per_e(bgu), hbm, per_e(bd)],
            out_specs=rows,
            scratch_shapes=[pltpu.VMEM(wgu.shape[1:], f32), pltpu.VMEM(wd.shape[1:], f32),
                            pltpu.VMEM(wgu.shape[1:], bf16), pltpu.VMEM(wd.shape[1:], bf16),
                            pltpu.SemaphoreType.DMA((2,))]),
        input_output_aliases={3: 0},
        compiler_params=_params(("arbitrary",)),
        name="experts",
    )(blk_e, nxt_e, n_used, xb, wgu, bgu, wd, bd)


def _combine_body(toff_ref, dst_ref, c8_ref, yb_ref, pos_ref, gt_ref, x1_ref, mod_ref, g_ref,
                  o_ref, sbuf, sems):
    i = pl.program_id(0)
    base = lax.rem(i, 2) * TG

    def fetch(step, first_slot):
        for j in range(TG):
            def make(so, do, n, sl=first_slot + j):
                return pltpu.make_async_copy(yb_ref.at[pl.ds(do, n)], sbuf.at[sl, pl.ds(so, n)],
                                             sems.at[sl])
            _start_runs(step * TG + j, toff_ref, dst_ref, c8_ref, make)

    @pl.when(i == 0)
    def _():
        sbuf[...] = jnp.zeros_like(sbuf)
        fetch(i, base)

    @pl.when(i + 1 < pl.num_programs(0))
    def _():
        fetch(i + 1, TG - base)

    r = lax.broadcasted_iota(i32, (RS, TS), 0)
    pms, grows = [], []
    for j in range(TG):
        p = [pos_ref[k:k + 1, j * TS:(j + 1) * TS] for k in range(TOP_K)]
        gt = gt_ref[:, j * TS:(j + 1) * TS]
        m = [r == p[k] for k in range(TOP_K)]
        pm = jnp.where(m[0], 1.0, jnp.where(m[1], 1.0, jnp.where(m[2], 1.0, jnp.where(m[3], 1.0, 0.0))))
        gm = jnp.where(m[0], gt[0:1, :], jnp.where(m[1], gt[1:2, :],
             jnp.where(m[2], gt[2:3, :], jnp.where(m[3], gt[3:4, :], 0.0))))
        pms.append(pm.astype(bf16))
        grows.append(jnp.sum(gm, axis=1, keepdims=True))

    for j in range(TG):
        n = _tile_rows(i * TG + j, toff_ref, c8_ref)
        pltpu.make_async_copy(yb_ref.at[pl.ds(0, n)], sbuf.at[base + j, pl.ds(0, n)],
                              sems.at[base + j]).wait()
    for j in range(TG):
        halves = [lax.dot_general(pms[j], (h * grows[j]).astype(bf16), (((0,), (0,)), ((), ())),
                                  preferred_element_type=f32)
                  for h in _unpack_halves(sbuf[base + j])]
        y = jnp.concatenate(halves, axis=1)
        rows = slice(j * TS, (j + 1) * TS)
        o_ref[rows, :] = x1_ref[rows, :] + mod_ref[0, 5:6, :] * _rms(y, g_ref[...])


def _combine(toff, dst, c8s, yb, pos, gt, x1, mod3, g, tiles_per_batch):
    n, d = x1.shape
    tok = pl.BlockSpec((TG * TS, d), lambda i, *_: (i, 0))
    lane = pl.BlockSpec((TOP_K, TG * TS), lambda i, *_: (0, i))
    steps_per_batch = tiles_per_batch // TG
    return pl.pallas_call(
        _combine_body,
        out_shape=jax.ShapeDtypeStruct((n, d), f32),
        grid_spec=pltpu.PrefetchScalarGridSpec(
            num_scalar_prefetch=3,
            grid=(n // (TG * TS),),
            in_specs=[pl.BlockSpec(memory_space=pl.ANY), lane, lane, tok,
                      pl.BlockSpec((1, 6, d), lambda i, *_: (i // steps_per_batch, 0, 0)),
                      pl.BlockSpec(g.shape, lambda i, *_: (0, 0))],
            out_specs=tok,
            scratch_shapes=[pltpu.VMEM((2 * TG, RS, d // 2), u32), pltpu.SemaphoreType.DMA((2 * TG,))]),
        compiler_params=_params(("arbitrary",)),
        name="combine",
    )(toff, dst, c8s, yb, pos, gt, x1, mod3, g)


def _layout_tables(tab, nt, p_rows):
    c8 = tab[:, :nt].T.astype(i32)
    toff = jnp.cumsum(c8, axis=1) - c8
    len8 = jnp.sum(c8, axis=0)
    seg = (len8 + BM - 1) // BM * BM
    gend = jnp.cumsum(seg)
    gstart = gend - seg
    dst = jnp.cumsum(c8, axis=0) - c8 + gstart[None, :]
    n_used = gend[-1] // BM
    blk = jnp.arange(p_rows // BM, dtype=i32)
    last = jnp.minimum(blk, n_used - 1)
    blk_e = jnp.sum((gend[None, :] <= (last * BM)[:, None]).astype(i32), axis=1)
    blk_e = jnp.minimum(blk_e, N_EXPERTS - 1)
    ids = jnp.arange(N_EXPERTS, dtype=i32)
    later = jnp.logical_and(ids[None, :] > ids[:, None], (seg > 0)[None, :])
    nxt_e = jnp.min(jnp.where(later, ids[None, :], N_EXPERTS), axis=1)
    nxt_e = jnp.where(nxt_e == N_EXPERTS, -1, nxt_e).astype(i32)
    return (toff.reshape(-1), dst.reshape(-1), c8.reshape(-1), gstart + len8, seg - len8,
            blk_e, nxt_e, n_used.reshape(1).astype(i32))


def kernel(x, c, w_ada, b_ada, g_pre_mix, g_post_mix, w_in, rel_bias, sgu_ln_g, sgu_ln_b,
           w_spatial, b_spatial, w_branch_a, w_branch_b, w_gate, b_gate, w_out,
           g_pre_ffn, g_post_ffn, w_router, b_router, w_gate_up, b_gate_up, w_down, b_down):
    b, s, d = x.shape
    assert d == D_MODEL and s % max(TM, TMP, TMX) == 0 and s % (TG * TS) == 0
    n = b * s
    nt = n // TS
    ntp = -(-nt // LANES) * LANES
    p_rows = -(-(n * TOP_K + nt * N_EXPERTS * (ROW_ALIGN - 1) + N_EXPERTS * (BM - 1)) // BM) * BM
    depth = w_ada.shape[0]
    c8 = jnp.pad(c, ((0, 8 - b), (0, 0)))
    row = lambda a: a.reshape(1, -1)

    for l in range(depth):
        mod = _ada(c8, w_ada[l], row(b_ada[l]))[:b]
        mod3 = mod.reshape(b, 6, d)

        q, k, v, u, vv = _proj(x, mod3, row(g_pre_mix[l]), w_in[l].astype(bf16),
                               row(sgu_ln_g[l]), row(sgu_ln_b[l]))
        ya = _attn(q, k, v, _attn_bias_vec(rel_bias[l]))
        ws2 = w_spatial[l].astype(bf16).reshape(-1, 2 * SGU_BLOCK, SGU_BLOCK)
        bsf = jnp.repeat(b_spatial[l].T, SGU_WIDTH // b_spatial.shape[1], axis=1)
        x1, h2, pos, gt, tab = _mix(x, mod3, row(g_pre_mix[l]), row(g_post_mix[l]), u, vv, ya, ws2, bsf,
                                   w_branch_a[l].astype(bf16), w_branch_b[l].astype(bf16),
                                   w_gate[l].astype(bf16), row(b_gate[l]), w_out[l].astype(bf16),
                                   row(g_pre_ffn[l]), w_router[l].T.astype(bf16),
                                   b_router[l].reshape(-1, 1), ntp)
        x1f = x1.reshape(n, d)
        h2 = h2.reshape(n, d)
        toff, dst, c8s, tstart, tlen, blk_e, nxt_e, n_used = _layout_tables(tab, nt, p_rows)
        xb = _dispatch(toff, dst, c8s, tstart, tlen, n_used, h2, pos, p_rows)
        yb = _experts(blk_e, nxt_e, n_used, xb, w_gate_up[l], b_gate_up[l][:, None, :],
                      w_down[l], b_down[l][:, None, :])
        x = _combine(toff, dst, c8s, yb, pos, gt, x1f, mod3, row(g_post_ffn[l]), s // TS).reshape(b, s, d)
    return x
```

```python
import jax
import jax.numpy as jnp
from jax import lax
from jax.experimental import pallas as pl
from jax.experimental.pallas import tpu as pltpu

bf16 = jnp.bfloat16
f32 = jnp.float32
i32 = jnp.int32
u32 = jnp.uint32

D_MODEL = 1024
CHUNK = 64
N_LEFT = 8
ATT_HEADS = 8
HEAD_DIM = 64
ATT_WIDTH = 512
MAX_REL = 128
SGU_BLOCK = 128
SGU_WIDTH = 512
N_EXPERTS = 32
TOP_K = 4
SWIGLU_LIMIT = 7.0
SWIGLU_ALPHA = 1.702
EPS = 1e-6
NEG = -1e30
LOG2E = 1.4426950408889634

LANES = 128
ROW_ALIGN = 8
TM = 512
TMP = 1024
TMX = 1024
QCH = 4
QG = QCH * CHUNK
KBAND = (N_LEFT + QCH) * CHUNK
BIAS_PERIOD = 1024
TS = 256
TG = 4
RS = TS * TOP_K + N_EXPERTS * ROW_ALIGN
BM = 512
VMEM_LIMIT = 56 * 2**20


def _params(sem):
    return pltpu.CompilerParams(dimension_semantics=sem, vmem_limit_bytes=VMEM_LIMIT)


def _adaln(x, g, sc, sh):
    ms = jnp.mean(x * x, axis=-1, keepdims=True)
    return (x * lax.rsqrt(ms + EPS) * g) * (1.0 + sc) + sh


def _rms(x, g):
    ms = jnp.mean(x * x, axis=-1, keepdims=True)
    return x * lax.rsqrt(ms + EPS) * g


def _sigmoid(x):
    return 1.0 / (1.0 + jnp.exp(-x))


def _ada_body(c_ref, w_ref, b_ref, o_ref):
    c = c_ref[...]
    ca = c * _sigmoid(c)
    o_ref[...] = jnp.dot(ca.astype(bf16), w_ref[...].astype(bf16),
                         preferred_element_type=f32) + b_ref[...]


def _ada(c8, w, b):
    d = w.shape[0]
    n = w.shape[1] // d
    return pl.pallas_call(
        _ada_body,
        out_shape=jax.ShapeDtypeStruct((8, n * d), f32),
        grid=(n,),
        in_specs=[pl.BlockSpec((8, d), lambda j: (0, 0)),
                  pl.BlockSpec((d, d), lambda j: (0, j)),
                  pl.BlockSpec((1, d), lambda j: (0, j))],
        out_specs=pl.BlockSpec((8, d), lambda j: (0, j)),
        compiler_params=_params(("arbitrary",)),
        name="ada",
    )(c8, w, b)


def _proj_body(x_ref, mod_ref, g_ref, w_ref, lng_ref, lnb_ref,
               q_ref, k_ref, v_ref, u_ref, vv_ref):
    h = _adaln(x_ref[0], g_ref[...], mod_ref[0, 1:2, :], mod_ref[0, 0:1, :]).astype(bf16)
    aw = ATT_WIDTH
    z = jnp.dot(h, w_ref[:, 3 * aw:], preferred_element_type=f32)
    p = jnp.dot(h, w_ref[:, :3 * aw], preferred_element_type=f32)
    q_ref[0] = (p[:, 0:aw] * (HEAD_DIM ** -0.5 * LOG2E)).astype(bf16)
    k_ref[0] = p[:, aw:2 * aw].astype(bf16)
    v_ref[0] = p[:, 2 * aw:3 * aw].astype(bf16)
    zg = 0.5 * z * (1.0 + lax.erf(z * (2.0 ** -0.5)))
    u_ref[0] = zg[:, :SGU_WIDTH].astype(bf16)
    vv = zg[:, SGU_WIDTH:]
    mu = jnp.mean(vv, axis=-1, keepdims=True)
    var = jnp.mean(jnp.square(vv - mu), axis=-1, keepdims=True)
    vn = (vv - mu) * lax.rsqrt(var + EPS) * lng_ref[...] + lnb_ref[...]
    vv_ref[0] = vn.astype(bf16)


def _proj(x, mod3, g, w_in, lng, lnb):
    b, s, d = x.shape
    tok = lambda w: pl.BlockSpec((1, TMP, w), lambda bi, i: (bi, i, 0))
    full = lambda a: pl.BlockSpec(a.shape, lambda bi, i: (0,) * a.ndim)
    o512 = jax.ShapeDtypeStruct((b, s, ATT_WIDTH), bf16)
    return pl.pallas_call(
        _proj_body,
        out_shape=(o512,) * 5,
        grid=(b, s // TMP),
        in_specs=[tok(d), pl.BlockSpec((1, 6, d), lambda bi, i: (bi, 0, 0)),
                  full(g), full(w_in), full(lng), full(lnb)],
        out_specs=(tok(ATT_WIDTH),) * 5,
        compiler_params=_params(("parallel", "arbitrary")),
        name="proj",
    )(x, mod3, g, w_in, lng, lnb)


def _attn_body(q_ref, kp_ref, kc_ref, vp_ref, vc_ref, bvec_ref, o_ref, kbuf, vbuf, bias_ref):
    first = pl.program_id(1) == 0
    nhp = ATT_HEADS // 2
    ones = jnp.ones((TM, LANES), bf16)

    @pl.when(first)
    def _():
        i = lax.broadcasted_iota(i32, (QG, KBAND), 0)
        j = lax.broadcasted_iota(i32, (QG, KBAND), 1)
        jb = j - (i // CHUNK) * CHUNK
        in_band = jnp.logical_and(jb >= 0, jb < CHUNK * (N_LEFT + 1))
        for h in range(ATT_HEADS):
            rows = jnp.broadcast_to(bvec_ref[h:h + 1, :], (QG, BIAS_PERIOD))
            toep = pltpu.roll(rows, 0, 1, stride=1, stride_axis=0)[:, :KBAND]
            bias_ref[h // 2, (h % 2) * QG:(h % 2 + 1) * QG, :] = jnp.where(in_band, toep, NEG)

    def fill(rows, k_src, v_src):
        kbuf[rows, :] = k_src[0]
        for hp in range(nhp):
            vbuf[rows, 2 * hp * LANES:(2 * hp + 1) * LANES] = v_src[0, :, hp * LANES:(hp + 1) * LANES]
            vbuf[rows, (2 * hp + 1) * LANES:(2 * hp + 2) * LANES] = ones

    fill(slice(TM, 2 * TM), kc_ref, vc_ref)

    @pl.when(first)
    def _():
        kbuf[0:TM, :] = jnp.zeros((TM, kbuf.shape[1]), bf16)
        vbuf[0:TM, :] = jnp.zeros((TM, vbuf.shape[1]), bf16)

    @pl.when(jnp.logical_not(first))
    def _():
        fill(slice(0, TM), kp_ref, vp_ref)

    lo = lax.broadcasted_iota(i32, (QG, LANES), 1) < HEAD_DIM

    def group(p, carry):
        r0 = pl.multiple_of(p * QG, QG)
        for hp in range(nhp):
            c0 = hp * LANES
            qp = q_ref[0, pl.ds(r0, QG), c0:c0 + LANES]
            zero = jnp.zeros_like(qp)
            q2 = jnp.concatenate([jnp.where(lo, qp, zero), jnp.where(lo, zero, qp)], axis=0)
            kb = kbuf[pl.ds(r0, KBAND), c0:c0 + LANES]
            s = lax.dot_general(q2, kb, (((1,), (1,)), ((), ())), preferred_element_type=f32)
            sb = (s + bias_ref[hp]).astype(bf16)
            m = jnp.max(sb, axis=-1, keepdims=True)
            e = jnp.exp2(sb - m)
            vb = vbuf[pl.ds(r0, KBAND), 2 * c0:2 * c0 + 2 * LANES]
            o2 = jnp.dot(e, vb, preferred_element_type=f32)
            on = o2[:, :LANES] / o2[:, LANES:]
            o = jnp.where(lo, on[:QG], on[QG:])
            o_ref[0, pl.ds(r0, QG), c0:c0 + LANES] = o.astype(bf16)
        return carry

    lax.fori_loop(0, TM // QG, group, 0, unroll=True)


def _attn(q, k, v, bvec):
    b, s, w = q.shape
    cur = pl.BlockSpec((1, TM, w), lambda bi, i: (bi, i, 0))
    prev = pl.BlockSpec((1, TM, w), lambda bi, i: (bi, jnp.maximum(i - 1, 0), 0))
    return pl.pallas_call(
        _attn_body,
        out_shape=jax.ShapeDtypeStruct((b, s, w), bf16),
        grid=(b, s // TM),
        in_specs=[cur, prev, cur, prev, cur,
                  pl.BlockSpec(bvec.shape, lambda bi, i: (0, 0))],
        out_specs=cur,
        scratch_shapes=[pltpu.VMEM((2 * TM, w), bf16), pltpu.VMEM((2 * TM, 2 * w), bf16),
                        pltpu.VMEM((ATT_HEADS // 2, 2 * QG, KBAND), f32)],
        compiler_params=_params(("parallel", "arbitrary")),
        name="attn",
    )(q, k, k, v, v, bvec)


def _attn_bias_vec(rel_bias):
    h = rel_bias.shape[0]
    n_far = N_LEFT * CHUNK - MAX_REL
    assert BIAS_PERIOD >= QG + KBAND - 1 and n_far >= 0
    far = jnp.broadcast_to(rel_bias[:, 2 * MAX_REL:], (h, BIAS_PERIOD))
    near = rel_bias[:, :0:-1]
    v = jnp.concatenate([far[:, :n_far], near, far[:, n_far + 2 * MAX_REL:]], axis=1)
    return v.astype(f32) * LOG2E


def _route(hb, wr_ref, br_ref):
    lg = lax.dot_general(wr_ref[...], hb, (((1,), (1,)), ((), ())),
                         preferred_element_type=f32) + br_ref[...]
    e_iota = lax.broadcasted_iota(i32, lg.shape, 0)
    vals, idxs = [], []
    hits = jnp.zeros(lg.shape, f32)
    for _ in range(TOP_K):
        m = jnp.max(lg, axis=0, keepdims=True)
        idx = jnp.min(jnp.where(lg == m, e_iota, N_EXPERTS), axis=0, keepdims=True)
        hit = e_iota == idx
        hits = hits + jnp.where(hit, 1.0, 0.0)
        lg = jnp.where(hit, -jnp.inf, lg)
        vals.append(m)
        idxs.append(idx)
    ex = [jnp.exp(v - vals[0]) for v in vals]
    den = ex[0] + ex[1] + ex[2] + ex[3]
    return (jnp.concatenate(idxs, axis=0), jnp.concatenate([e / den for e in ex], axis=0), hits)


def _mix_body(x_ref, mod_ref, gpre_ref, gpost_ref, u_ref, vv_ref, ya_ref, ws_ref, bs_ref,
              wa_ref, wb_ref, wg_ref, bg_ref, wo_ref, gffn_ref, wr_ref, br_ref,
              o_ref, h2_ref, pos_ref, gt_ref, tab_ref, ybuf):
    d = D_MODEL
    x = x_ref[0]
    h = _adaln(x, gpre_ref[...], mod_ref[0, 1:2, :], mod_ref[0, 0:1, :]).astype(bf16)
    gs = _sigmoid(jnp.dot(h, wg_ref[...], preferred_element_type=f32) + bg_ref[...])

    blk = SGU_BLOCK
    row = lax.broadcasted_iota(i32, (2 * blk, blk), 0)
    colv = lax.broadcasted_iota(i32, (2 * blk, blk), 1)
    causal = colv <= jnp.bitwise_and(row, blk - 1)
    lo = lax.broadcasted_iota(i32, (blk, LANES), 1) < (LANES // 2)
    for gp in range(SGU_WIDTH // LANES):
        c0 = gp * LANES
        w2 = ws_ref[gp]
        w2 = jnp.where(causal, w2, jnp.zeros_like(w2))
        for bi in range(TMX // blk):
            r0 = bi * blk
            s2 = jnp.dot(w2, vv_ref[0, r0:r0 + blk, c0:c0 + LANES], preferred_element_type=f32)
            s = jnp.where(lo, s2[:blk], s2[blk:]) + bs_ref[:, c0:c0 + LANES]
            yb = u_ref[0, r0:r0 + blk, c0:c0 + LANES].astype(f32) * s
            ybuf[r0:r0 + blk, c0:c0 + LANES] = yb.astype(bf16)

    a = jnp.dot(ya_ref[0], wa_ref[...], preferred_element_type=f32)
    bb = jnp.dot(ybuf[...], wb_ref[...], preferred_element_type=f32)
    merged = gs[:, :d] * a + gs[:, d:] * bb
    y = jnp.dot(merged.astype(bf16), wo_ref[...], preferred_element_type=f32)
    x1 = x + mod_ref[0, 2:3, :] * _rms(y, gpost_ref[...])
    o_ref[0] = x1

    hb = _adaln(x1, gffn_ref[...], mod_ref[0, 4:5, :], mod_ref[0, 3:4, :]).astype(bf16)
    h2_ref[0] = hb
    ids, gates, hits = _route(hb, wr_ref, br_ref)
    pos_ref[...] = jnp.concatenate(
        [jnp.concatenate(_sorted_positions(ids[:, j * TS:(j + 1) * TS]), axis=0)
         for j in range(TMX // TS)], axis=1)
    gt_ref[...] = gates

    step = pl.program_id(0) * pl.num_programs(1) + pl.program_id(1)

    @pl.when(step == 0)
    def _():
        tab_ref[...] = jnp.zeros_like(tab_ref)

    lane = lax.broadcasted_iota(i32, tab_ref.shape, 1)
    acc = tab_ref[...]
    for j in range(TMX // TS):
        cnt = jnp.sum(hits[:, j * TS:(j + 1) * TS], axis=1, keepdims=True)
        c8 = jnp.floor((cnt + (ROW_ALIGN - 1.0)) * (1.0 / ROW_ALIGN)) * ROW_ALIGN
        acc = acc + jnp.where(lane == step * (TMX // TS) + j, c8, 0.0)
    tab_ref[...] = acc


def _mix(x, mod3, gpre, gpost, u, vv, ya, ws2, bsf, wa, wb, wg, bg, wo, gffn, wrt, br, ntp):
    b, s, d = x.shape
    n = b * s
    tok = lambda w: pl.BlockSpec((1, TMX, w), lambda bi, i: (bi, i, 0))
    full = lambda a: pl.BlockSpec(a.shape, lambda bi, i: (0,) * a.ndim)
    lane = pl.BlockSpec((TOP_K, TMX), lambda bi, i: (0, bi * (s // TMX) + i))
    return pl.pallas_call(
        _mix_body,
        out_shape=(jax.ShapeDtypeStruct((b, s, d), f32),
                   jax.ShapeDtypeStruct((b, s, d), bf16),
                   jax.ShapeDtypeStruct((TOP_K, n), i32),
                   jax.ShapeDtypeStruct((TOP_K, n), f32),
                   jax.ShapeDtypeStruct((N_EXPERTS, ntp), f32)),
        grid=(b, s // TMX),
        in_specs=[tok(d), pl.BlockSpec((1, 6, d), lambda bi, i: (bi, 0, 0)),
                  full(gpre), full(gpost), tok(SGU_WIDTH), tok(SGU_WIDTH), tok(ATT_WIDTH),
                  full(ws2), full(bsf), full(wa), full(wb), full(wg), full(bg), full(wo),
                  full(gffn), full(wrt), full(br)],
        out_specs=(tok(d), tok(d), lane, lane,
                   pl.BlockSpec((N_EXPERTS, ntp), lambda bi, i: (0, 0))),
        scratch_shapes=[pltpu.VMEM((TMX, SGU_WIDTH), bf16)],
        compiler_params=_params(("arbitrary", "arbitrary")),
        name="mix",
    )(x, mod3, gpre, gpost, u, vv, ya, ws2, bsf, wa, wb, wg, bg, wo, gffn, wrt, br)


def _sorted_positions(ti):
    ts = ti.shape[1]
    e_iota = lax.broadcasted_iota(i32, (N_EXPERTS, ts), 0)
    upper = (lax.broadcasted_iota(i32, (ts, ts), 0) < lax.broadcasted_iota(i32, (ts, ts), 1))
    upper = jnp.where(upper, 1.0, 0.0).astype(bf16)
    hits, prefs, cnts = [], [], []
    for k in range(TOP_K):
        hit = ti[k:k + 1, :] == e_iota
        hf = jnp.where(hit, 1.0, 0.0)
        prefs.append(jnp.dot(hf.astype(bf16), upper, preferred_element_type=f32))
        cnts.append(jnp.sum(hf, axis=1, keepdims=True))
        hits.append(hit)
    total = cnts[0] + cnts[1] + cnts[2] + cnts[3]
    c8 = jnp.floor((total + (ROW_ALIGN - 1.0)) * (1.0 / ROW_ALIGN)) * ROW_ALIGN
    lower = (lax.broadcasted_iota(i32, (N_EXPERTS, N_EXPERTS), 1)
             < lax.broadcasted_iota(i32, (N_EXPERTS, N_EXPERTS), 0))
    lower = jnp.where(lower, 1.0, 0.0).astype(bf16)
    c8b = jnp.broadcast_to(c8, (N_EXPERTS, LANES)).astype(bf16)
    start = jnp.dot(lower, c8b, preferred_element_type=f32)[:, 0:1]
    pos = []
    for k in range(TOP_K):
        pe = start + prefs[k]
        pos.append(jnp.sum(jnp.where(hits[k], pe, 0.0), axis=0, keepdims=True).astype(i32))
        start = start + cnts[k]
    return pos


def _pack_halves(x, exact):
    half = x.shape[1] // 2
    lo = lax.bitcast_convert_type(x[:, :half], u32)
    hi = lax.bitcast_convert_type(x[:, half:], u32)
    if not exact:
        hi = jnp.bitwise_and(hi, jnp.uint32(0xFFFF0000))
    return jnp.bitwise_or(lax.shift_right_logical(lo, jnp.uint32(16)), hi)


def _unpack_halves(w):
    lo = lax.bitcast_convert_type(lax.shift_left(w, jnp.uint32(16)), f32)
    hi = lax.bitcast_convert_type(jnp.bitwise_and(w, jnp.uint32(0xFFFF0000)), f32)
    return lo, hi


def _start_runs(i, toff_ref, dst_ref, c8_ref, make):
    unroll = 4

    def some(q, carry):
        for k in range(unroll):
            j = i * N_EXPERTS + q * unroll + k
            n = pl.multiple_of(c8_ref[j], ROW_ALIGN)
            so = pl.multiple_of(toff_ref[j], ROW_ALIGN)
            do = pl.multiple_of(dst_ref[j], ROW_ALIGN)

            @pl.when(n > 0)
            def _():
                make(so, do, n).start(priority=k % 2)
        return carry
    lax.fori_loop(0, N_EXPERTS // unroll, some, 0)


def _tile_rows(i, toff_ref, c8_ref):
    j = i * N_EXPERTS + (N_EXPERTS - 1)
    return pl.multiple_of(toff_ref[j] + c8_ref[j], ROW_ALIGN)


def _dispatch_body(toff_ref, dst_ref, c8_ref, tstart_ref, tlen_ref, nu_ref,
                   h2_ref, pos_ref, xb_ref, sbuf, zbuf, sems, sem):
    i = pl.program_id(0)
    base = lax.rem(i, 2) * TG
    r = lax.broadcasted_iota(i32, (RS, TS), 0)
    for j in range(TG):
        p = [pos_ref[k:k + 1, j * TS:(j + 1) * TS] for k in range(TOP_K)]
        pm = jnp.where(r == p[0], 1.0, jnp.where(r == p[1], 1.0,
             jnp.where(r == p[2], 1.0, jnp.where(r == p[3], 1.0, 0.0))))
        srt = jnp.dot(pm.astype(bf16), h2_ref[j * TS:(j + 1) * TS, :], preferred_element_type=f32)
        sbuf[base + j] = _pack_halves(srt, exact=True)

    for j in range(TG):
        def make(so, do, n, sl=base + j):
            return pltpu.make_async_copy(sbuf.at[sl, pl.ds(so, n)], xb_ref.at[pl.ds(do, n)],
                                         sems.at[sl])
        _start_runs(i * TG + j, toff_ref, dst_ref, c8_ref, make)

    def wait_tile(t, sl):
        n = _tile_rows(t, toff_ref, c8_ref)
        pltpu.make_async_copy(sbuf.at[sl, pl.ds(0, n)], xb_ref.at[pl.ds(0, n)], sems.at[sl]).wait()

    @pl.when(i > 0)
    def _():
        for j in range(TG):
            wait_tile((i - 1) * TG + j, TG - base + j)

    @pl.when(i == pl.num_programs(0) - 1)
    def _():
        for j in range(TG):
            wait_tile(i * TG + j, base + j)
        zbuf[...] = jnp.zeros_like(zbuf)

        def fill(action):
            def tail(e, carry):
                n = pl.multiple_of(tlen_ref[e], ROW_ALIGN)
                do = pl.multiple_of(tstart_ref[e], ROW_ALIGN)

                @pl.when(n > 0)
                def _():
                    action(pltpu.make_async_copy(zbuf.at[pl.ds(0, n)], xb_ref.at[pl.ds(do, n)], sem))
                return carry
            lax.fori_loop(0, N_EXPERTS, tail, 0)

            def unused(b, carry):
                do = pl.multiple_of(b * BM, BM)
                action(pltpu.make_async_copy(zbuf, xb_ref.at[pl.ds(do, BM)], sem))
                return carry
            lax.fori_loop(nu_ref[0], xb_ref.shape[0] // BM, unused, 0)
        fill(lambda c: c.start())
        fill(lambda c: c.wait())


def _dispatch(toff, dst, c8s, tstart, tlen, n_used, h2, pos, p_rows):
    n, d = h2.shape
    return pl.pallas_call(
        _dispatch_body,
        out_shape=jax.ShapeDtypeStruct((p_rows, d // 2), u32),
        grid_spec=pltpu.PrefetchScalarGridSpec(
            num_scalar_prefetch=6,
            grid=(n // (TG * TS),),
            in_specs=[pl.BlockSpec((TG * TS, d), lambda i, *_: (i, 0)),
                      pl.BlockSpec((TOP_K, TG * TS), lambda i, *_: (0, i))],
            out_specs=pl.BlockSpec(memory_space=pl.ANY),
            scratch_shapes=[pltpu.VMEM((2 * TG, RS, d // 2), u32), pltpu.VMEM((BM, d // 2), u32),
                            pltpu.SemaphoreType.DMA((2 * TG,)), pltpu.SemaphoreType.DMA(())]),
        compiler_params=_params(("arbitrary",)),
        name="dispatch",
    )(toff, dst, c8s, tstart, tlen, n_used, h2, pos)


def _expert_body(be_ref, nxt_ref, nu_ref, x_ref, wgu_hbm, bgu_ref, wd_hbm, bd_ref, o_ref,
                 wgu_st, wd_st, wgu_bf, wd_bf, sems):
    del nu_ref
    b = pl.program_id(0)
    d = D_MODEL
    e = be_ref[b]

    def weight_copies(ex):
        return (pltpu.make_async_copy(wgu_hbm.at[ex], wgu_st, sems.at[0]),
                pltpu.make_async_copy(wd_hbm.at[ex], wd_st, sems.at[1]))

    @pl.when(b == 0)
    def _():
        for c in weight_copies(e):
            c.start()

    @pl.when(jnp.logical_or(b == 0, e != be_ref[jnp.maximum(b - 1, 0)]))
    def _():
        for c in weight_copies(e):
            c.wait()
        wgu_bf[...] = wgu_st[...].astype(bf16)
        wd_bf[...] = wd_st[...].astype(bf16)
        nxt = nxt_ref[e]

        @pl.when(nxt >= 0)
        def _():
            for c in weight_copies(nxt):
                c.start()

    xb = jnp.concatenate(_unpack_halves(x_ref[...]), axis=1).astype(bf16)
    gu = jnp.dot(xb, wgu_bf[...], preferred_element_type=f32) + bgu_ref[0]
    glu = jnp.minimum(gu[:, :d], SWIGLU_LIMIT)
    lin = jnp.clip(gu[:, d:], -SWIGLU_LIMIT, SWIGLU_LIMIT)
    act = glu * _sigmoid(SWIGLU_ALPHA * glu) * (lin + 1.0)
    y = jnp.dot(act.astype(bf16), wd_bf[...], preferred_element_type=f32) + bd_ref[0]
    o_ref[...] = _pack_halves(y, exact=False)


def _experts(blk_e, nxt_e, n_used, xb, wgu, bgu, wd, bd):
    rows = pl.BlockSpec((BM, xb.shape[1]), lambda b, be, nx, nu: (b, 0))
    per_e = lambda a: pl.BlockSpec((1,) + a.shape[1:], lambda b, be, nx, nu: (be[b], 0, 0))
    hbm = pl.BlockSpec(memory_space=pl.ANY)
    return pl.pallas_call(
        _expert_body,
        out_shape=jax.ShapeDtypeStruct(xb.shape, xb.dtype),
        grid_spec=pltpu.PrefetchScalarGridSpec(
            num_scalar_prefetch=3,
            grid=(n_used[0],),
            in_specs=[rows, hbm, per_e(bgu), hbm, per_e(bd)],
            out_specs=rows,
            scratch_shapes=[pltpu.VMEM(wgu.shape[1:], f32), pltpu.VMEM(wd.shape[1:], f32),
                            pltpu.VMEM(wgu.shape[1:], bf16), pltpu.VMEM(wd.shape[1:], bf16),
                            pltpu.SemaphoreType.DMA((2,))]),
        input_output_aliases={3: 0},
        compiler_params=_params(("arbitrary",)),
        name="experts",
    )(blk_e, nxt_e, n_used, xb, wgu, bgu, wd, bd)


def _combine_body(toff_ref, dst_ref, c8_ref, yb_ref, pos_ref, gt_ref, x1_ref, mod_ref, g_ref,
                  o_ref, sbuf, sems):
    i = pl.program_id(0)
    base = lax.rem(i, 2) * TG

    def fetch(step, first_slot):
        for j in range(TG):
            def make(so, do, n, sl=first_slot + j):
                return pltpu.make_async_copy(yb_ref.at[pl.ds(do, n)], sbuf.at[sl, pl.ds(so, n)],
                                             sems.at[sl])
            _start_runs(step * TG + j, toff_ref, dst_ref, c8_ref, make)

    @pl.when(i == 0)
    def _():
        sbuf[...] = jnp.zeros_like(sbuf)
        fetch(i, base)

    @pl.when(i + 1 < pl.num_programs(0))
    def _():
        fetch(i + 1, TG - base)

    r = lax.broadcasted_iota(i32, (RS, TS), 0)
    pms, grows = [], []
    for j in range(TG):
        p = [pos_ref[k:k + 1, j * TS:(j + 1) * TS] for k in range(TOP_K)]
        gt = gt_ref[:, j * TS:(j + 1) * TS]
        m = [r == p[k] for k in range(TOP_K)]
        pm = jnp.where(m[0], 1.0, jnp.where(m[1], 1.0, jnp.where(m[2], 1.0, jnp.where(m[3], 1.0, 0.0))))
        gm = jnp.where(m[0], gt[0:1, :], jnp.where(m[1], gt[1:2, :],
             jnp.where(m[2], gt[2:3, :], jnp.where(m[3], gt[3:4, :], 0.0))))
        pms.append(pm.astype(bf16))
        grows.append(jnp.sum(gm, axis=1, keepdims=True))

    for j in range(TG):
        n = _tile_rows(i * TG + j, toff_ref, c8_ref)
        pltpu.make_async_copy(yb_ref.at[pl.ds(0, n)], sbuf.at[base + j, pl.ds(0, n)],
                              sems.at[base + j]).wait()
    for j in range(TG):
        halves = [lax.dot_general(pms[j], (h * grows[j]).astype(bf16), (((0,), (0,)), ((), ())),
                                  preferred_element_type=f32)
                  for h in _unpack_halves(sbuf[base + j])]
        y = jnp.concatenate(halves, axis=1)
        rows = slice(j * TS, (j + 1) * TS)
        o_ref[rows, :] = x1_ref[rows, :] + mod_ref[0, 5:6, :] * _rms(y, g_ref[...])


def _combine(toff, dst, c8s, yb, pos, gt, x1, mod3, g, tiles_per_batch):
    n, d = x1.shape
    tok = pl.BlockSpec((TG * TS, d), lambda i, *_: (i, 0))
    lane = pl.BlockSpec((TOP_K, TG * TS), lambda i, *_: (0, i))
    steps_per_batch = tiles_per_batch // TG
    return pl.pallas_call(
        _combine_body,
        out_shape=jax.ShapeDtypeStruct((n, d), f32),
        grid_spec=pltpu.PrefetchScalarGridSpec(
            num_scalar_prefetch=3,
            grid=(n // (TG * TS),),
            in_specs=[pl.BlockSpec(memory_space=pl.ANY), lane, lane, tok,
                      pl.BlockSpec((1, 6, d), lambda i, *_: (i // steps_per_batch, 0, 0)),
                      pl.BlockSpec(g.shape, lambda i, *_: (0, 0))],
            out_specs=tok,
            scratch_shapes=[pltpu.VMEM((2 * TG, RS, d // 2), u32), pltpu.SemaphoreType.DMA((2 * TG,))]),
        compiler_params=_params(("arbitrary",)),
        name="combine",
    )(toff, dst, c8s, yb, pos, gt, x1, mod3, g)


def _layout_tables(tab, nt, p_rows):
    c8 = tab[:, :nt].T.astype(i32)
    toff = jnp.cumsum(c8, axis=1) - c8
    len8 = jnp.sum(c8, axis=0)
    seg = (len8 + BM - 1) // BM * BM
    gend = jnp.cumsum(seg)
    gstart = gend - seg
    dst = jnp.cumsum(c8, axis=0) - c8 + gstart[None, :]
    n_used = gend[-1] // BM
    blk = jnp.arange(p_rows // BM, dtype=i32)
    last = jnp.minimum(blk, n_used - 1)
    blk_e = jnp.sum((gend[None, :] <= (last * BM)[:, None]).astype(i32), axis=1)
    blk_e = jnp.minimum(blk_e, N_EXPERTS - 1)
    ids = jnp.arange(N_EXPERTS, dtype=i32)
    later = jnp.logical_and(ids[None, :] > ids[:, None], (seg > 0)[None, :])
    nxt_e = jnp.min(jnp.where(later, ids[None, :], N_EXPERTS), axis=1)
    nxt_e = jnp.where(nxt_e == N_EXPERTS, -1, nxt_e).astype(i32)
    return (toff.reshape(-1), dst.reshape(-1), c8.reshape(-1), gstart + len8, seg - len8,
            blk_e, nxt_e, n_used.reshape(1).astype(i32))


def kernel(x, c, w_ada, b_ada, g_pre_mix, g_post_mix, w_in, rel_bias, sgu_ln_g, sgu_ln_b,
           w_spatial, b_spatial, w_branch_a, w_branch_b, w_gate, b_gate, w_out,
           g_pre_ffn, g_post_ffn, w_router, b_router, w_gate_up, b_gate_up, w_down, b_down):
    b, s, d = x.shape
    assert d == D_MODEL and s % max(TM, TMP, TMX) == 0 and s % (TG * TS) == 0
    n = b * s
    nt = n // TS
    ntp = -(-nt // LANES) * LANES
    p_rows = -(-(n * TOP_K + nt * N_EXPERTS * (ROW_ALIGN - 1) + N_EXPERTS * (BM - 1)) // BM) * BM
    depth = w_ada.shape[0]
    c8 = jnp.pad(c, ((0, 8 - b), (0, 0)))
    row = lambda a: a.reshape(1, -1)

    for l in range(depth):
        mod = _ada(c8, w_ada[l], row(b_ada[l]))[:b]
        mod3 = mod.reshape(b, 6, d)

        q, k, v, u, vv = _proj(x, mod3, row(g_pre_mix[l]), w_in[l].astype(bf16),
                               row(sgu_ln_g[l]), row(sgu_ln_b[l]))
        ya = _attn(q, k, v, _attn_bias_vec(rel_bias[l]))
        ws2 = w_spatial[l].astype(bf16).reshape(-1, 2 * SGU_BLOCK, SGU_BLOCK)
        bsf = jnp.repeat(b_spatial[l].T, SGU_WIDTH // b_spatial.shape[1], axis=1)
        x1, h2, pos, gt, tab = _mix(x, mod3, row(g_pre_mix[l]), row(g_post_mix[l]), u, vv, ya, ws2, bsf,
                                   w_branch_a[l].astype(bf16), w_branch_b[l].astype(bf16),
                                   w_gate[l].astype(bf16), row(b_gate[l]), w_out[l].astype(bf16),
                                   row(g_pre_ffn[l]), w_router[l].T.astype(bf16),
                                   b_router[l].reshape(-1, 1), ntp)
        x1f = x1.reshape(n, d)
        h2 = h2.reshape(n, d)
        toff, dst, c8s, tstart, tlen, blk_e, nxt_e, n_used = _layout_tables(tab, nt, p_rows)
        xb = _dispatch(toff, dst, c8s, tstart, tlen, n_used, h2, pos, p_rows)
        yb = _experts(blk_e, nxt_e, n_used, xb, w_gate_up[l], b_gate_up[l][:, None, :],
                      w_down[l], b_down[l][:, None, :])
        x = _combine(toff, dst, c8s, yb, pos, gt, x1f, mod3, row(g_post_ffn[l]), s // TS).reshape(b, s, d)
    return x
```

```python
import jax
import jax.numpy as jnp
from jax import lax
from jax.experimental import pallas as pl
from jax.experimental.pallas import tpu as pltpu

bf16 = jnp.bfloat16
f32 = jnp.float32
i32 = jnp.int32
u32 = jnp.uint32

D_MODEL = 1024
CHUNK = 64
N_LEFT = 8
ATT_HEADS = 8
HEAD_DIM = 64
ATT_WIDTH = 512
MAX_REL = 128
SGU_BLOCK = 128
SGU_WIDTH = 512
N_EXPERTS = 32
TOP_K = 4
SWIGLU_LIMIT = 7.0
SWIGLU_ALPHA = 1.702
EPS = 1e-6
NEG = -1e30
LOG2E = 1.4426950408889634

LANES = 128
ROW_ALIGN = 8
TM = 512
TMP = 1024
TMX = 1024
QCH = 4
QG = QCH * CHUNK
KBAND = (N_LEFT + QCH) * CHUNK
BIAS_PERIOD = 1024
TS = 256
TG = 4
RS = TS * TOP_K + N_EXPERTS * ROW_ALIGN
BM = 512
VMEM_LIMIT = 56 * 2**20


def _params(sem):
    return pltpu.CompilerParams(dimension_semantics=sem, vmem_limit_bytes=VMEM_LIMIT)


def _adaln(x, g, sc, sh):
    ms = jnp.mean(x * x, axis=-1, keepdims=True)
    return (x * lax.rsqrt(ms + EPS) * g) * (1.0 + sc) + sh


def _rms(x, g):
    ms = jnp.mean(x * x, axis=-1, keepdims=True)
    return x * lax.rsqrt(ms + EPS) * g


def _sigmoid(x):
    return 1.0 / (1.0 + jnp.exp(-x))


def _ada_body(c_ref, w_ref, b_ref, o_ref):
    c = c_ref[...]
    ca = c * _sigmoid(c)
    o_ref[...] = jnp.dot(ca.astype(bf16), w_ref[...].astype(bf16),
                         preferred_element_type=f32) + b_ref[...]


def _ada(c8, w, b):
    d = w.shape[0]
    n = w.shape[1] // d
    return pl.pallas_call(
        _ada_body,
        out_shape=jax.ShapeDtypeStruct((8, n * d), f32),
        grid=(n,),
        in_specs=[pl.BlockSpec((8, d), lambda j: (0, 0)),
                  pl.BlockSpec((d, d), lambda j: (0, j)),
                  pl.BlockSpec((1, d), lambda j: (0, j))],
        out_specs=pl.BlockSpec((8, d), lambda j: (0, j)),
        compiler_params=_params(("arbitrary",)),
        name="ada",
    )(c8, w, b)


def _proj_body(x_ref, mod_ref, g_ref, w_ref, lng_ref, lnb_ref,
               q_ref, k_ref, v_ref, u_ref, vv_ref):
    h = _adaln(x_ref[0], g_ref[...], mod_ref[0, 1:2, :], mod_ref[0, 0:1, :]).astype(bf16)
    aw = ATT_WIDTH
    z = jnp.dot(h, w_ref[:, 3 * aw:], preferred_element_type=f32)
    p = jnp.dot(h, w_ref[:, :3 * aw], preferred_element_type=f32)
    q_ref[0] = (p[:, 0:aw] * (HEAD_DIM ** -0.5 * LOG2E)).astype(bf16)
    k_ref[0] = p[:, aw:2 * aw].astype(bf16)
    v_ref[0] = p[:, 2 * aw:3 * aw].astype(bf16)
    zg = 0.5 * z * (1.0 + lax.erf(z * (2.0 ** -0.5)))
    u_ref[0] = zg[:, :SGU_WIDTH].astype(bf16)
    vv = zg[:, SGU_WIDTH:]
    mu = jnp.mean(vv, axis=-1, keepdims=True)
    var = jnp.mean(jnp.square(vv - mu), axis=-1, keepdims=True)
    vn = (vv - mu) * lax.rsqrt(var + EPS) * lng_ref[...] + lnb_ref[...]
    vv_ref[0] = vn.astype(bf16)


def _proj(x, mod3, g, w_in, lng, lnb):
    b, s, d = x.shape
    tok = lambda w: pl.BlockSpec((1, TMP, w), lambda bi, i: (bi, i, 0))
    full = lambda a: pl.BlockSpec(a.shape, lambda bi, i: (0,) * a.ndim)
    o512 = jax.ShapeDtypeStruct((b, s, ATT_WIDTH), bf16)
    return pl.pallas_call(
        _proj_body,
        out_shape=(o512,) * 5,
        grid=(b, s // TMP),
        in_specs=[tok(d), pl.BlockSpec((1, 6, d), lambda bi, i: (bi, 0, 0)),
                  full(g), full(w_in), full(lng), full(lnb)],
        out_specs=(tok(ATT_WIDTH),) * 5,
        compiler_params=_params(("parallel", "arbitrary")),
        name="proj",
    )(x, mod3, g, w_in, lng, lnb)


def _attn_body(q_ref, kp_ref, kc_ref, vp_ref, vc_ref, bvec_ref, o_ref, kbuf, vbuf, bias_ref):
    first = pl.program_id(1) == 0
    nhp = ATT_HEADS // 2
    ones = jnp.ones((TM, LANES), bf16)

    @pl.when(first)
    def _():
        i = lax.broadcasted_iota(i32, (QG, KBAND), 0)
        j = lax.broadcasted_iota(i32, (QG, KBAND), 1)
        jb = j - (i // CHUNK) * CHUNK
        in_band = jnp.logical_and(jb >= 0, jb < CHUNK * (N_LEFT + 1))
        for h in range(ATT_HEADS):
            rows = jnp.broadcast_to(bvec_ref[h:h + 1, :], (QG, BIAS_PERIOD))
            toep = pltpu.roll(rows, 0, 1, stride=1, stride_axis=0)[:, :KBAND]
            bias_ref[h // 2, (h % 2) * QG:(h % 2 + 1) * QG, :] = jnp.where(in_band, toep, NEG)

    def fill(rows, k_src, v_src):
        kbuf[rows, :] = k_src[0]
        for hp in range(nhp):
            vbuf[rows, 2 * hp * LANES:(2 * hp + 1) * LANES] = v_src[0, :, hp * LANES:(hp + 1) * LANES]
            vbuf[rows, (2 * hp + 1) * LANES:(2 * hp + 2) * LANES] = ones

    fill(slice(TM, 2 * TM), kc_ref, vc_ref)

    @pl.when(first)
    def _():
        kbuf[0:TM, :] = jnp.zeros((TM, kbuf.shape[1]), bf16)
        vbuf[0:TM, :] = jnp.zeros((TM, vbuf.shape[1]), bf16)

    @pl.when(jnp.logical_not(first))
    def _():
        fill(slice(0, TM), kp_ref, vp_ref)

    lo = lax.broadcasted_iota(i32, (QG, LANES), 1) < HEAD_DIM

    def group(p, carry):
        r0 = pl.multiple_of(p * QG, QG)
        for hp in range(nhp):
            c0 = hp * LANES
            qp = q_ref[0, pl.ds(r0, QG), c0:c0 + LANES]
            zero = jnp.zeros_like(qp)
            q2 = jnp.concatenate([jnp.where(lo, qp, zero), jnp.where(lo, zero, qp)], axis=0)
            kb = kbuf[pl.ds(r0, KBAND), c0:c0 + LANES]
            s = lax.dot_general(q2, kb, (((1,), (1,)), ((), ())), preferred_element_type=f32)
            sb = (s + bias_ref[hp]).astype(bf16)
            m = jnp.max(sb, axis=-1, keepdims=True)
            e = jnp.exp2(sb - m)
            vb = vbuf[pl.ds(r0, KBAND), 2 * c0:2 * c0 + 2 * LANES]
            o2 = jnp.dot(e, vb, preferred_element_type=f32)
            on = o2[:, :LANES] / o2[:, LANES:]
            o = jnp.where(lo, on[:QG], on[QG:])
            o_ref[0, pl.ds(r0, QG), c0:c0 + LANES] = o.astype(bf16)
        return carry

    lax.fori_loop(0, TM // QG, group, 0, unroll=True)


def _attn(q, k, v, bvec):
    b, s, w = q.shape
    cur = pl.BlockSpec((1, TM, w), lambda bi, i: (bi, i, 0))
    prev = pl.BlockSpec((1, TM, w), lambda bi, i: (bi, jnp.maximum(i - 1, 0), 0))
    return pl.pallas_call(
        _attn_body,
        out_shape=jax.ShapeDtypeStruct((b, s, w), bf16),
        grid=(b, s // TM),
        in_specs=[cur, prev, cur, prev, cur,
                  pl.BlockSpec(bvec.shape, lambda bi, i: (0, 0))],
        out_specs=cur,
        scratch_shapes=[pltpu.VMEM((2 * TM, w), bf16), pltpu.VMEM((2 * TM, 2 * w), bf16),
                        pltpu.VMEM((ATT_HEADS // 2, 2 * QG, KBAND), f32)],
        compiler_params=_params(("parallel", "arbitrary")),
        name="attn",
    )(q, k, k, v, v, bvec)


def _attn_bias_vec(rel_bias):
    h = rel_bias.shape[0]
    n_far = N_LEFT * CHUNK - MAX_REL
    assert BIAS_PERIOD >= QG + KBAND - 1 and n_far >= 0
    far = jnp.broadcast_to(rel_bias[:, 2 * MAX_REL:], (h, BIAS_PERIOD))
    near = rel_bias[:, :0:-1]
    v = jnp.concatenate([far[:, :n_far], near, far[:, n_far + 2 * MAX_REL:]], axis=1)
    return v.astype(f32) * LOG2E


def _route(hb, wr_ref, br_ref):
    lg = lax.dot_general(wr_ref[...], hb, (((1,), (1,)), ((), ())),
                         preferred_element_type=f32) + br_ref[...]
    e_iota = lax.broadcasted_iota(i32, lg.shape, 0)
    vals, idxs = [], []
    hits = jnp.zeros(lg.shape, f32)
    for _ in range(TOP_K):
        m = jnp.max(lg, axis=0, keepdims=True)
        idx = jnp.min(jnp.where(lg == m, e_iota, N_EXPERTS), axis=0, keepdims=True)
        hit = e_iota == idx
        hits = hits + jnp.where(hit, 1.0, 0.0)
        lg = jnp.where(hit, -jnp.inf, lg)
        vals.append(m)
        idxs.append(idx)
    ex = [jnp.exp(v - vals[0]) for v in vals]
    den = ex[0] + ex[1] + ex[2] + ex[3]
    return (jnp.concatenate(idxs, axis=0), jnp.concatenate([e / den for e in ex], axis=0), hits)


def _mix_body(x_ref, mod_ref, gpre_ref, gpost_ref, u_ref, vv_ref, ya_ref, ws_ref, bs_ref,
              wa_ref, wb_ref, wg_ref, bg_ref, wo_ref, gffn_ref, wr_ref, br_ref,
              o_ref, h2_ref, pos_ref, gt_ref, tab_ref, ybuf):
    d = D_MODEL
    x = x_ref[0]
    h = _adaln(x, gpre_ref[...], mod_ref[0, 1:2, :], mod_ref[0, 0:1, :]).astype(bf16)
    gs = _sigmoid(jnp.dot(h, wg_ref[...], preferred_element_type=f32) + bg_ref[...])

    blk = SGU_BLOCK
    row = lax.broadcasted_iota(i32, (2 * blk, blk), 0)
    colv = lax.broadcasted_iota(i32, (2 * blk, blk), 1)
    causal = colv <= jnp.bitwise_and(row, blk - 1)
    lo = lax.broadcasted_iota(i32, (blk, LANES), 1) < (LANES // 2)
    for gp in range(SGU_WIDTH // LANES):
        c0 = gp * LANES
        w2 = ws_ref[gp]
        w2 = jnp.where(causal, w2, jnp.zeros_like(w2))
        for bi in range(TMX // blk):
            r0 = bi * blk
            s2 = jnp.dot(w2, vv_ref[0, r0:r0 + blk, c0:c0 + LANES], preferred_element_type=f32)
            s = jnp.where(lo, s2[:blk], s2[blk:]) + bs_ref[:, c0:c0 + LANES]
            yb = u_ref[0, r0:r0 + blk, c0:c0 + LANES].astype(f32) * s
            ybuf[r0:r0 + blk, c0:c0 + LANES] = yb.astype(bf16)

    a = jnp.dot(ya_ref[0], wa_ref[...], preferred_element_type=f32)
    bb = jnp.dot(ybuf[...], wb_ref[...], preferred_element_type=f32)
    merged = gs[:, :d] * a + gs[:, d:] * bb
    y = jnp.dot(merged.astype(bf16), wo_ref[...], preferred_element_type=f32)
    x1 = x + mod_ref[0, 2:3, :] * _rms(y, gpost_ref[...])
    o_ref[0] = x1

    hb = _adaln(x1, gffn_ref[...], mod_ref[0, 4:5, :], mod_ref[0, 3:4, :]).astype(bf16)
    h2_ref[0] = hb
    ids, gates, hits = _route(hb, wr_ref, br_ref)
    pos_ref[...] = jnp.concatenate(
        [jnp.concatenate(_sorted_positions(ids[:, j * TS:(j + 1) * TS]), axis=0)
         for j in range(TMX // TS)], axis=1)
    gt_ref[...] = gates

    step = pl.program_id(0) * pl.num_programs(1) + pl.program_id(1)

    @pl.when(step == 0)
    def _():
        tab_ref[...] = jnp.zeros_like(tab_ref)

    lane = lax.broadcasted_iota(i32, tab_ref.shape, 1)
    acc = tab_ref[...]
    for j in range(TMX // TS):
        cnt = jnp.sum(hits[:, j * TS:(j + 1) * TS], axis=1, keepdims=True)
        c8 = jnp.floor((cnt + (ROW_ALIGN - 1.0)) * (1.0 / ROW_ALIGN)) * ROW_ALIGN
        acc = acc + jnp.where(lane == step * (TMX // TS) + j, c8, 0.0)
    tab_ref[...] = acc


def _mix(x, mod3, gpre, gpost, u, vv, ya, ws2, bsf, wa, wb, wg, bg, wo, gffn, wrt, br, ntp):
    b, s, d = x.shape
    n = b * s
    tok = lambda w: pl.BlockSpec((1, TMX, w), lambda bi, i: (bi, i, 0))
    full = lambda a: pl.BlockSpec(a.shape, lambda bi, i: (0,) * a.ndim)
    lane = pl.BlockSpec((TOP_K, TMX), lambda bi, i: (0, bi * (s // TMX) + i))
    return pl.pallas_call(
        _mix_body,
        out_shape=(jax.ShapeDtypeStruct((b, s, d), f32),
                   jax.ShapeDtypeStruct((b, s, d), bf16),
                   jax.ShapeDtypeStruct((TOP_K, n), i32),
                   jax.ShapeDtypeStruct((TOP_K, n), f32),
                   jax.ShapeDtypeStruct((N_EXPERTS, ntp), f32)),
        grid=(b, s // TMX),
        in_specs=[tok(d), pl.BlockSpec((1, 6, d), lambda bi, i: (bi, 0, 0)),
                  full(gpre), full(gpost), tok(SGU_WIDTH), tok(SGU_WIDTH), tok(ATT_WIDTH),
                  full(ws2), full(bsf), full(wa), full(wb), full(wg), full(bg), full(wo),
                  full(gffn), full(wrt), full(br)],
        out_specs=(tok(d), tok(d), lane, lane,
                   pl.BlockSpec((N_EXPERTS, ntp), lambda bi, i: (0, 0))),
        scratch_shapes=[pltpu.VMEM((TMX, SGU_WIDTH), bf16)],
        compiler_params=_params(("arbitrary", "arbitrary")),
        name="mix",
    )(x, mod3, gpre, gpost, u, vv, ya, ws2, bsf, wa, wb, wg, bg, wo, gffn, wrt, br)


def _sorted_positions(ti):
    ts = ti.shape[1]
    e_iota = lax.broadcasted_iota(i32, (N_EXPERTS, ts), 0)
    upper = (lax.broadcasted_iota(i32, (ts, ts), 0) < lax.broadcasted_iota(i32, (ts, ts), 1))
    upper = jnp.where(upper, 1.0, 0.0).astype(bf16)
    hits, prefs, cnts = [], [], []
    for k in range(TOP_K):
        hit = ti[k:k + 1, :] == e_iota
        hf = jnp.where(hit, 1.0, 0.0)
        prefs.append(jnp.dot(hf.astype(bf16), upper, preferred_element_type=f32))
        cnts.append(jnp.sum(hf, axis=1, keepdims=True))
        hits.append(hit)
    total = cnts[0] + cnts[1] + cnts[2] + cnts[3]
    c8 = jnp.floor((total + (ROW_ALIGN - 1.0)) * (1.0 / ROW_ALIGN)) * ROW_ALIGN
    lower = (lax.broadcasted_iota(i32, (N_EXPERTS, N_EXPERTS), 1)
             < lax.broadcasted_iota(i32, (N_EXPERTS, N_EXPERTS), 0))
    lower = jnp.where(lower, 1.0, 0.0).astype(bf16)
    c8b = jnp.broadcast_to(c8, (N_EXPERTS, LANES)).astype(bf16)
    start = jnp.dot(lower, c8b, preferred_element_type=f32)[:, 0:1]
    pos = []
    for k in range(TOP_K):
        pe = start + prefs[k]
        pos.append(jnp.sum(jnp.where(hits[k], pe, 0.0), axis=0, keepdims=True).astype(i32))
        start = start + cnts[k]
    return pos


def _pack_halves(x, exact):
    half = x.shape[1] // 2
    lo = lax.bitcast_convert_type(x[:, :half], u32)
    hi = lax.bitcast_convert_type(x[:, half:], u32)
    if not exact:
        hi = jnp.bitwise_and(hi, jnp.uint32(0xFFFF0000))
    return jnp.bitwise_or(lax.shift_right_logical(lo, jnp.uint32(16)), hi)


def _unpack_halves(w):
    lo = lax.bitcast_convert_type(lax.shift_left(w, jnp.uint32(16)), f32)
    hi = lax.bitcast_convert_type(jnp.bitwise_and(w, jnp.uint32(0xFFFF0000)), f32)
    return lo, hi


def _start_runs(i, toff_ref, dst_ref, c8_ref, make):
    unroll = 4

    def some(q, carry):
        for k in range(unroll):
            j = i * N_EXPERTS + q * unroll + k
            n = pl.multiple_of(c8_ref[j], ROW_ALIGN)
            so = pl.multiple_of(toff_ref[j], ROW_ALIGN)
            do = pl.multiple_of(dst_ref[j], ROW_ALIGN)

            @pl.when(n > 0)
            def _():
                make(so, do, n).start(priority=k % 2)
        return carry
    lax.fori_loop(0, N_EXPERTS // unroll, some, 0)


def _tile_rows(i, toff_ref, c8_ref):
    j = i * N_EXPERTS + (N_EXPERTS - 1)
    return pl.multiple_of(toff_ref[j] + c8_ref[j], ROW_ALIGN)


def _dispatch_body(toff_ref, dst_ref, c8_ref, tstart_ref, tlen_ref, nu_ref,
                   h2_ref, pos_ref, xb_ref, sbuf, zbuf, sems, sem):
    i = pl.program_id(0)
    base = lax.rem(i, 2) * TG
    r = lax.broadcasted_iota(i32, (RS, TS), 0)
    for j in range(TG):
        p = [pos_ref[k:k + 1, j * TS:(j + 1) * TS] for k in range(TOP_K)]
        pm = jnp.where(r == p[0], 1.0, jnp.where(r == p[1], 1.0,
             jnp.where(r == p[2], 1.0, jnp.where(r == p[3], 1.0, 0.0))))
        srt = jnp.dot(pm.astype(bf16), h2_ref[j * TS:(j + 1) * TS, :], preferred_element_type=f32)
        sbuf[base + j] = _pack_halves(srt, exact=True)

    for j in range(TG):
        def make(so, do, n, sl=base + j):
            return pltpu.make_async_copy(sbuf.at[sl, pl.ds(so, n)], xb_ref.at[pl.ds(do, n)],
                                         sems.at[sl])
        _start_runs(i * TG + j, toff_ref, dst_ref, c8_ref, make)

    def wait_tile(t, sl):
        n = _tile_rows(t, toff_ref, c8_ref)
        pltpu.make_async_copy(sbuf.at[sl, pl.ds(0, n)], xb_ref.at[pl.ds(0, n)], sems.at[sl]).wait()

    @pl.when(i > 0)
    def _():
        for j in range(TG):
            wait_tile((i - 1) * TG + j, TG - base + j)

    @pl.when(i == pl.num_programs(0) - 1)
    def _():
        for j in range(TG):
            wait_tile(i * TG + j, base + j)
        zbuf[...] = jnp.zeros_like(zbuf)

        def fill(action):
            def tail(e, carry):
                n = pl.multiple_of(tlen_ref[e], ROW_ALIGN)
                do = pl.multiple_of(tstart_ref[e], ROW_ALIGN)

                @pl.when(n > 0)
                def _():
                    action(pltpu.make_async_copy(zbuf.at[pl.ds(0, n)], xb_ref.at[pl.ds(do, n)], sem))
                return carry
            lax.fori_loop(0, N_EXPERTS, tail, 0)

            def unused(b, carry):
                do = pl.multiple_of(b * BM, BM)
                action(pltpu.make_async_copy(zbuf, xb_ref.at[pl.ds(do, BM)], sem))
                return carry
            lax.fori_loop(nu_ref[0], xb_ref.shape[0] // BM, unused, 0)
        fill(lambda c: c.start())
        fill(lambda c: c.wait())


def _dispatch(toff, dst, c8s, tstart, tlen, n_used, h2, pos, p_rows):
    n, d = h2.shape
    return pl.pallas_call(
        _dispatch_body,
        out_shape=jax.ShapeDtypeStruct((p_rows, d // 2), u32),
        grid_spec=pltpu.PrefetchScalarGridSpec(
            num_scalar_prefetch=6,
            grid=(n // (TG * TS),),
            in_specs=[pl.BlockSpec((TG * TS, d), lambda i, *_: (i, 0)),
                      pl.BlockSpec((TOP_K, TG * TS), lambda i, *_: (0, i))],
            out_specs=pl.BlockSpec(memory_space=pl.ANY),
            scratch_shapes=[pltpu.VMEM((2 * TG, RS, d // 2), u32), pltpu.VMEM((BM, d // 2), u32),
                            pltpu.SemaphoreType.DMA((2 * TG,)), pltpu.SemaphoreType.DMA(())]),
        compiler_params=_params(("arbitrary",)),
        name="dispatch",
    )(toff, dst, c8s, tstart, tlen, n_used, h2, pos)


def _expert_body(be_ref, nxt_ref, nu_ref, x_ref, wgu_hbm, bgu_ref, wd_hbm, bd_ref, o_ref,
                 wgu_st, wd_st, wgu_bf, wd_bf, sems):
    del nu_ref
    b = pl.program_id(0)
    d = D_MODEL
    e = be_ref[b]

    def weight_copies(ex):
        return (pltpu.make_async_copy(wgu_hbm.at[ex], wgu_st, sems.at[0]),
                pltpu.make_async_copy(wd_hbm.at[ex], wd_st, sems.at[1]))

    @pl.when(b == 0)
    def _():
        for c in weight_copies(e):
            c.start()

    @pl.when(jnp.logical_or(b == 0, e != be_ref[jnp.maximum(b - 1, 0)]))
    def _():
        for c in weight_copies(e):
            c.wait()
        wgu_bf[...] = wgu_st[...].astype(bf16)
        wd_bf[...] = wd_st[...].astype(bf16)
        nxt = nxt_ref[e]

        @pl.when(nxt >= 0)
        def _():
            for c in weight_copies(nxt):
                c.start()

    xb = jnp.concatenate(_unpack_halves(x_ref[...]), axis=1).astype(bf16)
    gu = jnp.dot(xb, wgu_bf[...], preferred_element_type=f32) + bgu_ref[0]
    glu = jnp.minimum(gu[:, :d], SWIGLU_LIMIT)
    lin = jnp.clip(gu[:, d:], -SWIGLU_LIMIT, SWIGLU_LIMIT)
    act = glu * _sigmoid(SWIGLU_ALPHA * glu) * (lin + 1.0)
    y = jnp.dot(act.astype(bf16), wd_bf[...], preferred_element_type=f32) + bd_ref[0]
    o_ref[...] = _pack_halves(y, exact=False)


def _experts(blk_e, nxt_e, n_used, xb, wgu, bgu, wd, bd):
    rows = pl.BlockSpec((BM, xb.shape[1]), lambda b, be, nx, nu: (b, 0))
    per_e = lambda a: pl.BlockSpec((1,) + a.shape[1:], lambda b, be, nx, nu: (be[b], 0, 0))
    hbm = pl.BlockSpec(memory_space=pl.ANY)
    return pl.pallas_call(
        _expert_body,
        out_shape=jax.ShapeDtypeStruct(xb.shape, xb.dtype),
        grid_spec=pltpu.PrefetchScalarGridSpec(
            num_scalar_prefetch=3,
            grid=(n_used[0],),
            in_specs=[rows, hbm, per_e(bgu), hbm, per_e(bd)],
            out_specs=rows,
            scratch_shapes=[pltpu.VMEM(wgu.shape[1:], f32), pltpu.VMEM(wd.shape[1:], f32),
                            pltpu.VMEM(wgu.shape[1:], bf16), pltpu.VMEM(wd.shape[1:], bf16),
                            pltpu.SemaphoreType.DMA((2,))]),
        input_output_aliases={3: 0},
        compiler_params=_params(("arbitrary",)),
        name="experts",
    )(blk_e, nxt_e, n_used, xb, wgu, bgu, wd, bd)


def _combine_body(toff_ref, dst_ref, c8_ref, yb_ref, pos_ref, gt_ref, x1_ref, mod_ref, g_ref,
                  o_ref, sbuf, sems):
    i = pl.program_id(0)
    base = lax.rem(i, 2) * TG

    def fetch(step, first_slot):
        for j in range(TG):
            def make(so, do, n, sl=first_slot + j):
                return pltpu.make_async_copy(yb_ref.at[pl.ds(do, n)], sbuf.at[sl, pl.ds(so, n)],
                                             sems.at[sl])
            _start_runs(step * TG + j, toff_ref, dst_ref, c8_ref, make)

    @pl.when(i == 0)
    def _():
        sbuf[...] = jnp.zeros_like(sbuf)
        fetch(i, base)

    @pl.when(i + 1 < pl.num_programs(0))
    def _():
        fetch(i + 1, TG - base)

    r = lax.broadcasted_iota(i32, (RS, TS), 0)
    pms, grows = [], []
    for j in range(TG):
        p = [pos_ref[k:k + 1, j * TS:(j + 1) * TS] for k in range(TOP_K)]
        gt = gt_ref[:, j * TS:(j + 1) * TS]
        gm = jnp.where(r == p[0], gt[0:1, :], jnp.where(r == p[1], gt[1:2, :],
             jnp.where(r == p[2], gt[2:3, :], jnp.where(r == p[3], gt[3:4, :], 0.0))))
        pms.append(jnp.where(gm != 0.0, 1.0, 0.0).astype(bf16))
        grows.append(jnp.sum(gm, axis=1, keepdims=True))

    for j in range(TG):
        n = _tile_rows(i * TG + j, toff_ref, c8_ref)
        pltpu.make_async_copy(yb_ref.at[pl.ds(0, n)], sbuf.at[base + j, pl.ds(0, n)],
                              sems.at[base + j]).wait()
    for j in range(TG):
        halves = [lax.dot_general(pms[j], (h * grows[j]).astype(bf16), (((0,), (0,)), ((), ())),
                                  preferred_element_type=f32)
                  for h in _unpack_halves(sbuf[base + j])]
        y = jnp.concatenate(halves, axis=1)
        rows = slice(j * TS, (j + 1) * TS)
        o_ref[rows, :] = x1_ref[rows, :] + mod_ref[0, 5:6, :] * _rms(y, g_ref[...])


def _combine(toff, dst, c8s, yb, pos, gt, x1, mod3, g, tiles_per_batch):
    n, d = x1.shape
    tok = pl.BlockSpec((TG * TS, d), lambda i, *_: (i, 0))
    lane = pl.BlockSpec((TOP_K, TG * TS), lambda i, *_: (0, i))
    steps_per_batch = tiles_per_batch // TG
    return pl.pallas_call(
        _combine_body,
        out_shape=jax.ShapeDtypeStruct((n, d), f32),
        grid_spec=pltpu.PrefetchScalarGridSpec(
            num_scalar_prefetch=3,
            grid=(n // (TG * TS),),
            in_specs=[pl.BlockSpec(memory_space=pl.ANY), lane, lane, tok,
                      pl.BlockSpec((1, 6, d), lambda i, *_: (i // steps_per_batch, 0, 0)),
                      pl.BlockSpec(g.shape, lambda i, *_: (0, 0))],
            out_specs=tok,
            scratch_shapes=[pltpu.VMEM((2 * TG, RS, d // 2), u32), pltpu.SemaphoreType.DMA((2 * TG,))]),
        compiler_params=_params(("arbitrary",)),
        name="combine",
    )(toff, dst, c8s, yb, pos, gt, x1, mod3, g)


def _layout_tables(tab, nt, p_rows):
    c8 = tab[:, :nt].T.astype(i32)
    toff = jnp.cumsum(c8, axis=1) - c8
    len8 = jnp.sum(c8, axis=0)
    seg = (len8 + BM - 1) // BM * BM
    gend = jnp.cumsum(seg)
    gstart = gend - seg
    dst = jnp.cumsum(c8, axis=0) - c8 + gstart[None, :]
    n_used = gend[-1] // BM
    blk = jnp.arange(p_rows // BM, dtype=i32)
    last = jnp.minimum(blk, n_used - 1)
    blk_e = jnp.sum((gend[None, :] <= (last * BM)[:, None]).astype(i32), axis=1)
    blk_e = jnp.minimum(blk_e, N_EXPERTS - 1)
    ids = jnp.arange(N_EXPERTS, dtype=i32)
    later = jnp.logical_and(ids[None, :] > ids[:, None], (seg > 0)[None, :])
    nxt_e = jnp.min(jnp.where(later, ids[None, :], N_EXPERTS), axis=1)
    nxt_e = jnp.where(nxt_e == N_EXPERTS, -1, nxt_e).astype(i32)
    return (toff.reshape(-1), dst.reshape(-1), c8.reshape(-1), gstart + len8, seg - len8,
            blk_e, nxt_e, n_used.reshape(1).astype(i32))


def kernel(x, c, w_ada, b_ada, g_pre_mix, g_post_mix, w_in, rel_bias, sgu_ln_g, sgu_ln_b,
           w_spatial, b_spatial, w_branch_a, w_branch_b, w_gate, b_gate, w_out,
           g_pre_ffn, g_post_ffn, w_router, b_router, w_gate_up, b_gate_up, w_down, b_down):
    b, s, d = x.shape
    assert d == D_MODEL and s % max(TM, TMP, TMX) == 0 and s % (TG * TS) == 0
    n = b * s
    nt = n // TS
    ntp = -(-nt // LANES) * LANES
    p_rows = -(-(n * TOP_K + nt * N_EXPERTS * (ROW_ALIGN - 1) + N_EXPERTS * (BM - 1)) // BM) * BM
    depth = w_ada.shape[0]
    c8 = jnp.pad(c, ((0, 8 - b), (0, 0)))
    row = lambda a: a.reshape(1, -1)

    for l in range(depth):
        mod = _ada(c8, w_ada[l], row(b_ada[l]))[:b]
        mod3 = mod.reshape(b, 6, d)

        q, k, v, u, vv = _proj(x, mod3, row(g_pre_mix[l]), w_in[l].astype(bf16),
                               row(sgu_ln_g[l]), row(sgu_ln_b[l]))
        ya = _attn(q, k, v, _attn_bias_vec(rel_bias[l]))
        ws2 = w_spatial[l].astype(bf16).reshape(-1, 2 * SGU_BLOCK, SGU_BLOCK)
        bsf = jnp.repeat(b_spatial[l].T, SGU_WIDTH // b_spatial.shape[1], axis=1)
        x1, h2, pos, gt, tab = _mix(x, mod3, row(g_pre_mix[l]), row(g_post_mix[l]), u, vv, ya, ws2, bsf,
                                   w_branch_a[l].astype(bf16), w_branch_b[l].astype(bf16),
                                   w_gate[l].astype(bf16), row(b_gate[l]), w_out[l].astype(bf16),
                                   row(g_pre_ffn[l]), w_router[l].T.astype(bf16),
                                   b_router[l].reshape(-1, 1), ntp)
        x1f = x1.reshape(n, d)
        h2 = h2.reshape(n, d)
        toff, dst, c8s, tstart, tlen, blk_e, nxt_e, n_used = _layout_tables(tab, nt, p_rows)
        xb = _dispatch(toff, dst, c8s, tstart, tlen, n_used, h2, pos, p_rows)
        yb = _experts(blk_e, nxt_e, n_used, xb, w_gate_up[l], b_gate_up[l][:, None, :],
                      w_down[l], b_down[l][:, None, :])
        x = _combine(toff, dst, c8s, yb, pos, gt, x1f, mod3, row(g_post_ffn[l]), s // TS).reshape(b, s, d)
    return x
```

```python
import jax
import jax.numpy as jnp
from jax import lax
from jax.experimental import pallas as pl
from jax.experimental.pallas import tpu as pltpu

bf16 = jnp.bfloat16
f32 = jnp.float32
i32 = jnp.int32
u32 = jnp.uint32

D_MODEL = 1024
CHUNK = 64
N_LEFT = 8
ATT_HEADS = 8
HEAD_DIM = 64
ATT_WIDTH = 512
MAX_REL = 128
SGU_BLOCK = 128
SGU_WIDTH = 512
N_EXPERTS = 32
TOP_K = 4
SWIGLU_LIMIT = 7.0
SWIGLU_ALPHA = 1.702
EPS = 1e-6
NEG = -1e30
LOG2E = 1.4426950408889634

LANES = 128
ROW_ALIGN = 8
TM = 512
TMP = 1024
TMX = 1024
QCH = 4
QG = QCH * CHUNK
KBAND = (N_LEFT + QCH) * CHUNK
BIAS_PERIOD = 1024
TS = 256
TG = 4
RS = TS * TOP_K + N_EXPERTS * ROW_ALIGN
BM = 512
CAST_ROWS = 64
VMEM_LIMIT = 56 * 2**20


def _params(sem):
    return pltpu.CompilerParams(dimension_semantics=sem, vmem_limit_bytes=VMEM_LIMIT)


def _adaln(x, g, sc, sh):
    ms = jnp.mean(x * x, axis=-1, keepdims=True)
    return (x * lax.rsqrt(ms + EPS) * g) * (1.0 + sc) + sh


def _rms(x, g):
    ms = jnp.mean(x * x, axis=-1, keepdims=True)
    return x * lax.rsqrt(ms + EPS) * g


def _sigmoid(x):
    return 1.0 / (1.0 + jnp.exp(-x))


def _ada_body(c_ref, w_ref, b_ref, o_ref):
    c = c_ref[...]
    ca = c * _sigmoid(c)
    o_ref[...] = jnp.dot(ca.astype(bf16), w_ref[...].astype(bf16),
                         preferred_element_type=f32) + b_ref[...]


def _ada(c8, w, b):
    d = w.shape[0]
    n = w.shape[1] // d
    return pl.pallas_call(
        _ada_body,
        out_shape=jax.ShapeDtypeStruct((8, n * d), f32),
        grid=(n,),
        in_specs=[pl.BlockSpec((8, d), lambda j: (0, 0)),
                  pl.BlockSpec((d, d), lambda j: (0, j)),
                  pl.BlockSpec((1, d), lambda j: (0, j))],
        out_specs=pl.BlockSpec((8, d), lambda j: (0, j)),
        compiler_params=_params(("arbitrary",)),
        name="ada",
    )(c8, w, b)


def _proj_body(x_ref, mod_ref, g_ref, w_ref, lng_ref, lnb_ref,
               q_ref, k_ref, v_ref, u_ref, vv_ref):
    h = _adaln(x_ref[0], g_ref[...], mod_ref[0, 1:2, :], mod_ref[0, 0:1, :]).astype(bf16)
    aw = ATT_WIDTH
    z = jnp.dot(h, w_ref[:, 3 * aw:], preferred_element_type=f32)
    p = jnp.dot(h, w_ref[:, :3 * aw], preferred_element_type=f32)
    q_ref[0] = (p[:, 0:aw] * (HEAD_DIM ** -0.5 * LOG2E)).astype(bf16)
    k_ref[0] = p[:, aw:2 * aw].astype(bf16)
    v_ref[0] = p[:, 2 * aw:3 * aw].astype(bf16)
    zg = 0.5 * z * (1.0 + lax.erf(z * (2.0 ** -0.5)))
    u_ref[0] = zg[:, :SGU_WIDTH].astype(bf16)
    vv = zg[:, SGU_WIDTH:]
    mu = jnp.mean(vv, axis=-1, keepdims=True)
    var = jnp.mean(jnp.square(vv - mu), axis=-1, keepdims=True)
    vn = (vv - mu) * lax.rsqrt(var + EPS) * lng_ref[...] + lnb_ref[...]
    vv_ref[0] = vn.astype(bf16)


def _proj(x, mod3, g, w_in, lng, lnb):
    b, s, d = x.shape
    tok = lambda w: pl.BlockSpec((1, TMP, w), lambda bi, i: (bi, i, 0))
    full = lambda a: pl.BlockSpec(a.shape, lambda bi, i: (0,) * a.ndim)
    o512 = jax.ShapeDtypeStruct((b, s, ATT_WIDTH), bf16)
    return pl.pallas_call(
        _proj_body,
        out_shape=(o512,) * 5,
        grid=(b, s // TMP),
        in_specs=[tok(d), pl.BlockSpec((1, 6, d), lambda bi, i: (bi, 0, 0)),
                  full(g), full(w_in), full(lng), full(lnb)],
        out_specs=(tok(ATT_WIDTH),) * 5,
        compiler_params=_params(("parallel", "arbitrary")),
        name="proj",
    )(x, mod3, g, w_in, lng, lnb)


def _attn_body(q_ref, kp_ref, kc_ref, vp_ref, vc_ref, bvec_ref, o_ref, kbuf, vbuf, bias_ref):
    first = pl.program_id(1) == 0
    nhp = ATT_HEADS // 2
    ones = jnp.ones((TM, LANES), bf16)

    @pl.when(first)
    def _():
        i = lax.broadcasted_iota(i32, (QG, KBAND), 0)
        j = lax.broadcasted_iota(i32, (QG, KBAND), 1)
        jb = j - (i // CHUNK) * CHUNK
        in_band = jnp.logical_and(jb >= 0, jb < CHUNK * (N_LEFT + 1))
        for h in range(ATT_HEADS):
            rows = jnp.broadcast_to(bvec_ref[h:h + 1, :], (QG, BIAS_PERIOD))
            toep = pltpu.roll(rows, 0, 1, stride=1, stride_axis=0)[:, :KBAND]
            bias_ref[h // 2, (h % 2) * QG:(h % 2 + 1) * QG, :] = jnp.where(in_band, toep, NEG)

    def fill(rows, k_src, v_src):
        kbuf[rows, :] = k_src[0]
        for hp in range(nhp):
            vbuf[rows, 2 * hp * LANES:(2 * hp + 1) * LANES] = v_src[0, :, hp * LANES:(hp + 1) * LANES]
            vbuf[rows, (2 * hp + 1) * LANES:(2 * hp + 2) * LANES] = ones

    fill(slice(TM, 2 * TM), kc_ref, vc_ref)

    @pl.when(first)
    def _():
        kbuf[0:TM, :] = jnp.zeros((TM, kbuf.shape[1]), bf16)
        vbuf[0:TM, :] = jnp.zeros((TM, vbuf.shape[1]), bf16)

    @pl.when(jnp.logical_not(first))
    def _():
        fill(slice(0, TM), kp_ref, vp_ref)

    lo = lax.broadcasted_iota(i32, (QG, LANES), 1) < HEAD_DIM

    def group(p, carry):
        r0 = pl.multiple_of(p * QG, QG)
        for hp in range(nhp):
            c0 = hp * LANES
            qp = q_ref[0, pl.ds(r0, QG), c0:c0 + LANES]
            zero = jnp.zeros_like(qp)
            q2 = jnp.concatenate([jnp.where(lo, qp, zero), jnp.where(lo, zero, qp)], axis=0)
            kb = kbuf[pl.ds(r0, KBAND), c0:c0 + LANES]
            s = lax.dot_general(q2, kb, (((1,), (1,)), ((), ())), preferred_element_type=f32)
            sb = (s + bias_ref[hp]).astype(bf16)
            m = jnp.max(sb, axis=-1, keepdims=True)
            e = jnp.exp2(sb - m)
            vb = vbuf[pl.ds(r0, KBAND), 2 * c0:2 * c0 + 2 * LANES]
            o2 = jnp.dot(e, vb, preferred_element_type=f32)
            on = o2[:, :LANES] / o2[:, LANES:]
            o = jnp.where(lo, on[:QG], on[QG:])
            o_ref[0, pl.ds(r0, QG), c0:c0 + LANES] = o.astype(bf16)
        return carry

    lax.fori_loop(0, TM // QG, group, 0, unroll=True)


def _attn(q, k, v, bvec):
    b, s, w = q.shape
    cur = pl.BlockSpec((1, TM, w), lambda bi, i: (bi, i, 0))
    prev = pl.BlockSpec((1, TM, w), lambda bi, i: (bi, jnp.maximum(i - 1, 0), 0))
    return pl.pallas_call(
        _attn_body,
        out_shape=jax.ShapeDtypeStruct((b, s, w), bf16),
        grid=(b, s // TM),
        in_specs=[cur, prev, cur, prev, cur,
                  pl.BlockSpec(bvec.shape, lambda bi, i: (0, 0))],
        out_specs=cur,
        scratch_shapes=[pltpu.VMEM((2 * TM, w), bf16), pltpu.VMEM((2 * TM, 2 * w), bf16),
                        pltpu.VMEM((ATT_HEADS // 2, 2 * QG, KBAND), f32)],
        compiler_params=_params(("parallel", "arbitrary")),
        name="attn",
    )(q, k, k, v, v, bvec)


def _attn_bias_vec(rel_bias):
    h = rel_bias.shape[0]
    n_far = N_LEFT * CHUNK - MAX_REL
    assert BIAS_PERIOD >= QG + KBAND - 1 and n_far >= 0
    far = jnp.broadcast_to(rel_bias[:, 2 * MAX_REL:], (h, BIAS_PERIOD))
    near = rel_bias[:, :0:-1]
    v = jnp.concatenate([far[:, :n_far], near, far[:, n_far + 2 * MAX_REL:]], axis=1)
    return v.astype(f32) * LOG2E


def _route(hb, wr_ref, br_ref):
    lg = lax.dot_general(wr_ref[...], hb, (((1,), (1,)), ((), ())),
                         preferred_element_type=f32) + br_ref[...]
    e_iota = lax.broadcasted_iota(i32, lg.shape, 0)
    vals, idxs = [], []
    hits = jnp.zeros(lg.shape, f32)
    for _ in range(TOP_K):
        m = jnp.max(lg, axis=0, keepdims=True)
        idx = jnp.min(jnp.where(lg == m, e_iota, N_EXPERTS), axis=0, keepdims=True)
        hit = e_iota == idx
        hits = hits + jnp.where(hit, 1.0, 0.0)
        lg = jnp.where(hit, -jnp.inf, lg)
        vals.append(m)
        idxs.append(idx)
    ex = [jnp.exp(v - vals[0]) for v in vals]
    den = ex[0] + ex[1] + ex[2] + ex[3]
    return (jnp.concatenate(idxs, axis=0), jnp.concatenate([e / den for e in ex], axis=0), hits)


def _mix_body(x_ref, mod_ref, gpre_ref, gpost_ref, u_ref, vv_ref, ya_ref, ws_ref, bs_ref,
              wa_ref, wb_ref, wg_ref, bg_ref, wo_ref, gffn_ref, wr_ref, br_ref,
              o_ref, h2_ref, pos_ref, gt_ref, tab_ref, ybuf):
    d = D_MODEL
    x = x_ref[0]
    h = _adaln(x, gpre_ref[...], mod_ref[0, 1:2, :], mod_ref[0, 0:1, :]).astype(bf16)
    gs = _sigmoid(jnp.dot(h, wg_ref[...], preferred_element_type=f32) + bg_ref[...])

    blk = SGU_BLOCK
    row = lax.broadcasted_iota(i32, (2 * blk, blk), 0)
    colv = lax.broadcasted_iota(i32, (2 * blk, blk), 1)
    causal = colv <= jnp.bitwise_and(row, blk - 1)
    lo = lax.broadcasted_iota(i32, (blk, LANES), 1) < (LANES // 2)
    for gp in range(SGU_WIDTH // LANES):
        c0 = gp * LANES
        w2 = ws_ref[gp]
        w2 = jnp.where(causal, w2, jnp.zeros_like(w2))
        for bi in range(TMX // blk):
            r0 = bi * blk
            s2 = jnp.dot(w2, vv_ref[0, r0:r0 + blk, c0:c0 + LANES], preferred_element_type=f32)
            s = jnp.where(lo, s2[:blk], s2[blk:]) + bs_ref[:, c0:c0 + LANES]
            yb = u_ref[0, r0:r0 + blk, c0:c0 + LANES].astype(f32) * s
            ybuf[r0:r0 + blk, c0:c0 + LANES] = yb.astype(bf16)

    a = jnp.dot(ya_ref[0], wa_ref[...], preferred_element_type=f32)
    bb = jnp.dot(ybuf[...], wb_ref[...], preferred_element_type=f32)
    merged = gs[:, :d] * a + gs[:, d:] * bb
    y = jnp.dot(merged.astype(bf16), wo_ref[...], preferred_element_type=f32)
    x1 = x + mod_ref[0, 2:3, :] * _rms(y, gpost_ref[...])
    o_ref[0] = x1

    hb = _adaln(x1, gffn_ref[...], mod_ref[0, 4:5, :], mod_ref[0, 3:4, :]).astype(bf16)
    h2_ref[0] = hb
    ids, gates, hits = _route(hb, wr_ref, br_ref)
    pos_ref[...] = jnp.concatenate(
        [jnp.concatenate(_sorted_positions(ids[:, j * TS:(j + 1) * TS]), axis=0)
         for j in range(TMX // TS)], axis=1)
    gt_ref[...] = gates

    step = pl.program_id(0) * pl.num_programs(1) + pl.program_id(1)

    @pl.when(step == 0)
    def _():
        tab_ref[...] = jnp.zeros_like(tab_ref)

    lane = lax.broadcasted_iota(i32, tab_ref.shape, 1)
    acc = tab_ref[...]
    for j in range(TMX // TS):
        cnt = jnp.sum(hits[:, j * TS:(j + 1) * TS], axis=1, keepdims=True)
        c8 = jnp.floor((cnt + (ROW_ALIGN - 1.0)) * (1.0 / ROW_ALIGN)) * ROW_ALIGN
        acc = acc + jnp.where(lane == step * (TMX // TS) + j, c8, 0.0)
    tab_ref[...] = acc


def _mix(x, mod3, gpre, gpost, u, vv, ya, ws2, bsf, wa, wb, wg, bg, wo, gffn, wrt, br, ntp):
    b, s, d = x.shape
    n = b * s
    tok = lambda w: pl.BlockSpec((1, TMX, w), lambda bi, i: (bi, i, 0))
    full = lambda a: pl.BlockSpec(a.shape, lambda bi, i: (0,) * a.ndim)
    lane = pl.BlockSpec((TOP_K, TMX), lambda bi, i: (0, bi * (s // TMX) + i))
    return pl.pallas_call(
        _mix_body,
        out_shape=(jax.ShapeDtypeStruct((b, s, d), f32),
                   jax.ShapeDtypeStruct((b, s, d), bf16),
                   jax.ShapeDtypeStruct((TOP_K, n), i32),
                   jax.ShapeDtypeStruct((TOP_K, n), f32),
                   jax.ShapeDtypeStruct((N_EXPERTS, ntp), f32)),
        grid=(b, s // TMX),
        in_specs=[tok(d), pl.BlockSpec((1, 6, d), lambda bi, i: (bi, 0, 0)),
                  full(gpre), full(gpost), tok(SGU_WIDTH), tok(SGU_WIDTH), tok(ATT_WIDTH),
                  full(ws2), full(bsf), full(wa), full(wb), full(wg), full(bg), full(wo),
                  full(gffn), full(wrt), full(br)],
        out_specs=(tok(d), tok(d), lane, lane,
                   pl.BlockSpec((N_EXPERTS, ntp), lambda bi, i: (0, 0))),
        scratch_shapes=[pltpu.VMEM((TMX, SGU_WIDTH), bf16)],
        compiler_params=_params(("arbitrary", "arbitrary")),
        name="mix",
    )(x, mod3, gpre, gpost, u, vv, ya, ws2, bsf, wa, wb, wg, bg, wo, gffn, wrt, br)


def _sorted_positions(ti):
    ts = ti.shape[1]
    e_iota = lax.broadcasted_iota(i32, (N_EXPERTS, ts), 0)
    upper = (lax.broadcasted_iota(i32, (ts, ts), 0) < lax.broadcasted_iota(i32, (ts, ts), 1))
    upper = jnp.where(upper, 1.0, 0.0).astype(bf16)
    hits, prefs, cnts = [], [], []
    for k in range(TOP_K):
        hit = ti[k:k + 1, :] == e_iota
        hf = jnp.where(hit, 1.0, 0.0)
        prefs.append(jnp.dot(hf.astype(bf16), upper, preferred_element_type=f32))
        cnts.append(jnp.sum(hf, axis=1, keepdims=True))
        hits.append(hit)
    total = cnts[0] + cnts[1] + cnts[2] + cnts[3]
    c8 = jnp.floor((total + (ROW_ALIGN - 1.0)) * (1.0 / ROW_ALIGN)) * ROW_ALIGN
    lower = (lax.broadcasted_iota(i32, (N_EXPERTS, N_EXPERTS), 1)
             < lax.broadcasted_iota(i32, (N_EXPERTS, N_EXPERTS), 0))
    lower = jnp.where(lower, 1.0, 0.0).astype(bf16)
    c8b = jnp.broadcast_to(c8, (N_EXPERTS, LANES)).astype(bf16)
    start = jnp.dot(lower, c8b, preferred_element_type=f32)[:, 0:1]
    pos = []
    for k in range(TOP_K):
        pe = start + prefs[k]
        pos.append(jnp.sum(jnp.where(hits[k], pe, 0.0), axis=0, keepdims=True).astype(i32))
        start = start + cnts[k]
    return pos


def _pack_halves(x, exact):
    half = x.shape[1] // 2
    lo = lax.bitcast_convert_type(x[:, :half], u32)
    hi = lax.bitcast_convert_type(x[:, half:], u32)
    if not exact:
        hi = jnp.bitwise_and(hi, jnp.uint32(0xFFFF0000))
    return jnp.bitwise_or(lax.shift_right_logical(lo, jnp.uint32(16)), hi)


def _unpack_halves(w):
    lo = lax.bitcast_convert_type(lax.shift_left(w, jnp.uint32(16)), f32)
    hi = lax.bitcast_convert_type(jnp.bitwise_and(w, jnp.uint32(0xFFFF0000)), f32)
    return lo, hi


def _start_runs(i, toff_ref, dst_ref, c8_ref, make):
    unroll = 4

    def some(q, carry):
        for k in range(unroll):
            j = i * N_EXPERTS + q * unroll + k
            n = pl.multiple_of(c8_ref[j], ROW_ALIGN)
            so = pl.multiple_of(toff_ref[j], ROW_ALIGN)
            do = pl.multiple_of(dst_ref[j], ROW_ALIGN)

            @pl.when(n > 0)
            def _():
                make(so, do, n).start(priority=k % 2)
        return carry
    lax.fori_loop(0, N_EXPERTS // unroll, some, 0)


def _tile_rows(i, toff_ref, c8_ref):
    j = i * N_EXPERTS + (N_EXPERTS - 1)
    return pl.multiple_of(toff_ref[j] + c8_ref[j], ROW_ALIGN)


def _dispatch_body(toff_ref, dst_ref, c8_ref, tstart_ref, tlen_ref, nu_ref,
                   h2_ref, pos_ref, xb_ref, sbuf, zbuf, sems, sem):
    i = pl.program_id(0)
    base = lax.rem(i, 2) * TG
    r = lax.broadcasted_iota(i32, (RS, TS), 0)
    for j in range(TG):
        p = [pos_ref[k:k + 1, j * TS:(j + 1) * TS] for k in range(TOP_K)]
        pm = jnp.where(r == p[0], 1.0, jnp.where(r == p[1], 1.0,
             jnp.where(r == p[2], 1.0, jnp.where(r == p[3], 1.0, 0.0))))
        srt = jnp.dot(pm.astype(bf16), h2_ref[j * TS:(j + 1) * TS, :], preferred_element_type=f32)
        sbuf[base + j] = _pack_halves(srt, exact=True)

    for j in range(TG):
        def make(so, do, n, sl=base + j):
            return pltpu.make_async_copy(sbuf.at[sl, pl.ds(so, n)], xb_ref.at[pl.ds(do, n)],
                                         sems.at[sl])
        _start_runs(i * TG + j, toff_ref, dst_ref, c8_ref, make)

    def wait_tile(t, sl):
        n = _tile_rows(t, toff_ref, c8_ref)
        pltpu.make_async_copy(sbuf.at[sl, pl.ds(0, n)], xb_ref.at[pl.ds(0, n)], sems.at[sl]).wait()

    @pl.when(i > 0)
    def _():
        for j in range(TG):
            wait_tile((i - 1) * TG + j, TG - base + j)

    @pl.when(i == pl.num_programs(0) - 1)
    def _():
        for j in range(TG):
            wait_tile(i * TG + j, base + j)
        zbuf[...] = jnp.zeros_like(zbuf)

        def fill(action):
            def tail(e, carry):
                n = pl.multiple_of(tlen_ref[e], ROW_ALIGN)
                do = pl.multiple_of(tstart_ref[e], ROW_ALIGN)

                @pl.when(n > 0)
                def _():
                    action(pltpu.make_async_copy(zbuf.at[pl.ds(0, n)], xb_ref.at[pl.ds(do, n)], sem))
                return carry
            lax.fori_loop(0, N_EXPERTS, tail, 0)

            def unused(b, carry):
                do = pl.multiple_of(b * BM, BM)
                action(pltpu.make_async_copy(zbuf, xb_ref.at[pl.ds(do, BM)], sem))
                return carry
            lax.fori_loop(nu_ref[0], xb_ref.shape[0] // BM, unused, 0)
        fill(lambda c: c.start())
        fill(lambda c: c.wait())


def _dispatch(toff, dst, c8s, tstart, tlen, n_used, h2, pos, p_rows):
    n, d = h2.shape
    return pl.pallas_call(
        _dispatch_body,
        out_shape=jax.ShapeDtypeStruct((p_rows, d // 2), u32),
        grid_spec=pltpu.PrefetchScalarGridSpec(
            num_scalar_prefetch=6,
            grid=(n // (TG * TS),),
            in_specs=[pl.BlockSpec((TG * TS, d), lambda i, *_: (i, 0)),
                      pl.BlockSpec((TOP_K, TG * TS), lambda i, *_: (0, i))],
            out_specs=pl.BlockSpec(memory_space=pl.ANY),
            scratch_shapes=[pltpu.VMEM((2 * TG, RS, d // 2), u32), pltpu.VMEM((BM, d // 2), u32),
                            pltpu.SemaphoreType.DMA((2 * TG,)), pltpu.SemaphoreType.DMA(())]),
        compiler_params=_params(("arbitrary",)),
        name="dispatch",
    )(toff, dst, c8s, tstart, tlen, n_used, h2, pos)


def _expert_body(be_ref, nxt_ref, nu_ref, x_ref, wgu_hbm, bgu_ref, wd_hbm, bd_ref, o_ref,
                 wgu_st, wd_st, wgu_bf, wd_bf, sems):
    del nu_ref
    b = pl.program_id(0)
    d = D_MODEL
    e = be_ref[b]

    def weight_copies(ex):
        return (pltpu.make_async_copy(wgu_hbm.at[ex], wgu_st, sems.at[0]),
                pltpu.make_async_copy(wd_hbm.at[ex], wd_st, sems.at[1]))

    @pl.when(b == 0)
    def _():
        for c in weight_copies(e):
            c.start()

    @pl.when(jnp.logical_or(b == 0, e != be_ref[jnp.maximum(b - 1, 0)]))
    def _():
        for c in weight_copies(e):
            c.wait()
        def cast_rows(c, carry):
            rows = pl.ds(pl.multiple_of(c * CAST_ROWS, CAST_ROWS), CAST_ROWS)
            wgu_bf[rows, :] = wgu_st[rows, :].astype(bf16)
            wd_bf[rows, :] = wd_st[rows, :].astype(bf16)
            return carry
        lax.fori_loop(0, d // CAST_ROWS, cast_rows, 0)
        nxt = nxt_ref[e]

        @pl.when(nxt >= 0)
        def _():
            for c in weight_copies(nxt):
                c.start()

    xb = jnp.concatenate(_unpack_halves(x_ref[...]), axis=1).astype(bf16)
    gu = jnp.dot(xb, wgu_bf[...], preferred_element_type=f32) + bgu_ref[0]
    glu = jnp.minimum(gu[:, :d], SWIGLU_LIMIT)
    lin = jnp.clip(gu[:, d:], -SWIGLU_LIMIT, SWIGLU_LIMIT)
    act = glu * _sigmoid(SWIGLU_ALPHA * glu) * (lin + 1.0)
    y = jnp.dot(act.astype(bf16), wd_bf[...], preferred_element_type=f32) + bd_ref[0]
    o_ref[...] = _pack_halves(y, exact=False)


def _experts(blk_e, nxt_e, n_used, xb, wgu, bgu, wd, bd):
    rows = pl.BlockSpec((BM, xb.shape[1]), lambda b, be, nx, nu: (b, 0))
    per_e = lambda a: pl.BlockSpec((1,) + a.shape[1:], lambda b, be, nx, nu: (be[b], 0, 0))
    hbm = pl.BlockSpec(memory_space=pl.ANY)
    return pl.pallas_call(
        _expert_body,
        out_shape=jax.ShapeDtypeStruct(xb.shape, xb.dtype),
        grid_spec=pltpu.PrefetchScalarGridSpec(
            num_scalar_prefetch=3,
            grid=(n_used[0],),
            in_specs=[rows, hbm, per_e(bgu), hbm, per_e(bd)],
            out_specs=rows,
            scratch_shapes=[pltpu.VMEM(wgu.shape[1:], f32), pltpu.VMEM(wd.shape[1:], f32),
                            pltpu.VMEM(wgu.shape[1:], bf16), pltpu.VMEM(wd.shape[1:], bf16),
                            pltpu.SemaphoreType.DMA((2,))]),
        input_output_aliases={3: 0},
        compiler_params=_params(("arbitrary",)),
        name="experts",
    )(blk_e, nxt_e, n_used, xb, wgu, bgu, wd, bd)


def _combine_body(toff_ref, dst_ref, c8_ref, yb_ref, pos_ref, gt_ref, x1_ref, mod_ref, g_ref,
                  o_ref, sbuf, sems):
    i = pl.program_id(0)
    base = lax.rem(i, 2) * TG

    def fetch(step, first_slot):
        for j in range(TG):
            def make(so, do, n, sl=first_slot + j):
                return pltpu.make_async_copy(yb_ref.at[pl.ds(do, n)], sbuf.at[sl, pl.ds(so, n)],
                                             sems.at[sl])
            _start_runs(step * TG + j, toff_ref, dst_ref, c8_ref, make)

    @pl.when(i == 0)
    def _():
        sbuf[...] = jnp.zeros_like(sbuf)
        fetch(i, base)

    @pl.when(i + 1 < pl.num_programs(0))
    def _():
        fetch(i + 1, TG - base)

    r = lax.broadcasted_iota(i32, (RS, TS), 0)
    pms, grows = [], []
    for j in range(TG):
        p = [pos_ref[k:k + 1, j * TS:(j + 1) * TS] for k in range(TOP_K)]
        gt = gt_ref[:, j * TS:(j + 1) * TS]
        gm = jnp.where(r == p[0], gt[0:1, :], jnp.where(r == p[1], gt[1:2, :],
             jnp.where(r == p[2], gt[2:3, :], jnp.where(r == p[3], gt[3:4, :], 0.0))))
        pms.append(jnp.where(gm != 0.0, 1.0, 0.0).astype(bf16))
        grows.append(jnp.sum(gm, axis=1, keepdims=True))

    for j in range(TG):
        n = _tile_rows(i * TG + j, toff_ref, c8_ref)
        pltpu.make_async_copy(yb_ref.at[pl.ds(0, n)], sbuf.at[base + j, pl.ds(0, n)],
                              sems.at[base + j]).wait()
    for j in range(TG):
        halves = [lax.dot_general(pms[j], (h * grows[j]).astype(bf16), (((0,), (0,)), ((), ())),
                                  preferred_element_type=f32)
                  for h in _unpack_halves(sbuf[base + j])]
        y = jnp.concatenate(halves, axis=1)
        rows = slice(j * TS, (j + 1) * TS)
        o_ref[rows, :] = x1_ref[rows, :] + mod_ref[0, 5:6, :] * _rms(y, g_ref[...])


def _combine(toff, dst, c8s, yb, pos, gt, x1, mod3, g, tiles_per_batch):
    n, d = x1.shape
    tok = pl.BlockSpec((TG * TS, d), lambda i, *_: (i, 0))
    lane = pl.BlockSpec((TOP_K, TG * TS), lambda i, *_: (0, i))
    steps_per_batch = tiles_per_batch // TG
    return pl.pallas_call(
        _combine_body,
        out_shape=jax.ShapeDtypeStruct((n, d), f32),
        grid_spec=pltpu.PrefetchScalarGridSpec(
            num_scalar_prefetch=3,
            grid=(n // (TG * TS),),
            in_specs=[pl.BlockSpec(memory_space=pl.ANY), lane, lane, tok,
                      pl.BlockSpec((1, 6, d), lambda i, *_: (i // steps_per_batch, 0, 0)),
                      pl.BlockSpec(g.shape, lambda i, *_: (0, 0))],
            out_specs=tok,
            scratch_shapes=[pltpu.VMEM((2 * TG, RS, d // 2), u32), pltpu.SemaphoreType.DMA((2 * TG,))]),
        compiler_params=_params(("arbitrary",)),
        name="combine",
    )(toff, dst, c8s, yb, pos, gt, x1, mod3, g)


def _layout_tables(tab, nt, p_rows):
    c8 = tab[:, :nt].T.astype(i32)
    toff = jnp.cumsum(c8, axis=1) - c8
    len8 = jnp.sum(c8, axis=0)
    seg = (len8 + BM - 1) // BM * BM
    gend = jnp.cumsum(seg)
    gstart = gend - seg
    dst = jnp.cumsum(c8, axis=0) - c8 + gstart[None, :]
    n_used = gend[-1] // BM
    blk = jnp.arange(p_rows // BM, dtype=i32)
    last = jnp.minimum(blk, n_used - 1)
    blk_e = jnp.sum((gend[None, :] <= (last * BM)[:, None]).astype(i32), axis=1)
    blk_e = jnp.minimum(blk_e, N_EXPERTS - 1)
    ids = jnp.arange(N_EXPERTS, dtype=i32)
    later = jnp.logical_and(ids[None, :] > ids[:, None], (seg > 0)[None, :])
    nxt_e = jnp.min(jnp.where(later, ids[None, :], N_EXPERTS), axis=1)
    nxt_e = jnp.where(nxt_e == N_EXPERTS, -1, nxt_e).astype(i32)
    return (toff.reshape(-1), dst.reshape(-1), c8.reshape(-1), gstart + len8, seg - len8,
            blk_e, nxt_e, n_used.reshape(1).astype(i32))


def kernel(x, c, w_ada, b_ada, g_pre_mix, g_post_mix, w_in, rel_bias, sgu_ln_g, sgu_ln_b,
           w_spatial, b_spatial, w_branch_a, w_branch_b, w_gate, b_gate, w_out,
           g_pre_ffn, g_post_ffn, w_router, b_router, w_gate_up, b_gate_up, w_down, b_down):
    b, s, d = x.shape
    assert d == D_MODEL and s % max(TM, TMP, TMX) == 0 and s % (TG * TS) == 0
    n = b * s
    nt = n // TS
    ntp = -(-nt // LANES) * LANES
    p_rows = -(-(n * TOP_K + nt * N_EXPERTS * (ROW_ALIGN - 1) + N_EXPERTS * (BM - 1)) // BM) * BM
    depth = w_ada.shape[0]
    c8 = jnp.pad(c, ((0, 8 - b), (0, 0)))
    row = lambda a: a.reshape(1, -1)

    for l in range(depth):
        mod = _ada(c8, w_ada[l], row(b_ada[l]))[:b]
        mod3 = mod.reshape(b, 6, d)

        q, k, v, u, vv = _proj(x, mod3, row(g_pre_mix[l]), w_in[l].astype(bf16),
                               row(sgu_ln_g[l]), row(sgu_ln_b[l]))
        ya = _attn(q, k, v, _attn_bias_vec(rel_bias[l]))
        ws2 = w_spatial[l].astype(bf16).reshape(-1, 2 * SGU_BLOCK, SGU_BLOCK)
        bsf = jnp.repeat(b_spatial[l].T, SGU_WIDTH // b_spatial.shape[1], axis=1)
        x1, h2, pos, gt, tab = _mix(x, mod3, row(g_pre_mix[l]), row(g_post_mix[l]), u, vv, ya, ws2, bsf,
                                   w_branch_a[l].astype(bf16), w_branch_b[l].astype(bf16),
                                   w_gate[l].astype(bf16), row(b_gate[l]), w_out[l].astype(bf16),
                                   row(g_pre_ffn[l]), w_router[l].T.astype(bf16),
                                   b_router[l].reshape(-1, 1), ntp)
        x1f = x1.reshape(n, d)
        h2 = h2.reshape(n, d)
        toff, dst, c8s, tstart, tlen, blk_e, nxt_e, n_used = _layout_tables(tab, nt, p_rows)
        xb = _dispatch(toff, dst, c8s, tstart, tlen, n_used, h2, pos, p_rows)
        yb = _experts(blk_e, nxt_e, n_used, xb, w_gate_up[l], b_gate_up[l][:, None, :],
                      w_down[l], b_down[l][:, None, :])
        x = _combine(toff, dst, c8s, yb, pos, gt, x1f, mod3, row(g_post_ffn[l]), s // TS).reshape(b, s, d)
    return x
```

```python
import jax
import jax.numpy as jnp
from jax import lax
from jax.experimental import pallas as pl
from jax.experimental.pallas import tpu as pltpu

bf16 = jnp.bfloat16
f32 = jnp.float32
i32 = jnp.int32
u32 = jnp.uint32

D_MODEL = 1024
CHUNK = 64
N_LEFT = 8
ATT_HEADS = 8
HEAD_DIM = 64
ATT_WIDTH = 512
MAX_REL = 128
SGU_BLOCK = 128
SGU_WIDTH = 512
N_EXPERTS = 32
TOP_K = 4
SWIGLU_LIMIT = 7.0
SWIGLU_ALPHA = 1.702
EPS = 1e-6
NEG = -1e30
LOG2E = 1.4426950408889634

LANES = 128
ROW_ALIGN = 8
TM = 512
TMP = 1024
TMX = 1024
QCH = 4
QG = QCH * CHUNK
KBAND = (N_LEFT + QCH) * CHUNK
BIAS_PERIOD = 1024
TS = 256
TG = 4
RS = TS * TOP_K + N_EXPERTS * ROW_ALIGN
BM = 512
CAST_ROWS = 64
VMEM_LIMIT = 56 * 2**20


def _params(sem):
    return pltpu.CompilerParams(dimension_semantics=sem, vmem_limit_bytes=VMEM_LIMIT)


def _adaln(x, g, sc, sh):
    ms = jnp.mean(x * x, axis=-1, keepdims=True)
    return (x * lax.rsqrt(ms + EPS) * g) * (1.0 + sc) + sh


def _rms(x, g):
    ms = jnp.mean(x * x, axis=-1, keepdims=True)
    return x * lax.rsqrt(ms + EPS) * g


def _sigmoid(x):
    return 1.0 / (1.0 + jnp.exp(-x))


def _ada_body(c_ref, w_ref, b_ref, o_ref):
    c = c_ref[...]
    ca = c * _sigmoid(c)
    o_ref[...] = jnp.dot(ca.astype(bf16), w_ref[...].astype(bf16),
                         preferred_element_type=f32) + b_ref[...]


def _ada(c8, w, b):
    d = w.shape[0]
    n = w.shape[1] // d
    return pl.pallas_call(
        _ada_body,
        out_shape=jax.ShapeDtypeStruct((8, n * d), f32),
        grid=(n,),
        in_specs=[pl.BlockSpec((8, d), lambda j: (0, 0)),
                  pl.BlockSpec((d, d), lambda j: (0, j)),
                  pl.BlockSpec((1, d), lambda j: (0, j))],
        out_specs=pl.BlockSpec((8, d), lambda j: (0, j)),
        compiler_params=_params(("arbitrary",)),
        name="ada",
    )(c8, w, b)


def _proj_body(x_ref, mod_ref, g_ref, w_ref, lng_ref, lnb_ref,
               q_ref, k_ref, v_ref, u_ref, vv_ref):
    h = _adaln(x_ref[0], g_ref[...], mod_ref[0, 1:2, :], mod_ref[0, 0:1, :]).astype(bf16)
    aw = ATT_WIDTH
    z = jnp.dot(h, w_ref[:, 3 * aw:], preferred_element_type=f32)
    p = jnp.dot(h, w_ref[:, :3 * aw], preferred_element_type=f32)
    q_ref[0] = (p[:, 0:aw] * (HEAD_DIM ** -0.5 * LOG2E)).astype(bf16)
    k_ref[0] = p[:, aw:2 * aw].astype(bf16)
    v_ref[0] = p[:, 2 * aw:3 * aw].astype(bf16)
    zg = 0.5 * z * (1.0 + lax.erf(z * (2.0 ** -0.5)))
    u_ref[0] = zg[:, :SGU_WIDTH].astype(bf16)
    vv = zg[:, SGU_WIDTH:]
    mu = jnp.mean(vv, axis=-1, keepdims=True)
    var = jnp.mean(jnp.square(vv - mu), axis=-1, keepdims=True)
    vn = (vv - mu) * lax.rsqrt(var + EPS) * lng_ref[...] + lnb_ref[...]
    vv_ref[0] = vn.astype(bf16)


def _proj(x, mod3, g, w_in, lng, lnb):
    b, s, d = x.shape
    tok = lambda w: pl.BlockSpec((1, TMP, w), lambda bi, i: (bi, i, 0))
    full = lambda a: pl.BlockSpec(a.shape, lambda bi, i: (0,) * a.ndim)
    o512 = jax.ShapeDtypeStruct((b, s, ATT_WIDTH), bf16)
    return pl.pallas_call(
        _proj_body,
        out_shape=(o512,) * 5,
        grid=(b, s // TMP),
        in_specs=[tok(d), pl.BlockSpec((1, 6, d), lambda bi, i: (bi, 0, 0)),
                  full(g), full(w_in), full(lng), full(lnb)],
        out_specs=(tok(ATT_WIDTH),) * 5,
        compiler_params=_params(("parallel", "arbitrary")),
        name="proj",
    )(x, mod3, g, w_in, lng, lnb)


def _attn_body(q_ref, kp_ref, kc_ref, vp_ref, vc_ref, bvec_ref, o_ref, bias_ref):
    first = pl.program_id(1) == 0
    nhp = ATT_HEADS // 2

    @pl.when(first)
    def _():
        i = lax.broadcasted_iota(i32, (QG, KBAND), 0)
        j = lax.broadcasted_iota(i32, (QG, KBAND), 1)
        jb = j - (i // CHUNK) * CHUNK
        in_band = jnp.logical_and(jb >= 0, jb < CHUNK * (N_LEFT + 1))
        for h in range(ATT_HEADS):
            rows = jnp.broadcast_to(bvec_ref[h:h + 1, :], (QG, BIAS_PERIOD))
            toep = pltpu.roll(rows, 0, 1, stride=1, stride_axis=0)[:, :KBAND]
            bias_ref[h // 2, (h % 2) * QG:(h % 2 + 1) * QG, :] = jnp.where(in_band, toep, NEG)

    lo = lax.broadcasted_iota(i32, (QG, LANES), 1) < HEAD_DIM
    has_prev = jnp.logical_not(first)
    for p in range(TM // QG):
        r0 = p * QG
        n_cur = KBAND - (TM - r0)
        ind = jnp.concatenate([jnp.where(has_prev, 1.0, 0.0) * jnp.ones((TM - r0, LANES), f32),
                               jnp.ones((n_cur, LANES), f32)], axis=0).astype(bf16)
        for hp in range(nhp):
            c0 = hp * LANES
            qp = q_ref[0, r0:r0 + QG, c0:c0 + LANES]
            zero = jnp.zeros_like(qp)
            q2 = jnp.concatenate([jnp.where(lo, qp, zero), jnp.where(lo, zero, qp)], axis=0)
            kprev = kp_ref[0, r0:TM, c0:c0 + LANES]
            vprev = vp_ref[0, r0:TM, c0:c0 + LANES]
            kprev = jnp.where(has_prev, kprev, jnp.zeros_like(kprev))
            vprev = jnp.where(has_prev, vprev, jnp.zeros_like(vprev))
            kb = jnp.concatenate([kprev, kc_ref[0, 0:n_cur, c0:c0 + LANES]], axis=0)
            vb = jnp.concatenate([vprev, vc_ref[0, 0:n_cur, c0:c0 + LANES]], axis=0)
            s = lax.dot_general(q2, kb, (((1,), (1,)), ((), ())), preferred_element_type=f32)
            sb = (s + bias_ref[hp]).astype(bf16)
            m = jnp.max(sb, axis=-1, keepdims=True)
            e = jnp.exp2(sb - m)
            o2 = jnp.dot(e, jnp.concatenate([vb, ind], axis=1),
                         preferred_element_type=f32)
            on = o2[:, :LANES] / o2[:, LANES:]
            o = jnp.where(lo, on[:QG], on[QG:])
            o_ref[0, r0:r0 + QG, c0:c0 + LANES] = o.astype(bf16)


def _attn(q, k, v, bvec):
    b, s, w = q.shape
    cur = pl.BlockSpec((1, TM, w), lambda bi, i: (bi, i, 0))
    prev = pl.BlockSpec((1, TM, w), lambda bi, i: (bi, jnp.maximum(i - 1, 0), 0))
    return pl.pallas_call(
        _attn_body,
        out_shape=jax.ShapeDtypeStruct((b, s, w), bf16),
        grid=(b, s // TM),
        in_specs=[cur, prev, cur, prev, cur,
                  pl.BlockSpec(bvec.shape, lambda bi, i: (0, 0))],
        out_specs=cur,
        scratch_shapes=[pltpu.VMEM((ATT_HEADS // 2, 2 * QG, KBAND), f32)],
        compiler_params=_params(("parallel", "arbitrary")),
        name="attn",
    )(q, k, k, v, v, bvec)


def _attn_bias_vec(rel_bias):
    h = rel_bias.shape[0]
    n_far = N_LEFT * CHUNK - MAX_REL
    assert BIAS_PERIOD >= QG + KBAND - 1 and n_far >= 0
    far = jnp.broadcast_to(rel_bias[:, 2 * MAX_REL:], (h, BIAS_PERIOD))
    near = rel_bias[:, :0:-1]
    v = jnp.concatenate([far[:, :n_far], near, far[:, n_far + 2 * MAX_REL:]], axis=1)
    return v.astype(f32) * LOG2E


def _route(hb, wr_ref, br_ref):
    lg = lax.dot_general(wr_ref[...], hb, (((1,), (1,)), ((), ())),
                         preferred_element_type=f32) + br_ref[...]
    e_iota = lax.broadcasted_iota(i32, lg.shape, 0)
    vals, idxs = [], []
    hits = jnp.zeros(lg.shape, f32)
    for _ in range(TOP_K):
        m = jnp.max(lg, axis=0, keepdims=True)
        idx = jnp.min(jnp.where(lg == m, e_iota, N_EXPERTS), axis=0, keepdims=True)
        hit = e_iota == idx
        hits = hits + jnp.where(hit, 1.0, 0.0)
        lg = jnp.where(hit, -jnp.inf, lg)
        vals.append(m)
        idxs.append(idx)
    ex = [jnp.exp(v - vals[0]) for v in vals]
    den = ex[0] + ex[1] + ex[2] + ex[3]
    return (jnp.concatenate(idxs, axis=0), jnp.concatenate([e / den for e in ex], axis=0), hits)


def _mix_body(x_ref, mod_ref, gpre_ref, gpost_ref, u_ref, vv_ref, ya_ref, ws_ref, bs_ref,
              wa_ref, wb_ref, wg_ref, bg_ref, wo_ref, gffn_ref, wr_ref, br_ref,
              o_ref, h2_ref, pos_ref, gt_ref, tab_ref, ybuf):
    d = D_MODEL
    x = x_ref[0]
    h = _adaln(x, gpre_ref[...], mod_ref[0, 1:2, :], mod_ref[0, 0:1, :]).astype(bf16)
    gs = _sigmoid(jnp.dot(h, wg_ref[...], preferred_element_type=f32) + bg_ref[...])

    blk = SGU_BLOCK
    row = lax.broadcasted_iota(i32, (2 * blk, blk), 0)
    colv = lax.broadcasted_iota(i32, (2 * blk, blk), 1)
    causal = colv <= jnp.bitwise_and(row, blk - 1)
    lo = lax.broadcasted_iota(i32, (blk, LANES), 1) < (LANES // 2)
    for gp in range(SGU_WIDTH // LANES):
        c0 = gp * LANES
        w2 = ws_ref[gp]
        w2 = jnp.where(causal, w2, jnp.zeros_like(w2))
        for bi in range(TMX // blk):
            r0 = bi * blk
            s2 = jnp.dot(w2, vv_ref[0, r0:r0 + blk, c0:c0 + LANES], preferred_element_type=f32)
            s = jnp.where(lo, s2[:blk], s2[blk:]) + bs_ref[:, c0:c0 + LANES]
            yb = u_ref[0, r0:r0 + blk, c0:c0 + LANES].astype(f32) * s
            ybuf[r0:r0 + blk, c0:c0 + LANES] = yb.astype(bf16)

    a = jnp.dot(ya_ref[0], wa_ref[...], preferred_element_type=f32)
    bb = jnp.dot(ybuf[...], wb_ref[...], preferred_element_type=f32)
    merged = gs[:, :d] * a + gs[:, d:] * bb
    y = jnp.dot(merged.astype(bf16), wo_ref[...], preferred_element_type=f32)
    x1 = x + mod_ref[0, 2:3, :] * _rms(y, gpost_ref[...])
    o_ref[0] = x1

    hb = _adaln(x1, gffn_ref[...], mod_ref[0, 4:5, :], mod_ref[0, 3:4, :]).astype(bf16)
    h2_ref[0] = hb
    ids, gates, hits = _route(hb, wr_ref, br_ref)
    pos_ref[...] = jnp.concatenate(
        [jnp.concatenate(_sorted_positions(ids[:, j * TS:(j + 1) * TS]), axis=0)
         for j in range(TMX // TS)], axis=1)
    gt_ref[...] = gates

    step = pl.program_id(0) * pl.num_programs(1) + pl.program_id(1)

    @pl.when(step == 0)
    def _():
        tab_ref[...] = jnp.zeros_like(tab_ref)

    lane = lax.broadcasted_iota(i32, tab_ref.shape, 1)
    acc = tab_ref[...]
    for j in range(TMX // TS):
        cnt = jnp.sum(hits[:, j * TS:(j + 1) * TS], axis=1, keepdims=True)
        c8 = jnp.floor((cnt + (ROW_ALIGN - 1.0)) * (1.0 / ROW_ALIGN)) * ROW_ALIGN
        acc = acc + jnp.where(lane == step * (TMX // TS) + j, c8, 0.0)
    tab_ref[...] = acc


def _mix(x, mod3, gpre, gpost, u, vv, ya, ws2, bsf, wa, wb, wg, bg, wo, gffn, wrt, br, ntp):
    b, s, d = x.shape
    n = b * s
    tok = lambda w: pl.BlockSpec((1, TMX, w), lambda bi, i: (bi, i, 0))
    full = lambda a: pl.BlockSpec(a.shape, lambda bi, i: (0,) * a.ndim)
    lane = pl.BlockSpec((TOP_K, TMX), lambda bi, i: (0, bi * (s // TMX) + i))
    return pl.pallas_call(
        _mix_body,
        out_shape=(jax.ShapeDtypeStruct((b, s, d), f32),
                   jax.ShapeDtypeStruct((b, s, d), bf16),
                   jax.ShapeDtypeStruct((TOP_K, n), i32),
                   jax.ShapeDtypeStruct((TOP_K, n), f32),
                   jax.ShapeDtypeStruct((N_EXPERTS, ntp), f32)),
        grid=(b, s // TMX),
        in_specs=[tok(d), pl.BlockSpec((1, 6, d), lambda bi, i: (bi, 0, 0)),
                  full(gpre), full(gpost), tok(SGU_WIDTH), tok(SGU_WIDTH), tok(ATT_WIDTH),
                  full(ws2), full(bsf), full(wa), full(wb), full(wg), full(bg), full(wo),
                  full(gffn), full(wrt), full(br)],
        out_specs=(tok(d), tok(d), lane, lane,
                   pl.BlockSpec((N_EXPERTS, ntp), lambda bi, i: (0, 0))),
        scratch_shapes=[pltpu.VMEM((TMX, SGU_WIDTH), bf16)],
        compiler_params=_params(("arbitrary", "arbitrary")),
        name="mix",
    )(x, mod3, gpre, gpost, u, vv, ya, ws2, bsf, wa, wb, wg, bg, wo, gffn, wrt, br)


def _sorted_positions(ti):
    ts = ti.shape[1]
    e_iota = lax.broadcasted_iota(i32, (N_EXPERTS, ts), 0)
    upper = (lax.broadcasted_iota(i32, (ts, ts), 0) < lax.broadcasted_iota(i32, (ts, ts), 1))
    upper = jnp.where(upper, 1.0, 0.0).astype(bf16)
    hits, prefs, cnts = [], [], []
    for k in range(TOP_K):
        hit = ti[k:k + 1, :] == e_iota
        hf = jnp.where(hit, 1.0, 0.0)
        prefs.append(jnp.dot(hf.astype(bf16), upper, preferred_element_type=f32))
        cnts.append(jnp.sum(hf, axis=1, keepdims=True))
        hits.append(hit)
    total = cnts[0] + cnts[1] + cnts[2] + cnts[3]
    c8 = jnp.floor((total + (ROW_ALIGN - 1.0)) * (1.0 / ROW_ALIGN)) * ROW_ALIGN
    lower = (lax.broadcasted_iota(i32, (N_EXPERTS, N_EXPERTS), 1)
             < lax.broadcasted_iota(i32, (N_EXPERTS, N_EXPERTS), 0))
    lower = jnp.where(lower, 1.0, 0.0).astype(bf16)
    c8b = jnp.broadcast_to(c8, (N_EXPERTS, LANES)).astype(bf16)
    start = jnp.dot(lower, c8b, preferred_element_type=f32)[:, 0:1]
    pos = []
    for k in range(TOP_K):
        pe = start + prefs[k]
        pos.append(jnp.sum(jnp.where(hits[k], pe, 0.0), axis=0, keepdims=True).astype(i32))
        start = start + cnts[k]
    return pos


def _pack_halves(x, exact):
    half = x.shape[1] // 2
    lo = lax.bitcast_convert_type(x[:, :half], u32)
    hi = lax.bitcast_convert_type(x[:, half:], u32)
    if not exact:
        hi = jnp.bitwise_and(hi, jnp.uint32(0xFFFF0000))
    return jnp.bitwise_or(lax.shift_right_logical(lo, jnp.uint32(16)), hi)


def _unpack_halves(w):
    lo = lax.bitcast_convert_type(lax.shift_left(w, jnp.uint32(16)), f32)
    hi = lax.bitcast_convert_type(jnp.bitwise_and(w, jnp.uint32(0xFFFF0000)), f32)
    return lo, hi


def _start_runs(i, toff_ref, dst_ref, c8_ref, make):
    unroll = 4

    def some(q, carry):
        for k in range(unroll):
            j = i * N_EXPERTS + q * unroll + k
            n = pl.multiple_of(c8_ref[j], ROW_ALIGN)
            so = pl.multiple_of(toff_ref[j], ROW_ALIGN)
            do = pl.multiple_of(dst_ref[j], ROW_ALIGN)

            @pl.when(n > 0)
            def _():
                make(so, do, n).start(priority=k % 2)
        return carry
    lax.fori_loop(0, N_EXPERTS // unroll, some, 0)


def _tile_rows(i, toff_ref, c8_ref):
    j = i * N_EXPERTS + (N_EXPERTS - 1)
    return pl.multiple_of(toff_ref[j] + c8_ref[j], ROW_ALIGN)


def _dispatch_body(toff_ref, dst_ref, c8_ref, tstart_ref, tlen_ref, nu_ref,
                   h2_ref, pos_ref, xb_ref, sbuf, zbuf, sems, sem):
    i = pl.program_id(0)
    base = lax.rem(i, 2) * TG
    r = lax.broadcasted_iota(i32, (RS, TS), 0)
    for j in range(TG):
        p = [pos_ref[k:k + 1, j * TS:(j + 1) * TS] for k in range(TOP_K)]
        pm = jnp.where(r == p[0], 1.0, jnp.where(r == p[1], 1.0,
             jnp.where(r == p[2], 1.0, jnp.where(r == p[3], 1.0, 0.0))))
        srt = jnp.dot(pm.astype(bf16), h2_ref[j * TS:(j + 1) * TS, :], preferred_element_type=f32)
        sbuf[base + j] = _pack_halves(srt, exact=True)

    for j in range(TG):
        def make(so, do, n, sl=base + j):
            return pltpu.make_async_copy(sbuf.at[sl, pl.ds(so, n)], xb_ref.at[pl.ds(do, n)],
                                         sems.at[sl])
        _start_runs(i * TG + j, toff_ref, dst_ref, c8_ref, make)

    def wait_tile(t, sl):
        n = _tile_rows(t, toff_ref, c8_ref)
        pltpu.make_async_copy(sbuf.at[sl, pl.ds(0, n)], xb_ref.at[pl.ds(0, n)], sems.at[sl]).wait()

    @pl.when(i > 0)
    def _():
        for j in range(TG):
            wait_tile((i - 1) * TG + j, TG - base + j)

    @pl.when(i == pl.num_programs(0) - 1)
    def _():
        for j in range(TG):
            wait_tile(i * TG + j, base + j)
        zbuf[...] = jnp.zeros_like(zbuf)

        def fill(action):
            def tail(e, carry):
                n = pl.multiple_of(tlen_ref[e], ROW_ALIGN)
                do = pl.multiple_of(tstart_ref[e], ROW_ALIGN)

                @pl.when(n > 0)
                def _():
                    action(pltpu.make_async_copy(zbuf.at[pl.ds(0, n)], xb_ref.at[pl.ds(do, n)], sem))
                return carry
            lax.fori_loop(0, N_EXPERTS, tail, 0)

            def unused(b, carry):
                do = pl.multiple_of(b * BM, BM)
                action(pltpu.make_async_copy(zbuf, xb_ref.at[pl.ds(do, BM)], sem))
                return carry
            lax.fori_loop(nu_ref[0], xb_ref.shape[0] // BM, unused, 0)
        fill(lambda c: c.start())
        fill(lambda c: c.wait())


def _dispatch(toff, dst, c8s, tstart, tlen, n_used, h2, pos, p_rows):
    n, d = h2.shape
    return pl.pallas_call(
        _dispatch_body,
        out_shape=jax.ShapeDtypeStruct((p_rows, d // 2), u32),
        grid_spec=pltpu.PrefetchScalarGridSpec(
            num_scalar_prefetch=6,
            grid=(n // (TG * TS),),
            in_specs=[pl.BlockSpec((TG * TS, d), lambda i, *_: (i, 0)),
                      pl.BlockSpec((TOP_K, TG * TS), lambda i, *_: (0, i))],
            out_specs=pl.BlockSpec(memory_space=pl.ANY),
            scratch_shapes=[pltpu.VMEM((2 * TG, RS, d // 2), u32), pltpu.VMEM((BM, d // 2), u32),
                            pltpu.SemaphoreType.DMA((2 * TG,)), pltpu.SemaphoreType.DMA(())]),
        compiler_params=_params(("arbitrary",)),
        name="dispatch",
    )(toff, dst, c8s, tstart, tlen, n_used, h2, pos)


def _expert_body(be_ref, nxt_ref, nu_ref, x_ref, wgu_hbm, bgu_ref, wd_hbm, bd_ref, o_ref,
                 wgu_st, wd_st, wgu_bf, wd_bf, sems):
    del nu_ref
    b = pl.program_id(0)
    d = D_MODEL
    e = be_ref[b]

    def weight_copies(ex):
        return (pltpu.make_async_copy(wgu_hbm.at[ex], wgu_st, sems.at[0]),
                pltpu.make_async_copy(wd_hbm.at[ex], wd_st, sems.at[1]))

    @pl.when(b == 0)
    def _():
        for c in weight_copies(e):
            c.start()

    @pl.when(jnp.logical_or(b == 0, e != be_ref[jnp.maximum(b - 1, 0)]))
    def _():
        for c in weight_copies(e):
            c.wait()
        def cast_rows(c, carry):
            rows = pl.ds(pl.multiple_of(c * CAST_ROWS, CAST_ROWS), CAST_ROWS)
            wgu_bf[rows, :] = wgu_st[rows, :].astype(bf16)
            wd_bf[rows, :] = wd_st[rows, :].astype(bf16)
            return carry
        lax.fori_loop(0, d // CAST_ROWS, cast_rows, 0)
        nxt = nxt_ref[e]

        @pl.when(nxt >= 0)
        def _():
            for c in weight_copies(nxt):
                c.start()

    xb = jnp.concatenate(_unpack_halves(x_ref[...]), axis=1).astype(bf16)
    gu = jnp.dot(xb, wgu_bf[...], preferred_element_type=f32) + bgu_ref[0]
    glu = jnp.minimum(gu[:, :d], SWIGLU_LIMIT)
    lin = jnp.clip(gu[:, d:], -SWIGLU_LIMIT, SWIGLU_LIMIT)
    act = glu * _sigmoid(SWIGLU_ALPHA * glu) * (lin + 1.0)
    y = jnp.dot(act.astype(bf16), wd_bf[...], preferred_element_type=f32) + bd_ref[0]
    o_ref[...] = _pack_halves(y, exact=False)


def _experts(blk_e, nxt_e, n_used, xb, wgu, bgu, wd, bd):
    rows = pl.BlockSpec((BM, xb.shape[1]), lambda b, be, nx, nu: (b, 0))
    per_e = lambda a: pl.BlockSpec((1,) + a.shape[1:], lambda b, be, nx, nu: (be[b], 0, 0))
    hbm = pl.BlockSpec(memory_space=pl.ANY)
    return pl.pallas_call(
        _expert_body,
        out_shape=jax.ShapeDtypeStruct(xb.shape, xb.dtype),
        grid_spec=pltpu.PrefetchScalarGridSpec(
            num_scalar_prefetch=3,
            grid=(n_used[0],),
            in_specs=[rows, hbm, per_e(bgu), hbm, per_e(bd)],
            out_specs=rows,
            scratch_shapes=[pltpu.VMEM(wgu.shape[1:], f32), pltpu.VMEM(wd.shape[1:], f32),
                            pltpu.VMEM(wgu.shape[1:], bf16), pltpu.VMEM(wd.shape[1:], bf16),
                            pltpu.SemaphoreType.DMA((2,))]),
        input_output_aliases={3: 0},
        compiler_params=_params(("arbitrary",)),
        name="experts",
    )(blk_e, nxt_e, n_used, xb, wgu, bgu, wd, bd)


def _combine_body(toff_ref, dst_ref, c8_ref, yb_ref, pos_ref, gt_ref, x1_ref, mod_ref, g_ref,
                  o_ref, sbuf, sems):
    i = pl.program_id(0)
    base = lax.rem(i, 2) * TG

    def fetch(step, first_slot):
        for j in range(TG):
            def make(so, do, n, sl=first_slot + j):
                return pltpu.make_async_copy(yb_ref.at[pl.ds(do, n)], sbuf.at[sl, pl.ds(so, n)],
                                             sems.at[sl])
            _start_runs(step * TG + j, toff_ref, dst_ref, c8_ref, make)

    @pl.when(i == 0)
    def _():
        sbuf[...] = jnp.zeros_like(sbuf)
        fetch(i, base)

    @pl.when(i + 1 < pl.num_programs(0))
    def _():
        fetch(i + 1, TG - base)

    r = lax.broadcasted_iota(i32, (RS, TS), 0)
    pms, grows = [], []
    for j in range(TG):
        p = [pos_ref[k:k + 1, j * TS:(j + 1) * TS] for k in range(TOP_K)]
        gt = gt_ref[:, j * TS:(j + 1) * TS]
        gm = jnp.where(r == p[0], gt[0:1, :], jnp.where(r == p[1], gt[1:2, :],
             jnp.where(r == p[2], gt[2:3, :], jnp.where(r == p[3], gt[3:4, :], 0.0))))
        pms.append(jnp.where(gm != 0.0, 1.0, 0.0).astype(bf16))
        grows.append(jnp.sum(gm, axis=1, keepdims=True))

    for j in range(TG):
        n = _tile_rows(i * TG + j, toff_ref, c8_ref)
        pltpu.make_async_copy(yb_ref.at[pl.ds(0, n)], sbuf.at[base + j, pl.ds(0, n)],
                              sems.at[base + j]).wait()
    for j in range(TG):
        halves = [lax.dot_general(pms[j], (h * grows[j]).astype(bf16), (((0,), (0,)), ((), ())),
                                  preferred_element_type=f32)
                  for h in _unpack_halves(sbuf[base + j])]
        y = jnp.concatenate(halves, axis=1)
        rows = slice(j * TS, (j + 1) * TS)
        o_ref[rows, :] = x1_ref[rows, :] + mod_ref[0, 5:6, :] * _rms(y, g_ref[...])


def _combine(toff, dst, c8s, yb, pos, gt, x1, mod3, g, tiles_per_batch):
    n, d = x1.shape
    tok = pl.BlockSpec((TG * TS, d), lambda i, *_: (i, 0))
    lane = pl.BlockSpec((TOP_K, TG * TS), lambda i, *_: (0, i))
    steps_per_batch = tiles_per_batch // TG
    return pl.pallas_call(
        _combine_body,
        out_shape=jax.ShapeDtypeStruct((n, d), f32),
        grid_spec=pltpu.PrefetchScalarGridSpec(
            num_scalar_prefetch=3,
            grid=(n // (TG * TS),),
            in_specs=[pl.BlockSpec(memory_space=pl.ANY), lane, lane, tok,
                      pl.BlockSpec((1, 6, d), lambda i, *_: (i // steps_per_batch, 0, 0)),
                      pl.BlockSpec(g.shape, lambda i, *_: (0, 0))],
            out_specs=tok,
            scratch_shapes=[pltpu.VMEM((2 * TG, RS, d // 2), u32), pltpu.SemaphoreType.DMA((2 * TG,))]),
        compiler_params=_params(("arbitrary",)),
        name="combine",
    )(toff, dst, c8s, yb, pos, gt, x1, mod3, g)


def _layout_tables(tab, nt, p_rows):
    c8 = tab[:, :nt].T.astype(i32)
    toff = jnp.cumsum(c8, axis=1) - c8
    len8 = jnp.sum(c8, axis=0)
    seg = (len8 + BM - 1) // BM * BM
    gend = jnp.cumsum(seg)
    gstart = gend - seg
    dst = jnp.cumsum(c8, axis=0) - c8 + gstart[None, :]
    n_used = gend[-1] // BM
    blk = jnp.arange(p_rows // BM, dtype=i32)
    last = jnp.minimum(blk, n_used - 1)
    blk_e = jnp.sum((gend[None, :] <= (last * BM)[:, None]).astype(i32), axis=1)
    blk_e = jnp.minimum(blk_e, N_EXPERTS - 1)
    ids = jnp.arange(N_EXPERTS, dtype=i32)
    later = jnp.logical_and(ids[None, :] > ids[:, None], (seg > 0)[None, :])
    nxt_e = jnp.min(jnp.where(later, ids[None, :], N_EXPERTS), axis=1)
    nxt_e = jnp.where(nxt_e == N_EXPERTS, -1, nxt_e).astype(i32)
    return (toff.reshape(-1), dst.reshape(-1), c8.reshape(-1), gstart + len8, seg - len8,
            blk_e, nxt_e, n_used.reshape(1).astype(i32))


def kernel(x, c, w_ada, b_ada, g_pre_mix, g_post_mix, w_in, rel_bias, sgu_ln_g, sgu_ln_b,
           w_spatial, b_spatial, w_branch_a, w_branch_b, w_gate, b_gate, w_out,
           g_pre_ffn, g_post_ffn, w_router, b_router, w_gate_up, b_gate_up, w_down, b_down):
    b, s, d = x.shape
    assert d == D_MODEL and s % max(TM, TMP, TMX) == 0 and s % (TG * TS) == 0
    n = b * s
    nt = n // TS
    ntp = -(-nt // LANES) * LANES
    p_rows = -(-(n * TOP_K + nt * N_EXPERTS * (ROW_ALIGN - 1) + N_EXPERTS * (BM - 1)) // BM) * BM
    depth = w_ada.shape[0]
    c8 = jnp.pad(c, ((0, 8 - b), (0, 0)))
    row = lambda a: a.reshape(1, -1)

    for l in range(depth):
        mod = _ada(c8, w_ada[l], row(b_ada[l]))[:b]
        mod3 = mod.reshape(b, 6, d)

        q, k, v, u, vv = _proj(x, mod3, row(g_pre_mix[l]), w_in[l].astype(bf16),
                               row(sgu_ln_g[l]), row(sgu_ln_b[l]))
        ya = _attn(q, k, v, _attn_bias_vec(rel_bias[l]))
        ws2 = w_spatial[l].astype(bf16).reshape(-1, 2 * SGU_BLOCK, SGU_BLOCK)
        bsf = jnp.repeat(b_spatial[l].T, SGU_WIDTH // b_spatial.shape[1], axis=1)
        x1, h2, pos, gt, tab = _mix(x, mod3, row(g_pre_mix[l]), row(g_post_mix[l]), u, vv, ya, ws2, bsf,
                                   w_branch_a[l].astype(bf16), w_branch_b[l].astype(bf16),
                                   w_gate[l].astype(bf16), row(b_gate[l]), w_out[l].astype(bf16),
                                   row(g_pre_ffn[l]), w_router[l].T.astype(bf16),
                                   b_router[l].reshape(-1, 1), ntp)
        x1f = x1.reshape(n, d)
        h2 = h2.reshape(n, d)
        toff, dst, c8s, tstart, tlen, blk_e, nxt_e, n_used = _layout_tables(tab, nt, p_rows)
        xb = _dispatch(toff, dst, c8s, tstart, tlen, n_used, h2, pos, p_rows)
        yb = _experts(blk_e, nxt_e, n_used, xb, w_gate_up[l], b_gate_up[l][:, None, :],
                      w_down[l], b_down[l][:, None, :])
        x = _combine(toff, dst, c8s, yb, pos, gt, x1f, mod3, row(g_post_ffn[l]), s // TS).reshape(b, s, d)
    return x
```

```python
import jax
import jax.numpy as jnp
from jax import lax
from jax.experimental import pallas as pl
from jax.experimental.pallas import tpu as pltpu

bf16 = jnp.bfloat16
f32 = jnp.float32
i32 = jnp.int32
u32 = jnp.uint32

D_MODEL = 1024
CHUNK = 64
N_LEFT = 8
ATT_HEADS = 8
HEAD_DIM = 64
ATT_WIDTH = 512
MAX_REL = 128
SGU_BLOCK = 128
SGU_WIDTH = 512
N_EXPERTS = 32
TOP_K = 4
SWIGLU_LIMIT = 7.0
SWIGLU_ALPHA = 1.702
EPS = 1e-6
NEG = -1e30
LOG2E = 1.4426950408889634

LANES = 128
ROW_ALIGN = 8
TM = 512
TMP = 1024
TMX = 1024
QCH = 4
QG = QCH * CHUNK
KBAND = (N_LEFT + QCH) * CHUNK
BIAS_PERIOD = 1024
TS = 256
TG = 4
RS = TS * TOP_K + N_EXPERTS * ROW_ALIGN
BM = 512
CAST_ROWS = 64
VMEM_LIMIT = 56 * 2**20


def _params(sem):
    return pltpu.CompilerParams(dimension_semantics=sem, vmem_limit_bytes=VMEM_LIMIT)


def _adaln(x, g, sc, sh):
    ms = jnp.mean(x * x, axis=-1, keepdims=True)
    return (x * lax.rsqrt(ms + EPS) * g) * (1.0 + sc) + sh


def _rms(x, g):
    ms = jnp.mean(x * x, axis=-1, keepdims=True)
    return x * lax.rsqrt(ms + EPS) * g


def _sigmoid(x):
    return 1.0 / (1.0 + jnp.exp(-x))


def _ada_body(c_ref, w_ref, b_ref, o_ref):
    c = c_ref[...]
    ca = c * _sigmoid(c)
    o_ref[...] = jnp.dot(ca.astype(bf16), w_ref[...].astype(bf16),
                         preferred_element_type=f32) + b_ref[...]


def _ada(c8, w, b):
    d = w.shape[0]
    n = w.shape[1] // d
    return pl.pallas_call(
        _ada_body,
        out_shape=jax.ShapeDtypeStruct((8, n * d), f32),
        grid=(n,),
        in_specs=[pl.BlockSpec((8, d), lambda j: (0, 0)),
                  pl.BlockSpec((d, d), lambda j: (0, j)),
                  pl.BlockSpec((1, d), lambda j: (0, j))],
        out_specs=pl.BlockSpec((8, d), lambda j: (0, j)),
        compiler_params=_params(("arbitrary",)),
        name="ada",
    )(c8, w, b)


def _proj_body(x_ref, mod_ref, g_ref, w_ref, lng_ref, lnb_ref,
               q_ref, k_ref, v_ref, u_ref, vv_ref):
    h = _adaln(x_ref[0], g_ref[...], mod_ref[0, 1:2, :], mod_ref[0, 0:1, :]).astype(bf16)
    aw = ATT_WIDTH
    z = jnp.dot(h, w_ref[:, 3 * aw:], preferred_element_type=f32)
    p = jnp.dot(h, w_ref[:, :3 * aw], preferred_element_type=f32)
    q_ref[0] = (p[:, 0:aw] * (HEAD_DIM ** -0.5 * LOG2E)).astype(bf16)
    k_ref[0] = p[:, aw:2 * aw].astype(bf16)
    v_ref[0] = p[:, 2 * aw:3 * aw].astype(bf16)
    zg = 0.5 * z * (1.0 + lax.erf(z * (2.0 ** -0.5)))
    u_ref[0] = zg[:, :SGU_WIDTH].astype(bf16)
    vv = zg[:, SGU_WIDTH:]
    mu = jnp.mean(vv, axis=-1, keepdims=True)
    var = jnp.mean(jnp.square(vv - mu), axis=-1, keepdims=True)
    vn = (vv - mu) * lax.rsqrt(var + EPS) * lng_ref[...] + lnb_ref[...]
    vv_ref[0] = vn.astype(bf16)


def _proj(x, mod3, g, w_in, lng, lnb):
    b, s, d = x.shape
    tok = lambda w: pl.BlockSpec((1, TMP, w), lambda bi, i: (bi, i, 0))
    full = lambda a: pl.BlockSpec(a.shape, lambda bi, i: (0,) * a.ndim)
    o512 = jax.ShapeDtypeStruct((b, s, ATT_WIDTH), bf16)
    return pl.pallas_call(
        _proj_body,
        out_shape=(o512,) * 5,
        grid=(b, s // TMP),
        in_specs=[tok(d), pl.BlockSpec((1, 6, d), lambda bi, i: (bi, 0, 0)),
                  full(g), full(w_in), full(lng), full(lnb)],
        out_specs=(tok(ATT_WIDTH),) * 5,
        compiler_params=_params(("parallel", "arbitrary")),
        name="proj",
    )(x, mod3, g, w_in, lng, lnb)


def _attn_body(q_ref, kp_ref, kc_ref, vp_ref, vc_ref, bvec_ref, o_ref, bias_ref):
    first = pl.program_id(1) == 0
    nhp = ATT_HEADS // 2

    @pl.when(first)
    def _():
        i = lax.broadcasted_iota(i32, (QG, KBAND), 0)
        j = lax.broadcasted_iota(i32, (QG, KBAND), 1)
        jb = j - (i // CHUNK) * CHUNK
        in_band = jnp.logical_and(jb >= 0, jb < CHUNK * (N_LEFT + 1))
        for h in range(ATT_HEADS):
            rows = jnp.broadcast_to(bvec_ref[h:h + 1, :], (QG, BIAS_PERIOD))
            toep = pltpu.roll(rows, 0, 1, stride=1, stride_axis=0)[:, :KBAND]
            bias_ref[h // 2, (h % 2) * QG:(h % 2 + 1) * QG, :] = jnp.where(in_band, toep, NEG)

    lo = lax.broadcasted_iota(i32, (QG, LANES), 1) < HEAD_DIM
    has_prev = jnp.logical_not(first)
    for p in range(TM // QG):
        r0 = p * QG
        n_cur = KBAND - (TM - r0)
        ind = jnp.concatenate([jnp.where(has_prev, 1.0, 0.0) * jnp.ones((TM - r0, LANES), f32),
                               jnp.ones((n_cur, LANES), f32)], axis=0).astype(bf16)
        for hp in range(nhp):
            c0 = hp * LANES
            qp = q_ref[0, r0:r0 + QG, c0:c0 + LANES]
            zero = jnp.zeros_like(qp)
            q2 = jnp.concatenate([jnp.where(lo, qp, zero), jnp.where(lo, zero, qp)], axis=0)
            kprev = kp_ref[0, r0:TM, c0:c0 + LANES]
            vprev = vp_ref[0, r0:TM, c0:c0 + LANES]
            kprev = jnp.where(has_prev, kprev, jnp.zeros_like(kprev))
            vprev = jnp.where(has_prev, vprev, jnp.zeros_like(vprev))
            kb = jnp.concatenate([kprev, kc_ref[0, 0:n_cur, c0:c0 + LANES]], axis=0)
            vb = jnp.concatenate([vprev, vc_ref[0, 0:n_cur, c0:c0 + LANES]], axis=0)
            s = lax.dot_general(q2, kb, (((1,), (1,)), ((), ())), preferred_element_type=f32)
            sb = (s + bias_ref[hp]).astype(bf16)
            m = jnp.max(sb, axis=-1, keepdims=True)
            e = jnp.exp2(sb - m)
            o2 = jnp.dot(e, jnp.concatenate([vb, ind], axis=1),
                         preferred_element_type=f32)
            on = o2[:, :LANES] / o2[:, LANES:]
            o = jnp.where(lo, on[:QG], on[QG:])
            o_ref[0, r0:r0 + QG, c0:c0 + LANES] = o.astype(bf16)


def _attn(q, k, v, bvec):
    b, s, w = q.shape
    cur = pl.BlockSpec((1, TM, w), lambda bi, i: (bi, i, 0))
    prev = pl.BlockSpec((1, TM, w), lambda bi, i: (bi, jnp.maximum(i - 1, 0), 0))
    return pl.pallas_call(
        _attn_body,
        out_shape=jax.ShapeDtypeStruct((b, s, w), bf16),
        grid=(b, s // TM),
        in_specs=[cur, prev, cur, prev, cur,
                  pl.BlockSpec(bvec.shape, lambda bi, i: (0, 0))],
        out_specs=cur,
        scratch_shapes=[pltpu.VMEM((ATT_HEADS // 2, 2 * QG, KBAND), f32)],
        compiler_params=_params(("parallel", "arbitrary")),
        name="attn",
    )(q, k, k, v, v, bvec)


def _attn_bias_vec(rel_bias):
    h = rel_bias.shape[0]
    n_far = N_LEFT * CHUNK - MAX_REL
    assert BIAS_PERIOD >= QG + KBAND - 1 and n_far >= 0
    far = jnp.broadcast_to(rel_bias[:, 2 * MAX_REL:], (h, BIAS_PERIOD))
    near = rel_bias[:, :0:-1]
    v = jnp.concatenate([far[:, :n_far], near, far[:, n_far + 2 * MAX_REL:]], axis=1)
    return v.astype(f32) * LOG2E


def _route(hb, wr_ref, br_ref):
    lg = lax.dot_general(wr_ref[...], hb, (((1,), (1,)), ((), ())),
                         preferred_element_type=f32) + br_ref[...]
    e_iota = lax.broadcasted_iota(i32, lg.shape, 0)
    vals, idxs = [], []
    hits = jnp.zeros(lg.shape, f32)
    for _ in range(TOP_K):
        m = jnp.max(lg, axis=0, keepdims=True)
        idx = jnp.min(jnp.where(lg == m, e_iota, N_EXPERTS), axis=0, keepdims=True)
        hit = e_iota == idx
        hits = hits + jnp.where(hit, 1.0, 0.0)
        lg = jnp.where(hit, -jnp.inf, lg)
        vals.append(m)
        idxs.append(idx)
    ex = [jnp.exp(v - vals[0]) for v in vals]
    den = ex[0] + ex[1] + ex[2] + ex[3]
    return (jnp.concatenate(idxs, axis=0), jnp.concatenate([e / den for e in ex], axis=0), hits)


def _mix_body(x_ref, mod_ref, gpre_ref, gpost_ref, u_ref, vv_ref, ya_ref, ws_ref, bs_ref,
              wa_ref, wb_ref, wg_ref, bg_ref, wo_ref, gffn_ref, wr_ref, br_ref,
              o_ref, h2_ref, pos_ref, gt_ref, tab_ref, ybuf):
    d = D_MODEL
    x = x_ref[0]
    h = _adaln(x, gpre_ref[...], mod_ref[0, 1:2, :], mod_ref[0, 0:1, :]).astype(bf16)
    gs = _sigmoid(jnp.dot(h, wg_ref[...], preferred_element_type=f32) + bg_ref[...])

    blk = SGU_BLOCK
    row = lax.broadcasted_iota(i32, (2 * blk, blk), 0)
    colv = lax.broadcasted_iota(i32, (2 * blk, blk), 1)
    causal = colv <= jnp.bitwise_and(row, blk - 1)
    lo = lax.broadcasted_iota(i32, (blk, LANES), 1) < (LANES // 2)
    for gp in range(SGU_WIDTH // LANES):
        c0 = gp * LANES
        w2 = ws_ref[gp]
        w2 = jnp.where(causal, w2, jnp.zeros_like(w2))
        for bi in range(TMX // blk):
            r0 = bi * blk
            s2 = jnp.dot(w2, vv_ref[0, r0:r0 + blk, c0:c0 + LANES], preferred_element_type=f32)
            s = jnp.where(lo, s2[:blk], s2[blk:]) + bs_ref[:, c0:c0 + LANES]
            yb = u_ref[0, r0:r0 + blk, c0:c0 + LANES].astype(f32) * s
            ybuf[r0:r0 + blk, c0:c0 + LANES] = yb.astype(bf16)

    a = jnp.dot(ya_ref[0], wa_ref[...], preferred_element_type=f32)
    bb = jnp.dot(ybuf[...], wb_ref[...], preferred_element_type=f32)
    merged = gs[:, :d] * a + gs[:, d:] * bb
    y = jnp.dot(merged.astype(bf16), wo_ref[...], preferred_element_type=f32)
    x1 = x + mod_ref[0, 2:3, :] * _rms(y, gpost_ref[...])
    o_ref[0] = x1

    hb = _adaln(x1, gffn_ref[...], mod_ref[0, 4:5, :], mod_ref[0, 3:4, :]).astype(bf16)
    h2_ref[0] = hb
    ids, gates, hits = _route(hb, wr_ref, br_ref)
    pos_ref[...] = jnp.concatenate(
        [jnp.concatenate(_sorted_positions(ids[:, j * TS:(j + 1) * TS]), axis=0)
         for j in range(TMX // TS)], axis=1)
    gt_ref[...] = gates

    step = pl.program_id(0) * pl.num_programs(1) + pl.program_id(1)

    @pl.when(step == 0)
    def _():
        tab_ref[...] = jnp.zeros_like(tab_ref)

    lane = lax.broadcasted_iota(i32, tab_ref.shape, 1)
    acc = tab_ref[...]
    for j in range(TMX // TS):
        cnt = jnp.sum(hits[:, j * TS:(j + 1) * TS], axis=1, keepdims=True)
        c8 = jnp.floor((cnt + (ROW_ALIGN - 1.0)) * (1.0 / ROW_ALIGN)) * ROW_ALIGN
        acc = acc + jnp.where(lane == step * (TMX // TS) + j, c8, 0.0)
    tab_ref[...] = acc


def _mix(x, mod3, gpre, gpost, u, vv, ya, ws2, bsf, wa, wb, wg, bg, wo, gffn, wrt, br, ntp):
    b, s, d = x.shape
    n = b * s
    tok = lambda w: pl.BlockSpec((1, TMX, w), lambda bi, i: (bi, i, 0))
    full = lambda a: pl.BlockSpec(a.shape, lambda bi, i: (0,) * a.ndim)
    lane = pl.BlockSpec((TOP_K, TMX), lambda bi, i: (0, bi * (s // TMX) + i))
    return pl.pallas_call(
        _mix_body,
        out_shape=(jax.ShapeDtypeStruct((b, s, d), f32),
                   jax.ShapeDtypeStruct((b, s, d), bf16),
                   jax.ShapeDtypeStruct((TOP_K, n), i32),
                   jax.ShapeDtypeStruct((TOP_K, n), f32),
                   jax.ShapeDtypeStruct((N_EXPERTS, ntp), f32)),
        grid=(b, s // TMX),
        in_specs=[tok(d), pl.BlockSpec((1, 6, d), lambda bi, i: (bi, 0, 0)),
                  full(gpre), full(gpost), tok(SGU_WIDTH), tok(SGU_WIDTH), tok(ATT_WIDTH),
                  full(ws2), full(bsf), full(wa), full(wb), full(wg), full(bg), full(wo),
                  full(gffn), full(wrt), full(br)],
        out_specs=(tok(d), tok(d), lane, lane,
                   pl.BlockSpec((N_EXPERTS, ntp), lambda bi, i: (0, 0))),
        scratch_shapes=[pltpu.VMEM((TMX, SGU_WIDTH), bf16)],
        compiler_params=_params(("arbitrary", "arbitrary")),
        name="mix",
    )(x, mod3, gpre, gpost, u, vv, ya, ws2, bsf, wa, wb, wg, bg, wo, gffn, wrt, br)


def _sorted_positions(ti):
    ts = ti.shape[1]
    e_iota = lax.broadcasted_iota(i32, (N_EXPERTS, ts), 0)
    upper = (lax.broadcasted_iota(i32, (ts, ts), 0) < lax.broadcasted_iota(i32, (ts, ts), 1))
    upper = jnp.where(upper, 1.0, 0.0).astype(bf16)
    hits, prefs, cnts = [], [], []
    for k in range(TOP_K):
        hit = ti[k:k + 1, :] == e_iota
        hf = jnp.where(hit, 1.0, 0.0)
        prefs.append(jnp.dot(hf.astype(bf16), upper, preferred_element_type=f32))
        cnts.append(jnp.sum(hf, axis=1, keepdims=True))
        hits.append(hit)
    total = cnts[0] + cnts[1] + cnts[2] + cnts[3]
    c8 = jnp.floor((total + (ROW_ALIGN - 1.0)) * (1.0 / ROW_ALIGN)) * ROW_ALIGN
    lower = (lax.broadcasted_iota(i32, (N_EXPERTS, N_EXPERTS), 1)
             < lax.broadcasted_iota(i32, (N_EXPERTS, N_EXPERTS), 0))
    lower = jnp.where(lower, 1.0, 0.0).astype(bf16)
    c8b = jnp.broadcast_to(c8, (N_EXPERTS, LANES)).astype(bf16)
    start = jnp.dot(lower, c8b, preferred_element_type=f32)[:, 0:1]
    pos = []
    for k in range(TOP_K):
        pe = start + prefs[k]
        pos.append(jnp.sum(jnp.where(hits[k], pe, 0.0), axis=0, keepdims=True).astype(i32))
        start = start + cnts[k]
    return pos


def _pack_halves(x, exact):
    half = x.shape[1] // 2
    lo = lax.bitcast_convert_type(x[:, :half], u32)
    hi = lax.bitcast_convert_type(x[:, half:], u32)
    if not exact:
        hi = jnp.bitwise_and(hi, jnp.uint32(0xFFFF0000))
    return jnp.bitwise_or(lax.shift_right_logical(lo, jnp.uint32(16)), hi)


def _unpack_halves(w):
    lo = lax.bitcast_convert_type(lax.shift_left(w, jnp.uint32(16)), f32)
    hi = lax.bitcast_convert_type(jnp.bitwise_and(w, jnp.uint32(0xFFFF0000)), f32)
    return lo, hi


def _start_runs(i, toff_ref, dst_ref, c8_ref, make):
    unroll = 4

    def some(q, carry):
        for k in range(unroll):
            j = i * N_EXPERTS + q * unroll + k
            n = pl.multiple_of(c8_ref[j], ROW_ALIGN)
            so = pl.multiple_of(toff_ref[j], ROW_ALIGN)
            do = pl.multiple_of(dst_ref[j], ROW_ALIGN)

            @pl.when(n > 0)
            def _():
                make(so, do, n).start(priority=k % 2)
        return carry
    lax.fori_loop(0, N_EXPERTS // unroll, some, 0)


def _tile_rows(i, toff_ref, c8_ref):
    j = i * N_EXPERTS + (N_EXPERTS - 1)
    return pl.multiple_of(toff_ref[j] + c8_ref[j], ROW_ALIGN)


def _dispatch_body(toff_ref, dst_ref, c8_ref, tstart_ref, tlen_ref, nu_ref,
                   h2_ref, pos_ref, xb_ref, sbuf, zbuf, sems, sem):
    i = pl.program_id(0)
    base = lax.rem(i, 2) * TG

    def fill(action):
        def tail(e, carry):
            n = pl.multiple_of(tlen_ref[e], ROW_ALIGN)
            do = pl.multiple_of(tstart_ref[e], ROW_ALIGN)

            @pl.when(n > 0)
            def _():
                action(pltpu.make_async_copy(zbuf.at[pl.ds(0, n)], xb_ref.at[pl.ds(do, n)], sem))
            return carry
        lax.fori_loop(0, N_EXPERTS, tail, 0)

        def unused(b, carry):
            do = pl.multiple_of(b * BM, BM)
            action(pltpu.make_async_copy(zbuf, xb_ref.at[pl.ds(do, BM)], sem))
            return carry
        lax.fori_loop(nu_ref[0], xb_ref.shape[0] // BM, unused, 0)

    @pl.when(i == 0)
    def _():
        zbuf[...] = jnp.zeros_like(zbuf)
        fill(lambda c: c.start())

    r = lax.broadcasted_iota(i32, (RS, TS), 0)
    for j in range(TG):
        p = [pos_ref[k:k + 1, j * TS:(j + 1) * TS] for k in range(TOP_K)]
        pm = jnp.where(r == p[0], 1.0, jnp.where(r == p[1], 1.0,
             jnp.where(r == p[2], 1.0, jnp.where(r == p[3], 1.0, 0.0))))
        srt = jnp.dot(pm.astype(bf16), h2_ref[j * TS:(j + 1) * TS, :], preferred_element_type=f32)
        sbuf[base + j] = _pack_halves(srt, exact=True)

    for j in range(TG):
        def make(so, do, n, sl=base + j):
            return pltpu.make_async_copy(sbuf.at[sl, pl.ds(so, n)], xb_ref.at[pl.ds(do, n)],
                                         sems.at[sl])
        _start_runs(i * TG + j, toff_ref, dst_ref, c8_ref, make)

    def wait_tile(t, sl):
        n = _tile_rows(t, toff_ref, c8_ref)
        pltpu.make_async_copy(sbuf.at[sl, pl.ds(0, n)], xb_ref.at[pl.ds(0, n)], sems.at[sl]).wait()

    @pl.when(i > 0)
    def _():
        for j in range(TG):
            wait_tile((i - 1) * TG + j, TG - base + j)

    @pl.when(i == pl.num_programs(0) - 1)
    def _():
        for j in range(TG):
            wait_tile(i * TG + j, base + j)
        fill(lambda c: c.wait())


def _dispatch(toff, dst, c8s, tstart, tlen, n_used, h2, pos, p_rows):
    n, d = h2.shape
    return pl.pallas_call(
        _dispatch_body,
        out_shape=jax.ShapeDtypeStruct((p_rows, d // 2), u32),
        grid_spec=pltpu.PrefetchScalarGridSpec(
            num_scalar_prefetch=6,
            grid=(n // (TG * TS),),
            in_specs=[pl.BlockSpec((TG * TS, d), lambda i, *_: (i, 0)),
                      pl.BlockSpec((TOP_K, TG * TS), lambda i, *_: (0, i))],
            out_specs=pl.BlockSpec(memory_space=pl.ANY),
            scratch_shapes=[pltpu.VMEM((2 * TG, RS, d // 2), u32), pltpu.VMEM((BM, d // 2), u32),
                            pltpu.SemaphoreType.DMA((2 * TG,)), pltpu.SemaphoreType.DMA(())]),
        compiler_params=_params(("arbitrary",)),
        name="dispatch",
    )(toff, dst, c8s, tstart, tlen, n_used, h2, pos)


def _expert_body(be_ref, nxt_ref, nu_ref, x_ref, wgu_hbm, bgu_ref, wd_hbm, bd_ref, o_ref,
                 wgu_st, wd_st, wgu_bf, wd_bf, sems):
    del nu_ref
    b = pl.program_id(0)
    d = D_MODEL
    e = be_ref[b]

    def weight_copies(ex):
        return (pltpu.make_async_copy(wgu_hbm.at[ex], wgu_st, sems.at[0]),
                pltpu.make_async_copy(wd_hbm.at[ex], wd_st, sems.at[1]))

    @pl.when(b == 0)
    def _():
        for c in weight_copies(e):
            c.start()

    @pl.when(jnp.logical_or(b == 0, e != be_ref[jnp.maximum(b - 1, 0)]))
    def _():
        for c in weight_copies(e):
            c.wait()
        def cast_rows(c, carry):
            rows = pl.ds(pl.multiple_of(c * CAST_ROWS, CAST_ROWS), CAST_ROWS)
            wgu_bf[rows, :] = wgu_st[rows, :].astype(bf16)
            wd_bf[rows, :] = wd_st[rows, :].astype(bf16)
            return carry
        lax.fori_loop(0, d // CAST_ROWS, cast_rows, 0)
        nxt = nxt_ref[e]

        @pl.when(nxt >= 0)
        def _():
            for c in weight_copies(nxt):
                c.start()

    xb = jnp.concatenate(_unpack_halves(x_ref[...]), axis=1).astype(bf16)
    gu = jnp.dot(xb, wgu_bf[...], preferred_element_type=f32) + bgu_ref[0]
    glu = jnp.minimum(gu[:, :d], SWIGLU_LIMIT)
    lin = jnp.clip(gu[:, d:], -SWIGLU_LIMIT, SWIGLU_LIMIT)
    act = glu * _sigmoid(SWIGLU_ALPHA * glu) * (lin + 1.0)
    y = jnp.dot(act.astype(bf16), wd_bf[...], preferred_element_type=f32) + bd_ref[0]
    o_ref[...] = _pack_halves(y, exact=False)


def _experts(blk_e, nxt_e, n_used, xb, wgu, bgu, wd, bd):
    rows = pl.BlockSpec((BM, xb.shape[1]), lambda b, be, nx, nu: (b, 0))
    per_e = lambda a: pl.BlockSpec((1,) + a.shape[1:], lambda b, be, nx, nu: (be[b], 0, 0))
    hbm = pl.BlockSpec(memory_space=pl.ANY)
    return pl.pallas_call(
        _expert_body,
        out_shape=jax.ShapeDtypeStruct(xb.shape, xb.dtype),
        grid_spec=pltpu.PrefetchScalarGridSpec(
            num_scalar_prefetch=3,
            grid=(n_used[0],),
            in_specs=[rows, hbm, per_e(bgu), hbm, per_e(bd)],
            out_specs=rows,
            scratch_shapes=[pltpu.VMEM(wgu.shape[1:], f32), pltpu.VMEM(wd.shape[1:], f32),
                            pltpu.VMEM(wgu.shape[1:], bf16), pltpu.VMEM(wd.shape[1:], bf16),
                            pltpu.SemaphoreType.DMA((2,))]),
        input_output_aliases={3: 0},
        compiler_params=_params(("arbitrary",)),
        name="experts",
    )(blk_e, nxt_e, n_used, xb, wgu, bgu, wd, bd)


def _combine_body(toff_ref, dst_ref, c8_ref, yb_ref, pos_ref, gt_ref, x1_ref, mod_ref, g_ref,
                  o_ref, sbuf, sems):
    i = pl.program_id(0)
    base = lax.rem(i, 2) * TG

    def fetch(step, first_slot):
        for j in range(TG):
            def make(so, do, n, sl=first_slot + j):
                return pltpu.make_async_copy(yb_ref.at[pl.ds(do, n)], sbuf.at[sl, pl.ds(so, n)],
                                             sems.at[sl])
            _start_runs(step * TG + j, toff_ref, dst_ref, c8_ref, make)

    @pl.when(i == 0)
    def _():
        sbuf[...] = jnp.zeros_like(sbuf)
        fetch(i, base)

    @pl.when(i + 1 < pl.num_programs(0))
    def _():
        fetch(i + 1, TG - base)

    r = lax.broadcasted_iota(i32, (RS, TS), 0)
    pms, grows = [], []
    for j in range(TG):
        p = [pos_ref[k:k + 1, j * TS:(j + 1) * TS] for k in range(TOP_K)]
        gt = gt_ref[:, j * TS:(j + 1) * TS]
        gm = jnp.where(r == p[0], gt[0:1, :], jnp.where(r == p[1], gt[1:2, :],
             jnp.where(r == p[2], gt[2:3, :], jnp.where(r == p[3], gt[3:4, :], 0.0))))
        pms.append(jnp.where(gm != 0.0, 1.0, 0.0).astype(bf16))
        grows.append(jnp.sum(gm, axis=1, keepdims=True))

    for j in range(TG):
        n = _tile_rows(i * TG + j, toff_ref, c8_ref)
        pltpu.make_async_copy(yb_ref.at[pl.ds(0, n)], sbuf.at[base + j, pl.ds(0, n)],
                              sems.at[base + j]).wait()
    for j in range(TG):
        halves = [lax.dot_general(pms[j], (h * grows[j]).astype(bf16), (((0,), (0,)), ((), ())),
                                  preferred_element_type=f32)
                  for h in _unpack_halves(sbuf[base + j])]
        y = jnp.concatenate(halves, axis=1)
        rows = slice(j * TS, (j + 1) * TS)
        o_ref[rows, :] = x1_ref[rows, :] + mod_ref[0, 5:6, :] * _rms(y, g_ref[...])


def _combine(toff, dst, c8s, yb, pos, gt, x1, mod3, g, tiles_per_batch):
    n, d = x1.shape
    tok = pl.BlockSpec((TG * TS, d), lambda i, *_: (i, 0))
    lane = pl.BlockSpec((TOP_K, TG * TS), lambda i, *_: (0, i))
    steps_per_batch = tiles_per_batch // TG
    return pl.pallas_call(
        _combine_body,
        out_shape=jax.ShapeDtypeStruct((n, d), f32),
        grid_spec=pltpu.PrefetchScalarGridSpec(
            num_scalar_prefetch=3,
            grid=(n // (TG * TS),),
            in_specs=[pl.BlockSpec(memory_space=pl.ANY), lane, lane, tok,
                      pl.BlockSpec((1, 6, d), lambda i, *_: (i // steps_per_batch, 0, 0)),
                      pl.BlockSpec(g.shape, lambda i, *_: (0, 0))],
            out_specs=tok,
            scratch_shapes=[pltpu.VMEM((2 * TG, RS, d // 2), u32), pltpu.SemaphoreType.DMA((2 * TG,))]),
        compiler_params=_params(("arbitrary",)),
        name="combine",
    )(toff, dst, c8s, yb, pos, gt, x1, mod3, g)


def _layout_tables(tab, nt, p_rows):
    c8 = tab[:, :nt].T.astype(i32)
    toff = jnp.cumsum(c8, axis=1) - c8
    len8 = jnp.sum(c8, axis=0)
    seg = (len8 + BM - 1) // BM * BM
    gend = jnp.cumsum(seg)
    gstart = gend - seg
    dst = jnp.cumsum(c8, axis=0) - c8 + gstart[None, :]
    n_used = gend[-1] // BM
    blk = jnp.arange(p_rows // BM, dtype=i32)
    last = jnp.minimum(blk, n_used - 1)
    blk_e = jnp.sum((gend[None, :] <= (last * BM)[:, None]).astype(i32), axis=1)
    blk_e = jnp.minimum(blk_e, N_EXPERTS - 1)
    ids = jnp.arange(N_EXPERTS, dtype=i32)
    later = jnp.logical_and(ids[None, :] > ids[:, None], (seg > 0)[None, :])
    nxt_e = jnp.min(jnp.where(later, ids[None, :], N_EXPERTS), axis=1)
    nxt_e = jnp.where(nxt_e == N_EXPERTS, -1, nxt_e).astype(i32)
    return (toff.reshape(-1), dst.reshape(-1), c8.reshape(-1), gstart + len8, seg - len8,
            blk_e, nxt_e, n_used.reshape(1).astype(i32))


def kernel(x, c, w_ada, b_ada, g_pre_mix, g_post_mix, w_in, rel_bias, sgu_ln_g, sgu_ln_b,
           w_spatial, b_spatial, w_branch_a, w_branch_b, w_gate, b_gate, w_out,
           g_pre_ffn, g_post_ffn, w_router, b_router, w_gate_up, b_gate_up, w_down, b_down):
    b, s, d = x.shape
    assert d == D_MODEL and s % max(TM, TMP, TMX) == 0 and s % (TG * TS) == 0
    n = b * s
    nt = n // TS
    ntp = -(-nt // LANES) * LANES
    p_rows = -(-(n * TOP_K + nt * N_EXPERTS * (ROW_ALIGN - 1) + N_EXPERTS * (BM - 1)) // BM) * BM
    depth = w_ada.shape[0]
    c8 = jnp.pad(c, ((0, 8 - b), (0, 0)))
    row = lambda a: a.reshape(1, -1)

    for l in range(depth):
        mod = _ada(c8, w_ada[l], row(b_ada[l]))[:b]
        mod3 = mod.reshape(b, 6, d)

        q, k, v, u, vv = _proj(x, mod3, row(g_pre_mix[l]), w_in[l].astype(bf16),
                               row(sgu_ln_g[l]), row(sgu_ln_b[l]))
        ya = _attn(q, k, v, _attn_bias_vec(rel_bias[l]))
        ws2 = w_spatial[l].astype(bf16).reshape(-1, 2 * SGU_BLOCK, SGU_BLOCK)
        bsf = jnp.repeat(b_spatial[l].T, SGU_WIDTH // b_spatial.shape[1], axis=1)
        x1, h2, pos, gt, tab = _mix(x, mod3, row(g_pre_mix[l]), row(g_post_mix[l]), u, vv, ya, ws2, bsf,
                                   w_branch_a[l].astype(bf16), w_branch_b[l].astype(bf16),
                                   w_gate[l].astype(bf16), row(b_gate[l]), w_out[l].astype(bf16),
                                   row(g_pre_ffn[l]), w_router[l].T.astype(bf16),
                                   b_router[l].reshape(-1, 1), ntp)
        x1f = x1.reshape(n, d)
        h2 = h2.reshape(n, d)
        toff, dst, c8s, tstart, tlen, blk_e, nxt_e, n_used = _layout_tables(tab, nt, p_rows)
        xb = _dispatch(toff, dst, c8s, tstart, tlen, n_used, h2, pos, p_rows)
        yb = _experts(blk_e, nxt_e, n_used, xb, w_gate_up[l], b_gate_up[l][:, None, :],
                      w_down[l], b_down[l][:, None, :])
        x = _combine(toff, dst, c8s, yb, pos, gt, x1f, mod3, row(g_post_ffn[l]), s // TS).reshape(b, s, d)
    return x
```

```python
import jax
import jax.numpy as jnp
from jax import lax
from jax.experimental import pallas as pl
from jax.experimental.pallas import tpu as pltpu

bf16 = jnp.bfloat16
f32 = jnp.float32
i32 = jnp.int32
u32 = jnp.uint32

D_MODEL = 1024
CHUNK = 64
N_LEFT = 8
ATT_HEADS = 8
HEAD_DIM = 64
ATT_WIDTH = 512
MAX_REL = 128
SGU_BLOCK = 128
SGU_WIDTH = 512
N_EXPERTS = 32
TOP_K = 4
SWIGLU_LIMIT = 7.0
SWIGLU_ALPHA = 1.702
EPS = 1e-6
NEG = -1e30
LOG2E = 1.4426950408889634

LANES = 128
ROW_ALIGN = 8
TM = 512
TMP = 1024
TMX = 1024
QCH = 4
QG = QCH * CHUNK
KBAND = (N_LEFT + QCH) * CHUNK
BIAS_PERIOD = 1024
TS = 256
TG = 4
RS = TS * TOP_K + N_EXPERTS * ROW_ALIGN
BM = 512
CAST_ROWS = 64
VMEM_LIMIT = 56 * 2**20


def _params(sem):
    return pltpu.CompilerParams(dimension_semantics=sem, vmem_limit_bytes=VMEM_LIMIT)


def _adaln(x, g, sc, sh):
    ms = jnp.mean(x * x, axis=-1, keepdims=True)
    return (x * lax.rsqrt(ms + EPS) * g) * (1.0 + sc) + sh


def _rms(x, g):
    ms = jnp.mean(x * x, axis=-1, keepdims=True)
    return x * lax.rsqrt(ms + EPS) * g


def _sigmoid(x):
    return 0.5 * jnp.tanh(0.5 * x) + 0.5


def _ada_body(c_ref, w_ref, b_ref, o_ref):
    c = c_ref[...]
    ca = c * _sigmoid(c)
    o_ref[...] = jnp.dot(ca.astype(bf16), w_ref[...].astype(bf16),
                         preferred_element_type=f32) + b_ref[...]


def _ada(c8, w, b):
    d = w.shape[0]
    n = w.shape[1] // d
    return pl.pallas_call(
        _ada_body,
        out_shape=jax.ShapeDtypeStruct((8, n * d), f32),
        grid=(n,),
        in_specs=[pl.BlockSpec((8, d), lambda j: (0, 0)),
                  pl.BlockSpec((d, d), lambda j: (0, j)),
                  pl.BlockSpec((1, d), lambda j: (0, j))],
        out_specs=pl.BlockSpec((8, d), lambda j: (0, j)),
        compiler_params=_params(("arbitrary",)),
        name="ada",
    )(c8, w, b)


def _proj_body(x_ref, mod_ref, g_ref, w_ref, lng_ref, lnb_ref,
               q_ref, k_ref, v_ref, u_ref, vv_ref):
    h = _adaln(x_ref[0], g_ref[...], mod_ref[0, 1:2, :], mod_ref[0, 0:1, :]).astype(bf16)
    aw = ATT_WIDTH
    z = jnp.dot(h, w_ref[:, 3 * aw:], preferred_element_type=f32)
    p = jnp.dot(h, w_ref[:, :3 * aw], preferred_element_type=f32)
    q_ref[0] = (p[:, 0:aw] * (HEAD_DIM ** -0.5 * LOG2E)).astype(bf16)
    k_ref[0] = p[:, aw:2 * aw].astype(bf16)
    v_ref[0] = p[:, 2 * aw:3 * aw].astype(bf16)
    zg = 0.5 * z * (1.0 + lax.erf(z * (2.0 ** -0.5)))
    u_ref[0] = zg[:, :SGU_WIDTH].astype(bf16)
    vv = zg[:, SGU_WIDTH:]
    mu = jnp.mean(vv, axis=-1, keepdims=True)
    var = jnp.mean(jnp.square(vv - mu), axis=-1, keepdims=True)
    vn = (vv - mu) * lax.rsqrt(var + EPS) * lng_ref[...] + lnb_ref[...]
    vv_ref[0] = vn.astype(bf16)


def _proj(x, mod3, g, w_in, lng, lnb):
    b, s, d = x.shape
    tok = lambda w: pl.BlockSpec((1, TMP, w), lambda bi, i: (bi, i, 0))
    full = lambda a: pl.BlockSpec(a.shape, lambda bi, i: (0,) * a.ndim)
    o512 = jax.ShapeDtypeStruct((b, s, ATT_WIDTH), bf16)
    return pl.pallas_call(
        _proj_body,
        out_shape=(o512,) * 5,
        grid=(b, s // TMP),
        in_specs=[tok(d), pl.BlockSpec((1, 6, d), lambda bi, i: (bi, 0, 0)),
                  full(g), full(w_in), full(lng), full(lnb)],
        out_specs=(tok(ATT_WIDTH),) * 5,
        compiler_params=_params(("parallel", "arbitrary")),
        name="proj",
    )(x, mod3, g, w_in, lng, lnb)


def _attn_body(q_ref, kp_ref, kc_ref, vp_ref, vc_ref, bvec_ref, o_ref, bias_ref):
    first = pl.program_id(1) == 0
    nhp = ATT_HEADS // 2

    @pl.when(first)
    def _():
        i = lax.broadcasted_iota(i32, (QG, KBAND), 0)
        j = lax.broadcasted_iota(i32, (QG, KBAND), 1)
        jb = j - (i // CHUNK) * CHUNK
        in_band = jnp.logical_and(jb >= 0, jb < CHUNK * (N_LEFT + 1))
        for h in range(ATT_HEADS):
            rows = jnp.broadcast_to(bvec_ref[h:h + 1, :], (QG, BIAS_PERIOD))
            toep = pltpu.roll(rows, 0, 1, stride=1, stride_axis=0)[:, :KBAND]
            bias_ref[h // 2, (h % 2) * QG:(h % 2 + 1) * QG, :] = jnp.where(in_band, toep, NEG)

    lo = lax.broadcasted_iota(i32, (QG, LANES), 1) < HEAD_DIM
    has_prev = jnp.logical_not(first)
    for p in range(TM // QG):
        r0 = p * QG
        n_cur = KBAND - (TM - r0)
        ind = jnp.concatenate([jnp.where(has_prev, 1.0, 0.0) * jnp.ones((TM - r0, LANES), f32),
                               jnp.ones((n_cur, LANES), f32)], axis=0).astype(bf16)
        for hp in range(nhp):
            c0 = hp * LANES
            qp = q_ref[0, r0:r0 + QG, c0:c0 + LANES]
            zero = jnp.zeros_like(qp)
            q2 = jnp.concatenate([jnp.where(lo, qp, zero), jnp.where(lo, zero, qp)], axis=0)
            kprev = kp_ref[0, r0:TM, c0:c0 + LANES]
            vprev = vp_ref[0, r0:TM, c0:c0 + LANES]
            kprev = jnp.where(has_prev, kprev, jnp.zeros_like(kprev))
            vprev = jnp.where(has_prev, vprev, jnp.zeros_like(vprev))
            kb = jnp.concatenate([kprev, kc_ref[0, 0:n_cur, c0:c0 + LANES]], axis=0)
            vb = jnp.concatenate([vprev, vc_ref[0, 0:n_cur, c0:c0 + LANES]], axis=0)
            s = lax.dot_general(q2, kb, (((1,), (1,)), ((), ())), preferred_element_type=f32)
            sb = (s + bias_ref[hp]).astype(bf16)
            m = jnp.max(sb, axis=-1, keepdims=True)
            e = jnp.exp2(sb - m)
            o2 = jnp.dot(e, jnp.concatenate([vb, ind], axis=1),
                         preferred_element_type=f32)
            on = o2[:, :LANES] / o2[:, LANES:]
            o = jnp.where(lo, on[:QG], on[QG:])
            o_ref[0, r0:r0 + QG, c0:c0 + LANES] = o.astype(bf16)


def _attn(q, k, v, bvec):
    b, s, w = q.shape
    cur = pl.BlockSpec((1, TM, w), lambda bi, i: (bi, i, 0))
    prev = pl.BlockSpec((1, TM, w), lambda bi, i: (bi, jnp.maximum(i - 1, 0), 0))
    return pl.pallas_call(
        _attn_body,
        out_shape=jax.ShapeDtypeStruct((b, s, w), bf16),
        grid=(b, s // TM),
        in_specs=[cur, prev, cur, prev, cur,
                  pl.BlockSpec(bvec.shape, lambda bi, i: (0, 0))],
        out_specs=cur,
        scratch_shapes=[pltpu.VMEM((ATT_HEADS // 2, 2 * QG, KBAND), f32)],
        compiler_params=_params(("parallel", "arbitrary")),
        name="attn",
    )(q, k, k, v, v, bvec)


def _attn_bias_vec(rel_bias):
    h = rel_bias.shape[0]
    n_far = N_LEFT * CHUNK - MAX_REL
    assert BIAS_PERIOD >= QG + KBAND - 1 and n_far >= 0
    far = jnp.broadcast_to(rel_bias[:, 2 * MAX_REL:], (h, BIAS_PERIOD))
    near = rel_bias[:, :0:-1]
    v = jnp.concatenate([far[:, :n_far], near, far[:, n_far + 2 * MAX_REL:]], axis=1)
    return v.astype(f32) * LOG2E


def _route(hb, wr_ref, br_ref):
    lg = lax.dot_general(wr_ref[...], hb, (((1,), (1,)), ((), ())),
                         preferred_element_type=f32) + br_ref[...]
    e_iota = lax.broadcasted_iota(i32, lg.shape, 0)
    vals, idxs = [], []
    hits = jnp.zeros(lg.shape, f32)
    for _ in range(TOP_K):
        m = jnp.max(lg, axis=0, keepdims=True)
        idx = jnp.min(jnp.where(lg == m, e_iota, N_EXPERTS), axis=0, keepdims=True)
        hit = e_iota == idx
        hits = hits + jnp.where(hit, 1.0, 0.0)
        lg = jnp.where(hit, -jnp.inf, lg)
        vals.append(m)
        idxs.append(idx)
    ex = [jnp.exp(v - vals[0]) for v in vals]
    den = ex[0] + ex[1] + ex[2] + ex[3]
    return (jnp.concatenate(idxs, axis=0), jnp.concatenate([e / den for e in ex], axis=0), hits)


def _mix_body(x_ref, mod_ref, gpre_ref, gpost_ref, u_ref, vv_ref, ya_ref, ws_ref, bs_ref,
              wa_ref, wb_ref, wg_ref, bg_ref, wo_ref, gffn_ref, wr_ref, br_ref,
              o_ref, h2_ref, pos_ref, gt_ref, tab_ref, ybuf):
    d = D_MODEL
    x = x_ref[0]
    h = _adaln(x, gpre_ref[...], mod_ref[0, 1:2, :], mod_ref[0, 0:1, :]).astype(bf16)
    gs = _sigmoid(jnp.dot(h, wg_ref[...], preferred_element_type=f32) + bg_ref[...])

    blk = SGU_BLOCK
    row = lax.broadcasted_iota(i32, (2 * blk, blk), 0)
    colv = lax.broadcasted_iota(i32, (2 * blk, blk), 1)
    causal = colv <= jnp.bitwise_and(row, blk - 1)
    lo = lax.broadcasted_iota(i32, (blk, LANES), 1) < (LANES // 2)
    for gp in range(SGU_WIDTH // LANES):
        c0 = gp * LANES
        w2 = ws_ref[gp]
        w2 = jnp.where(causal, w2, jnp.zeros_like(w2))
        for bi in range(TMX // blk):
            r0 = bi * blk
            s2 = jnp.dot(w2, vv_ref[0, r0:r0 + blk, c0:c0 + LANES], preferred_element_type=f32)
            s = jnp.where(lo, s2[:blk], s2[blk:]) + bs_ref[:, c0:c0 + LANES]
            yb = u_ref[0, r0:r0 + blk, c0:c0 + LANES].astype(f32) * s
            ybuf[r0:r0 + blk, c0:c0 + LANES] = yb.astype(bf16)

    a = jnp.dot(ya_ref[0], wa_ref[...], preferred_element_type=f32)
    bb = jnp.dot(ybuf[...], wb_ref[...], preferred_element_type=f32)
    merged = gs[:, :d] * a + gs[:, d:] * bb
    y = jnp.dot(merged.astype(bf16), wo_ref[...], preferred_element_type=f32)
    x1 = x + mod_ref[0, 2:3, :] * _rms(y, gpost_ref[...])
    o_ref[0] = x1

    hb = _adaln(x1, gffn_ref[...], mod_ref[0, 4:5, :], mod_ref[0, 3:4, :]).astype(bf16)
    h2_ref[0] = hb
    ids, gates, hits = _route(hb, wr_ref, br_ref)
    pos_ref[...] = jnp.concatenate(
        [jnp.concatenate(_sorted_positions(ids[:, j * TS:(j + 1) * TS]), axis=0)
         for j in range(TMX // TS)], axis=1)
    gt_ref[...] = gates

    step = pl.program_id(0) * pl.num_programs(1) + pl.program_id(1)

    @pl.when(step == 0)
    def _():
        tab_ref[...] = jnp.zeros_like(tab_ref)

    lane = lax.broadcasted_iota(i32, tab_ref.shape, 1)
    acc = tab_ref[...]
    for j in range(TMX // TS):
        cnt = jnp.sum(hits[:, j * TS:(j + 1) * TS], axis=1, keepdims=True)
        c8 = jnp.floor((cnt + (ROW_ALIGN - 1.0)) * (1.0 / ROW_ALIGN)) * ROW_ALIGN
        acc = acc + jnp.where(lane == step * (TMX // TS) + j, c8, 0.0)
    tab_ref[...] = acc


def _mix(x, mod3, gpre, gpost, u, vv, ya, ws2, bsf, wa, wb, wg, bg, wo, gffn, wrt, br, ntp):
    b, s, d = x.shape
    n = b * s
    tok = lambda w: pl.BlockSpec((1, TMX, w), lambda bi, i: (bi, i, 0))
    full = lambda a: pl.BlockSpec(a.shape, lambda bi, i: (0,) * a.ndim)
    lane = pl.BlockSpec((TOP_K, TMX), lambda bi, i: (0, bi * (s // TMX) + i))
    return pl.pallas_call(
        _mix_body,
        out_shape=(jax.ShapeDtypeStruct((b, s, d), f32),
                   jax.ShapeDtypeStruct((b, s, d), bf16),
                   jax.ShapeDtypeStruct((TOP_K, n), i32),
                   jax.ShapeDtypeStruct((TOP_K, n), f32),
                   jax.ShapeDtypeStruct((N_EXPERTS, ntp), f32)),
        grid=(b, s // TMX),
        in_specs=[tok(d), pl.BlockSpec((1, 6, d), lambda bi, i: (bi, 0, 0)),
                  full(gpre), full(gpost), tok(SGU_WIDTH), tok(SGU_WIDTH), tok(ATT_WIDTH),
                  full(ws2), full(bsf), full(wa), full(wb), full(wg), full(bg), full(wo),
                  full(gffn), full(wrt), full(br)],
        out_specs=(tok(d), tok(d), lane, lane,
                   pl.BlockSpec((N_EXPERTS, ntp), lambda bi, i: (0, 0))),
        scratch_shapes=[pltpu.VMEM((TMX, SGU_WIDTH), bf16)],
        compiler_params=_params(("arbitrary", "arbitrary")),
        name="mix",
    )(x, mod3, gpre, gpost, u, vv, ya, ws2, bsf, wa, wb, wg, bg, wo, gffn, wrt, br)


def _sorted_positions(ti):
    ts = ti.shape[1]
    e_iota = lax.broadcasted_iota(i32, (N_EXPERTS, ts), 0)
    upper = (lax.broadcasted_iota(i32, (ts, ts), 0) < lax.broadcasted_iota(i32, (ts, ts), 1))
    upper = jnp.where(upper, 1.0, 0.0).astype(bf16)
    hits, prefs, cnts = [], [], []
    for k in range(TOP_K):
        hit = ti[k:k + 1, :] == e_iota
        hf = jnp.where(hit, 1.0, 0.0)
        prefs.append(jnp.dot(hf.astype(bf16), upper, preferred_element_type=f32))
        cnts.append(jnp.sum(hf, axis=1, keepdims=True))
        hits.append(hit)
    total = cnts[0] + cnts[1] + cnts[2] + cnts[3]
    c8 = jnp.floor((total + (ROW_ALIGN - 1.0)) * (1.0 / ROW_ALIGN)) * ROW_ALIGN
    lower = (lax.broadcasted_iota(i32, (N_EXPERTS, N_EXPERTS), 1)
             < lax.broadcasted_iota(i32, (N_EXPERTS, N_EXPERTS), 0))
    lower = jnp.where(lower, 1.0, 0.0).astype(bf16)
    c8b = jnp.broadcast_to(c8, (N_EXPERTS, LANES)).astype(bf16)
    start = jnp.dot(lower, c8b, preferred_element_type=f32)[:, 0:1]
    pos = []
    for k in range(TOP_K):
        pe = start + prefs[k]
        pos.append(jnp.sum(jnp.where(hits[k], pe, 0.0), axis=0, keepdims=True).astype(i32))
        start = start + cnts[k]
    return pos


def _pack_halves(x, exact):
    half = x.shape[1] // 2
    lo = lax.bitcast_convert_type(x[:, :half], u32)
    hi = lax.bitcast_convert_type(x[:, half:], u32)
    if not exact:
        hi = jnp.bitwise_and(hi, jnp.uint32(0xFFFF0000))
    return jnp.bitwise_or(lax.shift_right_logical(lo, jnp.uint32(16)), hi)


def _unpack_halves(w):
    lo = lax.bitcast_convert_type(lax.shift_left(w, jnp.uint32(16)), f32)
    hi = lax.bitcast_convert_type(jnp.bitwise_and(w, jnp.uint32(0xFFFF0000)), f32)
    return lo, hi


def _start_runs(i, toff_ref, dst_ref, c8_ref, make):
    unroll = 4

    def some(q, carry):
        for k in range(unroll):
            j = i * N_EXPERTS + q * unroll + k
            n = pl.multiple_of(c8_ref[j], ROW_ALIGN)
            so = pl.multiple_of(toff_ref[j], ROW_ALIGN)
            do = pl.multiple_of(dst_ref[j], ROW_ALIGN)

            @pl.when(n > 0)
            def _():
                make(so, do, n).start(priority=k % 2)
        return carry
    lax.fori_loop(0, N_EXPERTS // unroll, some, 0)


def _tile_rows(i, toff_ref, c8_ref):
    j = i * N_EXPERTS + (N_EXPERTS - 1)
    return pl.multiple_of(toff_ref[j] + c8_ref[j], ROW_ALIGN)


def _dispatch_body(toff_ref, dst_ref, c8_ref, tstart_ref, tlen_ref, nu_ref,
                   h2_ref, pos_ref, xb_ref, sbuf, zbuf, sems, sem):
    i = pl.program_id(0)
    base = lax.rem(i, 2) * TG
    r = lax.broadcasted_iota(i32, (RS, TS), 0)
    for j in range(TG):
        p = [pos_ref[k:k + 1, j * TS:(j + 1) * TS] for k in range(TOP_K)]
        pm = jnp.where(r == p[0], 1.0, jnp.where(r == p[1], 1.0,
             jnp.where(r == p[2], 1.0, jnp.where(r == p[3], 1.0, 0.0))))
        srt = jnp.dot(pm.astype(bf16), h2_ref[j * TS:(j + 1) * TS, :], preferred_element_type=f32)
        sbuf[base + j] = _pack_halves(srt, exact=True)

    for j in range(TG):
        def make(so, do, n, sl=base + j):
            return pltpu.make_async_copy(sbuf.at[sl, pl.ds(so, n)], xb_ref.at[pl.ds(do, n)],
                                         sems.at[sl])
        _start_runs(i * TG + j, toff_ref, dst_ref, c8_ref, make)

    def wait_tile(t, sl):
        n = _tile_rows(t, toff_ref, c8_ref)
        pltpu.make_async_copy(sbuf.at[sl, pl.ds(0, n)], xb_ref.at[pl.ds(0, n)], sems.at[sl]).wait()

    @pl.when(i > 0)
    def _():
        for j in range(TG):
            wait_tile((i - 1) * TG + j, TG - base + j)

    @pl.when(i == pl.num_programs(0) - 1)
    def _():
        for j in range(TG):
            wait_tile(i * TG + j, base + j)
        zbuf[...] = jnp.zeros_like(zbuf)

        def fill(action):
            def tail(e, carry):
                n = pl.multiple_of(tlen_ref[e], ROW_ALIGN)
                do = pl.multiple_of(tstart_ref[e], ROW_ALIGN)

                @pl.when(n > 0)
                def _():
                    action(pltpu.make_async_copy(zbuf.at[pl.ds(0, n)], xb_ref.at[pl.ds(do, n)], sem))
                return carry
            lax.fori_loop(0, N_EXPERTS, tail, 0)

            def unused(b, carry):
                do = pl.multiple_of(b * BM, BM)
                action(pltpu.make_async_copy(zbuf, xb_ref.at[pl.ds(do, BM)], sem))
                return carry
            lax.fori_loop(nu_ref[0], xb_ref.shape[0] // BM, unused, 0)
        fill(lambda c: c.start())
        fill(lambda c: c.wait())


def _dispatch(toff, dst, c8s, tstart, tlen, n_used, h2, pos, p_rows):
    n, d = h2.shape
    return pl.pallas_call(
        _dispatch_body,
        out_shape=jax.ShapeDtypeStruct((p_rows, d // 2), u32),
        grid_spec=pltpu.PrefetchScalarGridSpec(
            num_scalar_prefetch=6,
            grid=(n // (TG * TS),),
            in_specs=[pl.BlockSpec((TG * TS, d), lambda i, *_: (i, 0)),
                      pl.BlockSpec((TOP_K, TG * TS), lambda i, *_: (0, i))],
            out_specs=pl.BlockSpec(memory_space=pl.ANY),
            scratch_shapes=[pltpu.VMEM((2 * TG, RS, d // 2), u32), pltpu.VMEM((BM, d // 2), u32),
                            pltpu.SemaphoreType.DMA((2 * TG,)), pltpu.SemaphoreType.DMA(())]),
        compiler_params=_params(("arbitrary",)),
        name="dispatch",
    )(toff, dst, c8s, tstart, tlen, n_used, h2, pos)


def _expert_body(be_ref, nxt_ref, nu_ref, x_ref, wgu_hbm, bgu_ref, wd_hbm, bd_ref, o_ref,
                 wgu_st, wd_st, wgu_bf, wd_bf, sems):
    del nu_ref
    b = pl.program_id(0)
    d = D_MODEL
    e = be_ref[b]

    def weight_copies(ex):
        return (pltpu.make_async_copy(wgu_hbm.at[ex], wgu_st, sems.at[0]),
                pltpu.make_async_copy(wd_hbm.at[ex], wd_st, sems.at[1]))

    @pl.when(b == 0)
    def _():
        for c in weight_copies(e):
            c.start()

    @pl.when(jnp.logical_or(b == 0, e != be_ref[jnp.maximum(b - 1, 0)]))
    def _():
        for c in weight_copies(e):
            c.wait()
        def cast_rows(c, carry):
            rows = pl.ds(pl.multiple_of(c * CAST_ROWS, CAST_ROWS), CAST_ROWS)
            wgu_bf[rows, :] = wgu_st[rows, :].astype(bf16)
            wd_bf[rows, :] = wd_st[rows, :].astype(bf16)
            return carry
        lax.fori_loop(0, d // CAST_ROWS, cast_rows, 0)
        nxt = nxt_ref[e]

        @pl.when(nxt >= 0)
        def _():
            for c in weight_copies(nxt):
                c.start()

    xb = jnp.concatenate(_unpack_halves(x_ref[...]), axis=1).astype(bf16)
    gu = jnp.dot(xb, wgu_bf[...], preferred_element_type=f32) + bgu_ref[0]
    glu = jnp.minimum(gu[:, :d], SWIGLU_LIMIT)
    lin = jnp.clip(gu[:, d:], -SWIGLU_LIMIT, SWIGLU_LIMIT)
    act = glu * _sigmoid(SWIGLU_ALPHA * glu) * (lin + 1.0)
    y = jnp.dot(act.astype(bf16), wd_bf[...], preferred_element_type=f32) + bd_ref[0]
    o_ref[...] = _pack_halves(y, exact=False)


def _experts(blk_e, nxt_e, n_used, xb, wgu, bgu, wd, bd):
    rows = pl.BlockSpec((BM, xb.shape[1]), lambda b, be, nx, nu: (b, 0))
    per_e = lambda a: pl.BlockSpec((1,) + a.shape[1:], lambda b, be, nx, nu: (be[b], 0, 0))
    hbm = pl.BlockSpec(memory_space=pl.ANY)
    return pl.pallas_call(
        _expert_body,
        out_shape=jax.ShapeDtypeStruct(xb.shape, xb.dtype),
        grid_spec=pltpu.PrefetchScalarGridSpec(
            num_scalar_prefetch=3,
            grid=(n_used[0],),
            in_specs=[rows, hbm, per_e(bgu), hbm, per_e(bd)],
            out_specs=rows,
            scratch_shapes=[pltpu.VMEM(wgu.shape[1:], f32), pltpu.VMEM(wd.shape[1:], f32),
                            pltpu.VMEM(wgu.shape[1:], bf16), pltpu.VMEM(wd.shape[1:], bf16),
                            pltpu.SemaphoreType.DMA((2,))]),
        input_output_aliases={3: 0},
        compiler_params=_params(("arbitrary",)),
        name="experts",
    )(blk_e, nxt_e, n_used, xb, wgu, bgu, wd, bd)


def _combine_body(toff_ref, dst_ref, c8_ref, yb_ref, pos_ref, gt_ref, x1_ref, mod_ref, g_ref,
                  o_ref, sbuf, sems):
    i = pl.program_id(0)
    base = lax.rem(i, 2) * TG

    def fetch(step, first_slot):
        for j in range(TG):
            def make(so, do, n, sl=first_slot + j):
                return pltpu.make_async_copy(yb_ref.at[pl.ds(do, n)], sbuf.at[sl, pl.ds(so, n)],
                                             sems.at[sl])
            _start_runs(step * TG + j, toff_ref, dst_ref, c8_ref, make)

    @pl.when(i == 0)
    def _():
        sbuf[...] = jnp.zeros_like(sbuf)
        fetch(i, base)

    @pl.when(i + 1 < pl.num_programs(0))
    def _():
        fetch(i + 1, TG - base)

    r = lax.broadcasted_iota(i32, (RS, TS), 0)
    pms, grows = [], []
    for j in range(TG):
        p = [pos_ref[k:k + 1, j * TS:(j + 1) * TS] for k in range(TOP_K)]
        gt = gt_ref[:, j * TS:(j + 1) * TS]
        gm = jnp.where(r == p[0], gt[0:1, :], jnp.where(r == p[1], gt[1:2, :],
             jnp.where(r == p[2], gt[2:3, :], jnp.where(r == p[3], gt[3:4, :], 0.0))))
        pms.append(jnp.where(gm != 0.0, 1.0, 0.0).astype(bf16))
        grows.append(jnp.sum(gm, axis=1, keepdims=True))

    for j in range(TG):
        n = _tile_rows(i * TG + j, toff_ref, c8_ref)
        pltpu.make_async_copy(yb_ref.at[pl.ds(0, n)], sbuf.at[base + j, pl.ds(0, n)],
                              sems.at[base + j]).wait()
    for j in range(TG):
        halves = [lax.dot_general(pms[j], (h * grows[j]).astype(bf16), (((0,), (0,)), ((), ())),
                                  preferred_element_type=f32)
                  for h in _unpack_halves(sbuf[base + j])]
        y = jnp.concatenate(halves, axis=1)
        rows = slice(j * TS, (j + 1) * TS)
        o_ref[rows, :] = x1_ref[rows, :] + mod_ref[0, 5:6, :] * _rms(y, g_ref[...])


def _combine(toff, dst, c8s, yb, pos, gt, x1, mod3, g, tiles_per_batch):
    n, d = x1.shape
    tok = pl.BlockSpec((TG * TS, d), lambda i, *_: (i, 0))
    lane = pl.BlockSpec((TOP_K, TG * TS), lambda i, *_: (0, i))
    steps_per_batch = tiles_per_batch // TG
    return pl.pallas_call(
        _combine_body,
        out_shape=jax.ShapeDtypeStruct((n, d), f32),
        grid_spec=pltpu.PrefetchScalarGridSpec(
            num_scalar_prefetch=3,
            grid=(n // (TG * TS),),
            in_specs=[pl.BlockSpec(memory_space=pl.ANY), lane, lane, tok,
                      pl.BlockSpec((1, 6, d), lambda i, *_: (i // steps_per_batch, 0, 0)),
                      pl.BlockSpec(g.shape, lambda i, *_: (0, 0))],
            out_specs=tok,
            scratch_shapes=[pltpu.VMEM((2 * TG, RS, d // 2), u32), pltpu.SemaphoreType.DMA((2 * TG,))]),
        compiler_params=_params(("arbitrary",)),
        name="combine",
    )(toff, dst, c8s, yb, pos, gt, x1, mod3, g)


def _layout_tables(tab, nt, p_rows):
    c8 = tab[:, :nt].T.astype(i32)
    toff = jnp.cumsum(c8, axis=1) - c8
    len8 = jnp.sum(c8, axis=0)
    seg = (len8 + BM - 1) // BM * BM
    gend = jnp.cumsum(seg)
    gstart = gend - seg
    dst = jnp.cumsum(c8, axis=0) - c8 + gstart[None, :]
    n_used = gend[-1] // BM
    blk = jnp.arange(p_rows // BM, dtype=i32)
    last = jnp.minimum(blk, n_used - 1)
    blk_e = jnp.sum((gend[None, :] <= (last * BM)[:, None]).astype(i32), axis=1)
    blk_e = jnp.minimum(blk_e, N_EXPERTS - 1)
    ids = jnp.arange(N_EXPERTS, dtype=i32)
    later = jnp.logical_and(ids[None, :] > ids[:, None], (seg > 0)[None, :])
    nxt_e = jnp.min(jnp.where(later, ids[None, :], N_EXPERTS), axis=1)
    nxt_e = jnp.where(nxt_e == N_EXPERTS, -1, nxt_e).astype(i32)
    return (toff.reshape(-1), dst.reshape(-1), c8.reshape(-1), gstart + len8, seg - len8,
            blk_e, nxt_e, n_used.reshape(1).astype(i32))


def kernel(x, c, w_ada, b_ada, g_pre_mix, g_post_mix, w_in, rel_bias, sgu_ln_g, sgu_ln_b,
           w_spatial, b_spatial, w_branch_a, w_branch_b, w_gate, b_gate, w_out,
           g_pre_ffn, g_post_ffn, w_router, b_router, w_gate_up, b_gate_up, w_down, b_down):
    b, s, d = x.shape
    assert d == D_MODEL and s % max(TM, TMP, TMX) == 0 and s % (TG * TS) == 0
    n = b * s
    nt = n // TS
    ntp = -(-nt // LANES) * LANES
    p_rows = -(-(n * TOP_K + nt * N_EXPERTS * (ROW_ALIGN - 1) + N_EXPERTS * (BM - 1)) // BM) * BM
    depth = w_ada.shape[0]
    c8 = jnp.pad(c, ((0, 8 - b), (0, 0)))
    row = lambda a: a.reshape(1, -1)

    for l in range(depth):
        mod = _ada(c8, w_ada[l], row(b_ada[l]))[:b]
        mod3 = mod.reshape(b, 6, d)

        q, k, v, u, vv = _proj(x, mod3, row(g_pre_mix[l]), w_in[l].astype(bf16),
                               row(sgu_ln_g[l]), row(sgu_ln_b[l]))
        ya = _attn(q, k, v, _attn_bias_vec(rel_bias[l]))
        ws2 = w_spatial[l].astype(bf16).reshape(-1, 2 * SGU_BLOCK, SGU_BLOCK)
        bsf = jnp.repeat(b_spatial[l].T, SGU_WIDTH // b_spatial.shape[1], axis=1)
        x1, h2, pos, gt, tab = _mix(x, mod3, row(g_pre_mix[l]), row(g_post_mix[l]), u, vv, ya, ws2, bsf,
                                   w_branch_a[l].astype(bf16), w_branch_b[l].astype(bf16),
                                   w_gate[l].astype(bf16), row(b_gate[l]), w_out[l].astype(bf16),
                                   row(g_pre_ffn[l]), w_router[l].T.astype(bf16),
                                   b_router[l].reshape(-1, 1), ntp)
        x1f = x1.reshape(n, d)
        h2 = h2.reshape(n, d)
        toff, dst, c8s, tstart, tlen, blk_e, nxt_e, n_used = _layout_tables(tab, nt, p_rows)
        xb = _dispatch(toff, dst, c8s, tstart, tlen, n_used, h2, pos, p_rows)
        yb = _experts(blk_e, nxt_e, n_used, xb, w_gate_up[l], b_gate_up[l][:, None, :],
                      w_down[l], b_down[l][:, None, :])
        x = _combine(toff, dst, c8s, yb, pos, gt, x1f, mod3, row(g_post_ffn[l]), s // TS).reshape(b, s, d)
    return x
```

```python
import jax
import jax.numpy as jnp
from jax import lax
from jax.experimental import pallas as pl
from jax.experimental.pallas import tpu as pltpu

bf16 = jnp.bfloat16
f32 = jnp.float32
i32 = jnp.int32
u32 = jnp.uint32

D_MODEL = 1024
CHUNK = 64
N_LEFT = 8
ATT_HEADS = 8
HEAD_DIM = 64
ATT_WIDTH = 512
MAX_REL = 128
SGU_BLOCK = 128
SGU_WIDTH = 512
N_EXPERTS = 32
TOP_K = 4
SWIGLU_LIMIT = 7.0
SWIGLU_ALPHA = 1.702
EPS = 1e-6
NEG = -1e30
LOG2E = 1.4426950408889634

LANES = 128
ROW_ALIGN = 8
TM = 512
TMP = 1024
TMX = 1024
QCH = 4
QG = QCH * CHUNK
KBAND = (N_LEFT + QCH) * CHUNK
BIAS_PERIOD = 1024
TS = 256
TG = 4
RS = TS * TOP_K + N_EXPERTS * ROW_ALIGN
BM = 512
BM_STEPS = (BM // 2, BM)
CAST_ROWS = 64
VMEM_LIMIT = 56 * 2**20


def _params(sem):
    return pltpu.CompilerParams(dimension_semantics=sem, vmem_limit_bytes=VMEM_LIMIT)


def _adaln(x, g, sc, sh):
    ms = jnp.mean(x * x, axis=-1, keepdims=True)
    return (x * lax.rsqrt(ms + EPS) * g) * (1.0 + sc) + sh


def _rms(x, g):
    ms = jnp.mean(x * x, axis=-1, keepdims=True)
    return x * lax.rsqrt(ms + EPS) * g


def _sigmoid(x):
    return 0.5 * jnp.tanh(0.5 * x) + 0.5


def _ada_body(c_ref, w_ref, b_ref, o_ref):
    c = c_ref[...]
    ca = c * _sigmoid(c)
    o_ref[...] = jnp.dot(ca.astype(bf16), w_ref[...].astype(bf16),
                         preferred_element_type=f32) + b_ref[...]


def _ada(c8, w, b):
    d = w.shape[0]
    n = w.shape[1] // d
    return pl.pallas_call(
        _ada_body,
        out_shape=jax.ShapeDtypeStruct((8, n * d), f32),
        grid=(n,),
        in_specs=[pl.BlockSpec((8, d), lambda j: (0, 0)),
                  pl.BlockSpec((d, d), lambda j: (0, j)),
                  pl.BlockSpec((1, d), lambda j: (0, j))],
        out_specs=pl.BlockSpec((8, d), lambda j: (0, j)),
        compiler_params=_params(("arbitrary",)),
        name="ada",
    )(c8, w, b)


def _proj_body(x_ref, mod_ref, g_ref, w_ref, lng_ref, lnb_ref,
               q_ref, k_ref, v_ref, u_ref, vv_ref):
    h = _adaln(x_ref[0], g_ref[...], mod_ref[0, 1:2, :], mod_ref[0, 0:1, :]).astype(bf16)
    aw = ATT_WIDTH
    z = jnp.dot(h, w_ref[:, 3 * aw:], preferred_element_type=f32)
    p = jnp.dot(h, w_ref[:, :3 * aw], preferred_element_type=f32)
    q_ref[0] = (p[:, 0:aw] * (HEAD_DIM ** -0.5 * LOG2E)).astype(bf16)
    k_ref[0] = p[:, aw:2 * aw].astype(bf16)
    v_ref[0] = p[:, 2 * aw:3 * aw].astype(bf16)
    zg = 0.5 * z * (1.0 + lax.erf(z * (2.0 ** -0.5)))
    u_ref[0] = zg[:, :SGU_WIDTH].astype(bf16)
    vv = zg[:, SGU_WIDTH:]
    mu = jnp.mean(vv, axis=-1, keepdims=True)
    var = jnp.mean(jnp.square(vv - mu), axis=-1, keepdims=True)
    vn = (vv - mu) * lax.rsqrt(var + EPS) * lng_ref[...] + lnb_ref[...]
    vv_ref[0] = vn.astype(bf16)


def _proj(x, mod3, g, w_in, lng, lnb):
    b, s, d = x.shape
    tok = lambda w: pl.BlockSpec((1, TMP, w), lambda bi, i: (bi, i, 0))
    full = lambda a: pl.BlockSpec(a.shape, lambda bi, i: (0,) * a.ndim)
    o512 = jax.ShapeDtypeStruct((b, s, ATT_WIDTH), bf16)
    return pl.pallas_call(
        _proj_body,
        out_shape=(o512,) * 5,
        grid=(b, s // TMP),
        in_specs=[tok(d), pl.BlockSpec((1, 6, d), lambda bi, i: (bi, 0, 0)),
                  full(g), full(w_in), full(lng), full(lnb)],
        out_specs=(tok(ATT_WIDTH),) * 5,
        compiler_params=_params(("parallel", "arbitrary")),
        name="proj",
    )(x, mod3, g, w_in, lng, lnb)


def _attn_body(q_ref, kp_ref, kc_ref, vp_ref, vc_ref, bvec_ref, o_ref, bias_ref):
    first = pl.program_id(1) == 0
    nhp = ATT_HEADS // 2

    @pl.when(first)
    def _():
        i = lax.broadcasted_iota(i32, (QG, KBAND), 0)
        j = lax.broadcasted_iota(i32, (QG, KBAND), 1)
        jb = j - (i // CHUNK) * CHUNK
        in_band = jnp.logical_and(jb >= 0, jb < CHUNK * (N_LEFT + 1))
        for h in range(ATT_HEADS):
            rows = jnp.broadcast_to(bvec_ref[h:h + 1, :], (QG, BIAS_PERIOD))
            toep = pltpu.roll(rows, 0, 1, stride=1, stride_axis=0)[:, :KBAND]
            bias_ref[h // 2, (h % 2) * QG:(h % 2 + 1) * QG, :] = jnp.where(in_band, toep, NEG)

    lo = lax.broadcasted_iota(i32, (QG, LANES), 1) < HEAD_DIM
    has_prev = jnp.logical_not(first)
    for p in range(TM // QG):
        r0 = p * QG
        n_cur = KBAND - (TM - r0)
        ind = jnp.concatenate([jnp.where(has_prev, 1.0, 0.0) * jnp.ones((TM - r0, LANES), f32),
                               jnp.ones((n_cur, LANES), f32)], axis=0).astype(bf16)
        for hp in range(nhp):
            c0 = hp * LANES
            qp = q_ref[0, r0:r0 + QG, c0:c0 + LANES]
            zero = jnp.zeros_like(qp)
            q2 = jnp.concatenate([jnp.where(lo, qp, zero), jnp.where(lo, zero, qp)], axis=0)
            kprev = kp_ref[0, r0:TM, c0:c0 + LANES]
            vprev = vp_ref[0, r0:TM, c0:c0 + LANES]
            kprev = jnp.where(has_prev, kprev, jnp.zeros_like(kprev))
            vprev = jnp.where(has_prev, vprev, jnp.zeros_like(vprev))
            kb = jnp.concatenate([kprev, kc_ref[0, 0:n_cur, c0:c0 + LANES]], axis=0)
            vb = jnp.concatenate([vprev, vc_ref[0, 0:n_cur, c0:c0 + LANES]], axis=0)
            s = lax.dot_general(q2, kb, (((1,), (1,)), ((), ())), preferred_element_type=f32)
            sb = (s + bias_ref[hp]).astype(bf16)
            m = jnp.max(sb, axis=-1, keepdims=True)
            e = jnp.exp2(sb - m)
            o2 = jnp.dot(e, jnp.concatenate([vb, ind], axis=1),
                         preferred_element_type=f32)
            on = o2[:, :LANES] / o2[:, LANES:]
            o = jnp.where(lo, on[:QG], on[QG:])
            o_ref[0, r0:r0 + QG, c0:c0 + LANES] = o.astype(bf16)


def _attn(q, k, v, bvec):
    b, s, w = q.shape
    cur = pl.BlockSpec((1, TM, w), lambda bi, i: (bi, i, 0))
    prev = pl.BlockSpec((1, TM, w), lambda bi, i: (bi, jnp.maximum(i - 1, 0), 0))
    return pl.pallas_call(
        _attn_body,
        out_shape=jax.ShapeDtypeStruct((b, s, w), bf16),
        grid=(b, s // TM),
        in_specs=[cur, prev, cur, prev, cur,
                  pl.BlockSpec(bvec.shape, lambda bi, i: (0, 0))],
        out_specs=cur,
        scratch_shapes=[pltpu.VMEM((ATT_HEADS // 2, 2 * QG, KBAND), f32)],
        compiler_params=_params(("parallel", "arbitrary")),
        name="attn",
    )(q, k, k, v, v, bvec)


def _attn_bias_vec(rel_bias):
    h = rel_bias.shape[0]
    n_far = N_LEFT * CHUNK - MAX_REL
    assert BIAS_PERIOD >= QG + KBAND - 1 and n_far >= 0
    far = jnp.broadcast_to(rel_bias[:, 2 * MAX_REL:], (h, BIAS_PERIOD))
    near = rel_bias[:, :0:-1]
    v = jnp.concatenate([far[:, :n_far], near, far[:, n_far + 2 * MAX_REL:]], axis=1)
    return v.astype(f32) * LOG2E


def _route(hb, wr_ref, br_ref):
    lg = lax.dot_general(wr_ref[...], hb, (((1,), (1,)), ((), ())),
                         preferred_element_type=f32) + br_ref[...]
    e_iota = lax.broadcasted_iota(i32, lg.shape, 0)
    vals, idxs = [], []
    hits = jnp.zeros(lg.shape, f32)
    for _ in range(TOP_K):
        m = jnp.max(lg, axis=0, keepdims=True)
        idx = jnp.min(jnp.where(lg == m, e_iota, N_EXPERTS), axis=0, keepdims=True)
        hit = e_iota == idx
        hits = hits + jnp.where(hit, 1.0, 0.0)
        lg = jnp.where(hit, -jnp.inf, lg)
        vals.append(m)
        idxs.append(idx)
    ex = [jnp.exp(v - vals[0]) for v in vals]
    den = ex[0] + ex[1] + ex[2] + ex[3]
    return (jnp.concatenate(idxs, axis=0), jnp.concatenate([e / den for e in ex], axis=0), hits)


def _mix_body(x_ref, mod_ref, gpre_ref, gpost_ref, u_ref, vv_ref, ya_ref, ws_ref, bs_ref,
              wa_ref, wb_ref, wg_ref, bg_ref, wo_ref, gffn_ref, wr_ref, br_ref,
              o_ref, h2_ref, pos_ref, gt_ref, tab_ref, ybuf):
    d = D_MODEL
    x = x_ref[0]
    h = _adaln(x, gpre_ref[...], mod_ref[0, 1:2, :], mod_ref[0, 0:1, :]).astype(bf16)
    gs = _sigmoid(jnp.dot(h, wg_ref[...], preferred_element_type=f32) + bg_ref[...])

    blk = SGU_BLOCK
    row = lax.broadcasted_iota(i32, (2 * blk, blk), 0)
    colv = lax.broadcasted_iota(i32, (2 * blk, blk), 1)
    causal = colv <= jnp.bitwise_and(row, blk - 1)
    lo = lax.broadcasted_iota(i32, (blk, LANES), 1) < (LANES // 2)
    for gp in range(SGU_WIDTH // LANES):
        c0 = gp * LANES
        w2 = ws_ref[gp]
        w2 = jnp.where(causal, w2, jnp.zeros_like(w2))
        for bi in range(TMX // blk):
            r0 = bi * blk
            s2 = jnp.dot(w2, vv_ref[0, r0:r0 + blk, c0:c0 + LANES], preferred_element_type=f32)
            s = jnp.where(lo, s2[:blk], s2[blk:]) + bs_ref[:, c0:c0 + LANES]
            yb = u_ref[0, r0:r0 + blk, c0:c0 + LANES].astype(f32) * s
            ybuf[r0:r0 + blk, c0:c0 + LANES] = yb.astype(bf16)

    a = jnp.dot(ya_ref[0], wa_ref[...], preferred_element_type=f32)
    bb = jnp.dot(ybuf[...], wb_ref[...], preferred_element_type=f32)
    merged = gs[:, :d] * a + gs[:, d:] * bb
    y = jnp.dot(merged.astype(bf16), wo_ref[...], preferred_element_type=f32)
    x1 = x + mod_ref[0, 2:3, :] * _rms(y, gpost_ref[...])
    o_ref[0] = x1

    hb = _adaln(x1, gffn_ref[...], mod_ref[0, 4:5, :], mod_ref[0, 3:4, :]).astype(bf16)
    h2_ref[0] = hb
    ids, gates, hits = _route(hb, wr_ref, br_ref)
    pos_ref[...] = jnp.concatenate(
        [jnp.concatenate(_sorted_positions(ids[:, j * TS:(j + 1) * TS]), axis=0)
         for j in range(TMX // TS)], axis=1)
    gt_ref[...] = gates

    step = pl.program_id(0) * pl.num_programs(1) + pl.program_id(1)

    @pl.when(step == 0)
    def _():
        tab_ref[...] = jnp.zeros_like(tab_ref)

    lane = lax.broadcasted_iota(i32, tab_ref.shape, 1)
    acc = tab_ref[...]
    for j in range(TMX // TS):
        cnt = jnp.sum(hits[:, j * TS:(j + 1) * TS], axis=1, keepdims=True)
        c8 = jnp.floor((cnt + (ROW_ALIGN - 1.0)) * (1.0 / ROW_ALIGN)) * ROW_ALIGN
        acc = acc + jnp.where(lane == step * (TMX // TS) + j, c8, 0.0)
    tab_ref[...] = acc


def _mix(x, mod3, gpre, gpost, u, vv, ya, ws2, bsf, wa, wb, wg, bg, wo, gffn, wrt, br, ntp):
    b, s, d = x.shape
    n = b * s
    tok = lambda w: pl.BlockSpec((1, TMX, w), lambda bi, i: (bi, i, 0))
    full = lambda a: pl.BlockSpec(a.shape, lambda bi, i: (0,) * a.ndim)
    lane = pl.BlockSpec((TOP_K, TMX), lambda bi, i: (0, bi * (s // TMX) + i))
    return pl.pallas_call(
        _mix_body,
        out_shape=(jax.ShapeDtypeStruct((b, s, d), f32),
                   jax.ShapeDtypeStruct((b, s, d), bf16),
                   jax.ShapeDtypeStruct((TOP_K, n), i32),
                   jax.ShapeDtypeStruct((TOP_K, n), f32),
                   jax.ShapeDtypeStruct((N_EXPERTS, ntp), f32)),
        grid=(b, s // TMX),
        in_specs=[tok(d), pl.BlockSpec((1, 6, d), lambda bi, i: (bi, 0, 0)),
                  full(gpre), full(gpost), tok(SGU_WIDTH), tok(SGU_WIDTH), tok(ATT_WIDTH),
                  full(ws2), full(bsf), full(wa), full(wb), full(wg), full(bg), full(wo),
                  full(gffn), full(wrt), full(br)],
        out_specs=(tok(d), tok(d), lane, lane,
                   pl.BlockSpec((N_EXPERTS, ntp), lambda bi, i: (0, 0))),
        scratch_shapes=[pltpu.VMEM((TMX, SGU_WIDTH), bf16)],
        compiler_params=_params(("arbitrary", "arbitrary")),
        name="mix",
    )(x, mod3, gpre, gpost, u, vv, ya, ws2, bsf, wa, wb, wg, bg, wo, gffn, wrt, br)


def _sorted_positions(ti):
    ts = ti.shape[1]
    e_iota = lax.broadcasted_iota(i32, (N_EXPERTS, ts), 0)
    upper = (lax.broadcasted_iota(i32, (ts, ts), 0) < lax.broadcasted_iota(i32, (ts, ts), 1))
    upper = jnp.where(upper, 1.0, 0.0).astype(bf16)
    hits, prefs, cnts = [], [], []
    for k in range(TOP_K):
        hit = ti[k:k + 1, :] == e_iota
        hf = jnp.where(hit, 1.0, 0.0)
        prefs.append(jnp.dot(hf.astype(bf16), upper, preferred_element_type=f32))
        cnts.append(jnp.sum(hf, axis=1, keepdims=True))
        hits.append(hit)
    total = cnts[0] + cnts[1] + cnts[2] + cnts[3]
    c8 = jnp.floor((total + (ROW_ALIGN - 1.0)) * (1.0 / ROW_ALIGN)) * ROW_ALIGN
    lower = (lax.broadcasted_iota(i32, (N_EXPERTS, N_EXPERTS), 1)
             < lax.broadcasted_iota(i32, (N_EXPERTS, N_EXPERTS), 0))
    lower = jnp.where(lower, 1.0, 0.0).astype(bf16)
    c8b = jnp.broadcast_to(c8, (N_EXPERTS, LANES)).astype(bf16)
    start = jnp.dot(lower, c8b, preferred_element_type=f32)[:, 0:1]
    pos = []
    for k in range(TOP_K):
        pe = start + prefs[k]
        pos.append(jnp.sum(jnp.where(hits[k], pe, 0.0), axis=0, keepdims=True).astype(i32))
        start = start + cnts[k]
    return pos


def _pack_halves(x, exact):
    half = x.shape[1] // 2
    lo = lax.bitcast_convert_type(x[:, :half], u32)
    hi = lax.bitcast_convert_type(x[:, half:], u32)
    if not exact:
        hi = jnp.bitwise_and(hi, jnp.uint32(0xFFFF0000))
    return jnp.bitwise_or(lax.shift_right_logical(lo, jnp.uint32(16)), hi)


def _unpack_halves(w):
    lo = lax.bitcast_convert_type(lax.shift_left(w, jnp.uint32(16)), f32)
    hi = lax.bitcast_convert_type(jnp.bitwise_and(w, jnp.uint32(0xFFFF0000)), f32)
    return lo, hi


def _start_runs(i, toff_ref, dst_ref, c8_ref, make):
    unroll = 4

    def some(q, carry):
        for k in range(unroll):
            j = i * N_EXPERTS + q * unroll + k
            n = pl.multiple_of(c8_ref[j], ROW_ALIGN)
            so = pl.multiple_of(toff_ref[j], ROW_ALIGN)
            do = pl.multiple_of(dst_ref[j], ROW_ALIGN)

            @pl.when(n > 0)
            def _():
                make(so, do, n).start(priority=k % 2)
        return carry
    lax.fori_loop(0, N_EXPERTS // unroll, some, 0)


def _tile_rows(i, toff_ref, c8_ref):
    j = i * N_EXPERTS + (N_EXPERTS - 1)
    return pl.multiple_of(toff_ref[j] + c8_ref[j], ROW_ALIGN)


def _dispatch_body(toff_ref, dst_ref, c8_ref, tstart_ref, tlen_ref, nu_ref,
                   h2_ref, pos_ref, xb_ref, sbuf, zbuf, sems, sem):
    i = pl.program_id(0)
    base = lax.rem(i, 2) * TG
    r = lax.broadcasted_iota(i32, (RS, TS), 0)
    for j in range(TG):
        p = [pos_ref[k:k + 1, j * TS:(j + 1) * TS] for k in range(TOP_K)]
        pm = jnp.where(r == p[0], 1.0, jnp.where(r == p[1], 1.0,
             jnp.where(r == p[2], 1.0, jnp.where(r == p[3], 1.0, 0.0))))
        srt = jnp.dot(pm.astype(bf16), h2_ref[j * TS:(j + 1) * TS, :], preferred_element_type=f32)
        sbuf[base + j] = _pack_halves(srt, exact=True)

    for j in range(TG):
        def make(so, do, n, sl=base + j):
            return pltpu.make_async_copy(sbuf.at[sl, pl.ds(so, n)], xb_ref.at[pl.ds(do, n)],
                                         sems.at[sl])
        _start_runs(i * TG + j, toff_ref, dst_ref, c8_ref, make)

    def wait_tile(t, sl):
        n = _tile_rows(t, toff_ref, c8_ref)
        pltpu.make_async_copy(sbuf.at[sl, pl.ds(0, n)], xb_ref.at[pl.ds(0, n)], sems.at[sl]).wait()

    @pl.when(i > 0)
    def _():
        for j in range(TG):
            wait_tile((i - 1) * TG + j, TG - base + j)

    @pl.when(i == pl.num_programs(0) - 1)
    def _():
        for j in range(TG):
            wait_tile(i * TG + j, base + j)
        zbuf[...] = jnp.zeros_like(zbuf)

        def fill(action):
            def tail(e, carry):
                n = pl.multiple_of(tlen_ref[e], ROW_ALIGN)
                do = pl.multiple_of(tstart_ref[e], ROW_ALIGN)

                @pl.when(n > 0)
                def _():
                    action(pltpu.make_async_copy(zbuf.at[pl.ds(0, n)], xb_ref.at[pl.ds(do, n)], sem))
                return carry
            lax.fori_loop(0, N_EXPERTS, tail, 0)

            def unused(b, carry):
                do = pl.multiple_of(b * BM, BM)
                action(pltpu.make_async_copy(zbuf, xb_ref.at[pl.ds(do, BM)], sem))
                return carry
            lax.fori_loop(nu_ref[0], xb_ref.shape[0] // BM, unused, 0)
        fill(lambda c: c.start())
        fill(lambda c: c.wait())


def _dispatch(toff, dst, c8s, tstart, tlen, n_used, h2, pos, p_rows):
    n, d = h2.shape
    return pl.pallas_call(
        _dispatch_body,
        out_shape=jax.ShapeDtypeStruct((p_rows, d // 2), u32),
        grid_spec=pltpu.PrefetchScalarGridSpec(
            num_scalar_prefetch=6,
            grid=(n // (TG * TS),),
            in_specs=[pl.BlockSpec((TG * TS, d), lambda i, *_: (i, 0)),
                      pl.BlockSpec((TOP_K, TG * TS), lambda i, *_: (0, i))],
            out_specs=pl.BlockSpec(memory_space=pl.ANY),
            scratch_shapes=[pltpu.VMEM((2 * TG, RS, d // 2), u32), pltpu.VMEM((BM, d // 2), u32),
                            pltpu.SemaphoreType.DMA((2 * TG,)), pltpu.SemaphoreType.DMA(())]),
        compiler_params=_params(("arbitrary",)),
        name="dispatch",
    )(toff, dst, c8s, tstart, tlen, n_used, h2, pos)


def _expert_body(be_ref, nxt_ref, nu_ref, rows_ref, x_ref, wgu_hbm, bgu_ref, wd_hbm, bd_ref, o_ref,
                 wgu_st, wd_st, wgu_bf, wd_bf, sems):
    del nu_ref
    b = pl.program_id(0)
    d = D_MODEL
    e = be_ref[b]

    def weight_copies(ex):
        return (pltpu.make_async_copy(wgu_hbm.at[ex], wgu_st, sems.at[0]),
                pltpu.make_async_copy(wd_hbm.at[ex], wd_st, sems.at[1]))

    @pl.when(b == 0)
    def _():
        for c in weight_copies(e):
            c.start()

    @pl.when(jnp.logical_or(b == 0, e != be_ref[jnp.maximum(b - 1, 0)]))
    def _():
        for c in weight_copies(e):
            c.wait()
        def cast_rows(c, carry):
            rows = pl.ds(pl.multiple_of(c * CAST_ROWS, CAST_ROWS), CAST_ROWS)
            wgu_bf[rows, :] = wgu_st[rows, :].astype(bf16)
            wd_bf[rows, :] = wd_st[rows, :].astype(bf16)
            return carry
        lax.fori_loop(0, d // CAST_ROWS, cast_rows, 0)
        nxt = nxt_ref[e]

        @pl.when(nxt >= 0)
        def _():
            for c in weight_copies(nxt):
                c.start()

    def mlp_rows(m):
        xb = jnp.concatenate(_unpack_halves(x_ref[:m, :]), axis=1).astype(bf16)
        gu = jnp.dot(xb, wgu_bf[...], preferred_element_type=f32) + bgu_ref[0]
        glu = jnp.minimum(gu[:, :d], SWIGLU_LIMIT)
        lin = jnp.clip(gu[:, d:], -SWIGLU_LIMIT, SWIGLU_LIMIT)
        act = glu * _sigmoid(SWIGLU_ALPHA * glu) * (lin + 1.0)
        y = jnp.dot(act.astype(bf16), wd_bf[...], preferred_element_type=f32) + bd_ref[0]
        o_ref[:m, :] = _pack_halves(y, exact=False)
        if m < BM:
            o_ref[m:, :] = jnp.zeros((BM - m, o_ref.shape[1]), o_ref.dtype)

    valid = rows_ref[b]
    for lo, hi in zip((0,) + BM_STEPS[:-1], BM_STEPS):
        @pl.when(jnp.logical_and(valid > lo, valid <= hi))
        def _(hi=hi):
            mlp_rows(hi)


def _experts(blk_e, nxt_e, n_used, blk_rows, xb, wgu, bgu, wd, bd):
    rows = pl.BlockSpec((BM, xb.shape[1]), lambda b, be, nx, nu, br: (b, 0))
    per_e = lambda a: pl.BlockSpec((1,) + a.shape[1:], lambda b, be, nx, nu, br: (be[b], 0, 0))
    hbm = pl.BlockSpec(memory_space=pl.ANY)
    return pl.pallas_call(
        _expert_body,
        out_shape=jax.ShapeDtypeStruct(xb.shape, xb.dtype),
        grid_spec=pltpu.PrefetchScalarGridSpec(
            num_scalar_prefetch=4,
            grid=(n_used[0],),
            in_specs=[rows, hbm, per_e(bgu), hbm, per_e(bd)],
            out_specs=rows,
            scratch_shapes=[pltpu.VMEM(wgu.shape[1:], f32), pltpu.VMEM(wd.shape[1:], f32),
                            pltpu.VMEM(wgu.shape[1:], bf16), pltpu.VMEM(wd.shape[1:], bf16),
                            pltpu.SemaphoreType.DMA((2,))]),
        input_output_aliases={4: 0},
        compiler_params=_params(("arbitrary",)),
        name="experts",
    )(blk_e, nxt_e, n_used, blk_rows, xb, wgu, bgu, wd, bd)


def _combine_body(toff_ref, dst_ref, c8_ref, yb_ref, pos_ref, gt_ref, x1_ref, mod_ref, g_ref,
                  o_ref, sbuf, sems):
    i = pl.program_id(0)
    base = lax.rem(i, 2) * TG

    def fetch(step, first_slot):
        for j in range(TG):
            def make(so, do, n, sl=first_slot + j):
                return pltpu.make_async_copy(yb_ref.at[pl.ds(do, n)], sbuf.at[sl, pl.ds(so, n)],
                                             sems.at[sl])
            _start_runs(step * TG + j, toff_ref, dst_ref, c8_ref, make)

    @pl.when(i == 0)
    def _():
        sbuf[...] = jnp.zeros_like(sbuf)
        fetch(i, base)

    @pl.when(i + 1 < pl.num_programs(0))
    def _():
        fetch(i + 1, TG - base)

    r = lax.broadcasted_iota(i32, (RS, TS), 0)
    pms, grows = [], []
    for j in range(TG):
        p = [pos_ref[k:k + 1, j * TS:(j + 1) * TS] for k in range(TOP_K)]
        gt = gt_ref[:, j * TS:(j + 1) * TS]
        gm = jnp.where(r == p[0], gt[0:1, :], jnp.where(r == p[1], gt[1:2, :],
             jnp.where(r == p[2], gt[2:3, :], jnp.where(r == p[3], gt[3:4, :], 0.0))))
        pms.append(jnp.where(gm != 0.0, 1.0, 0.0).astype(bf16))
        grows.append(jnp.sum(gm, axis=1, keepdims=True))

    for j in range(TG):
        n = _tile_rows(i * TG + j, toff_ref, c8_ref)
        pltpu.make_async_copy(yb_ref.at[pl.ds(0, n)], sbuf.at[base + j, pl.ds(0, n)],
                              sems.at[base + j]).wait()
    for j in range(TG):
        halves = [lax.dot_general(pms[j], (h * grows[j]).astype(bf16), (((0,), (0,)), ((), ())),
                                  preferred_element_type=f32)
                  for h in _unpack_halves(sbuf[base + j])]
        y = jnp.concatenate(halves, axis=1)
        rows = slice(j * TS, (j + 1) * TS)
        o_ref[rows, :] = x1_ref[rows, :] + mod_ref[0, 5:6, :] * _rms(y, g_ref[...])


def _combine(toff, dst, c8s, yb, pos, gt, x1, mod3, g, tiles_per_batch):
    n, d = x1.shape
    tok = pl.BlockSpec((TG * TS, d), lambda i, *_: (i, 0))
    lane = pl.BlockSpec((TOP_K, TG * TS), lambda i, *_: (0, i))
    steps_per_batch = tiles_per_batch // TG
    return pl.pallas_call(
        _combine_body,
        out_shape=jax.ShapeDtypeStruct((n, d), f32),
        grid_spec=pltpu.PrefetchScalarGridSpec(
            num_scalar_prefetch=3,
            grid=(n // (TG * TS),),
            in_specs=[pl.BlockSpec(memory_space=pl.ANY), lane, lane, tok,
                      pl.BlockSpec((1, 6, d), lambda i, *_: (i // steps_per_batch, 0, 0)),
                      pl.BlockSpec(g.shape, lambda i, *_: (0, 0))],
            out_specs=tok,
            scratch_shapes=[pltpu.VMEM((2 * TG, RS, d // 2), u32), pltpu.SemaphoreType.DMA((2 * TG,))]),
        compiler_params=_params(("arbitrary",)),
        name="combine",
    )(toff, dst, c8s, yb, pos, gt, x1, mod3, g)


def _layout_tables(tab, nt, p_rows):
    c8 = tab[:, :nt].T.astype(i32)
    toff = jnp.cumsum(c8, axis=1) - c8
    len8 = jnp.sum(c8, axis=0)
    seg = (len8 + BM - 1) // BM * BM
    gend = jnp.cumsum(seg)
    gstart = gend - seg
    dst = jnp.cumsum(c8, axis=0) - c8 + gstart[None, :]
    n_used = gend[-1] // BM
    blk = jnp.arange(p_rows // BM, dtype=i32)
    last = jnp.minimum(blk, n_used - 1)
    blk_e = jnp.sum((gend[None, :] <= (last * BM)[:, None]).astype(i32), axis=1)
    blk_e = jnp.minimum(blk_e, N_EXPERTS - 1)
    lo = jnp.maximum(gstart[None, :], (blk * BM)[:, None])
    hi = jnp.minimum((gstart + len8)[None, :], ((blk + 1) * BM)[:, None])
    blk_rows = jnp.sum(jnp.maximum(hi - lo, 0), axis=1).astype(i32)
    ids = jnp.arange(N_EXPERTS, dtype=i32)
    later = jnp.logical_and(ids[None, :] > ids[:, None], (seg > 0)[None, :])
    nxt_e = jnp.min(jnp.where(later, ids[None, :], N_EXPERTS), axis=1)
    nxt_e = jnp.where(nxt_e == N_EXPERTS, -1, nxt_e).astype(i32)
    return (toff.reshape(-1), dst.reshape(-1), c8.reshape(-1), gstart + len8, seg - len8,
            blk_e, nxt_e, n_used.reshape(1).astype(i32), blk_rows)


def kernel(x, c, w_ada, b_ada, g_pre_mix, g_post_mix, w_in, rel_bias, sgu_ln_g, sgu_ln_b,
           w_spatial, b_spatial, w_branch_a, w_branch_b, w_gate, b_gate, w_out,
           g_pre_ffn, g_post_ffn, w_router, b_router, w_gate_up, b_gate_up, w_down, b_down):
    b, s, d = x.shape
    assert d == D_MODEL and s % max(TM, TMP, TMX) == 0 and s % (TG * TS) == 0
    n = b * s
    nt = n // TS
    ntp = -(-nt // LANES) * LANES
    p_rows = -(-(n * TOP_K + nt * N_EXPERTS * (ROW_ALIGN - 1) + N_EXPERTS * (BM - 1)) // BM) * BM
    depth = w_ada.shape[0]
    c8 = jnp.pad(c, ((0, 8 - b), (0, 0)))
    row = lambda a: a.reshape(1, -1)

    for l in range(depth):
        mod = _ada(c8, w_ada[l], row(b_ada[l]))[:b]
        mod3 = mod.reshape(b, 6, d)

        q, k, v, u, vv = _proj(x, mod3, row(g_pre_mix[l]), w_in[l].astype(bf16),
                               row(sgu_ln_g[l]), row(sgu_ln_b[l]))
        ya = _attn(q, k, v, _attn_bias_vec(rel_bias[l]))
        ws2 = w_spatial[l].astype(bf16).reshape(-1, 2 * SGU_BLOCK, SGU_BLOCK)
        bsf = jnp.repeat(b_spatial[l].T, SGU_WIDTH // b_spatial.shape[1], axis=1)
        x1, h2, pos, gt, tab = _mix(x, mod3, row(g_pre_mix[l]), row(g_post_mix[l]), u, vv, ya, ws2, bsf,
                                   w_branch_a[l].astype(bf16), w_branch_b[l].astype(bf16),
                                   w_gate[l].astype(bf16), row(b_gate[l]), w_out[l].astype(bf16),
                                   row(g_pre_ffn[l]), w_router[l].T.astype(bf16),
                                   b_router[l].reshape(-1, 1), ntp)
        x1f = x1.reshape(n, d)
        h2 = h2.reshape(n, d)
        toff, dst, c8s, tstart, tlen, blk_e, nxt_e, n_used, blk_rows = _layout_tables(tab, nt, p_rows)
        xb = _dispatch(toff, dst, c8s, tstart, tlen, n_used, h2, pos, p_rows)
        yb = _experts(blk_e, nxt_e, n_used, blk_rows, xb, w_gate_up[l], b_gate_up[l][:, None, :],
                      w_down[l], b_down[l][:, None, :])
        x = _combine(toff, dst, c8s, yb, pos, gt, x1f, mod3, row(g_post_ffn[l]), s // TS).reshape(b, s, d)
    return x
```

```python
import jax
import jax.numpy as jnp
from jax import lax
from jax.experimental import pallas as pl
from jax.experimental.pallas import tpu as pltpu

bf16 = jnp.bfloat16
f32 = jnp.float32
i32 = jnp.int32
u32 = jnp.uint32

D_MODEL = 1024
CHUNK = 64
N_LEFT = 8
ATT_HEADS = 8
HEAD_DIM = 64
ATT_WIDTH = 512
MAX_REL = 128
SGU_BLOCK = 128
SGU_WIDTH = 512
N_EXPERTS = 32
TOP_K = 4
SWIGLU_LIMIT = 7.0
SWIGLU_ALPHA = 1.702
EPS = 1e-6
NEG = -1e30
LOG2E = 1.4426950408889634

LANES = 128
ROW_ALIGN = 8
TM = 512
TMP = 1024
TMX = 1024
QCH = 4
QG = QCH * CHUNK
KBAND = (N_LEFT + QCH) * CHUNK
BIAS_PERIOD = 1024
TS = 256
TG = 4
RS = TS * TOP_K + N_EXPERTS * ROW_ALIGN
BM = 1024
BM_STEPS = tuple(range(LANES, BM + 1, LANES))
CAST_ROWS = 64
VMEM_LIMIT = 56 * 2**20


def _params(sem):
    return pltpu.CompilerParams(dimension_semantics=sem, vmem_limit_bytes=VMEM_LIMIT)


def _adaln(x, g, sc, sh):
    ms = jnp.mean(x * x, axis=-1, keepdims=True)
    return (x * lax.rsqrt(ms + EPS) * g) * (1.0 + sc) + sh


def _rms(x, g):
    ms = jnp.mean(x * x, axis=-1, keepdims=True)
    return x * lax.rsqrt(ms + EPS) * g


def _sigmoid(x):
    return 0.5 * jnp.tanh(0.5 * x) + 0.5


def _ada_body(c_ref, w_ref, b_ref, o_ref):
    c = c_ref[...]
    ca = c * _sigmoid(c)
    o_ref[...] = jnp.dot(ca.astype(bf16), w_ref[...].astype(bf16),
                         preferred_element_type=f32) + b_ref[...]


def _ada(c8, w, b):
    d = w.shape[0]
    n = w.shape[1] // d
    return pl.pallas_call(
        _ada_body,
        out_shape=jax.ShapeDtypeStruct((8, n * d), f32),
        grid=(n,),
        in_specs=[pl.BlockSpec((8, d), lambda j: (0, 0)),
                  pl.BlockSpec((d, d), lambda j: (0, j)),
                  pl.BlockSpec((1, d), lambda j: (0, j))],
        out_specs=pl.BlockSpec((8, d), lambda j: (0, j)),
        compiler_params=_params(("arbitrary",)),
        name="ada",
    )(c8, w, b)


def _proj_body(x_ref, mod_ref, g_ref, w_ref, lng_ref, lnb_ref,
               q_ref, k_ref, v_ref, u_ref, vv_ref):
    h = _adaln(x_ref[0], g_ref[...], mod_ref[0, 1:2, :], mod_ref[0, 0:1, :]).astype(bf16)
    aw = ATT_WIDTH
    z = jnp.dot(h, w_ref[:, 3 * aw:], preferred_element_type=f32)
    p = jnp.dot(h, w_ref[:, :3 * aw], preferred_element_type=f32)
    q_ref[0] = (p[:, 0:aw] * (HEAD_DIM ** -0.5 * LOG2E)).astype(bf16)
    k_ref[0] = p[:, aw:2 * aw].astype(bf16)
    v_ref[0] = p[:, 2 * aw:3 * aw].astype(bf16)
    zg = 0.5 * z * (1.0 + lax.erf(z * (2.0 ** -0.5)))
    u_ref[0] = zg[:, :SGU_WIDTH].astype(bf16)
    vv = zg[:, SGU_WIDTH:]
    mu = jnp.mean(vv, axis=-1, keepdims=True)
    var = jnp.mean(jnp.square(vv - mu), axis=-1, keepdims=True)
    vn = (vv - mu) * lax.rsqrt(var + EPS) * lng_ref[...] + lnb_ref[...]
    vv_ref[0] = vn.astype(bf16)


def _proj(x, mod3, g, w_in, lng, lnb):
    b, s, d = x.shape
    tok = lambda w: pl.BlockSpec((1, TMP, w), lambda bi, i: (bi, i, 0))
    full = lambda a: pl.BlockSpec(a.shape, lambda bi, i: (0,) * a.ndim)
    o512 = jax.ShapeDtypeStruct((b, s, ATT_WIDTH), bf16)
    return pl.pallas_call(
        _proj_body,
        out_shape=(o512,) * 5,
        grid=(b, s // TMP),
        in_specs=[tok(d), pl.BlockSpec((1, 6, d), lambda bi, i: (bi, 0, 0)),
                  full(g), full(w_in), full(lng), full(lnb)],
        out_specs=(tok(ATT_WIDTH),) * 5,
        compiler_params=_params(("parallel", "arbitrary")),
        name="proj",
    )(x, mod3, g, w_in, lng, lnb)


def _attn_body(q_ref, kp_ref, kc_ref, vp_ref, vc_ref, bvec_ref, o_ref, bias_ref):
    first = pl.program_id(1) == 0
    nhp = ATT_HEADS // 2

    @pl.when(first)
    def _():
        i = lax.broadcasted_iota(i32, (QG, KBAND), 0)
        j = lax.broadcasted_iota(i32, (QG, KBAND), 1)
        jb = j - (i // CHUNK) * CHUNK
        in_band = jnp.logical_and(jb >= 0, jb < CHUNK * (N_LEFT + 1))
        for h in range(ATT_HEADS):
            rows = jnp.broadcast_to(bvec_ref[h:h + 1, :], (QG, BIAS_PERIOD))
            toep = pltpu.roll(rows, 0, 1, stride=1, stride_axis=0)[:, :KBAND]
            bias_ref[h // 2, (h % 2) * QG:(h % 2 + 1) * QG, :] = jnp.where(in_band, toep, NEG)

    lo = lax.broadcasted_iota(i32, (QG, LANES), 1) < HEAD_DIM
    has_prev = jnp.logical_not(first)
    for p in range(TM // QG):
        r0 = p * QG
        n_cur = KBAND - (TM - r0)
        ind = jnp.concatenate([jnp.where(has_prev, 1.0, 0.0) * jnp.ones((TM - r0, LANES), f32),
                               jnp.ones((n_cur, LANES), f32)], axis=0).astype(bf16)
        for hp in range(nhp):
            c0 = hp * LANES
            qp = q_ref[0, r0:r0 + QG, c0:c0 + LANES]
            zero = jnp.zeros_like(qp)
            q2 = jnp.concatenate([jnp.where(lo, qp, zero), jnp.where(lo, zero, qp)], axis=0)
            kprev = kp_ref[0, r0:TM, c0:c0 + LANES]
            vprev = vp_ref[0, r0:TM, c0:c0 + LANES]
            kprev = jnp.where(has_prev, kprev, jnp.zeros_like(kprev))
            vprev = jnp.where(has_prev, vprev, jnp.zeros_like(vprev))
            kb = jnp.concatenate([kprev, kc_ref[0, 0:n_cur, c0:c0 + LANES]], axis=0)
            vb = jnp.concatenate([vprev, vc_ref[0, 0:n_cur, c0:c0 + LANES]], axis=0)
            s = lax.dot_general(q2, kb, (((1,), (1,)), ((), ())), preferred_element_type=f32)
            sb = (s + bias_ref[hp]).astype(bf16)
            m = jnp.max(sb, axis=-1, keepdims=True)
            e = jnp.exp2(sb - m)
            o2 = jnp.dot(e, jnp.concatenate([vb, ind], axis=1),
                         preferred_element_type=f32)
            on = o2[:, :LANES] / o2[:, LANES:]
            o = jnp.where(lo, on[:QG], on[QG:])
            o_ref[0, r0:r0 + QG, c0:c0 + LANES] = o.astype(bf16)


def _attn(q, k, v, bvec):
    b, s, w = q.shape
    cur = pl.BlockSpec((1, TM, w), lambda bi, i: (bi, i, 0))
    prev = pl.BlockSpec((1, TM, w), lambda bi, i: (bi, jnp.maximum(i - 1, 0), 0))
    return pl.pallas_call(
        _attn_body,
        out_shape=jax.ShapeDtypeStruct((b, s, w), bf16),
        grid=(b, s // TM),
        in_specs=[cur, prev, cur, prev, cur,
                  pl.BlockSpec(bvec.shape, lambda bi, i: (0, 0))],
        out_specs=cur,
        scratch_shapes=[pltpu.VMEM((ATT_HEADS // 2, 2 * QG, KBAND), f32)],
        compiler_params=_params(("parallel", "arbitrary")),
        name="attn",
    )(q, k, k, v, v, bvec)


def _attn_bias_vec(rel_bias):
    h = rel_bias.shape[0]
    n_far = N_LEFT * CHUNK - MAX_REL
    assert BIAS_PERIOD >= QG + KBAND - 1 and n_far >= 0
    far = jnp.broadcast_to(rel_bias[:, 2 * MAX_REL:], (h, BIAS_PERIOD))
    near = rel_bias[:, :0:-1]
    v = jnp.concatenate([far[:, :n_far], near, far[:, n_far + 2 * MAX_REL:]], axis=1)
    return v.astype(f32) * LOG2E


def _route(hb, wr_ref, br_ref):
    lg = lax.dot_general(wr_ref[...], hb, (((1,), (1,)), ((), ())),
                         preferred_element_type=f32) + br_ref[...]
    e_iota = lax.broadcasted_iota(i32, lg.shape, 0)
    vals, idxs = [], []
    hits = jnp.zeros(lg.shape, f32)
    for _ in range(TOP_K):
        m = jnp.max(lg, axis=0, keepdims=True)
        idx = jnp.min(jnp.where(lg == m, e_iota, N_EXPERTS), axis=0, keepdims=True)
        hit = e_iota == idx
        hits = hits + jnp.where(hit, 1.0, 0.0)
        lg = jnp.where(hit, -jnp.inf, lg)
        vals.append(m)
        idxs.append(idx)
    ex = [jnp.exp(v - vals[0]) for v in vals]
    den = ex[0] + ex[1] + ex[2] + ex[3]
    return (jnp.concatenate(idxs, axis=0), jnp.concatenate([e / den for e in ex], axis=0), hits)


def _mix_body(x_ref, mod_ref, gpre_ref, gpost_ref, u_ref, vv_ref, ya_ref, ws_ref, bs_ref,
              wa_ref, wb_ref, wg_ref, bg_ref, wo_ref, gffn_ref, wr_ref, br_ref,
              o_ref, h2_ref, pos_ref, gt_ref, tab_ref, ybuf):
    d = D_MODEL
    x = x_ref[0]
    h = _adaln(x, gpre_ref[...], mod_ref[0, 1:2, :], mod_ref[0, 0:1, :]).astype(bf16)
    gs = _sigmoid(jnp.dot(h, wg_ref[...], preferred_element_type=f32) + bg_ref[...])

    blk = SGU_BLOCK
    row = lax.broadcasted_iota(i32, (2 * blk, blk), 0)
    colv = lax.broadcasted_iota(i32, (2 * blk, blk), 1)
    causal = colv <= jnp.bitwise_and(row, blk - 1)
    lo = lax.broadcasted_iota(i32, (blk, LANES), 1) < (LANES // 2)
    for gp in range(SGU_WIDTH // LANES):
        c0 = gp * LANES
        w2 = ws_ref[gp]
        w2 = jnp.where(causal, w2, jnp.zeros_like(w2))
        for bi in range(TMX // blk):
            r0 = bi * blk
            s2 = jnp.dot(w2, vv_ref[0, r0:r0 + blk, c0:c0 + LANES], preferred_element_type=f32)
            s = jnp.where(lo, s2[:blk], s2[blk:]) + bs_ref[:, c0:c0 + LANES]
            yb = u_ref[0, r0:r0 + blk, c0:c0 + LANES].astype(f32) * s
            ybuf[r0:r0 + blk, c0:c0 + LANES] = yb.astype(bf16)

    a = jnp.dot(ya_ref[0], wa_ref[...], preferred_element_type=f32)
    bb = jnp.dot(ybuf[...], wb_ref[...], preferred_element_type=f32)
    merged = gs[:, :d] * a + gs[:, d:] * bb
    y = jnp.dot(merged.astype(bf16), wo_ref[...], preferred_element_type=f32)
    x1 = x + mod_ref[0, 2:3, :] * _rms(y, gpost_ref[...])
    o_ref[0] = x1

    hb = _adaln(x1, gffn_ref[...], mod_ref[0, 4:5, :], mod_ref[0, 3:4, :]).astype(bf16)
    h2_ref[0] = hb
    ids, gates, hits = _route(hb, wr_ref, br_ref)
    pos_ref[...] = jnp.concatenate(
        [jnp.concatenate(_sorted_positions(ids[:, j * TS:(j + 1) * TS]), axis=0)
         for j in range(TMX // TS)], axis=1)
    gt_ref[...] = gates

    step = pl.program_id(0) * pl.num_programs(1) + pl.program_id(1)

    @pl.when(step == 0)
    def _():
        tab_ref[...] = jnp.zeros_like(tab_ref)

    lane = lax.broadcasted_iota(i32, tab_ref.shape, 1)
    acc = tab_ref[...]
    for j in range(TMX // TS):
        cnt = jnp.sum(hits[:, j * TS:(j + 1) * TS], axis=1, keepdims=True)
        c8 = jnp.floor((cnt + (ROW_ALIGN - 1.0)) * (1.0 / ROW_ALIGN)) * ROW_ALIGN
        acc = acc + jnp.where(lane == step * (TMX // TS) + j, c8, 0.0)
    tab_ref[...] = acc


def _mix(x, mod3, gpre, gpost, u, vv, ya, ws2, bsf, wa, wb, wg, bg, wo, gffn, wrt, br, ntp):
    b, s, d = x.shape
    n = b * s
    tok = lambda w: pl.BlockSpec((1, TMX, w), lambda bi, i: (bi, i, 0))
    full = lambda a: pl.BlockSpec(a.shape, lambda bi, i: (0,) * a.ndim)
    lane = pl.BlockSpec((TOP_K, TMX), lambda bi, i: (0, bi * (s // TMX) + i))
    return pl.pallas_call(
        _mix_body,
        out_shape=(jax.ShapeDtypeStruct((b, s, d), f32),
                   jax.ShapeDtypeStruct((b, s, d), bf16),
                   jax.ShapeDtypeStruct((TOP_K, n), i32),
                   jax.ShapeDtypeStruct((TOP_K, n), f32),
                   jax.ShapeDtypeStruct((N_EXPERTS, ntp), f32)),
        grid=(b, s // TMX),
        in_specs=[tok(d), pl.BlockSpec((1, 6, d), lambda bi, i: (bi, 0, 0)),
                  full(gpre), full(gpost), tok(SGU_WIDTH), tok(SGU_WIDTH), tok(ATT_WIDTH),
                  full(ws2), full(bsf), full(wa), full(wb), full(wg), full(bg), full(wo),
                  full(gffn), full(wrt), full(br)],
        out_specs=(tok(d), tok(d), lane, lane,
                   pl.BlockSpec((N_EXPERTS, ntp), lambda bi, i: (0, 0))),
        scratch_shapes=[pltpu.VMEM((TMX, SGU_WIDTH), bf16)],
        compiler_params=_params(("arbitrary", "arbitrary")),
        name="mix",
    )(x, mod3, gpre, gpost, u, vv, ya, ws2, bsf, wa, wb, wg, bg, wo, gffn, wrt, br)


def _sorted_positions(ti):
    ts = ti.shape[1]
    e_iota = lax.broadcasted_iota(i32, (N_EXPERTS, ts), 0)
    upper = (lax.broadcasted_iota(i32, (ts, ts), 0) < lax.broadcasted_iota(i32, (ts, ts), 1))
    upper = jnp.where(upper, 1.0, 0.0).astype(bf16)
    hits, prefs, cnts = [], [], []
    for k in range(TOP_K):
        hit = ti[k:k + 1, :] == e_iota
        hf = jnp.where(hit, 1.0, 0.0)
        prefs.append(jnp.dot(hf.astype(bf16), upper, preferred_element_type=f32))
        cnts.append(jnp.sum(hf, axis=1, keepdims=True))
        hits.append(hit)
    total = cnts[0] + cnts[1] + cnts[2] + cnts[3]
    c8 = jnp.floor((total + (ROW_ALIGN - 1.0)) * (1.0 / ROW_ALIGN)) * ROW_ALIGN
    lower = (lax.broadcasted_iota(i32, (N_EXPERTS, N_EXPERTS), 1)
             < lax.broadcasted_iota(i32, (N_EXPERTS, N_EXPERTS), 0))
    lower = jnp.where(lower, 1.0, 0.0).astype(bf16)
    c8b = jnp.broadcast_to(c8, (N_EXPERTS, LANES)).astype(bf16)
    start = jnp.dot(lower, c8b, preferred_element_type=f32)[:, 0:1]
    pos = []
    for k in range(TOP_K):
        pe = start + prefs[k]
        pos.append(jnp.sum(jnp.where(hits[k], pe, 0.0), axis=0, keepdims=True).astype(i32))
        start = start + cnts[k]
    return pos


def _pack_halves(x, exact):
    half = x.shape[1] // 2
    lo = lax.bitcast_convert_type(x[:, :half], u32)
    hi = lax.bitcast_convert_type(x[:, half:], u32)
    if not exact:
        hi = jnp.bitwise_and(hi, jnp.uint32(0xFFFF0000))
    return jnp.bitwise_or(lax.shift_right_logical(lo, jnp.uint32(16)), hi)


def _unpack_halves(w):
    lo = lax.bitcast_convert_type(lax.shift_left(w, jnp.uint32(16)), f32)
    hi = lax.bitcast_convert_type(jnp.bitwise_and(w, jnp.uint32(0xFFFF0000)), f32)
    return lo, hi


def _start_runs(i, toff_ref, dst_ref, c8_ref, make):
    unroll = 4

    def some(q, carry):
        for k in range(unroll):
            j = i * N_EXPERTS + q * unroll + k
            n = pl.multiple_of(c8_ref[j], ROW_ALIGN)
            so = pl.multiple_of(toff_ref[j], ROW_ALIGN)
            do = pl.multiple_of(dst_ref[j], ROW_ALIGN)

            @pl.when(n > 0)
            def _():
                make(so, do, n).start(priority=k % 2)
        return carry
    lax.fori_loop(0, N_EXPERTS // unroll, some, 0)


def _tile_rows(i, toff_ref, c8_ref):
    j = i * N_EXPERTS + (N_EXPERTS - 1)
    return pl.multiple_of(toff_ref[j] + c8_ref[j], ROW_ALIGN)


def _dispatch_body(toff_ref, dst_ref, c8_ref, tstart_ref, tlen_ref, nu_ref,
                   h2_ref, pos_ref, xb_ref, sbuf, zbuf, sems, sem):
    i = pl.program_id(0)
    base = lax.rem(i, 2) * TG
    r = lax.broadcasted_iota(i32, (RS, TS), 0)
    for j in range(TG):
        p = [pos_ref[k:k + 1, j * TS:(j + 1) * TS] for k in range(TOP_K)]
        pm = jnp.where(r == p[0], 1.0, jnp.where(r == p[1], 1.0,
             jnp.where(r == p[2], 1.0, jnp.where(r == p[3], 1.0, 0.0))))
        srt = jnp.dot(pm.astype(bf16), h2_ref[j * TS:(j + 1) * TS, :], preferred_element_type=f32)
        sbuf[base + j] = _pack_halves(srt, exact=True)

    for j in range(TG):
        def make(so, do, n, sl=base + j):
            return pltpu.make_async_copy(sbuf.at[sl, pl.ds(so, n)], xb_ref.at[pl.ds(do, n)],
                                         sems.at[sl])
        _start_runs(i * TG + j, toff_ref, dst_ref, c8_ref, make)

    def wait_tile(t, sl):
        n = _tile_rows(t, toff_ref, c8_ref)
        pltpu.make_async_copy(sbuf.at[sl, pl.ds(0, n)], xb_ref.at[pl.ds(0, n)], sems.at[sl]).wait()

    @pl.when(i > 0)
    def _():
        for j in range(TG):
            wait_tile((i - 1) * TG + j, TG - base + j)

    @pl.when(i == pl.num_programs(0) - 1)
    def _():
        for j in range(TG):
            wait_tile(i * TG + j, base + j)
        zbuf[...] = jnp.zeros_like(zbuf)

        def fill(action):
            def tail(e, carry):
                n = pl.multiple_of(tlen_ref[e], ROW_ALIGN)
                do = pl.multiple_of(tstart_ref[e], ROW_ALIGN)

                @pl.when(n > 0)
                def _():
                    action(pltpu.make_async_copy(zbuf.at[pl.ds(0, n)], xb_ref.at[pl.ds(do, n)], sem))
                return carry
            lax.fori_loop(0, N_EXPERTS, tail, 0)

            def unused(b, carry):
                do = pl.multiple_of(b * BM, BM)
                action(pltpu.make_async_copy(zbuf, xb_ref.at[pl.ds(do, BM)], sem))
                return carry
            lax.fori_loop(nu_ref[0], xb_ref.shape[0] // BM, unused, 0)
        fill(lambda c: c.start())
        fill(lambda c: c.wait())


def _dispatch(toff, dst, c8s, tstart, tlen, n_used, h2, pos, p_rows):
    n, d = h2.shape
    return pl.pallas_call(
        _dispatch_body,
        out_shape=jax.ShapeDtypeStruct((p_rows, d // 2), u32),
        grid_spec=pltpu.PrefetchScalarGridSpec(
            num_scalar_prefetch=6,
            grid=(n // (TG * TS),),
            in_specs=[pl.BlockSpec((TG * TS, d), lambda i, *_: (i, 0)),
                      pl.BlockSpec((TOP_K, TG * TS), lambda i, *_: (0, i))],
            out_specs=pl.BlockSpec(memory_space=pl.ANY),
            scratch_shapes=[pltpu.VMEM((2 * TG, RS, d // 2), u32), pltpu.VMEM((BM, d // 2), u32),
                            pltpu.SemaphoreType.DMA((2 * TG,)), pltpu.SemaphoreType.DMA(())]),
        compiler_params=_params(("arbitrary",)),
        name="dispatch",
    )(toff, dst, c8s, tstart, tlen, n_used, h2, pos)


def _expert_body(be_ref, nxt_ref, nu_ref, rows_ref, x_ref, wgu_hbm, bgu_ref, wd_hbm, bd_ref, o_ref,
                 wgu_st, wd_st, wgu_bf, wd_bf, sems):
    del nu_ref
    b = pl.program_id(0)
    d = D_MODEL
    e = be_ref[b]

    def weight_copies(ex):
        return (pltpu.make_async_copy(wgu_hbm.at[ex], wgu_st, sems.at[0]),
                pltpu.make_async_copy(wd_hbm.at[ex], wd_st, sems.at[1]))

    @pl.when(b == 0)
    def _():
        for c in weight_copies(e):
            c.start()

    @pl.when(jnp.logical_or(b == 0, e != be_ref[jnp.maximum(b - 1, 0)]))
    def _():
        for c in weight_copies(e):
            c.wait()
        def cast_rows(c, carry):
            rows = pl.ds(pl.multiple_of(c * CAST_ROWS, CAST_ROWS), CAST_ROWS)
            wgu_bf[rows, :] = wgu_st[rows, :].astype(bf16)
            wd_bf[rows, :] = wd_st[rows, :].astype(bf16)
            return carry
        lax.fori_loop(0, d // CAST_ROWS, cast_rows, 0)
        nxt = nxt_ref[e]

        @pl.when(nxt >= 0)
        def _():
            for c in weight_copies(nxt):
                c.start()

    def mlp_rows(m):
        xb = jnp.concatenate(_unpack_halves(x_ref[:m, :]), axis=1).astype(bf16)
        gu = jnp.dot(xb, wgu_bf[...], preferred_element_type=f32) + bgu_ref[0]
        glu = jnp.minimum(gu[:, :d], SWIGLU_LIMIT)
        lin = jnp.clip(gu[:, d:], -SWIGLU_LIMIT, SWIGLU_LIMIT)
        act = glu * _sigmoid(SWIGLU_ALPHA * glu) * (lin + 1.0)
        y = jnp.dot(act.astype(bf16), wd_bf[...], preferred_element_type=f32) + bd_ref[0]
        o_ref[:m, :] = _pack_halves(y, exact=False)
        if m < BM:
            o_ref[m:, :] = jnp.zeros((BM - m, o_ref.shape[1]), o_ref.dtype)

    valid = rows_ref[b]
    for lo, hi in zip((0,) + BM_STEPS[:-1], BM_STEPS):
        @pl.when(jnp.logical_and(valid > lo, valid <= hi))
        def _(hi=hi):
            mlp_rows(hi)


def _experts(blk_e, nxt_e, n_used, blk_rows, xb, wgu, bgu, wd, bd):
    rows = pl.BlockSpec((BM, xb.shape[1]), lambda b, be, nx, nu, br: (b, 0))
    per_e = lambda a: pl.BlockSpec((1,) + a.shape[1:], lambda b, be, nx, nu, br: (be[b], 0, 0))
    hbm = pl.BlockSpec(memory_space=pl.ANY)
    return pl.pallas_call(
        _expert_body,
        out_shape=jax.ShapeDtypeStruct(xb.shape, xb.dtype),
        grid_spec=pltpu.PrefetchScalarGridSpec(
            num_scalar_prefetch=4,
            grid=(n_used[0],),
            in_specs=[rows, hbm, per_e(bgu), hbm, per_e(bd)],
            out_specs=rows,
            scratch_shapes=[pltpu.VMEM(wgu.shape[1:], f32), pltpu.VMEM(wd.shape[1:], f32),
                            pltpu.VMEM(wgu.shape[1:], bf16), pltpu.VMEM(wd.shape[1:], bf16),
                            pltpu.SemaphoreType.DMA((2,))]),
        input_output_aliases={4: 0},
        compiler_params=_params(("arbitrary",)),
        name="experts",
    )(blk_e, nxt_e, n_used, blk_rows, xb, wgu, bgu, wd, bd)


def _combine_body(toff_ref, dst_ref, c8_ref, yb_ref, pos_ref, gt_ref, x1_ref, mod_ref, g_ref,
                  o_ref, sbuf, sems):
    i = pl.program_id(0)
    base = lax.rem(i, 2) * TG

    def fetch(step, first_slot):
        for j in range(TG):
            def make(so, do, n, sl=first_slot + j):
                return pltpu.make_async_copy(yb_ref.at[pl.ds(do, n)], sbuf.at[sl, pl.ds(so, n)],
                                             sems.at[sl])
            _start_runs(step * TG + j, toff_ref, dst_ref, c8_ref, make)

    @pl.when(i == 0)
    def _():
        sbuf[...] = jnp.zeros_like(sbuf)
        fetch(i, base)

    @pl.when(i + 1 < pl.num_programs(0))
    def _():
        fetch(i + 1, TG - base)

    r = lax.broadcasted_iota(i32, (RS, TS), 0)
    pms, grows = [], []
    for j in range(TG):
        p = [pos_ref[k:k + 1, j * TS:(j + 1) * TS] for k in range(TOP_K)]
        gt = gt_ref[:, j * TS:(j + 1) * TS]
        gm = jnp.where(r == p[0], gt[0:1, :], jnp.where(r == p[1], gt[1:2, :],
             jnp.where(r == p[2], gt[2:3, :], jnp.where(r == p[3], gt[3:4, :], 0.0))))
        pms.append(jnp.where(gm != 0.0, 1.0, 0.0).astype(bf16))
        grows.append(jnp.sum(gm, axis=1, keepdims=True))

    for j in range(TG):
        n = _tile_rows(i * TG + j, toff_ref, c8_ref)
        pltpu.make_async_copy(yb_ref.at[pl.ds(0, n)], sbuf.at[base + j, pl.ds(0, n)],
                              sems.at[base + j]).wait()
    for j in range(TG):
        halves = [lax.dot_general(pms[j], (h * grows[j]).astype(bf16), (((0,), (0,)), ((), ())),
                                  preferred_element_type=f32)
                  for h in _unpack_halves(sbuf[base + j])]
        y = jnp.concatenate(halves, axis=1)
        rows = slice(j * TS, (j + 1) * TS)
        o_ref[rows, :] = x1_ref[rows, :] + mod_ref[0, 5:6, :] * _rms(y, g_ref[...])


def _combine(toff, dst, c8s, yb, pos, gt, x1, mod3, g, tiles_per_batch):
    n, d = x1.shape
    tok = pl.BlockSpec((TG * TS, d), lambda i, *_: (i, 0))
    lane = pl.BlockSpec((TOP_K, TG * TS), lambda i, *_: (0, i))
    steps_per_batch = tiles_per_batch // TG
    return pl.pallas_call(
        _combine_body,
        out_shape=jax.ShapeDtypeStruct((n, d), f32),
        grid_spec=pltpu.PrefetchScalarGridSpec(
            num_scalar_prefetch=3,
            grid=(n // (TG * TS),),
            in_specs=[pl.BlockSpec(memory_space=pl.ANY), lane, lane, tok,
                      pl.BlockSpec((1, 6, d), lambda i, *_: (i // steps_per_batch, 0, 0)),
                      pl.BlockSpec(g.shape, lambda i, *_: (0, 0))],
            out_specs=tok,
            scratch_shapes=[pltpu.VMEM((2 * TG, RS, d // 2), u32), pltpu.SemaphoreType.DMA((2 * TG,))]),
        compiler_params=_params(("arbitrary",)),
        name="combine",
    )(toff, dst, c8s, yb, pos, gt, x1, mod3, g)


def _layout_tables(tab, nt, p_rows):
    c8 = tab[:, :nt].T.astype(i32)
    toff = jnp.cumsum(c8, axis=1) - c8
    len8 = jnp.sum(c8, axis=0)
    seg = (len8 + BM - 1) // BM * BM
    gend = jnp.cumsum(seg)
    gstart = gend - seg
    dst = jnp.cumsum(c8, axis=0) - c8 + gstart[None, :]
    n_used = gend[-1] // BM
    blk = jnp.arange(p_rows // BM, dtype=i32)
    last = jnp.minimum(blk, n_used - 1)
    blk_e = jnp.sum((gend[None, :] <= (last * BM)[:, None]).astype(i32), axis=1)
    blk_e = jnp.minimum(blk_e, N_EXPERTS - 1)
    lo = jnp.maximum(gstart[None, :], (blk * BM)[:, None])
    hi = jnp.minimum((gstart + len8)[None, :], ((blk + 1) * BM)[:, None])
    blk_rows = jnp.sum(jnp.maximum(hi - lo, 0), axis=1).astype(i32)
    ids = jnp.arange(N_EXPERTS, dtype=i32)
    later = jnp.logical_and(ids[None, :] > ids[:, None], (seg > 0)[None, :])
    nxt_e = jnp.min(jnp.where(later, ids[None, :], N_EXPERTS), axis=1)
    nxt_e = jnp.where(nxt_e == N_EXPERTS, -1, nxt_e).astype(i32)
    return (toff.reshape(-1), dst.reshape(-1), c8.reshape(-1), gstart + len8, seg - len8,
            blk_e, nxt_e, n_used.reshape(1).astype(i32), blk_rows)


def kernel(x, c, w_ada, b_ada, g_pre_mix, g_post_mix, w_in, rel_bias, sgu_ln_g, sgu_ln_b,
           w_spatial, b_spatial, w_branch_a, w_branch_b, w_gate, b_gate, w_out,
           g_pre_ffn, g_post_ffn, w_router, b_router, w_gate_up, b_gate_up, w_down, b_down):
    b, s, d = x.shape
    assert d == D_MODEL and s % max(TM, TMP, TMX) == 0 and s % (TG * TS) == 0
    n = b * s
    nt = n // TS
    ntp = -(-nt // LANES) * LANES
    p_rows = -(-(n * TOP_K + nt * N_EXPERTS * (ROW_ALIGN - 1) + N_EXPERTS * (BM - 1)) // BM) * BM
    depth = w_ada.shape[0]
    c8 = jnp.pad(c, ((0, 8 - b), (0, 0)))
    row = lambda a: a.reshape(1, -1)

    for l in range(depth):
        mod = _ada(c8, w_ada[l], row(b_ada[l]))[:b]
        mod3 = mod.reshape(b, 6, d)

        q, k, v, u, vv = _proj(x, mod3, row(g_pre_mix[l]), w_in[l].astype(bf16),
                               row(sgu_ln_g[l]), row(sgu_ln_b[l]))
        ya = _attn(q, k, v, _attn_bias_vec(rel_bias[l]))
        ws2 = w_spatial[l].astype(bf16).reshape(-1, 2 * SGU_BLOCK, SGU_BLOCK)
        bsf = jnp.repeat(b_spatial[l].T, SGU_WIDTH // b_spatial.shape[1], axis=1)
        x1, h2, pos, gt, tab = _mix(x, mod3, row(g_pre_mix[l]), row(g_post_mix[l]), u, vv, ya, ws2, bsf,
                                   w_branch_a[l].astype(bf16), w_branch_b[l].astype(bf16),
                                   w_gate[l].astype(bf16), row(b_gate[l]), w_out[l].astype(bf16),
                                   row(g_pre_ffn[l]), w_router[l].T.astype(bf16),
                                   b_router[l].reshape(-1, 1), ntp)
        x1f = x1.reshape(n, d)
        h2 = h2.reshape(n, d)
        toff, dst, c8s, tstart, tlen, blk_e, nxt_e, n_used, blk_rows = _layout_tables(tab, nt, p_rows)
        xb = _dispatch(toff, dst, c8s, tstart, tlen, n_used, h2, pos, p_rows)
        yb = _experts(blk_e, nxt_e, n_used, blk_rows, xb, w_gate_up[l], b_gate_up[l][:, None, :],
                      w_down[l], b_down[l][:, None, :])
        x = _combine(toff, dst, c8s, yb, pos, gt, x1f, mod3, row(g_post_ffn[l]), s // TS).reshape(b, s, d)
    return x
```

```python
import jax
import jax.numpy as jnp
from jax import lax
from jax.experimental import pallas as pl
from jax.experimental.pallas import tpu as pltpu

bf16 = jnp.bfloat16
f32 = jnp.float32
i32 = jnp.int32
u32 = jnp.uint32

D_MODEL = 1024
CHUNK = 64
N_LEFT = 8
ATT_HEADS = 8
HEAD_DIM = 64
ATT_WIDTH = 512
MAX_REL = 128
SGU_BLOCK = 128
SGU_WIDTH = 512
N_EXPERTS = 32
TOP_K = 4
SWIGLU_LIMIT = 7.0
SWIGLU_ALPHA = 1.702
EPS = 1e-6
NEG = -1e30
LOG2E = 1.4426950408889634

LANES = 128
ROW_ALIGN = 8
TM = 512
TMP = 1024
TMX = 1024
QCH = 4
QG = QCH * CHUNK
KBAND = (N_LEFT + QCH) * CHUNK
BIAS_PERIOD = 1024
TS = 256
TG = 4
RS = TS * TOP_K + N_EXPERTS * ROW_ALIGN
BM = 1024
BM_STEPS = tuple(range(LANES, BM + 1, LANES))
CAST_ROWS = 64
VMEM_LIMIT = 56 * 2**20


def _params(sem, cast_inputs=None):
    return pltpu.CompilerParams(dimension_semantics=sem, vmem_limit_bytes=VMEM_LIMIT,
                                allow_input_fusion=cast_inputs)


def _adaln(x, g, sc, sh):
    ms = jnp.mean(x * x, axis=-1, keepdims=True)
    return (x * lax.rsqrt(ms + EPS) * g) * (1.0 + sc) + sh


def _rms(x, g):
    ms = jnp.mean(x * x, axis=-1, keepdims=True)
    return x * lax.rsqrt(ms + EPS) * g


def _sigmoid(x):
    return 0.5 * jnp.tanh(0.5 * x) + 0.5


def _ada_body(c_ref, w_ref, b_ref, o_ref):
    c = c_ref[...]
    ca = c * _sigmoid(c)
    o_ref[...] = jnp.dot(ca.astype(bf16), w_ref[...].astype(bf16),
                         preferred_element_type=f32) + b_ref[...]


def _ada(c8, w, b):
    d = w.shape[0]
    n = w.shape[1] // d
    return pl.pallas_call(
        _ada_body,
        out_shape=jax.ShapeDtypeStruct((8, n * d), f32),
        grid=(n,),
        in_specs=[pl.BlockSpec((8, d), lambda j: (0, 0)),
                  pl.BlockSpec((d, d), lambda j: (0, j)),
                  pl.BlockSpec((1, d), lambda j: (0, j))],
        out_specs=pl.BlockSpec((8, d), lambda j: (0, j)),
        compiler_params=_params(("arbitrary",)),
        name="ada",
    )(c8, w, b)


def _proj_body(x_ref, mod_ref, g_ref, w_ref, lng_ref, lnb_ref,
               q_ref, k_ref, v_ref, u_ref, vv_ref):
    h = _adaln(x_ref[0], g_ref[...], mod_ref[0, 1:2, :], mod_ref[0, 0:1, :]).astype(bf16)
    aw = ATT_WIDTH
    z = jnp.dot(h, w_ref[:, 3 * aw:], preferred_element_type=f32)
    p = jnp.dot(h, w_ref[:, :3 * aw], preferred_element_type=f32)
    q_ref[0] = (p[:, 0:aw] * (HEAD_DIM ** -0.5 * LOG2E)).astype(bf16)
    k_ref[0] = p[:, aw:2 * aw].astype(bf16)
    v_ref[0] = p[:, 2 * aw:3 * aw].astype(bf16)
    zg = 0.5 * z * (1.0 + lax.erf(z * (2.0 ** -0.5)))
    u_ref[0] = zg[:, :SGU_WIDTH].astype(bf16)
    vv = zg[:, SGU_WIDTH:]
    mu = jnp.mean(vv, axis=-1, keepdims=True)
    var = jnp.mean(jnp.square(vv - mu), axis=-1, keepdims=True)
    vn = (vv - mu) * lax.rsqrt(var + EPS) * lng_ref[...] + lnb_ref[...]
    vv_ref[0] = vn.astype(bf16)


def _proj(x, mod3, g, w_in, lng, lnb):
    b, s, d = x.shape
    tok = lambda w: pl.BlockSpec((1, TMP, w), lambda bi, i: (bi, i, 0))
    full = lambda a: pl.BlockSpec(a.shape, lambda bi, i: (0,) * a.ndim)
    o512 = jax.ShapeDtypeStruct((b, s, ATT_WIDTH), bf16)
    return pl.pallas_call(
        _proj_body,
        out_shape=(o512,) * 5,
        grid=(b, s // TMP),
        in_specs=[tok(d), pl.BlockSpec((1, 6, d), lambda bi, i: (bi, 0, 0)),
                  full(g), full(w_in), full(lng), full(lnb)],
        out_specs=(tok(ATT_WIDTH),) * 5,
        compiler_params=_params(("parallel", "arbitrary"), [a is w_in for a in (x, mod3, g, w_in, lng, lnb)]),
        name="proj",
    )(x, mod3, g, w_in, lng, lnb)


def _attn_body(q_ref, kp_ref, kc_ref, vp_ref, vc_ref, bvec_ref, o_ref, bias_ref):
    first = pl.program_id(1) == 0
    nhp = ATT_HEADS // 2

    @pl.when(first)
    def _():
        i = lax.broadcasted_iota(i32, (QG, KBAND), 0)
        j = lax.broadcasted_iota(i32, (QG, KBAND), 1)
        jb = j - (i // CHUNK) * CHUNK
        in_band = jnp.logical_and(jb >= 0, jb < CHUNK * (N_LEFT + 1))
        for h in range(ATT_HEADS):
            rows = jnp.broadcast_to(bvec_ref[h:h + 1, :], (QG, BIAS_PERIOD))
            toep = pltpu.roll(rows, 0, 1, stride=1, stride_axis=0)[:, :KBAND]
            bias_ref[h // 2, (h % 2) * QG:(h % 2 + 1) * QG, :] = jnp.where(in_band, toep, NEG)

    lo = lax.broadcasted_iota(i32, (QG, LANES), 1) < HEAD_DIM
    has_prev = jnp.logical_not(first)
    for p in range(TM // QG):
        r0 = p * QG
        n_cur = KBAND - (TM - r0)
        ind = jnp.concatenate([jnp.where(has_prev, 1.0, 0.0) * jnp.ones((TM - r0, LANES), f32),
                               jnp.ones((n_cur, LANES), f32)], axis=0).astype(bf16)
        for hp in range(nhp):
            c0 = hp * LANES
            qp = q_ref[0, r0:r0 + QG, c0:c0 + LANES]
            zero = jnp.zeros_like(qp)
            q2 = jnp.concatenate([jnp.where(lo, qp, zero), jnp.where(lo, zero, qp)], axis=0)
            kprev = kp_ref[0, r0:TM, c0:c0 + LANES]
            vprev = vp_ref[0, r0:TM, c0:c0 + LANES]
            kprev = jnp.where(has_prev, kprev, jnp.zeros_like(kprev))
            vprev = jnp.where(has_prev, vprev, jnp.zeros_like(vprev))
            kb = jnp.concatenate([kprev, kc_ref[0, 0:n_cur, c0:c0 + LANES]], axis=0)
            vb = jnp.concatenate([vprev, vc_ref[0, 0:n_cur, c0:c0 + LANES]], axis=0)
            s = lax.dot_general(q2, kb, (((1,), (1,)), ((), ())), preferred_element_type=f32)
            sb = (s + bias_ref[hp]).astype(bf16)
            m = jnp.max(sb, axis=-1, keepdims=True)
            e = jnp.exp2(sb - m)
            o2 = jnp.dot(e, jnp.concatenate([vb, ind], axis=1),
                         preferred_element_type=f32)
            on = o2[:, :LANES] / o2[:, LANES:]
            o = jnp.where(lo, on[:QG], on[QG:])
            o_ref[0, r0:r0 + QG, c0:c0 + LANES] = o.astype(bf16)


def _attn(q, k, v, bvec):
    b, s, w = q.shape
    cur = pl.BlockSpec((1, TM, w), lambda bi, i: (bi, i, 0))
    prev = pl.BlockSpec((1, TM, w), lambda bi, i: (bi, jnp.maximum(i - 1, 0), 0))
    return pl.pallas_call(
        _attn_body,
        out_shape=jax.ShapeDtypeStruct((b, s, w), bf16),
        grid=(b, s // TM),
        in_specs=[cur, prev, cur, prev, cur,
                  pl.BlockSpec(bvec.shape, lambda bi, i: (0, 0))],
        out_specs=cur,
        scratch_shapes=[pltpu.VMEM((ATT_HEADS // 2, 2 * QG, KBAND), f32)],
        compiler_params=_params(("parallel", "arbitrary")),
        name="attn",
    )(q, k, k, v, v, bvec)


def _attn_bias_vec(rel_bias):
    h = rel_bias.shape[0]
    n_far = N_LEFT * CHUNK - MAX_REL
    assert BIAS_PERIOD >= QG + KBAND - 1 and n_far >= 0
    far = jnp.broadcast_to(rel_bias[:, 2 * MAX_REL:], (h, BIAS_PERIOD))
    near = rel_bias[:, :0:-1]
    v = jnp.concatenate([far[:, :n_far], near, far[:, n_far + 2 * MAX_REL:]], axis=1)
    return v.astype(f32) * LOG2E


def _route(hb, wr_ref, br_ref):
    lg = lax.dot_general(wr_ref[...], hb, (((1,), (1,)), ((), ())),
                         preferred_element_type=f32) + br_ref[...]
    e_iota = lax.broadcasted_iota(i32, lg.shape, 0)
    vals, idxs = [], []
    hits = jnp.zeros(lg.shape, f32)
    for _ in range(TOP_K):
        m = jnp.max(lg, axis=0, keepdims=True)
        idx = jnp.min(jnp.where(lg == m, e_iota, N_EXPERTS), axis=0, keepdims=True)
        hit = e_iota == idx
        hits = hits + jnp.where(hit, 1.0, 0.0)
        lg = jnp.where(hit, -jnp.inf, lg)
        vals.append(m)
        idxs.append(idx)
    ex = [jnp.exp(v - vals[0]) for v in vals]
    den = ex[0] + ex[1] + ex[2] + ex[3]
    return (jnp.concatenate(idxs, axis=0), jnp.concatenate([e / den for e in ex], axis=0), hits)


def _mix_body(x_ref, mod_ref, gpre_ref, gpost_ref, u_ref, vv_ref, ya_ref, ws_ref, bs_ref,
              wa_ref, wb_ref, wg_ref, bg_ref, wo_ref, gffn_ref, wr_ref, br_ref,
              o_ref, h2_ref, pos_ref, gt_ref, tab_ref, ybuf):
    d = D_MODEL
    x = x_ref[0]
    h = _adaln(x, gpre_ref[...], mod_ref[0, 1:2, :], mod_ref[0, 0:1, :]).astype(bf16)
    gs = _sigmoid(jnp.dot(h, wg_ref[...], preferred_element_type=f32) + bg_ref[...])

    blk = SGU_BLOCK
    row = lax.broadcasted_iota(i32, (2 * blk, blk), 0)
    colv = lax.broadcasted_iota(i32, (2 * blk, blk), 1)
    causal = colv <= jnp.bitwise_and(row, blk - 1)
    lo = lax.broadcasted_iota(i32, (blk, LANES), 1) < (LANES // 2)
    for gp in range(SGU_WIDTH // LANES):
        c0 = gp * LANES
        w2 = ws_ref[gp]
        w2 = jnp.where(causal, w2, jnp.zeros_like(w2))
        for bi in range(TMX // blk):
            r0 = bi * blk
            s2 = jnp.dot(w2, vv_ref[0, r0:r0 + blk, c0:c0 + LANES], preferred_element_type=f32)
            s = jnp.where(lo, s2[:blk], s2[blk:]) + bs_ref[:, c0:c0 + LANES]
            yb = u_ref[0, r0:r0 + blk, c0:c0 + LANES].astype(f32) * s
            ybuf[r0:r0 + blk, c0:c0 + LANES] = yb.astype(bf16)

    a = jnp.dot(ya_ref[0], wa_ref[...], preferred_element_type=f32)
    bb = jnp.dot(ybuf[...], wb_ref[...], preferred_element_type=f32)
    merged = gs[:, :d] * a + gs[:, d:] * bb
    y = jnp.dot(merged.astype(bf16), wo_ref[...], preferred_element_type=f32)
    x1 = x + mod_ref[0, 2:3, :] * _rms(y, gpost_ref[...])
    o_ref[0] = x1

    hb = _adaln(x1, gffn_ref[...], mod_ref[0, 4:5, :], mod_ref[0, 3:4, :]).astype(bf16)
    h2_ref[0] = hb
    ids, gates, hits = _route(hb, wr_ref, br_ref)
    pos_ref[...] = jnp.concatenate(
        [jnp.concatenate(_sorted_positions(ids[:, j * TS:(j + 1) * TS]), axis=0)
         for j in range(TMX // TS)], axis=1)
    gt_ref[...] = gates

    step = pl.program_id(0) * pl.num_programs(1) + pl.program_id(1)

    @pl.when(step == 0)
    def _():
        tab_ref[...] = jnp.zeros_like(tab_ref)

    lane = lax.broadcasted_iota(i32, tab_ref.shape, 1)
    acc = tab_ref[...]
    for j in range(TMX // TS):
        cnt = jnp.sum(hits[:, j * TS:(j + 1) * TS], axis=1, keepdims=True)
        c8 = jnp.floor((cnt + (ROW_ALIGN - 1.0)) * (1.0 / ROW_ALIGN)) * ROW_ALIGN
        acc = acc + jnp.where(lane == step * (TMX // TS) + j, c8, 0.0)
    tab_ref[...] = acc


def _mix(x, mod3, gpre, gpost, u, vv, ya, ws2, bsf, wa, wb, wg, bg, wo, gffn, wrt, br, ntp):
    b, s, d = x.shape
    n = b * s
    tok = lambda w: pl.BlockSpec((1, TMX, w), lambda bi, i: (bi, i, 0))
    full = lambda a: pl.BlockSpec(a.shape, lambda bi, i: (0,) * a.ndim)
    lane = pl.BlockSpec((TOP_K, TMX), lambda bi, i: (0, bi * (s // TMX) + i))
    args = (x, mod3, gpre, gpost, u, vv, ya, ws2, bsf, wa, wb, wg, bg, wo, gffn, wrt, br)
    cast = (ws2, wa, wb, wg, wo, wrt)
    return pl.pallas_call(
        _mix_body,
        out_shape=(jax.ShapeDtypeStruct((b, s, d), f32),
                   jax.ShapeDtypeStruct((b, s, d), bf16),
                   jax.ShapeDtypeStruct((TOP_K, n), i32),
                   jax.ShapeDtypeStruct((TOP_K, n), f32),
                   jax.ShapeDtypeStruct((N_EXPERTS, ntp), f32)),
        grid=(b, s // TMX),
        in_specs=[tok(d), pl.BlockSpec((1, 6, d), lambda bi, i: (bi, 0, 0)),
                  full(gpre), full(gpost), tok(SGU_WIDTH), tok(SGU_WIDTH), tok(ATT_WIDTH),
                  full(ws2), full(bsf), full(wa), full(wb), full(wg), full(bg), full(wo),
                  full(gffn), full(wrt), full(br)],
        out_specs=(tok(d), tok(d), lane, lane,
                   pl.BlockSpec((N_EXPERTS, ntp), lambda bi, i: (0, 0))),
        scratch_shapes=[pltpu.VMEM((TMX, SGU_WIDTH), bf16)],
        compiler_params=_params(("arbitrary", "arbitrary"), [any(a is w for w in cast) for a in args]),
        name="mix",
    )(*args)


def _sorted_positions(ti):
    ts = ti.shape[1]
    e_iota = lax.broadcasted_iota(i32, (N_EXPERTS, ts), 0)
    upper = (lax.broadcasted_iota(i32, (ts, ts), 0) < lax.broadcasted_iota(i32, (ts, ts), 1))
    upper = jnp.where(upper, 1.0, 0.0).astype(bf16)
    hits, prefs, cnts = [], [], []
    for k in range(TOP_K):
        hit = ti[k:k + 1, :] == e_iota
        hf = jnp.where(hit, 1.0, 0.0)
        prefs.append(jnp.dot(hf.astype(bf16), upper, preferred_element_type=f32))
        cnts.append(jnp.sum(hf, axis=1, keepdims=True))
        hits.append(hit)
    total = cnts[0] + cnts[1] + cnts[2] + cnts[3]
    c8 = jnp.floor((total + (ROW_ALIGN - 1.0)) * (1.0 / ROW_ALIGN)) * ROW_ALIGN
    lower = (lax.broadcasted_iota(i32, (N_EXPERTS, N_EXPERTS), 1)
             < lax.broadcasted_iota(i32, (N_EXPERTS, N_EXPERTS), 0))
    lower = jnp.where(lower, 1.0, 0.0).astype(bf16)
    c8b = jnp.broadcast_to(c8, (N_EXPERTS, LANES)).astype(bf16)
    start = jnp.dot(lower, c8b, preferred_element_type=f32)[:, 0:1]
    pos = []
    for k in range(TOP_K):
        pe = start + prefs[k]
        pos.append(jnp.sum(jnp.where(hits[k], pe, 0.0), axis=0, keepdims=True).astype(i32))
        start = start + cnts[k]
    return pos


def _pack_halves(x, exact):
    half = x.shape[1] // 2
    lo = lax.bitcast_convert_type(x[:, :half], u32)
    hi = lax.bitcast_convert_type(x[:, half:], u32)
    if not exact:
        hi = jnp.bitwise_and(hi, jnp.uint32(0xFFFF0000))
    return jnp.bitwise_or(lax.shift_right_logical(lo, jnp.uint32(16)), hi)


def _unpack_halves(w):
    lo = lax.bitcast_convert_type(lax.shift_left(w, jnp.uint32(16)), f32)
    hi = lax.bitcast_convert_type(jnp.bitwise_and(w, jnp.uint32(0xFFFF0000)), f32)
    return lo, hi


def _start_runs(i, toff_ref, dst_ref, c8_ref, make):
    unroll = 4

    def some(q, carry):
        for k in range(unroll):
            j = i * N_EXPERTS + q * unroll + k
            n = pl.multiple_of(c8_ref[j], ROW_ALIGN)
            so = pl.multiple_of(toff_ref[j], ROW_ALIGN)
            do = pl.multiple_of(dst_ref[j], ROW_ALIGN)

            @pl.when(n > 0)
            def _():
                make(so, do, n).start(priority=k % 2)
        return carry
    lax.fori_loop(0, N_EXPERTS // unroll, some, 0)


def _tile_rows(i, toff_ref, c8_ref):
    j = i * N_EXPERTS + (N_EXPERTS - 1)
    return pl.multiple_of(toff_ref[j] + c8_ref[j], ROW_ALIGN)


def _dispatch_body(toff_ref, dst_ref, c8_ref, tstart_ref, tlen_ref, nu_ref,
                   h2_ref, pos_ref, xb_ref, sbuf, zbuf, sems, sem):
    i = pl.program_id(0)
    base = lax.rem(i, 2) * TG
    r = lax.broadcasted_iota(i32, (RS, TS), 0)
    for j in range(TG):
        p = [pos_ref[k:k + 1, j * TS:(j + 1) * TS] for k in range(TOP_K)]
        pm = jnp.where(r == p[0], 1.0, jnp.where(r == p[1], 1.0,
             jnp.where(r == p[2], 1.0, jnp.where(r == p[3], 1.0, 0.0))))
        srt = jnp.dot(pm.astype(bf16), h2_ref[j * TS:(j + 1) * TS, :], preferred_element_type=f32)
        sbuf[base + j] = _pack_halves(srt, exact=True)

    for j in range(TG):
        def make(so, do, n, sl=base + j):
            return pltpu.make_async_copy(sbuf.at[sl, pl.ds(so, n)], xb_ref.at[pl.ds(do, n)],
                                         sems.at[sl])
        _start_runs(i * TG + j, toff_ref, dst_ref, c8_ref, make)

    def wait_tile(t, sl):
        n = _tile_rows(t, toff_ref, c8_ref)
        pltpu.make_async_copy(sbuf.at[sl, pl.ds(0, n)], xb_ref.at[pl.ds(0, n)], sems.at[sl]).wait()

    @pl.when(i > 0)
    def _():
        for j in range(TG):
            wait_tile((i - 1) * TG + j, TG - base + j)

    @pl.when(i == pl.num_programs(0) - 1)
    def _():
        for j in range(TG):
            wait_tile(i * TG + j, base + j)
        zbuf[...] = jnp.zeros_like(zbuf)

        def fill(action):
            def tail(e, carry):
                n = pl.multiple_of(tlen_ref[e], ROW_ALIGN)
                do = pl.multiple_of(tstart_ref[e], ROW_ALIGN)

                @pl.when(n > 0)
                def _():
                    action(pltpu.make_async_copy(zbuf.at[pl.ds(0, n)], xb_ref.at[pl.ds(do, n)], sem))
                return carry
            lax.fori_loop(0, N_EXPERTS, tail, 0)

            def unused(b, carry):
                do = pl.multiple_of(b * BM, BM)
                action(pltpu.make_async_copy(zbuf, xb_ref.at[pl.ds(do, BM)], sem))
                return carry
            lax.fori_loop(nu_ref[0], xb_ref.shape[0] // BM, unused, 0)
        fill(lambda c: c.start())
        fill(lambda c: c.wait())


def _dispatch(toff, dst, c8s, tstart, tlen, n_used, h2, pos, p_rows):
    n, d = h2.shape
    return pl.pallas_call(
        _dispatch_body,
        out_shape=jax.ShapeDtypeStruct((p_rows, d // 2), u32),
        grid_spec=pltpu.PrefetchScalarGridSpec(
            num_scalar_prefetch=6,
            grid=(n // (TG * TS),),
            in_specs=[pl.BlockSpec((TG * TS, d), lambda i, *_: (i, 0)),
                      pl.BlockSpec((TOP_K, TG * TS), lambda i, *_: (0, i))],
            out_specs=pl.BlockSpec(memory_space=pl.ANY),
            scratch_shapes=[pltpu.VMEM((2 * TG, RS, d // 2), u32), pltpu.VMEM((BM, d // 2), u32),
                            pltpu.SemaphoreType.DMA((2 * TG,)), pltpu.SemaphoreType.DMA(())]),
        compiler_params=_params(("arbitrary",)),
        name="dispatch",
    )(toff, dst, c8s, tstart, tlen, n_used, h2, pos)


def _expert_body(be_ref, nxt_ref, nu_ref, rows_ref, x_ref, wgu_hbm, bgu_ref, wd_hbm, bd_ref, o_ref,
                 wgu_st, wd_st, wgu_bf, wd_bf, sems):
    del nu_ref
    b = pl.program_id(0)
    d = D_MODEL
    e = be_ref[b]

    def weight_copies(ex):
        return (pltpu.make_async_copy(wgu_hbm.at[ex], wgu_st, sems.at[0]),
                pltpu.make_async_copy(wd_hbm.at[ex], wd_st, sems.at[1]))

    @pl.when(b == 0)
    def _():
        for c in weight_copies(e):
            c.start()

    @pl.when(jnp.logical_or(b == 0, e != be_ref[jnp.maximum(b - 1, 0)]))
    def _():
        for c in weight_copies(e):
            c.wait()
        def cast_rows(c, carry):
            rows = pl.ds(pl.multiple_of(c * CAST_ROWS, CAST_ROWS), CAST_ROWS)
            wgu_bf[rows, :] = wgu_st[rows, :].astype(bf16)
            wd_bf[rows, :] = wd_st[rows, :].astype(bf16)
            return carry
        lax.fori_loop(0, d // CAST_ROWS, cast_rows, 0)
        nxt = nxt_ref[e]

        @pl.when(nxt >= 0)
        def _():
            for c in weight_copies(nxt):
                c.start()

    def mlp_rows(m):
        xb = jnp.concatenate(_unpack_halves(x_ref[:m, :]), axis=1).astype(bf16)
        gu = jnp.dot(xb, wgu_bf[...], preferred_element_type=f32) + bgu_ref[0]
        glu = jnp.minimum(gu[:, :d], SWIGLU_LIMIT)
        lin = jnp.clip(gu[:, d:], -SWIGLU_LIMIT, SWIGLU_LIMIT)
        act = glu * _sigmoid(SWIGLU_ALPHA * glu) * (lin + 1.0)
        y = jnp.dot(act.astype(bf16), wd_bf[...], preferred_element_type=f32) + bd_ref[0]
        o_ref[:m, :] = _pack_halves(y, exact=False)
        if m < BM:
            o_ref[m:, :] = jnp.zeros((BM - m, o_ref.shape[1]), o_ref.dtype)

    valid = rows_ref[b]
    for lo, hi in zip((0,) + BM_STEPS[:-1], BM_STEPS):
        @pl.when(jnp.logical_and(valid > lo, valid <= hi))
        def _(hi=hi):
            mlp_rows(hi)


def _experts(blk_e, nxt_e, n_used, blk_rows, xb, wgu, bgu, wd, bd):
    rows = pl.BlockSpec((BM, xb.shape[1]), lambda b, be, nx, nu, br: (b, 0))
    per_e = lambda a: pl.BlockSpec((1,) + a.shape[1:], lambda b, be, nx, nu, br: (be[b], 0, 0))
    hbm = pl.BlockSpec(memory_space=pl.ANY)
    return pl.pallas_call(
        _expert_body,
        out_shape=jax.ShapeDtypeStruct(xb.shape, xb.dtype),
        grid_spec=pltpu.PrefetchScalarGridSpec(
            num_scalar_prefetch=4,
            grid=(n_used[0],),
            in_specs=[rows, hbm, per_e(bgu), hbm, per_e(bd)],
            out_specs=rows,
            scratch_shapes=[pltpu.VMEM(wgu.shape[1:], f32), pltpu.VMEM(wd.shape[1:], f32),
                            pltpu.VMEM(wgu.shape[1:], bf16), pltpu.VMEM(wd.shape[1:], bf16),
                            pltpu.SemaphoreType.DMA((2,))]),
        input_output_aliases={4: 0},
        compiler_params=_params(("arbitrary",)),
        name="experts",
    )(blk_e, nxt_e, n_used, blk_rows, xb, wgu, bgu, wd, bd)


def _combine_body(toff_ref, dst_ref, c8_ref, yb_ref, pos_ref, gt_ref, x1_ref, mod_ref, g_ref,
                  o_ref, sbuf, sems):
    i = pl.program_id(0)
    base = lax.rem(i, 2) * TG

    def fetch(step, first_slot):
        for j in range(TG):
            def make(so, do, n, sl=first_slot + j):
                return pltpu.make_async_copy(yb_ref.at[pl.ds(do, n)], sbuf.at[sl, pl.ds(so, n)],
                                             sems.at[sl])
            _start_runs(step * TG + j, toff_ref, dst_ref, c8_ref, make)

    @pl.when(i == 0)
    def _():
        sbuf[...] = jnp.zeros_like(sbuf)
        fetch(i, base)

    @pl.when(i + 1 < pl.num_programs(0))
    def _():
        fetch(i + 1, TG - base)

    r = lax.broadcasted_iota(i32, (RS, TS), 0)
    pms, grows = [], []
    for j in range(TG):
        p = [pos_ref[k:k + 1, j * TS:(j + 1) * TS] for k in range(TOP_K)]
        gt = gt_ref[:, j * TS:(j + 1) * TS]
        gm = jnp.where(r == p[0], gt[0:1, :], jnp.where(r == p[1], gt[1:2, :],
             jnp.where(r == p[2], gt[2:3, :], jnp.where(r == p[3], gt[3:4, :], 0.0))))
        pms.append(jnp.where(gm != 0.0, 1.0, 0.0).astype(bf16))
        grows.append(jnp.sum(gm, axis=1, keepdims=True))

    for j in range(TG):
        n = _tile_rows(i * TG + j, toff_ref, c8_ref)
        pltpu.make_async_copy(yb_ref.at[pl.ds(0, n)], sbuf.at[base + j, pl.ds(0, n)],
                              sems.at[base + j]).wait()
    for j in range(TG):
        halves = [lax.dot_general(pms[j], (h * grows[j]).astype(bf16), (((0,), (0,)), ((), ())),
                                  preferred_element_type=f32)
                  for h in _unpack_halves(sbuf[base + j])]
        y = jnp.concatenate(halves, axis=1)
        rows = slice(j * TS, (j + 1) * TS)
        o_ref[rows, :] = x1_ref[rows, :] + mod_ref[0, 5:6, :] * _rms(y, g_ref[...])


def _combine(toff, dst, c8s, yb, pos, gt, x1, mod3, g, tiles_per_batch):
    n, d = x1.shape
    tok = pl.BlockSpec((TG * TS, d), lambda i, *_: (i, 0))
    lane = pl.BlockSpec((TOP_K, TG * TS), lambda i, *_: (0, i))
    steps_per_batch = tiles_per_batch // TG
    return pl.pallas_call(
        _combine_body,
        out_shape=jax.ShapeDtypeStruct((n, d), f32),
        grid_spec=pltpu.PrefetchScalarGridSpec(
            num_scalar_prefetch=3,
            grid=(n // (TG * TS),),
            in_specs=[pl.BlockSpec(memory_space=pl.ANY), lane, lane, tok,
                      pl.BlockSpec((1, 6, d), lambda i, *_: (i // steps_per_batch, 0, 0)),
                      pl.BlockSpec(g.shape, lambda i, *_: (0, 0))],
            out_specs=tok,
            scratch_shapes=[pltpu.VMEM((2 * TG, RS, d // 2), u32), pltpu.SemaphoreType.DMA((2 * TG,))]),
        compiler_params=_params(("arbitrary",)),
        name="combine",
    )(toff, dst, c8s, yb, pos, gt, x1, mod3, g)


def _layout_tables(tab, nt, p_rows):
    c8 = tab[:, :nt].T.astype(i32)
    toff = jnp.cumsum(c8, axis=1) - c8
    len8 = jnp.sum(c8, axis=0)
    seg = (len8 + BM - 1) // BM * BM
    gend = jnp.cumsum(seg)
    gstart = gend - seg
    dst = jnp.cumsum(c8, axis=0) - c8 + gstart[None, :]
    n_used = gend[-1] // BM
    blk = jnp.arange(p_rows // BM, dtype=i32)
    last = jnp.minimum(blk, n_used - 1)
    blk_e = jnp.sum((gend[None, :] <= (last * BM)[:, None]).astype(i32), axis=1)
    blk_e = jnp.minimum(blk_e, N_EXPERTS - 1)
    lo = jnp.maximum(gstart[None, :], (blk * BM)[:, None])
    hi = jnp.minimum((gstart + len8)[None, :], ((blk + 1) * BM)[:, None])
    blk_rows = jnp.sum(jnp.maximum(hi - lo, 0), axis=1).astype(i32)
    ids = jnp.arange(N_EXPERTS, dtype=i32)
    later = jnp.logical_and(ids[None, :] > ids[:, None], (seg > 0)[None, :])
    nxt_e = jnp.min(jnp.where(later, ids[None, :], N_EXPERTS), axis=1)
    nxt_e = jnp.where(nxt_e == N_EXPERTS, -1, nxt_e).astype(i32)
    return (toff.reshape(-1), dst.reshape(-1), c8.reshape(-1), gstart + len8, seg - len8,
            blk_e, nxt_e, n_used.reshape(1).astype(i32), blk_rows)


def kernel(x, c, w_ada, b_ada, g_pre_mix, g_post_mix, w_in, rel_bias, sgu_ln_g, sgu_ln_b,
           w_spatial, b_spatial, w_branch_a, w_branch_b, w_gate, b_gate, w_out,
           g_pre_ffn, g_post_ffn, w_router, b_router, w_gate_up, b_gate_up, w_down, b_down):
    b, s, d = x.shape
    assert d == D_MODEL and s % max(TM, TMP, TMX) == 0 and s % (TG * TS) == 0
    n = b * s
    nt = n // TS
    ntp = -(-nt // LANES) * LANES
    p_rows = -(-(n * TOP_K + nt * N_EXPERTS * (ROW_ALIGN - 1) + N_EXPERTS * (BM - 1)) // BM) * BM
    depth = w_ada.shape[0]
    c8 = jnp.pad(c, ((0, 8 - b), (0, 0)))
    row = lambda a: a.reshape(1, -1)

    for l in range(depth):
        mod = _ada(c8, w_ada[l], row(b_ada[l]))[:b]
        mod3 = mod.reshape(b, 6, d)

        q, k, v, u, vv = _proj(x, mod3, row(g_pre_mix[l]), w_in[l].astype(bf16),
                               row(sgu_ln_g[l]), row(sgu_ln_b[l]))
        ya = _attn(q, k, v, _attn_bias_vec(rel_bias[l]))
        ws2 = w_spatial[l].astype(bf16).reshape(-1, 2 * SGU_BLOCK, SGU_BLOCK)
        bsf = jnp.repeat(b_spatial[l].T, SGU_WIDTH // b_spatial.shape[1], axis=1)
        x1, h2, pos, gt, tab = _mix(x, mod3, row(g_pre_mix[l]), row(g_post_mix[l]), u, vv, ya, ws2, bsf,
                                   w_branch_a[l].astype(bf16), w_branch_b[l].astype(bf16),
                                   w_gate[l].astype(bf16), row(b_gate[l]), w_out[l].astype(bf16),
                                   row(g_pre_ffn[l]), w_router[l].T.astype(bf16),
                                   b_router[l].reshape(-1, 1), ntp)
        x1f = x1.reshape(n, d)
        h2 = h2.reshape(n, d)
        toff, dst, c8s, tstart, tlen, blk_e, nxt_e, n_used, blk_rows = _layout_tables(tab, nt, p_rows)
        xb = _dispatch(toff, dst, c8s, tstart, tlen, n_used, h2, pos, p_rows)
        yb = _experts(blk_e, nxt_e, n_used, blk_rows, xb, w_gate_up[l], b_gate_up[l][:, None, :],
                      w_down[l], b_down[l][:, None, :])
        x = _combine(toff, dst, c8s, yb, pos, gt, x1f, mod3, row(g_post_ffn[l]), s // TS).reshape(b, s, d)
    return x
```

```python
import jax
import jax.numpy as jnp
from jax import lax
from jax.experimental import pallas as pl
from jax.experimental.pallas import tpu as pltpu

bf16 = jnp.bfloat16
f32 = jnp.float32
i32 = jnp.int32
u32 = jnp.uint32

D_MODEL = 1024
CHUNK = 64
N_LEFT = 8
ATT_HEADS = 8
HEAD_DIM = 64
ATT_WIDTH = 512
MAX_REL = 128
SGU_BLOCK = 128
SGU_WIDTH = 512
N_EXPERTS = 32
TOP_K = 4
SWIGLU_LIMIT = 7.0
SWIGLU_ALPHA = 1.702
EPS = 1e-6
NEG = -1e30
LOG2E = 1.4426950408889634

LANES = 128
ROW_ALIGN = 8
TM = 512
TMP = 1024
TMX = 1024
QCH = 4
QG = QCH * CHUNK
KBAND = (N_LEFT + QCH) * CHUNK
BIAS_PERIOD = 1024
TS = 256
TG = 4
RS = TS * TOP_K + N_EXPERTS * ROW_ALIGN
BM = 1024
BM_STEPS = tuple(range(LANES, BM + 1, LANES))
CAST_ROWS = 64
VMEM_LIMIT = 56 * 2**20


def _params(sem):
    return pltpu.CompilerParams(dimension_semantics=sem, vmem_limit_bytes=VMEM_LIMIT)


def _adaln(x, g, sc, sh):
    ms = jnp.mean(x * x, axis=-1, keepdims=True)
    return (x * lax.rsqrt(ms + EPS) * g) * (1.0 + sc) + sh


def _rms(x, g):
    ms = jnp.mean(x * x, axis=-1, keepdims=True)
    return x * lax.rsqrt(ms + EPS) * g


def _sigmoid(x):
    return 0.5 * jnp.tanh(0.5 * x) + 0.5


def _ada_body(c_ref, w_ref, b_ref, o_ref):
    c = c_ref[...]
    ca = c * _sigmoid(c)
    o_ref[...] = jnp.dot(ca.astype(bf16), w_ref[...].astype(bf16),
                         preferred_element_type=f32) + b_ref[...]


def _ada(c8, w, b):
    d = w.shape[0]
    n = w.shape[1] // d
    return pl.pallas_call(
        _ada_body,
        out_shape=jax.ShapeDtypeStruct((8, n * d), f32),
        grid=(n,),
        in_specs=[pl.BlockSpec((8, d), lambda j: (0, 0)),
                  pl.BlockSpec((d, d), lambda j: (0, j)),
                  pl.BlockSpec((1, d), lambda j: (0, j))],
        out_specs=pl.BlockSpec((8, d), lambda j: (0, j)),
        compiler_params=_params(("arbitrary",)),
        name="ada",
    )(c8, w, b)


def _proj_body(x_ref, mod_ref, g_ref, w_ref, lng_ref, lnb_ref,
               q_ref, k_ref, v_ref, u_ref, vv_ref):
    h = _adaln(x_ref[0], g_ref[...], mod_ref[0, 1:2, :], mod_ref[0, 0:1, :]).astype(bf16)
    aw = ATT_WIDTH
    z = jnp.dot(h, w_ref[:, 3 * aw:], preferred_element_type=f32)
    p = jnp.dot(h, w_ref[:, :3 * aw], preferred_element_type=f32)
    q_ref[0] = (p[:, 0:aw] * (HEAD_DIM ** -0.5 * LOG2E)).astype(bf16)
    k_ref[0] = p[:, aw:2 * aw].astype(bf16)
    v_ref[0] = p[:, 2 * aw:3 * aw].astype(bf16)
    zg = 0.5 * z * (1.0 + lax.erf(z * (2.0 ** -0.5)))
    u_ref[0] = zg[:, :SGU_WIDTH].astype(bf16)
    vv = zg[:, SGU_WIDTH:]
    mu = jnp.mean(vv, axis=-1, keepdims=True)
    var = jnp.mean(jnp.square(vv - mu), axis=-1, keepdims=True)
    vn = (vv - mu) * lax.rsqrt(var + EPS) * lng_ref[...] + lnb_ref[...]
    vv_ref[0] = vn.astype(bf16)


def _proj(x, mod3, g, w_in, lng, lnb):
    b, s, d = x.shape
    tok = lambda w: pl.BlockSpec((1, TMP, w), lambda bi, i: (bi, i, 0))
    full = lambda a: pl.BlockSpec(a.shape, lambda bi, i: (0,) * a.ndim)
    o512 = jax.ShapeDtypeStruct((b, s, ATT_WIDTH), bf16)
    return pl.pallas_call(
        _proj_body,
        out_shape=(o512,) * 5,
        grid=(b, s // TMP),
        in_specs=[tok(d), pl.BlockSpec((1, 6, d), lambda bi, i: (bi, 0, 0)),
                  full(g), full(w_in), full(lng), full(lnb)],
        out_specs=(tok(ATT_WIDTH),) * 5,
        compiler_params=_params(("parallel", "arbitrary")),
        name="proj",
    )(x, mod3, g, w_in, lng, lnb)


def _attn_body(q_ref, kp_ref, kc_ref, vp_ref, vc_ref, bvec_ref, o_ref, bias_ref):
    first = pl.program_id(1) == 0
    nhp = ATT_HEADS // 2

    @pl.when(first)
    def _():
        i = lax.broadcasted_iota(i32, (QG, KBAND), 0)
        j = lax.broadcasted_iota(i32, (QG, KBAND), 1)
        jb = j - (i // CHUNK) * CHUNK
        in_band = jnp.logical_and(jb >= 0, jb < CHUNK * (N_LEFT + 1))
        for h in range(ATT_HEADS):
            rows = jnp.broadcast_to(bvec_ref[h:h + 1, :], (QG, BIAS_PERIOD))
            toep = pltpu.roll(rows, 0, 1, stride=1, stride_axis=0)[:, :KBAND]
            bias_ref[h // 2, (h % 2) * QG:(h % 2 + 1) * QG, :] = jnp.where(in_band, toep, NEG)

    lo = lax.broadcasted_iota(i32, (QG, LANES), 1) < HEAD_DIM
    has_prev = jnp.logical_not(first)
    for p in range(TM // QG):
        r0 = p * QG
        n_cur = KBAND - (TM - r0)
        ind = jnp.concatenate([jnp.where(has_prev, 1.0, 0.0) * jnp.ones((TM - r0, LANES), f32),
                               jnp.ones((n_cur, LANES), f32)], axis=0).astype(bf16)
        for hp in range(nhp):
            c0 = hp * LANES
            qp = q_ref[0, r0:r0 + QG, c0:c0 + LANES]
            zero = jnp.zeros_like(qp)
            q2 = jnp.concatenate([jnp.where(lo, qp, zero), jnp.where(lo, zero, qp)], axis=0)
            kprev = kp_ref[0, r0:TM, c0:c0 + LANES]
            vprev = vp_ref[0, r0:TM, c0:c0 + LANES]
            kprev = jnp.where(has_prev, kprev, jnp.zeros_like(kprev))
            vprev = jnp.where(has_prev, vprev, jnp.zeros_like(vprev))
            kb = jnp.concatenate([kprev, kc_ref[0, 0:n_cur, c0:c0 + LANES]], axis=0)
            vb = jnp.concatenate([vprev, vc_ref[0, 0:n_cur, c0:c0 + LANES]], axis=0)
            s = lax.dot_general(q2, kb, (((1,), (1,)), ((), ())), preferred_element_type=f32)
            sb = (s + bias_ref[hp]).astype(bf16)
            m = jnp.max(sb, axis=-1, keepdims=True)
            e = jnp.exp2(sb - m)
            o2 = jnp.dot(e, jnp.concatenate([vb, ind], axis=1),
                         preferred_element_type=f32)
            on = o2[:, :LANES] / o2[:, LANES:]
            o = jnp.where(lo, on[:QG], on[QG:])
            o_ref[0, r0:r0 + QG, c0:c0 + LANES] = o.astype(bf16)


def _attn(q, k, v, bvec):
    b, s, w = q.shape
    cur = pl.BlockSpec((1, TM, w), lambda bi, i: (bi, i, 0))
    prev = pl.BlockSpec((1, TM, w), lambda bi, i: (bi, jnp.maximum(i - 1, 0), 0))
    return pl.pallas_call(
        _attn_body,
        out_shape=jax.ShapeDtypeStruct((b, s, w), bf16),
        grid=(b, s // TM),
        in_specs=[cur, prev, cur, prev, cur,
                  pl.BlockSpec(bvec.shape, lambda bi, i: (0, 0))],
        out_specs=cur,
        scratch_shapes=[pltpu.VMEM((ATT_HEADS // 2, 2 * QG, KBAND), f32)],
        compiler_params=_params(("parallel", "arbitrary")),
        name="attn",
    )(q, k, k, v, v, bvec)


def _attn_bias_vec(rel_bias):
    h = rel_bias.shape[0]
    n_far = N_LEFT * CHUNK - MAX_REL
    assert BIAS_PERIOD >= QG + KBAND - 1 and n_far >= 0
    far = jnp.broadcast_to(rel_bias[:, 2 * MAX_REL:], (h, BIAS_PERIOD))
    near = rel_bias[:, :0:-1]
    v = jnp.concatenate([far[:, :n_far], near, far[:, n_far + 2 * MAX_REL:]], axis=1)
    return v.astype(f32) * LOG2E


def _route(hb, wr_ref, br_ref):
    lg = lax.dot_general(wr_ref[...], hb, (((1,), (1,)), ((), ())),
                         preferred_element_type=f32) + br_ref[...]
    e_iota = lax.broadcasted_iota(i32, lg.shape, 0)
    vals, idxs = [], []
    hits = jnp.zeros(lg.shape, f32)
    for _ in range(TOP_K):
        m = jnp.max(lg, axis=0, keepdims=True)
        idx = jnp.min(jnp.where(lg == m, e_iota, N_EXPERTS), axis=0, keepdims=True)
        hit = e_iota == idx
        hits = hits + jnp.where(hit, 1.0, 0.0)
        lg = jnp.where(hit, -jnp.inf, lg)
        vals.append(m)
        idxs.append(idx)
    ex = [jnp.exp(v - vals[0]) for v in vals]
    den = ex[0] + ex[1] + ex[2] + ex[3]
    return (jnp.concatenate(idxs, axis=0), jnp.concatenate([e / den for e in ex], axis=0), hits)


def _mix_body(x_ref, mod_ref, gpre_ref, gpost_ref, u_ref, vv_ref, ya_ref, ws_ref, bs_ref,
              wa_ref, wb_ref, wg_ref, bg_ref, wo_ref, gffn_ref, wr_ref, br_ref,
              o_ref, h2_ref, pos_ref, gt_ref, tab_ref, zb_ref, ybuf, zsrc, zsem):
    d = D_MODEL
    step = pl.program_id(0) * pl.num_programs(1) + pl.program_id(1)
    nsteps = pl.num_programs(0) * pl.num_programs(1)

    @pl.when(step == 0)
    def _():
        zsrc[...] = jnp.zeros_like(zsrc)

    def zero_blocks(action):
        def one(k, carry):
            do = pl.multiple_of((step + k * nsteps) * BM, BM)
            action(pltpu.make_async_copy(zsrc, zb_ref.at[pl.ds(do, BM)], zsem))
            return carry
        lax.fori_loop(0, (zb_ref.shape[0] // BM - step + nsteps - 1) // nsteps, one, 0)
    zero_blocks(lambda c: c.start())
    x = x_ref[0]
    h = _adaln(x, gpre_ref[...], mod_ref[0, 1:2, :], mod_ref[0, 0:1, :]).astype(bf16)
    gs = _sigmoid(jnp.dot(h, wg_ref[...], preferred_element_type=f32) + bg_ref[...])

    blk = SGU_BLOCK
    row = lax.broadcasted_iota(i32, (2 * blk, blk), 0)
    colv = lax.broadcasted_iota(i32, (2 * blk, blk), 1)
    causal = colv <= jnp.bitwise_and(row, blk - 1)
    lo = lax.broadcasted_iota(i32, (blk, LANES), 1) < (LANES // 2)
    for gp in range(SGU_WIDTH // LANES):
        c0 = gp * LANES
        w2 = ws_ref[gp]
        w2 = jnp.where(causal, w2, jnp.zeros_like(w2))
        for bi in range(TMX // blk):
            r0 = bi * blk
            s2 = jnp.dot(w2, vv_ref[0, r0:r0 + blk, c0:c0 + LANES], preferred_element_type=f32)
            s = jnp.where(lo, s2[:blk], s2[blk:]) + bs_ref[:, c0:c0 + LANES]
            yb = u_ref[0, r0:r0 + blk, c0:c0 + LANES].astype(f32) * s
            ybuf[r0:r0 + blk, c0:c0 + LANES] = yb.astype(bf16)

    a = jnp.dot(ya_ref[0], wa_ref[...], preferred_element_type=f32)
    bb = jnp.dot(ybuf[...], wb_ref[...], preferred_element_type=f32)
    merged = gs[:, :d] * a + gs[:, d:] * bb
    y = jnp.dot(merged.astype(bf16), wo_ref[...], preferred_element_type=f32)
    x1 = x + mod_ref[0, 2:3, :] * _rms(y, gpost_ref[...])
    o_ref[0] = x1

    hb = _adaln(x1, gffn_ref[...], mod_ref[0, 4:5, :], mod_ref[0, 3:4, :]).astype(bf16)
    h2_ref[0] = hb
    ids, gates, hits = _route(hb, wr_ref, br_ref)
    pos_ref[...] = jnp.concatenate(
        [jnp.concatenate(_sorted_positions(ids[:, j * TS:(j + 1) * TS]), axis=0)
         for j in range(TMX // TS)], axis=1)
    gt_ref[...] = gates

    @pl.when(step == 0)
    def _():
        tab_ref[...] = jnp.zeros_like(tab_ref)

    lane = lax.broadcasted_iota(i32, tab_ref.shape, 1)
    acc = tab_ref[...]
    for j in range(TMX // TS):
        cnt = jnp.sum(hits[:, j * TS:(j + 1) * TS], axis=1, keepdims=True)
        c8 = jnp.floor((cnt + (ROW_ALIGN - 1.0)) * (1.0 / ROW_ALIGN)) * ROW_ALIGN
        acc = acc + jnp.where(lane == step * (TMX // TS) + j, c8, 0.0)
    tab_ref[...] = acc
    zero_blocks(lambda c: c.wait())


def _mix(x, mod3, gpre, gpost, u, vv, ya, ws2, bsf, wa, wb, wg, bg, wo, gffn, wrt, br, ntp, p_rows):
    b, s, d = x.shape
    n = b * s
    tok = lambda w: pl.BlockSpec((1, TMX, w), lambda bi, i: (bi, i, 0))
    full = lambda a: pl.BlockSpec(a.shape, lambda bi, i: (0,) * a.ndim)
    lane = pl.BlockSpec((TOP_K, TMX), lambda bi, i: (0, bi * (s // TMX) + i))
    return pl.pallas_call(
        _mix_body,
        out_shape=(jax.ShapeDtypeStruct((b, s, d), f32),
                   jax.ShapeDtypeStruct((b, s, d), bf16),
                   jax.ShapeDtypeStruct((TOP_K, n), i32),
                   jax.ShapeDtypeStruct((TOP_K, n), f32),
                   jax.ShapeDtypeStruct((N_EXPERTS, ntp), f32),
                   jax.ShapeDtypeStruct((p_rows, d // 2), u32)),
        grid=(b, s // TMX),
        in_specs=[tok(d), pl.BlockSpec((1, 6, d), lambda bi, i: (bi, 0, 0)),
                  full(gpre), full(gpost), tok(SGU_WIDTH), tok(SGU_WIDTH), tok(ATT_WIDTH),
                  full(ws2), full(bsf), full(wa), full(wb), full(wg), full(bg), full(wo),
                  full(gffn), full(wrt), full(br)],
        out_specs=(tok(d), tok(d), lane, lane,
                   pl.BlockSpec((N_EXPERTS, ntp), lambda bi, i: (0, 0)),
                   pl.BlockSpec(memory_space=pl.ANY)),
        scratch_shapes=[pltpu.VMEM((TMX, SGU_WIDTH), bf16), pltpu.VMEM((BM, d // 2), u32),
                        pltpu.SemaphoreType.DMA(())],
        compiler_params=_params(("arbitrary", "arbitrary")),
        name="mix",
    )(x, mod3, gpre, gpost, u, vv, ya, ws2, bsf, wa, wb, wg, bg, wo, gffn, wrt, br)


def _sorted_positions(ti):
    ts = ti.shape[1]
    e_iota = lax.broadcasted_iota(i32, (N_EXPERTS, ts), 0)
    upper = (lax.broadcasted_iota(i32, (ts, ts), 0) < lax.broadcasted_iota(i32, (ts, ts), 1))
    upper = jnp.where(upper, 1.0, 0.0).astype(bf16)
    hits, prefs, cnts = [], [], []
    for k in range(TOP_K):
        hit = ti[k:k + 1, :] == e_iota
        hf = jnp.where(hit, 1.0, 0.0)
        prefs.append(jnp.dot(hf.astype(bf16), upper, preferred_element_type=f32))
        cnts.append(jnp.sum(hf, axis=1, keepdims=True))
        hits.append(hit)
    total = cnts[0] + cnts[1] + cnts[2] + cnts[3]
    c8 = jnp.floor((total + (ROW_ALIGN - 1.0)) * (1.0 / ROW_ALIGN)) * ROW_ALIGN
    lower = (lax.broadcasted_iota(i32, (N_EXPERTS, N_EXPERTS), 1)
             < lax.broadcasted_iota(i32, (N_EXPERTS, N_EXPERTS), 0))
    lower = jnp.where(lower, 1.0, 0.0).astype(bf16)
    c8b = jnp.broadcast_to(c8, (N_EXPERTS, LANES)).astype(bf16)
    start = jnp.dot(lower, c8b, preferred_element_type=f32)[:, 0:1]
    pos = []
    for k in range(TOP_K):
        pe = start + prefs[k]
        pos.append(jnp.sum(jnp.where(hits[k], pe, 0.0), axis=0, keepdims=True).astype(i32))
        start = start + cnts[k]
    return pos


def _pack_halves(x, exact):
    half = x.shape[1] // 2
    lo = lax.bitcast_convert_type(x[:, :half], u32)
    hi = lax.bitcast_convert_type(x[:, half:], u32)
    if not exact:
        hi = jnp.bitwise_and(hi, jnp.uint32(0xFFFF0000))
    return jnp.bitwise_or(lax.shift_right_logical(lo, jnp.uint32(16)), hi)


def _unpack_halves(w):
    lo = lax.bitcast_convert_type(lax.shift_left(w, jnp.uint32(16)), f32)
    hi = lax.bitcast_convert_type(jnp.bitwise_and(w, jnp.uint32(0xFFFF0000)), f32)
    return lo, hi


def _start_runs(i, toff_ref, dst_ref, c8_ref, make):
    unroll = 4

    def some(q, carry):
        for k in range(unroll):
            j = i * N_EXPERTS + q * unroll + k
            n = pl.multiple_of(c8_ref[j], ROW_ALIGN)
            so = pl.multiple_of(toff_ref[j], ROW_ALIGN)
            do = pl.multiple_of(dst_ref[j], ROW_ALIGN)

            @pl.when(n > 0)
            def _():
                make(so, do, n).start(priority=k % 2)
        return carry
    lax.fori_loop(0, N_EXPERTS // unroll, some, 0)


def _tile_rows(i, toff_ref, c8_ref):
    j = i * N_EXPERTS + (N_EXPERTS - 1)
    return pl.multiple_of(toff_ref[j] + c8_ref[j], ROW_ALIGN)


def _dispatch_body(toff_ref, dst_ref, c8_ref, h2_ref, pos_ref, zb_ref, xb_ref, sbuf, sems):
    del zb_ref
    i = pl.program_id(0)
    base = lax.rem(i, 2) * TG
    r = lax.broadcasted_iota(i32, (RS, TS), 0)
    for j in range(TG):
        p = [pos_ref[k:k + 1, j * TS:(j + 1) * TS] for k in range(TOP_K)]
        pm = jnp.where(r == p[0], 1.0, jnp.where(r == p[1], 1.0,
             jnp.where(r == p[2], 1.0, jnp.where(r == p[3], 1.0, 0.0))))
        srt = jnp.dot(pm.astype(bf16), h2_ref[j * TS:(j + 1) * TS, :], preferred_element_type=f32)
        sbuf[base + j] = _pack_halves(srt, exact=True)

    for j in range(TG):
        def make(so, do, n, sl=base + j):
            return pltpu.make_async_copy(sbuf.at[sl, pl.ds(so, n)], xb_ref.at[pl.ds(do, n)],
                                         sems.at[sl])
        _start_runs(i * TG + j, toff_ref, dst_ref, c8_ref, make)

    def wait_tile(t, sl):
        n = _tile_rows(t, toff_ref, c8_ref)
        pltpu.make_async_copy(sbuf.at[sl, pl.ds(0, n)], xb_ref.at[pl.ds(0, n)], sems.at[sl]).wait()

    @pl.when(i > 0)
    def _():
        for j in range(TG):
            wait_tile((i - 1) * TG + j, TG - base + j)

    @pl.when(i == pl.num_programs(0) - 1)
    def _():
        for j in range(TG):
            wait_tile(i * TG + j, base + j)


def _dispatch(toff, dst, c8s, h2, pos, zb):
    n, d = h2.shape
    return pl.pallas_call(
        _dispatch_body,
        out_shape=jax.ShapeDtypeStruct(zb.shape, zb.dtype),
        grid_spec=pltpu.PrefetchScalarGridSpec(
            num_scalar_prefetch=3,
            grid=(n // (TG * TS),),
            in_specs=[pl.BlockSpec((TG * TS, d), lambda i, *_: (i, 0)),
                      pl.BlockSpec((TOP_K, TG * TS), lambda i, *_: (0, i)),
                      pl.BlockSpec(memory_space=pl.ANY)],
            out_specs=pl.BlockSpec(memory_space=pl.ANY),
            scratch_shapes=[pltpu.VMEM((2 * TG, RS, d // 2), u32),
                            pltpu.SemaphoreType.DMA((2 * TG,))]),
        input_output_aliases={5: 0},
        compiler_params=_params(("arbitrary",)),
        name="dispatch",
    )(toff, dst, c8s, h2, pos, zb)


def _expert_body(be_ref, nxt_ref, nu_ref, rows_ref, x_ref, wgu_hbm, bgu_ref, wd_hbm, bd_ref, o_ref,
                 wgu_st, wd_st, wgu_bf, wd_bf, sems):
    del nu_ref
    b = pl.program_id(0)
    d = D_MODEL
    e = be_ref[b]

    def weight_copies(ex):
        return (pltpu.make_async_copy(wgu_hbm.at[ex], wgu_st, sems.at[0]),
                pltpu.make_async_copy(wd_hbm.at[ex], wd_st, sems.at[1]))

    @pl.when(b == 0)
    def _():
        for c in weight_copies(e):
            c.start()

    @pl.when(jnp.logical_or(b == 0, e != be_ref[jnp.maximum(b - 1, 0)]))
    def _():
        for c in weight_copies(e):
            c.wait()
        def cast_rows(c, carry):
            rows = pl.ds(pl.multiple_of(c * CAST_ROWS, CAST_ROWS), CAST_ROWS)
            wgu_bf[rows, :] = wgu_st[rows, :].astype(bf16)
            wd_bf[rows, :] = wd_st[rows, :].astype(bf16)
            return carry
        lax.fori_loop(0, d // CAST_ROWS, cast_rows, 0)
        nxt = nxt_ref[e]

        @pl.when(nxt >= 0)
        def _():
            for c in weight_copies(nxt):
                c.start()

    def mlp_rows(m):
        xb = jnp.concatenate(_unpack_halves(x_ref[:m, :]), axis=1).astype(bf16)
        gu = jnp.dot(xb, wgu_bf[...], preferred_element_type=f32) + bgu_ref[0]
        glu = jnp.minimum(gu[:, :d], SWIGLU_LIMIT)
        lin = jnp.clip(gu[:, d:], -SWIGLU_LIMIT, SWIGLU_LIMIT)
        act = glu * _sigmoid(SWIGLU_ALPHA * glu) * (lin + 1.0)
        y = jnp.dot(act.astype(bf16), wd_bf[...], preferred_element_type=f32) + bd_ref[0]
        o_ref[:m, :] = _pack_halves(y, exact=False)
        if m < BM:
            o_ref[m:, :] = jnp.zeros((BM - m, o_ref.shape[1]), o_ref.dtype)

    valid = rows_ref[b]
    for lo, hi in zip((0,) + BM_STEPS[:-1], BM_STEPS):
        @pl.when(jnp.logical_and(valid > lo, valid <= hi))
        def _(hi=hi):
            mlp_rows(hi)


def _experts(blk_e, nxt_e, n_used, blk_rows, xb, wgu, bgu, wd, bd):
    rows = pl.BlockSpec((BM, xb.shape[1]), lambda b, be, nx, nu, br: (b, 0))
    per_e = lambda a: pl.BlockSpec((1,) + a.shape[1:], lambda b, be, nx, nu, br: (be[b], 0, 0))
    hbm = pl.BlockSpec(memory_space=pl.ANY)
    return pl.pallas_call(
        _expert_body,
        out_shape=jax.ShapeDtypeStruct(xb.shape, xb.dtype),
        grid_spec=pltpu.PrefetchScalarGridSpec(
            num_scalar_prefetch=4,
            grid=(n_used[0],),
            in_specs=[rows, hbm, per_e(bgu), hbm, per_e(bd)],
            out_specs=rows,
            scratch_shapes=[pltpu.VMEM(wgu.shape[1:], f32), pltpu.VMEM(wd.shape[1:], f32),
                            pltpu.VMEM(wgu.shape[1:], bf16), pltpu.VMEM(wd.shape[1:], bf16),
                            pltpu.SemaphoreType.DMA((2,))]),
        input_output_aliases={4: 0},
        compiler_params=_params(("arbitrary",)),
        name="experts",
    )(blk_e, nxt_e, n_used, blk_rows, xb, wgu, bgu, wd, bd)


def _combine_body(toff_ref, dst_ref, c8_ref, yb_ref, pos_ref, gt_ref, x1_ref, mod_ref, g_ref,
                  o_ref, sbuf, sems):
    i = pl.program_id(0)
    base = lax.rem(i, 2) * TG

    def fetch(step, first_slot):
        for j in range(TG):
            def make(so, do, n, sl=first_slot + j):
                return pltpu.make_async_copy(yb_ref.at[pl.ds(do, n)], sbuf.at[sl, pl.ds(so, n)],
                                             sems.at[sl])
            _start_runs(step * TG + j, toff_ref, dst_ref, c8_ref, make)

    @pl.when(i == 0)
    def _():
        sbuf[...] = jnp.zeros_like(sbuf)
        fetch(i, base)

    @pl.when(i + 1 < pl.num_programs(0))
    def _():
        fetch(i + 1, TG - base)

    r = lax.broadcasted_iota(i32, (RS, TS), 0)
    pms, grows = [], []
    for j in range(TG):
        p = [pos_ref[k:k + 1, j * TS:(j + 1) * TS] for k in range(TOP_K)]
        gt = gt_ref[:, j * TS:(j + 1) * TS]
        gm = jnp.where(r == p[0], gt[0:1, :], jnp.where(r == p[1], gt[1:2, :],
             jnp.where(r == p[2], gt[2:3, :], jnp.where(r == p[3], gt[3:4, :], 0.0))))
        pms.append(jnp.where(gm != 0.0, 1.0, 0.0).astype(bf16))
        grows.append(jnp.sum(gm, axis=1, keepdims=True))

    for j in range(TG):
        n = _tile_rows(i * TG + j, toff_ref, c8_ref)
        pltpu.make_async_copy(yb_ref.at[pl.ds(0, n)], sbuf.at[base + j, pl.ds(0, n)],
                              sems.at[base + j]).wait()
    for j in range(TG):
        halves = [lax.dot_general(pms[j], (h * grows[j]).astype(bf16), (((0,), (0,)), ((), ())),
                                  preferred_element_type=f32)
                  for h in _unpack_halves(sbuf[base + j])]
        y = jnp.concatenate(halves, axis=1)
        rows = slice(j * TS, (j + 1) * TS)
        o_ref[rows, :] = x1_ref[rows, :] + mod_ref[0, 5:6, :] * _rms(y, g_ref[...])


def _combine(toff, dst, c8s, yb, pos, gt, x1, mod3, g, tiles_per_batch):
    n, d = x1.shape
    tok = pl.BlockSpec((TG * TS, d), lambda i, *_: (i, 0))
    lane = pl.BlockSpec((TOP_K, TG * TS), lambda i, *_: (0, i))
    steps_per_batch = tiles_per_batch // TG
    return pl.pallas_call(
        _combine_body,
        out_shape=jax.ShapeDtypeStruct((n, d), f32),
        grid_spec=pltpu.PrefetchScalarGridSpec(
            num_scalar_prefetch=3,
            grid=(n // (TG * TS),),
            in_specs=[pl.BlockSpec(memory_space=pl.ANY), lane, lane, tok,
                      pl.BlockSpec((1, 6, d), lambda i, *_: (i // steps_per_batch, 0, 0)),
                      pl.BlockSpec(g.shape, lambda i, *_: (0, 0))],
            out_specs=tok,
            scratch_shapes=[pltpu.VMEM((2 * TG, RS, d // 2), u32), pltpu.SemaphoreType.DMA((2 * TG,))]),
        compiler_params=_params(("arbitrary",)),
        name="combine",
    )(toff, dst, c8s, yb, pos, gt, x1, mod3, g)


def _layout_tables(tab, nt, p_rows):
    c8 = tab[:, :nt].T.astype(i32)
    toff = jnp.cumsum(c8, axis=1) - c8
    len8 = jnp.sum(c8, axis=0)
    seg = (len8 + BM - 1) // BM * BM
    gend = jnp.cumsum(seg)
    gstart = gend - seg
    dst = jnp.cumsum(c8, axis=0) - c8 + gstart[None, :]
    n_used = gend[-1] // BM
    blk = jnp.arange(p_rows // BM, dtype=i32)
    last = jnp.minimum(blk, n_used - 1)
    blk_e = jnp.sum((gend[None, :] <= (last * BM)[:, None]).astype(i32), axis=1)
    blk_e = jnp.minimum(blk_e, N_EXPERTS - 1)
    lo = jnp.maximum(gstart[None, :], (blk * BM)[:, None])
    hi = jnp.minimum((gstart + len8)[None, :], ((blk + 1) * BM)[:, None])
    blk_rows = jnp.sum(jnp.maximum(hi - lo, 0), axis=1).astype(i32)
    ids = jnp.arange(N_EXPERTS, dtype=i32)
    later = jnp.logical_and(ids[None, :] > ids[:, None], (seg > 0)[None, :])
    nxt_e = jnp.min(jnp.where(later, ids[None, :], N_EXPERTS), axis=1)
    nxt_e = jnp.where(nxt_e == N_EXPERTS, -1, nxt_e).astype(i32)
    return (toff.reshape(-1), dst.reshape(-1), c8.reshape(-1),
            blk_e, nxt_e, n_used.reshape(1).astype(i32), blk_rows)


def kernel(x, c, w_ada, b_ada, g_pre_mix, g_post_mix, w_in, rel_bias, sgu_ln_g, sgu_ln_b,
           w_spatial, b_spatial, w_branch_a, w_branch_b, w_gate, b_gate, w_out,
           g_pre_ffn, g_post_ffn, w_router, b_router, w_gate_up, b_gate_up, w_down, b_down):
    b, s, d = x.shape
    assert d == D_MODEL and s % max(TM, TMP, TMX) == 0 and s % (TG * TS) == 0
    n = b * s
    nt = n // TS
    ntp = -(-nt // LANES) * LANES
    p_rows = -(-(n * TOP_K + nt * N_EXPERTS * (ROW_ALIGN - 1) + N_EXPERTS * (BM - 1)) // BM) * BM
    depth = w_ada.shape[0]
    c8 = jnp.pad(c, ((0, 8 - b), (0, 0)))
    row = lambda a: a.reshape(1, -1)

    for l in range(depth):
        mod = _ada(c8, w_ada[l], row(b_ada[l]))[:b]
        mod3 = mod.reshape(b, 6, d)

        q, k, v, u, vv = _proj(x, mod3, row(g_pre_mix[l]), w_in[l].astype(bf16),
                               row(sgu_ln_g[l]), row(sgu_ln_b[l]))
        ya = _attn(q, k, v, _attn_bias_vec(rel_bias[l]))
        ws2 = w_spatial[l].astype(bf16).reshape(-1, 2 * SGU_BLOCK, SGU_BLOCK)
        bsf = jnp.repeat(b_spatial[l].T, SGU_WIDTH // b_spatial.shape[1], axis=1)
        x1, h2, pos, gt, tab, zb = _mix(x, mod3, row(g_pre_mix[l]), row(g_post_mix[l]), u, vv, ya, ws2, bsf,
                                   w_branch_a[l].astype(bf16), w_branch_b[l].astype(bf16),
                                   w_gate[l].astype(bf16), row(b_gate[l]), w_out[l].astype(bf16),
                                   row(g_pre_ffn[l]), w_router[l].T.astype(bf16),
                                   b_router[l].reshape(-1, 1), ntp, p_rows)
        x1f = x1.reshape(n, d)
        h2 = h2.reshape(n, d)
        toff, dst, c8s, blk_e, nxt_e, n_used, blk_rows = _layout_tables(tab, nt, p_rows)
        xb = _dispatch(toff, dst, c8s, h2, pos, zb)
        yb = _experts(blk_e, nxt_e, n_used, blk_rows, xb, w_gate_up[l], b_gate_up[l][:, None, :],
                      w_down[l], b_down[l][:, None, :])
        x = _combine(toff, dst, c8s, yb, pos, gt, x1f, mod3, row(g_post_ffn[l]), s // TS).reshape(b, s, d)
    return x
```
